```python
import math
import jax, jax.numpy as jnp
from jax import lax
import numpy as np

D_MODEL = 1024
BATCH = 4
SEQ = 4096
DEPTH = 1

MIX_WIDTH = D_MODEL
ATT_WIDTH = MIX_WIDTH // 2
RNN_WIDTH = MIX_WIDTH - ATT_WIDTH
N_ATT_HEADS = 4
ATT_HEAD_DIM = ATT_WIDTH // N_ATT_HEADS
QK_DIM = ATT_HEAD_DIM // 2
N_RNN_BLOCKS = 8
RNN_BLOCK = RNN_WIDTH // N_RNN_BLOCKS
CONV_WIDTH = 4
LRU_C = 8.0
N_GROUPS = 4
EXPERTS_PER_GROUP = 8
N_EXPERTS = N_GROUPS * EXPERTS_PER_GROUP
TOP_K = 2
EXPERT_FF = D_MODEL // 2
Q_BLOCK = 128
MOE_CHUNK = 128
NORM_EPS = 1e-6
HEAD_NORM_EPS = 1e-5
IN_COLS = 3 * ATT_WIDTH + 2 * RNN_WIDTH

kernel_name = 'hymba_diffattn_rglru_hiermoe'


def rms_norm(x, w, eps=NORM_EPS):
    xf = x.astype(jnp.float32)
    y = xf * lax.rsqrt(jnp.mean(xf * xf, axis=-1, keepdims=True) + eps)
    return (y * w.astype(jnp.float32)).astype(x.dtype)


def alibi_slopes(n_heads):
    return np.array([2.0 ** (-8.0 * (i + 1) / n_heads) for i in range(n_heads)], dtype=np.float32)


def diff_attention(q, k, v, lam):
    b, h, _, s, dk = q.shape
    n_blocks = s // Q_BLOCK
    slopes = jnp.asarray(alibi_slopes(h))
    scale = dk ** -0.5
    k_pos = jnp.arange(s)

    def block(qb):
        start = qb * Q_BLOCK
        q_blk = lax.dynamic_slice_in_dim(q, start, Q_BLOCK, axis=3)
        sc = jnp.einsum('bhcqd,bhckd->bhcqk', q_blk, k).astype(jnp.float32) * scale
        dist = (start + jnp.arange(Q_BLOCK))[:, None] - k_pos[None, :]
        bias = jnp.where(dist >= 0, -slopes[:, None, None] * dist.astype(jnp.float32), -jnp.inf)
        prob = jax.nn.softmax(sc + bias[None, :, None], axis=-1)
        att = prob[:, :, 0] - lam * prob[:, :, 1]
        return jnp.einsum('bhqk,bhkd->bhqd', att.astype(v.dtype), v)

    out = lax.map(block, jnp.arange(n_blocks))
    return jnp.moveaxis(out, 0, 2).reshape(b, h, s, -1)


def rglru_branch(xr, gate, conv_w, conv_b, w_r, b_r, w_i, b_i, lru_lambda):
    b, s, c = xr.shape
    xc = lax.conv_general_dilated(xr, conv_w[:, None, :], window_strides=(1,),
                                  padding=[(CONV_WIDTH - 1, 0)],
                                  dimension_numbers=('NWC', 'WIO', 'NWC'),
                                  feature_group_count=c) + conv_b
    xb = xc.reshape(b, s, N_RNN_BLOCKS, RNN_BLOCK)
    r = jax.nn.sigmoid((jnp.einsum('bsni,nij->bsnj', xb, w_r).reshape(b, s, c) + b_r).astype(jnp.float32))
    i = jax.nn.sigmoid((jnp.einsum('bsni,nij->bsnj', xb, w_i).reshape(b, s, c) + b_i).astype(jnp.float32))
    log_a = -LRU_C * r * jax.nn.softplus(-lru_lambda.astype(jnp.float32))
    a = jnp.exp(log_a)
    u = jnp.sqrt(-jnp.expm1(2.0 * log_a)) * i * xc.astype(jnp.float32)

    def combine(left, right):
        a1, b1 = left
        a2, b2 = right
        return a1 * a2, a2 * b1 + b2

    _, hs = lax.associative_scan(combine, (a, u), axis=1)
    return (hs * jax.nn.gelu(gate.astype(jnp.float32))).astype(xr.dtype)


def hier_moe(xn, w_rg, b_rg, w_re, b_re, w_g, w_u, w_d):
    b, s, d = xn.shape
    t = b * s
    xf = xn.reshape(t, d)
    lg = (xf @ w_rg).astype(jnp.float32) + b_rg.astype(jnp.float32)
    pg = jax.nn.softmax(lg, axis=-1)
    g_sel = jnp.argmax(lg, axis=-1).astype(jnp.int32)
    pg_sel = jnp.take_along_axis(pg, g_sel[:, None], axis=1)[:, 0]
    le = ((xf @ w_re).astype(jnp.float32) + b_re.astype(jnp.float32)).reshape(t, N_GROUPS, EXPERTS_PER_GROUP)
    le_sel = jnp.take_along_axis(le, g_sel[:, None, None], axis=1)[:, 0]
    top_v, top_i = lax.top_k(le_sel, TOP_K)
    gates = pg_sel[:, None] * jax.nn.softmax(top_v, axis=-1)
    eid = (g_sel[:, None] * EXPERTS_PER_GROUP + top_i.astype(jnp.int32)).reshape(-1)
    tok = jnp.repeat(jnp.arange(t, dtype=jnp.int32), TOP_K)
    gate_flat = gates.reshape(-1)
    m = t * TOP_K

    order = jnp.argsort(eid)
    s_eid = eid[order]
    counts = jnp.bincount(eid, length=N_EXPERTS).astype(jnp.int32)
    start = jnp.cumsum(counts) - counts
    padded = (counts + MOE_CHUNK - 1) // MOE_CHUNK * MOE_CHUNK
    pend = jnp.cumsum(padded)
    pstart = pend - padded
    dest = pstart[s_eid] + jnp.arange(m, dtype=jnp.int32) - start[s_eid]
    n_chunks = -(-(m + N_EXPERTS * (MOE_CHUNK - 1)) // MOE_CHUNK)
    p = n_chunks * MOE_CHUNK
    slot_tok = jnp.full((p,), t, dtype=jnp.int32).at[dest].set(tok[order])
    slot_gate = jnp.zeros((p,), jnp.float32).at[dest].set(gate_flat[order])
    chunk_e = jnp.minimum(jnp.searchsorted(pend, jnp.arange(n_chunks, dtype=jnp.int32) * MOE_CHUNK, side='right'),
                          N_EXPERTS - 1).astype(jnp.int32)
    xpad = jnp.concatenate([xf, jnp.zeros((1, d), xf.dtype)], axis=0)

    def run_chunk(args):
        ct, cg, e = args
        xc = xpad[ct]
        hdn = jax.nn.silu(xc @ w_g[e]) * (xc @ w_u[e])
        return (hdn @ w_d[e]) * cg[:, None].astype(xc.dtype)

    y = lax.map(run_chunk, (slot_tok.reshape(n_chunks, MOE_CHUNK),
                            slot_gate.reshape(n_chunks, MOE_CHUNK), chunk_e))
    out = jnp.zeros((t + 1, d), xf.dtype).at[slot_tok].add(y.reshape(p, d))[:t]
    return out.reshape(b, s, d)


def setup_inputs(seed: int = 0) -> dict:
    key = jax.random.key(seed)
    ks = jax.random.split(key, 24)
    f32 = jnp.float32
    nrm = lambda k, shape, scale: jax.random.normal(k, shape, f32) * scale
    u = jax.random.uniform(ks[12], (DEPTH, RNN_WIDTH), f32, 0.9, 0.999)
    sa = u ** (1.0 / LRU_C)
    return {
        'x': jax.random.normal(ks[0], (BATCH, SEQ, D_MODEL), f32),
        'mix_norm_w': 1.0 + nrm(ks[1], (DEPTH, D_MODEL), 0.02),
        'w_in': nrm(ks[2], (DEPTH, D_MODEL, IN_COLS), D_MODEL ** -0.5),
        'lambda_q1': nrm(ks[3], (DEPTH, QK_DIM), 0.1),
        'lambda_k1': nrm(ks[4], (DEPTH, QK_DIM), 0.1),
        'lambda_q2': nrm(ks[5], (DEPTH, QK_DIM), 0.1),
        'lambda_k2': nrm(ks[6], (DEPTH, QK_DIM), 0.1),
        'head_norm_w': 1.0 + nrm(ks[7], (DEPTH, ATT_HEAD_DIM), 0.02),
        'conv_w': nrm(ks[8], (DEPTH, CONV_WIDTH, RNN_WIDTH), CONV_WIDTH ** -0.5),
        'conv_b': nrm(ks[9], (DEPTH, RNN_WIDTH), 0.01),
        'w_rgate': nrm(ks[10], (DEPTH, N_RNN_BLOCKS, RNN_BLOCK, RNN_BLOCK), RNN_BLOCK ** -0.5),
        'b_rgate': nrm(ks[11], (DEPTH, RNN_WIDTH), 0.01),
        'w_igate': nrm(ks[13], (DEPTH, N_RNN_BLOCKS, RNN_BLOCK, RNN_BLOCK), RNN_BLOCK ** -0.5),
        'b_igate': nrm(ks[14], (DEPTH, RNN_WIDTH), 0.01),
        'lru_lambda': jnp.log(sa) - jnp.log1p(-sa),
        'w_out': nrm(ks[15], (DEPTH, MIX_WIDTH, D_MODEL), MIX_WIDTH ** -0.5),
        'ffn_norm_w': 1.0 + nrm(ks[16], (DEPTH, D_MODEL), 0.02),
        'w_router_group': nrm(ks[17], (DEPTH, D_MODEL, N_GROUPS), D_MODEL ** -0.5),
        'b_router_group': nrm(ks[18], (DEPTH, N_GROUPS), 0.01),
        'w_router_expert': nrm(ks[19], (DEPTH, D_MODEL, N_EXPERTS), D_MODEL ** -0.5),
        'b_router_expert': nrm(ks[20], (DEPTH, N_EXPERTS), 0.01),
        'w_exp_gate': nrm(ks[21], (DEPTH, N_EXPERTS, D_MODEL, EXPERT_FF), D_MODEL ** -0.5),
        'w_exp_up': nrm(ks[22], (DEPTH, N_EXPERTS, D_MODEL, EXPERT_FF), D_MODEL ** -0.5),
        'w_exp_down': nrm(ks[23], (DEPTH, N_EXPERTS, EXPERT_FF, D_MODEL), EXPERT_FF ** -0.5),
        'final_norm_w': 1.0 + nrm(jax.random.fold_in(key, 99), (D_MODEL,), 0.02),
    }


def reference(x, mix_norm_w, w_in, lambda_q1, lambda_k1, lambda_q2, lambda_k2, head_norm_w,
              conv_w, conv_b, w_rgate, b_rgate, w_igate, b_igate, lru_lambda, w_out,
              ffn_norm_w, w_router_group, b_router_group, w_router_expert, b_router_expert,
              w_exp_gate, w_exp_up, w_exp_down, final_norm_w):
    b, s, _ = x.shape
    splits = [ATT_WIDTH, 2 * ATT_WIDTH, 3 * ATT_WIDTH, 3 * ATT_WIDTH + RNN_WIDTH]
    for l in range(DEPTH):
        lambda_init = 0.8 - 0.6 * math.exp(-0.3 * l)
        hn = rms_norm(x, mix_norm_w[l])
        proj = hn @ w_in[l]
        q, k, v, xr, gr = jnp.split(proj, splits, axis=-1)
        q = q.reshape(b, s, N_ATT_HEADS, 2, QK_DIM).transpose(0, 2, 3, 1, 4)
        k = k.reshape(b, s, N_ATT_HEADS, 2, QK_DIM).transpose(0, 2, 3, 1, 4)
        v = v.reshape(b, s, N_ATT_HEADS, ATT_HEAD_DIM).transpose(0, 2, 1, 3)
        lam = (jnp.exp(jnp.sum(lambda_q1[l].astype(jnp.float32) * lambda_k1[l].astype(jnp.float32)))
               - jnp.exp(jnp.sum(lambda_q2[l].astype(jnp.float32) * lambda_k2[l].astype(jnp.float32)))
               + lambda_init)
        att = diff_attention(q, k, v, lam)
        att = rms_norm(att, head_norm_w[l], HEAD_NORM_EPS) * (1.0 - lambda_init)
        att = att.transpose(0, 2, 1, 3).reshape(b, s, ATT_WIDTH)
        rnn = rglru_branch(xr, gr, conv_w[l], conv_b[l], w_rgate[l], b_rgate[l],
                           w_igate[l], b_igate[l], lru_lambda[l])
        x = x + jnp.concatenate([att, rnn], axis=-1) @ w_out[l]
        hn = rms_norm(x, ffn_norm_w[l])
        x = x + hier_moe(hn, w_router_group[l], b_router_group[l], w_router_expert[l],
                         b_router_expert[l], w_exp_gate[l], w_exp_up[l], w_exp_down[l])
    return rms_norm(x, final_norm_w)
```

```python
import functools
import math

import numpy as np
import jax
import jax.numpy as jnp
from jax import lax
from jax.experimental import pallas as pl
from jax.experimental.pallas import tpu as pltpu

F32 = jnp.float32
BF16 = jnp.bfloat16

N_ATT_HEADS = 4
HEAD_DIM = 128
QK_DIM = 64
N_RNN_BLOCKS = 8
CONV_WIDTH = 4
LRU_C = 8.0
N_GROUPS = 4
EXPERTS_PER_GROUP = 8
N_EXPERTS = N_GROUPS * EXPERTS_PER_GROUP
TOP_K = 2
NORM_EPS = 1e-6
HEAD_NORM_EPS = 1e-5
LANES = 128
NEG_BIG = -1e30

ROW_TILE = 512
ATT_TILE = 256
LRU_TILE = 512
LRU_CHUNK = 128
MOE_TILE = 256
VMEM_LIMIT = 48 * 1024 * 1024


def _cparams(sem, vmem=VMEM_LIMIT, **kw):
    return pltpu.CompilerParams(dimension_semantics=sem, vmem_limit_bytes=vmem, **kw)


def _inproj_kernel(x_ref, nw_ref, w_ref, qkv_ref, xg_ref, *, att_w):
    x = x_ref[...]
    ms = jnp.mean(x * x, axis=-1, keepdims=True)
    hn = (x * lax.rsqrt(ms + NORM_EPS) * nw_ref[...]).astype(BF16)
    p = jnp.dot(hn, w_ref[...], preferred_element_type=F32)
    scale = QK_DIM ** -0.5
    qkv_ref[:, :att_w] = (p[:, :att_w] * scale).astype(BF16)
    qkv_ref[:, att_w:] = p[:, att_w:3 * att_w].astype(BF16)
    xg_ref[...] = p[:, 3 * att_w:]


def _in_proj(xf, norm_w, w_in_bf, att_w):
    t, d = xf.shape
    n = w_in_bf.shape[1]
    tm = min(ROW_TILE, t)
    return pl.pallas_call(
        functools.partial(_inproj_kernel, att_w=att_w),
        grid=(t // tm,),
        in_specs=[pl.BlockSpec((tm, d), lambda i: (i, 0)),
                  pl.BlockSpec((1, d), lambda i: (0, 0)),
                  pl.BlockSpec((d, n), lambda i: (0, 0))],
        out_specs=[pl.BlockSpec((tm, 3 * att_w), lambda i: (i, 0)),
                   pl.BlockSpec((tm, n - 3 * att_w), lambda i: (i, 0))],
        out_shape=[jax.ShapeDtypeStruct((t, 3 * att_w), BF16),
                   jax.ShapeDtypeStruct((t, n - 3 * att_w), F32)],
        compiler_params=_cparams(("parallel",)),
        name="in_proj",
    )(xf, norm_w.reshape(1, d), w_in_bf)


def _attn_kernel(slope_ref, lam_ref, hw_ref, q_ref, k_ref, v_ref, o_ref,
                 k1t, k2t, m1, l1, a1, m2, l2, a2, *, tq, s_len, lambda_init):
    h = pl.program_id(1)
    qi = pl.program_id(2)
    n_kv = s_len // tq
    slope = slope_ref[h]

    @pl.when(qi == 0)
    def _():
        for c in range(n_kv):
            kk = k_ref[0, c * tq:(c + 1) * tq, :].astype(F32)
            kkt = kk.T
            j = c * tq + lax.broadcasted_iota(jnp.int32, (QK_DIM, tq), 1)
            r = lax.broadcasted_iota(jnp.int32, (QK_DIM, tq), 0)
            j_lo = (j & 255).astype(F32)
            j_hi = (j - (j & 255)).astype(F32)
            aug = jnp.where(r == 0, slope * j_hi, jnp.where(r == 1, slope * j_lo, 0.0)).astype(BF16)
            k1t[c, :QK_DIM, :] = kkt[:QK_DIM].astype(BF16)
            k1t[c, QK_DIM:, :] = aug
            k2t[c, :QK_DIM, :] = kkt[QK_DIM:].astype(BF16)
            k2t[c, QK_DIM:, :] = aug

    lane = lax.broadcasted_iota(jnp.int32, (tq, HEAD_DIM), 1)
    ones_cols = jnp.where((lane == QK_DIM) | (lane == QK_DIM + 1), 1.0, 0.0)
    q = q_ref[0].astype(F32)
    q1a = jnp.where(lane < QK_DIM, q, ones_cols).astype(BF16)
    q2a = jnp.where(lane < QK_DIM, pltpu.roll(q, QK_DIM, axis=1), ones_cols).astype(BF16)

    for m, l, a in ((m1, l1, a1), (m2, l2, a2)):
        m[...] = jnp.full(m.shape, NEG_BIG, F32)
        l[...] = jnp.zeros(l.shape, F32)
        a[...] = jnp.zeros(a.shape, F32)

    def tile(j, masked):
        vv = v_ref[0, pl.ds(pl.multiple_of(j * tq, tq), tq), :]
        if masked:
            ri = lax.broadcasted_iota(jnp.int32, (tq, tq), 0)
            ci = lax.broadcasted_iota(jnp.int32, (tq, tq), 1)
            keep = ri >= ci
        for qa, kt, m, l, a in ((q1a, k1t, m1, l1, a1), (q2a, k2t, m2, l2, a2)):
            s = jnp.dot(qa, kt[j], preferred_element_type=F32)
            if masked:
                s = jnp.where(keep, s, NEG_BIG)
            m_prev = m[...]
            m_new = jnp.maximum(m_prev, jnp.max(s, axis=1, keepdims=True))
            alpha = jnp.exp(m_prev - m_new)
            p = jnp.exp(s - m_new)
            l[...] = alpha * l[...] + jnp.sum(p, axis=1, keepdims=True)
            a[...] = alpha * a[...] + jnp.dot(p.astype(BF16), vv, preferred_element_type=F32)
            m[...] = m_new

    def body(j, c):
        tile(j, False)
        return c

    lax.fori_loop(0, qi, body, 0)
    tile(qi, True)

    lam = (jnp.exp(jnp.sum(lam_ref[0:1, :] * lam_ref[1:2, :], axis=1, keepdims=True))
           - jnp.exp(jnp.sum(lam_ref[2:3, :] * lam_ref[3:4, :], axis=1, keepdims=True))
           + lambda_init)
    o = a1[...] / l1[...] - lam * (a2[...] / l2[...])
    o = o * lax.rsqrt(jnp.mean(o * o, axis=-1, keepdims=True) + HEAD_NORM_EPS)
    o_ref[0] = (o * hw_ref[...] * (1.0 - lambda_init)).astype(o_ref.dtype)


def _diff_attention(qkv, lam_params, head_norm_w, lambda_init):
    b, s, w3 = qkv.shape
    nh = N_ATT_HEADS
    tq = min(ATT_TILE, s)
    n_kv = s // tq
    slopes = jnp.asarray(np.array([2.0 ** (-8.0 * (i + 1) / nh) for i in range(nh)], dtype=np.float32))
    return pl.pallas_call(
        functools.partial(_attn_kernel, tq=tq, s_len=s, lambda_init=lambda_init),
        grid=(b, nh, s // tq),
        in_specs=[pl.BlockSpec(memory_space=pltpu.SMEM),
                  pl.BlockSpec((4, QK_DIM), lambda bi, hi, qi: (0, 0)),
                  pl.BlockSpec((1, HEAD_DIM), lambda bi, hi, qi: (0, 0)),
                  pl.BlockSpec((1, tq, HEAD_DIM), lambda bi, hi, qi: (bi, qi, hi)),
                  pl.BlockSpec((1, s, HEAD_DIM), lambda bi, hi, qi: (bi, 0, nh + hi)),
                  pl.BlockSpec((1, s, HEAD_DIM), lambda bi, hi, qi: (bi, 0, 2 * nh + hi))],
        out_specs=pl.BlockSpec((1, tq, HEAD_DIM), lambda bi, hi, qi: (bi, qi, hi)),
        out_shape=jax.ShapeDtypeStruct((b, s, nh * HEAD_DIM), BF16),
        scratch_shapes=[pltpu.VMEM((n_kv, HEAD_DIM, tq), BF16),
                        pltpu.VMEM((n_kv, HEAD_DIM, tq), BF16),
                        pltpu.VMEM((tq, 1), F32), pltpu.VMEM((tq, 1), F32), pltpu.VMEM((tq, HEAD_DIM), F32),
                        pltpu.VMEM((tq, 1), F32), pltpu.VMEM((tq, 1), F32), pltpu.VMEM((tq, HEAD_DIM), F32)],
        compiler_params=_cparams(("parallel", "parallel", "arbitrary")),
        name="diff_attn",
    )(slopes, lam_params, head_norm_w.reshape(1, HEAD_DIM), qkv, qkv, qkv)


def _gelu_tanh(x):
    return 0.5 * x * (1.0 + jnp.tanh(math.sqrt(2.0 / math.pi) * (x + 0.044715 * (x * x * x))))


def _rglru_kernel(xr_ref, gr_ref, cw_ref, cb_ref, w_ref, b_ref, lam_ref, o_ref,
                  xs, carry_h, a_s, u_s, *, ts, ch, c_w):
    si = pl.program_id(1)

    @pl.when(si == 0)
    def _():
        xs[0:8, :] = jnp.zeros((8, c_w), F32)
        carry_h[...] = jnp.zeros(carry_h.shape, F32)

    xs[8:, :] = xr_ref[0]
    neg_lam = -lam_ref[...]
    sp = jnp.maximum(neg_lam, 0.0) + jnp.log1p(jnp.exp(-jnp.abs(neg_lam)))
    cw = cw_ref[...]
    cb = cb_ref[...]
    bias = b_ref[...]
    r8 = lax.broadcasted_iota(jnp.int32, (ch, c_w), 0) & 7

    def chunk(c, carry):
        r0 = pl.multiple_of(c * ch, ch)
        win = xs[pl.ds(r0, ch + 8), :]
        xc = cw[3:4, :] * win[8:] + cb
        for k in (1, 2, 3):
            xc = xc + cw[3 - k:4 - k, :] * pltpu.roll(win, k, axis=0)[8:]
        z = jnp.dot(xc.astype(BF16), w_ref[...], preferred_element_type=F32) + bias
        r = jax.nn.sigmoid(z[:, :c_w])
        ig = jax.nn.sigmoid(z[:, c_w:])
        log_a = (-LRU_C) * r * sp
        a = jnp.exp(log_a)
        u = jnp.sqrt(jnp.tanh(-log_a) * (1.0 + a * a)) * ig * xc
        for k in (1, 2, 4):
            a_sh = pltpu.roll(a, k, axis=0)
            u_sh = pltpu.roll(u, k, axis=0)
            ok = r8 >= k
            u = jnp.where(ok, u + a * u_sh, u)
            a = jnp.where(ok, a * a_sh, a)
        a_s[pl.ds(r0, ch), :] = a
        u_s[pl.ds(r0, ch), :] = u
        return carry

    lax.fori_loop(0, ts // ch, chunk, 0)

    def grp(g, hprev):
        r0 = pl.multiple_of(g * 8, 8)
        hg = u_s[pl.ds(r0, 8), :] + a_s[pl.ds(r0, 8), :] * hprev
        u_s[pl.ds(r0, 8), :] = hg
        return hg[7:8, :]

    hlast = lax.fori_loop(0, ts // 8, grp, carry_h[0:1, :], unroll=8)
    carry_h[0:1, :] = hlast
    xs[0:8, :] = xs[ts:ts + 8, :]
    o_ref[0] = (u_s[...] * _gelu_tanh(gr_ref[0])).astype(o_ref.dtype)


def _rglru(xg, conv_w, conv_b, w_bd, b_cat, lru_lambda):
    b, s, w2 = xg.shape
    c_w = w2 // 2
    ts = min(LRU_TILE, s)
    ch = min(LRU_CHUNK, ts)
    return pl.pallas_call(
        functools.partial(_rglru_kernel, ts=ts, ch=ch, c_w=c_w),
        grid=(b, s // ts),
        in_specs=[pl.BlockSpec((1, ts, c_w), lambda bi, si: (bi, si, 0)),
                  pl.BlockSpec((1, ts, c_w), lambda bi, si: (bi, si, 1)),
                  pl.BlockSpec((CONV_WIDTH, c_w), lambda bi, si: (0, 0)),
                  pl.BlockSpec((1, c_w), lambda bi, si: (0, 0)),
                  pl.BlockSpec((c_w, 2 * c_w), lambda bi, si: (0, 0)),
                  pl.BlockSpec((1, 2 * c_w), lambda bi, si: (0, 0)),
                  pl.BlockSpec((1, c_w), lambda bi, si: (0, 0))],
        out_specs=pl.BlockSpec((1, ts, c_w), lambda bi, si: (bi, si, 0)),
        out_shape=jax.ShapeDtypeStruct((b, s, c_w), BF16),
        scratch_shapes=[pltpu.VMEM((ts + 8, c_w), F32), pltpu.VMEM((8, c_w), F32),
                        pltpu.VMEM((ts, c_w), F32), pltpu.VMEM((ts, c_w), F32)],
        compiler_params=_cparams(("parallel", "arbitrary")),
        name="rglru",
    )(xg, xg, conv_w, conv_b.reshape(1, c_w), w_bd, b_cat.reshape(1, 2 * c_w), lru_lambda.reshape(1, c_w))


def _outproj_kernel(att_ref, rnn_ref, x_ref, wo_ref, nw_ref, wr_ref, br_ref,
                    x1_ref, hn_ref, route_ref, *, att_w):
    y = jnp.dot(att_ref[...], wo_ref[:att_w, :], preferred_element_type=F32)
    y = y + jnp.dot(rnn_ref[...], wo_ref[att_w:, :], preferred_element_type=F32)
    x1 = x_ref[...] + y
    x1_ref[...] = x1
    hn = x1 * lax.rsqrt(jnp.mean(x1 * x1, axis=-1, keepdims=True) + NORM_EPS) * nw_ref[...]
    hn_ref[...] = hn
    lg = jnp.dot(hn.astype(BF16), wr_ref[...], preferred_element_type=F32) + br_ref[...]

    col = lax.broadcasted_iota(jnp.int32, lg.shape, 1)
    colf = col.astype(F32)
    big = float(LANES)
    ninf = -jnp.inf
    is_g = col < N_GROUPS
    lgm = jnp.where(is_g, lg, ninf)
    mg = jnp.max(lgm, axis=1, keepdims=True)
    g_sel = jnp.min(jnp.where(lgm == mg, colf, big), axis=1, keepdims=True)
    pg = 1.0 / jnp.sum(jnp.where(is_g, jnp.exp(lgm - mg), 0.0), axis=1, keepdims=True)
    lo = N_GROUPS + EXPERTS_PER_GROUP * g_sel
    in_grp = (colf >= lo) & (colf < lo + EXPERTS_PER_GROUP)
    lem = jnp.where(in_grp, lg, ninf)
    v1 = jnp.max(lem, axis=1, keepdims=True)
    i1 = jnp.min(jnp.where(lem == v1, colf, big), axis=1, keepdims=True)
    lem2 = jnp.where(colf == i1, ninf, lem)
    v2 = jnp.max(lem2, axis=1, keepdims=True)
    i2 = jnp.min(jnp.where(lem2 == v2, colf, big), axis=1, keepdims=True)
    e2 = jnp.exp(v2 - v1)
    den = 1.0 + e2
    g1 = pg / den
    g2 = pg * e2 / den
    route_ref[...] = jnp.where(col == 0, g1,
                     jnp.where(col == 1, g2,
                     jnp.where(col == 2, i1 - N_GROUPS,
                     jnp.where(col == 3, i2 - N_GROUPS, 0.0))))


def _out_proj(att, rnn, xf, w_out_bf, norm_w, w_route_bf, b_route):
    t, d = xf.shape
    att_w = att.shape[1]
    tm = min(ROW_TILE, t)
    row = lambda i: (i, 0)
    fix = lambda i: (0, 0)
    return pl.pallas_call(
        functools.partial(_outproj_kernel, att_w=att_w),
        grid=(t // tm,),
        in_specs=[pl.BlockSpec((tm, att_w), row), pl.BlockSpec((tm, rnn.shape[1]), row),
                  pl.BlockSpec((tm, d), row), pl.BlockSpec(w_out_bf.shape, fix),
                  pl.BlockSpec((1, d), fix), pl.BlockSpec((d, LANES), fix), pl.BlockSpec((1, LANES), fix)],
        out_specs=[pl.BlockSpec((tm, d), row), pl.BlockSpec((tm, d), row), pl.BlockSpec((tm, LANES), row)],
        out_shape=[jax.ShapeDtypeStruct((t, d), F32), jax.ShapeDtypeStruct((t, d), F32),
                   jax.ShapeDtypeStruct((t, LANES), F32)],
        compiler_params=_cparams(("parallel",)),
        name="out_proj",
    )(att, rnn, xf, w_out_bf, norm_w.reshape(1, d), w_route_bf, b_route)


def _moe_kernel(te_ref, nu_ref, tok_ref, tokn_ref, dst_ref, hn_hbm, wg_ref, wu_ref, wd_ref, y_hbm,
                xbuf, ybuf, wgb, wub, wdb, gsem, ssem, *, tm):
    i = pl.program_id(0)
    nu = nu_ref[0]
    slot = i % 2

    def gather(tref, sl):
        def body(r, c):
            tok = tref[0, 0, r]
            pltpu.make_async_copy(hn_hbm.at[pl.ds(tok, 1), :], xbuf.at[sl, pl.ds(r, 1), :], gsem.at[sl]).start()
            return c
        lax.fori_loop(0, tm, body, 0, unroll=8)

    def wait_gather(sl):
        pltpu.make_async_copy(hn_hbm.at[pl.ds(0, tm), :], xbuf.at[sl], gsem.at[sl]).wait()

    def wait_scatter(sl):
        pltpu.make_async_copy(ybuf.at[sl], y_hbm.at[pl.ds(0, tm), :], ssem.at[sl]).wait()

    @pl.when(i == 0)
    def _():
        gather(tok_ref, 0)
        ybuf[...] = jnp.zeros(ybuf.shape, F32)
        n_out = y_hbm.shape[0]
        for sl in range(2):
            cp = pltpu.make_async_copy(ybuf.at[sl], y_hbm.at[pl.ds(n_out - (2 - sl) * tm, tm), :], ssem.at[sl])
            cp.start()
            cp.wait()

    @pl.when(i < nu)
    def _():
        @pl.when(i + 1 < nu)
        def _():
            gather(tokn_ref, 1 - slot)

        changed = jnp.logical_or(i == 0, te_ref[i] != te_ref[jnp.maximum(i - 1, 0)])

        @pl.when(changed)
        def _():
            wgb[...] = wg_ref[0].astype(BF16)
            wub[...] = wu_ref[0].astype(BF16)
            wdb[...] = wd_ref[0].astype(BF16)

        wait_gather(slot)
        x = xbuf[slot].astype(BF16)
        g = jnp.dot(x, wgb[...], preferred_element_type=F32)
        u = jnp.dot(x, wub[...], preferred_element_type=F32)
        hdn = (g * jax.nn.sigmoid(g) * u).astype(BF16)
        y = jnp.dot(hdn, wdb[...], preferred_element_type=F32)

        @pl.when(i >= 2)
        def _():
            wait_scatter(slot)

        ybuf[slot] = y

        def sbody(r, c):
            d = dst_ref[0, 0, r]
            pltpu.make_async_copy(ybuf.at[slot, pl.ds(r, 1), :], y_hbm.at[pl.ds(d, 1), :], ssem.at[slot]).start()
            return c
        lax.fori_loop(0, tm, sbody, 0, unroll=8)

        @pl.when(i == nu - 1)
        def _():
            wait_scatter(slot)

            @pl.when(i >= 1)
            def _():
                wait_scatter(1 - slot)


def _moe(hn, tile_e, n_used, slot_tok, slot_dst, w_g, w_u, w_d, n_rows_out, tm):
    t, d = hn.shape
    n_tiles = tile_e.shape[0]
    ff = w_g.shape[2]
    tok3 = slot_tok.reshape(n_tiles, 1, tm)
    dst3 = slot_dst.reshape(n_tiles, 1, tm)
    smem_blk = lambda f: pl.BlockSpec((1, 1, tm), f, memory_space=pltpu.SMEM)
    grid_spec = pltpu.PrefetchScalarGridSpec(
        num_scalar_prefetch=2,
        grid=(n_tiles,),
        in_specs=[smem_blk(lambda i, te, nu: (i, 0, 0)),
                  smem_blk(lambda i, te, nu: (jnp.minimum(i + 1, n_tiles - 1), 0, 0)),
                  smem_blk(lambda i, te, nu: (i, 0, 0)),
                  pl.BlockSpec(memory_space=pl.ANY),
                  pl.BlockSpec((1, d, ff), lambda i, te, nu: (te[i], 0, 0)),
                  pl.BlockSpec((1, d, ff), lambda i, te, nu: (te[i], 0, 0)),
                  pl.BlockSpec((1, ff, d), lambda i, te, nu: (te[i], 0, 0))],
        out_specs=pl.BlockSpec(memory_space=pl.ANY),
        scratch_shapes=[pltpu.VMEM((2, tm, d), F32), pltpu.VMEM((2, tm, d), F32),
                        pltpu.VMEM((d, ff), BF16), pltpu.VMEM((d, ff), BF16), pltpu.VMEM((ff, d), BF16),
                        pltpu.SemaphoreType.DMA((2,)), pltpu.SemaphoreType.DMA((2,))],
    )
    return pl.pallas_call(
        functools.partial(_moe_kernel, tm=tm),
        grid_spec=grid_spec,
        out_shape=jax.ShapeDtypeStruct((n_rows_out, d), F32),
        compiler_params=_cparams(("arbitrary",), has_side_effects=True),
        name="moe",
    )(tile_e, n_used, tok3, tok3, dst3, hn, w_g, w_u, w_d)


def _combine_kernel(x1_ref, y0_ref, y1_ref, route_ref, nw_ref, o_ref):
    g0 = route_ref[:, 0:1]
    g1 = route_ref[:, 1:2]
    x = x1_ref[...] + g0 * y0_ref[...] + g1 * y1_ref[...]
    o_ref[...] = x * lax.rsqrt(jnp.mean(x * x, axis=-1, keepdims=True) + NORM_EPS) * nw_ref[...]


def _combine(x1, y, route, norm_w):
    t, d = x1.shape
    tm = min(ROW_TILE, t)
    nb = t // tm
    return pl.pallas_call(
        _combine_kernel,
        grid=(nb,),
        in_specs=[pl.BlockSpec((tm, d), lambda i: (i, 0)),
                  pl.BlockSpec((tm, d), lambda i: (i, 0)),
                  pl.BlockSpec((tm, d), lambda i: (nb + i, 0)),
                  pl.BlockSpec((tm, LANES), lambda i: (i, 0)),
                  pl.BlockSpec((1, d), lambda i: (0, 0))],
        out_specs=pl.BlockSpec((tm, d), lambda i: (i, 0)),
        out_shape=jax.ShapeDtypeStruct((t, d), F32),
        compiler_params=_cparams(("parallel",)),
        name="combine",
    )(x1, y, y, route, norm_w.reshape(1, d))


def _routing_tables(route, t, tm):
    eid = route[:, 2:4].astype(jnp.int32).reshape(-1)
    m = eid.shape[0]
    onehot = (eid[:, None] == jnp.arange(N_EXPERTS, dtype=jnp.int32)[None, :]).astype(jnp.int32)
    csum = jnp.cumsum(onehot, axis=0)
    rank = jnp.sum(csum * onehot, axis=1) - 1
    counts = csum[-1]
    padded = (counts + tm - 1) // tm * tm
    pend = jnp.cumsum(padded)
    pstart = pend - padded
    dest = pstart[eid] + rank
    n_tiles = -(-(m + N_EXPERTS * (tm - 1)) // tm)
    p = n_tiles * tm
    a = jnp.arange(m, dtype=jnp.int32)
    tok = a // TOP_K
    slot = jnp.arange(p, dtype=jnp.int32)
    dump = TOP_K * t + ((slot // tm) % 2) * tm + slot % tm
    slot_tok = jnp.zeros((p,), jnp.int32).at[dest].set(tok)
    slot_dst = dump.at[dest].set((a % TOP_K) * t + tok)
    tile_e = jnp.minimum(jnp.searchsorted(pend, jnp.arange(n_tiles, dtype=jnp.int32) * tm, side='right'),
                         N_EXPERTS - 1).astype(jnp.int32)
    n_used = (pend[-1] // tm).astype(jnp.int32).reshape(1)
    return tile_e, n_used, slot_tok, slot_dst


def _block_diag(w):
    n, bi, bj = w.shape
    eye = jnp.eye(n, dtype=w.dtype)
    return jnp.einsum('nij,nm->nimj', w, eye).reshape(n * bi, n * bj)


def kernel(x, mix_norm_w, w_in, lambda_q1, lambda_k1, lambda_q2, lambda_k2, head_norm_w, conv_w, conv_b, w_rgate, b_rgate, w_igate, b_igate, lru_lambda, w_out, ffn_norm_w, w_router_group, b_router_group, w_router_expert, b_router_expert, w_exp_gate, w_exp_up, w_exp_down, final_norm_w):
    b, s, d = x.shape
    t = b * s
    assert w_in.shape[0] == 1, "single-layer stack only"
    att_w = N_ATT_HEADS * HEAD_DIM
    tm_moe = MOE_TILE
    xf = x.reshape(t, d)
    for l in range(1):
        lambda_init = 0.8 - 0.6 * math.exp(-0.3 * l)
        qkv, xg = _in_proj(xf, mix_norm_w[l], w_in[l].astype(BF16), att_w)
        lam_params = jnp.stack([lambda_q1[l], lambda_k1[l], lambda_q2[l], lambda_k2[l]]).astype(F32)
        att = _diff_attention(qkv.reshape(b, s, 3 * att_w), lam_params, head_norm_w[l], lambda_init)
        w_bd = jnp.concatenate([_block_diag(w_rgate[l]), _block_diag(w_igate[l])], axis=1).astype(BF16)
        b_cat = jnp.concatenate([b_rgate[l], b_igate[l]])
        rnn = _rglru(xg.reshape(b, s, xg.shape[1]), conv_w[l], conv_b[l], w_bd, b_cat, lru_lambda[l])
        w_route = jnp.concatenate([w_router_group[l], w_router_expert[l]], axis=1)
        w_route = jnp.pad(w_route, ((0, 0), (0, LANES - w_route.shape[1]))).astype(BF16)
        b_route = jnp.concatenate([b_router_group[l], b_router_expert[l]])
        b_route = jnp.pad(b_route, (0, LANES - b_route.shape[0])).reshape(1, LANES).astype(F32)
        x1, hn, route = _out_proj(att.reshape(t, att_w), rnn.reshape(t, -1), xf, w_out[l].astype(BF16),
                                  ffn_norm_w[l], w_route, b_route)
        tile_e, n_used, slot_tok, slot_dst = _routing_tables(route, t, tm_moe)
        n_rows_out = TOP_K * t + 2 * tm_moe
        y = _moe(hn, tile_e, n_used, slot_tok, slot_dst, w_exp_gate[l], w_exp_up[l], w_exp_down[l],
                 n_rows_out, tm_moe)
        out = _combine(x1, y, route, final_norm_w)
    return out.reshape(b, s, d)
```

```python
import functools
import math

import numpy as np
import jax
import jax.numpy as jnp
from jax import lax
from jax.experimental import pallas as pl
from jax.experimental.pallas import tpu as pltpu

F32 = jnp.float32
BF16 = jnp.bfloat16

N_ATT_HEADS = 4
HEAD_DIM = 128
QK_DIM = 64
N_RNN_BLOCKS = 8
CONV_WIDTH = 4
LRU_C = 8.0
N_GROUPS = 4
EXPERTS_PER_GROUP = 8
N_EXPERTS = N_GROUPS * EXPERTS_PER_GROUP
TOP_K = 2
NORM_EPS = 1e-6
HEAD_NORM_EPS = 1e-5
LANES = 128
NEG_BIG = -1e30

ROW_TILE = 512
ATT_Q_TILE = 512
ATT_KV_TILE = 1024
V_ROWS = HEAD_DIM + 16
LRU_TILE = 512
LRU_CHUNK = 128
MOE_TILE = 256
VMEM_LIMIT = 48 * 1024 * 1024


def _cparams(sem, vmem=VMEM_LIMIT, **kw):
    return pltpu.CompilerParams(dimension_semantics=sem, vmem_limit_bytes=vmem, **kw)


def _inproj_kernel(x_ref, nw_ref, w_ref, qkv_ref, xg_ref, *, att_w):
    x = x_ref[...]
    ms = jnp.mean(x * x, axis=-1, keepdims=True)
    hn = (x * lax.rsqrt(ms + NORM_EPS) * nw_ref[...]).astype(BF16)
    p = jnp.dot(hn, w_ref[...], preferred_element_type=F32)
    scale = QK_DIM ** -0.5
    qkv_ref[:, :att_w] = (p[:, :att_w] * scale).astype(BF16)
    qkv_ref[:, att_w:] = p[:, att_w:3 * att_w].astype(BF16)
    xg_ref[...] = p[:, 3 * att_w:]


def _in_proj(xf, norm_w, w_in_bf, att_w):
    t, d = xf.shape
    n = w_in_bf.shape[1]
    tm = min(ROW_TILE, t)
    return pl.pallas_call(
        functools.partial(_inproj_kernel, att_w=att_w),
        grid=(t // tm,),
        in_specs=[pl.BlockSpec((tm, d), lambda i: (i, 0)),
                  pl.BlockSpec((1, d), lambda i: (0, 0)),
                  pl.BlockSpec((d, n), lambda i: (0, 0))],
        out_specs=[pl.BlockSpec((tm, 3 * att_w), lambda i: (i, 0)),
                   pl.BlockSpec((tm, n - 3 * att_w), lambda i: (i, 0))],
        out_shape=[jax.ShapeDtypeStruct((t, 3 * att_w), BF16),
                   jax.ShapeDtypeStruct((t, n - 3 * att_w), F32)],
        compiler_params=_cparams(("parallel",)),
        name="in_proj",
    )(xf, norm_w.reshape(1, d), w_in_bf)


def _attn_kernel(slope_ref, lam_ref, hw_ref, q_ref, k_ref, v_ref, o_ref,
                 k1a, k2a, vt, m1, a1, m2, a2, *, tq, tk, s_len, lambda_init):
    h = pl.program_id(1)
    qi = pl.program_id(2)
    grp = tk // tq
    slope = slope_ref[h]
    lane = lax.broadcasted_iota(jnp.int32, (tq, HEAD_DIM), 1)

    @pl.when(qi == 0)
    def _():
        ones_row = jnp.where(lax.broadcasted_iota(jnp.int32, (V_ROWS - HEAD_DIM, tq), 0) == 0, 1.0, 0.0)
        for c in range(s_len // tq):
            rows = slice(c * tq, (c + 1) * tq)
            kk = k_ref[0, rows, :].astype(F32)
            j = c * tq + lax.broadcasted_iota(jnp.int32, (tq, HEAD_DIM), 0)
            j_lo = (j & 255).astype(F32)
            j_hi = (j - (j & 255)).astype(F32)
            aug = jnp.where(lane == QK_DIM, slope * j_hi, jnp.where(lane == QK_DIM + 1, slope * j_lo, 0.0))
            k1a[rows, :] = jnp.where(lane < QK_DIM, kk, aug).astype(BF16)
            k2a[rows, :] = jnp.where(lane < QK_DIM, pltpu.roll(kk, QK_DIM, axis=1), aug).astype(BF16)
            vt[c, :HEAD_DIM, :] = v_ref[0, rows, :].astype(F32).T.astype(BF16)
            vt[c, HEAD_DIM:, :] = ones_row.astype(BF16)

    ones_cols = jnp.where((lane == QK_DIM) | (lane == QK_DIM + 1), 1.0, 0.0)
    q = q_ref[0].astype(F32)
    q1t = jnp.where(lane < QK_DIM, q, ones_cols).T.astype(BF16)
    q2t = jnp.where(lane < QK_DIM, pltpu.roll(q, QK_DIM, axis=1), ones_cols).T.astype(BF16)

    for m, a in ((m1, a1), (m2, a2)):
        m[...] = jnp.full(m.shape, NEG_BIG, F32)
        a[...] = jnp.zeros(a.shape, F32)

    def tile(c0, n_sub, masked):
        width = n_sub * tq
        rows = pl.ds(pl.multiple_of(c0 * tq, tq), width)
        if masked:
            keep = (lax.broadcasted_iota(jnp.int32, (width, tq), 0)
                    <= lax.broadcasted_iota(jnp.int32, (width, tq), 1))
        for qt, ka, m, a in ((q1t, k1a, m1, a1), (q2t, k2a, m2, a2)):
            s = jnp.dot(ka[rows, :], qt, preferred_element_type=F32)
            if masked:
                s = jnp.where(keep, s, NEG_BIG)
            m_prev = m[...]
            m_new = jnp.maximum(m_prev, jnp.max(s, axis=0, keepdims=True))
            alpha = jnp.exp(m_prev - m_new)
            p = jnp.exp(s - m_new).astype(BF16)
            pv = jnp.dot(vt[c0], p[:tq], preferred_element_type=F32)
            for g in range(1, n_sub):
                pv = pv + jnp.dot(vt[c0 + g], p[g * tq:(g + 1) * tq], preferred_element_type=F32)
            a[...] = alpha * a[...] + pv
            m[...] = m_new

    def full_body(j, c):
        tile(j * grp, grp, False)
        return c

    def rest_body(c0, c):
        tile(c0, 1, False)
        return c

    n_full = qi // grp
    lax.fori_loop(0, n_full, full_body, 0)
    if grp > 1:
        lax.fori_loop(n_full * grp, qi, rest_body, 0)
    tile(qi, 1, True)

    lam = (jnp.exp(jnp.sum(lam_ref[0:1, :] * lam_ref[1:2, :], axis=1, keepdims=True))
           - jnp.exp(jnp.sum(lam_ref[2:3, :] * lam_ref[3:4, :], axis=1, keepdims=True))
           + lambda_init)
    o1 = a1[:HEAD_DIM, :] / a1[HEAD_DIM:HEAD_DIM + 1, :]
    o2 = a2[:HEAD_DIM, :] / a2[HEAD_DIM:HEAD_DIM + 1, :]
    o = (o1 - lam * o2).T
    o = o * lax.rsqrt(jnp.mean(o * o, axis=-1, keepdims=True) + HEAD_NORM_EPS)
    o_ref[0] = (o * hw_ref[...] * (1.0 - lambda_init)).astype(o_ref.dtype)


def _diff_attention(qkv, lam_params, head_norm_w, lambda_init):
    b, s, w3 = qkv.shape
    nh = N_ATT_HEADS
    tq = min(ATT_Q_TILE, s)
    tk = min(ATT_KV_TILE, s)
    slopes = jnp.asarray(np.array([2.0 ** (-8.0 * (i + 1) / nh) for i in range(nh)], dtype=np.float32))
    return pl.pallas_call(
        functools.partial(_attn_kernel, tq=tq, tk=tk, s_len=s, lambda_init=lambda_init),
        grid=(b, nh, s // tq),
        in_specs=[pl.BlockSpec(memory_space=pltpu.SMEM),
                  pl.BlockSpec((4, QK_DIM), lambda bi, hi, qi: (0, 0)),
                  pl.BlockSpec((1, HEAD_DIM), lambda bi, hi, qi: (0, 0)),
                  pl.BlockSpec((1, tq, HEAD_DIM), lambda bi, hi, qi: (bi, qi, hi)),
                  pl.BlockSpec((1, s, HEAD_DIM), lambda bi, hi, qi: (bi, 0, nh + hi)),
                  pl.BlockSpec((1, s, HEAD_DIM), lambda bi, hi, qi: (bi, 0, 2 * nh + hi))],
        out_specs=pl.BlockSpec((1, tq, HEAD_DIM), lambda bi, hi, qi: (bi, qi, hi)),
        out_shape=jax.ShapeDtypeStruct((b, s, nh * HEAD_DIM), BF16),
        scratch_shapes=[pltpu.VMEM((s, HEAD_DIM), BF16), pltpu.VMEM((s, HEAD_DIM), BF16),
                        pltpu.VMEM((s // tq, V_ROWS, tq), BF16),
                        pltpu.VMEM((1, tq), F32), pltpu.VMEM((V_ROWS, tq), F32),
                        pltpu.VMEM((1, tq), F32), pltpu.VMEM((V_ROWS, tq), F32)],
        compiler_params=_cparams(("parallel", "parallel", "arbitrary")),
        name="diff_attn",
    )(slopes, lam_params, head_norm_w.reshape(1, HEAD_DIM), qkv, qkv, qkv)


def _gelu_tanh(x):
    return 0.5 * x * (1.0 + jnp.tanh(math.sqrt(2.0 / math.pi) * (x + 0.044715 * (x * x * x))))


def _rglru_kernel(xr_ref, gr_ref, cw_ref, cb_ref, w_ref, b_ref, lam_ref, o_ref,
                  xs, carry_h, a_s, u_s, *, ts, ch, c_w):
    si = pl.program_id(1)

    @pl.when(si == 0)
    def _():
        xs[0:8, :] = jnp.zeros((8, c_w), F32)
        carry_h[...] = jnp.zeros(carry_h.shape, F32)

    xs[8:, :] = xr_ref[0]
    neg_lam = -lam_ref[...]
    sp = jnp.maximum(neg_lam, 0.0) + jnp.log1p(jnp.exp(-jnp.abs(neg_lam)))
    cw = cw_ref[...]
    cb = cb_ref[...]
    bias = b_ref[...]
    r8 = lax.broadcasted_iota(jnp.int32, (ch, c_w), 0) & 7

    def chunk(c, carry):
        r0 = pl.multiple_of(c * ch, ch)
        win = xs[pl.ds(r0, ch + 8), :]
        xc = cw[3:4, :] * win[8:] + cb
        for k in (1, 2, 3):
            xc = xc + cw[3 - k:4 - k, :] * pltpu.roll(win, k, axis=0)[8:]
        z = jnp.dot(xc.astype(BF16), w_ref[...], preferred_element_type=F32) + bias
        r = jax.nn.sigmoid(z[:, :c_w])
        ig = jax.nn.sigmoid(z[:, c_w:])
        log_a = (-LRU_C) * r * sp
        a = jnp.exp(log_a)
        u = jnp.sqrt(jnp.tanh(-log_a) * (1.0 + a * a)) * ig * xc
        for k in (1, 2, 4):
            a_sh = pltpu.roll(a, k, axis=0)
            u_sh = pltpu.roll(u, k, axis=0)
            ok = r8 >= k
            u = jnp.where(ok, u + a * u_sh, u)
            a = jnp.where(ok, a * a_sh, a)
        a_s[pl.ds(r0, ch), :] = a
        u_s[pl.ds(r0, ch), :] = u
        return carry

    lax.fori_loop(0, ts // ch, chunk, 0)

    def grp(g, hprev):
        r0 = pl.multiple_of(g * 8, 8)
        hg = u_s[pl.ds(r0, 8), :] + a_s[pl.ds(r0, 8), :] * hprev
        u_s[pl.ds(r0, 8), :] = hg
        return hg[7:8, :]

    hlast = lax.fori_loop(0, ts // 8, grp, carry_h[0:1, :], unroll=8)
    carry_h[0:1, :] = hlast
    xs[0:8, :] = xs[ts:ts + 8, :]
    o_ref[0] = (u_s[...] * _gelu_tanh(gr_ref[0])).astype(o_ref.dtype)


def _rglru(xg, conv_w, conv_b, w_bd, b_cat, lru_lambda):
    b, s, w2 = xg.shape
    c_w = w2 // 2
    ts = min(LRU_TILE, s)
    ch = min(LRU_CHUNK, ts)
    return pl.pallas_call(
        functools.partial(_rglru_kernel, ts=ts, ch=ch, c_w=c_w),
        grid=(b, s // ts),
        in_specs=[pl.BlockSpec((1, ts, c_w), lambda bi, si: (bi, si, 0)),
                  pl.BlockSpec((1, ts, c_w), lambda bi, si: (bi, si, 1)),
                  pl.BlockSpec((CONV_WIDTH, c_w), lambda bi, si: (0, 0)),
                  pl.BlockSpec((1, c_w), lambda bi, si: (0, 0)),
                  pl.BlockSpec((c_w, 2 * c_w), lambda bi, si: (0, 0)),
                  pl.BlockSpec((1, 2 * c_w), lambda bi, si: (0, 0)),
                  pl.BlockSpec((1, c_w), lambda bi, si: (0, 0))],
        out_specs=pl.BlockSpec((1, ts, c_w), lambda bi, si: (bi, si, 0)),
        out_shape=jax.ShapeDtypeStruct((b, s, c_w), BF16),
        scratch_shapes=[pltpu.VMEM((ts + 8, c_w), F32), pltpu.VMEM((8, c_w), F32),
                        pltpu.VMEM((ts, c_w), F32), pltpu.VMEM((ts, c_w), F32)],
        compiler_params=_cparams(("parallel", "arbitrary")),
        name="rglru",
    )(xg, xg, conv_w, conv_b.reshape(1, c_w), w_bd, b_cat.reshape(1, 2 * c_w), lru_lambda.reshape(1, c_w))


def _outproj_kernel(att_ref, rnn_ref, x_ref, wo_ref, nw_ref, wr_ref, br_ref,
                    x1_ref, hn_ref, route_ref, *, att_w):
    y = jnp.dot(att_ref[...], wo_ref[:att_w, :], preferred_element_type=F32)
    y = y + jnp.dot(rnn_ref[...], wo_ref[att_w:, :], preferred_element_type=F32)
    x1 = x_ref[...] + y
    x1_ref[...] = x1
    hn = x1 * lax.rsqrt(jnp.mean(x1 * x1, axis=-1, keepdims=True) + NORM_EPS) * nw_ref[...]
    hn_ref[...] = hn
    lg = jnp.dot(hn.astype(BF16), wr_ref[...], preferred_element_type=F32) + br_ref[...]

    col = lax.broadcasted_iota(jnp.int32, lg.shape, 1)
    colf = col.astype(F32)
    big = float(LANES)
    ninf = -jnp.inf
    is_g = col < N_GROUPS
    lgm = jnp.where(is_g, lg, ninf)
    mg = jnp.max(lgm, axis=1, keepdims=True)
    g_sel = jnp.min(jnp.where(lgm == mg, colf, big), axis=1, keepdims=True)
    pg = 1.0 / jnp.sum(jnp.where(is_g, jnp.exp(lgm - mg), 0.0), axis=1, keepdims=True)
    lo = N_GROUPS + EXPERTS_PER_GROUP * g_sel
    in_grp = (colf >= lo) & (colf < lo + EXPERTS_PER_GROUP)
    lem = jnp.where(in_grp, lg, ninf)
    v1 = jnp.max(lem, axis=1, keepdims=True)
    i1 = jnp.min(jnp.where(lem == v1, colf, big), axis=1, keepdims=True)
    lem2 = jnp.where(colf == i1, ninf, lem)
    v2 = jnp.max(lem2, axis=1, keepdims=True)
    i2 = jnp.min(jnp.where(lem2 == v2, colf, big), axis=1, keepdims=True)
    e2 = jnp.exp(v2 - v1)
    den = 1.0 + e2
    g1 = pg / den
    g2 = pg * e2 / den
    route_ref[...] = jnp.where(col == 0, g1,
                     jnp.where(col == 1, g2,
                     jnp.where(col == 2, i1 - N_GROUPS,
                     jnp.where(col == 3, i2 - N_GROUPS, 0.0))))


def _out_proj(att, rnn, xf, w_out_bf, norm_w, w_route_bf, b_route):
    t, d = xf.shape
    att_w = att.shape[1]
    tm = min(ROW_TILE, t)
    row = lambda i: (i, 0)
    fix = lambda i: (0, 0)
    return pl.pallas_call(
        functools.partial(_outproj_kernel, att_w=att_w),
        grid=(t // tm,),
        in_specs=[pl.BlockSpec((tm, att_w), row), pl.BlockSpec((tm, rnn.shape[1]), row),
                  pl.BlockSpec((tm, d), row), pl.BlockSpec(w_out_bf.shape, fix),
                  pl.BlockSpec((1, d), fix), pl.BlockSpec((d, LANES), fix), pl.BlockSpec((1, LANES), fix)],
        out_specs=[pl.BlockSpec((tm, d), row), pl.BlockSpec((tm, d), row), pl.BlockSpec((tm, LANES), row)],
        out_shape=[jax.ShapeDtypeStruct((t, d), F32), jax.ShapeDtypeStruct((t, d), F32),
                   jax.ShapeDtypeStruct((t, LANES), F32)],
        compiler_params=_cparams(("parallel",)),
        name="out_proj",
    )(att, rnn, xf, w_out_bf, norm_w.reshape(1, d), w_route_bf, b_route)


def _moe_kernel(te_ref, nu_ref, tok_ref, tokn_ref, dst_ref, hn_hbm, wg_ref, wu_ref, wd_ref, y_hbm,
                xbuf, ybuf, wgb, wub, wdb, gsem, ssem, *, tm):
    i = pl.program_id(0)
    nu = nu_ref[0]
    slot = i % 2

    def gather(tref, sl):
        def body(r, c):
            tok = tref[0, 0, r]
            pltpu.make_async_copy(hn_hbm.at[pl.ds(tok, 1), :], xbuf.at[sl, pl.ds(r, 1), :], gsem.at[sl]).start()
            return c
        lax.fori_loop(0, tm, body, 0, unroll=8)

    def wait_gather(sl):
        pltpu.make_async_copy(hn_hbm.at[pl.ds(0, tm), :], xbuf.at[sl], gsem.at[sl]).wait()

    def wait_scatter(sl):
        pltpu.make_async_copy(ybuf.at[sl], y_hbm.at[pl.ds(0, tm), :], ssem.at[sl]).wait()

    @pl.when(i == 0)
    def _():
        gather(tok_ref, 0)
        ybuf[...] = jnp.zeros(ybuf.shape, F32)
        n_out = y_hbm.shape[0]
        for sl in range(2):
            cp = pltpu.make_async_copy(ybuf.at[sl], y_hbm.at[pl.ds(n_out - (2 - sl) * tm, tm), :], ssem.at[sl])
            cp.start()
            cp.wait()

    @pl.when(i < nu)
    def _():
        @pl.when(i + 1 < nu)
        def _():
            gather(tokn_ref, 1 - slot)

        changed = jnp.logical_or(i == 0, te_ref[i] != te_ref[jnp.maximum(i - 1, 0)])

        @pl.when(changed)
        def _():
            wgb[...] = wg_ref[0].astype(BF16)
            wub[...] = wu_ref[0].astype(BF16)
            wdb[...] = wd_ref[0].astype(BF16)

        wait_gather(slot)
        x = xbuf[slot].astype(BF16)
        g = jnp.dot(x, wgb[...], preferred_element_type=F32)
        u = jnp.dot(x, wub[...], preferred_element_type=F32)
        hdn = (g * jax.nn.sigmoid(g) * u).astype(BF16)
        y = jnp.dot(hdn, wdb[...], preferred_element_type=F32)

        @pl.when(i >= 2)
        def _():
            wait_scatter(slot)

        ybuf[slot] = y

        def sbody(r, c):
            d = dst_ref[0, 0, r]
            pltpu.make_async_copy(ybuf.at[slot, pl.ds(r, 1), :], y_hbm.at[pl.ds(d, 1), :], ssem.at[slot]).start()
            return c
        lax.fori_loop(0, tm, sbody, 0, unroll=8)

        @pl.when(i == nu - 1)
        def _():
            wait_scatter(slot)

            @pl.when(i >= 1)
            def _():
                wait_scatter(1 - slot)


def _moe(hn, tile_e, n_used, slot_tok, slot_dst, w_g, w_u, w_d, n_rows_out, tm):
    t, d = hn.shape
    n_tiles = tile_e.shape[0]
    ff = w_g.shape[2]
    tok3 = slot_tok.reshape(n_tiles, 1, tm)
    dst3 = slot_dst.reshape(n_tiles, 1, tm)
    smem_blk = lambda f: pl.BlockSpec((1, 1, tm), f, memory_space=pltpu.SMEM)
    grid_spec = pltpu.PrefetchScalarGridSpec(
        num_scalar_prefetch=2,
        grid=(n_tiles,),
        in_specs=[smem_blk(lambda i, te, nu: (i, 0, 0)),
                  smem_blk(lambda i, te, nu: (jnp.minimum(i + 1, n_tiles - 1), 0, 0)),
                  smem_blk(lambda i, te, nu: (i, 0, 0)),
                  pl.BlockSpec(memory_space=pl.ANY),
                  pl.BlockSpec((1, d, ff), lambda i, te, nu: (te[i], 0, 0)),
                  pl.BlockSpec((1, d, ff), lambda i, te, nu: (te[i], 0, 0)),
                  pl.BlockSpec((1, ff, d), lambda i, te, nu: (te[i], 0, 0))],
        out_specs=pl.BlockSpec(memory_space=pl.ANY),
        scratch_shapes=[pltpu.VMEM((2, tm, d), F32), pltpu.VMEM((2, tm, d), F32),
                        pltpu.VMEM((d, ff), BF16), pltpu.VMEM((d, ff), BF16), pltpu.VMEM((ff, d), BF16),
                        pltpu.SemaphoreType.DMA((2,)), pltpu.SemaphoreType.DMA((2,))],
    )
    return pl.pallas_call(
        functools.partial(_moe_kernel, tm=tm),
        grid_spec=grid_spec,
        out_shape=jax.ShapeDtypeStruct((n_rows_out, d), F32),
        compiler_params=_cparams(("arbitrary",), has_side_effects=True),
        name="moe",
    )(tile_e, n_used, tok3, tok3, dst3, hn, w_g, w_u, w_d)


def _combine_kernel(x1_ref, y0_ref, y1_ref, route_ref, nw_ref, o_ref):
    g0 = route_ref[:, 0:1]
    g1 = route_ref[:, 1:2]
    x = x1_ref[...] + g0 * y0_ref[...] + g1 * y1_ref[...]
    o_ref[...] = x * lax.rsqrt(jnp.mean(x * x, axis=-1, keepdims=True) + NORM_EPS) * nw_ref[...]


def _combine(x1, y, route, norm_w):
    t, d = x1.shape
    tm = min(ROW_TILE, t)
    nb = t // tm
    return pl.pallas_call(
        _combine_kernel,
        grid=(nb,),
        in_specs=[pl.BlockSpec((tm, d), lambda i: (i, 0)),
                  pl.BlockSpec((tm, d), lambda i: (i, 0)),
                  pl.BlockSpec((tm, d), lambda i: (nb + i, 0)),
                  pl.BlockSpec((tm, LANES), lambda i: (i, 0)),
                  pl.BlockSpec((1, d), lambda i: (0, 0))],
        out_specs=pl.BlockSpec((tm, d), lambda i: (i, 0)),
        out_shape=jax.ShapeDtypeStruct((t, d), F32),
        compiler_params=_cparams(("parallel",)),
        name="combine",
    )(x1, y, y, route, norm_w.reshape(1, d))


def _routing_tables(route, t, tm):
    eid = route[:, 2:4].astype(jnp.int32).reshape(-1)
    m = eid.shape[0]
    onehot = (eid[:, None] == jnp.arange(N_EXPERTS, dtype=jnp.int32)[None, :]).astype(jnp.int32)
    csum = jnp.cumsum(onehot, axis=0)
    rank = jnp.sum(csum * onehot, axis=1) - 1
    counts = csum[-1]
    padded = (counts + tm - 1) // tm * tm
    pend = jnp.cumsum(padded)
    pstart = pend - padded
    dest = pstart[eid] + rank
    n_tiles = -(-(m + N_EXPERTS * (tm - 1)) // tm)
    p = n_tiles * tm
    a = jnp.arange(m, dtype=jnp.int32)
    tok = a // TOP_K
    slot = jnp.arange(p, dtype=jnp.int32)
    dump = TOP_K * t + ((slot // tm) % 2) * tm + slot % tm
    slot_tok = jnp.zeros((p,), jnp.int32).at[dest].set(tok)
    slot_dst = dump.at[dest].set((a % TOP_K) * t + tok)
    tile_e = jnp.minimum(jnp.searchsorted(pend, jnp.arange(n_tiles, dtype=jnp.int32) * tm, side='right'),
                         N_EXPERTS - 1).astype(jnp.int32)
    n_used = (pend[-1] // tm).astype(jnp.int32).reshape(1)
    return tile_e, n_used, slot_tok, slot_dst


def _block_diag(w):
    n, bi, bj = w.shape
    eye = jnp.eye(n, dtype=w.dtype)
    return jnp.einsum('nij,nm->nimj', w, eye).reshape(n * bi, n * bj)


def kernel(x, mix_norm_w, w_in, lambda_q1, lambda_k1, lambda_q2, lambda_k2, head_norm_w, conv_w, conv_b, w_rgate, b_rgate, w_igate, b_igate, lru_lambda, w_out, ffn_norm_w, w_router_group, b_router_group, w_router_expert, b_router_expert, w_exp_gate, w_exp_up, w_exp_down, final_norm_w):
    b, s, d = x.shape
    t = b * s
    assert w_in.shape[0] == 1, "single-layer stack only"
    att_w = N_ATT_HEADS * HEAD_DIM
    tm_moe = MOE_TILE
    xf = x.reshape(t, d)
    for l in range(1):
        lambda_init = 0.8 - 0.6 * math.exp(-0.3 * l)
        qkv, xg = _in_proj(xf, mix_norm_w[l], w_in[l].astype(BF16), att_w)
        lam_params = jnp.stack([lambda_q1[l], lambda_k1[l], lambda_q2[l], lambda_k2[l]]).astype(F32)
        att = _diff_attention(qkv.reshape(b, s, 3 * att_w), lam_params, head_norm_w[l], lambda_init)
        w_bd = jnp.concatenate([_block_diag(w_rgate[l]), _block_diag(w_igate[l])], axis=1).astype(BF16)
        b_cat = jnp.concatenate([b_rgate[l], b_igate[l]])
        rnn = _rglru(xg.reshape(b, s, xg.shape[1]), conv_w[l], conv_b[l], w_bd, b_cat, lru_lambda[l])
        w_route = jnp.concatenate([w_router_group[l], w_router_expert[l]], axis=1)
        w_route = jnp.pad(w_route, ((0, 0), (0, LANES - w_route.shape[1]))).astype(BF16)
        b_route = jnp.concatenate([b_router_group[l], b_router_expert[l]])
        b_route = jnp.pad(b_route, (0, LANES - b_route.shape[0])).reshape(1, LANES).astype(F32)
        x1, hn, route = _out_proj(att.reshape(t, att_w), rnn.reshape(t, -1), xf, w_out[l].astype(BF16),
                                  ffn_norm_w[l], w_route, b_route)
        tile_e, n_used, slot_tok, slot_dst = _routing_tables(route, t, tm_moe)
        n_rows_out = TOP_K * t + 2 * tm_moe
        y = _moe(hn, tile_e, n_used, slot_tok, slot_dst, w_exp_gate[l], w_exp_up[l], w_exp_down[l],
                 n_rows_out, tm_moe)
        out = _combine(x1, y, route, final_norm_w)
    return out.reshape(b, s, d)
```

```python
import functools
import math

import numpy as np
import jax
import jax.numpy as jnp
from jax import lax
from jax.experimental import pallas as pl
from jax.experimental.pallas import tpu as pltpu

F32 = jnp.float32
BF16 = jnp.bfloat16

N_ATT_HEADS = 4
HEAD_DIM = 128
QK_DIM = 64
N_RNN_BLOCKS = 8
CONV_WIDTH = 4
LRU_C = 8.0
N_GROUPS = 4
EXPERTS_PER_GROUP = 8
N_EXPERTS = N_GROUPS * EXPERTS_PER_GROUP
TOP_K = 2
NORM_EPS = 1e-6
HEAD_NORM_EPS = 1e-5
LANES = 128
SUBLANES = 8
NEG_BIG = -1e30

ROW_TILE = 512
ATT_Q_TILE = 512
ATT_KV_TILE = 1024
V_ROWS = HEAD_DIM + 16
LRU_TILE = 512
LRU_CHUNK = 128
MOE_TILE = 256
VMEM_LIMIT = 48 * 1024 * 1024


def _cparams(sem, vmem=VMEM_LIMIT, **kw):
    return pltpu.CompilerParams(dimension_semantics=sem, vmem_limit_bytes=vmem, **kw)


def _inproj_kernel(x_ref, nw_ref, w_ref, qkv_ref, xg_ref, *, att_w):
    x = x_ref[...]
    ms = jnp.mean(x * x, axis=-1, keepdims=True)
    hn = (x * lax.rsqrt(ms + NORM_EPS) * nw_ref[...]).astype(BF16)
    p = jnp.dot(hn, w_ref[...], preferred_element_type=F32)
    scale = QK_DIM ** -0.5
    qkv_ref[:, :att_w] = (p[:, :att_w] * scale).astype(BF16)
    qkv_ref[:, att_w:] = p[:, att_w:3 * att_w].astype(BF16)
    xg_ref[...] = p[:, 3 * att_w:]


def _in_proj(xf, norm_w, w_in_bf, att_w):
    t, d = xf.shape
    n = w_in_bf.shape[1]
    tm = min(ROW_TILE, t)
    return pl.pallas_call(
        functools.partial(_inproj_kernel, att_w=att_w),
        grid=(t // tm,),
        in_specs=[pl.BlockSpec((tm, d), lambda i: (i, 0)),
                  pl.BlockSpec((1, d), lambda i: (0, 0)),
                  pl.BlockSpec((d, n), lambda i: (0, 0))],
        out_specs=[pl.BlockSpec((tm, 3 * att_w), lambda i: (i, 0)),
                   pl.BlockSpec((tm, n - 3 * att_w), lambda i: (i, 0))],
        out_shape=[jax.ShapeDtypeStruct((t, 3 * att_w), BF16),
                   jax.ShapeDtypeStruct((t, n - 3 * att_w), F32)],
        compiler_params=_cparams(("parallel",)),
        name="in_proj",
    )(xf, norm_w.reshape(1, d), w_in_bf)


def _attn_kernel(slope_ref, lam_ref, hw_ref, q_ref, k_ref, v_ref, o_ref,
                 k1a, k2a, vt, m1, a1, m2, a2, *, tq, tk, s_len, lambda_init):
    h = pl.program_id(1)
    qi = pl.program_id(2)
    grp = tk // tq
    slope = slope_ref[h]
    lane = lax.broadcasted_iota(jnp.int32, (tq, HEAD_DIM), 1)

    @pl.when(qi == 0)
    def _():
        ones_row = jnp.where(lax.broadcasted_iota(jnp.int32, (V_ROWS - HEAD_DIM, tq), 0) == 0, 1.0, 0.0)
        for c in range(s_len // tq):
            rows = slice(c * tq, (c + 1) * tq)
            kk = k_ref[0, rows, :].astype(F32)
            j = c * tq + lax.broadcasted_iota(jnp.int32, (tq, HEAD_DIM), 0)
            j_lo = (j & 255).astype(F32)
            j_hi = (j - (j & 255)).astype(F32)
            aug = jnp.where(lane == QK_DIM, slope * j_hi, jnp.where(lane == QK_DIM + 1, slope * j_lo, 0.0))
            k1a[rows, :] = jnp.where(lane < QK_DIM, kk, aug).astype(BF16)
            k2a[rows, :] = jnp.where(lane < QK_DIM, pltpu.roll(kk, QK_DIM, axis=1), aug).astype(BF16)
            vt[c, :HEAD_DIM, :] = v_ref[0, rows, :].astype(F32).T.astype(BF16)
            vt[c, HEAD_DIM:, :] = ones_row.astype(BF16)

    ones_cols = jnp.where((lane == QK_DIM) | (lane == QK_DIM + 1), 1.0, 0.0)
    q = q_ref[0].astype(F32)
    q1t = jnp.where(lane < QK_DIM, q, ones_cols).T.astype(BF16)
    q2t = jnp.where(lane < QK_DIM, pltpu.roll(q, QK_DIM, axis=1), ones_cols).T.astype(BF16)

    for m, a in ((m1, a1), (m2, a2)):
        m[...] = jnp.full(m.shape, NEG_BIG, F32)
        a[...] = jnp.zeros(a.shape, F32)

    def tile(c0, n_sub, masked):
        width = n_sub * tq
        rows = pl.ds(pl.multiple_of(c0 * tq, tq), width)
        if masked:
            keep = (lax.broadcasted_iota(jnp.int32, (width, tq), 0)
                    <= lax.broadcasted_iota(jnp.int32, (width, tq), 1))
        for qt, ka, m, a in ((q1t, k1a, m1, a1), (q2t, k2a, m2, a2)):
            s = jnp.dot(ka[rows, :], qt, preferred_element_type=F32)
            if masked:
                s = jnp.where(keep, s, NEG_BIG)
            m_prev = m[...]
            m_new = jnp.maximum(m_prev, jnp.max(s, axis=0, keepdims=True))
            alpha = jnp.exp(m_prev - m_new)
            p = jnp.exp(s - m_new).astype(BF16)
            pv = jnp.dot(vt[c0], p[:tq], preferred_element_type=F32)
            for g in range(1, n_sub):
                pv = pv + jnp.dot(vt[c0 + g], p[g * tq:(g + 1) * tq], preferred_element_type=F32)
            a[...] = alpha * a[...] + pv
            m[...] = m_new

    def full_body(j, c):
        tile(j * grp, grp, False)
        return c

    def rest_body(c0, c):
        tile(c0, 1, False)
        return c

    n_full = qi // grp
    lax.fori_loop(0, n_full, full_body, 0)
    if grp > 1:
        lax.fori_loop(n_full * grp, qi, rest_body, 0)
    tile(qi, 1, True)

    lam = (jnp.exp(jnp.sum(lam_ref[0:1, :] * lam_ref[1:2, :], axis=1, keepdims=True))
           - jnp.exp(jnp.sum(lam_ref[2:3, :] * lam_ref[3:4, :], axis=1, keepdims=True))
           + lambda_init)
    o1 = a1[:HEAD_DIM, :] / a1[HEAD_DIM:HEAD_DIM + 1, :]
    o2 = a2[:HEAD_DIM, :] / a2[HEAD_DIM:HEAD_DIM + 1, :]
    o = (o1 - lam * o2).T
    o = o * lax.rsqrt(jnp.mean(o * o, axis=-1, keepdims=True) + HEAD_NORM_EPS)
    o_ref[0] = (o * hw_ref[...] * (1.0 - lambda_init)).astype(o_ref.dtype)


def _diff_attention(qkv, lam_params, head_norm_w, lambda_init):
    b, s, w3 = qkv.shape
    nh = N_ATT_HEADS
    tq = min(ATT_Q_TILE, s)
    tk = min(ATT_KV_TILE, s)
    slopes = jnp.asarray(np.array([2.0 ** (-8.0 * (i + 1) / nh) for i in range(nh)], dtype=np.float32))
    return pl.pallas_call(
        functools.partial(_attn_kernel, tq=tq, tk=tk, s_len=s, lambda_init=lambda_init),
        grid=(b, nh, s // tq),
        in_specs=[pl.BlockSpec(memory_space=pltpu.SMEM),
                  pl.BlockSpec((4, QK_DIM), lambda bi, hi, qi: (0, 0)),
                  pl.BlockSpec((1, HEAD_DIM), lambda bi, hi, qi: (0, 0)),
                  pl.BlockSpec((1, tq, HEAD_DIM), lambda bi, hi, qi: (bi, qi, hi)),
                  pl.BlockSpec((1, s, HEAD_DIM), lambda bi, hi, qi: (bi, 0, nh + hi)),
                  pl.BlockSpec((1, s, HEAD_DIM), lambda bi, hi, qi: (bi, 0, 2 * nh + hi))],
        out_specs=pl.BlockSpec((1, tq, HEAD_DIM), lambda bi, hi, qi: (bi, qi, hi)),
        out_shape=jax.ShapeDtypeStruct((b, s, nh * HEAD_DIM), BF16),
        scratch_shapes=[pltpu.VMEM((s, HEAD_DIM), BF16), pltpu.VMEM((s, HEAD_DIM), BF16),
                        pltpu.VMEM((s // tq, V_ROWS, tq), BF16),
                        pltpu.VMEM((1, tq), F32), pltpu.VMEM((V_ROWS, tq), F32),
                        pltpu.VMEM((1, tq), F32), pltpu.VMEM((V_ROWS, tq), F32)],
        compiler_params=_cparams(("parallel", "parallel", "arbitrary")),
        name="diff_attn",
    )(slopes, lam_params, head_norm_w.reshape(1, HEAD_DIM), qkv, qkv, qkv)


def _gelu_tanh(x):
    return 0.5 * x * (1.0 + jnp.tanh(math.sqrt(2.0 / math.pi) * (x + 0.044715 * (x * x * x))))


def _rglru_kernel(xr_ref, gr_ref, cw_ref, cb_ref, w_ref, b_ref, lam_ref, o_ref,
                  xs, carry_h, a_s, u_s, *, ts, ch, c_w):
    si = pl.program_id(1)

    @pl.when(si == 0)
    def _():
        xs[0:8, :] = jnp.zeros((8, c_w), F32)
        carry_h[...] = jnp.zeros(carry_h.shape, F32)

    xs[8:, :] = xr_ref[0]
    neg_lam = -lam_ref[...]
    sp = jnp.maximum(neg_lam, 0.0) + jnp.log1p(jnp.exp(-jnp.abs(neg_lam)))
    cw = cw_ref[...]
    cb = cb_ref[...]
    bias = b_ref[...]
    r8 = lax.broadcasted_iota(jnp.int32, (ch, c_w), 0) & 7

    def chunk(c, carry):
        r0 = pl.multiple_of(c * ch, ch)
        win = xs[pl.ds(r0, ch + 8), :]
        xc = cw[3:4, :] * win[8:] + cb
        for k in (1, 2, 3):
            xc = xc + cw[3 - k:4 - k, :] * pltpu.roll(win, k, axis=0)[8:]
        z = jnp.dot(xc.astype(BF16), w_ref[...], preferred_element_type=F32) + bias
        r = jax.nn.sigmoid(z[:, :c_w])
        ig = jax.nn.sigmoid(z[:, c_w:])
        log_a = (-LRU_C) * r * sp
        a = jnp.exp(log_a)
        u = jnp.sqrt(jnp.tanh(-log_a) * (1.0 + a * a)) * ig * xc
        for k in (1, 2, 4):
            a_sh = pltpu.roll(a, k, axis=0)
            u_sh = pltpu.roll(u, k, axis=0)
            ok = r8 >= k
            u = jnp.where(ok, u + a * u_sh, u)
            a = jnp.where(ok, a * a_sh, a)
        a_s[pl.ds(r0, ch), :] = a
        u_s[pl.ds(r0, ch), :] = u
        return carry

    lax.fori_loop(0, ts // ch, chunk, 0)

    def grp(g, hprev):
        r0 = pl.multiple_of(g * 8, 8)
        hg = u_s[pl.ds(r0, 8), :] + a_s[pl.ds(r0, 8), :] * hprev
        u_s[pl.ds(r0, 8), :] = hg
        return hg[7:8, :]

    hlast = lax.fori_loop(0, ts // 8, grp, carry_h[0:1, :], unroll=8)
    carry_h[0:1, :] = hlast
    xs[0:8, :] = xs[ts:ts + 8, :]
    o_ref[0] = (u_s[...] * _gelu_tanh(gr_ref[0])).astype(o_ref.dtype)


def _rglru(xg, conv_w, conv_b, w_bd, b_cat, lru_lambda):
    b, s, w2 = xg.shape
    c_w = w2 // 2
    ts = min(LRU_TILE, s)
    ch = min(LRU_CHUNK, ts)
    return pl.pallas_call(
        functools.partial(_rglru_kernel, ts=ts, ch=ch, c_w=c_w),
        grid=(b, s // ts),
        in_specs=[pl.BlockSpec((1, ts, c_w), lambda bi, si: (bi, si, 0)),
                  pl.BlockSpec((1, ts, c_w), lambda bi, si: (bi, si, 1)),
                  pl.BlockSpec((CONV_WIDTH, c_w), lambda bi, si: (0, 0)),
                  pl.BlockSpec((1, c_w), lambda bi, si: (0, 0)),
                  pl.BlockSpec((c_w, 2 * c_w), lambda bi, si: (0, 0)),
                  pl.BlockSpec((1, 2 * c_w), lambda bi, si: (0, 0)),
                  pl.BlockSpec((1, c_w), lambda bi, si: (0, 0))],
        out_specs=pl.BlockSpec((1, ts, c_w), lambda bi, si: (bi, si, 0)),
        out_shape=jax.ShapeDtypeStruct((b, s, c_w), BF16),
        scratch_shapes=[pltpu.VMEM((ts + 8, c_w), F32), pltpu.VMEM((8, c_w), F32),
                        pltpu.VMEM((ts, c_w), F32), pltpu.VMEM((ts, c_w), F32)],
        compiler_params=_cparams(("parallel", "arbitrary")),
        name="rglru",
    )(xg, xg, conv_w, conv_b.reshape(1, c_w), w_bd, b_cat.reshape(1, 2 * c_w), lru_lambda.reshape(1, c_w))


def _outproj_kernel(att_ref, rnn_ref, x_ref, wo_ref, nw_ref, wr_ref, br_ref,
                    x1_ref, hn_ref, route_ref, n8_ref, *, att_w):
    y = jnp.dot(att_ref[...], wo_ref[:att_w, :], preferred_element_type=F32)
    y = y + jnp.dot(rnn_ref[...], wo_ref[att_w:, :], preferred_element_type=F32)
    x1 = x_ref[...] + y
    x1_ref[...] = x1
    hn = (x1 * lax.rsqrt(jnp.mean(x1 * x1, axis=-1, keepdims=True) + NORM_EPS) * nw_ref[...]).astype(BF16)
    hn_ref[...] = hn
    lg = jnp.dot(hn, wr_ref[...], preferred_element_type=F32) + br_ref[...]
    tm = lg.shape[0]

    col = lax.broadcasted_iota(jnp.int32, lg.shape, 1)
    colf = col.astype(F32)
    big = float(LANES)
    ninf = -jnp.inf
    is_g = col < N_GROUPS
    lgm = jnp.where(is_g, lg, ninf)
    mg = jnp.max(lgm, axis=1, keepdims=True)
    g_sel = jnp.min(jnp.where(lgm == mg, colf, big), axis=1, keepdims=True)
    pg = 1.0 / jnp.sum(jnp.where(is_g, jnp.exp(lgm - mg), 0.0), axis=1, keepdims=True)
    lo = N_GROUPS + EXPERTS_PER_GROUP * g_sel
    in_grp = (colf >= lo) & (colf < lo + EXPERTS_PER_GROUP)
    lem = jnp.where(in_grp, lg, ninf)
    v1 = jnp.max(lem, axis=1, keepdims=True)
    i1 = jnp.min(jnp.where(lem == v1, colf, big), axis=1, keepdims=True)
    lem2 = jnp.where(colf == i1, ninf, lem)
    v2 = jnp.max(lem2, axis=1, keepdims=True)
    i2 = jnp.min(jnp.where(lem2 == v2, colf, big), axis=1, keepdims=True)
    e2 = jnp.exp(v2 - v1)
    den = 1.0 + e2
    g1 = pg / den
    g2 = pg * e2 / den

    oh1 = jnp.where(colf == i1, 1.0, 0.0)
    oh2 = jnp.where(colf == i2, 1.0, 0.0)
    oh = oh1 + oh2
    earlier = (lax.broadcasted_iota(jnp.int32, (tm, tm), 1)
               < lax.broadcasted_iota(jnp.int32, (tm, tm), 0)).astype(BF16)
    pref = jnp.dot(earlier, oh.astype(BF16), preferred_element_type=F32)
    cnt = jnp.sum(oh, axis=0, keepdims=True)
    n8 = jnp.floor((cnt + (SUBLANES - 1)) * (1.0 / SUBLANES))
    before = (lax.broadcasted_iota(jnp.int32, (LANES, LANES), 0)
              < lax.broadcasted_iota(jnp.int32, (LANES, LANES), 1)).astype(BF16)
    loff8 = jnp.dot(jnp.broadcast_to(n8, (SUBLANES, LANES)).astype(BF16), before,
                    preferred_element_type=F32)[0:1]
    pos = SUBLANES * loff8 + pref
    lp1 = jnp.sum(oh1 * pos, axis=1, keepdims=True)
    lp2 = jnp.sum(oh2 * pos, axis=1, keepdims=True)
    route_ref[...] = jnp.where(col == 0, g1,
                     jnp.where(col == 1, g2,
                     jnp.where(col == 2, lp1,
                     jnp.where(col == 3, lp2, 0.0))))
    n8_ref[0] = n8


def _out_proj(att, rnn, xf, w_out_bf, norm_w, w_route_bf, b_route):
    t, d = xf.shape
    att_w = att.shape[1]
    tm = min(ROW_TILE, t)
    row = lambda i: (i, 0)
    fix = lambda i: (0, 0)
    return pl.pallas_call(
        functools.partial(_outproj_kernel, att_w=att_w),
        grid=(t // tm,),
        in_specs=[pl.BlockSpec((tm, att_w), row), pl.BlockSpec((tm, rnn.shape[1]), row),
                  pl.BlockSpec((tm, d), row), pl.BlockSpec(w_out_bf.shape, fix),
                  pl.BlockSpec((1, d), fix), pl.BlockSpec((d, LANES), fix), pl.BlockSpec((1, LANES), fix)],
        out_specs=[pl.BlockSpec((tm, d), row), pl.BlockSpec((tm, d), row), pl.BlockSpec((tm, LANES), row),
                   pl.BlockSpec((1, 1, LANES), lambda i: (i, 0, 0))],
        out_shape=[jax.ShapeDtypeStruct((t, d), F32), jax.ShapeDtypeStruct((t, d), BF16),
                   jax.ShapeDtypeStruct((t, LANES), F32),
                   jax.ShapeDtypeStruct((t // tm, 1, LANES), F32)],
        compiler_params=_cparams(("parallel",)),
        name="out_proj",
    )(att, rnn, xf, w_out_bf, norm_w.reshape(1, d), w_route_bf, b_route)


def _local_rows(tm):
    return -(-(TOP_K * tm + N_EXPERTS * (SUBLANES - 1)) // LANES) * LANES


def _segment_tables(n8_tiles, tm_moe, n_tiles):
    n8 = n8_tiles[:, 0, N_GROUPS:N_GROUPS + N_EXPERTS].astype(jnp.int32)
    c8 = n8 * SUBLANES
    loff = jnp.cumsum(c8, axis=1) - c8
    gtot = jnp.sum(c8, axis=0)
    gpad = (gtot + tm_moe - 1) // tm_moe * tm_moe
    gend = jnp.cumsum(gpad)
    gstart = gend - gpad
    gbase = gstart[None, :] + jnp.cumsum(c8, axis=0) - c8
    tile_e = jnp.minimum(jnp.searchsorted(gend, jnp.arange(n_tiles, dtype=jnp.int32) * tm_moe, side='right'),
                         N_EXPERTS - 1).astype(jnp.int32)
    n_used = (gend[-1] // tm_moe).astype(jnp.int32).reshape(1)
    tail_start = (gstart + gtot).astype(jnp.int32)
    tail_n8 = ((gpad - gtot) // SUBLANES).astype(jnp.int32)
    return (n8.reshape(-1), loff.reshape(-1).astype(jnp.int32), gbase.reshape(-1).astype(jnp.int32),
            tile_e, n_used, tail_start, tail_n8)


def _segment_copies(n8_ref, src_off_ref, dst_off_ref, src, dst, sem, tile, n_bits, wait):
    def per_expert(e, c):
        k = tile * N_EXPERTS + e
        n = n8_ref[k]
        s0 = src_off_ref[k]
        d0 = dst_off_ref[k]
        for b in range(n_bits):
            @pl.when(((n >> b) & 1) == 1)
            def _():
                off = (n & ((1 << b) - 1)) * SUBLANES
                cp = pltpu.make_async_copy(
                    src.at[pl.ds(pl.multiple_of(s0 + off, SUBLANES), SUBLANES << b), :],
                    dst.at[pl.ds(pl.multiple_of(d0 + off, SUBLANES), SUBLANES << b), :], sem)
                if wait:
                    cp.wait()
                else:
                    cp.start()
        return c
    lax.fori_loop(0, N_EXPERTS, per_expert, 0)


def _dispatch_kernel(n8_ref, loff_ref, gbase_ref, tstart_ref, tn8_ref, nu_ref, hn_ref, route_ref, xs_hbm,
                     stage, zbuf, sem, zsem, *, lcap, n_tt, seg_bits, tail_bits):
    i = pl.program_id(0)
    slot = i % 2
    tm = hn_ref.shape[0]
    tm_moe = zbuf.shape[0]
    n_tiles = xs_hbm.shape[0] // tm_moe

    def tail_copies(wait):
        def go(cp):
            if wait:
                cp.wait()
            else:
                cp.start()

        def per_expert(e, c):
            n = tn8_ref[e]
            d0 = tstart_ref[e]
            for b in range(tail_bits):
                @pl.when(((n >> b) & 1) == 1)
                def _():
                    off = (n & ((1 << b) - 1)) * SUBLANES
                    go(pltpu.make_async_copy(
                        zbuf.at[pl.ds(0, SUBLANES << b), :],
                        xs_hbm.at[pl.ds(pl.multiple_of(d0 + off, SUBLANES), SUBLANES << b), :], zsem.at[0]))
            return c
        lax.fori_loop(0, N_EXPERTS, per_expert, 0)

        def per_unused_tile(j, c):
            go(pltpu.make_async_copy(zbuf, xs_hbm.at[pl.ds(pl.multiple_of(j * tm_moe, tm_moe), tm_moe), :],
                                     zsem.at[0]))
            return c
        lax.fori_loop(nu_ref[0], n_tiles, per_unused_tile, 0)

    @pl.when(i == 0)
    def _():
        zbuf[...] = jnp.zeros(zbuf.shape, F32)
        tail_copies(False)

    @pl.when(i >= 2)
    def _():
        _segment_copies(n8_ref, loff_ref, gbase_ref, stage.at[slot], xs_hbm, sem.at[slot], i - 2, seg_bits, True)

    lp1 = route_ref[:, 2:3]
    lp2 = route_ref[:, 3:4]
    cpos = lax.broadcasted_iota(jnp.int32, (tm, lcap), 1).astype(F32)
    sel_t = jnp.where((cpos == lp1) | (cpos == lp2), 1.0, 0.0)
    stage[slot] = jnp.dot(sel_t.T.astype(BF16), hn_ref[...], preferred_element_type=F32)
    _segment_copies(n8_ref, loff_ref, gbase_ref, stage.at[slot], xs_hbm, sem.at[slot], i, seg_bits, False)

    @pl.when(i == n_tt - 1)
    def _():
        _segment_copies(n8_ref, loff_ref, gbase_ref, stage.at[slot], xs_hbm, sem.at[slot], i, seg_bits, True)
        if n_tt > 1:
            _segment_copies(n8_ref, loff_ref, gbase_ref, stage.at[1 - slot], xs_hbm, sem.at[1 - slot],
                            i - 1, seg_bits, True)
        tail_copies(True)


def _dispatch(hn, route, tables, n_rows, tm_moe):
    t, d = hn.shape
    tm = min(ROW_TILE, t)
    n_tt = t // tm
    lcap = _local_rows(tm)
    n8, loff, gbase, _, n_used, tail_start, tail_n8 = tables
    grid_spec = pltpu.PrefetchScalarGridSpec(
        num_scalar_prefetch=6,
        grid=(n_tt,),
        in_specs=[pl.BlockSpec((tm, d), lambda i, *_: (i, 0)),
                  pl.BlockSpec((tm, LANES), lambda i, *_: (i, 0))],
        out_specs=pl.BlockSpec(memory_space=pl.ANY),
        scratch_shapes=[pltpu.VMEM((2, lcap, d), F32), pltpu.VMEM((tm_moe, d), F32),
                        pltpu.SemaphoreType.DMA((2,)), pltpu.SemaphoreType.DMA((1,))],
    )
    return pl.pallas_call(
        functools.partial(_dispatch_kernel, lcap=lcap, n_tt=n_tt,
                          seg_bits=(tm // SUBLANES).bit_length(), tail_bits=(tm_moe // SUBLANES - 1).bit_length()),
        grid_spec=grid_spec,
        out_shape=jax.ShapeDtypeStruct((n_rows, d), F32),
        compiler_params=_cparams(("arbitrary",), has_side_effects=True),
        name="dispatch",
    )(n8, loff, gbase, tail_start, tail_n8, n_used, hn, route)


def _moe_kernel(te_ref, nu_ref, xs_ref, wg_ref, wu_ref, wd_ref, y_ref, wgb, wub, wdb):
    i = pl.program_id(0)

    @pl.when(i < nu_ref[0])
    def _():
        changed = jnp.logical_or(i == 0, te_ref[i] != te_ref[jnp.maximum(i - 1, 0)])

        @pl.when(changed)
        def _():
            wgb[...] = wg_ref[0].astype(BF16)
            wub[...] = wu_ref[0].astype(BF16)
            wdb[...] = wd_ref[0].astype(BF16)

        x = xs_ref[...].astype(BF16)
        g = jnp.dot(x, wgb[...], preferred_element_type=F32)
        u = jnp.dot(x, wub[...], preferred_element_type=F32)
        hdn = (g * jax.nn.sigmoid(g) * u).astype(BF16)
        y_ref[...] = jnp.dot(hdn, wdb[...], preferred_element_type=F32)

    @pl.when(i >= nu_ref[0])
    def _():
        y_ref[...] = jnp.zeros(y_ref.shape, F32)


def _moe(xs, tile_e, n_used, w_g, w_u, w_d, tm):
    n_rows, d = xs.shape
    n_tiles = n_rows // tm
    ff = w_g.shape[2]
    row_blk = lambda i, te, nu: (jnp.minimum(i, nu[0] - 1), 0)
    w_blk = lambda i, te, nu: (te[jnp.minimum(i, nu[0] - 1)], 0, 0)
    grid_spec = pltpu.PrefetchScalarGridSpec(
        num_scalar_prefetch=2,
        grid=(n_tiles,),
        in_specs=[pl.BlockSpec((tm, d), row_blk),
                  pl.BlockSpec((1, d, ff), w_blk), pl.BlockSpec((1, d, ff), w_blk), pl.BlockSpec((1, ff, d), w_blk)],
        out_specs=pl.BlockSpec((tm, d), lambda i, te, nu: (i, 0)),
        scratch_shapes=[pltpu.VMEM((d, ff), BF16), pltpu.VMEM((d, ff), BF16), pltpu.VMEM((ff, d), BF16)],
    )
    return pl.pallas_call(
        _moe_kernel,
        grid_spec=grid_spec,
        out_shape=jax.ShapeDtypeStruct((n_rows, d), F32),
        compiler_params=_cparams(("arbitrary",)),
        name="moe",
    )(tile_e, n_used, xs, w_g, w_u, w_d)


def _combine_kernel(n8_ref, loff_ref, gbase_ref, x1_ref, route_ref, nw_ref, y_hbm, o_ref,
                    ybuf, sem, *, lcap, n_tt, seg_bits):
    i = pl.program_id(0)
    slot = i % 2
    tm = x1_ref.shape[0]

    def fetch(tile, sl, wait):
        _segment_copies(n8_ref, gbase_ref, loff_ref, y_hbm, ybuf.at[sl], sem.at[sl], tile, seg_bits, wait)

    @pl.when(i == 0)
    def _():
        ybuf[...] = jnp.zeros(ybuf.shape, F32)
        fetch(0, 0, False)

    @pl.when(i + 1 < n_tt)
    def _():
        fetch(i + 1, 1 - slot, False)

    fetch(i, slot, True)
    g1 = route_ref[:, 0:1]
    g2 = route_ref[:, 1:2]
    lp1 = route_ref[:, 2:3]
    lp2 = route_ref[:, 3:4]
    cpos = lax.broadcasted_iota(jnp.int32, (tm, lcap), 1).astype(F32)
    gsel = (jnp.where(cpos == lp1, g1, 0.0) + jnp.where(cpos == lp2, g2, 0.0)).astype(BF16)
    moe = jnp.dot(gsel, ybuf[slot].astype(BF16), preferred_element_type=F32)
    x = x1_ref[...] + moe
    o_ref[...] = x * lax.rsqrt(jnp.mean(x * x, axis=-1, keepdims=True) + NORM_EPS) * nw_ref[...]


def _combine(x1, y, route, norm_w, tables):
    t, d = x1.shape
    tm = min(ROW_TILE, t)
    n_tt = t // tm
    lcap = _local_rows(tm)
    n8, loff, gbase = tables[:3]
    grid_spec = pltpu.PrefetchScalarGridSpec(
        num_scalar_prefetch=3,
        grid=(n_tt,),
        in_specs=[pl.BlockSpec((tm, d), lambda i, *_: (i, 0)),
                  pl.BlockSpec((tm, LANES), lambda i, *_: (i, 0)),
                  pl.BlockSpec((1, d), lambda i, *_: (0, 0)),
                  pl.BlockSpec(memory_space=pl.ANY)],
        out_specs=pl.BlockSpec((tm, d), lambda i, *_: (i, 0)),
        scratch_shapes=[pltpu.VMEM((2, lcap, d), F32), pltpu.SemaphoreType.DMA((2,))],
    )
    return pl.pallas_call(
        functools.partial(_combine_kernel, lcap=lcap, n_tt=n_tt, seg_bits=(tm // SUBLANES).bit_length()),
        grid_spec=grid_spec,
        out_shape=jax.ShapeDtypeStruct((t, d), F32),
        compiler_params=_cparams(("arbitrary",)),
        name="combine",
    )(n8, loff, gbase, x1, route, norm_w.reshape(1, d), y)


def _block_diag(w):
    n, bi, bj = w.shape
    eye = jnp.eye(n, dtype=w.dtype)
    return jnp.einsum('nij,nm->nimj', w, eye).reshape(n * bi, n * bj)


def kernel(x, mix_norm_w, w_in, lambda_q1, lambda_k1, lambda_q2, lambda_k2, head_norm_w, conv_w, conv_b, w_rgate, b_rgate, w_igate, b_igate, lru_lambda, w_out, ffn_norm_w, w_router_group, b_router_group, w_router_expert, b_router_expert, w_exp_gate, w_exp_up, w_exp_down, final_norm_w):
    b, s, d = x.shape
    t = b * s
    assert w_in.shape[0] == 1, "single-layer stack only"
    att_w = N_ATT_HEADS * HEAD_DIM
    tm_moe = MOE_TILE
    xf = x.reshape(t, d)
    for l in range(1):
        lambda_init = 0.8 - 0.6 * math.exp(-0.3 * l)
        qkv, xg = _in_proj(xf, mix_norm_w[l], w_in[l].astype(BF16), att_w)
        lam_params = jnp.stack([lambda_q1[l], lambda_k1[l], lambda_q2[l], lambda_k2[l]]).astype(F32)
        att = _diff_attention(qkv.reshape(b, s, 3 * att_w), lam_params, head_norm_w[l], lambda_init)
        w_bd = jnp.concatenate([_block_diag(w_rgate[l]), _block_diag(w_igate[l])], axis=1).astype(BF16)
        b_cat = jnp.concatenate([b_rgate[l], b_igate[l]])
        rnn = _rglru(xg.reshape(b, s, xg.shape[1]), conv_w[l], conv_b[l], w_bd, b_cat, lru_lambda[l])
        w_route = jnp.concatenate([w_router_group[l], w_router_expert[l]], axis=1)
        w_route = jnp.pad(w_route, ((0, 0), (0, LANES - w_route.shape[1]))).astype(BF16)
        b_route = jnp.concatenate([b_router_group[l], b_router_expert[l]])
        b_route = jnp.pad(b_route, (0, LANES - b_route.shape[0])).reshape(1, LANES).astype(F32)
        x1, hn, route, n8_tiles = _out_proj(att.reshape(t, att_w), rnn.reshape(t, -1), xf, w_out[l].astype(BF16),
                                            ffn_norm_w[l], w_route, b_route)
        n_tt = n8_tiles.shape[0]
        max_rows = TOP_K * t + n_tt * N_EXPERTS * (SUBLANES - 1) + N_EXPERTS * (tm_moe - 1)
        n_tiles = -(-max_rows // tm_moe)
        tables = _segment_tables(n8_tiles, tm_moe, n_tiles)
        xs = _dispatch(hn, route, tables, n_tiles * tm_moe, tm_moe)
        y = _moe(xs, tables[3], tables[4], w_exp_gate[l], w_exp_up[l], w_exp_down[l], tm_moe)
        out = _combine(x1, y, route, final_norm_w, tables)
    return out.reshape(b, s, d)
```

```python
import functools
import math

import numpy as np
import jax
import jax.numpy as jnp
from jax import lax
from jax.experimental import pallas as pl
from jax.experimental.pallas import tpu as pltpu

F32 = jnp.float32
BF16 = jnp.bfloat16

N_ATT_HEADS = 4
HEAD_DIM = 128
QK_DIM = 64
N_RNN_BLOCKS = 8
CONV_WIDTH = 4
LRU_C = 8.0
N_GROUPS = 4
EXPERTS_PER_GROUP = 8
N_EXPERTS = N_GROUPS * EXPERTS_PER_GROUP
TOP_K = 2
NORM_EPS = 1e-6
HEAD_NORM_EPS = 1e-5
LANES = 128
SUBLANES = 8
NEG_BIG = -1e30

ROW_TILE = 512
ATT_Q_TILE = 512
ATT_KV_TILE = 1024
V_ROWS = HEAD_DIM + 16
LRU_TILE = 512
LRU_CHUNK = 128
MOE_TILE = 256
VMEM_LIMIT = 48 * 1024 * 1024


def _cparams(sem, vmem=VMEM_LIMIT, **kw):
    return pltpu.CompilerParams(dimension_semantics=sem, vmem_limit_bytes=vmem, **kw)


def _inproj_kernel(x_ref, nw_ref, w_ref, qkv_ref, xg_ref, *, att_w):
    x = x_ref[...]
    ms = jnp.mean(x * x, axis=-1, keepdims=True)
    hn = (x * lax.rsqrt(ms + NORM_EPS) * nw_ref[...]).astype(BF16)
    p = jnp.dot(hn, w_ref[...], preferred_element_type=F32)
    scale = QK_DIM ** -0.5
    qkv_ref[:, :att_w] = (p[:, :att_w] * scale).astype(BF16)
    qkv_ref[:, att_w:] = p[:, att_w:3 * att_w].astype(BF16)
    xg_ref[...] = p[:, 3 * att_w:]


def _in_proj(xf, norm_w, w_in_bf, att_w):
    t, d = xf.shape
    n = w_in_bf.shape[1]
    tm = min(ROW_TILE, t)
    return pl.pallas_call(
        functools.partial(_inproj_kernel, att_w=att_w),
        grid=(t // tm,),
        in_specs=[pl.BlockSpec((tm, d), lambda i: (i, 0)),
                  pl.BlockSpec((1, d), lambda i: (0, 0)),
                  pl.BlockSpec((d, n), lambda i: (0, 0))],
        out_specs=[pl.BlockSpec((tm, 3 * att_w), lambda i: (i, 0)),
                   pl.BlockSpec((tm, n - 3 * att_w), lambda i: (i, 0))],
        out_shape=[jax.ShapeDtypeStruct((t, 3 * att_w), BF16),
                   jax.ShapeDtypeStruct((t, n - 3 * att_w), F32)],
        compiler_params=_cparams(("parallel",)),
        name="in_proj",
    )(xf, norm_w.reshape(1, d), w_in_bf)


def _attn_kernel(slope_ref, lam_ref, hw_ref, q_ref, k_ref, v_ref, o_ref,
                 k1a, k2a, vt, m1, a1, m2, a2, *, tq, tk, s_len, lambda_init):
    h = pl.program_id(1)
    qi = pl.program_id(2)
    grp = tk // tq
    slope = slope_ref[h]
    lane = lax.broadcasted_iota(jnp.int32, (tq, HEAD_DIM), 1)

    @pl.when(qi == 0)
    def _():
        ones_row = jnp.where(lax.broadcasted_iota(jnp.int32, (V_ROWS - HEAD_DIM, tq), 0) == 0, 1.0, 0.0)
        for c in range(s_len // tq):
            rows = slice(c * tq, (c + 1) * tq)
            kk = k_ref[0, rows, :].astype(F32)
            j = c * tq + lax.broadcasted_iota(jnp.int32, (tq, HEAD_DIM), 0)
            j_lo = (j & 255).astype(F32)
            j_hi = (j - (j & 255)).astype(F32)
            aug = jnp.where(lane == QK_DIM, slope * j_hi, jnp.where(lane == QK_DIM + 1, slope * j_lo, 0.0))
            k1a[rows, :] = jnp.where(lane < QK_DIM, kk, aug).astype(BF16)
            k2a[rows, :] = jnp.where(lane < QK_DIM, pltpu.roll(kk, QK_DIM, axis=1), aug).astype(BF16)
            vt[c, :HEAD_DIM, :] = v_ref[0, rows, :].astype(F32).T.astype(BF16)
            vt[c, HEAD_DIM:, :] = ones_row.astype(BF16)

    ones_cols = jnp.where((lane == QK_DIM) | (lane == QK_DIM + 1), 1.0, 0.0)
    q = q_ref[0].astype(F32)
    q1t = jnp.where(lane < QK_DIM, q, ones_cols).T.astype(BF16)
    q2t = jnp.where(lane < QK_DIM, pltpu.roll(q, QK_DIM, axis=1), ones_cols).T.astype(BF16)

    for m, a in ((m1, a1), (m2, a2)):
        m[...] = jnp.full(m.shape, NEG_BIG, F32)
        a[...] = jnp.zeros(a.shape, F32)

    def tile(c0, n_sub, masked):
        width = n_sub * tq
        rows = pl.ds(pl.multiple_of(c0 * tq, tq), width)
        if masked:
            keep = (lax.broadcasted_iota(jnp.int32, (width, tq), 0)
                    <= lax.broadcasted_iota(jnp.int32, (width, tq), 1))
        for qt, ka, m, a in ((q1t, k1a, m1, a1), (q2t, k2a, m2, a2)):
            s = jnp.dot(ka[rows, :], qt, preferred_element_type=F32)
            if masked:
                s = jnp.where(keep, s, NEG_BIG)
            m_prev = m[...]
            m_new = jnp.maximum(m_prev, jnp.max(s, axis=0, keepdims=True))
            alpha = jnp.exp(m_prev - m_new)
            p = jnp.exp(s - m_new).astype(BF16)
            pv = jnp.dot(vt[c0], p[:tq], preferred_element_type=F32)
            for g in range(1, n_sub):
                pv = pv + jnp.dot(vt[c0 + g], p[g * tq:(g + 1) * tq], preferred_element_type=F32)
            a[...] = alpha * a[...] + pv
            m[...] = m_new

    def full_body(j, c):
        tile(j * grp, grp, False)
        return c

    def rest_body(c0, c):
        tile(c0, 1, False)
        return c

    n_full = qi // grp
    lax.fori_loop(0, n_full, full_body, 0)
    if grp > 1:
        lax.fori_loop(n_full * grp, qi, rest_body, 0)
    tile(qi, 1, True)

    lam = (jnp.exp(jnp.sum(lam_ref[0:1, :] * lam_ref[1:2, :], axis=1, keepdims=True))
           - jnp.exp(jnp.sum(lam_ref[2:3, :] * lam_ref[3:4, :], axis=1, keepdims=True))
           + lambda_init)
    o1 = a1[:HEAD_DIM, :] / a1[HEAD_DIM:HEAD_DIM + 1, :]
    o2 = a2[:HEAD_DIM, :] / a2[HEAD_DIM:HEAD_DIM + 1, :]
    o = (o1 - lam * o2).T
    o = o * lax.rsqrt(jnp.mean(o * o, axis=-1, keepdims=True) + HEAD_NORM_EPS)
    o_ref[0] = (o * hw_ref[...] * (1.0 - lambda_init)).astype(o_ref.dtype)


def _diff_attention(qkv, lam_params, head_norm_w, lambda_init):
    b, s, w3 = qkv.shape
    nh = N_ATT_HEADS
    tq = min(ATT_Q_TILE, s)
    tk = min(ATT_KV_TILE, s)
    slopes = jnp.asarray(np.array([2.0 ** (-8.0 * (i + 1) / nh) for i in range(nh)], dtype=np.float32))
    return pl.pallas_call(
        functools.partial(_attn_kernel, tq=tq, tk=tk, s_len=s, lambda_init=lambda_init),
        grid=(b, nh, s // tq),
        in_specs=[pl.BlockSpec(memory_space=pltpu.SMEM),
                  pl.BlockSpec((4, QK_DIM), lambda bi, hi, qi: (0, 0)),
                  pl.BlockSpec((1, HEAD_DIM), lambda bi, hi, qi: (0, 0)),
                  pl.BlockSpec((1, tq, HEAD_DIM), lambda bi, hi, qi: (bi, qi, hi)),
                  pl.BlockSpec((1, s, HEAD_DIM), lambda bi, hi, qi: (bi, 0, nh + hi)),
                  pl.BlockSpec((1, s, HEAD_DIM), lambda bi, hi, qi: (bi, 0, 2 * nh + hi))],
        out_specs=pl.BlockSpec((1, tq, HEAD_DIM), lambda bi, hi, qi: (bi, qi, hi)),
        out_shape=jax.ShapeDtypeStruct((b, s, nh * HEAD_DIM), BF16),
        scratch_shapes=[pltpu.VMEM((s, HEAD_DIM), BF16), pltpu.VMEM((s, HEAD_DIM), BF16),
                        pltpu.VMEM((s // tq, V_ROWS, tq), BF16),
                        pltpu.VMEM((1, tq), F32), pltpu.VMEM((V_ROWS, tq), F32),
                        pltpu.VMEM((1, tq), F32), pltpu.VMEM((V_ROWS, tq), F32)],
        compiler_params=_cparams(("parallel", "parallel", "arbitrary")),
        name="diff_attn",
    )(slopes, lam_params, head_norm_w.reshape(1, HEAD_DIM), qkv, qkv, qkv)


def _gelu_tanh(x):
    return 0.5 * x * (1.0 + jnp.tanh(math.sqrt(2.0 / math.pi) * (x + 0.044715 * (x * x * x))))


def _rglru_kernel(xr_ref, gr_ref, cw_ref, cb_ref, w_ref, b_ref, lam_ref, o_ref,
                  xs, carry_h, a_s, u_s, *, ts, ch, c_w):
    si = pl.program_id(1)

    @pl.when(si == 0)
    def _():
        xs[0:8, :] = jnp.zeros((8, c_w), F32)
        carry_h[...] = jnp.zeros(carry_h.shape, F32)

    xs[8:, :] = xr_ref[0]
    neg_lam = -lam_ref[...]
    sp = jnp.maximum(neg_lam, 0.0) + jnp.log1p(jnp.exp(-jnp.abs(neg_lam)))
    cw = cw_ref[...]
    cb = cb_ref[...]
    bias = b_ref[...]
    r8 = lax.broadcasted_iota(jnp.int32, (ch // SUBLANES, SUBLANES, c_w), 1)

    def chunk(c, carry):
        r0 = pl.multiple_of(c * ch, ch)
        win = xs[pl.ds(r0, ch + 8), :]
        xc = cw[3:4, :] * win[8:] + cb
        for k in (1, 2, 3):
            xc = xc + cw[3 - k:4 - k, :] * pltpu.roll(win, k, axis=0)[8:]
        z = jnp.dot(xc.astype(BF16), w_ref[...], preferred_element_type=F32) + bias
        r = jax.nn.sigmoid(z[:, :c_w])
        ig = jax.nn.sigmoid(z[:, c_w:])
        log_a = (-LRU_C) * r * sp
        a = jnp.exp(log_a)
        w = jnp.tanh(-log_a) * (1.0 + a * a)
        u = jnp.where(w > 0.0, w * lax.rsqrt(w), 0.0) * ig * xc
        a = a.reshape(ch // SUBLANES, SUBLANES, c_w)
        u = u.reshape(ch // SUBLANES, SUBLANES, c_w)
        for k in (1, 2, 4):
            a_sh = pltpu.roll(a, k, axis=1)
            u_sh = pltpu.roll(u, k, axis=1)
            ok = r8 >= k
            u = jnp.where(ok, u + a * u_sh, u)
            a = jnp.where(ok, a * a_sh, a)
        a_s[pl.ds(r0, ch), :] = a.reshape(ch, c_w)
        u_s[pl.ds(r0, ch), :] = u.reshape(ch, c_w)
        return carry

    lax.fori_loop(0, ts // ch, chunk, 0)

    def grp(g, hprev):
        r0 = pl.multiple_of(g * 8, 8)
        hg = u_s[pl.ds(r0, 8), :] + a_s[pl.ds(r0, 8), :] * hprev
        u_s[pl.ds(r0, 8), :] = hg
        return hg[7:8, :]

    hlast = lax.fori_loop(0, ts // 8, grp, carry_h[0:1, :], unroll=8)
    carry_h[0:1, :] = hlast
    xs[0:8, :] = xs[ts:ts + 8, :]
    o_ref[0] = (u_s[...] * _gelu_tanh(gr_ref[0])).astype(o_ref.dtype)


def _rglru(xg, conv_w, conv_b, w_bd, b_cat, lru_lambda):
    b, s, w2 = xg.shape
    c_w = w2 // 2
    ts = min(LRU_TILE, s)
    ch = min(LRU_CHUNK, ts)
    return pl.pallas_call(
        functools.partial(_rglru_kernel, ts=ts, ch=ch, c_w=c_w),
        grid=(b, s // ts),
        in_specs=[pl.BlockSpec((1, ts, c_w), lambda bi, si: (bi, si, 0)),
                  pl.BlockSpec((1, ts, c_w), lambda bi, si: (bi, si, 1)),
                  pl.BlockSpec((CONV_WIDTH, c_w), lambda bi, si: (0, 0)),
                  pl.BlockSpec((1, c_w), lambda bi, si: (0, 0)),
                  pl.BlockSpec((c_w, 2 * c_w), lambda bi, si: (0, 0)),
                  pl.BlockSpec((1, 2 * c_w), lambda bi, si: (0, 0)),
                  pl.BlockSpec((1, c_w), lambda bi, si: (0, 0))],
        out_specs=pl.BlockSpec((1, ts, c_w), lambda bi, si: (bi, si, 0)),
        out_shape=jax.ShapeDtypeStruct((b, s, c_w), BF16),
        scratch_shapes=[pltpu.VMEM((ts + 8, c_w), F32), pltpu.VMEM((8, c_w), F32),
                        pltpu.VMEM((ts, c_w), F32), pltpu.VMEM((ts, c_w), F32)],
        compiler_params=_cparams(("parallel", "arbitrary")),
        name="rglru",
    )(xg, xg, conv_w, conv_b.reshape(1, c_w), w_bd, b_cat.reshape(1, 2 * c_w), lru_lambda.reshape(1, c_w))


def _outproj_kernel(att_ref, rnn_ref, x_ref, wo_ref, nw_ref, wr_ref, br_ref,
                    x1_ref, hn_ref, route_ref, n8_ref, *, att_w):
    y = jnp.dot(att_ref[...], wo_ref[:att_w, :], preferred_element_type=F32)
    y = y + jnp.dot(rnn_ref[...], wo_ref[att_w:, :], preferred_element_type=F32)
    x1 = x_ref[...] + y
    x1_ref[...] = x1
    hn = (x1 * lax.rsqrt(jnp.mean(x1 * x1, axis=-1, keepdims=True) + NORM_EPS) * nw_ref[...]).astype(BF16)
    hn_ref[...] = hn
    lg = jnp.dot(hn, wr_ref[...], preferred_element_type=F32) + br_ref[...]
    tm = lg.shape[0]

    col = lax.broadcasted_iota(jnp.int32, lg.shape, 1)
    colf = col.astype(F32)
    big = float(LANES)
    ninf = -jnp.inf
    is_g = col < N_GROUPS
    lgm = jnp.where(is_g, lg, ninf)
    mg = jnp.max(lgm, axis=1, keepdims=True)
    g_sel = jnp.min(jnp.where(lgm == mg, colf, big), axis=1, keepdims=True)
    pg = 1.0 / jnp.sum(jnp.where(is_g, jnp.exp(lgm - mg), 0.0), axis=1, keepdims=True)
    lo = N_GROUPS + EXPERTS_PER_GROUP * g_sel
    in_grp = (colf >= lo) & (colf < lo + EXPERTS_PER_GROUP)
    lem = jnp.where(in_grp, lg, ninf)
    v1 = jnp.max(lem, axis=1, keepdims=True)
    i1 = jnp.min(jnp.where(lem == v1, colf, big), axis=1, keepdims=True)
    lem2 = jnp.where(colf == i1, ninf, lem)
    v2 = jnp.max(lem2, axis=1, keepdims=True)
    i2 = jnp.min(jnp.where(lem2 == v2, colf, big), axis=1, keepdims=True)
    e2 = jnp.exp(v2 - v1)
    den = 1.0 + e2
    g1 = pg / den
    g2 = pg * e2 / den

    oh1 = jnp.where(colf == i1, 1.0, 0.0)
    oh2 = jnp.where(colf == i2, 1.0, 0.0)
    oh = oh1 + oh2
    earlier = (lax.broadcasted_iota(jnp.int32, (tm, tm), 1)
               < lax.broadcasted_iota(jnp.int32, (tm, tm), 0)).astype(BF16)
    pref = jnp.dot(earlier, oh.astype(BF16), preferred_element_type=F32)
    cnt = jnp.sum(oh, axis=0, keepdims=True)
    n8 = jnp.floor((cnt + (SUBLANES - 1)) * (1.0 / SUBLANES))
    before = (lax.broadcasted_iota(jnp.int32, (LANES, LANES), 0)
              < lax.broadcasted_iota(jnp.int32, (LANES, LANES), 1)).astype(BF16)
    loff8 = jnp.dot(jnp.broadcast_to(n8, (SUBLANES, LANES)).astype(BF16), before,
                    preferred_element_type=F32)[0:1]
    pos = SUBLANES * loff8 + pref
    lp1 = jnp.sum(oh1 * pos, axis=1, keepdims=True)
    lp2 = jnp.sum(oh2 * pos, axis=1, keepdims=True)
    route_ref[...] = jnp.where(col == 0, g1,
                     jnp.where(col == 1, g2,
                     jnp.where(col == 2, lp1,
                     jnp.where(col == 3, lp2, 0.0))))
    n8_ref[0] = n8


def _out_proj(att, rnn, xf, w_out_bf, norm_w, w_route_bf, b_route):
    t, d = xf.shape
    att_w = att.shape[1]
    tm = min(ROW_TILE, t)
    row = lambda i: (i, 0)
    fix = lambda i: (0, 0)
    return pl.pallas_call(
        functools.partial(_outproj_kernel, att_w=att_w),
        grid=(t // tm,),
        in_specs=[pl.BlockSpec((tm, att_w), row), pl.BlockSpec((tm, rnn.shape[1]), row),
                  pl.BlockSpec((tm, d), row), pl.BlockSpec(w_out_bf.shape, fix),
                  pl.BlockSpec((1, d), fix), pl.BlockSpec((d, LANES), fix), pl.BlockSpec((1, LANES), fix)],
        out_specs=[pl.BlockSpec((tm, d), row), pl.BlockSpec((tm, d), row), pl.BlockSpec((tm, LANES), row),
                   pl.BlockSpec((1, 1, LANES), lambda i: (i, 0, 0))],
        out_shape=[jax.ShapeDtypeStruct((t, d), F32), jax.ShapeDtypeStruct((t, d), BF16),
                   jax.ShapeDtypeStruct((t, LANES), F32),
                   jax.ShapeDtypeStruct((t // tm, 1, LANES), F32)],
        compiler_params=_cparams(("parallel",)),
        name="out_proj",
    )(att, rnn, xf, w_out_bf, norm_w.reshape(1, d), w_route_bf, b_route)


def _local_rows(tm):
    return -(-(TOP_K * tm + N_EXPERTS * (SUBLANES - 1)) // LANES) * LANES


def _segment_tables(n8_tiles, tm_moe, n_tiles):
    n8 = n8_tiles[:, 0, N_GROUPS:N_GROUPS + N_EXPERTS].astype(jnp.int32)
    c8 = n8 * SUBLANES
    loff = jnp.cumsum(c8, axis=1) - c8
    gtot = jnp.sum(c8, axis=0)
    gpad = (gtot + tm_moe - 1) // tm_moe * tm_moe
    gend = jnp.cumsum(gpad)
    gstart = gend - gpad
    gbase = gstart[None, :] + jnp.cumsum(c8, axis=0) - c8
    tile_row0 = jnp.arange(n_tiles, dtype=jnp.int32) * tm_moe
    tile_e = jnp.minimum(jnp.sum((gend[None, :] <= tile_row0[:, None]).astype(jnp.int32), axis=1),
                         N_EXPERTS - 1).astype(jnp.int32)
    n_used = (gend[-1] // tm_moe).astype(jnp.int32).reshape(1)
    tail_start = (gstart + gtot).astype(jnp.int32)
    tail_n8 = ((gpad - gtot) // SUBLANES).astype(jnp.int32)
    after = gend[tile_e] // tm_moe
    next_e = jnp.where(after < n_used[0], tile_e[jnp.minimum(after, n_tiles - 1)], -1).astype(jnp.int32)
    first = jnp.concatenate([jnp.ones((1,), jnp.int32), (tile_e[1:] != tile_e[:-1]).astype(jnp.int32)])
    w_slot = ((jnp.cumsum(first) - 1) % 2).astype(jnp.int32)
    return (n8.reshape(-1), loff.reshape(-1).astype(jnp.int32), gbase.reshape(-1).astype(jnp.int32),
            tile_e, n_used, tail_start, tail_n8, next_e, w_slot)


def _segment_copies(n8_ref, src_off_ref, dst_off_ref, src, dst, sem, tile, n_bits, wait):
    def per_expert(e, c):
        k = tile * N_EXPERTS + e
        n = n8_ref[k]
        s0 = src_off_ref[k]
        d0 = dst_off_ref[k]
        for b in range(n_bits):
            @pl.when(((n >> b) & 1) == 1)
            def _():
                off = (n & ((1 << b) - 1)) * SUBLANES
                cp = pltpu.make_async_copy(
                    src.at[pl.ds(pl.multiple_of(s0 + off, SUBLANES), SUBLANES << b), :],
                    dst.at[pl.ds(pl.multiple_of(d0 + off, SUBLANES), SUBLANES << b), :], sem)
                if wait:
                    cp.wait()
                else:
                    cp.start()
        return c
    lax.fori_loop(0, N_EXPERTS, per_expert, 0)


def _dispatch_kernel(n8_ref, loff_ref, gbase_ref, tstart_ref, tn8_ref, nu_ref, hn_ref, route_ref, xs_hbm,
                     stage, zbuf, sem, zsem, *, lcap, n_tt, seg_bits, tail_bits):
    i = pl.program_id(0)
    slot = i % 2
    tm = hn_ref.shape[0]
    tm_moe = zbuf.shape[0]
    n_tiles = xs_hbm.shape[0] // tm_moe

    def tail_copies(wait):
        def go(cp):
            if wait:
                cp.wait()
            else:
                cp.start()

        def per_expert(e, c):
            n = tn8_ref[e]
            d0 = tstart_ref[e]
            for b in range(tail_bits):
                @pl.when(((n >> b) & 1) == 1)
                def _():
                    off = (n & ((1 << b) - 1)) * SUBLANES
                    go(pltpu.make_async_copy(
                        zbuf.at[pl.ds(0, SUBLANES << b), :],
                        xs_hbm.at[pl.ds(pl.multiple_of(d0 + off, SUBLANES), SUBLANES << b), :], zsem.at[0]))
            return c
        lax.fori_loop(0, N_EXPERTS, per_expert, 0)

        def per_unused_tile(j, c):
            go(pltpu.make_async_copy(zbuf, xs_hbm.at[pl.ds(pl.multiple_of(j * tm_moe, tm_moe), tm_moe), :],
                                     zsem.at[0]))
            return c
        lax.fori_loop(nu_ref[0], n_tiles, per_unused_tile, 0)

    @pl.when(i == 0)
    def _():
        zbuf[...] = jnp.zeros(zbuf.shape, F32)
        tail_copies(False)

    @pl.when(i >= 2)
    def _():
        _segment_copies(n8_ref, loff_ref, gbase_ref, stage.at[slot], xs_hbm, sem.at[slot], i - 2, seg_bits, True)

    lp1 = route_ref[:, 2:3]
    lp2 = route_ref[:, 3:4]
    cpos = lax.broadcasted_iota(jnp.int32, (tm, lcap), 1).astype(F32)
    sel_t = jnp.where((cpos == lp1) | (cpos == lp2), 1.0, 0.0)
    stage[slot] = jnp.dot(sel_t.T.astype(BF16), hn_ref[...], preferred_element_type=F32)
    _segment_copies(n8_ref, loff_ref, gbase_ref, stage.at[slot], xs_hbm, sem.at[slot], i, seg_bits, False)

    @pl.when(i == n_tt - 1)
    def _():
        _segment_copies(n8_ref, loff_ref, gbase_ref, stage.at[slot], xs_hbm, sem.at[slot], i, seg_bits, True)
        if n_tt > 1:
            _segment_copies(n8_ref, loff_ref, gbase_ref, stage.at[1 - slot], xs_hbm, sem.at[1 - slot],
                            i - 1, seg_bits, True)
        tail_copies(True)


def _dispatch(hn, route, tables, n_rows, tm_moe):
    t, d = hn.shape
    tm = min(ROW_TILE, t)
    n_tt = t // tm
    lcap = _local_rows(tm)
    n8, loff, gbase, _, n_used, tail_start, tail_n8 = tables[:7]
    grid_spec = pltpu.PrefetchScalarGridSpec(
        num_scalar_prefetch=6,
        grid=(n_tt,),
        in_specs=[pl.BlockSpec((tm, d), lambda i, *_: (i, 0)),
                  pl.BlockSpec((tm, LANES), lambda i, *_: (i, 0))],
        out_specs=pl.BlockSpec(memory_space=pl.ANY),
        scratch_shapes=[pltpu.VMEM((2, lcap, d), F32), pltpu.VMEM((tm_moe, d), F32),
                        pltpu.SemaphoreType.DMA((2,)), pltpu.SemaphoreType.DMA((1,))],
    )
    return pl.pallas_call(
        functools.partial(_dispatch_kernel, lcap=lcap, n_tt=n_tt,
                          seg_bits=(tm // SUBLANES).bit_length(), tail_bits=(tm_moe // SUBLANES - 1).bit_length()),
        grid_spec=grid_spec,
        out_shape=jax.ShapeDtypeStruct((n_rows, d), F32),
        compiler_params=_cparams(("arbitrary",), has_side_effects=True),
        name="dispatch",
    )(n8, loff, gbase, tail_start, tail_n8, n_used, hn, route)


def _moe_kernel(te_ref, nu_ref, nxt_ref, wslot_ref, xs_ref, wg_hbm, wu_hbm, wd_hbm, y_ref,
                wgf, wuf, wdf, wgb, wub, wdb, wsem):
    i = pl.program_id(0)

    def weight_copies(e, sl):
        return (pltpu.make_async_copy(wg_hbm.at[e], wgf.at[sl], wsem.at[sl, 0]),
                pltpu.make_async_copy(wu_hbm.at[e], wuf.at[sl], wsem.at[sl, 1]),
                pltpu.make_async_copy(wd_hbm.at[e], wdf.at[sl], wsem.at[sl, 2]))

    @pl.when(i == 0)
    def _():
        for cp in weight_copies(te_ref[0], wslot_ref[0]):
            cp.start()

    @pl.when(i < nu_ref[0])
    def _():
        changed = jnp.logical_or(i == 0, te_ref[i] != te_ref[jnp.maximum(i - 1, 0)])

        @pl.when(changed)
        def _():
            sl = wslot_ref[i]
            for cp in weight_copies(te_ref[i], sl):
                cp.wait()
            wgb[...] = wgf[sl].astype(BF16)
            wub[...] = wuf[sl].astype(BF16)
            wdb[...] = wdf[sl].astype(BF16)

            @pl.when(nxt_ref[i] >= 0)
            def _():
                for cp in weight_copies(nxt_ref[i], 1 - sl):
                    cp.start()

        x = xs_ref[...].astype(BF16)
        g = jnp.dot(x, wgb[...], preferred_element_type=F32)
        u = jnp.dot(x, wub[...], preferred_element_type=F32)
        hdn = (g * jax.nn.sigmoid(g) * u).astype(BF16)
        y_ref[...] = jnp.dot(hdn, wdb[...], preferred_element_type=F32)

    @pl.when(i >= nu_ref[0])
    def _():
        y_ref[...] = jnp.zeros(y_ref.shape, F32)


def _moe(xs, tile_e, n_used, next_e, w_slot, w_g, w_u, w_d, tm):
    n_rows, d = xs.shape
    n_tiles = n_rows // tm
    ff = w_g.shape[2]
    row_blk = lambda i, te, nu, *_: (jnp.minimum(i, nu[0] - 1), 0)
    hbm = pl.BlockSpec(memory_space=pl.ANY)
    grid_spec = pltpu.PrefetchScalarGridSpec(
        num_scalar_prefetch=4,
        grid=(n_tiles,),
        in_specs=[pl.BlockSpec((tm, d), row_blk), hbm, hbm, hbm],
        out_specs=pl.BlockSpec((tm, d), lambda i, *_: (i, 0)),
        scratch_shapes=[pltpu.VMEM((2, d, ff), F32), pltpu.VMEM((2, d, ff), F32), pltpu.VMEM((2, ff, d), F32),
                        pltpu.VMEM((d, ff), BF16), pltpu.VMEM((d, ff), BF16), pltpu.VMEM((ff, d), BF16),
                        pltpu.SemaphoreType.DMA((2, 3))],
    )
    return pl.pallas_call(
        _moe_kernel,
        grid_spec=grid_spec,
        out_shape=jax.ShapeDtypeStruct((n_rows, d), F32),
        compiler_params=_cparams(("arbitrary",)),
        name="moe",
    )(tile_e, n_used, next_e, w_slot, xs, w_g, w_u, w_d)


def _combine_kernel(n8_ref, loff_ref, gbase_ref, x1_ref, route_ref, nw_ref, y_hbm, o_ref,
                    ybuf, sem, *, lcap, n_tt, seg_bits):
    i = pl.program_id(0)
    slot = i % 2
    tm = x1_ref.shape[0]

    def fetch(tile, sl, wait):
        _segment_copies(n8_ref, gbase_ref, loff_ref, y_hbm, ybuf.at[sl], sem.at[sl], tile, seg_bits, wait)

    @pl.when(i == 0)
    def _():
        ybuf[...] = jnp.zeros(ybuf.shape, F32)
        fetch(0, 0, False)

    @pl.when(i + 1 < n_tt)
    def _():
        fetch(i + 1, 1 - slot, False)

    fetch(i, slot, True)
    g1 = route_ref[:, 0:1]
    g2 = route_ref[:, 1:2]
    lp1 = route_ref[:, 2:3]
    lp2 = route_ref[:, 3:4]
    cpos = lax.broadcasted_iota(jnp.int32, (tm, lcap), 1).astype(F32)
    gsel = (jnp.where(cpos == lp1, g1, 0.0) + jnp.where(cpos == lp2, g2, 0.0)).astype(BF16)
    moe = jnp.dot(gsel, ybuf[slot].astype(BF16), preferred_element_type=F32)
    x = x1_ref[...] + moe
    o_ref[...] = x * lax.rsqrt(jnp.mean(x * x, axis=-1, keepdims=True) + NORM_EPS) * nw_ref[...]


def _combine(x1, y, route, norm_w, tables):
    t, d = x1.shape
    tm = min(ROW_TILE, t)
    n_tt = t // tm
    lcap = _local_rows(tm)
    n8, loff, gbase = tables[:3]
    grid_spec = pltpu.PrefetchScalarGridSpec(
        num_scalar_prefetch=3,
        grid=(n_tt,),
        in_specs=[pl.BlockSpec((tm, d), lambda i, *_: (i, 0)),
                  pl.BlockSpec((tm, LANES), lambda i, *_: (i, 0)),
                  pl.BlockSpec((1, d), lambda i, *_: (0, 0)),
                  pl.BlockSpec(memory_space=pl.ANY)],
        out_specs=pl.BlockSpec((tm, d), lambda i, *_: (i, 0)),
        scratch_shapes=[pltpu.VMEM((2, lcap, d), F32), pltpu.SemaphoreType.DMA((2,))],
    )
    return pl.pallas_call(
        functools.partial(_combine_kernel, lcap=lcap, n_tt=n_tt, seg_bits=(tm // SUBLANES).bit_length()),
        grid_spec=grid_spec,
        out_shape=jax.ShapeDtypeStruct((t, d), F32),
        compiler_params=_cparams(("arbitrary",)),
        name="combine",
    )(n8, loff, gbase, x1, route, norm_w.reshape(1, d), y)


def _block_diag(w):
    n, bi, bj = w.shape
    eye = jnp.eye(n, dtype=w.dtype)
    return jnp.einsum('nij,nm->nimj', w, eye).reshape(n * bi, n * bj)


def kernel(x, mix_norm_w, w_in, lambda_q1, lambda_k1, lambda_q2, lambda_k2, head_norm_w, conv_w, conv_b, w_rgate, b_rgate, w_igate, b_igate, lru_lambda, w_out, ffn_norm_w, w_router_group, b_router_group, w_router_expert, b_router_expert, w_exp_gate, w_exp_up, w_exp_down, final_norm_w):
    b, s, d = x.shape
    t = b * s
    assert w_in.shape[0] == 1, "single-layer stack only"
    att_w = N_ATT_HEADS * HEAD_DIM
    tm_moe = MOE_TILE
    xf = x.reshape(t, d)
    for l in range(1):
        lambda_init = 0.8 - 0.6 * math.exp(-0.3 * l)
        qkv, xg = _in_proj(xf, mix_norm_w[l], w_in[l].astype(BF16), att_w)
        lam_params = jnp.stack([lambda_q1[l], lambda_k1[l], lambda_q2[l], lambda_k2[l]]).astype(F32)
        att = _diff_attention(qkv.reshape(b, s, 3 * att_w), lam_params, head_norm_w[l], lambda_init)
        w_bd = jnp.concatenate([_block_diag(w_rgate[l]), _block_diag(w_igate[l])], axis=1).astype(BF16)
        b_cat = jnp.concatenate([b_rgate[l], b_igate[l]])
        rnn = _rglru(xg.reshape(b, s, xg.shape[1]), conv_w[l], conv_b[l], w_bd, b_cat, lru_lambda[l])
        w_route = jnp.concatenate([w_router_group[l], w_router_expert[l]], axis=1)
        w_route = jnp.pad(w_route, ((0, 0), (0, LANES - w_route.shape[1]))).astype(BF16)
        b_route = jnp.concatenate([b_router_group[l], b_router_expert[l]])
        b_route = jnp.pad(b_route, (0, LANES - b_route.shape[0])).reshape(1, LANES).astype(F32)
        x1, hn, route, n8_tiles = _out_proj(att.reshape(t, att_w), rnn.reshape(t, -1), xf, w_out[l].astype(BF16),
                                            ffn_norm_w[l], w_route, b_route)
        n_tt = n8_tiles.shape[0]
        max_rows = TOP_K * t + n_tt * N_EXPERTS * (SUBLANES - 1) + N_EXPERTS * (tm_moe - 1)
        n_tiles = -(-max_rows // tm_moe)
        tables = _segment_tables(n8_tiles, tm_moe, n_tiles)
        xs = _dispatch(hn, route, tables, n_tiles * tm_moe, tm_moe)
        y = _moe(xs, tables[3], tables[4], tables[7], tables[8], w_exp_gate[l], w_exp_up[l], w_exp_down[l], tm_moe)
        out = _combine(x1, y, route, final_norm_w, tables)
    return out.reshape(b, s, d)
```

```python
import functools
import math

import numpy as np
import jax
import jax.numpy as jnp
from jax import lax
from jax.experimental import pallas as pl
from jax.experimental.pallas import tpu as pltpu

F32 = jnp.float32
BF16 = jnp.bfloat16

N_ATT_HEADS = 4
HEAD_DIM = 128
QK_DIM = 64
N_RNN_BLOCKS = 8
CONV_WIDTH = 4
LRU_C = 8.0
N_GROUPS = 4
EXPERTS_PER_GROUP = 8
N_EXPERTS = N_GROUPS * EXPERTS_PER_GROUP
TOP_K = 2
NORM_EPS = 1e-6
HEAD_NORM_EPS = 1e-5
LANES = 128
SUBLANES = 8
NEG_BIG = -1e30

ROW_TILE = 512
ATT_TILE = 512
V_ROWS = HEAD_DIM + 16
LRU_TILE = 512
LRU_CHUNK = 128
MOE_TILE = 256
VMEM_LIMIT = 48 * 1024 * 1024


def _cparams(sem, vmem=VMEM_LIMIT, **kw):
    return pltpu.CompilerParams(dimension_semantics=sem, vmem_limit_bytes=vmem, **kw)


def _inproj_kernel(x_ref, nw_ref, w_ref, qkv_ref, xg_ref, *, att_w):
    x = x_ref[...]
    ms = jnp.mean(x * x, axis=-1, keepdims=True)
    hn = (x * lax.rsqrt(ms + NORM_EPS) * nw_ref[...]).astype(BF16)
    p = jnp.dot(hn, w_ref[...], preferred_element_type=F32)
    scale = QK_DIM ** -0.5
    qkv_ref[:, :att_w] = (p[:, :att_w] * scale).astype(BF16)
    qkv_ref[:, att_w:] = p[:, att_w:3 * att_w].astype(BF16)
    xg_ref[...] = p[:, 3 * att_w:]


def _in_proj(xf, norm_w, w_in_bf, att_w):
    t, d = xf.shape
    n = w_in_bf.shape[1]
    tm = min(ROW_TILE, t)
    return pl.pallas_call(
        functools.partial(_inproj_kernel, att_w=att_w),
        grid=(t // tm,),
        in_specs=[pl.BlockSpec((tm, d), lambda i: (i, 0)),
                  pl.BlockSpec((1, d), lambda i: (0, 0)),
                  pl.BlockSpec((d, n), lambda i: (0, 0))],
        out_specs=[pl.BlockSpec((tm, 3 * att_w), lambda i: (i, 0)),
                   pl.BlockSpec((tm, n - 3 * att_w), lambda i: (i, 0))],
        out_shape=[jax.ShapeDtypeStruct((t, 3 * att_w), BF16),
                   jax.ShapeDtypeStruct((t, n - 3 * att_w), F32)],
        compiler_params=_cparams(("parallel",)),
        name="in_proj",
    )(xf, norm_w.reshape(1, d), w_in_bf)


def _attn_kernel(slope_ref, lam_ref, hw_ref, q_ref, k_ref, v_ref, o_ref,
                 k1a, k2a, vt, s1, m1, a1, s2, m2, a2, *, tq, s_len, lambda_init):
    h = pl.program_id(1)
    qi = pl.program_id(2)
    slope = slope_ref[h]
    lane = lax.broadcasted_iota(jnp.int32, (tq, HEAD_DIM), 1)

    @pl.when(qi == 0)
    def _():
        ones_row = jnp.where(lax.broadcasted_iota(jnp.int32, (V_ROWS - HEAD_DIM, tq), 0) == 0, 1.0, 0.0)
        for c in range(s_len // tq):
            rows = slice(c * tq, (c + 1) * tq)
            kk = k_ref[0, rows, :].astype(F32)
            j = c * tq + lax.broadcasted_iota(jnp.int32, (tq, HEAD_DIM), 0)
            j_lo = (j & 255).astype(F32)
            j_hi = (j - (j & 255)).astype(F32)
            aug = jnp.where(lane == QK_DIM, slope * j_hi, jnp.where(lane == QK_DIM + 1, slope * j_lo, 0.0))
            k1a[rows, :] = jnp.where(lane < QK_DIM, kk, aug).astype(BF16)
            k2a[rows, :] = jnp.where(lane < QK_DIM, pltpu.roll(kk, QK_DIM, axis=1), aug).astype(BF16)
            vt[c, :HEAD_DIM, :] = v_ref[0, rows, :].astype(F32).T.astype(BF16)
            vt[c, HEAD_DIM:, :] = ones_row.astype(BF16)

    ones_cols = jnp.where((lane == QK_DIM) | (lane == QK_DIM + 1), 1.0, 0.0)
    q = q_ref[0].astype(F32)
    q1t = jnp.where(lane < QK_DIM, q, ones_cols).T.astype(BF16)
    q2t = jnp.where(lane < QK_DIM, pltpu.roll(q, QK_DIM, axis=1), ones_cols).T.astype(BF16)

    for m, a in ((m1, a1), (m2, a2)):
        m[...] = jnp.full(m.shape, NEG_BIG, F32)
        a[...] = jnp.zeros(a.shape, F32)

    maps = ((q1t, k1a, s1, m1, a1), (q2t, k2a, s2, m2, a2))

    def scores(c, slot):
        rows = pl.ds(pl.multiple_of(c * tq, tq), tq)
        for qt, ka, sb, _, _ in maps:
            sb[slot] = jnp.dot(ka[rows, :], qt, preferred_element_type=F32)

    def softmax_pv(c, slot):
        for _, _, sb, m, a in maps:
            s = sb[slot]
            m_prev = m[...]
            m_new = jnp.maximum(m_prev, jnp.max(s, axis=0, keepdims=True))
            p = jnp.exp(s - m_new).astype(BF16)
            a[...] = jnp.exp(m_prev - m_new) * a[...] + jnp.dot(vt[c], p, preferred_element_type=F32)
            m[...] = m_new

    def softmax_pv_diagonal(c, slot):
        hq = tq // 2
        keep_t = (lax.broadcasted_iota(jnp.int32, (hq, tq), 0) <= lax.broadcasted_iota(jnp.int32, (hq, tq), 1))
        keep_b = (lax.broadcasted_iota(jnp.int32, (hq, hq), 0) <= lax.broadcasted_iota(jnp.int32, (hq, hq), 1))
        for _, _, sb, m, a in maps:
            top = jnp.where(keep_t, sb[slot, :hq, :], NEG_BIG)
            bot = jnp.where(keep_b, sb[slot, hq:, hq:], NEG_BIG)
            mt = jnp.max(top, axis=0, keepdims=True)
            mb = jnp.max(bot, axis=0, keepdims=True)
            m_prev = m[...]
            m_new = jnp.maximum(m_prev, jnp.concatenate([mt[:, :hq], jnp.maximum(mt[:, hq:], mb)], axis=1))
            p_top = jnp.exp(top - m_new).astype(BF16)
            p_bot = jnp.exp(bot - m_new[:, hq:]).astype(BF16)
            a[...] = (jnp.exp(m_prev - m_new) * a[...]
                      + jnp.dot(vt[c, :, :hq], p_top, preferred_element_type=F32))
            a[:, hq:] += jnp.dot(vt[c, :, hq:], p_bot, preferred_element_type=F32)
            m[...] = m_new

    scores(0, 0)

    def body(j, c):
        scores(2 * j + 1, 1)
        softmax_pv(2 * j, 0)
        scores(2 * j + 2, 0)
        softmax_pv(2 * j + 1, 1)
        return c

    lax.fori_loop(0, qi // 2, body, 0)

    @pl.when(qi % 2 == 0)
    def _():
        softmax_pv_diagonal(qi, 0)

    @pl.when(qi % 2 == 1)
    def _():
        scores(qi, 1)
        softmax_pv(qi - 1, 0)
        softmax_pv_diagonal(qi, 1)

    lam = (jnp.exp(jnp.sum(lam_ref[0:1, :] * lam_ref[1:2, :], axis=1, keepdims=True))
           - jnp.exp(jnp.sum(lam_ref[2:3, :] * lam_ref[3:4, :], axis=1, keepdims=True))
           + lambda_init)
    o1 = a1[:HEAD_DIM, :] / a1[HEAD_DIM:HEAD_DIM + 1, :]
    o2 = a2[:HEAD_DIM, :] / a2[HEAD_DIM:HEAD_DIM + 1, :]
    o = (o1 - lam * o2).T
    o = o * lax.rsqrt(jnp.mean(o * o, axis=-1, keepdims=True) + HEAD_NORM_EPS)
    o_ref[0] = (o * hw_ref[...] * (1.0 - lambda_init)).astype(o_ref.dtype)


def _diff_attention(qkv, lam_params, head_norm_w, lambda_init):
    b, s, w3 = qkv.shape
    nh = N_ATT_HEADS
    tq = min(ATT_TILE, s)
    slopes = jnp.asarray(np.array([2.0 ** (-8.0 * (i + 1) / nh) for i in range(nh)], dtype=np.float32))
    return pl.pallas_call(
        functools.partial(_attn_kernel, tq=tq, s_len=s, lambda_init=lambda_init),
        grid=(b, nh, s // tq),
        in_specs=[pl.BlockSpec(memory_space=pltpu.SMEM),
                  pl.BlockSpec((4, QK_DIM), lambda bi, hi, qi: (0, 0)),
                  pl.BlockSpec((1, HEAD_DIM), lambda bi, hi, qi: (0, 0)),
                  pl.BlockSpec((1, tq, HEAD_DIM), lambda bi, hi, qi: (bi, qi, hi)),
                  pl.BlockSpec((1, s, HEAD_DIM), lambda bi, hi, qi: (bi, 0, nh + hi)),
                  pl.BlockSpec((1, s, HEAD_DIM), lambda bi, hi, qi: (bi, 0, 2 * nh + hi))],
        out_specs=pl.BlockSpec((1, tq, HEAD_DIM), lambda bi, hi, qi: (bi, qi, hi)),
        out_shape=jax.ShapeDtypeStruct((b, s, nh * HEAD_DIM), BF16),
        scratch_shapes=[pltpu.VMEM((s, HEAD_DIM), BF16), pltpu.VMEM((s, HEAD_DIM), BF16),
                        pltpu.VMEM((s // tq, V_ROWS, tq), BF16),
                        pltpu.VMEM((2, tq, tq), F32), pltpu.VMEM((1, tq), F32), pltpu.VMEM((V_ROWS, tq), F32),
                        pltpu.VMEM((2, tq, tq), F32), pltpu.VMEM((1, tq), F32), pltpu.VMEM((V_ROWS, tq), F32)],
        compiler_params=_cparams(("parallel", "parallel", "arbitrary")),
        name="diff_attn",
    )(slopes, lam_params, head_norm_w.reshape(1, HEAD_DIM), qkv, qkv, qkv)


def _gelu_tanh(x):
    return 0.5 * x * (1.0 + jnp.tanh(math.sqrt(2.0 / math.pi) * (x + 0.044715 * (x * x * x))))


def _rglru_kernel(xr_ref, gr_ref, cw_ref, cb_ref, w_ref, b_ref, lam_ref, o_ref,
                  xs, carry_h, a_s, u_s, *, ts, ch, c_w):
    si = pl.program_id(1)

    @pl.when(si == 0)
    def _():
        xs[0:8, :] = jnp.zeros((8, c_w), F32)
        carry_h[...] = jnp.zeros(carry_h.shape, F32)

    xs[8:, :] = xr_ref[0]
    neg_lam = -lam_ref[...]
    sp = jnp.maximum(neg_lam, 0.0) + jnp.log1p(jnp.exp(-jnp.abs(neg_lam)))
    cw = cw_ref[...]
    cb = cb_ref[...]
    bias = b_ref[...]
    r8 = lax.broadcasted_iota(jnp.int32, (ch // SUBLANES, SUBLANES, c_w), 1)

    def chunk(c, carry):
        r0 = pl.multiple_of(c * ch, ch)
        win = xs[pl.ds(r0, ch + 8), :]
        xc = cw[3:4, :] * win[8:] + cb
        for k in (1, 2, 3):
            xc = xc + cw[3 - k:4 - k, :] * pltpu.roll(win, k, axis=0)[8:]
        z = jnp.dot(xc.astype(BF16), w_ref[...], preferred_element_type=F32) + bias
        r = jax.nn.sigmoid(z[:, :c_w])
        ig = jax.nn.sigmoid(z[:, c_w:])
        log_a = (-LRU_C) * r * sp
        a = jnp.exp(log_a)
        w = jnp.tanh(-log_a) * (1.0 + a * a)
        u = jnp.where(w > 0.0, w * lax.rsqrt(w), 0.0) * ig * xc
        a = a.reshape(ch // SUBLANES, SUBLANES, c_w)
        u = u.reshape(ch // SUBLANES, SUBLANES, c_w)
        for k in (1, 2, 4):
            a_sh = pltpu.roll(a, k, axis=1)
            u_sh = pltpu.roll(u, k, axis=1)
            ok = r8 >= k
            u = jnp.where(ok, u + a * u_sh, u)
            a = jnp.where(ok, a * a_sh, a)
        a_s[pl.ds(r0, ch), :] = a.reshape(ch, c_w)
        u_s[pl.ds(r0, ch), :] = u.reshape(ch, c_w)
        return carry

    lax.fori_loop(0, ts // ch, chunk, 0)

    def grp(g, hprev):
        r0 = pl.multiple_of(g * 8, 8)
        hg = u_s[pl.ds(r0, 8), :] + a_s[pl.ds(r0, 8), :] * hprev
        u_s[pl.ds(r0, 8), :] = hg
        return hg[7:8, :]

    hlast = lax.fori_loop(0, ts // 8, grp, carry_h[0:1, :], unroll=8)
    carry_h[0:1, :] = hlast
    xs[0:8, :] = xs[ts:ts + 8, :]
    o_ref[0] = (u_s[...] * _gelu_tanh(gr_ref[0])).astype(o_ref.dtype)


def _rglru(xg, conv_w, conv_b, w_bd, b_cat, lru_lambda):
    b, s, w2 = xg.shape
    c_w = w2 // 2
    ts = min(LRU_TILE, s)
    ch = min(LRU_CHUNK, ts)
    return pl.pallas_call(
        functools.partial(_rglru_kernel, ts=ts, ch=ch, c_w=c_w),
        grid=(b, s // ts),
        in_specs=[pl.BlockSpec((1, ts, c_w), lambda bi, si: (bi, si, 0)),
                  pl.BlockSpec((1, ts, c_w), lambda bi, si: (bi, si, 1)),
                  pl.BlockSpec((CONV_WIDTH, c_w), lambda bi, si: (0, 0)),
                  pl.BlockSpec((1, c_w), lambda bi, si: (0, 0)),
                  pl.BlockSpec((c_w, 2 * c_w), lambda bi, si: (0, 0)),
                  pl.BlockSpec((1, 2 * c_w), lambda bi, si: (0, 0)),
                  pl.BlockSpec((1, c_w), lambda bi, si: (0, 0))],
        out_specs=pl.BlockSpec((1, ts, c_w), lambda bi, si: (bi, si, 0)),
        out_shape=jax.ShapeDtypeStruct((b, s, c_w), BF16),
        scratch_shapes=[pltpu.VMEM((ts + 8, c_w), F32), pltpu.VMEM((8, c_w), F32),
                        pltpu.VMEM((ts, c_w), F32), pltpu.VMEM((ts, c_w), F32)],
        compiler_params=_cparams(("parallel", "arbitrary")),
        name="rglru",
    )(xg, xg, conv_w, conv_b.reshape(1, c_w), w_bd, b_cat.reshape(1, 2 * c_w), lru_lambda.reshape(1, c_w))


def _outproj_kernel(att_ref, rnn_ref, x_ref, wo_ref, nw_ref, wr_ref, br_ref,
                    x1_ref, hn_ref, route_ref, n8_ref, *, att_w):
    y = jnp.dot(att_ref[...], wo_ref[:att_w, :], preferred_element_type=F32)
    y = y + jnp.dot(rnn_ref[...], wo_ref[att_w:, :], preferred_element_type=F32)
    x1 = x_ref[...] + y
    x1_ref[...] = x1
    hn = (x1 * lax.rsqrt(jnp.mean(x1 * x1, axis=-1, keepdims=True) + NORM_EPS) * nw_ref[...]).astype(BF16)
    hn_ref[...] = hn
    lg = jnp.dot(hn, wr_ref[...], preferred_element_type=F32) + br_ref[...]
    tm = lg.shape[0]

    col = lax.broadcasted_iota(jnp.int32, lg.shape, 1)
    colf = col.astype(F32)
    big = float(LANES)
    ninf = -jnp.inf
    is_g = col < N_GROUPS
    lgm = jnp.where(is_g, lg, ninf)
    mg = jnp.max(lgm, axis=1, keepdims=True)
    g_sel = jnp.min(jnp.where(lgm == mg, colf, big), axis=1, keepdims=True)
    pg = 1.0 / jnp.sum(jnp.where(is_g, jnp.exp(lgm - mg), 0.0), axis=1, keepdims=True)
    lo = N_GROUPS + EXPERTS_PER_GROUP * g_sel
    in_grp = (colf >= lo) & (colf < lo + EXPERTS_PER_GROUP)
    lem = jnp.where(in_grp, lg, ninf)
    v1 = jnp.max(lem, axis=1, keepdims=True)
    i1 = jnp.min(jnp.where(lem == v1, colf, big), axis=1, keepdims=True)
    lem2 = jnp.where(colf == i1, ninf, lem)
    v2 = jnp.max(lem2, axis=1, keepdims=True)
    i2 = jnp.min(jnp.where(lem2 == v2, colf, big), axis=1, keepdims=True)
    e2 = jnp.exp(v2 - v1)
    den = 1.0 + e2
    g1 = pg / den
    g2 = pg * e2 / den

    oh1 = jnp.where(colf == i1, 1.0, 0.0)
    oh2 = jnp.where(colf == i2, 1.0, 0.0)
    oh = oh1 + oh2
    earlier = (lax.broadcasted_iota(jnp.int32, (tm, tm), 1)
               < lax.broadcasted_iota(jnp.int32, (tm, tm), 0)).astype(BF16)
    pref = jnp.dot(earlier, oh.astype(BF16), preferred_element_type=F32)
    cnt = jnp.sum(oh, axis=0, keepdims=True)
    n8 = jnp.floor((cnt + (SUBLANES - 1)) * (1.0 / SUBLANES))
    before = (lax.broadcasted_iota(jnp.int32, (LANES, LANES), 0)
              < lax.broadcasted_iota(jnp.int32, (LANES, LANES), 1)).astype(BF16)
    loff8 = jnp.dot(jnp.broadcast_to(n8, (SUBLANES, LANES)).astype(BF16), before,
                    preferred_element_type=F32)[0:1]
    pos = SUBLANES * loff8 + pref
    lp1 = jnp.sum(oh1 * pos, axis=1, keepdims=True)
    lp2 = jnp.sum(oh2 * pos, axis=1, keepdims=True)
    route_ref[...] = jnp.where(col == 0, g1,
                     jnp.where(col == 1, g2,
                     jnp.where(col == 2, lp1,
                     jnp.where(col == 3, lp2, 0.0))))
    n8_ref[0] = n8


def _out_proj(att, rnn, xf, w_out_bf, norm_w, w_route_bf, b_route):
    t, d = xf.shape
    att_w = att.shape[1]
    tm = min(ROW_TILE, t)
    row = lambda i: (i, 0)
    fix = lambda i: (0, 0)
    return pl.pallas_call(
        functools.partial(_outproj_kernel, att_w=att_w),
        grid=(t // tm,),
        in_specs=[pl.BlockSpec((tm, att_w), row), pl.BlockSpec((tm, rnn.shape[1]), row),
                  pl.BlockSpec((tm, d), row), pl.BlockSpec(w_out_bf.shape, fix),
                  pl.BlockSpec((1, d), fix), pl.BlockSpec((d, LANES), fix), pl.BlockSpec((1, LANES), fix)],
        out_specs=[pl.BlockSpec((tm, d), row), pl.BlockSpec((tm, d), row), pl.BlockSpec((tm, LANES), row),
                   pl.BlockSpec((1, 1, LANES), lambda i: (i, 0, 0))],
        out_shape=[jax.ShapeDtypeStruct((t, d), F32), jax.ShapeDtypeStruct((t, d), BF16),
                   jax.ShapeDtypeStruct((t, LANES), F32),
                   jax.ShapeDtypeStruct((t // tm, 1, LANES), F32)],
        compiler_params=_cparams(("parallel",)),
        name="out_proj",
    )(att, rnn, xf, w_out_bf, norm_w.reshape(1, d), w_route_bf, b_route)


def _local_rows(tm):
    return -(-(TOP_K * tm + N_EXPERTS * (SUBLANES - 1)) // LANES) * LANES


def _segment_tables(n8_tiles, tm_moe, n_tiles):
    n8 = n8_tiles[:, 0, N_GROUPS:N_GROUPS + N_EXPERTS].astype(jnp.int32)
    c8 = n8 * SUBLANES
    loff = jnp.cumsum(c8, axis=1) - c8
    gtot = jnp.sum(c8, axis=0)
    gpad = (gtot + tm_moe - 1) // tm_moe * tm_moe
    gend = jnp.cumsum(gpad)
    gstart = gend - gpad
    gbase = gstart[None, :] + jnp.cumsum(c8, axis=0) - c8
    tile_row0 = jnp.arange(n_tiles, dtype=jnp.int32) * tm_moe
    tile_e = jnp.minimum(jnp.sum((gend[None, :] <= tile_row0[:, None]).astype(jnp.int32), axis=1),
                         N_EXPERTS - 1).astype(jnp.int32)
    n_used = (gend[-1] // tm_moe).astype(jnp.int32).reshape(1)
    tail_start = (gstart + gtot).astype(jnp.int32)
    tail_n8 = ((gpad - gtot) // SUBLANES).astype(jnp.int32)
    after = gend[tile_e] // tm_moe
    next_e = jnp.where(after < n_used[0], tile_e[jnp.minimum(after, n_tiles - 1)], -1).astype(jnp.int32)
    first = jnp.concatenate([jnp.ones((1,), jnp.int32), (tile_e[1:] != tile_e[:-1]).astype(jnp.int32)])
    w_slot = ((jnp.cumsum(first) - 1) % 2).astype(jnp.int32)
    return (n8.reshape(-1), loff.reshape(-1).astype(jnp.int32), gbase.reshape(-1).astype(jnp.int32),
            tile_e, n_used, tail_start, tail_n8, next_e, w_slot)


def _segment_copies(n8_ref, src_off_ref, dst_off_ref, src, dst, sem, tile, n_bits, wait):
    def per_expert(e, c):
        k = tile * N_EXPERTS + e
        n = n8_ref[k]
        s0 = src_off_ref[k]
        d0 = dst_off_ref[k]
        for b in range(n_bits):
            @pl.when(((n >> b) & 1) == 1)
            def _():
                off = (n & ((1 << b) - 1)) * SUBLANES
                cp = pltpu.make_async_copy(
                    src.at[pl.ds(pl.multiple_of(s0 + off, SUBLANES), SUBLANES << b), :],
                    dst.at[pl.ds(pl.multiple_of(d0 + off, SUBLANES), SUBLANES << b), :], sem)
                if wait:
                    cp.wait()
                else:
                    cp.start()
        return c
    lax.fori_loop(0, N_EXPERTS, per_expert, 0)


def _dispatch_kernel(n8_ref, loff_ref, gbase_ref, tstart_ref, tn8_ref, nu_ref, hn_ref, route_ref, xs_hbm,
                     stage, zbuf, sem, zsem, *, lcap, n_tt, seg_bits, tail_bits):
    i = pl.program_id(0)
    slot = i % 2
    tm = hn_ref.shape[0]
    tm_moe = zbuf.shape[0]
    n_tiles = xs_hbm.shape[0] // tm_moe

    def tail_copies(wait):
        def go(cp):
            if wait:
                cp.wait()
            else:
                cp.start()

        def per_expert(e, c):
            n = tn8_ref[e]
            d0 = tstart_ref[e]
            for b in range(tail_bits):
                @pl.when(((n >> b) & 1) == 1)
                def _():
                    off = (n & ((1 << b) - 1)) * SUBLANES
                    go(pltpu.make_async_copy(
                        zbuf.at[pl.ds(0, SUBLANES << b), :],
                        xs_hbm.at[pl.ds(pl.multiple_of(d0 + off, SUBLANES), SUBLANES << b), :], zsem.at[0]))
            return c
        lax.fori_loop(0, N_EXPERTS, per_expert, 0)

        def per_unused_tile(j, c):
            go(pltpu.make_async_copy(zbuf, xs_hbm.at[pl.ds(pl.multiple_of(j * tm_moe, tm_moe), tm_moe), :],
                                     zsem.at[0]))
            return c
        lax.fori_loop(nu_ref[0], n_tiles, per_unused_tile, 0)

    @pl.when(i == 0)
    def _():
        zbuf[...] = jnp.zeros(zbuf.shape, F32)
        tail_copies(False)

    @pl.when(i >= 2)
    def _():
        _segment_copies(n8_ref, loff_ref, gbase_ref, stage.at[slot], xs_hbm, sem.at[slot], i - 2, seg_bits, True)

    lp1 = route_ref[:, 2:3]
    lp2 = route_ref[:, 3:4]
    cpos = lax.broadcasted_iota(jnp.int32, (tm, lcap), 1).astype(F32)
    sel_t = jnp.where((cpos == lp1) | (cpos == lp2), 1.0, 0.0)
    stage[slot] = jnp.dot(sel_t.T.astype(BF16), hn_ref[...], preferred_element_type=F32)
    _segment_copies(n8_ref, loff_ref, gbase_ref, stage.at[slot], xs_hbm, sem.at[slot], i, seg_bits, False)

    @pl.when(i == n_tt - 1)
    def _():
        _segment_copies(n8_ref, loff_ref, gbase_ref, stage.at[slot], xs_hbm, sem.at[slot], i, seg_bits, True)
        if n_tt > 1:
            _segment_copies(n8_ref, loff_ref, gbase_ref, stage.at[1 - slot], xs_hbm, sem.at[1 - slot],
                            i - 1, seg_bits, True)
        tail_copies(True)


def _dispatch(hn, route, tables, n_rows, tm_moe):
    t, d = hn.shape
    tm = min(ROW_TILE, t)
    n_tt = t // tm
    lcap = _local_rows(tm)
    n8, loff, gbase, _, n_used, tail_start, tail_n8 = tables[:7]
    grid_spec = pltpu.PrefetchScalarGridSpec(
        num_scalar_prefetch=6,
        grid=(n_tt,),
        in_specs=[pl.BlockSpec((tm, d), lambda i, *_: (i, 0)),
                  pl.BlockSpec((tm, LANES), lambda i, *_: (i, 0))],
        out_specs=pl.BlockSpec(memory_space=pl.ANY),
        scratch_shapes=[pltpu.VMEM((2, lcap, d), F32), pltpu.VMEM((tm_moe, d), F32),
                        pltpu.SemaphoreType.DMA((2,)), pltpu.SemaphoreType.DMA((1,))],
    )
    return pl.pallas_call(
        functools.partial(_dispatch_kernel, lcap=lcap, n_tt=n_tt,
                          seg_bits=(tm // SUBLANES).bit_length(), tail_bits=(tm_moe // SUBLANES - 1).bit_length()),
        grid_spec=grid_spec,
        out_shape=jax.ShapeDtypeStruct((n_rows, d), F32),
        compiler_params=_cparams(("arbitrary",), has_side_effects=True),
        name="dispatch",
    )(n8, loff, gbase, tail_start, tail_n8, n_used, hn, route)


def _moe_kernel(te_ref, nu_ref, nxt_ref, wslot_ref, xs_ref, wg_hbm, wu_hbm, wd_hbm, y_ref,
                wgf, wuf, wdf, wgb, wub, wdb, wsem):
    i = pl.program_id(0)

    def weight_copies(e, sl):
        return (pltpu.make_async_copy(wg_hbm.at[e], wgf.at[sl], wsem.at[sl, 0]),
                pltpu.make_async_copy(wu_hbm.at[e], wuf.at[sl], wsem.at[sl, 1]),
                pltpu.make_async_copy(wd_hbm.at[e], wdf.at[sl], wsem.at[sl, 2]))

    @pl.when(i == 0)
    def _():
        for cp in weight_copies(te_ref[0], wslot_ref[0]):
            cp.start()

    @pl.when(i < nu_ref[0])
    def _():
        changed = jnp.logical_or(i == 0, te_ref[i] != te_ref[jnp.maximum(i - 1, 0)])

        @pl.when(changed)
        def _():
            sl = wslot_ref[i]
            for cp in weight_copies(te_ref[i], sl):
                cp.wait()
            wgb[...] = wgf[sl].astype(BF16)
            wub[...] = wuf[sl].astype(BF16)
            wdb[...] = wdf[sl].astype(BF16)

            @pl.when(nxt_ref[i] >= 0)
            def _():
                for cp in weight_copies(nxt_ref[i], 1 - sl):
                    cp.start()

        x = xs_ref[...].astype(BF16)
        g = jnp.dot(x, wgb[...], preferred_element_type=F32)
        u = jnp.dot(x, wub[...], preferred_element_type=F32)
        hdn = (g * jax.nn.sigmoid(g) * u).astype(BF16)
        y_ref[...] = jnp.dot(hdn, wdb[...], preferred_element_type=F32)

    @pl.when(i >= nu_ref[0])
    def _():
        y_ref[...] = jnp.zeros(y_ref.shape, F32)


def _moe(xs, tile_e, n_used, next_e, w_slot, w_g, w_u, w_d, tm):
    n_rows, d = xs.shape
    n_tiles = n_rows // tm
    ff = w_g.shape[2]
    row_blk = lambda i, te, nu, *_: (jnp.minimum(i, nu[0] - 1), 0)
    hbm = pl.BlockSpec(memory_space=pl.ANY)
    grid_spec = pltpu.PrefetchScalarGridSpec(
        num_scalar_prefetch=4,
        grid=(n_tiles,),
        in_specs=[pl.BlockSpec((tm, d), row_blk), hbm, hbm, hbm],
        out_specs=pl.BlockSpec((tm, d), lambda i, *_: (i, 0)),
        scratch_shapes=[pltpu.VMEM((2, d, ff), F32), pltpu.VMEM((2, d, ff), F32), pltpu.VMEM((2, ff, d), F32),
                        pltpu.VMEM((d, ff), BF16), pltpu.VMEM((d, ff), BF16), pltpu.VMEM((ff, d), BF16),
                        pltpu.SemaphoreType.DMA((2, 3))],
    )
    return pl.pallas_call(
        _moe_kernel,
        grid_spec=grid_spec,
        out_shape=jax.ShapeDtypeStruct((n_rows, d), F32),
        compiler_params=_cparams(("arbitrary",)),
        name="moe",
    )(tile_e, n_used, next_e, w_slot, xs, w_g, w_u, w_d)


def _combine_kernel(n8_ref, loff_ref, gbase_ref, x1_ref, route_ref, nw_ref, y_hbm, o_ref,
                    ybuf, sem, *, lcap, n_tt, seg_bits):
    i = pl.program_id(0)
    slot = i % 2
    tm = x1_ref.shape[0]

    def fetch(tile, sl, wait):
        _segment_copies(n8_ref, gbase_ref, loff_ref, y_hbm, ybuf.at[sl], sem.at[sl], tile, seg_bits, wait)

    @pl.when(i == 0)
    def _():
        ybuf[...] = jnp.zeros(ybuf.shape, F32)
        fetch(0, 0, False)

    @pl.when(i + 1 < n_tt)
    def _():
        fetch(i + 1, 1 - slot, False)

    fetch(i, slot, True)
    g1 = route_ref[:, 0:1]
    g2 = route_ref[:, 1:2]
    lp1 = route_ref[:, 2:3]
    lp2 = route_ref[:, 3:4]
    cpos = lax.broadcasted_iota(jnp.int32, (tm, lcap), 1).astype(F32)
    gsel = (jnp.where(cpos == lp1, g1, 0.0) + jnp.where(cpos == lp2, g2, 0.0)).astype(BF16)
    moe = jnp.dot(gsel, ybuf[slot].astype(BF16), preferred_element_type=F32)
    x = x1_ref[...] + moe
    o_ref[...] = x * lax.rsqrt(jnp.mean(x * x, axis=-1, keepdims=True) + NORM_EPS) * nw_ref[...]


def _combine(x1, y, route, norm_w, tables):
    t, d = x1.shape
    tm = min(ROW_TILE, t)
    n_tt = t // tm
    lcap = _local_rows(tm)
    n8, loff, gbase = tables[:3]
    grid_spec = pltpu.PrefetchScalarGridSpec(
        num_scalar_prefetch=3,
        grid=(n_tt,),
        in_specs=[pl.BlockSpec((tm, d), lambda i, *_: (i, 0)),
                  pl.BlockSpec((tm, LANES), lambda i, *_: (i, 0)),
                  pl.BlockSpec((1, d), lambda i, *_: (0, 0)),
                  pl.BlockSpec(memory_space=pl.ANY)],
        out_specs=pl.BlockSpec((tm, d), lambda i, *_: (i, 0)),
        scratch_shapes=[pltpu.VMEM((2, lcap, d), F32), pltpu.SemaphoreType.DMA((2,))],
    )
    return pl.pallas_call(
        functools.partial(_combine_kernel, lcap=lcap, n_tt=n_tt, seg_bits=(tm // SUBLANES).bit_length()),
        grid_spec=grid_spec,
        out_shape=jax.ShapeDtypeStruct((t, d), F32),
        compiler_params=_cparams(("arbitrary",)),
        name="combine",
    )(n8, loff, gbase, x1, route, norm_w.reshape(1, d), y)


def _block_diag(w):
    n, bi, bj = w.shape
    eye = jnp.eye(n, dtype=w.dtype)
    return jnp.einsum('nij,nm->nimj', w, eye).reshape(n * bi, n * bj)


def kernel(x, mix_norm_w, w_in, lambda_q1, lambda_k1, lambda_q2, lambda_k2, head_norm_w, conv_w, conv_b, w_rgate, b_rgate, w_igate, b_igate, lru_lambda, w_out, ffn_norm_w, w_router_group, b_router_group, w_router_expert, b_router_expert, w_exp_gate, w_exp_up, w_exp_down, final_norm_w):
    b, s, d = x.shape
    t = b * s
    assert w_in.shape[0] == 1, "single-layer stack only"
    att_w = N_ATT_HEADS * HEAD_DIM
    tm_moe = MOE_TILE
    xf = x.reshape(t, d)
    for l in range(1):
        lambda_init = 0.8 - 0.6 * math.exp(-0.3 * l)
        qkv, xg = _in_proj(xf, mix_norm_w[l], w_in[l].astype(BF16), att_w)
        lam_params = jnp.stack([lambda_q1[l], lambda_k1[l], lambda_q2[l], lambda_k2[l]]).astype(F32)
        att = _diff_attention(qkv.reshape(b, s, 3 * att_w), lam_params, head_norm_w[l], lambda_init)
        w_bd = jnp.concatenate([_block_diag(w_rgate[l]), _block_diag(w_igate[l])], axis=1).astype(BF16)
        b_cat = jnp.concatenate([b_rgate[l], b_igate[l]])
        rnn = _rglru(xg.reshape(b, s, xg.shape[1]), conv_w[l], conv_b[l], w_bd, b_cat, lru_lambda[l])
        w_route = jnp.concatenate([w_router_group[l], w_router_expert[l]], axis=1)
        w_route = jnp.pad(w_route, ((0, 0), (0, LANES - w_route.shape[1]))).astype(BF16)
        b_route = jnp.concatenate([b_router_group[l], b_router_expert[l]])
        b_route = jnp.pad(b_route, (0, LANES - b_route.shape[0])).reshape(1, LANES).astype(F32)
        x1, hn, route, n8_tiles = _out_proj(att.reshape(t, att_w), rnn.reshape(t, -1), xf, w_out[l].astype(BF16),
                                            ffn_norm_w[l], w_route, b_route)
        n_tt = n8_tiles.shape[0]
        max_rows = TOP_K * t + n_tt * N_EXPERTS * (SUBLANES - 1) + N_EXPERTS * (tm_moe - 1)
        n_tiles = -(-max_rows // tm_moe)
        tables = _segment_tables(n8_tiles, tm_moe, n_tiles)
        xs = _dispatch(hn, route, tables, n_tiles * tm_moe, tm_moe)
        y = _moe(xs, tables[3], tables[4], tables[7], tables[8], w_exp_gate[l], w_exp_up[l], w_exp_down[l], tm_moe)
        out = _combine(x1, y, route, final_norm_w, tables)
    return out.reshape(b, s, d)
```

```python
import functools
import math

import numpy as np
import jax
import jax.numpy as jnp
from jax import lax
from jax.experimental import pallas as pl
from jax.experimental.pallas import tpu as pltpu

F32 = jnp.float32
BF16 = jnp.bfloat16

N_ATT_HEADS = 4
HEAD_DIM = 128
QK_DIM = 64
N_RNN_BLOCKS = 8
CONV_WIDTH = 4
LRU_C = 8.0
N_GROUPS = 4
EXPERTS_PER_GROUP = 8
N_EXPERTS = N_GROUPS * EXPERTS_PER_GROUP
TOP_K = 2
NORM_EPS = 1e-6
HEAD_NORM_EPS = 1e-5
LANES = 128
SUBLANES = 8
NEG_BIG = -1e30

ROW_TILE = 512
ATT_TILE = 512
V_ROWS = HEAD_DIM + 16
LRU_TILE = 512
LRU_CHUNK = 128
MOE_TILE = 256
VMEM_LIMIT = 48 * 1024 * 1024


def _cparams(sem, vmem=VMEM_LIMIT, **kw):
    return pltpu.CompilerParams(dimension_semantics=sem, vmem_limit_bytes=vmem, **kw)


def _inproj_kernel(x_ref, nw_ref, w_ref, qkv_ref, xg_ref, *, att_w):
    x = x_ref[...]
    ms = jnp.mean(x * x, axis=-1, keepdims=True)
    hn = (x * lax.rsqrt(ms + NORM_EPS) * nw_ref[...]).astype(BF16)
    p = jnp.dot(hn, w_ref[...], preferred_element_type=F32)
    scale = QK_DIM ** -0.5
    qkv_ref[:, :att_w] = (p[:, :att_w] * scale).astype(BF16)
    qkv_ref[:, att_w:] = p[:, att_w:3 * att_w].astype(BF16)
    xg_ref[...] = p[:, 3 * att_w:]


def _in_proj(xf, norm_w, w_in_bf, att_w):
    t, d = xf.shape
    n = w_in_bf.shape[1]
    tm = min(ROW_TILE, t)
    return pl.pallas_call(
        functools.partial(_inproj_kernel, att_w=att_w),
        grid=(t // tm,),
        in_specs=[pl.BlockSpec((tm, d), lambda i: (i, 0)),
                  pl.BlockSpec((1, d), lambda i: (0, 0)),
                  pl.BlockSpec((d, n), lambda i: (0, 0))],
        out_specs=[pl.BlockSpec((tm, 3 * att_w), lambda i: (i, 0)),
                   pl.BlockSpec((tm, n - 3 * att_w), lambda i: (i, 0))],
        out_shape=[jax.ShapeDtypeStruct((t, 3 * att_w), BF16),
                   jax.ShapeDtypeStruct((t, n - 3 * att_w), F32)],
        compiler_params=_cparams(("parallel",)),
        name="in_proj",
    )(xf, norm_w.reshape(1, d), w_in_bf)


def _attn_kernel(slope_ref, lam_ref, hw_ref, q_ref, k_ref, v_ref, o_ref,
                 k1a, k2a, vt, s1, m1, a1, s2, m2, a2, *, tq, s_len, lambda_init):
    h = pl.program_id(1)
    qi = pl.program_id(2)
    slope = slope_ref[h]
    lane = lax.broadcasted_iota(jnp.int32, (tq, HEAD_DIM), 1)

    @pl.when(qi == 0)
    def _():
        ones_row = jnp.where(lax.broadcasted_iota(jnp.int32, (V_ROWS - HEAD_DIM, tq), 0) == 0, 1.0, 0.0)
        for c in range(s_len // tq):
            rows = slice(c * tq, (c + 1) * tq)
            kk = k_ref[0, rows, :].astype(F32)
            j = c * tq + lax.broadcasted_iota(jnp.int32, (tq, HEAD_DIM), 0)
            j_lo = (j & 255).astype(F32)
            j_hi = (j - (j & 255)).astype(F32)
            aug = jnp.where(lane == QK_DIM, slope * j_hi, jnp.where(lane == QK_DIM + 1, slope * j_lo, 0.0))
            k1a[rows, :] = jnp.where(lane < QK_DIM, kk, aug).astype(BF16)
            k2a[rows, :] = jnp.where(lane < QK_DIM, pltpu.roll(kk, QK_DIM, axis=1), aug).astype(BF16)
            vt[c, :HEAD_DIM, :] = v_ref[0, rows, :].astype(F32).T.astype(BF16)
            vt[c, HEAD_DIM:, :] = ones_row.astype(BF16)

    ones_cols = jnp.where((lane == QK_DIM) | (lane == QK_DIM + 1), 1.0, 0.0)
    q = q_ref[0].astype(F32)
    q1t = jnp.where(lane < QK_DIM, q, ones_cols).T.astype(BF16)
    q2t = jnp.where(lane < QK_DIM, pltpu.roll(q, QK_DIM, axis=1), ones_cols).T.astype(BF16)

    for m, a in ((m1, a1), (m2, a2)):
        m[...] = jnp.full(m.shape, NEG_BIG, F32)
        a[...] = jnp.zeros(a.shape, F32)

    maps = ((q1t, k1a, s1, m1, a1), (q2t, k2a, s2, m2, a2))

    def scores(c, slot):
        rows = pl.ds(pl.multiple_of(c * tq, tq), tq)
        for qt, ka, sb, _, _ in maps:
            sb[slot] = jnp.dot(ka[rows, :], qt, preferred_element_type=F32)

    def softmax_pv(c, slot):
        for _, _, sb, m, a in maps:
            s = sb[slot]
            m_prev = m[...]
            m_new = jnp.maximum(m_prev, jnp.max(s, axis=0, keepdims=True))
            p = jnp.exp(s - m_new).astype(BF16)
            a[...] = jnp.exp(m_prev - m_new) * a[...] + jnp.dot(vt[c], p, preferred_element_type=F32)
            m[...] = m_new

    def softmax_pv_diagonal(c, slot):
        hq = tq // 2
        keep_t = (lax.broadcasted_iota(jnp.int32, (hq, tq), 0) <= lax.broadcasted_iota(jnp.int32, (hq, tq), 1))
        keep_b = (lax.broadcasted_iota(jnp.int32, (hq, hq), 0) <= lax.broadcasted_iota(jnp.int32, (hq, hq), 1))
        for _, _, sb, m, a in maps:
            top = jnp.where(keep_t, sb[slot, :hq, :], NEG_BIG)
            bot = jnp.where(keep_b, sb[slot, hq:, hq:], NEG_BIG)
            mt = jnp.max(top, axis=0, keepdims=True)
            mb = jnp.max(bot, axis=0, keepdims=True)
            m_prev = m[...]
            m_new = jnp.maximum(m_prev, jnp.concatenate([mt[:, :hq], jnp.maximum(mt[:, hq:], mb)], axis=1))
            p_top = jnp.exp(top - m_new).astype(BF16)
            p_bot = jnp.exp(bot - m_new[:, hq:]).astype(BF16)
            a[...] = (jnp.exp(m_prev - m_new) * a[...]
                      + jnp.dot(vt[c, :, :hq], p_top, preferred_element_type=F32))
            a[:, hq:] += jnp.dot(vt[c, :, hq:], p_bot, preferred_element_type=F32)
            m[...] = m_new

    scores(0, 0)

    def body(j, c):
        scores(2 * j + 1, 1)
        softmax_pv(2 * j, 0)
        scores(2 * j + 2, 0)
        softmax_pv(2 * j + 1, 1)
        return c

    lax.fori_loop(0, qi // 2, body, 0)

    @pl.when(qi % 2 == 0)
    def _():
        softmax_pv_diagonal(qi, 0)

    @pl.when(qi % 2 == 1)
    def _():
        scores(qi, 1)
        softmax_pv(qi - 1, 0)
        softmax_pv_diagonal(qi, 1)

    lam = (jnp.exp(jnp.sum(lam_ref[0:1, :] * lam_ref[1:2, :], axis=1, keepdims=True))
           - jnp.exp(jnp.sum(lam_ref[2:3, :] * lam_ref[3:4, :], axis=1, keepdims=True))
           + lambda_init)
    o1 = a1[:HEAD_DIM, :] / a1[HEAD_DIM:HEAD_DIM + 1, :]
    o2 = a2[:HEAD_DIM, :] / a2[HEAD_DIM:HEAD_DIM + 1, :]
    o = (o1 - lam * o2).T
    o = o * lax.rsqrt(jnp.mean(o * o, axis=-1, keepdims=True) + HEAD_NORM_EPS)
    o_ref[0] = (o * hw_ref[...] * (1.0 - lambda_init)).astype(o_ref.dtype)


def _diff_attention(qkv, lam_params, head_norm_w, lambda_init):
    b, s, w3 = qkv.shape
    nh = N_ATT_HEADS
    tq = min(ATT_TILE, s)
    slopes = jnp.asarray(np.array([2.0 ** (-8.0 * (i + 1) / nh) for i in range(nh)], dtype=np.float32))
    return pl.pallas_call(
        functools.partial(_attn_kernel, tq=tq, s_len=s, lambda_init=lambda_init),
        grid=(b, nh, s // tq),
        in_specs=[pl.BlockSpec(memory_space=pltpu.SMEM),
                  pl.BlockSpec((4, QK_DIM), lambda bi, hi, qi: (0, 0)),
                  pl.BlockSpec((1, HEAD_DIM), lambda bi, hi, qi: (0, 0)),
                  pl.BlockSpec((1, tq, HEAD_DIM), lambda bi, hi, qi: (bi, qi, hi)),
                  pl.BlockSpec((1, s, HEAD_DIM), lambda bi, hi, qi: (bi, 0, nh + hi)),
                  pl.BlockSpec((1, s, HEAD_DIM), lambda bi, hi, qi: (bi, 0, 2 * nh + hi))],
        out_specs=pl.BlockSpec((1, tq, HEAD_DIM), lambda bi, hi, qi: (bi, qi, hi)),
        out_shape=jax.ShapeDtypeStruct((b, s, nh * HEAD_DIM), BF16),
        scratch_shapes=[pltpu.VMEM((s, HEAD_DIM), BF16), pltpu.VMEM((s, HEAD_DIM), BF16),
                        pltpu.VMEM((s // tq, V_ROWS, tq), BF16),
                        pltpu.VMEM((2, tq, tq), F32), pltpu.VMEM((1, tq), F32), pltpu.VMEM((V_ROWS, tq), F32),
                        pltpu.VMEM((2, tq, tq), F32), pltpu.VMEM((1, tq), F32), pltpu.VMEM((V_ROWS, tq), F32)],
        compiler_params=_cparams(("parallel", "parallel", "arbitrary")),
        name="diff_attn",
    )(slopes, lam_params, head_norm_w.reshape(1, HEAD_DIM), qkv, qkv, qkv)


def _gelu_tanh(x):
    return 0.5 * x * (1.0 + jnp.tanh(math.sqrt(2.0 / math.pi) * (x + 0.044715 * (x * x * x))))


def _rglru_kernel(xr_ref, gr_ref, cw_ref, cb_ref, w_ref, b_ref, lam_ref, o_ref,
                  xs, carry_h, a_s, u_s, *, ts, ch, c_w):
    si = pl.program_id(1)

    @pl.when(si == 0)
    def _():
        xs[0:8, :] = jnp.zeros((8, c_w), F32)
        carry_h[...] = jnp.zeros(carry_h.shape, F32)

    xs[8:, :] = xr_ref[0]
    neg_lam = -lam_ref[...]
    sp = jnp.maximum(neg_lam, 0.0) + jnp.log1p(jnp.exp(-jnp.abs(neg_lam)))
    cw = cw_ref[...]
    cb = cb_ref[...]
    bias = b_ref[...]
    r8 = lax.broadcasted_iota(jnp.int32, (ch // SUBLANES, SUBLANES, c_w), 1)

    def chunk(c, carry):
        r0 = pl.multiple_of(c * ch, ch)
        win = xs[pl.ds(r0, ch + 8), :]
        xc = cw[3:4, :] * win[8:] + cb
        for k in (1, 2, 3):
            xc = xc + cw[3 - k:4 - k, :] * pltpu.roll(win, k, axis=0)[8:]
        z = jnp.dot(xc.astype(BF16), w_ref[...], preferred_element_type=F32) + bias
        r = jax.nn.sigmoid(z[:, :c_w])
        ig = jax.nn.sigmoid(z[:, c_w:])
        log_a = (-LRU_C) * r * sp
        a = jnp.exp(log_a)
        w = jnp.tanh(-log_a) * (1.0 + a * a)
        u = jnp.where(w > 0.0, w * lax.rsqrt(w), 0.0) * ig * xc
        a = a.reshape(ch // SUBLANES, SUBLANES, c_w)
        u = u.reshape(ch // SUBLANES, SUBLANES, c_w)
        for k in (1, 2, 4):
            a_sh = pltpu.roll(a, k, axis=1)
            u_sh = pltpu.roll(u, k, axis=1)
            ok = r8 >= k
            u = jnp.where(ok, u + a * u_sh, u)
            a = jnp.where(ok, a * a_sh, a)
        a_s[pl.ds(r0, ch), :] = a.reshape(ch, c_w)
        u_s[pl.ds(r0, ch), :] = u.reshape(ch, c_w)
        return carry

    lax.fori_loop(0, ts // ch, chunk, 0)

    def grp(g, hprev):
        r0 = pl.multiple_of(g * 8, 8)
        hg = u_s[pl.ds(r0, 8), :] + a_s[pl.ds(r0, 8), :] * hprev
        u_s[pl.ds(r0, 8), :] = hg
        return hg[7:8, :]

    hlast = lax.fori_loop(0, ts // 8, grp, carry_h[0:1, :], unroll=8)
    carry_h[0:1, :] = hlast
    xs[0:8, :] = xs[ts:ts + 8, :]
    o_ref[0] = (u_s[...] * _gelu_tanh(gr_ref[0])).astype(o_ref.dtype)


def _rglru(xg, conv_w, conv_b, w_bd, b_cat, lru_lambda):
    b, s, w2 = xg.shape
    c_w = w2 // 2
    ts = min(LRU_TILE, s)
    ch = min(LRU_CHUNK, ts)
    return pl.pallas_call(
        functools.partial(_rglru_kernel, ts=ts, ch=ch, c_w=c_w),
        grid=(b, s // ts),
        in_specs=[pl.BlockSpec((1, ts, c_w), lambda bi, si: (bi, si, 0)),
                  pl.BlockSpec((1, ts, c_w), lambda bi, si: (bi, si, 1)),
                  pl.BlockSpec((CONV_WIDTH, c_w), lambda bi, si: (0, 0)),
                  pl.BlockSpec((1, c_w), lambda bi, si: (0, 0)),
                  pl.BlockSpec((c_w, 2 * c_w), lambda bi, si: (0, 0)),
                  pl.BlockSpec((1, 2 * c_w), lambda bi, si: (0, 0)),
                  pl.BlockSpec((1, c_w), lambda bi, si: (0, 0))],
        out_specs=pl.BlockSpec((1, ts, c_w), lambda bi, si: (bi, si, 0)),
        out_shape=jax.ShapeDtypeStruct((b, s, c_w), BF16),
        scratch_shapes=[pltpu.VMEM((ts + 8, c_w), F32), pltpu.VMEM((8, c_w), F32),
                        pltpu.VMEM((ts, c_w), F32), pltpu.VMEM((ts, c_w), F32)],
        compiler_params=_cparams(("parallel", "arbitrary")),
        name="rglru",
    )(xg, xg, conv_w, conv_b.reshape(1, c_w), w_bd, b_cat.reshape(1, 2 * c_w), lru_lambda.reshape(1, c_w))


def _outproj_kernel(att_ref, rnn_ref, x_ref, wo_ref, nw_ref, wr_ref, br_ref,
                    x1_ref, hn_ref, route_ref, n8_ref, *, att_w):
    y = jnp.dot(att_ref[...], wo_ref[:att_w, :], preferred_element_type=F32)
    y = y + jnp.dot(rnn_ref[...], wo_ref[att_w:, :], preferred_element_type=F32)
    x1 = x_ref[...] + y
    x1_ref[...] = x1
    hn = (x1 * lax.rsqrt(jnp.mean(x1 * x1, axis=-1, keepdims=True) + NORM_EPS) * nw_ref[...]).astype(BF16)
    hn_ref[...] = hn
    lg = jnp.dot(hn, wr_ref[...], preferred_element_type=F32) + br_ref[...]
    tm = lg.shape[0]

    col = lax.broadcasted_iota(jnp.int32, lg.shape, 1)
    colf = col.astype(F32)
    big = float(LANES)
    ninf = -jnp.inf
    is_g = col < N_GROUPS
    lgm = jnp.where(is_g, lg, ninf)
    mg = jnp.max(lgm, axis=1, keepdims=True)
    g_sel = jnp.min(jnp.where(lgm == mg, colf, big), axis=1, keepdims=True)
    pg = 1.0 / jnp.sum(jnp.where(is_g, jnp.exp(lgm - mg), 0.0), axis=1, keepdims=True)
    lo = N_GROUPS + EXPERTS_PER_GROUP * g_sel
    in_grp = (colf >= lo) & (colf < lo + EXPERTS_PER_GROUP)
    lem = jnp.where(in_grp, lg, ninf)
    v1 = jnp.max(lem, axis=1, keepdims=True)
    i1 = jnp.min(jnp.where(lem == v1, colf, big), axis=1, keepdims=True)
    lem2 = jnp.where(colf == i1, ninf, lem)
    v2 = jnp.max(lem2, axis=1, keepdims=True)
    i2 = jnp.min(jnp.where(lem2 == v2, colf, big), axis=1, keepdims=True)
    e2 = jnp.exp(v2 - v1)
    den = 1.0 + e2
    g1 = pg / den
    g2 = pg * e2 / den

    oh1 = jnp.where(colf == i1, 1.0, 0.0)
    oh2 = jnp.where(colf == i2, 1.0, 0.0)
    oh = oh1 + oh2
    earlier = (lax.broadcasted_iota(jnp.int32, (tm, tm), 1)
               < lax.broadcasted_iota(jnp.int32, (tm, tm), 0)).astype(BF16)
    pref = jnp.dot(earlier, oh.astype(BF16), preferred_element_type=F32)
    cnt = jnp.sum(oh, axis=0, keepdims=True)
    n8 = jnp.floor((cnt + (SUBLANES - 1)) * (1.0 / SUBLANES))
    before = (lax.broadcasted_iota(jnp.int32, (LANES, LANES), 0)
              < lax.broadcasted_iota(jnp.int32, (LANES, LANES), 1)).astype(BF16)
    loff8 = jnp.dot(jnp.broadcast_to(n8, (SUBLANES, LANES)).astype(BF16), before,
                    preferred_element_type=F32)[0:1]
    pos = SUBLANES * loff8 + pref
    lp1 = jnp.sum(oh1 * pos, axis=1, keepdims=True)
    lp2 = jnp.sum(oh2 * pos, axis=1, keepdims=True)
    route_ref[...] = jnp.where(col == 0, g1,
                     jnp.where(col == 1, g2,
                     jnp.where(col == 2, lp1,
                     jnp.where(col == 3, lp2, 0.0))))
    n8_ref[0] = n8


def _out_proj(att, rnn, xf, w_out_bf, norm_w, w_route_bf, b_route):
    t, d = xf.shape
    att_w = att.shape[1]
    tm = min(ROW_TILE, t)
    row = lambda i: (i, 0)
    fix = lambda i: (0, 0)
    return pl.pallas_call(
        functools.partial(_outproj_kernel, att_w=att_w),
        grid=(t // tm,),
        in_specs=[pl.BlockSpec((tm, att_w), row), pl.BlockSpec((tm, rnn.shape[1]), row),
                  pl.BlockSpec((tm, d), row), pl.BlockSpec(w_out_bf.shape, fix),
                  pl.BlockSpec((1, d), fix), pl.BlockSpec((d, LANES), fix), pl.BlockSpec((1, LANES), fix)],
        out_specs=[pl.BlockSpec((tm, d), row), pl.BlockSpec((tm, d), row), pl.BlockSpec((tm, LANES), row),
                   pl.BlockSpec((1, 1, LANES), lambda i: (i, 0, 0))],
        out_shape=[jax.ShapeDtypeStruct((t, d), F32), jax.ShapeDtypeStruct((t, d), BF16),
                   jax.ShapeDtypeStruct((t, LANES), F32),
                   jax.ShapeDtypeStruct((t // tm, 1, LANES), F32)],
        compiler_params=_cparams(("parallel",)),
        name="out_proj",
    )(att, rnn, xf, w_out_bf, norm_w.reshape(1, d), w_route_bf, b_route)


def _local_rows(tm):
    return -(-(TOP_K * tm + N_EXPERTS * (SUBLANES - 1)) // LANES) * LANES


def _segment_tables(n8_tiles, tm_moe, n_tiles):
    n8 = n8_tiles[:, 0, N_GROUPS:N_GROUPS + N_EXPERTS].astype(jnp.int32)
    c8 = n8 * SUBLANES
    loff = jnp.cumsum(c8, axis=1) - c8
    gtot = jnp.sum(c8, axis=0)
    gpad = (gtot + tm_moe - 1) // tm_moe * tm_moe
    gend = jnp.cumsum(gpad)
    gstart = gend - gpad
    gbase = gstart[None, :] + jnp.cumsum(c8, axis=0) - c8
    tile_row0 = jnp.arange(n_tiles, dtype=jnp.int32) * tm_moe
    tile_e = jnp.minimum(jnp.sum((gend[None, :] <= tile_row0[:, None]).astype(jnp.int32), axis=1),
                         N_EXPERTS - 1).astype(jnp.int32)
    n_used = (gend[-1] // tm_moe).astype(jnp.int32).reshape(1)
    tail_start = (gstart + gtot).astype(jnp.int32)
    tail_n8 = ((gpad - gtot) // SUBLANES).astype(jnp.int32)
    after = gend[tile_e] // tm_moe
    next_e = jnp.where(after < n_used[0], tile_e[jnp.minimum(after, n_tiles - 1)], -1).astype(jnp.int32)
    first = jnp.concatenate([jnp.ones((1,), jnp.int32), (tile_e[1:] != tile_e[:-1]).astype(jnp.int32)])
    w_slot = ((jnp.cumsum(first) - 1) % 2).astype(jnp.int32)
    return (n8.reshape(-1), loff.reshape(-1).astype(jnp.int32), gbase.reshape(-1).astype(jnp.int32),
            tile_e, n_used, tail_start, tail_n8, next_e, w_slot)


def _segment_copies(n8_ref, src_off_ref, dst_off_ref, src, dst, sem, tile, wait):
    def rows_of(e):
        return pl.multiple_of(n8_ref[tile * N_EXPERTS + e] * SUBLANES, SUBLANES)

    if wait:
        total = lax.fori_loop(0, N_EXPERTS, lambda e, acc: acc + rows_of(e), 0)
        total = pl.multiple_of(total, SUBLANES)
        pltpu.make_async_copy(src.at[pl.ds(0, total), :], dst.at[pl.ds(0, total), :], sem).wait()
        return

    def per_expert(e, c):
        k = tile * N_EXPERTS + e
        rows = rows_of(e)

        @pl.when(rows > 0)
        def _():
            pltpu.make_async_copy(
                src.at[pl.ds(pl.multiple_of(src_off_ref[k], SUBLANES), rows), :],
                dst.at[pl.ds(pl.multiple_of(dst_off_ref[k], SUBLANES), rows), :], sem).start()
        return c
    lax.fori_loop(0, N_EXPERTS, per_expert, 0)


def _dispatch_kernel(n8_ref, loff_ref, gbase_ref, tstart_ref, tn8_ref, nu_ref, hn_ref, route_ref, xs_hbm,
                     stage, zbuf, sem, zsem, *, lcap, n_tt):
    i = pl.program_id(0)
    slot = i % 2
    tm = hn_ref.shape[0]
    tm_moe = zbuf.shape[0]
    n_tiles = xs_hbm.shape[0] // tm_moe

    def tail_copies(wait):
        def go(cp):
            if wait:
                cp.wait()
            else:
                cp.start()

        def per_expert(e, c):
            rows = pl.multiple_of(tn8_ref[e] * SUBLANES, SUBLANES)

            @pl.when(rows > 0)
            def _():
                go(pltpu.make_async_copy(
                    zbuf.at[pl.ds(0, rows), :],
                    xs_hbm.at[pl.ds(pl.multiple_of(tstart_ref[e], SUBLANES), rows), :], zsem.at[0]))
            return c
        lax.fori_loop(0, N_EXPERTS, per_expert, 0)

        def per_unused_tile(j, c):
            go(pltpu.make_async_copy(zbuf, xs_hbm.at[pl.ds(pl.multiple_of(j * tm_moe, tm_moe), tm_moe), :],
                                     zsem.at[0]))
            return c
        lax.fori_loop(nu_ref[0], n_tiles, per_unused_tile, 0)

    @pl.when(i == 0)
    def _():
        zbuf[...] = jnp.zeros(zbuf.shape, F32)
        tail_copies(False)

    @pl.when(i >= 2)
    def _():
        _segment_copies(n8_ref, loff_ref, gbase_ref, stage.at[slot], xs_hbm, sem.at[slot], i - 2, True)

    lp1 = route_ref[:, 2:3]
    lp2 = route_ref[:, 3:4]
    cpos = lax.broadcasted_iota(jnp.int32, (tm, lcap), 1).astype(F32)
    sel_t = jnp.where((cpos == lp1) | (cpos == lp2), 1.0, 0.0)
    stage[slot] = jnp.dot(sel_t.T.astype(BF16), hn_ref[...], preferred_element_type=F32)
    _segment_copies(n8_ref, loff_ref, gbase_ref, stage.at[slot], xs_hbm, sem.at[slot], i, False)

    @pl.when(i == n_tt - 1)
    def _():
        _segment_copies(n8_ref, loff_ref, gbase_ref, stage.at[slot], xs_hbm, sem.at[slot], i, True)
        if n_tt > 1:
            _segment_copies(n8_ref, loff_ref, gbase_ref, stage.at[1 - slot], xs_hbm, sem.at[1 - slot],
                            i - 1, True)
        tail_copies(True)


def _dispatch(hn, route, tables, n_rows, tm_moe):
    t, d = hn.shape
    tm = min(ROW_TILE, t)
    n_tt = t // tm
    lcap = _local_rows(tm)
    n8, loff, gbase, _, n_used, tail_start, tail_n8 = tables[:7]
    grid_spec = pltpu.PrefetchScalarGridSpec(
        num_scalar_prefetch=6,
        grid=(n_tt,),
        in_specs=[pl.BlockSpec((tm, d), lambda i, *_: (i, 0)),
                  pl.BlockSpec((tm, LANES), lambda i, *_: (i, 0))],
        out_specs=pl.BlockSpec(memory_space=pl.ANY),
        scratch_shapes=[pltpu.VMEM((2, lcap, d), F32), pltpu.VMEM((tm_moe, d), F32),
                        pltpu.SemaphoreType.DMA((2,)), pltpu.SemaphoreType.DMA((1,))],
    )
    return pl.pallas_call(
        functools.partial(_dispatch_kernel, lcap=lcap, n_tt=n_tt),
        grid_spec=grid_spec,
        out_shape=jax.ShapeDtypeStruct((n_rows, d), F32),
        compiler_params=_cparams(("arbitrary",), has_side_effects=True),
        name="dispatch",
    )(n8, loff, gbase, tail_start, tail_n8, n_used, hn, route)


def _moe_kernel(te_ref, nu_ref, nxt_ref, wslot_ref, xs_ref, wg_hbm, wu_hbm, wd_hbm, y_ref,
                wgf, wuf, wdf, wgb, wub, wdb, wsem):
    i = pl.program_id(0)

    def weight_copies(e, sl):
        return (pltpu.make_async_copy(wg_hbm.at[e], wgf.at[sl], wsem.at[sl, 0]),
                pltpu.make_async_copy(wu_hbm.at[e], wuf.at[sl], wsem.at[sl, 1]),
                pltpu.make_async_copy(wd_hbm.at[e], wdf.at[sl], wsem.at[sl, 2]))

    @pl.when(i == 0)
    def _():
        for cp in weight_copies(te_ref[0], wslot_ref[0]):
            cp.start()

    @pl.when(i < nu_ref[0])
    def _():
        changed = jnp.logical_or(i == 0, te_ref[i] != te_ref[jnp.maximum(i - 1, 0)])

        @pl.when(changed)
        def _():
            sl = wslot_ref[i]
            for cp in weight_copies(te_ref[i], sl):
                cp.wait()
            wgb[...] = wgf[sl].astype(BF16)
            wub[...] = wuf[sl].astype(BF16)
            wdb[...] = wdf[sl].astype(BF16)

            @pl.when(nxt_ref[i] >= 0)
            def _():
                for cp in weight_copies(nxt_ref[i], 1 - sl):
                    cp.start()

        x = xs_ref[...].astype(BF16)
        g = jnp.dot(x, wgb[...], preferred_element_type=F32)
        u = jnp.dot(x, wub[...], preferred_element_type=F32)
        hdn = (g * jax.nn.sigmoid(g) * u).astype(BF16)
        y_ref[...] = jnp.dot(hdn, wdb[...], preferred_element_type=F32)

    @pl.when(i >= nu_ref[0])
    def _():
        y_ref[...] = jnp.zeros(y_ref.shape, F32)


def _moe(xs, tile_e, n_used, next_e, w_slot, w_g, w_u, w_d, tm):
    n_rows, d = xs.shape
    n_tiles = n_rows // tm
    ff = w_g.shape[2]
    row_blk = lambda i, te, nu, *_: (jnp.minimum(i, nu[0] - 1), 0)
    hbm = pl.BlockSpec(memory_space=pl.ANY)
    grid_spec = pltpu.PrefetchScalarGridSpec(
        num_scalar_prefetch=4,
        grid=(n_tiles,),
        in_specs=[pl.BlockSpec((tm, d), row_blk), hbm, hbm, hbm],
        out_specs=pl.BlockSpec((tm, d), lambda i, *_: (i, 0)),
        scratch_shapes=[pltpu.VMEM((2, d, ff), F32), pltpu.VMEM((2, d, ff), F32), pltpu.VMEM((2, ff, d), F32),
                        pltpu.VMEM((d, ff), BF16), pltpu.VMEM((d, ff), BF16), pltpu.VMEM((ff, d), BF16),
                        pltpu.SemaphoreType.DMA((2, 3))],
    )
    return pl.pallas_call(
        _moe_kernel,
        grid_spec=grid_spec,
        out_shape=jax.ShapeDtypeStruct((n_rows, d), F32),
        compiler_params=_cparams(("arbitrary",)),
        name="moe",
    )(tile_e, n_used, next_e, w_slot, xs, w_g, w_u, w_d)


def _combine_kernel(n8_ref, loff_ref, gbase_ref, x1_ref, route_ref, nw_ref, y_hbm, o_ref,
                    ybuf, sem, *, lcap, n_tt):
    i = pl.program_id(0)
    slot = i % 2
    tm = x1_ref.shape[0]

    def fetch(tile, sl, wait):
        _segment_copies(n8_ref, gbase_ref, loff_ref, y_hbm, ybuf.at[sl], sem.at[sl], tile, wait)

    @pl.when(i == 0)
    def _():
        ybuf[...] = jnp.zeros(ybuf.shape, F32)
        fetch(0, 0, False)

    @pl.when(i + 1 < n_tt)
    def _():
        fetch(i + 1, 1 - slot, False)

    fetch(i, slot, True)
    g1 = route_ref[:, 0:1]
    g2 = route_ref[:, 1:2]
    lp1 = route_ref[:, 2:3]
    lp2 = route_ref[:, 3:4]
    cpos = lax.broadcasted_iota(jnp.int32, (tm, lcap), 1).astype(F32)
    gsel = (jnp.where(cpos == lp1, g1, 0.0) + jnp.where(cpos == lp2, g2, 0.0)).astype(BF16)
    moe = jnp.dot(gsel, ybuf[slot].astype(BF16), preferred_element_type=F32)
    x = x1_ref[...] + moe
    o_ref[...] = x * lax.rsqrt(jnp.mean(x * x, axis=-1, keepdims=True) + NORM_EPS) * nw_ref[...]


def _combine(x1, y, route, norm_w, tables):
    t, d = x1.shape
    tm = min(ROW_TILE, t)
    n_tt = t // tm
    lcap = _local_rows(tm)
    n8, loff, gbase = tables[:3]
    grid_spec = pltpu.PrefetchScalarGridSpec(
        num_scalar_prefetch=3,
        grid=(n_tt,),
        in_specs=[pl.BlockSpec((tm, d), lambda i, *_: (i, 0)),
                  pl.BlockSpec((tm, LANES), lambda i, *_: (i, 0)),
                  pl.BlockSpec((1, d), lambda i, *_: (0, 0)),
                  pl.BlockSpec(memory_space=pl.ANY)],
        out_specs=pl.BlockSpec((tm, d), lambda i, *_: (i, 0)),
        scratch_shapes=[pltpu.VMEM((2, lcap, d), F32), pltpu.SemaphoreType.DMA((2,))],
    )
    return pl.pallas_call(
        functools.partial(_combine_kernel, lcap=lcap, n_tt=n_tt),
        grid_spec=grid_spec,
        out_shape=jax.ShapeDtypeStruct((t, d), F32),
        compiler_params=_cparams(("arbitrary",)),
        name="combine",
    )(n8, loff, gbase, x1, route, norm_w.reshape(1, d), y)


def _block_diag(w):
    n, bi, bj = w.shape
    eye = jnp.eye(n, dtype=w.dtype)
    return jnp.einsum('nij,nm->nimj', w, eye).reshape(n * bi, n * bj)


def kernel(x, mix_norm_w, w_in, lambda_q1, lambda_k1, lambda_q2, lambda_k2, head_norm_w, conv_w, conv_b, w_rgate, b_rgate, w_igate, b_igate, lru_lambda, w_out, ffn_norm_w, w_router_group, b_router_group, w_router_expert, b_router_expert, w_exp_gate, w_exp_up, w_exp_down, final_norm_w):
    b, s, d = x.shape
    t = b * s
    assert w_in.shape[0] == 1, "single-layer stack only"
    att_w = N_ATT_HEADS * HEAD_DIM
    tm_moe = MOE_TILE
    xf = x.reshape(t, d)
    for l in range(1):
        lambda_init = 0.8 - 0.6 * math.exp(-0.3 * l)
        qkv, xg = _in_proj(xf, mix_norm_w[l], w_in[l].astype(BF16), att_w)
        lam_params = jnp.stack([lambda_q1[l], lambda_k1[l], lambda_q2[l], lambda_k2[l]]).astype(F32)
        att = _diff_attention(qkv.reshape(b, s, 3 * att_w), lam_params, head_norm_w[l], lambda_init)
        w_bd = jnp.concatenate([_block_diag(w_rgate[l]), _block_diag(w_igate[l])], axis=1).astype(BF16)
        b_cat = jnp.concatenate([b_rgate[l], b_igate[l]])
        rnn = _rglru(xg.reshape(b, s, xg.shape[1]), conv_w[l], conv_b[l], w_bd, b_cat, lru_lambda[l])
        w_route = jnp.concatenate([w_router_group[l], w_router_expert[l]], axis=1)
        w_route = jnp.pad(w_route, ((0, 0), (0, LANES - w_route.shape[1]))).astype(BF16)
        b_route = jnp.concatenate([b_router_group[l], b_router_expert[l]])
        b_route = jnp.pad(b_route, (0, LANES - b_route.shape[0])).reshape(1, LANES).astype(F32)
        x1, hn, route, n8_tiles = _out_proj(att.reshape(t, att_w), rnn.reshape(t, -1), xf, w_out[l].astype(BF16),
                                            ffn_norm_w[l], w_route, b_route)
        n_tt = n8_tiles.shape[0]
        max_rows = TOP_K * t + n_tt * N_EXPERTS * (SUBLANES - 1) + N_EXPERTS * (tm_moe - 1)
        n_tiles = -(-max_rows // tm_moe)
        tables = _segment_tables(n8_tiles, tm_moe, n_tiles)
        xs = _dispatch(hn, route, tables, n_tiles * tm_moe, tm_moe)
        y = _moe(xs, tables[3], tables[4], tables[7], tables[8], w_exp_gate[l], w_exp_up[l], w_exp_down[l], tm_moe)
        out = _combine(x1, y, route, final_norm_w, tables)
    return out.reshape(b, s, d)
```

```python
import functools
import math

import numpy as np
import jax
import jax.numpy as jnp
from jax import lax
from jax.experimental import pallas as pl
from jax.experimental.pallas import tpu as pltpu

F32 = jnp.float32
BF16 = jnp.bfloat16

N_ATT_HEADS = 4
HEAD_DIM = 128
QK_DIM = 64
N_RNN_BLOCKS = 8
CONV_WIDTH = 4
LRU_C = 8.0
N_GROUPS = 4
EXPERTS_PER_GROUP = 8
N_EXPERTS = N_GROUPS * EXPERTS_PER_GROUP
TOP_K = 2
NORM_EPS = 1e-6
HEAD_NORM_EPS = 1e-5
LANES = 128
SUBLANES = 8
NEG_BIG = -1e30

ROW_TILE = 512
ATT_TILE = 512
V_ROWS = HEAD_DIM + 16
LRU_TILE = 512
LRU_CHUNK = 128
MOE_TILE = 256
VMEM_LIMIT = 48 * 1024 * 1024


def _cparams(sem, vmem=VMEM_LIMIT, **kw):
    return pltpu.CompilerParams(dimension_semantics=sem, vmem_limit_bytes=vmem, **kw)


def _inproj_kernel(x_ref, nw_ref, w_ref, qkv_ref, xg_ref, *, att_w):
    x = x_ref[...]
    ms = jnp.mean(x * x, axis=-1, keepdims=True)
    hn = (x * lax.rsqrt(ms + NORM_EPS) * nw_ref[...]).astype(BF16)
    p = jnp.dot(hn, w_ref[...], preferred_element_type=F32)
    scale = QK_DIM ** -0.5
    qkv_ref[:, :att_w] = (p[:, :att_w] * scale).astype(BF16)
    qkv_ref[:, att_w:] = p[:, att_w:3 * att_w].astype(BF16)
    xg_ref[...] = p[:, 3 * att_w:]


def _in_proj(xf, norm_w, w_in_bf, att_w):
    t, d = xf.shape
    n = w_in_bf.shape[1]
    tm = min(ROW_TILE, t)
    return pl.pallas_call(
        functools.partial(_inproj_kernel, att_w=att_w),
        grid=(t // tm,),
        in_specs=[pl.BlockSpec((tm, d), lambda i: (i, 0)),
                  pl.BlockSpec((1, d), lambda i: (0, 0)),
                  pl.BlockSpec((d, n), lambda i: (0, 0))],
        out_specs=[pl.BlockSpec((tm, 3 * att_w), lambda i: (i, 0)),
                   pl.BlockSpec((tm, n - 3 * att_w), lambda i: (i, 0))],
        out_shape=[jax.ShapeDtypeStruct((t, 3 * att_w), BF16),
                   jax.ShapeDtypeStruct((t, n - 3 * att_w), F32)],
        compiler_params=_cparams(("parallel",)),
        name="in_proj",
    )(xf, norm_w.reshape(1, d), w_in_bf)


def _attn_kernel(slope_ref, lam_ref, hw_ref, q_ref, k_ref, v_ref, o_ref,
                 k1a, k2a, vt, s1, m1, a1, s2, m2, a2, *, tq, s_len, lambda_init):
    h = pl.program_id(1)
    qi = pl.program_id(2)
    slope = slope_ref[h]
    lane = lax.broadcasted_iota(jnp.int32, (tq, HEAD_DIM), 1)

    @pl.when(qi == 0)
    def _():
        ones_row = jnp.where(lax.broadcasted_iota(jnp.int32, (V_ROWS - HEAD_DIM, tq), 0) == 0, 1.0, 0.0)
        for c in range(s_len // tq):
            rows = slice(c * tq, (c + 1) * tq)
            kk = k_ref[0, rows, :].astype(F32)
            j = c * tq + lax.broadcasted_iota(jnp.int32, (tq, HEAD_DIM), 0)
            j_lo = (j & 255).astype(F32)
            j_hi = (j - (j & 255)).astype(F32)
            aug = jnp.where(lane == QK_DIM, slope * j_hi, jnp.where(lane == QK_DIM + 1, slope * j_lo, 0.0))
            k1a[rows, :] = jnp.where(lane < QK_DIM, kk, aug).astype(BF16)
            k2a[rows, :] = jnp.where(lane < QK_DIM, pltpu.roll(kk, QK_DIM, axis=1), aug).astype(BF16)
            vt[c, :HEAD_DIM, :] = v_ref[0, rows, :].astype(F32).T.astype(BF16)
            vt[c, HEAD_DIM:, :] = ones_row.astype(BF16)

    ones_cols = jnp.where((lane == QK_DIM) | (lane == QK_DIM + 1), 1.0, 0.0)
    q = q_ref[0].astype(F32)
    q1t = jnp.where(lane < QK_DIM, q, ones_cols).T.astype(BF16)
    q2t = jnp.where(lane < QK_DIM, pltpu.roll(q, QK_DIM, axis=1), ones_cols).T.astype(BF16)

    for m, a in ((m1, a1), (m2, a2)):
        m[...] = jnp.full(m.shape, NEG_BIG, F32)
        a[...] = jnp.zeros(a.shape, F32)

    maps = ((q1t, k1a, s1, m1, a1), (q2t, k2a, s2, m2, a2))

    def scores(c, slot):
        rows = pl.ds(pl.multiple_of(c * tq, tq), tq)
        for qt, ka, sb, _, _ in maps:
            sb[slot] = jnp.dot(ka[rows, :], qt, preferred_element_type=F32)

    def softmax_pv(c, slot):
        for _, _, sb, m, a in maps:
            s = sb[slot]
            m_prev = m[...]
            m_new = jnp.maximum(m_prev, jnp.max(s, axis=0, keepdims=True))
            p = jnp.exp(s - m_new).astype(BF16)
            a[...] = jnp.exp(m_prev - m_new) * a[...] + jnp.dot(vt[c], p, preferred_element_type=F32)
            m[...] = m_new

    def softmax_pv_diagonal(c, slot):
        hq = tq // 2
        keep_t = (lax.broadcasted_iota(jnp.int32, (hq, tq), 0) <= lax.broadcasted_iota(jnp.int32, (hq, tq), 1))
        keep_b = (lax.broadcasted_iota(jnp.int32, (hq, hq), 0) <= lax.broadcasted_iota(jnp.int32, (hq, hq), 1))
        for _, _, sb, m, a in maps:
            top = jnp.where(keep_t, sb[slot, :hq, :], NEG_BIG)
            bot = jnp.where(keep_b, sb[slot, hq:, hq:], NEG_BIG)
            mt = jnp.max(top, axis=0, keepdims=True)
            mb = jnp.max(bot, axis=0, keepdims=True)
            m_prev = m[...]
            m_new = jnp.maximum(m_prev, jnp.concatenate([mt[:, :hq], jnp.maximum(mt[:, hq:], mb)], axis=1))
            p_top = jnp.exp(top - m_new).astype(BF16)
            p_bot = jnp.exp(bot - m_new[:, hq:]).astype(BF16)
            a[...] = (jnp.exp(m_prev - m_new) * a[...]
                      + jnp.dot(vt[c, :, :hq], p_top, preferred_element_type=F32))
            a[:, hq:] += jnp.dot(vt[c, :, hq:], p_bot, preferred_element_type=F32)
            m[...] = m_new

    scores(0, 0)

    def body(j, c):
        scores(2 * j + 1, 1)
        softmax_pv(2 * j, 0)
        scores(2 * j + 2, 0)
        softmax_pv(2 * j + 1, 1)
        return c

    lax.fori_loop(0, qi // 2, body, 0)

    @pl.when(qi % 2 == 0)
    def _():
        softmax_pv_diagonal(qi, 0)

    @pl.when(qi % 2 == 1)
    def _():
        scores(qi, 1)
        softmax_pv(qi - 1, 0)
        softmax_pv_diagonal(qi, 1)

    lam = (jnp.exp(jnp.sum(lam_ref[0:1, :] * lam_ref[1:2, :], axis=1, keepdims=True))
           - jnp.exp(jnp.sum(lam_ref[2:3, :] * lam_ref[3:4, :], axis=1, keepdims=True))
           + lambda_init)
    o1 = a1[:HEAD_DIM, :] / a1[HEAD_DIM:HEAD_DIM + 1, :]
    o2 = a2[:HEAD_DIM, :] / a2[HEAD_DIM:HEAD_DIM + 1, :]
    o = (o1 - lam * o2).T
    o = o * lax.rsqrt(jnp.mean(o * o, axis=-1, keepdims=True) + HEAD_NORM_EPS)
    o_ref[0] = (o * hw_ref[...] * (1.0 - lambda_init)).astype(o_ref.dtype)


def _diff_attention(qkv, lam_params, head_norm_w, lambda_init):
    b, s, w3 = qkv.shape
    nh = N_ATT_HEADS
    tq = min(ATT_TILE, s)
    slopes = jnp.asarray(np.array([2.0 ** (-8.0 * (i + 1) / nh) for i in range(nh)], dtype=np.float32))
    return pl.pallas_call(
        functools.partial(_attn_kernel, tq=tq, s_len=s, lambda_init=lambda_init),
        grid=(b, nh, s // tq),
        in_specs=[pl.BlockSpec(memory_space=pltpu.SMEM),
                  pl.BlockSpec((4, QK_DIM), lambda bi, hi, qi: (0, 0)),
                  pl.BlockSpec((1, HEAD_DIM), lambda bi, hi, qi: (0, 0)),
                  pl.BlockSpec((1, tq, HEAD_DIM), lambda bi, hi, qi: (bi, qi, hi)),
                  pl.BlockSpec((1, s, HEAD_DIM), lambda bi, hi, qi: (bi, 0, nh + hi)),
                  pl.BlockSpec((1, s, HEAD_DIM), lambda bi, hi, qi: (bi, 0, 2 * nh + hi))],
        out_specs=pl.BlockSpec((1, tq, HEAD_DIM), lambda bi, hi, qi: (bi, qi, hi)),
        out_shape=jax.ShapeDtypeStruct((b, s, nh * HEAD_DIM), BF16),
        scratch_shapes=[pltpu.VMEM((s, HEAD_DIM), BF16), pltpu.VMEM((s, HEAD_DIM), BF16),
                        pltpu.VMEM((s // tq, V_ROWS, tq), BF16),
                        pltpu.VMEM((2, tq, tq), F32), pltpu.VMEM((1, tq), F32), pltpu.VMEM((V_ROWS, tq), F32),
                        pltpu.VMEM((2, tq, tq), F32), pltpu.VMEM((1, tq), F32), pltpu.VMEM((V_ROWS, tq), F32)],
        compiler_params=_cparams(("parallel", "parallel", "arbitrary")),
        name="diff_attn",
    )(slopes, lam_params, head_norm_w.reshape(1, HEAD_DIM), qkv, qkv, qkv)


def _gelu_tanh(x):
    return 0.5 * x * (1.0 + jnp.tanh(math.sqrt(2.0 / math.pi) * (x + 0.044715 * (x * x * x))))


def _rglru_kernel(xr_ref, gr_ref, cw_ref, cb_ref, w_ref, b_ref, lam_ref, o_ref,
                  xs, carry_h, a_s, u_s, *, ts, ch, c_w):
    si = pl.program_id(1)

    @pl.when(si == 0)
    def _():
        xs[0:8, :] = jnp.zeros((8, c_w), F32)
        carry_h[...] = jnp.zeros(carry_h.shape, F32)

    xs[8:, :] = xr_ref[0]
    neg_lam = -lam_ref[...]
    sp = jnp.maximum(neg_lam, 0.0) + jnp.log1p(jnp.exp(-jnp.abs(neg_lam)))
    cw = cw_ref[...]
    cb = cb_ref[...]
    bias = b_ref[...]
    r8 = lax.broadcasted_iota(jnp.int32, (ch // SUBLANES, SUBLANES, c_w), 1)

    def chunk(c, carry):
        r0 = pl.multiple_of(c * ch, ch)
        win = xs[pl.ds(r0, ch + 8), :]
        xc = cw[3:4, :] * win[8:] + cb
        for k in (1, 2, 3):
            xc = xc + cw[3 - k:4 - k, :] * pltpu.roll(win, k, axis=0)[8:]
        z = jnp.dot(xc.astype(BF16), w_ref[...], preferred_element_type=F32) + bias
        r = jax.nn.sigmoid(z[:, :c_w])
        ig = jax.nn.sigmoid(z[:, c_w:])
        log_a = (-LRU_C) * r * sp
        a = jnp.exp(log_a)
        w = jnp.tanh(-log_a) * (1.0 + a * a)
        u = jnp.where(w > 0.0, w * lax.rsqrt(w), 0.0) * ig * xc
        a = a.reshape(ch // SUBLANES, SUBLANES, c_w)
        u = u.reshape(ch // SUBLANES, SUBLANES, c_w)
        for k in (1, 2, 4):
            a_sh = pltpu.roll(a, k, axis=1)
            u_sh = pltpu.roll(u, k, axis=1)
            ok = r8 >= k
            u = jnp.where(ok, u + a * u_sh, u)
            a = jnp.where(ok, a * a_sh, a)
        a_s[pl.ds(r0, ch), :] = a.reshape(ch, c_w)
        u_s[pl.ds(r0, ch), :] = u.reshape(ch, c_w)
        return carry

    lax.fori_loop(0, ts // ch, chunk, 0)

    def grp(g, hprev):
        r0 = pl.multiple_of(g * 8, 8)
        hg = u_s[pl.ds(r0, 8), :] + a_s[pl.ds(r0, 8), :] * hprev
        u_s[pl.ds(r0, 8), :] = hg
        return hg[7:8, :]

    hlast = lax.fori_loop(0, ts // 8, grp, carry_h[0:1, :], unroll=8)
    carry_h[0:1, :] = hlast
    xs[0:8, :] = xs[ts:ts + 8, :]
    o_ref[0] = (u_s[...] * _gelu_tanh(gr_ref[0])).astype(o_ref.dtype)


def _rglru(xg, conv_w, conv_b, w_bd, b_cat, lru_lambda):
    b, s, w2 = xg.shape
    c_w = w2 // 2
    ts = min(LRU_TILE, s)
    ch = min(LRU_CHUNK, ts)
    return pl.pallas_call(
        functools.partial(_rglru_kernel, ts=ts, ch=ch, c_w=c_w),
        grid=(b, s // ts),
        in_specs=[pl.BlockSpec((1, ts, c_w), lambda bi, si: (bi, si, 0)),
                  pl.BlockSpec((1, ts, c_w), lambda bi, si: (bi, si, 1)),
                  pl.BlockSpec((CONV_WIDTH, c_w), lambda bi, si: (0, 0)),
                  pl.BlockSpec((1, c_w), lambda bi, si: (0, 0)),
                  pl.BlockSpec((c_w, 2 * c_w), lambda bi, si: (0, 0)),
                  pl.BlockSpec((1, 2 * c_w), lambda bi, si: (0, 0)),
                  pl.BlockSpec((1, c_w), lambda bi, si: (0, 0))],
        out_specs=pl.BlockSpec((1, ts, c_w), lambda bi, si: (bi, si, 0)),
        out_shape=jax.ShapeDtypeStruct((b, s, c_w), BF16),
        scratch_shapes=[pltpu.VMEM((ts + 8, c_w), F32), pltpu.VMEM((8, c_w), F32),
                        pltpu.VMEM((ts, c_w), F32), pltpu.VMEM((ts, c_w), F32)],
        compiler_params=_cparams(("parallel", "arbitrary")),
        name="rglru",
    )(xg, xg, conv_w, conv_b.reshape(1, c_w), w_bd, b_cat.reshape(1, 2 * c_w), lru_lambda.reshape(1, c_w))


def _outproj_kernel(att_ref, rnn_ref, x_ref, wo_ref, nw_ref, wr_ref, br_ref,
                    x1_ref, hn_ref, route_ref, n8_ref, *, att_w):
    y = jnp.dot(att_ref[...], wo_ref[:att_w, :], preferred_element_type=F32)
    y = y + jnp.dot(rnn_ref[...], wo_ref[att_w:, :], preferred_element_type=F32)
    x1 = x_ref[...] + y
    x1_ref[...] = x1
    hn = (x1 * lax.rsqrt(jnp.mean(x1 * x1, axis=-1, keepdims=True) + NORM_EPS) * nw_ref[...]).astype(BF16)
    hn_ref[...] = hn
    lg = jnp.dot(hn, wr_ref[...], preferred_element_type=F32) + br_ref[...]
    tm = lg.shape[0]

    col = lax.broadcasted_iota(jnp.int32, lg.shape, 1)
    colf = col.astype(F32)
    big = float(LANES)
    ninf = -jnp.inf
    is_g = col < N_GROUPS
    lgm = jnp.where(is_g, lg, ninf)
    mg = jnp.max(lgm, axis=1, keepdims=True)
    g_sel = jnp.min(jnp.where(lgm == mg, colf, big), axis=1, keepdims=True)
    pg = 1.0 / jnp.sum(jnp.where(is_g, jnp.exp(lgm - mg), 0.0), axis=1, keepdims=True)
    lo = N_GROUPS + EXPERTS_PER_GROUP * g_sel
    in_grp = (colf >= lo) & (colf < lo + EXPERTS_PER_GROUP)
    lem = jnp.where(in_grp, lg, ninf)
    v1 = jnp.max(lem, axis=1, keepdims=True)
    i1 = jnp.min(jnp.where(lem == v1, colf, big), axis=1, keepdims=True)
    lem2 = jnp.where(colf == i1, ninf, lem)
    v2 = jnp.max(lem2, axis=1, keepdims=True)
    i2 = jnp.min(jnp.where(lem2 == v2, colf, big), axis=1, keepdims=True)
    e2 = jnp.exp(v2 - v1)
    den = 1.0 + e2
    g1 = pg / den
    g2 = pg * e2 / den

    oh1 = jnp.where(colf == i1, 1.0, 0.0)
    oh2 = jnp.where(colf == i2, 1.0, 0.0)
    oh = oh1 + oh2
    earlier = (lax.broadcasted_iota(jnp.int32, (tm, tm), 1)
               < lax.broadcasted_iota(jnp.int32, (tm, tm), 0)).astype(BF16)
    pref = jnp.dot(earlier, oh.astype(BF16), preferred_element_type=F32)
    cnt = jnp.sum(oh, axis=0, keepdims=True)
    n8 = jnp.floor((cnt + (SUBLANES - 1)) * (1.0 / SUBLANES))
    before = (lax.broadcasted_iota(jnp.int32, (LANES, LANES), 0)
              < lax.broadcasted_iota(jnp.int32, (LANES, LANES), 1)).astype(BF16)
    loff8 = jnp.dot(jnp.broadcast_to(n8, (SUBLANES, LANES)).astype(BF16), before,
                    preferred_element_type=F32)[0:1]
    pos = SUBLANES * loff8 + pref
    lp1 = jnp.sum(oh1 * pos, axis=1, keepdims=True)
    lp2 = jnp.sum(oh2 * pos, axis=1, keepdims=True)
    route_ref[...] = jnp.where(col == 0, g1,
                     jnp.where(col == 1, g2,
                     jnp.where(col == 2, lp1,
                     jnp.where(col == 3, lp2, 0.0))))
    n8_ref[0] = n8


def _out_proj(att, rnn, xf, w_out_bf, norm_w, w_route_bf, b_route):
    t, d = xf.shape
    att_w = att.shape[1]
    tm = min(ROW_TILE, t)
    row = lambda i: (i, 0)
    fix = lambda i: (0, 0)
    return pl.pallas_call(
        functools.partial(_outproj_kernel, att_w=att_w),
        grid=(t // tm,),
        in_specs=[pl.BlockSpec((tm, att_w), row), pl.BlockSpec((tm, rnn.shape[1]), row),
                  pl.BlockSpec((tm, d), row), pl.BlockSpec(w_out_bf.shape, fix),
                  pl.BlockSpec((1, d), fix), pl.BlockSpec((d, LANES), fix), pl.BlockSpec((1, LANES), fix)],
        out_specs=[pl.BlockSpec((tm, d), row), pl.BlockSpec((tm, d), row), pl.BlockSpec((tm, LANES), row),
                   pl.BlockSpec((1, 1, LANES), lambda i: (i, 0, 0))],
        out_shape=[jax.ShapeDtypeStruct((t, d), F32), jax.ShapeDtypeStruct((t, d), BF16),
                   jax.ShapeDtypeStruct((t, LANES), F32),
                   jax.ShapeDtypeStruct((t // tm, 1, LANES), F32)],
        compiler_params=_cparams(("parallel",)),
        name="out_proj",
    )(att, rnn, xf, w_out_bf, norm_w.reshape(1, d), w_route_bf, b_route)


def _local_rows(tm):
    return -(-(TOP_K * tm + N_EXPERTS * (SUBLANES - 1)) // LANES) * LANES


def _segment_tables(n8_tiles, tm_moe, n_tiles):
    n8 = n8_tiles[:, 0, N_GROUPS:N_GROUPS + N_EXPERTS].astype(jnp.int32)
    c8 = n8 * SUBLANES
    loff = jnp.cumsum(c8, axis=1) - c8
    gtot = jnp.sum(c8, axis=0)
    gpad = (gtot + tm_moe - 1) // tm_moe * tm_moe
    gend = jnp.cumsum(gpad)
    gstart = gend - gpad
    gbase = gstart[None, :] + jnp.cumsum(c8, axis=0) - c8
    tile_row0 = jnp.arange(n_tiles, dtype=jnp.int32) * tm_moe
    tile_e = jnp.minimum(jnp.sum((gend[None, :] <= tile_row0[:, None]).astype(jnp.int32), axis=1),
                         N_EXPERTS - 1).astype(jnp.int32)
    n_used = (gend[-1] // tm_moe).astype(jnp.int32).reshape(1)
    tail_start = (gstart + gtot).astype(jnp.int32)
    tail_n8 = ((gpad - gtot) // SUBLANES).astype(jnp.int32)
    after = gend[tile_e] // tm_moe
    next_e = jnp.where(after < n_used[0], tile_e[jnp.minimum(after, n_tiles - 1)], -1).astype(jnp.int32)
    first = jnp.concatenate([jnp.ones((1,), jnp.int32), (tile_e[1:] != tile_e[:-1]).astype(jnp.int32)])
    w_slot = ((jnp.cumsum(first) - 1) % 2).astype(jnp.int32)
    return (n8.reshape(-1), loff.reshape(-1).astype(jnp.int32), gbase.reshape(-1).astype(jnp.int32),
            tile_e, n_used, tail_start, tail_n8, next_e, w_slot)


def _segment_copies(n8_ref, src_off_ref, dst_off_ref, src, dst, sem, tile, wait):
    def rows_of(e):
        return pl.multiple_of(n8_ref[tile * N_EXPERTS + e] * SUBLANES, SUBLANES)

    if wait:
        total = lax.fori_loop(0, N_EXPERTS, lambda e, acc: acc + rows_of(e), 0)
        total = pl.multiple_of(total, SUBLANES)
        pltpu.make_async_copy(src.at[pl.ds(0, total), :], dst.at[pl.ds(0, total), :], sem).wait()
        return

    def per_expert(e, c):
        k = tile * N_EXPERTS + e
        rows = rows_of(e)

        @pl.when(rows > 0)
        def _():
            pltpu.make_async_copy(
                src.at[pl.ds(pl.multiple_of(src_off_ref[k], SUBLANES), rows), :],
                dst.at[pl.ds(pl.multiple_of(dst_off_ref[k], SUBLANES), rows), :], sem).start()
        return c
    lax.fori_loop(0, N_EXPERTS, per_expert, 0)


def _pack_bf16_pairs(x):
    n = x.shape[1] // 2
    bits = lax.bitcast_convert_type(x, jnp.uint32)
    return (bits[:, :n] >> 16) | (bits[:, n:] & jnp.uint32(0xFFFF0000))


def _unpack_bf16_pairs(p):
    lo = lax.bitcast_convert_type(p << 16, F32)
    hi = lax.bitcast_convert_type(p & jnp.uint32(0xFFFF0000), F32)
    return jnp.concatenate([lo, hi], axis=1).astype(BF16)


def _dispatch_kernel(n8_ref, loff_ref, gbase_ref, tstart_ref, tn8_ref, nu_ref, hn_ref, route_ref, xs_hbm,
                     stage, zbuf, sem, zsem, *, lcap, n_tt):
    i = pl.program_id(0)
    slot = i % 2
    tm = hn_ref.shape[0]
    tm_moe = zbuf.shape[0]
    n_tiles = xs_hbm.shape[0] // tm_moe

    def tail_copies(wait):
        def go(cp):
            if wait:
                cp.wait()
            else:
                cp.start()

        def per_expert(e, c):
            rows = pl.multiple_of(tn8_ref[e] * SUBLANES, SUBLANES)

            @pl.when(rows > 0)
            def _():
                go(pltpu.make_async_copy(
                    zbuf.at[pl.ds(0, rows), :],
                    xs_hbm.at[pl.ds(pl.multiple_of(tstart_ref[e], SUBLANES), rows), :], zsem.at[0]))
            return c
        lax.fori_loop(0, N_EXPERTS, per_expert, 0)

        def per_unused_tile(j, c):
            go(pltpu.make_async_copy(zbuf, xs_hbm.at[pl.ds(pl.multiple_of(j * tm_moe, tm_moe), tm_moe), :],
                                     zsem.at[0]))
            return c
        lax.fori_loop(nu_ref[0], n_tiles, per_unused_tile, 0)

    @pl.when(i == 0)
    def _():
        zbuf[...] = jnp.zeros(zbuf.shape, zbuf.dtype)
        tail_copies(False)

    @pl.when(i >= 2)
    def _():
        _segment_copies(n8_ref, loff_ref, gbase_ref, stage.at[slot], xs_hbm, sem.at[slot], i - 2, True)

    lp1 = route_ref[:, 2:3]
    lp2 = route_ref[:, 3:4]
    cpos = lax.broadcasted_iota(jnp.int32, (tm, lcap), 1).astype(F32)
    sel_t = jnp.where((cpos == lp1) | (cpos == lp2), 1.0, 0.0)
    stage[slot] = _pack_bf16_pairs(jnp.dot(sel_t.T.astype(BF16), hn_ref[...], preferred_element_type=F32))
    _segment_copies(n8_ref, loff_ref, gbase_ref, stage.at[slot], xs_hbm, sem.at[slot], i, False)

    @pl.when(i == n_tt - 1)
    def _():
        _segment_copies(n8_ref, loff_ref, gbase_ref, stage.at[slot], xs_hbm, sem.at[slot], i, True)
        if n_tt > 1:
            _segment_copies(n8_ref, loff_ref, gbase_ref, stage.at[1 - slot], xs_hbm, sem.at[1 - slot],
                            i - 1, True)
        tail_copies(True)


def _dispatch(hn, route, tables, n_rows, tm_moe):
    t, d = hn.shape
    tm = min(ROW_TILE, t)
    n_tt = t // tm
    lcap = _local_rows(tm)
    n8, loff, gbase, _, n_used, tail_start, tail_n8 = tables[:7]
    grid_spec = pltpu.PrefetchScalarGridSpec(
        num_scalar_prefetch=6,
        grid=(n_tt,),
        in_specs=[pl.BlockSpec((tm, d), lambda i, *_: (i, 0)),
                  pl.BlockSpec((tm, LANES), lambda i, *_: (i, 0))],
        out_specs=pl.BlockSpec(memory_space=pl.ANY),
        scratch_shapes=[pltpu.VMEM((2, lcap, d // 2), jnp.uint32), pltpu.VMEM((tm_moe, d // 2), jnp.uint32),
                        pltpu.SemaphoreType.DMA((2,)), pltpu.SemaphoreType.DMA((1,))],
    )
    return pl.pallas_call(
        functools.partial(_dispatch_kernel, lcap=lcap, n_tt=n_tt),
        grid_spec=grid_spec,
        out_shape=jax.ShapeDtypeStruct((n_rows, d // 2), jnp.uint32),
        compiler_params=_cparams(("arbitrary",), has_side_effects=True),
        name="dispatch",
    )(n8, loff, gbase, tail_start, tail_n8, n_used, hn, route)


def _moe_kernel(te_ref, nu_ref, nxt_ref, wslot_ref, xs_ref, wg_hbm, wu_hbm, wd_hbm, y_ref,
                wgf, wuf, wdf, wgb, wub, wdb, wsem):
    i = pl.program_id(0)

    def weight_copies(e, sl):
        return (pltpu.make_async_copy(wg_hbm.at[e], wgf.at[sl], wsem.at[sl, 0]),
                pltpu.make_async_copy(wu_hbm.at[e], wuf.at[sl], wsem.at[sl, 1]),
                pltpu.make_async_copy(wd_hbm.at[e], wdf.at[sl], wsem.at[sl, 2]))

    @pl.when(i == 0)
    def _():
        for cp in weight_copies(te_ref[0], wslot_ref[0]):
            cp.start()

    @pl.when(i < nu_ref[0])
    def _():
        changed = jnp.logical_or(i == 0, te_ref[i] != te_ref[jnp.maximum(i - 1, 0)])

        @pl.when(changed)
        def _():
            sl = wslot_ref[i]
            for cp in weight_copies(te_ref[i], sl):
                cp.wait()
            wgb[...] = wgf[sl].astype(BF16)
            wub[...] = wuf[sl].astype(BF16)
            wdb[...] = wdf[sl].astype(BF16)

            @pl.when(nxt_ref[i] >= 0)
            def _():
                for cp in weight_copies(nxt_ref[i], 1 - sl):
                    cp.start()

        x = _unpack_bf16_pairs(xs_ref[...])
        g = jnp.dot(x, wgb[...], preferred_element_type=F32)
        u = jnp.dot(x, wub[...], preferred_element_type=F32)
        hdn = (g * jax.nn.sigmoid(g) * u).astype(BF16)
        y = jnp.dot(hdn, wdb[...], preferred_element_type=F32)
        y_ref[...] = _pack_bf16_pairs(y.astype(BF16).astype(F32))

    @pl.when(i >= nu_ref[0])
    def _():
        y_ref[...] = jnp.zeros(y_ref.shape, y_ref.dtype)


def _moe(xs, tile_e, n_used, next_e, w_slot, w_g, w_u, w_d, tm):
    n_rows = xs.shape[0]
    d = w_g.shape[1]
    dp = xs.shape[1]
    n_tiles = n_rows // tm
    ff = w_g.shape[2]
    row_blk = lambda i, te, nu, *_: (jnp.minimum(i, nu[0] - 1), 0)
    hbm = pl.BlockSpec(memory_space=pl.ANY)
    grid_spec = pltpu.PrefetchScalarGridSpec(
        num_scalar_prefetch=4,
        grid=(n_tiles,),
        in_specs=[pl.BlockSpec((tm, dp), row_blk), hbm, hbm, hbm],
        out_specs=pl.BlockSpec((tm, dp), lambda i, *_: (i, 0)),
        scratch_shapes=[pltpu.VMEM((2, d, ff), F32), pltpu.VMEM((2, d, ff), F32), pltpu.VMEM((2, ff, d), F32),
                        pltpu.VMEM((d, ff), BF16), pltpu.VMEM((d, ff), BF16), pltpu.VMEM((ff, d), BF16),
                        pltpu.SemaphoreType.DMA((2, 3))],
    )
    return pl.pallas_call(
        _moe_kernel,
        grid_spec=grid_spec,
        out_shape=jax.ShapeDtypeStruct((n_rows, dp), jnp.uint32),
        compiler_params=_cparams(("arbitrary",)),
        name="moe",
    )(tile_e, n_used, next_e, w_slot, xs, w_g, w_u, w_d)


def _combine_kernel(n8_ref, loff_ref, gbase_ref, x1_ref, route_ref, nw_ref, y_hbm, o_ref,
                    ybuf, sem, *, lcap, n_tt):
    i = pl.program_id(0)
    slot = i % 2
    tm = x1_ref.shape[0]

    def fetch(tile, sl, wait):
        _segment_copies(n8_ref, gbase_ref, loff_ref, y_hbm, ybuf.at[sl], sem.at[sl], tile, wait)

    @pl.when(i == 0)
    def _():
        ybuf[...] = jnp.zeros(ybuf.shape, ybuf.dtype)
        fetch(0, 0, False)

    @pl.when(i + 1 < n_tt)
    def _():
        fetch(i + 1, 1 - slot, False)

    fetch(i, slot, True)
    g1 = route_ref[:, 0:1]
    g2 = route_ref[:, 1:2]
    lp1 = route_ref[:, 2:3]
    lp2 = route_ref[:, 3:4]
    cpos = lax.broadcasted_iota(jnp.int32, (tm, lcap), 1).astype(F32)
    gsel = (jnp.where(cpos == lp1, g1, 0.0) + jnp.where(cpos == lp2, g2, 0.0)).astype(BF16)
    moe = jnp.dot(gsel, _unpack_bf16_pairs(ybuf[slot]), preferred_element_type=F32)
    x = x1_ref[...] + moe
    o_ref[...] = x * lax.rsqrt(jnp.mean(x * x, axis=-1, keepdims=True) + NORM_EPS) * nw_ref[...]


def _combine(x1, y, route, norm_w, tables):
    t, d = x1.shape
    tm = min(ROW_TILE, t)
    n_tt = t // tm
    lcap = _local_rows(tm)
    n8, loff, gbase = tables[:3]
    grid_spec = pltpu.PrefetchScalarGridSpec(
        num_scalar_prefetch=3,
        grid=(n_tt,),
        in_specs=[pl.BlockSpec((tm, d), lambda i, *_: (i, 0)),
                  pl.BlockSpec((tm, LANES), lambda i, *_: (i, 0)),
                  pl.BlockSpec((1, d), lambda i, *_: (0, 0)),
                  pl.BlockSpec(memory_space=pl.ANY)],
        out_specs=pl.BlockSpec((tm, d), lambda i, *_: (i, 0)),
        scratch_shapes=[pltpu.VMEM((2, lcap, d // 2), jnp.uint32), pltpu.SemaphoreType.DMA((2,))],
    )
    return pl.pallas_call(
        functools.partial(_combine_kernel, lcap=lcap, n_tt=n_tt),
        grid_spec=grid_spec,
        out_shape=jax.ShapeDtypeStruct((t, d), F32),
        compiler_params=_cparams(("arbitrary",)),
        name="combine",
    )(n8, loff, gbase, x1, route, norm_w.reshape(1, d), y)


def _block_diag(w):
    n, bi, bj = w.shape
    eye = jnp.eye(n, dtype=w.dtype)
    return jnp.einsum('nij,nm->nimj', w, eye).reshape(n * bi, n * bj)


def kernel(x, mix_norm_w, w_in, lambda_q1, lambda_k1, lambda_q2, lambda_k2, head_norm_w, conv_w, conv_b, w_rgate, b_rgate, w_igate, b_igate, lru_lambda, w_out, ffn_norm_w, w_router_group, b_router_group, w_router_expert, b_router_expert, w_exp_gate, w_exp_up, w_exp_down, final_norm_w):
    b, s, d = x.shape
    t = b * s
    assert w_in.shape[0] == 1, "single-layer stack only"
    att_w = N_ATT_HEADS * HEAD_DIM
    tm_moe = MOE_TILE
    xf = x.reshape(t, d)
    for l in range(1):
        lambda_init = 0.8 - 0.6 * math.exp(-0.3 * l)
        qkv, xg = _in_proj(xf, mix_norm_w[l], w_in[l].astype(BF16), att_w)
        lam_params = jnp.stack([lambda_q1[l], lambda_k1[l], lambda_q2[l], lambda_k2[l]]).astype(F32)
        att = _diff_attention(qkv.reshape(b, s, 3 * att_w), lam_params, head_norm_w[l], lambda_init)
        w_bd = jnp.concatenate([_block_diag(w_rgate[l]), _block_diag(w_igate[l])], axis=1).astype(BF16)
        b_cat = jnp.concatenate([b_rgate[l], b_igate[l]])
        rnn = _rglru(xg.reshape(b, s, xg.shape[1]), conv_w[l], conv_b[l], w_bd, b_cat, lru_lambda[l])
        w_route = jnp.concatenate([w_router_group[l], w_router_expert[l]], axis=1)
        w_route = jnp.pad(w_route, ((0, 0), (0, LANES - w_route.shape[1]))).astype(BF16)
        b_route = jnp.concatenate([b_router_group[l], b_router_expert[l]])
        b_route = jnp.pad(b_route, (0, LANES - b_route.shape[0])).reshape(1, LANES).astype(F32)
        x1, hn, route, n8_tiles = _out_proj(att.reshape(t, att_w), rnn.reshape(t, -1), xf, w_out[l].astype(BF16),
                                            ffn_norm_w[l], w_route, b_route)
        n_tt = n8_tiles.shape[0]
        max_rows = TOP_K * t + n_tt * N_EXPERTS * (SUBLANES - 1) + N_EXPERTS * (tm_moe - 1)
        n_tiles = -(-max_rows // tm_moe)
        tables = _segment_tables(n8_tiles, tm_moe, n_tiles)
        xs = _dispatch(hn, route, tables, n_tiles * tm_moe, tm_moe)
        y = _moe(xs, tables[3], tables[4], tables[7], tables[8], w_exp_gate[l], w_exp_up[l], w_exp_down[l], tm_moe)
        out = _combine(x1, y, route, final_norm_w, tables)
    return out.reshape(b, s, d)
```

```python
import functools
import math

import numpy as np
import jax
import jax.numpy as jnp
from jax import lax
from jax.experimental import pallas as pl
from jax.experimental.pallas import tpu as pltpu

F32 = jnp.float32
BF16 = jnp.bfloat16

N_ATT_HEADS = 4
HEAD_DIM = 128
QK_DIM = 64
N_RNN_BLOCKS = 8
CONV_WIDTH = 4
LRU_C = 8.0
N_GROUPS = 4
EXPERTS_PER_GROUP = 8
N_EXPERTS = N_GROUPS * EXPERTS_PER_GROUP
TOP_K = 2
NORM_EPS = 1e-6
HEAD_NORM_EPS = 1e-5
LANES = 128
SUBLANES = 8
NEG_BIG = -1e30

ROW_TILE = 512
ATT_TILE = 512
V_ROWS = HEAD_DIM + 16
LRU_TILE = 512
LRU_CHUNK = 128
MOE_TILE = 512
VMEM_LIMIT = 48 * 1024 * 1024


def _cparams(sem, vmem=VMEM_LIMIT, **kw):
    return pltpu.CompilerParams(dimension_semantics=sem, vmem_limit_bytes=vmem, **kw)


def _inproj_kernel(x_ref, nw_ref, w_ref, qkv_ref, xg_ref, *, att_w):
    x = x_ref[...]
    ms = jnp.mean(x * x, axis=-1, keepdims=True)
    hn = (x * lax.rsqrt(ms + NORM_EPS) * nw_ref[...]).astype(BF16)
    p = jnp.dot(hn, w_ref[...], preferred_element_type=F32)
    scale = QK_DIM ** -0.5
    qkv_ref[:, :att_w] = (p[:, :att_w] * scale).astype(BF16)
    qkv_ref[:, att_w:] = p[:, att_w:3 * att_w].astype(BF16)
    xg_ref[...] = p[:, 3 * att_w:]


def _in_proj(xf, norm_w, w_in_bf, att_w):
    t, d = xf.shape
    n = w_in_bf.shape[1]
    tm = min(ROW_TILE, t)
    return pl.pallas_call(
        functools.partial(_inproj_kernel, att_w=att_w),
        grid=(t // tm,),
        in_specs=[pl.BlockSpec((tm, d), lambda i: (i, 0)),
                  pl.BlockSpec((1, d), lambda i: (0, 0)),
                  pl.BlockSpec((d, n), lambda i: (0, 0))],
        out_specs=[pl.BlockSpec((tm, 3 * att_w), lambda i: (i, 0)),
                   pl.BlockSpec((tm, n - 3 * att_w), lambda i: (i, 0))],
        out_shape=[jax.ShapeDtypeStruct((t, 3 * att_w), BF16),
                   jax.ShapeDtypeStruct((t, n - 3 * att_w), F32)],
        compiler_params=_cparams(("parallel",)),
        name="in_proj",
    )(xf, norm_w.reshape(1, d), w_in_bf)


def _attn_kernel(slope_ref, lam_ref, hw_ref, q_ref, k_ref, v_ref, o_ref,
                 k1a, k2a, vt, s1, m1, a1, s2, m2, a2, *, tq, s_len, lambda_init):
    h = pl.program_id(1)
    qi = pl.program_id(2)
    slope = slope_ref[h]
    lane = lax.broadcasted_iota(jnp.int32, (tq, HEAD_DIM), 1)

    @pl.when(qi == 0)
    def _():
        ones_row = jnp.where(lax.broadcasted_iota(jnp.int32, (V_ROWS - HEAD_DIM, tq), 0) == 0, 1.0, 0.0)
        for c in range(s_len // tq):
            rows = slice(c * tq, (c + 1) * tq)
            kk = k_ref[0, rows, :].astype(F32)
            j = c * tq + lax.broadcasted_iota(jnp.int32, (tq, HEAD_DIM), 0)
            j_lo = (j & 255).astype(F32)
            j_hi = (j - (j & 255)).astype(F32)
            aug = jnp.where(lane == QK_DIM, slope * j_hi, jnp.where(lane == QK_DIM + 1, slope * j_lo, 0.0))
            k1a[rows, :] = jnp.where(lane < QK_DIM, kk, aug).astype(BF16)
            k2a[rows, :] = jnp.where(lane < QK_DIM, pltpu.roll(kk, QK_DIM, axis=1), aug).astype(BF16)
            vt[c, :HEAD_DIM, :] = v_ref[0, rows, :].astype(F32).T.astype(BF16)
            vt[c, HEAD_DIM:, :] = ones_row.astype(BF16)

    ones_cols = jnp.where((lane == QK_DIM) | (lane == QK_DIM + 1), 1.0, 0.0)
    q = q_ref[0].astype(F32)
    q1t = jnp.where(lane < QK_DIM, q, ones_cols).T.astype(BF16)
    q2t = jnp.where(lane < QK_DIM, pltpu.roll(q, QK_DIM, axis=1), ones_cols).T.astype(BF16)

    for m, a in ((m1, a1), (m2, a2)):
        m[...] = jnp.full(m.shape, NEG_BIG, F32)
        a[...] = jnp.zeros(a.shape, F32)

    maps = ((q1t, k1a, s1, m1, a1), (q2t, k2a, s2, m2, a2))

    def scores(c, slot):
        rows = pl.ds(pl.multiple_of(c * tq, tq), tq)
        for qt, ka, sb, _, _ in maps:
            sb[slot] = jnp.dot(ka[rows, :], qt, preferred_element_type=F32)

    def softmax_pv(c, slot):
        for _, _, sb, m, a in maps:
            s = sb[slot]
            m_prev = m[...]
            m_new = jnp.maximum(m_prev, jnp.max(s, axis=0, keepdims=True))
            p = jnp.exp(s - m_new).astype(BF16)
            a[...] = jnp.exp(m_prev - m_new) * a[...] + jnp.dot(vt[c], p, preferred_element_type=F32)
            m[...] = m_new

    def softmax_pv_diagonal(c, slot):
        hq = tq // 2
        keep_t = (lax.broadcasted_iota(jnp.int32, (hq, tq), 0) <= lax.broadcasted_iota(jnp.int32, (hq, tq), 1))
        keep_b = (lax.broadcasted_iota(jnp.int32, (hq, hq), 0) <= lax.broadcasted_iota(jnp.int32, (hq, hq), 1))
        for _, _, sb, m, a in maps:
            top = jnp.where(keep_t, sb[slot, :hq, :], NEG_BIG)
            bot = jnp.where(keep_b, sb[slot, hq:, hq:], NEG_BIG)
            mt = jnp.max(top, axis=0, keepdims=True)
            mb = jnp.max(bot, axis=0, keepdims=True)
            m_prev = m[...]
            m_new = jnp.maximum(m_prev, jnp.concatenate([mt[:, :hq], jnp.maximum(mt[:, hq:], mb)], axis=1))
            p_top = jnp.exp(top - m_new).astype(BF16)
            p_bot = jnp.exp(bot - m_new[:, hq:]).astype(BF16)
            a[...] = (jnp.exp(m_prev - m_new) * a[...]
                      + jnp.dot(vt[c, :, :hq], p_top, preferred_element_type=F32))
            a[:, hq:] += jnp.dot(vt[c, :, hq:], p_bot, preferred_element_type=F32)
            m[...] = m_new

    scores(0, 0)

    def body(j, c):
        scores(2 * j + 1, 1)
        softmax_pv(2 * j, 0)
        scores(2 * j + 2, 0)
        softmax_pv(2 * j + 1, 1)
        return c

    lax.fori_loop(0, qi // 2, body, 0)

    @pl.when(qi % 2 == 0)
    def _():
        softmax_pv_diagonal(qi, 0)

    @pl.when(qi % 2 == 1)
    def _():
        scores(qi, 1)
        softmax_pv(qi - 1, 0)
        softmax_pv_diagonal(qi, 1)

    lam = (jnp.exp(jnp.sum(lam_ref[0:1, :] * lam_ref[1:2, :], axis=1, keepdims=True))
           - jnp.exp(jnp.sum(lam_ref[2:3, :] * lam_ref[3:4, :], axis=1, keepdims=True))
           + lambda_init)
    o1 = a1[:HEAD_DIM, :] / a1[HEAD_DIM:HEAD_DIM + 1, :]
    o2 = a2[:HEAD_DIM, :] / a2[HEAD_DIM:HEAD_DIM + 1, :]
    o = (o1 - lam * o2).T
    o = o * lax.rsqrt(jnp.mean(o * o, axis=-1, keepdims=True) + HEAD_NORM_EPS)
    o_ref[0] = (o * hw_ref[...] * (1.0 - lambda_init)).astype(o_ref.dtype)


def _diff_attention(qkv, lam_params, head_norm_w, lambda_init):
    b, s, w3 = qkv.shape
    nh = N_ATT_HEADS
    tq = min(ATT_TILE, s)
    slopes = jnp.asarray(np.array([2.0 ** (-8.0 * (i + 1) / nh) for i in range(nh)], dtype=np.float32))
    return pl.pallas_call(
        functools.partial(_attn_kernel, tq=tq, s_len=s, lambda_init=lambda_init),
        grid=(b, nh, s // tq),
        in_specs=[pl.BlockSpec(memory_space=pltpu.SMEM),
                  pl.BlockSpec((4, QK_DIM), lambda bi, hi, qi: (0, 0)),
                  pl.BlockSpec((1, HEAD_DIM), lambda bi, hi, qi: (0, 0)),
                  pl.BlockSpec((1, tq, HEAD_DIM), lambda bi, hi, qi: (bi, qi, hi)),
                  pl.BlockSpec((1, s, HEAD_DIM), lambda bi, hi, qi: (bi, 0, nh + hi)),
                  pl.BlockSpec((1, s, HEAD_DIM), lambda bi, hi, qi: (bi, 0, 2 * nh + hi))],
        out_specs=pl.BlockSpec((1, tq, HEAD_DIM), lambda bi, hi, qi: (bi, qi, hi)),
        out_shape=jax.ShapeDtypeStruct((b, s, nh * HEAD_DIM), BF16),
        scratch_shapes=[pltpu.VMEM((s, HEAD_DIM), BF16), pltpu.VMEM((s, HEAD_DIM), BF16),
                        pltpu.VMEM((s // tq, V_ROWS, tq), BF16),
                        pltpu.VMEM((2, tq, tq), F32), pltpu.VMEM((1, tq), F32), pltpu.VMEM((V_ROWS, tq), F32),
                        pltpu.VMEM((2, tq, tq), F32), pltpu.VMEM((1, tq), F32), pltpu.VMEM((V_ROWS, tq), F32)],
        compiler_params=_cparams(("parallel", "parallel", "arbitrary")),
        name="diff_attn",
    )(slopes, lam_params, head_norm_w.reshape(1, HEAD_DIM), qkv, qkv, qkv)


def _gelu_tanh(x):
    return 0.5 * x * (1.0 + jnp.tanh(math.sqrt(2.0 / math.pi) * (x + 0.044715 * (x * x * x))))


def _rglru_kernel(xr_ref, gr_ref, cw_ref, cb_ref, w_ref, b_ref, lam_ref, o_ref,
                  xs, carry_h, a_s, u_s, *, ts, ch, c_w):
    si = pl.program_id(1)

    @pl.when(si == 0)
    def _():
        xs[0:8, :] = jnp.zeros((8, c_w), F32)
        carry_h[...] = jnp.zeros(carry_h.shape, F32)

    xs[8:, :] = xr_ref[0]
    neg_lam = -lam_ref[...]
    sp = jnp.maximum(neg_lam, 0.0) + jnp.log1p(jnp.exp(-jnp.abs(neg_lam)))
    cw = cw_ref[...]
    cb = cb_ref[...]
    bias = b_ref[...]
    r8 = lax.broadcasted_iota(jnp.int32, (ch // SUBLANES, SUBLANES, c_w), 1)

    def chunk(c, carry):
        r0 = pl.multiple_of(c * ch, ch)
        win = xs[pl.ds(r0, ch + 8), :]
        xc = cw[3:4, :] * win[8:] + cb
        for k in (1, 2, 3):
            xc = xc + cw[3 - k:4 - k, :] * pltpu.roll(win, k, axis=0)[8:]
        z = jnp.dot(xc.astype(BF16), w_ref[...], preferred_element_type=F32) + bias
        r = jax.nn.sigmoid(z[:, :c_w])
        ig = jax.nn.sigmoid(z[:, c_w:])
        log_a = (-LRU_C) * r * sp
        a = jnp.exp(log_a)
        w = jnp.tanh(-log_a) * (1.0 + a * a)
        u = jnp.where(w > 0.0, w * lax.rsqrt(w), 0.0) * ig * xc
        a = a.reshape(ch // SUBLANES, SUBLANES, c_w)
        u = u.reshape(ch // SUBLANES, SUBLANES, c_w)
        for k in (1, 2, 4):
            a_sh = pltpu.roll(a, k, axis=1)
            u_sh = pltpu.roll(u, k, axis=1)
            ok = r8 >= k
            u = jnp.where(ok, u + a * u_sh, u)
            a = jnp.where(ok, a * a_sh, a)
        a_s[pl.ds(r0, ch), :] = a.reshape(ch, c_w)
        u_s[pl.ds(r0, ch), :] = u.reshape(ch, c_w)
        return carry

    lax.fori_loop(0, ts // ch, chunk, 0)

    def grp(g, hprev):
        r0 = pl.multiple_of(g * 8, 8)
        hg = u_s[pl.ds(r0, 8), :] + a_s[pl.ds(r0, 8), :] * hprev
        u_s[pl.ds(r0, 8), :] = hg
        return hg[7:8, :]

    hlast = lax.fori_loop(0, ts // 8, grp, carry_h[0:1, :], unroll=8)
    carry_h[0:1, :] = hlast
    xs[0:8, :] = xs[ts:ts + 8, :]
    o_ref[0] = (u_s[...] * _gelu_tanh(gr_ref[0])).astype(o_ref.dtype)


def _rglru(xg, conv_w, conv_b, w_bd, b_cat, lru_lambda):
    b, s, w2 = xg.shape
    c_w = w2 // 2
    ts = min(LRU_TILE, s)
    ch = min(LRU_CHUNK, ts)
    return pl.pallas_call(
        functools.partial(_rglru_kernel, ts=ts, ch=ch, c_w=c_w),
        grid=(b, s // ts),
        in_specs=[pl.BlockSpec((1, ts, c_w), lambda bi, si: (bi, si, 0)),
                  pl.BlockSpec((1, ts, c_w), lambda bi, si: (bi, si, 1)),
                  pl.BlockSpec((CONV_WIDTH, c_w), lambda bi, si: (0, 0)),
                  pl.BlockSpec((1, c_w), lambda bi, si: (0, 0)),
                  pl.BlockSpec((c_w, 2 * c_w), lambda bi, si: (0, 0)),
                  pl.BlockSpec((1, 2 * c_w), lambda bi, si: (0, 0)),
                  pl.BlockSpec((1, c_w), lambda bi, si: (0, 0))],
        out_specs=pl.BlockSpec((1, ts, c_w), lambda bi, si: (bi, si, 0)),
        out_shape=jax.ShapeDtypeStruct((b, s, c_w), BF16),
        scratch_shapes=[pltpu.VMEM((ts + 8, c_w), F32), pltpu.VMEM((8, c_w), F32),
                        pltpu.VMEM((ts, c_w), F32), pltpu.VMEM((ts, c_w), F32)],
        compiler_params=_cparams(("parallel", "arbitrary")),
        name="rglru",
    )(xg, xg, conv_w, conv_b.reshape(1, c_w), w_bd, b_cat.reshape(1, 2 * c_w), lru_lambda.reshape(1, c_w))


def _outproj_kernel(att_ref, rnn_ref, x_ref, wo_ref, nw_ref, wr_ref, br_ref,
                    x1_ref, hn_ref, route_ref, n8_ref, *, att_w):
    y = jnp.dot(att_ref[...], wo_ref[:att_w, :], preferred_element_type=F32)
    y = y + jnp.dot(rnn_ref[...], wo_ref[att_w:, :], preferred_element_type=F32)
    x1 = x_ref[...] + y
    x1_ref[...] = x1
    hn = (x1 * lax.rsqrt(jnp.mean(x1 * x1, axis=-1, keepdims=True) + NORM_EPS) * nw_ref[...]).astype(BF16)
    hn_ref[...] = hn
    lg = jnp.dot(hn, wr_ref[...], preferred_element_type=F32) + br_ref[...]
    tm = lg.shape[0]

    col = lax.broadcasted_iota(jnp.int32, lg.shape, 1)
    colf = col.astype(F32)
    big = float(LANES)
    ninf = -jnp.inf
    is_g = col < N_GROUPS
    lgm = jnp.where(is_g, lg, ninf)
    mg = jnp.max(lgm, axis=1, keepdims=True)
    g_sel = jnp.min(jnp.where(lgm == mg, colf, big), axis=1, keepdims=True)
    pg = 1.0 / jnp.sum(jnp.where(is_g, jnp.exp(lgm - mg), 0.0), axis=1, keepdims=True)
    lo = N_GROUPS + EXPERTS_PER_GROUP * g_sel
    in_grp = (colf >= lo) & (colf < lo + EXPERTS_PER_GROUP)
    lem = jnp.where(in_grp, lg, ninf)
    v1 = jnp.max(lem, axis=1, keepdims=True)
    i1 = jnp.min(jnp.where(lem == v1, colf, big), axis=1, keepdims=True)
    lem2 = jnp.where(colf == i1, ninf, lem)
    v2 = jnp.max(lem2, axis=1, keepdims=True)
    i2 = jnp.min(jnp.where(lem2 == v2, colf, big), axis=1, keepdims=True)
    e2 = jnp.exp(v2 - v1)
    den = 1.0 + e2
    g1 = pg / den
    g2 = pg * e2 / den

    oh1 = jnp.where(colf == i1, 1.0, 0.0)
    oh2 = jnp.where(colf == i2, 1.0, 0.0)
    oh = oh1 + oh2
    earlier = (lax.broadcasted_iota(jnp.int32, (tm, tm), 1)
               < lax.broadcasted_iota(jnp.int32, (tm, tm), 0)).astype(BF16)
    pref = jnp.dot(earlier, oh.astype(BF16), preferred_element_type=F32)
    cnt = jnp.sum(oh, axis=0, keepdims=True)
    n8 = jnp.floor((cnt + (SUBLANES - 1)) * (1.0 / SUBLANES))
    before = (lax.broadcasted_iota(jnp.int32, (LANES, LANES), 0)
              < lax.broadcasted_iota(jnp.int32, (LANES, LANES), 1)).astype(BF16)
    loff8 = jnp.dot(jnp.broadcast_to(n8, (SUBLANES, LANES)).astype(BF16), before,
                    preferred_element_type=F32)[0:1]
    pos = SUBLANES * loff8 + pref
    lp1 = jnp.sum(oh1 * pos, axis=1, keepdims=True)
    lp2 = jnp.sum(oh2 * pos, axis=1, keepdims=True)
    route_ref[...] = jnp.where(col == 0, g1,
                     jnp.where(col == 1, g2,
                     jnp.where(col == 2, lp1,
                     jnp.where(col == 3, lp2, 0.0))))
    n8_ref[0] = n8


def _out_proj(att, rnn, xf, w_out_bf, norm_w, w_route_bf, b_route):
    t, d = xf.shape
    att_w = att.shape[1]
    tm = min(ROW_TILE, t)
    row = lambda i: (i, 0)
    fix = lambda i: (0, 0)
    return pl.pallas_call(
        functools.partial(_outproj_kernel, att_w=att_w),
        grid=(t // tm,),
        in_specs=[pl.BlockSpec((tm, att_w), row), pl.BlockSpec((tm, rnn.shape[1]), row),
                  pl.BlockSpec((tm, d), row), pl.BlockSpec(w_out_bf.shape, fix),
                  pl.BlockSpec((1, d), fix), pl.BlockSpec((d, LANES), fix), pl.BlockSpec((1, LANES), fix)],
        out_specs=[pl.BlockSpec((tm, d), row), pl.BlockSpec((tm, d), row), pl.BlockSpec((tm, LANES), row),
                   pl.BlockSpec((1, 1, LANES), lambda i: (i, 0, 0))],
        out_shape=[jax.ShapeDtypeStruct((t, d), F32), jax.ShapeDtypeStruct((t, d), BF16),
                   jax.ShapeDtypeStruct((t, LANES), F32),
                   jax.ShapeDtypeStruct((t // tm, 1, LANES), F32)],
        compiler_params=_cparams(("parallel",)),
        name="out_proj",
    )(att, rnn, xf, w_out_bf, norm_w.reshape(1, d), w_route_bf, b_route)


def _local_rows(tm):
    return -(-(TOP_K * tm + N_EXPERTS * (SUBLANES - 1)) // LANES) * LANES


def _segment_tables(n8_tiles, tm_moe, n_tiles):
    n8 = n8_tiles[:, 0, N_GROUPS:N_GROUPS + N_EXPERTS].astype(jnp.int32)
    c8 = n8 * SUBLANES
    loff = jnp.cumsum(c8, axis=1) - c8
    gtot = jnp.sum(c8, axis=0)
    gpad = (gtot + tm_moe - 1) // tm_moe * tm_moe
    gend = jnp.cumsum(gpad)
    gstart = gend - gpad
    gbase = gstart[None, :] + jnp.cumsum(c8, axis=0) - c8
    tile_row0 = jnp.arange(n_tiles, dtype=jnp.int32) * tm_moe
    tile_e = jnp.minimum(jnp.sum((gend[None, :] <= tile_row0[:, None]).astype(jnp.int32), axis=1),
                         N_EXPERTS - 1).astype(jnp.int32)
    n_used = (gend[-1] // tm_moe).astype(jnp.int32).reshape(1)
    tail_start = (gstart + gtot).astype(jnp.int32)
    tail_n8 = ((gpad - gtot) // SUBLANES).astype(jnp.int32)
    after = gend[tile_e] // tm_moe
    next_e = jnp.where(after < n_used[0], tile_e[jnp.minimum(after, n_tiles - 1)], -1).astype(jnp.int32)
    first = jnp.concatenate([jnp.ones((1,), jnp.int32), (tile_e[1:] != tile_e[:-1]).astype(jnp.int32)])
    w_slot = ((jnp.cumsum(first) - 1) % 2).astype(jnp.int32)
    return (n8.reshape(-1), loff.reshape(-1).astype(jnp.int32), gbase.reshape(-1).astype(jnp.int32),
            tile_e, n_used, tail_start, tail_n8, next_e, w_slot)


def _segment_copies(n8_ref, src_off_ref, dst_off_ref, src, dst, sem, tile, wait):
    def rows_of(e):
        return pl.multiple_of(n8_ref[tile * N_EXPERTS + e] * SUBLANES, SUBLANES)

    if wait:
        total = lax.fori_loop(0, N_EXPERTS, lambda e, acc: acc + rows_of(e), 0)
        total = pl.multiple_of(total, SUBLANES)
        pltpu.make_async_copy(src.at[pl.ds(0, total), :], dst.at[pl.ds(0, total), :], sem).wait()
        return

    def per_expert(e, c):
        k = tile * N_EXPERTS + e
        rows = rows_of(e)

        @pl.when(rows > 0)
        def _():
            pltpu.make_async_copy(
                src.at[pl.ds(pl.multiple_of(src_off_ref[k], SUBLANES), rows), :],
                dst.at[pl.ds(pl.multiple_of(dst_off_ref[k], SUBLANES), rows), :], sem).start()
        return c
    lax.fori_loop(0, N_EXPERTS, per_expert, 0)


def _pack_bf16_pairs(x):
    n = x.shape[1] // 2
    bits = lax.bitcast_convert_type(x, jnp.uint32)
    return (bits[:, :n] >> 16) | (bits[:, n:] & jnp.uint32(0xFFFF0000))


def _unpack_bf16_pairs(p):
    lo = lax.bitcast_convert_type(p << 16, F32)
    hi = lax.bitcast_convert_type(p & jnp.uint32(0xFFFF0000), F32)
    return jnp.concatenate([lo, hi], axis=1).astype(BF16)


def _dispatch_kernel(n8_ref, loff_ref, gbase_ref, tstart_ref, tn8_ref, nu_ref, hn_ref, route_ref, xs_hbm,
                     stage, zbuf, sem, zsem, *, lcap, n_tt):
    i = pl.program_id(0)
    slot = i % 2
    tm = hn_ref.shape[0]
    tm_moe = zbuf.shape[0]
    n_tiles = xs_hbm.shape[0] // tm_moe

    def tail_copies(wait):
        def go(cp):
            if wait:
                cp.wait()
            else:
                cp.start()

        def per_expert(e, c):
            rows = pl.multiple_of(tn8_ref[e] * SUBLANES, SUBLANES)

            @pl.when(rows > 0)
            def _():
                go(pltpu.make_async_copy(
                    zbuf.at[pl.ds(0, rows), :],
                    xs_hbm.at[pl.ds(pl.multiple_of(tstart_ref[e], SUBLANES), rows), :], zsem.at[0]))
            return c
        lax.fori_loop(0, N_EXPERTS, per_expert, 0)

        def per_unused_tile(j, c):
            go(pltpu.make_async_copy(zbuf, xs_hbm.at[pl.ds(pl.multiple_of(j * tm_moe, tm_moe), tm_moe), :],
                                     zsem.at[0]))
            return c
        lax.fori_loop(nu_ref[0], n_tiles, per_unused_tile, 0)

    @pl.when(i == 0)
    def _():
        zbuf[...] = jnp.zeros(zbuf.shape, zbuf.dtype)
        tail_copies(False)

    @pl.when(i >= 2)
    def _():
        _segment_copies(n8_ref, loff_ref, gbase_ref, stage.at[slot], xs_hbm, sem.at[slot], i - 2, True)

    lp1 = route_ref[:, 2:3]
    lp2 = route_ref[:, 3:4]
    cpos = lax.broadcasted_iota(jnp.int32, (tm, lcap), 1).astype(F32)
    sel_t = jnp.where((cpos == lp1) | (cpos == lp2), 1.0, 0.0)
    stage[slot] = _pack_bf16_pairs(jnp.dot(sel_t.T.astype(BF16), hn_ref[...], preferred_element_type=F32))
    _segment_copies(n8_ref, loff_ref, gbase_ref, stage.at[slot], xs_hbm, sem.at[slot], i, False)

    @pl.when(i == n_tt - 1)
    def _():
        _segment_copies(n8_ref, loff_ref, gbase_ref, stage.at[slot], xs_hbm, sem.at[slot], i, True)
        if n_tt > 1:
            _segment_copies(n8_ref, loff_ref, gbase_ref, stage.at[1 - slot], xs_hbm, sem.at[1 - slot],
                            i - 1, True)
        tail_copies(True)


def _dispatch(hn, route, tables, n_rows, tm_moe):
    t, d = hn.shape
    tm = min(ROW_TILE, t)
    n_tt = t // tm
    lcap = _local_rows(tm)
    n8, loff, gbase, _, n_used, tail_start, tail_n8 = tables[:7]
    grid_spec = pltpu.PrefetchScalarGridSpec(
        num_scalar_prefetch=6,
        grid=(n_tt,),
        in_specs=[pl.BlockSpec((tm, d), lambda i, *_: (i, 0)),
                  pl.BlockSpec((tm, LANES), lambda i, *_: (i, 0))],
        out_specs=pl.BlockSpec(memory_space=pl.ANY),
        scratch_shapes=[pltpu.VMEM((2, lcap, d // 2), jnp.uint32), pltpu.VMEM((tm_moe, d // 2), jnp.uint32),
                        pltpu.SemaphoreType.DMA((2,)), pltpu.SemaphoreType.DMA((1,))],
    )
    return pl.pallas_call(
        functools.partial(_dispatch_kernel, lcap=lcap, n_tt=n_tt),
        grid_spec=grid_spec,
        out_shape=jax.ShapeDtypeStruct((n_rows, d // 2), jnp.uint32),
        compiler_params=_cparams(("arbitrary",), has_side_effects=True),
        name="dispatch",
    )(n8, loff, gbase, tail_start, tail_n8, n_used, hn, route)


def _moe_kernel(te_ref, nu_ref, nxt_ref, wslot_ref, xs_ref, wg_hbm, wu_hbm, wd_hbm, y_ref,
                wgf, wuf, wdf, wgb, wub, wdb, wsem):
    i = pl.program_id(0)

    def weight_copies(e, sl):
        return (pltpu.make_async_copy(wg_hbm.at[e], wgf.at[sl], wsem.at[sl, 0]),
                pltpu.make_async_copy(wu_hbm.at[e], wuf.at[sl], wsem.at[sl, 1]),
                pltpu.make_async_copy(wd_hbm.at[e], wdf.at[sl], wsem.at[sl, 2]))

    @pl.when(i == 0)
    def _():
        for cp in weight_copies(te_ref[0], wslot_ref[0]):
            cp.start()

    @pl.when(i < nu_ref[0])
    def _():
        changed = jnp.logical_or(i == 0, te_ref[i] != te_ref[jnp.maximum(i - 1, 0)])

        @pl.when(changed)
        def _():
            sl = wslot_ref[i]
            for cp in weight_copies(te_ref[i], sl):
                cp.wait()
            wgb[...] = wgf[sl].astype(BF16)
            wub[...] = wuf[sl].astype(BF16)
            wdb[...] = wdf[sl].astype(BF16)

            @pl.when(nxt_ref[i] >= 0)
            def _():
                for cp in weight_copies(nxt_ref[i], 1 - sl):
                    cp.start()

        x = _unpack_bf16_pairs(xs_ref[...])
        g = jnp.dot(x, wgb[...], preferred_element_type=F32)
        u = jnp.dot(x, wub[...], preferred_element_type=F32)
        hdn = (g * jax.nn.sigmoid(g) * u).astype(BF16)
        y = jnp.dot(hdn, wdb[...], preferred_element_type=F32)
        y_ref[...] = _pack_bf16_pairs(y.astype(BF16).astype(F32))

    @pl.when(i >= nu_ref[0])
    def _():
        y_ref[...] = jnp.zeros(y_ref.shape, y_ref.dtype)


def _moe(xs, tile_e, n_used, next_e, w_slot, w_g, w_u, w_d, tm):
    n_rows = xs.shape[0]
    d = w_g.shape[1]
    dp = xs.shape[1]
    n_tiles = n_rows // tm
    ff = w_g.shape[2]
    row_blk = lambda i, te, nu, *_: (jnp.minimum(i, nu[0] - 1), 0)
    hbm = pl.BlockSpec(memory_space=pl.ANY)
    grid_spec = pltpu.PrefetchScalarGridSpec(
        num_scalar_prefetch=4,
        grid=(n_tiles,),
        in_specs=[pl.BlockSpec((tm, dp), row_blk), hbm, hbm, hbm],
        out_specs=pl.BlockSpec((tm, dp), lambda i, *_: (i, 0)),
        scratch_shapes=[pltpu.VMEM((2, d, ff), F32), pltpu.VMEM((2, d, ff), F32), pltpu.VMEM((2, ff, d), F32),
                        pltpu.VMEM((d, ff), BF16), pltpu.VMEM((d, ff), BF16), pltpu.VMEM((ff, d), BF16),
                        pltpu.SemaphoreType.DMA((2, 3))],
    )
    return pl.pallas_call(
        _moe_kernel,
        grid_spec=grid_spec,
        out_shape=jax.ShapeDtypeStruct((n_rows, dp), jnp.uint32),
        compiler_params=_cparams(("arbitrary",)),
        name="moe",
    )(tile_e, n_used, next_e, w_slot, xs, w_g, w_u, w_d)


def _combine_kernel(n8_ref, loff_ref, gbase_ref, x1_ref, route_ref, nw_ref, y_hbm, o_ref,
                    ybuf, sem, *, lcap, n_tt):
    i = pl.program_id(0)
    slot = i % 2
    tm = x1_ref.shape[0]

    def fetch(tile, sl, wait):
        _segment_copies(n8_ref, gbase_ref, loff_ref, y_hbm, ybuf.at[sl], sem.at[sl], tile, wait)

    @pl.when(i == 0)
    def _():
        ybuf[...] = jnp.zeros(ybuf.shape, ybuf.dtype)
        fetch(0, 0, False)

    @pl.when(i + 1 < n_tt)
    def _():
        fetch(i + 1, 1 - slot, False)

    fetch(i, slot, True)
    g1 = route_ref[:, 0:1]
    g2 = route_ref[:, 1:2]
    lp1 = route_ref[:, 2:3]
    lp2 = route_ref[:, 3:4]
    cpos = lax.broadcasted_iota(jnp.int32, (tm, lcap), 1).astype(F32)
    gsel = (jnp.where(cpos == lp1, g1, 0.0) + jnp.where(cpos == lp2, g2, 0.0)).astype(BF16)
    moe = jnp.dot(gsel, _unpack_bf16_pairs(ybuf[slot]), preferred_element_type=F32)
    x = x1_ref[...] + moe
    o_ref[...] = x * lax.rsqrt(jnp.mean(x * x, axis=-1, keepdims=True) + NORM_EPS) * nw_ref[...]


def _combine(x1, y, route, norm_w, tables):
    t, d = x1.shape
    tm = min(ROW_TILE, t)
    n_tt = t // tm
    lcap = _local_rows(tm)
    n8, loff, gbase = tables[:3]
    grid_spec = pltpu.PrefetchScalarGridSpec(
        num_scalar_prefetch=3,
        grid=(n_tt,),
        in_specs=[pl.BlockSpec((tm, d), lambda i, *_: (i, 0)),
                  pl.BlockSpec((tm, LANES), lambda i, *_: (i, 0)),
                  pl.BlockSpec((1, d), lambda i, *_: (0, 0)),
                  pl.BlockSpec(memory_space=pl.ANY)],
        out_specs=pl.BlockSpec((tm, d), lambda i, *_: (i, 0)),
        scratch_shapes=[pltpu.VMEM((2, lcap, d // 2), jnp.uint32), pltpu.SemaphoreType.DMA((2,))],
    )
    return pl.pallas_call(
        functools.partial(_combine_kernel, lcap=lcap, n_tt=n_tt),
        grid_spec=grid_spec,
        out_shape=jax.ShapeDtypeStruct((t, d), F32),
        compiler_params=_cparams(("arbitrary",)),
        name="combine",
    )(n8, loff, gbase, x1, route, norm_w.reshape(1, d), y)


def _block_diag(w):
    n, bi, bj = w.shape
    eye = jnp.eye(n, dtype=w.dtype)
    return jnp.einsum('nij,nm->nimj', w, eye).reshape(n * bi, n * bj)


def kernel(x, mix_norm_w, w_in, lambda_q1, lambda_k1, lambda_q2, lambda_k2, head_norm_w, conv_w, conv_b, w_rgate, b_rgate, w_igate, b_igate, lru_lambda, w_out, ffn_norm_w, w_router_group, b_router_group, w_router_expert, b_router_expert, w_exp_gate, w_exp_up, w_exp_down, final_norm_w):
    b, s, d = x.shape
    t = b * s
    assert w_in.shape[0] == 1, "single-layer stack only"
    att_w = N_ATT_HEADS * HEAD_DIM
    tm_moe = MOE_TILE
    xf = x.reshape(t, d)
    for l in range(1):
        lambda_init = 0.8 - 0.6 * math.exp(-0.3 * l)
        qkv, xg = _in_proj(xf, mix_norm_w[l], w_in[l].astype(BF16), att_w)
        lam_params = jnp.stack([lambda_q1[l], lambda_k1[l], lambda_q2[l], lambda_k2[l]]).astype(F32)
        att = _diff_attention(qkv.reshape(b, s, 3 * att_w), lam_params, head_norm_w[l], lambda_init)
        w_bd = jnp.concatenate([_block_diag(w_rgate[l]), _block_diag(w_igate[l])], axis=1).astype(BF16)
        b_cat = jnp.concatenate([b_rgate[l], b_igate[l]])
        rnn = _rglru(xg.reshape(b, s, xg.shape[1]), conv_w[l], conv_b[l], w_bd, b_cat, lru_lambda[l])
        w_route = jnp.concatenate([w_router_group[l], w_router_expert[l]], axis=1)
        w_route = jnp.pad(w_route, ((0, 0), (0, LANES - w_route.shape[1]))).astype(BF16)
        b_route = jnp.concatenate([b_router_group[l], b_router_expert[l]])
        b_route = jnp.pad(b_route, (0, LANES - b_route.shape[0])).reshape(1, LANES).astype(F32)
        x1, hn, route, n8_tiles = _out_proj(att.reshape(t, att_w), rnn.reshape(t, -1), xf, w_out[l].astype(BF16),
                                            ffn_norm_w[l], w_route, b_route)
        n_tt = n8_tiles.shape[0]
        max_rows = TOP_K * t + n_tt * N_EXPERTS * (SUBLANES - 1) + N_EXPERTS * (tm_moe - 1)
        n_tiles = -(-max_rows // tm_moe)
        tables = _segment_tables(n8_tiles, tm_moe, n_tiles)
        xs = _dispatch(hn, route, tables, n_tiles * tm_moe, tm_moe)
        y = _moe(xs, tables[3], tables[4], tables[7], tables[8], w_exp_gate[l], w_exp_up[l], w_exp_down[l], tm_moe)
        out = _combine(x1, y, route, final_norm_w, tables)
    return out.reshape(b, s, d)
```

```python
import functools
import math

import numpy as np
import jax
import jax.numpy as jnp
from jax import lax
from jax.experimental import pallas as pl
from jax.experimental.pallas import tpu as pltpu

F32 = jnp.float32
BF16 = jnp.bfloat16

N_ATT_HEADS = 4
HEAD_DIM = 128
QK_DIM = 64
N_RNN_BLOCKS = 8
CONV_WIDTH = 4
LRU_C = 8.0
N_GROUPS = 4
EXPERTS_PER_GROUP = 8
N_EXPERTS = N_GROUPS * EXPERTS_PER_GROUP
TOP_K = 2
NORM_EPS = 1e-6
HEAD_NORM_EPS = 1e-5
LANES = 128
SUBLANES = 8
NEG_BIG = -1e30

ROW_TILE = 512
ATT_TILE = 512
V_ROWS = HEAD_DIM + 16
LRU_TILE = 512
LRU_CHUNK = 128
MOE_TILE = 512
VMEM_LIMIT = 48 * 1024 * 1024


def _cparams(sem, vmem=VMEM_LIMIT, **kw):
    return pltpu.CompilerParams(dimension_semantics=sem, vmem_limit_bytes=vmem, **kw)


def _inproj_kernel(slope_ref, x_ref, nw_ref, w_ref, qt_ref, ka_ref, vt_ref, xg_ref, *, att_w, s_len):
    i = pl.program_id(0)
    x = x_ref[...]
    tm = x.shape[0]
    ms = jnp.mean(x * x, axis=-1, keepdims=True)
    hn = (x * lax.rsqrt(ms + NORM_EPS) * nw_ref[...]).astype(BF16)
    p = jnp.dot(hn, w_ref[...], preferred_element_type=F32)
    xg_ref[...] = p[:, 3 * att_w:]

    qt = (p[:, :att_w] * (QK_DIM ** -0.5)).T
    ones2 = jnp.where(lax.broadcasted_iota(jnp.int32, (QK_DIM, tm), 0) < 2, 1.0, 0.0)
    pieces = []
    for g in range(2 * N_ATT_HEADS):
        pieces += [qt[g * QK_DIM:(g + 1) * QK_DIM], ones2]
    qt_ref[0] = jnp.concatenate(pieces, axis=0).astype(BF16)

    lane = lax.broadcasted_iota(jnp.int32, (tm, HEAD_DIM), 1)
    j = (i * tm) % s_len + lax.broadcasted_iota(jnp.int32, (tm, HEAD_DIM), 0)
    j_lo = (j & 255).astype(F32)
    j_hi = (j - (j & 255)).astype(F32)
    vtt = p[:, 2 * att_w:3 * att_w].T
    ones_rows = jnp.where(lax.broadcasted_iota(jnp.int32, (V_ROWS - HEAD_DIM, tm), 0) == 0, 1.0, 0.0)
    for h in range(N_ATT_HEADS):
        slope = slope_ref[h]
        kk = p[:, att_w + h * HEAD_DIM:att_w + (h + 1) * HEAD_DIM]
        aug = jnp.where(lane == QK_DIM, slope * j_hi, jnp.where(lane == QK_DIM + 1, slope * j_lo, 0.0))
        ka_ref[:, 2 * h * HEAD_DIM:(2 * h + 1) * HEAD_DIM] = jnp.where(lane < QK_DIM, kk, aug).astype(BF16)
        ka_ref[:, (2 * h + 1) * HEAD_DIM:(2 * h + 2) * HEAD_DIM] = jnp.where(
            lane < QK_DIM, pltpu.roll(kk, QK_DIM, axis=1), aug).astype(BF16)
        vt_ref[0, h * V_ROWS:h * V_ROWS + HEAD_DIM, :] = vtt[h * HEAD_DIM:(h + 1) * HEAD_DIM].astype(BF16)
        vt_ref[0, h * V_ROWS + HEAD_DIM:(h + 1) * V_ROWS, :] = ones_rows.astype(BF16)


def _alibi_slopes():
    nh = N_ATT_HEADS
    return jnp.asarray(np.array([2.0 ** (-8.0 * (i + 1) / nh) for i in range(nh)], dtype=np.float32))


def _in_proj(xf, norm_w, w_in_bf, att_w, s_len):
    t, d = xf.shape
    n = w_in_bf.shape[1]
    tm = min(ATT_TILE, t)
    nh = N_ATT_HEADS
    return pl.pallas_call(
        functools.partial(_inproj_kernel, att_w=att_w, s_len=s_len),
        grid=(t // tm,),
        in_specs=[pl.BlockSpec(memory_space=pltpu.SMEM),
                  pl.BlockSpec((tm, d), lambda i: (i, 0)),
                  pl.BlockSpec((1, d), lambda i: (0, 0)),
                  pl.BlockSpec((d, n), lambda i: (0, 0))],
        out_specs=[pl.BlockSpec((1, 2 * att_w, tm), lambda i: (i, 0, 0)),
                   pl.BlockSpec((tm, 2 * att_w), lambda i: (i, 0)),
                   pl.BlockSpec((1, nh * V_ROWS, tm), lambda i: (i, 0, 0)),
                   pl.BlockSpec((tm, n - 3 * att_w), lambda i: (i, 0))],
        out_shape=[jax.ShapeDtypeStruct((t // tm, 2 * att_w, tm), BF16),
                   jax.ShapeDtypeStruct((t, 2 * att_w), BF16),
                   jax.ShapeDtypeStruct((t // tm, nh * V_ROWS, tm), BF16),
                   jax.ShapeDtypeStruct((t, n - 3 * att_w), F32)],
        compiler_params=_cparams(("parallel",)),
        name="in_proj",
    )(_alibi_slopes(), xf, norm_w.reshape(1, d), w_in_bf)


def _attn_kernel(lam_ref, hw_ref, q_ref, k_ref, vt, o_ref,
                 s1, m1, a1, s2, m2, a2, *, tq, lambda_init):
    qi = pl.program_id(2)

    for m, a in ((m1, a1), (m2, a2)):
        m[...] = jnp.full(m.shape, NEG_BIG, F32)
        a[...] = jnp.zeros(a.shape, F32)

    maps = ((0, s1, m1, a1), (HEAD_DIM, s2, m2, a2))

    def scores(c, slot):
        rows = pl.ds(pl.multiple_of(c * tq, tq), tq)
        for off, sb, _, _ in maps:
            sb[slot] = jnp.dot(k_ref[0, rows, off:off + HEAD_DIM], q_ref[0, off:off + HEAD_DIM, :],
                               preferred_element_type=F32)

    def softmax_pv(c, slot):
        for _, sb, m, a in maps:
            s = sb[slot]
            m_prev = m[...]
            m_new = jnp.maximum(m_prev, jnp.max(s, axis=0, keepdims=True))
            p = jnp.exp(s - m_new).astype(BF16)
            a[...] = jnp.exp(m_prev - m_new) * a[...] + jnp.dot(vt[c], p, preferred_element_type=F32)
            m[...] = m_new

    def softmax_pv_diagonal(c, slot):
        hq = tq // 2
        keep_t = (lax.broadcasted_iota(jnp.int32, (hq, tq), 0) <= lax.broadcasted_iota(jnp.int32, (hq, tq), 1))
        keep_b = (lax.broadcasted_iota(jnp.int32, (hq, hq), 0) <= lax.broadcasted_iota(jnp.int32, (hq, hq), 1))
        for _, sb, m, a in maps:
            top = jnp.where(keep_t, sb[slot, :hq, :], NEG_BIG)
            bot = jnp.where(keep_b, sb[slot, hq:, hq:], NEG_BIG)
            mt = jnp.max(top, axis=0, keepdims=True)
            mb = jnp.max(bot, axis=0, keepdims=True)
            m_prev = m[...]
            m_new = jnp.maximum(m_prev, jnp.concatenate([mt[:, :hq], jnp.maximum(mt[:, hq:], mb)], axis=1))
            p_top = jnp.exp(top - m_new).astype(BF16)
            p_bot = jnp.exp(bot - m_new[:, hq:]).astype(BF16)
            a[...] = (jnp.exp(m_prev - m_new) * a[...]
                      + jnp.dot(vt[c, :, :hq], p_top, preferred_element_type=F32))
            a[:, hq:] += jnp.dot(vt[c, :, hq:], p_bot, preferred_element_type=F32)
            m[...] = m_new

    scores(0, 0)

    def body(j, c):
        scores(2 * j + 1, 1)
        softmax_pv(2 * j, 0)
        scores(2 * j + 2, 0)
        softmax_pv(2 * j + 1, 1)
        return c

    lax.fori_loop(0, qi // 2, body, 0)

    @pl.when(qi % 2 == 0)
    def _():
        softmax_pv_diagonal(qi, 0)

    @pl.when(qi % 2 == 1)
    def _():
        scores(qi, 1)
        softmax_pv(qi - 1, 0)
        softmax_pv_diagonal(qi, 1)

    lam = (jnp.exp(jnp.sum(lam_ref[0:1, :] * lam_ref[1:2, :], axis=1, keepdims=True))
           - jnp.exp(jnp.sum(lam_ref[2:3, :] * lam_ref[3:4, :], axis=1, keepdims=True))
           + lambda_init)
    o1 = a1[:HEAD_DIM, :] * (1.0 / a1[HEAD_DIM:HEAD_DIM + 1, :])
    o2 = a2[:HEAD_DIM, :] * (1.0 / a2[HEAD_DIM:HEAD_DIM + 1, :])
    o = o1 - lam * o2
    o = o * lax.rsqrt(jnp.mean(o * o, axis=0, keepdims=True) + HEAD_NORM_EPS)
    o_ref[0] = (o.T * hw_ref[...] * (1.0 - lambda_init)).astype(o_ref.dtype)


def _diff_attention(qt, ka, vt, lam_params, head_norm_w, lambda_init, b, s):
    nh = N_ATT_HEADS
    tq = qt.shape[2]
    nq = s // tq
    return pl.pallas_call(
        functools.partial(_attn_kernel, tq=tq, lambda_init=lambda_init),
        grid=(b, nh, nq),
        in_specs=[pl.BlockSpec((4, QK_DIM), lambda bi, hi, qi: (0, 0)),
                  pl.BlockSpec((1, HEAD_DIM), lambda bi, hi, qi: (0, 0)),
                  pl.BlockSpec((1, 2 * HEAD_DIM, tq), lambda bi, hi, qi: (bi * nq + qi, hi, 0)),
                  pl.BlockSpec((1, s, 2 * HEAD_DIM), lambda bi, hi, qi: (bi, 0, hi)),
                  pl.BlockSpec((nq, V_ROWS, tq), lambda bi, hi, qi: (bi, hi, 0))],
        out_specs=pl.BlockSpec((1, tq, HEAD_DIM), lambda bi, hi, qi: (bi, qi, hi)),
        out_shape=jax.ShapeDtypeStruct((b, s, nh * HEAD_DIM), BF16),
        scratch_shapes=[pltpu.VMEM((2, tq, tq), F32), pltpu.VMEM((1, tq), F32), pltpu.VMEM((V_ROWS, tq), F32),
                        pltpu.VMEM((2, tq, tq), F32), pltpu.VMEM((1, tq), F32), pltpu.VMEM((V_ROWS, tq), F32)],
        compiler_params=_cparams(("parallel", "parallel", "arbitrary")),
        name="diff_attn",
    )(lam_params, head_norm_w.reshape(1, HEAD_DIM), qt, ka.reshape(b, s, ka.shape[1]), vt)


def _gelu_tanh(x):
    return 0.5 * x * (1.0 + jnp.tanh(math.sqrt(2.0 / math.pi) * (x + 0.044715 * (x * x * x))))


def _rglru_kernel(xr_ref, gr_ref, cw_ref, cb_ref, w_ref, b_ref, lam_ref, o_ref,
                  xs, carry_h, a_s, u_s, *, ts, ch, c_w):
    si = pl.program_id(1)

    @pl.when(si == 0)
    def _():
        xs[0:8, :] = jnp.zeros((8, c_w), F32)
        carry_h[...] = jnp.zeros(carry_h.shape, F32)

    xs[8:, :] = xr_ref[0]
    neg_lam = -lam_ref[...]
    sp = jnp.maximum(neg_lam, 0.0) + jnp.log1p(jnp.exp(-jnp.abs(neg_lam)))
    cw = cw_ref[...]
    cb = cb_ref[...]
    bias = b_ref[...]
    r8 = lax.broadcasted_iota(jnp.int32, (ch // SUBLANES, SUBLANES, c_w), 1)

    def chunk(c, carry):
        r0 = pl.multiple_of(c * ch, ch)
        win = xs[pl.ds(r0, ch + 8), :]
        xc = cw[3:4, :] * win[8:] + cb
        for k in (1, 2, 3):
            xc = xc + cw[3 - k:4 - k, :] * pltpu.roll(win, k, axis=0)[8:]
        z = jnp.dot(xc.astype(BF16), w_ref[...], preferred_element_type=F32) + bias
        r = jax.nn.sigmoid(z[:, :c_w])
        ig = jax.nn.sigmoid(z[:, c_w:])
        log_a = (-LRU_C) * r * sp
        a = jnp.exp(log_a)
        w = jnp.tanh(-log_a) * (1.0 + a * a)
        u = jnp.where(w > 0.0, w * lax.rsqrt(w), 0.0) * ig * xc
        a = a.reshape(ch // SUBLANES, SUBLANES, c_w)
        u = u.reshape(ch // SUBLANES, SUBLANES, c_w)
        for k in (1, 2, 4):
            a_sh = pltpu.roll(a, k, axis=1)
            u_sh = pltpu.roll(u, k, axis=1)
            ok = r8 >= k
            u = jnp.where(ok, u + a * u_sh, u)
            a = jnp.where(ok, a * a_sh, a)
        a_s[pl.ds(r0, ch), :] = a.reshape(ch, c_w)
        u_s[pl.ds(r0, ch), :] = u.reshape(ch, c_w)
        return carry

    lax.fori_loop(0, ts // ch, chunk, 0)

    def grp(g, hprev):
        r0 = pl.multiple_of(g * 8, 8)
        hg = u_s[pl.ds(r0, 8), :] + a_s[pl.ds(r0, 8), :] * hprev
        u_s[pl.ds(r0, 8), :] = hg
        return hg[7:8, :]

    hlast = lax.fori_loop(0, ts // 8, grp, carry_h[0:1, :], unroll=8)
    carry_h[0:1, :] = hlast
    xs[0:8, :] = xs[ts:ts + 8, :]
    o_ref[0] = (u_s[...] * _gelu_tanh(gr_ref[0])).astype(o_ref.dtype)


def _rglru(xg, conv_w, conv_b, w_bd, b_cat, lru_lambda):
    b, s, w2 = xg.shape
    c_w = w2 // 2
    ts = min(LRU_TILE, s)
    ch = min(LRU_CHUNK, ts)
    return pl.pallas_call(
        functools.partial(_rglru_kernel, ts=ts, ch=ch, c_w=c_w),
        grid=(b, s // ts),
        in_specs=[pl.BlockSpec((1, ts, c_w), lambda bi, si: (bi, si, 0)),
                  pl.BlockSpec((1, ts, c_w), lambda bi, si: (bi, si, 1)),
                  pl.BlockSpec((CONV_WIDTH, c_w), lambda bi, si: (0, 0)),
                  pl.BlockSpec((1, c_w), lambda bi, si: (0, 0)),
                  pl.BlockSpec((c_w, 2 * c_w), lambda bi, si: (0, 0)),
                  pl.BlockSpec((1, 2 * c_w), lambda bi, si: (0, 0)),
                  pl.BlockSpec((1, c_w), lambda bi, si: (0, 0))],
        out_specs=pl.BlockSpec((1, ts, c_w), lambda bi, si: (bi, si, 0)),
        out_shape=jax.ShapeDtypeStruct((b, s, c_w), BF16),
        scratch_shapes=[pltpu.VMEM((ts + 8, c_w), F32), pltpu.VMEM((8, c_w), F32),
                        pltpu.VMEM((ts, c_w), F32), pltpu.VMEM((ts, c_w), F32)],
        compiler_params=_cparams(("parallel", "arbitrary")),
        name="rglru",
    )(xg, xg, conv_w, conv_b.reshape(1, c_w), w_bd, b_cat.reshape(1, 2 * c_w), lru_lambda.reshape(1, c_w))


def _outproj_kernel(att_ref, rnn_ref, x_ref, wo_ref, nw_ref, wr_ref, br_ref,
                    x1_ref, hn_ref, route_ref, n8_ref, *, att_w):
    y = jnp.dot(att_ref[...], wo_ref[:att_w, :], preferred_element_type=F32)
    y = y + jnp.dot(rnn_ref[...], wo_ref[att_w:, :], preferred_element_type=F32)
    x1 = x_ref[...] + y
    x1_ref[...] = x1
    hn = (x1 * lax.rsqrt(jnp.mean(x1 * x1, axis=-1, keepdims=True) + NORM_EPS) * nw_ref[...]).astype(BF16)
    hn_ref[...] = hn
    lg = jnp.dot(hn, wr_ref[...], preferred_element_type=F32) + br_ref[...]
    tm = lg.shape[0]

    col = lax.broadcasted_iota(jnp.int32, lg.shape, 1)
    colf = col.astype(F32)
    big = float(LANES)
    ninf = -jnp.inf
    is_g = col < N_GROUPS
    lgm = jnp.where(is_g, lg, ninf)
    mg = jnp.max(lgm, axis=1, keepdims=True)
    g_sel = jnp.min(jnp.where(lgm == mg, colf, big), axis=1, keepdims=True)
    pg = 1.0 / jnp.sum(jnp.where(is_g, jnp.exp(lgm - mg), 0.0), axis=1, keepdims=True)
    lo = N_GROUPS + EXPERTS_PER_GROUP * g_sel
    in_grp = (colf >= lo) & (colf < lo + EXPERTS_PER_GROUP)
    lem = jnp.where(in_grp, lg, ninf)
    v1 = jnp.max(lem, axis=1, keepdims=True)
    i1 = jnp.min(jnp.where(lem == v1, colf, big), axis=1, keepdims=True)
    lem2 = jnp.where(colf == i1, ninf, lem)
    v2 = jnp.max(lem2, axis=1, keepdims=True)
    i2 = jnp.min(jnp.where(lem2 == v2, colf, big), axis=1, keepdims=True)
    e2 = jnp.exp(v2 - v1)
    den = 1.0 + e2
    g1 = pg / den
    g2 = pg * e2 / den

    oh1 = jnp.where(colf == i1, 1.0, 0.0)
    oh2 = jnp.where(colf == i2, 1.0, 0.0)
    oh = oh1 + oh2
    earlier = (lax.broadcasted_iota(jnp.int32, (tm, tm), 1)
               < lax.broadcasted_iota(jnp.int32, (tm, tm), 0)).astype(BF16)
    pref = jnp.dot(earlier, oh.astype(BF16), preferred_element_type=F32)
    cnt = jnp.sum(oh, axis=0, keepdims=True)
    n8 = jnp.floor((cnt + (SUBLANES - 1)) * (1.0 / SUBLANES))
    before = (lax.broadcasted_iota(jnp.int32, (LANES, LANES), 0)
              < lax.broadcasted_iota(jnp.int32, (LANES, LANES), 1)).astype(BF16)
    loff8 = jnp.dot(jnp.broadcast_to(n8, (SUBLANES, LANES)).astype(BF16), before,
                    preferred_element_type=F32)[0:1]
    pos = SUBLANES * loff8 + pref
    lp1 = jnp.sum(oh1 * pos, axis=1, keepdims=True)
    lp2 = jnp.sum(oh2 * pos, axis=1, keepdims=True)
    route_ref[...] = jnp.where(col == 0, g1,
                     jnp.where(col == 1, g2,
                     jnp.where(col == 2, lp1,
                     jnp.where(col == 3, lp2, 0.0))))
    n8_ref[0] = n8


def _out_proj(att, rnn, xf, w_out_bf, norm_w, w_route_bf, b_route):
    t, d = xf.shape
    att_w = att.shape[1]
    tm = min(ROW_TILE, t)
    row = lambda i: (i, 0)
    fix = lambda i: (0, 0)
    return pl.pallas_call(
        functools.partial(_outproj_kernel, att_w=att_w),
        grid=(t // tm,),
        in_specs=[pl.BlockSpec((tm, att_w), row), pl.BlockSpec((tm, rnn.shape[1]), row),
                  pl.BlockSpec((tm, d), row), pl.BlockSpec(w_out_bf.shape, fix),
                  pl.BlockSpec((1, d), fix), pl.BlockSpec((d, LANES), fix), pl.BlockSpec((1, LANES), fix)],
        out_specs=[pl.BlockSpec((tm, d), row), pl.BlockSpec((tm, d), row), pl.BlockSpec((tm, LANES), row),
                   pl.BlockSpec((1, 1, LANES), lambda i: (i, 0, 0))],
        out_shape=[jax.ShapeDtypeStruct((t, d), F32), jax.ShapeDtypeStruct((t, d), BF16),
                   jax.ShapeDtypeStruct((t, LANES), F32),
                   jax.ShapeDtypeStruct((t // tm, 1, LANES), F32)],
        compiler_params=_cparams(("parallel",)),
        name="out_proj",
    )(att, rnn, xf, w_out_bf, norm_w.reshape(1, d), w_route_bf, b_route)


def _local_rows(tm):
    return -(-(TOP_K * tm + N_EXPERTS * (SUBLANES - 1)) // LANES) * LANES


def _segment_tables(n8_tiles, tm_moe, n_tiles):
    n8 = n8_tiles[:, 0, N_GROUPS:N_GROUPS + N_EXPERTS].astype(jnp.int32)
    c8 = n8 * SUBLANES
    loff = jnp.cumsum(c8, axis=1) - c8
    gtot = jnp.sum(c8, axis=0)
    gpad = (gtot + tm_moe - 1) // tm_moe * tm_moe
    gend = jnp.cumsum(gpad)
    gstart = gend - gpad
    gbase = gstart[None, :] + jnp.cumsum(c8, axis=0) - c8
    tile_row0 = jnp.arange(n_tiles, dtype=jnp.int32) * tm_moe
    tile_e = jnp.minimum(jnp.sum((gend[None, :] <= tile_row0[:, None]).astype(jnp.int32), axis=1),
                         N_EXPERTS - 1).astype(jnp.int32)
    n_used = (gend[-1] // tm_moe).astype(jnp.int32).reshape(1)
    tail_start = (gstart + gtot).astype(jnp.int32)
    tail_n8 = ((gpad - gtot) // SUBLANES).astype(jnp.int32)
    after = gend[tile_e] // tm_moe
    next_e = jnp.where(after < n_used[0], tile_e[jnp.minimum(after, n_tiles - 1)], -1).astype(jnp.int32)
    first = jnp.concatenate([jnp.ones((1,), jnp.int32), (tile_e[1:] != tile_e[:-1]).astype(jnp.int32)])
    w_slot = ((jnp.cumsum(first) - 1) % 2).astype(jnp.int32)
    return (n8.reshape(-1), loff.reshape(-1).astype(jnp.int32), gbase.reshape(-1).astype(jnp.int32),
            tile_e, n_used, tail_start, tail_n8, next_e, w_slot)


def _segment_copies(n8_ref, src_off_ref, dst_off_ref, src, dst, sem, tile, wait):
    def rows_of(e):
        return pl.multiple_of(n8_ref[tile * N_EXPERTS + e] * SUBLANES, SUBLANES)

    if wait:
        total = lax.fori_loop(0, N_EXPERTS, lambda e, acc: acc + rows_of(e), 0)
        total = pl.multiple_of(total, SUBLANES)
        pltpu.make_async_copy(src.at[pl.ds(0, total), :], dst.at[pl.ds(0, total), :], sem).wait()
        return

    def per_expert(e, c):
        k = tile * N_EXPERTS + e
        rows = rows_of(e)

        @pl.when(rows > 0)
        def _():
            pltpu.make_async_copy(
                src.at[pl.ds(pl.multiple_of(src_off_ref[k], SUBLANES), rows), :],
                dst.at[pl.ds(pl.multiple_of(dst_off_ref[k], SUBLANES), rows), :], sem).start()
        return c
    lax.fori_loop(0, N_EXPERTS, per_expert, 0)


def _pack_bf16_pairs(x):
    n = x.shape[1] // 2
    bits = lax.bitcast_convert_type(x, jnp.uint32)
    return (bits[:, :n] >> 16) | (bits[:, n:] & jnp.uint32(0xFFFF0000))


def _unpack_bf16_pairs(p):
    lo = lax.bitcast_convert_type(p << 16, F32)
    hi = lax.bitcast_convert_type(p & jnp.uint32(0xFFFF0000), F32)
    return jnp.concatenate([lo, hi], axis=1).astype(BF16)


def _dispatch_kernel(n8_ref, loff_ref, gbase_ref, tstart_ref, tn8_ref, nu_ref, hn_ref, route_ref, xs_hbm,
                     stage, zbuf, sem, zsem, *, lcap, n_tt):
    i = pl.program_id(0)
    slot = i % 2
    tm = hn_ref.shape[0]
    tm_moe = zbuf.shape[0]
    n_tiles = xs_hbm.shape[0] // tm_moe

    def tail_copies(wait):
        def go(cp):
            if wait:
                cp.wait()
            else:
                cp.start()

        def per_expert(e, c):
            rows = pl.multiple_of(tn8_ref[e] * SUBLANES, SUBLANES)

            @pl.when(rows > 0)
            def _():
                go(pltpu.make_async_copy(
                    zbuf.at[pl.ds(0, rows), :],
                    xs_hbm.at[pl.ds(pl.multiple_of(tstart_ref[e], SUBLANES), rows), :], zsem.at[0]))
            return c
        lax.fori_loop(0, N_EXPERTS, per_expert, 0)

        def per_unused_tile(j, c):
            go(pltpu.make_async_copy(zbuf, xs_hbm.at[pl.ds(pl.multiple_of(j * tm_moe, tm_moe), tm_moe), :],
                                     zsem.at[0]))
            return c
        lax.fori_loop(nu_ref[0], n_tiles, per_unused_tile, 0)

    @pl.when(i == 0)
    def _():
        zbuf[...] = jnp.zeros(zbuf.shape, zbuf.dtype)
        tail_copies(False)

    @pl.when(i >= 2)
    def _():
        _segment_copies(n8_ref, loff_ref, gbase_ref, stage.at[slot], xs_hbm, sem.at[slot], i - 2, True)

    lp1 = route_ref[:, 2:3]
    lp2 = route_ref[:, 3:4]
    cpos = lax.broadcasted_iota(jnp.int32, (tm, lcap), 1).astype(F32)
    sel_t = jnp.where((cpos == lp1) | (cpos == lp2), 1.0, 0.0)
    stage[slot] = _pack_bf16_pairs(jnp.dot(sel_t.T.astype(BF16), hn_ref[...], preferred_element_type=F32))
    _segment_copies(n8_ref, loff_ref, gbase_ref, stage.at[slot], xs_hbm, sem.at[slot], i, False)

    @pl.when(i == n_tt - 1)
    def _():
        _segment_copies(n8_ref, loff_ref, gbase_ref, stage.at[slot], xs_hbm, sem.at[slot], i, True)
        if n_tt > 1:
            _segment_copies(n8_ref, loff_ref, gbase_ref, stage.at[1 - slot], xs_hbm, sem.at[1 - slot],
                            i - 1, True)
        tail_copies(True)


def _dispatch(hn, route, tables, n_rows, tm_moe):
    t, d = hn.shape
    tm = min(ROW_TILE, t)
    n_tt = t // tm
    lcap = _local_rows(tm)
    n8, loff, gbase, _, n_used, tail_start, tail_n8 = tables[:7]
    grid_spec = pltpu.PrefetchScalarGridSpec(
        num_scalar_prefetch=6,
        grid=(n_tt,),
        in_specs=[pl.BlockSpec((tm, d), lambda i, *_: (i, 0)),
                  pl.BlockSpec((tm, LANES), lambda i, *_: (i, 0))],
        out_specs=pl.BlockSpec(memory_space=pl.ANY),
        scratch_shapes=[pltpu.VMEM((2, lcap, d // 2), jnp.uint32), pltpu.VMEM((tm_moe, d // 2), jnp.uint32),
                        pltpu.SemaphoreType.DMA((2,)), pltpu.SemaphoreType.DMA((1,))],
    )
    return pl.pallas_call(
        functools.partial(_dispatch_kernel, lcap=lcap, n_tt=n_tt),
        grid_spec=grid_spec,
        out_shape=jax.ShapeDtypeStruct((n_rows, d // 2), jnp.uint32),
        compiler_params=_cparams(("arbitrary",), has_side_effects=True),
        name="dispatch",
    )(n8, loff, gbase, tail_start, tail_n8, n_used, hn, route)


def _moe_kernel(te_ref, nu_ref, nxt_ref, wslot_ref, xs_ref, wg_hbm, wu_hbm, wd_hbm, y_ref,
                wgf, wuf, wdf, wgb, wub, wdb, wsem):
    i = pl.program_id(0)

    def weight_copies(e, sl):
        return (pltpu.make_async_copy(wg_hbm.at[e], wgf.at[sl], wsem.at[sl, 0]),
                pltpu.make_async_copy(wu_hbm.at[e], wuf.at[sl], wsem.at[sl, 1]),
                pltpu.make_async_copy(wd_hbm.at[e], wdf.at[sl], wsem.at[sl, 2]))

    @pl.when(i == 0)
    def _():
        for cp in weight_copies(te_ref[0], wslot_ref[0]):
            cp.start()

    @pl.when(i < nu_ref[0])
    def _():
        changed = jnp.logical_or(i == 0, te_ref[i] != te_ref[jnp.maximum(i - 1, 0)])

        @pl.when(changed)
        def _():
            sl = wslot_ref[i]
            for cp in weight_copies(te_ref[i], sl):
                cp.wait()
            wgb[...] = wgf[sl].astype(BF16)
            wub[...] = wuf[sl].astype(BF16)
            wdb[...] = wdf[sl].astype(BF16)

            @pl.when(nxt_ref[i] >= 0)
            def _():
                for cp in weight_copies(nxt_ref[i], 1 - sl):
                    cp.start()

        x = _unpack_bf16_pairs(xs_ref[...])
        g = jnp.dot(x, wgb[...], preferred_element_type=F32)
        u = jnp.dot(x, wub[...], preferred_element_type=F32)
        hdn = (g * jax.nn.sigmoid(g) * u).astype(BF16)
        y = jnp.dot(hdn, wdb[...], preferred_element_type=F32)
        y_ref[...] = _pack_bf16_pairs(y.astype(BF16).astype(F32))

    @pl.when(i >= nu_ref[0])
    def _():
        y_ref[...] = jnp.zeros(y_ref.shape, y_ref.dtype)


def _moe(xs, tile_e, n_used, next_e, w_slot, w_g, w_u, w_d, tm):
    n_rows = xs.shape[0]
    d = w_g.shape[1]
    dp = xs.shape[1]
    n_tiles = n_rows // tm
    ff = w_g.shape[2]
    row_blk = lambda i, te, nu, *_: (jnp.minimum(i, nu[0] - 1), 0)
    hbm = pl.BlockSpec(memory_space=pl.ANY)
    grid_spec = pltpu.PrefetchScalarGridSpec(
        num_scalar_prefetch=4,
        grid=(n_tiles,),
        in_specs=[pl.BlockSpec((tm, dp), row_blk), hbm, hbm, hbm],
        out_specs=pl.BlockSpec((tm, dp), lambda i, *_: (i, 0)),
        scratch_shapes=[pltpu.VMEM((2, d, ff), F32), pltpu.VMEM((2, d, ff), F32), pltpu.VMEM((2, ff, d), F32),
                        pltpu.VMEM((d, ff), BF16), pltpu.VMEM((d, ff), BF16), pltpu.VMEM((ff, d), BF16),
                        pltpu.SemaphoreType.DMA((2, 3))],
    )
    return pl.pallas_call(
        _moe_kernel,
        grid_spec=grid_spec,
        out_shape=jax.ShapeDtypeStruct((n_rows, dp), jnp.uint32),
        compiler_params=_cparams(("arbitrary",)),
        name="moe",
    )(tile_e, n_used, next_e, w_slot, xs, w_g, w_u, w_d)


def _combine_kernel(n8_ref, loff_ref, gbase_ref, x1_ref, route_ref, nw_ref, y_hbm, o_ref,
                    ybuf, sem, *, lcap, n_tt):
    i = pl.program_id(0)
    slot = i % 2
    tm = x1_ref.shape[0]

    def fetch(tile, sl, wait):
        _segment_copies(n8_ref, gbase_ref, loff_ref, y_hbm, ybuf.at[sl], sem.at[sl], tile, wait)

    @pl.when(i == 0)
    def _():
        ybuf[...] = jnp.zeros(ybuf.shape, ybuf.dtype)
        fetch(0, 0, False)

    @pl.when(i + 1 < n_tt)
    def _():
        fetch(i + 1, 1 - slot, False)

    fetch(i, slot, True)
    g1 = route_ref[:, 0:1]
    g2 = route_ref[:, 1:2]
    lp1 = route_ref[:, 2:3]
    lp2 = route_ref[:, 3:4]
    cpos = lax.broadcasted_iota(jnp.int32, (tm, lcap), 1).astype(F32)
    gsel = (jnp.where(cpos == lp1, g1, 0.0) + jnp.where(cpos == lp2, g2, 0.0)).astype(BF16)
    moe = jnp.dot(gsel, _unpack_bf16_pairs(ybuf[slot]), preferred_element_type=F32)
    x = x1_ref[...] + moe
    o_ref[...] = x * lax.rsqrt(jnp.mean(x * x, axis=-1, keepdims=True) + NORM_EPS) * nw_ref[...]


def _combine(x1, y, route, norm_w, tables):
    t, d = x1.shape
    tm = min(ROW_TILE, t)
    n_tt = t // tm
    lcap = _local_rows(tm)
    n8, loff, gbase = tables[:3]
    grid_spec = pltpu.PrefetchScalarGridSpec(
        num_scalar_prefetch=3,
        grid=(n_tt,),
        in_specs=[pl.BlockSpec((tm, d), lambda i, *_: (i, 0)),
                  pl.BlockSpec((tm, LANES), lambda i, *_: (i, 0)),
                  pl.BlockSpec((1, d), lambda i, *_: (0, 0)),
                  pl.BlockSpec(memory_space=pl.ANY)],
        out_specs=pl.BlockSpec((tm, d), lambda i, *_: (i, 0)),
        scratch_shapes=[pltpu.VMEM((2, lcap, d // 2), jnp.uint32), pltpu.SemaphoreType.DMA((2,))],
    )
    return pl.pallas_call(
        functools.partial(_combine_kernel, lcap=lcap, n_tt=n_tt),
        grid_spec=grid_spec,
        out_shape=jax.ShapeDtypeStruct((t, d), F32),
        compiler_params=_cparams(("arbitrary",)),
        name="combine",
    )(n8, loff, gbase, x1, route, norm_w.reshape(1, d), y)


def _block_diag(w):
    n, bi, bj = w.shape
    eye = jnp.eye(n, dtype=w.dtype)
    return jnp.einsum('nij,nm->nimj', w, eye).reshape(n * bi, n * bj)


def kernel(x, mix_norm_w, w_in, lambda_q1, lambda_k1, lambda_q2, lambda_k2, head_norm_w, conv_w, conv_b, w_rgate, b_rgate, w_igate, b_igate, lru_lambda, w_out, ffn_norm_w, w_router_group, b_router_group, w_router_expert, b_router_expert, w_exp_gate, w_exp_up, w_exp_down, final_norm_w):
    b, s, d = x.shape
    t = b * s
    assert w_in.shape[0] == 1, "single-layer stack only"
    att_w = N_ATT_HEADS * HEAD_DIM
    tm_moe = MOE_TILE
    xf = x.reshape(t, d)
    for l in range(1):
        lambda_init = 0.8 - 0.6 * math.exp(-0.3 * l)
        assert s % ATT_TILE == 0, "sequence length must be a multiple of the attention tile"
        qt, ka, vt, xg = _in_proj(xf, mix_norm_w[l], w_in[l].astype(BF16), att_w, s)
        lam_params = jnp.stack([lambda_q1[l], lambda_k1[l], lambda_q2[l], lambda_k2[l]]).astype(F32)
        att = _diff_attention(qt, ka, vt, lam_params, head_norm_w[l], lambda_init, b, s)
        w_bd = jnp.concatenate([_block_diag(w_rgate[l]), _block_diag(w_igate[l])], axis=1).astype(BF16)
        b_cat = jnp.concatenate([b_rgate[l], b_igate[l]])
        rnn = _rglru(xg.reshape(b, s, xg.shape[1]), conv_w[l], conv_b[l], w_bd, b_cat, lru_lambda[l])
        w_route = jnp.concatenate([w_router_group[l], w_router_expert[l]], axis=1)
        w_route = jnp.pad(w_route, ((0, 0), (0, LANES - w_route.shape[1]))).astype(BF16)
        b_route = jnp.concatenate([b_router_group[l], b_router_expert[l]])
        b_route = jnp.pad(b_route, (0, LANES - b_route.shape[0])).reshape(1, LANES).astype(F32)
        x1, hn, route, n8_tiles = _out_proj(att.reshape(t, att_w), rnn.reshape(t, -1), xf, w_out[l].astype(BF16),
                                            ffn_norm_w[l], w_route, b_route)
        n_tt = n8_tiles.shape[0]
        max_rows = TOP_K * t + n_tt * N_EXPERTS * (SUBLANES - 1) + N_EXPERTS * (tm_moe - 1)
        n_tiles = -(-max_rows // tm_moe)
        tables = _segment_tables(n8_tiles, tm_moe, n_tiles)
        xs = _dispatch(hn, route, tables, n_tiles * tm_moe, tm_moe)
        y = _moe(xs, tables[3], tables[4], tables[7], tables[8], w_exp_gate[l], w_exp_up[l], w_exp_down[l], tm_moe)
        out = _combine(x1, y, route, final_norm_w, tables)
    return out.reshape(b, s, d)
```

```python
import functools
import math

import numpy as np
import jax
import jax.numpy as jnp
from jax import lax
from jax.experimental import pallas as pl
from jax.experimental.pallas import tpu as pltpu

F32 = jnp.float32
BF16 = jnp.bfloat16

N_ATT_HEADS = 4
HEAD_DIM = 128
QK_DIM = 64
N_RNN_BLOCKS = 8
CONV_WIDTH = 4
LRU_C = 8.0
N_GROUPS = 4
EXPERTS_PER_GROUP = 8
N_EXPERTS = N_GROUPS * EXPERTS_PER_GROUP
TOP_K = 2
NORM_EPS = 1e-6
HEAD_NORM_EPS = 1e-5
LANES = 128
SUBLANES = 8
NEG_BIG = -1e30

ROW_TILE = 512
ATT_TILE = 512
ATT_HEADS_PER_STEP = 4
V_ROWS = HEAD_DIM + 16
LRU_TILE = 512
LRU_CHUNK = 128
MOE_TILE = 512
VMEM_LIMIT = 48 * 1024 * 1024


def _cparams(sem, vmem=VMEM_LIMIT, **kw):
    return pltpu.CompilerParams(dimension_semantics=sem, vmem_limit_bytes=vmem, **kw)


def _inproj_kernel(slope_ref, x_ref, nw_ref, w_ref, qt_ref, ka_ref, vt_ref, xg_ref, *, att_w, s_len):
    i = pl.program_id(0)
    x = x_ref[...]
    tm = x.shape[0]
    ms = jnp.mean(x * x, axis=-1, keepdims=True)
    hn = (x * lax.rsqrt(ms + NORM_EPS) * nw_ref[...]).astype(BF16)
    p = jnp.dot(hn, w_ref[...], preferred_element_type=F32)
    xg_ref[...] = p[:, 3 * att_w:]

    qt = (p[:, :att_w] * (QK_DIM ** -0.5)).T
    ones2 = jnp.where(lax.broadcasted_iota(jnp.int32, (QK_DIM, tm), 0) < 2, 1.0, 0.0)
    pieces = []
    for g in range(2 * N_ATT_HEADS):
        pieces += [qt[g * QK_DIM:(g + 1) * QK_DIM], ones2]
    qt_ref[0] = jnp.concatenate(pieces, axis=0).astype(BF16)

    lane = lax.broadcasted_iota(jnp.int32, (tm, HEAD_DIM), 1)
    j = (i * tm) % s_len + lax.broadcasted_iota(jnp.int32, (tm, HEAD_DIM), 0)
    j_lo = (j & 255).astype(F32)
    j_hi = (j - (j & 255)).astype(F32)
    vtt = p[:, 2 * att_w:3 * att_w].T
    ones_rows = jnp.where(lax.broadcasted_iota(jnp.int32, (V_ROWS - HEAD_DIM, tm), 0) == 0, 1.0, 0.0)
    for h in range(N_ATT_HEADS):
        slope = slope_ref[h]
        kk = p[:, att_w + h * HEAD_DIM:att_w + (h + 1) * HEAD_DIM]
        aug = jnp.where(lane == QK_DIM, slope * j_hi, jnp.where(lane == QK_DIM + 1, slope * j_lo, 0.0))
        ka_ref[:, 2 * h * HEAD_DIM:(2 * h + 1) * HEAD_DIM] = jnp.where(lane < QK_DIM, kk, aug).astype(BF16)
        ka_ref[:, (2 * h + 1) * HEAD_DIM:(2 * h + 2) * HEAD_DIM] = jnp.where(
            lane < QK_DIM, pltpu.roll(kk, QK_DIM, axis=1), aug).astype(BF16)
        vt_ref[0, h * V_ROWS:h * V_ROWS + HEAD_DIM, :] = vtt[h * HEAD_DIM:(h + 1) * HEAD_DIM].astype(BF16)
        vt_ref[0, h * V_ROWS + HEAD_DIM:(h + 1) * V_ROWS, :] = ones_rows.astype(BF16)


def _alibi_slopes():
    nh = N_ATT_HEADS
    return jnp.asarray(np.array([2.0 ** (-8.0 * (i + 1) / nh) for i in range(nh)], dtype=np.float32))


def _in_proj(xf, norm_w, w_in_bf, att_w, s_len):
    t, d = xf.shape
    n = w_in_bf.shape[1]
    tm = min(ATT_TILE, t)
    nh = N_ATT_HEADS
    return pl.pallas_call(
        functools.partial(_inproj_kernel, att_w=att_w, s_len=s_len),
        grid=(t // tm,),
        in_specs=[pl.BlockSpec(memory_space=pltpu.SMEM),
                  pl.BlockSpec((tm, d), lambda i: (i, 0)),
                  pl.BlockSpec((1, d), lambda i: (0, 0)),
                  pl.BlockSpec((d, n), lambda i: (0, 0))],
        out_specs=[pl.BlockSpec((1, 2 * att_w, tm), lambda i: (i, 0, 0)),
                   pl.BlockSpec((tm, 2 * att_w), lambda i: (i, 0)),
                   pl.BlockSpec((1, nh * V_ROWS, tm), lambda i: (i, 0, 0)),
                   pl.BlockSpec((tm, n - 3 * att_w), lambda i: (i, 0))],
        out_shape=[jax.ShapeDtypeStruct((t // tm, 2 * att_w, tm), BF16),
                   jax.ShapeDtypeStruct((t, 2 * att_w), BF16),
                   jax.ShapeDtypeStruct((t // tm, nh * V_ROWS, tm), BF16),
                   jax.ShapeDtypeStruct((t, n - 3 * att_w), F32)],
        compiler_params=_cparams(("parallel",)),
        name="in_proj",
    )(_alibi_slopes(), xf, norm_w.reshape(1, d), w_in_bf)


def _attn_kernel(lam_ref, hw_ref, q_ref, k_ref, vt, o_ref, sb, mx, acc, *, tq, n_heads, lambda_init):
    qi = pl.program_id(2)
    n_maps = 2 * n_heads
    mx[...] = jnp.full(mx.shape, NEG_BIG, F32)
    acc[...] = jnp.zeros(acc.shape, F32)

    def values(c, n, lanes=slice(None)):
        return vt[c, (n // 2) * V_ROWS:(n // 2 + 1) * V_ROWS, lanes]

    def scores(c, slot):
        rows = pl.ds(pl.multiple_of(c * tq, tq), tq)
        for n in range(n_maps):
            sb[n, slot] = jnp.dot(k_ref[0, rows, n * HEAD_DIM:(n + 1) * HEAD_DIM],
                                  q_ref[0, n * HEAD_DIM:(n + 1) * HEAD_DIM, :],
                                  preferred_element_type=F32)

    def softmax_pv(c, slot):
        for n in range(n_maps):
            s = sb[n, slot]
            m_prev = mx[n]
            m_new = jnp.maximum(m_prev, jnp.max(s, axis=0, keepdims=True))
            p = jnp.exp(s - m_new).astype(BF16)
            acc[n] = jnp.exp(m_prev - m_new) * acc[n] + jnp.dot(values(c, n), p, preferred_element_type=F32)
            mx[n] = m_new

    def softmax_pv_diagonal(c, slot):
        hq = tq // 2
        keep_t = (lax.broadcasted_iota(jnp.int32, (hq, tq), 0) <= lax.broadcasted_iota(jnp.int32, (hq, tq), 1))
        keep_b = (lax.broadcasted_iota(jnp.int32, (hq, hq), 0) <= lax.broadcasted_iota(jnp.int32, (hq, hq), 1))
        for n in range(n_maps):
            top = jnp.where(keep_t, sb[n, slot, :hq, :], NEG_BIG)
            bot = jnp.where(keep_b, sb[n, slot, hq:, hq:], NEG_BIG)
            mt = jnp.max(top, axis=0, keepdims=True)
            mb = jnp.max(bot, axis=0, keepdims=True)
            m_prev = mx[n]
            m_new = jnp.maximum(m_prev, jnp.concatenate([mt[:, :hq], jnp.maximum(mt[:, hq:], mb)], axis=1))
            p_top = jnp.exp(top - m_new).astype(BF16)
            p_bot = jnp.exp(bot - m_new[:, hq:]).astype(BF16)
            acc[n] = (jnp.exp(m_prev - m_new) * acc[n]
                      + jnp.dot(values(c, n, slice(0, hq)), p_top, preferred_element_type=F32))
            acc[n, :, hq:] += jnp.dot(values(c, n, slice(hq, tq)), p_bot, preferred_element_type=F32)
            mx[n] = m_new

    scores(0, 0)

    def body(j, c):
        scores(2 * j + 1, 1)
        softmax_pv(2 * j, 0)
        scores(2 * j + 2, 0)
        softmax_pv(2 * j + 1, 1)
        return c

    lax.fori_loop(0, qi // 2, body, 0)

    @pl.when(qi % 2 == 0)
    def _():
        softmax_pv_diagonal(qi, 0)

    @pl.when(qi % 2 == 1)
    def _():
        scores(qi, 1)
        softmax_pv(qi - 1, 0)
        softmax_pv_diagonal(qi, 1)

    lam = (jnp.exp(jnp.sum(lam_ref[0:1, :] * lam_ref[1:2, :], axis=1, keepdims=True))
           - jnp.exp(jnp.sum(lam_ref[2:3, :] * lam_ref[3:4, :], axis=1, keepdims=True))
           + lambda_init)
    for hh in range(n_heads):
        o1 = acc[2 * hh, :HEAD_DIM, :] * (1.0 / acc[2 * hh, HEAD_DIM:HEAD_DIM + 1, :])
        o2 = acc[2 * hh + 1, :HEAD_DIM, :] * (1.0 / acc[2 * hh + 1, HEAD_DIM:HEAD_DIM + 1, :])
        o = o1 - lam * o2
        o = o * lax.rsqrt(jnp.mean(o * o, axis=0, keepdims=True) + HEAD_NORM_EPS)
        o_ref[0, :, hh * HEAD_DIM:(hh + 1) * HEAD_DIM] = (
            o.T * hw_ref[...] * (1.0 - lambda_init)).astype(o_ref.dtype)


def _diff_attention(qt, ka, vt, lam_params, head_norm_w, lambda_init, b, s):
    nh = N_ATT_HEADS
    hp = ATT_HEADS_PER_STEP
    tq = qt.shape[2]
    nq = s // tq
    return pl.pallas_call(
        functools.partial(_attn_kernel, tq=tq, n_heads=hp, lambda_init=lambda_init),
        grid=(b, nh // hp, nq),
        in_specs=[pl.BlockSpec((4, QK_DIM), lambda bi, hi, qi: (0, 0)),
                  pl.BlockSpec((1, HEAD_DIM), lambda bi, hi, qi: (0, 0)),
                  pl.BlockSpec((1, hp * 2 * HEAD_DIM, tq), lambda bi, hi, qi: (bi * nq + qi, hi, 0)),
                  pl.BlockSpec((1, s, hp * 2 * HEAD_DIM), lambda bi, hi, qi: (bi, 0, hi)),
                  pl.BlockSpec((nq, hp * V_ROWS, tq), lambda bi, hi, qi: (bi, hi, 0))],
        out_specs=pl.BlockSpec((1, tq, hp * HEAD_DIM), lambda bi, hi, qi: (bi, qi, hi)),
        out_shape=jax.ShapeDtypeStruct((b, s, nh * HEAD_DIM), BF16),
        scratch_shapes=[pltpu.VMEM((2 * hp, 2, tq, tq), F32), pltpu.VMEM((2 * hp, 1, tq), F32),
                        pltpu.VMEM((2 * hp, V_ROWS, tq), F32)],
        compiler_params=_cparams(("parallel", "parallel", "arbitrary"), vmem=56 * 1024 * 1024),
        name="diff_attn",
    )(lam_params, head_norm_w.reshape(1, HEAD_DIM), qt, ka.reshape(b, s, ka.shape[1]), vt)


def _gelu_tanh(x):
    return 0.5 * x * (1.0 + jnp.tanh(math.sqrt(2.0 / math.pi) * (x + 0.044715 * (x * x * x))))


def _rglru_kernel(xr_ref, gr_ref, cw_ref, cb_ref, w_ref, b_ref, lam_ref, o_ref,
                  xs, carry_h, a_s, u_s, *, ts, ch, c_w):
    si = pl.program_id(1)

    @pl.when(si == 0)
    def _():
        xs[0:8, :] = jnp.zeros((8, c_w), F32)
        carry_h[...] = jnp.zeros(carry_h.shape, F32)

    xs[8:, :] = xr_ref[0]
    neg_lam = -lam_ref[...]
    sp = jnp.maximum(neg_lam, 0.0) + jnp.log1p(jnp.exp(-jnp.abs(neg_lam)))
    cw = cw_ref[...]
    cb = cb_ref[...]
    bias = b_ref[...]
    r8 = lax.broadcasted_iota(jnp.int32, (ch // SUBLANES, SUBLANES, c_w), 1)

    def chunk(c, carry):
        r0 = pl.multiple_of(c * ch, ch)
        win = xs[pl.ds(r0, ch + 8), :]
        xc = cw[3:4, :] * win[8:] + cb
        for k in (1, 2, 3):
            xc = xc + cw[3 - k:4 - k, :] * pltpu.roll(win, k, axis=0)[8:]
        z = jnp.dot(xc.astype(BF16), w_ref[...], preferred_element_type=F32) + bias
        r = jax.nn.sigmoid(z[:, :c_w])
        ig = jax.nn.sigmoid(z[:, c_w:])
        log_a = (-LRU_C) * r * sp
        a = jnp.exp(log_a)
        w = jnp.tanh(-log_a) * (1.0 + a * a)
        u = jnp.where(w > 0.0, w * lax.rsqrt(w), 0.0) * ig * xc
        a = a.reshape(ch // SUBLANES, SUBLANES, c_w)
        u = u.reshape(ch // SUBLANES, SUBLANES, c_w)
        for k in (1, 2, 4):
            a_sh = pltpu.roll(a, k, axis=1)
            u_sh = pltpu.roll(u, k, axis=1)
            ok = r8 >= k
            u = jnp.where(ok, u + a * u_sh, u)
            a = jnp.where(ok, a * a_sh, a)
        a_s[pl.ds(r0, ch), :] = a.reshape(ch, c_w)
        u_s[pl.ds(r0, ch), :] = u.reshape(ch, c_w)
        return carry

    lax.fori_loop(0, ts // ch, chunk, 0)

    def grp(g, hprev):
        r0 = pl.multiple_of(g * 8, 8)
        hg = u_s[pl.ds(r0, 8), :] + a_s[pl.ds(r0, 8), :] * hprev
        u_s[pl.ds(r0, 8), :] = hg
        return hg[7:8, :]

    hlast = lax.fori_loop(0, ts // 8, grp, carry_h[0:1, :], unroll=8)
    carry_h[0:1, :] = hlast
    xs[0:8, :] = xs[ts:ts + 8, :]
    o_ref[0] = (u_s[...] * _gelu_tanh(gr_ref[0])).astype(o_ref.dtype)


def _rglru(xg, conv_w, conv_b, w_bd, b_cat, lru_lambda):
    b, s, w2 = xg.shape
    c_w = w2 // 2
    ts = min(LRU_TILE, s)
    ch = min(LRU_CHUNK, ts)
    return pl.pallas_call(
        functools.partial(_rglru_kernel, ts=ts, ch=ch, c_w=c_w),
        grid=(b, s // ts),
        in_specs=[pl.BlockSpec((1, ts, c_w), lambda bi, si: (bi, si, 0)),
                  pl.BlockSpec((1, ts, c_w), lambda bi, si: (bi, si, 1)),
                  pl.BlockSpec((CONV_WIDTH, c_w), lambda bi, si: (0, 0)),
                  pl.BlockSpec((1, c_w), lambda bi, si: (0, 0)),
                  pl.BlockSpec((c_w, 2 * c_w), lambda bi, si: (0, 0)),
                  pl.BlockSpec((1, 2 * c_w), lambda bi, si: (0, 0)),
                  pl.BlockSpec((1, c_w), lambda bi, si: (0, 0))],
        out_specs=pl.BlockSpec((1, ts, c_w), lambda bi, si: (bi, si, 0)),
        out_shape=jax.ShapeDtypeStruct((b, s, c_w), BF16),
        scratch_shapes=[pltpu.VMEM((ts + 8, c_w), F32), pltpu.VMEM((8, c_w), F32),
                        pltpu.VMEM((ts, c_w), F32), pltpu.VMEM((ts, c_w), F32)],
        compiler_params=_cparams(("parallel", "arbitrary")),
        name="rglru",
    )(xg, xg, conv_w, conv_b.reshape(1, c_w), w_bd, b_cat.reshape(1, 2 * c_w), lru_lambda.reshape(1, c_w))


def _outproj_kernel(att_ref, rnn_ref, x_ref, wo_ref, nw_ref, wr_ref, br_ref,
                    x1_ref, hn_ref, route_ref, n8_ref, *, att_w):
    y = jnp.dot(att_ref[...], wo_ref[:att_w, :], preferred_element_type=F32)
    y = y + jnp.dot(rnn_ref[...], wo_ref[att_w:, :], preferred_element_type=F32)
    x1 = x_ref[...] + y
    x1_ref[...] = x1
    hn = (x1 * lax.rsqrt(jnp.mean(x1 * x1, axis=-1, keepdims=True) + NORM_EPS) * nw_ref[...]).astype(BF16)
    hn_ref[...] = hn
    lg = jnp.dot(hn, wr_ref[...], preferred_element_type=F32) + br_ref[...]
    tm = lg.shape[0]

    col = lax.broadcasted_iota(jnp.int32, lg.shape, 1)
    colf = col.astype(F32)
    big = float(LANES)
    ninf = -jnp.inf
    is_g = col < N_GROUPS
    lgm = jnp.where(is_g, lg, ninf)
    mg = jnp.max(lgm, axis=1, keepdims=True)
    g_sel = jnp.min(jnp.where(lgm == mg, colf, big), axis=1, keepdims=True)
    pg = 1.0 / jnp.sum(jnp.where(is_g, jnp.exp(lgm - mg), 0.0), axis=1, keepdims=True)
    lo = N_GROUPS + EXPERTS_PER_GROUP * g_sel
    in_grp = (colf >= lo) & (colf < lo + EXPERTS_PER_GROUP)
    lem = jnp.where(in_grp, lg, ninf)
    v1 = jnp.max(lem, axis=1, keepdims=True)
    i1 = jnp.min(jnp.where(lem == v1, colf, big), axis=1, keepdims=True)
    lem2 = jnp.where(colf == i1, ninf, lem)
    v2 = jnp.max(lem2, axis=1, keepdims=True)
    i2 = jnp.min(jnp.where(lem2 == v2, colf, big), axis=1, keepdims=True)
    e2 = jnp.exp(v2 - v1)
    den = 1.0 + e2
    g1 = pg / den
    g2 = pg * e2 / den

    oh1 = jnp.where(colf == i1, 1.0, 0.0)
    oh2 = jnp.where(colf == i2, 1.0, 0.0)
    oh = oh1 + oh2
    earlier = (lax.broadcasted_iota(jnp.int32, (tm, tm), 1)
               < lax.broadcasted_iota(jnp.int32, (tm, tm), 0)).astype(BF16)
    pref = jnp.dot(earlier, oh.astype(BF16), preferred_element_type=F32)
    cnt = jnp.sum(oh, axis=0, keepdims=True)
    n8 = jnp.floor((cnt + (SUBLANES - 1)) * (1.0 / SUBLANES))
    before = (lax.broadcasted_iota(jnp.int32, (LANES, LANES), 0)
              < lax.broadcasted_iota(jnp.int32, (LANES, LANES), 1)).astype(BF16)
    loff8 = jnp.dot(jnp.broadcast_to(n8, (SUBLANES, LANES)).astype(BF16), before,
                    preferred_element_type=F32)[0:1]
    pos = SUBLANES * loff8 + pref
    lp1 = jnp.sum(oh1 * pos, axis=1, keepdims=True)
    lp2 = jnp.sum(oh2 * pos, axis=1, keepdims=True)
    route_ref[...] = jnp.where(col == 0, g1,
                     jnp.where(col == 1, g2,
                     jnp.where(col == 2, lp1,
                     jnp.where(col == 3, lp2, 0.0))))
    n8_ref[0] = n8


def _out_proj(att, rnn, xf, w_out_bf, norm_w, w_route_bf, b_route):
    t, d = xf.shape
    att_w = att.shape[1]
    tm = min(ROW_TILE, t)
    row = lambda i: (i, 0)
    fix = lambda i: (0, 0)
    return pl.pallas_call(
        functools.partial(_outproj_kernel, att_w=att_w),
        grid=(t // tm,),
        in_specs=[pl.BlockSpec((tm, att_w), row), pl.BlockSpec((tm, rnn.shape[1]), row),
                  pl.BlockSpec((tm, d), row), pl.BlockSpec(w_out_bf.shape, fix),
                  pl.BlockSpec((1, d), fix), pl.BlockSpec((d, LANES), fix), pl.BlockSpec((1, LANES), fix)],
        out_specs=[pl.BlockSpec((tm, d), row), pl.BlockSpec((tm, d), row), pl.BlockSpec((tm, LANES), row),
                   pl.BlockSpec((1, 1, LANES), lambda i: (i, 0, 0))],
        out_shape=[jax.ShapeDtypeStruct((t, d), F32), jax.ShapeDtypeStruct((t, d), BF16),
                   jax.ShapeDtypeStruct((t, LANES), F32),
                   jax.ShapeDtypeStruct((t // tm, 1, LANES), F32)],
        compiler_params=_cparams(("parallel",)),
        name="out_proj",
    )(att, rnn, xf, w_out_bf, norm_w.reshape(1, d), w_route_bf, b_route)


def _local_rows(tm):
    return -(-(TOP_K * tm + N_EXPERTS * (SUBLANES - 1)) // LANES) * LANES


def _segment_tables(n8_tiles, tm_moe, n_tiles):
    n8 = n8_tiles[:, 0, N_GROUPS:N_GROUPS + N_EXPERTS].astype(jnp.int32)
    c8 = n8 * SUBLANES
    loff = jnp.cumsum(c8, axis=1) - c8
    gtot = jnp.sum(c8, axis=0)
    gpad = (gtot + tm_moe - 1) // tm_moe * tm_moe
    gend = jnp.cumsum(gpad)
    gstart = gend - gpad
    gbase = gstart[None, :] + jnp.cumsum(c8, axis=0) - c8
    tile_row0 = jnp.arange(n_tiles, dtype=jnp.int32) * tm_moe
    tile_e = jnp.minimum(jnp.sum((gend[None, :] <= tile_row0[:, None]).astype(jnp.int32), axis=1),
                         N_EXPERTS - 1).astype(jnp.int32)
    n_used = (gend[-1] // tm_moe).astype(jnp.int32).reshape(1)
    tail_start = (gstart + gtot).astype(jnp.int32)
    tail_n8 = ((gpad - gtot) // SUBLANES).astype(jnp.int32)
    after = gend[tile_e] // tm_moe
    next_e = jnp.where(after < n_used[0], tile_e[jnp.minimum(after, n_tiles - 1)], -1).astype(jnp.int32)
    first = jnp.concatenate([jnp.ones((1,), jnp.int32), (tile_e[1:] != tile_e[:-1]).astype(jnp.int32)])
    w_slot = ((jnp.cumsum(first) - 1) % 2).astype(jnp.int32)
    return (n8.reshape(-1), loff.reshape(-1).astype(jnp.int32), gbase.reshape(-1).astype(jnp.int32),
            tile_e, n_used, tail_start, tail_n8, next_e, w_slot)


def _segment_copies(n8_ref, src_off_ref, dst_off_ref, src, dst, sem, tile, wait):
    def rows_of(e):
        return pl.multiple_of(n8_ref[tile * N_EXPERTS + e] * SUBLANES, SUBLANES)

    if wait:
        total = lax.fori_loop(0, N_EXPERTS, lambda e, acc: acc + rows_of(e), 0)
        total = pl.multiple_of(total, SUBLANES)
        pltpu.make_async_copy(src.at[pl.ds(0, total), :], dst.at[pl.ds(0, total), :], sem).wait()
        return

    def per_expert(e, c):
        k = tile * N_EXPERTS + e
        rows = rows_of(e)

        @pl.when(rows > 0)
        def _():
            pltpu.make_async_copy(
                src.at[pl.ds(pl.multiple_of(src_off_ref[k], SUBLANES), rows), :],
                dst.at[pl.ds(pl.multiple_of(dst_off_ref[k], SUBLANES), rows), :], sem).start()
        return c
    lax.fori_loop(0, N_EXPERTS, per_expert, 0)


def _pack_bf16_pairs(x):
    n = x.shape[1] // 2
    bits = lax.bitcast_convert_type(x, jnp.uint32)
    return (bits[:, :n] >> 16) | (bits[:, n:] & jnp.uint32(0xFFFF0000))


def _unpack_bf16_pairs(p):
    lo = lax.bitcast_convert_type(p << 16, F32)
    hi = lax.bitcast_convert_type(p & jnp.uint32(0xFFFF0000), F32)
    return jnp.concatenate([lo, hi], axis=1).astype(BF16)


def _dispatch_kernel(n8_ref, loff_ref, gbase_ref, tstart_ref, tn8_ref, nu_ref, hn_ref, route_ref, xs_hbm,
                     stage, zbuf, sem, zsem, *, lcap, n_tt):
    i = pl.program_id(0)
    slot = i % 2
    tm = hn_ref.shape[0]
    tm_moe = zbuf.shape[0]
    n_tiles = xs_hbm.shape[0] // tm_moe

    def tail_copies(wait):
        def go(cp):
            if wait:
                cp.wait()
            else:
                cp.start()

        def per_expert(e, c):
            rows = pl.multiple_of(tn8_ref[e] * SUBLANES, SUBLANES)

            @pl.when(rows > 0)
            def _():
                go(pltpu.make_async_copy(
                    zbuf.at[pl.ds(0, rows), :],
                    xs_hbm.at[pl.ds(pl.multiple_of(tstart_ref[e], SUBLANES), rows), :], zsem.at[0]))
            return c
        lax.fori_loop(0, N_EXPERTS, per_expert, 0)

        def per_unused_tile(j, c):
            go(pltpu.make_async_copy(zbuf, xs_hbm.at[pl.ds(pl.multiple_of(j * tm_moe, tm_moe), tm_moe), :],
                                     zsem.at[0]))
            return c
        lax.fori_loop(nu_ref[0], n_tiles, per_unused_tile, 0)

    @pl.when(i == 0)
    def _():
        zbuf[...] = jnp.zeros(zbuf.shape, zbuf.dtype)
        tail_copies(False)

    @pl.when(i >= 2)
    def _():
        _segment_copies(n8_ref, loff_ref, gbase_ref, stage.at[slot], xs_hbm, sem.at[slot], i - 2, True)

    lp1 = route_ref[:, 2:3]
    lp2 = route_ref[:, 3:4]
    cpos = lax.broadcasted_iota(jnp.int32, (tm, lcap), 1).astype(F32)
    sel_t = jnp.where((cpos == lp1) | (cpos == lp2), 1.0, 0.0)
    stage[slot] = _pack_bf16_pairs(jnp.dot(sel_t.T.astype(BF16), hn_ref[...], preferred_element_type=F32))
    _segment_copies(n8_ref, loff_ref, gbase_ref, stage.at[slot], xs_hbm, sem.at[slot], i, False)

    @pl.when(i == n_tt - 1)
    def _():
        _segment_copies(n8_ref, loff_ref, gbase_ref, stage.at[slot], xs_hbm, sem.at[slot], i, True)
        if n_tt > 1:
            _segment_copies(n8_ref, loff_ref, gbase_ref, stage.at[1 - slot], xs_hbm, sem.at[1 - slot],
                            i - 1, True)
        tail_copies(True)


def _dispatch(hn, route, tables, n_rows, tm_moe):
    t, d = hn.shape
    tm = min(ROW_TILE, t)
    n_tt = t // tm
    lcap = _local_rows(tm)
    n8, loff, gbase, _, n_used, tail_start, tail_n8 = tables[:7]
    grid_spec = pltpu.PrefetchScalarGridSpec(
        num_scalar_prefetch=6,
        grid=(n_tt,),
        in_specs=[pl.BlockSpec((tm, d), lambda i, *_: (i, 0)),
                  pl.BlockSpec((tm, LANES), lambda i, *_: (i, 0))],
        out_specs=pl.BlockSpec(memory_space=pl.ANY),
        scratch_shapes=[pltpu.VMEM((2, lcap, d // 2), jnp.uint32), pltpu.VMEM((tm_moe, d // 2), jnp.uint32),
                        pltpu.SemaphoreType.DMA((2,)), pltpu.SemaphoreType.DMA((1,))],
    )
    return pl.pallas_call(
        functools.partial(_dispatch_kernel, lcap=lcap, n_tt=n_tt),
        grid_spec=grid_spec,
        out_shape=jax.ShapeDtypeStruct((n_rows, d // 2), jnp.uint32),
        compiler_params=_cparams(("arbitrary",), has_side_effects=True),
        name="dispatch",
    )(n8, loff, gbase, tail_start, tail_n8, n_used, hn, route)


def _moe_kernel(te_ref, nu_ref, nxt_ref, wslot_ref, xs_ref, wg_hbm, wu_hbm, wd_hbm, y_ref,
                wgf, wuf, wdf, wgb, wub, wdb, wsem):
    i = pl.program_id(0)

    def weight_copies(e, sl):
        return (pltpu.make_async_copy(wg_hbm.at[e], wgf.at[sl], wsem.at[sl, 0]),
                pltpu.make_async_copy(wu_hbm.at[e], wuf.at[sl], wsem.at[sl, 1]),
                pltpu.make_async_copy(wd_hbm.at[e], wdf.at[sl], wsem.at[sl, 2]))

    @pl.when(i == 0)
    def _():
        for cp in weight_copies(te_ref[0], wslot_ref[0]):
            cp.start()

    @pl.when(i < nu_ref[0])
    def _():
        changed = jnp.logical_or(i == 0, te_ref[i] != te_ref[jnp.maximum(i - 1, 0)])

        @pl.when(changed)
        def _():
            sl = wslot_ref[i]
            for cp in weight_copies(te_ref[i], sl):
                cp.wait()
            wgb[...] = wgf[sl].astype(BF16)
            wub[...] = wuf[sl].astype(BF16)
            wdb[...] = wdf[sl].astype(BF16)

            @pl.when(nxt_ref[i] >= 0)
            def _():
                for cp in weight_copies(nxt_ref[i], 1 - sl):
                    cp.start()

        x = _unpack_bf16_pairs(xs_ref[...])
        g = jnp.dot(x, wgb[...], preferred_element_type=F32)
        u = jnp.dot(x, wub[...], preferred_element_type=F32)
        hdn = (g * jax.nn.sigmoid(g) * u).astype(BF16)
        y = jnp.dot(hdn, wdb[...], preferred_element_type=F32)
        y_ref[...] = _pack_bf16_pairs(y.astype(BF16).astype(F32))

    @pl.when(i >= nu_ref[0])
    def _():
        y_ref[...] = jnp.zeros(y_ref.shape, y_ref.dtype)


def _moe(xs, tile_e, n_used, next_e, w_slot, w_g, w_u, w_d, tm):
    n_rows = xs.shape[0]
    d = w_g.shape[1]
    dp = xs.shape[1]
    n_tiles = n_rows // tm
    ff = w_g.shape[2]
    row_blk = lambda i, te, nu, *_: (jnp.minimum(i, nu[0] - 1), 0)
    hbm = pl.BlockSpec(memory_space=pl.ANY)
    grid_spec = pltpu.PrefetchScalarGridSpec(
        num_scalar_prefetch=4,
        grid=(n_tiles,),
        in_specs=[pl.BlockSpec((tm, dp), row_blk), hbm, hbm, hbm],
        out_specs=pl.BlockSpec((tm, dp), lambda i, *_: (i, 0)),
        scratch_shapes=[pltpu.VMEM((2, d, ff), F32), pltpu.VMEM((2, d, ff), F32), pltpu.VMEM((2, ff, d), F32),
                        pltpu.VMEM((d, ff), BF16), pltpu.VMEM((d, ff), BF16), pltpu.VMEM((ff, d), BF16),
                        pltpu.SemaphoreType.DMA((2, 3))],
    )
    return pl.pallas_call(
        _moe_kernel,
        grid_spec=grid_spec,
        out_shape=jax.ShapeDtypeStruct((n_rows, dp), jnp.uint32),
        compiler_params=_cparams(("arbitrary",)),
        name="moe",
    )(tile_e, n_used, next_e, w_slot, xs, w_g, w_u, w_d)


def _combine_kernel(n8_ref, loff_ref, gbase_ref, x1_ref, route_ref, nw_ref, y_hbm, o_ref,
                    ybuf, sem, *, lcap, n_tt):
    i = pl.program_id(0)
    slot = i % 2
    tm = x1_ref.shape[0]

    def fetch(tile, sl, wait):
        _segment_copies(n8_ref, gbase_ref, loff_ref, y_hbm, ybuf.at[sl], sem.at[sl], tile, wait)

    @pl.when(i == 0)
    def _():
        ybuf[...] = jnp.zeros(ybuf.shape, ybuf.dtype)
        fetch(0, 0, False)

    @pl.when(i + 1 < n_tt)
    def _():
        fetch(i + 1, 1 - slot, False)

    fetch(i, slot, True)
    g1 = route_ref[:, 0:1]
    g2 = route_ref[:, 1:2]
    lp1 = route_ref[:, 2:3]
    lp2 = route_ref[:, 3:4]
    cpos = lax.broadcasted_iota(jnp.int32, (tm, lcap), 1).astype(F32)
    gsel = (jnp.where(cpos == lp1, g1, 0.0) + jnp.where(cpos == lp2, g2, 0.0)).astype(BF16)
    moe = jnp.dot(gsel, _unpack_bf16_pairs(ybuf[slot]), preferred_element_type=F32)
    x = x1_ref[...] + moe
    o_ref[...] = x * lax.rsqrt(jnp.mean(x * x, axis=-1, keepdims=True) + NORM_EPS) * nw_ref[...]


def _combine(x1, y, route, norm_w, tables):
    t, d = x1.shape
    tm = min(ROW_TILE, t)
    n_tt = t // tm
    lcap = _local_rows(tm)
    n8, loff, gbase = tables[:3]
    grid_spec = pltpu.PrefetchScalarGridSpec(
        num_scalar_prefetch=3,
        grid=(n_tt,),
        in_specs=[pl.BlockSpec((tm, d), lambda i, *_: (i, 0)),
                  pl.BlockSpec((tm, LANES), lambda i, *_: (i, 0)),
                  pl.BlockSpec((1, d), lambda i, *_: (0, 0)),
                  pl.BlockSpec(memory_space=pl.ANY)],
        out_specs=pl.BlockSpec((tm, d), lambda i, *_: (i, 0)),
        scratch_shapes=[pltpu.VMEM((2, lcap, d // 2), jnp.uint32), pltpu.SemaphoreType.DMA((2,))],
    )
    return pl.pallas_call(
        functools.partial(_combine_kernel, lcap=lcap, n_tt=n_tt),
        grid_spec=grid_spec,
        out_shape=jax.ShapeDtypeStruct((t, d), F32),
        compiler_params=_cparams(("arbitrary",)),
        name="combine",
    )(n8, loff, gbase, x1, route, norm_w.reshape(1, d), y)


def _block_diag(w):
    n, bi, bj = w.shape
    eye = jnp.eye(n, dtype=w.dtype)
    return jnp.einsum('nij,nm->nimj', w, eye).reshape(n * bi, n * bj)


def kernel(x, mix_norm_w, w_in, lambda_q1, lambda_k1, lambda_q2, lambda_k2, head_norm_w, conv_w, conv_b, w_rgate, b_rgate, w_igate, b_igate, lru_lambda, w_out, ffn_norm_w, w_router_group, b_router_group, w_router_expert, b_router_expert, w_exp_gate, w_exp_up, w_exp_down, final_norm_w):
    b, s, d = x.shape
    t = b * s
    assert w_in.shape[0] == 1, "single-layer stack only"
    att_w = N_ATT_HEADS * HEAD_DIM
    tm_moe = MOE_TILE
    xf = x.reshape(t, d)
    for l in range(1):
        lambda_init = 0.8 - 0.6 * math.exp(-0.3 * l)
        assert s % ATT_TILE == 0, "sequence length must be a multiple of the attention tile"
        qt, ka, vt, xg = _in_proj(xf, mix_norm_w[l], w_in[l].astype(BF16), att_w, s)
        lam_params = jnp.stack([lambda_q1[l], lambda_k1[l], lambda_q2[l], lambda_k2[l]]).astype(F32)
        att = _diff_attention(qt, ka, vt, lam_params, head_norm_w[l], lambda_init, b, s)
        w_bd = jnp.concatenate([_block_diag(w_rgate[l]), _block_diag(w_igate[l])], axis=1).astype(BF16)
        b_cat = jnp.concatenate([b_rgate[l], b_igate[l]])
        rnn = _rglru(xg.reshape(b, s, xg.shape[1]), conv_w[l], conv_b[l], w_bd, b_cat, lru_lambda[l])
        w_route = jnp.concatenate([w_router_group[l], w_router_expert[l]], axis=1)
        w_route = jnp.pad(w_route, ((0, 0), (0, LANES - w_route.shape[1]))).astype(BF16)
        b_route = jnp.concatenate([b_router_group[l], b_router_expert[l]])
        b_route = jnp.pad(b_route, (0, LANES - b_route.shape[0])).reshape(1, LANES).astype(F32)
        x1, hn, route, n8_tiles = _out_proj(att.reshape(t, att_w), rnn.reshape(t, -1), xf, w_out[l].astype(BF16),
                                            ffn_norm_w[l], w_route, b_route)
        n_tt = n8_tiles.shape[0]
        max_rows = TOP_K * t + n_tt * N_EXPERTS * (SUBLANES - 1) + N_EXPERTS * (tm_moe - 1)
        n_tiles = -(-max_rows // tm_moe)
        tables = _segment_tables(n8_tiles, tm_moe, n_tiles)
        xs = _dispatch(hn, route, tables, n_tiles * tm_moe, tm_moe)
        y = _moe(xs, tables[3], tables[4], tables[7], tables[8], w_exp_gate[l], w_exp_up[l], w_exp_down[l], tm_moe)
        out = _combine(x1, y, route, final_norm_w, tables)
    return out.reshape(b, s, d)
```

```python
import functools
import math

import numpy as np
import jax
import jax.numpy as jnp
from jax import lax
from jax.experimental import pallas as pl
from jax.experimental.pallas import tpu as pltpu

F32 = jnp.float32
BF16 = jnp.bfloat16

N_ATT_HEADS = 4
HEAD_DIM = 128
QK_DIM = 64
N_RNN_BLOCKS = 8
CONV_WIDTH = 4
LRU_C = 8.0
N_GROUPS = 4
EXPERTS_PER_GROUP = 8
N_EXPERTS = N_GROUPS * EXPERTS_PER_GROUP
TOP_K = 2
NORM_EPS = 1e-6
HEAD_NORM_EPS = 1e-5
LANES = 128
SUBLANES = 8
NEG_BIG = -1e30

ROW_TILE = 512
ATT_TILE = 512
ATT_HEADS_PER_STEP = 4
V_ROWS = HEAD_DIM + 16
LRU_TILE = 512
LRU_CHUNK = 128
MOE_TILE = 512
VMEM_LIMIT = 48 * 1024 * 1024


def _cparams(sem, vmem=VMEM_LIMIT, **kw):
    return pltpu.CompilerParams(dimension_semantics=sem, vmem_limit_bytes=vmem, **kw)


def _inproj_kernel(slope_ref, x_ref, nw_ref, w_ref, qt_ref, ka_ref, vt_ref, xg_ref, *, att_w, s_len):
    i = pl.program_id(0)
    x = x_ref[...]
    tm = x.shape[0]
    ms = jnp.mean(x * x, axis=-1, keepdims=True)
    hn = (x * lax.rsqrt(ms + NORM_EPS) * nw_ref[...]).astype(BF16)
    c_w = (w_ref.shape[1] - 3 * att_w) // 2
    p_lru = jnp.dot(hn, w_ref[:, 3 * att_w:], preferred_element_type=F32)
    xg_ref[:, :c_w] = p_lru[:, :c_w]
    xg_ref[:, c_w:] = _gelu_tanh(p_lru[:, c_w:])
    p_k = jnp.dot(hn, w_ref[:, att_w:2 * att_w], preferred_element_type=F32)
    p_v = jnp.dot(hn, w_ref[:, 2 * att_w:3 * att_w], preferred_element_type=F32)
    p_q = jnp.dot(hn, w_ref[:, :att_w], preferred_element_type=F32)

    qt = (p_q * (QK_DIM ** -0.5)).T
    ones2 = jnp.where(lax.broadcasted_iota(jnp.int32, (QK_DIM, tm), 0) < 2, 1.0, 0.0)
    pieces = []
    for g in range(2 * N_ATT_HEADS):
        pieces += [qt[g * QK_DIM:(g + 1) * QK_DIM], ones2]
    qt_ref[0] = jnp.concatenate(pieces, axis=0).astype(BF16)

    lane = lax.broadcasted_iota(jnp.int32, (tm, HEAD_DIM), 1)
    j = (i * tm) % s_len + lax.broadcasted_iota(jnp.int32, (tm, HEAD_DIM), 0)
    j_lo = (j & 255).astype(F32)
    j_hi = (j - (j & 255)).astype(F32)
    vtt = p_v.T
    ones_rows = jnp.where(lax.broadcasted_iota(jnp.int32, (V_ROWS - HEAD_DIM, tm), 0) == 0, 1.0, 0.0)
    for h in range(N_ATT_HEADS):
        slope = slope_ref[h]
        kk = p_k[:, h * HEAD_DIM:(h + 1) * HEAD_DIM]
        aug = jnp.where(lane == QK_DIM, slope * j_hi, jnp.where(lane == QK_DIM + 1, slope * j_lo, 0.0))
        ka_ref[:, 2 * h * HEAD_DIM:(2 * h + 1) * HEAD_DIM] = jnp.where(lane < QK_DIM, kk, aug).astype(BF16)
        ka_ref[:, (2 * h + 1) * HEAD_DIM:(2 * h + 2) * HEAD_DIM] = jnp.where(
            lane < QK_DIM, pltpu.roll(kk, QK_DIM, axis=1), aug).astype(BF16)
        vt_ref[0, h * V_ROWS:h * V_ROWS + HEAD_DIM, :] = vtt[h * HEAD_DIM:(h + 1) * HEAD_DIM].astype(BF16)
        vt_ref[0, h * V_ROWS + HEAD_DIM:(h + 1) * V_ROWS, :] = ones_rows.astype(BF16)


def _alibi_slopes():
    nh = N_ATT_HEADS
    return jnp.asarray(np.array([2.0 ** (-8.0 * (i + 1) / nh) for i in range(nh)], dtype=np.float32))


def _in_proj(xf, norm_w, w_in_bf, att_w, s_len):
    t, d = xf.shape
    n = w_in_bf.shape[1]
    tm = min(ATT_TILE, t)
    nh = N_ATT_HEADS
    return pl.pallas_call(
        functools.partial(_inproj_kernel, att_w=att_w, s_len=s_len),
        grid=(t // tm,),
        in_specs=[pl.BlockSpec(memory_space=pltpu.SMEM),
                  pl.BlockSpec((tm, d), lambda i: (i, 0)),
                  pl.BlockSpec((1, d), lambda i: (0, 0)),
                  pl.BlockSpec((d, n), lambda i: (0, 0))],
        out_specs=[pl.BlockSpec((1, 2 * att_w, tm), lambda i: (i, 0, 0)),
                   pl.BlockSpec((tm, 2 * att_w), lambda i: (i, 0)),
                   pl.BlockSpec((1, nh * V_ROWS, tm), lambda i: (i, 0, 0)),
                   pl.BlockSpec((tm, n - 3 * att_w), lambda i: (i, 0))],
        out_shape=[jax.ShapeDtypeStruct((t // tm, 2 * att_w, tm), BF16),
                   jax.ShapeDtypeStruct((t, 2 * att_w), BF16),
                   jax.ShapeDtypeStruct((t // tm, nh * V_ROWS, tm), BF16),
                   jax.ShapeDtypeStruct((t, n - 3 * att_w), F32)],
        compiler_params=_cparams(("parallel",)),
        name="in_proj",
    )(_alibi_slopes(), xf, norm_w.reshape(1, d), w_in_bf)


def _attn_kernel(lam_ref, hw_ref, q_ref, k_ref, vt, o_ref, sb, mx, acc, *, tq, n_heads, lambda_init):
    qi = pl.program_id(2)
    n_maps = 2 * n_heads
    mx[...] = jnp.full(mx.shape, NEG_BIG, F32)
    acc[...] = jnp.zeros(acc.shape, F32)

    def values(c, n, lanes=slice(None)):
        return vt[c, (n // 2) * V_ROWS:(n // 2 + 1) * V_ROWS, lanes]

    def scores(c, slot):
        rows = pl.ds(pl.multiple_of(c * tq, tq), tq)
        for n in range(n_maps):
            sb[n, slot] = jnp.dot(k_ref[0, rows, n * HEAD_DIM:(n + 1) * HEAD_DIM],
                                  q_ref[0, n * HEAD_DIM:(n + 1) * HEAD_DIM, :],
                                  preferred_element_type=F32)

    def softmax_pv(c, slot):
        for n in range(n_maps):
            s = sb[n, slot]
            m_prev = mx[n]
            m_new = jnp.maximum(m_prev, jnp.max(s, axis=0, keepdims=True))
            p = jnp.exp(s - m_new).astype(BF16)
            acc[n] = jnp.exp(m_prev - m_new) * acc[n] + jnp.dot(values(c, n), p, preferred_element_type=F32)
            mx[n] = m_new

    def softmax_pv_diagonal(c, slot):
        hq = tq // 2
        keep_t = (lax.broadcasted_iota(jnp.int32, (hq, tq), 0) <= lax.broadcasted_iota(jnp.int32, (hq, tq), 1))
        keep_b = (lax.broadcasted_iota(jnp.int32, (hq, hq), 0) <= lax.broadcasted_iota(jnp.int32, (hq, hq), 1))
        for n in range(n_maps):
            top = jnp.where(keep_t, sb[n, slot, :hq, :], NEG_BIG)
            bot = jnp.where(keep_b, sb[n, slot, hq:, hq:], NEG_BIG)
            mt = jnp.max(top, axis=0, keepdims=True)
            mb = jnp.max(bot, axis=0, keepdims=True)
            m_prev = mx[n]
            m_new = jnp.maximum(m_prev, jnp.concatenate([mt[:, :hq], jnp.maximum(mt[:, hq:], mb)], axis=1))
            p_top = jnp.exp(top - m_new).astype(BF16)
            p_bot = jnp.exp(bot - m_new[:, hq:]).astype(BF16)
            acc[n] = (jnp.exp(m_prev - m_new) * acc[n]
                      + jnp.dot(values(c, n, slice(0, hq)), p_top, preferred_element_type=F32))
            acc[n, :, hq:] += jnp.dot(values(c, n, slice(hq, tq)), p_bot, preferred_element_type=F32)
            mx[n] = m_new

    scores(0, 0)

    def body(j, c):
        scores(2 * j + 1, 1)
        softmax_pv(2 * j, 0)
        scores(2 * j + 2, 0)
        softmax_pv(2 * j + 1, 1)
        return c

    lax.fori_loop(0, qi // 2, body, 0)

    @pl.when(qi % 2 == 0)
    def _():
        softmax_pv_diagonal(qi, 0)

    @pl.when(qi % 2 == 1)
    def _():
        scores(qi, 1)
        softmax_pv(qi - 1, 0)
        softmax_pv_diagonal(qi, 1)

    lam = (jnp.exp(jnp.sum(lam_ref[0:1, :] * lam_ref[1:2, :], axis=1, keepdims=True))
           - jnp.exp(jnp.sum(lam_ref[2:3, :] * lam_ref[3:4, :], axis=1, keepdims=True))
           + lambda_init)
    for hh in range(n_heads):
        o1 = acc[2 * hh, :HEAD_DIM, :] * (1.0 / acc[2 * hh, HEAD_DIM:HEAD_DIM + 1, :])
        o2 = acc[2 * hh + 1, :HEAD_DIM, :] * (1.0 / acc[2 * hh + 1, HEAD_DIM:HEAD_DIM + 1, :])
        o = o1 - lam * o2
        o = o * lax.rsqrt(jnp.mean(o * o, axis=0, keepdims=True) + HEAD_NORM_EPS)
        o_ref[0, :, hh * HEAD_DIM:(hh + 1) * HEAD_DIM] = (
            o.T * hw_ref[...] * (1.0 - lambda_init)).astype(o_ref.dtype)


def _diff_attention(qt, ka, vt, lam_params, head_norm_w, lambda_init, b, s):
    nh = N_ATT_HEADS
    hp = ATT_HEADS_PER_STEP
    tq = qt.shape[2]
    nq = s // tq
    return pl.pallas_call(
        functools.partial(_attn_kernel, tq=tq, n_heads=hp, lambda_init=lambda_init),
        grid=(b, nh // hp, nq),
        in_specs=[pl.BlockSpec((4, QK_DIM), lambda bi, hi, qi: (0, 0)),
                  pl.BlockSpec((1, HEAD_DIM), lambda bi, hi, qi: (0, 0)),
                  pl.BlockSpec((1, hp * 2 * HEAD_DIM, tq), lambda bi, hi, qi: (bi * nq + qi, hi, 0)),
                  pl.BlockSpec((1, s, hp * 2 * HEAD_DIM), lambda bi, hi, qi: (bi, 0, hi)),
                  pl.BlockSpec((nq, hp * V_ROWS, tq), lambda bi, hi, qi: (bi, hi, 0))],
        out_specs=pl.BlockSpec((1, tq, hp * HEAD_DIM), lambda bi, hi, qi: (bi, qi, hi)),
        out_shape=jax.ShapeDtypeStruct((b, s, nh * HEAD_DIM), BF16),
        scratch_shapes=[pltpu.VMEM((2 * hp, 2, tq, tq), F32), pltpu.VMEM((2 * hp, 1, tq), F32),
                        pltpu.VMEM((2 * hp, V_ROWS, tq), F32)],
        compiler_params=_cparams(("parallel", "parallel", "arbitrary"), vmem=56 * 1024 * 1024),
        name="diff_attn",
    )(lam_params, head_norm_w.reshape(1, HEAD_DIM), qt, ka.reshape(b, s, ka.shape[1]), vt)


def _gelu_tanh(x):
    return 0.5 * x * (1.0 + jnp.tanh(math.sqrt(2.0 / math.pi) * (x + 0.044715 * (x * x * x))))


def _rglru_kernel(xr_ref, gr_ref, cw_ref, cb_ref, w_ref, b_ref, lam_ref, o_ref,
                  xs, carry_h, a_s, u_s, *, ts, ch, c_w):
    si = pl.program_id(1)

    @pl.when(si == 0)
    def _():
        xs[0:8, :] = jnp.zeros((8, c_w), F32)
        carry_h[...] = jnp.zeros(carry_h.shape, F32)

    xs[8:, :] = xr_ref[0]
    neg_lam = -lam_ref[...]
    sp = jnp.maximum(neg_lam, 0.0) + jnp.log1p(jnp.exp(-jnp.abs(neg_lam)))
    cw = cw_ref[...]
    cb = cb_ref[...]
    bias = b_ref[...]
    r8 = lax.broadcasted_iota(jnp.int32, (ch // SUBLANES, SUBLANES, c_w), 1)

    def chunk(c, carry):
        r0 = pl.multiple_of(c * ch, ch)
        win = xs[pl.ds(r0, ch + 8), :]
        xc = cw[3:4, :] * win[8:] + cb
        for k in (1, 2, 3):
            xc = xc + cw[3 - k:4 - k, :] * pltpu.roll(win, k, axis=0)[8:]
        z = jnp.dot(xc.astype(BF16), w_ref[...], preferred_element_type=F32) + bias
        r = jax.nn.sigmoid(z[:, :c_w])
        ig = jax.nn.sigmoid(z[:, c_w:])
        log_a = (-LRU_C) * r * sp
        a = jnp.exp(log_a)
        w = jnp.tanh(-log_a) * (1.0 + a * a)
        u = jnp.where(w > 0.0, w * lax.rsqrt(w), 0.0) * ig * xc
        a = a.reshape(ch // SUBLANES, SUBLANES, c_w)
        u = u.reshape(ch // SUBLANES, SUBLANES, c_w)
        for k in (1, 2, 4):
            a_sh = pltpu.roll(a, k, axis=1)
            u_sh = pltpu.roll(u, k, axis=1)
            ok = r8 >= k
            u = jnp.where(ok, u + a * u_sh, u)
            a = jnp.where(ok, a * a_sh, a)
        a_s[pl.ds(r0, ch), :] = a.reshape(ch, c_w)
        u_s[pl.ds(r0, ch), :] = u.reshape(ch, c_w)
        return carry

    lax.fori_loop(0, ts // ch, chunk, 0)

    def grp(g, hprev):
        r0 = pl.multiple_of(g * 8, 8)
        hg = u_s[pl.ds(r0, 8), :] + a_s[pl.ds(r0, 8), :] * hprev
        u_s[pl.ds(r0, 8), :] = hg
        return hg[7:8, :]

    hlast = lax.fori_loop(0, ts // 8, grp, carry_h[0:1, :], unroll=8)
    carry_h[0:1, :] = hlast
    xs[0:8, :] = xs[ts:ts + 8, :]
    o_ref[0] = (u_s[...] * gr_ref[0]).astype(o_ref.dtype)


def _rglru(xg, conv_w, conv_b, w_bd, b_cat, lru_lambda):
    b, s, w2 = xg.shape
    c_w = w2 // 2
    ts = min(LRU_TILE, s)
    ch = min(LRU_CHUNK, ts)
    return pl.pallas_call(
        functools.partial(_rglru_kernel, ts=ts, ch=ch, c_w=c_w),
        grid=(b, s // ts),
        in_specs=[pl.BlockSpec((1, ts, c_w), lambda bi, si: (bi, si, 0)),
                  pl.BlockSpec((1, ts, c_w), lambda bi, si: (bi, si, 1)),
                  pl.BlockSpec((CONV_WIDTH, c_w), lambda bi, si: (0, 0)),
                  pl.BlockSpec((1, c_w), lambda bi, si: (0, 0)),
                  pl.BlockSpec((c_w, 2 * c_w), lambda bi, si: (0, 0)),
                  pl.BlockSpec((1, 2 * c_w), lambda bi, si: (0, 0)),
                  pl.BlockSpec((1, c_w), lambda bi, si: (0, 0))],
        out_specs=pl.BlockSpec((1, ts, c_w), lambda bi, si: (bi, si, 0)),
        out_shape=jax.ShapeDtypeStruct((b, s, c_w), BF16),
        scratch_shapes=[pltpu.VMEM((ts + 8, c_w), F32), pltpu.VMEM((8, c_w), F32),
                        pltpu.VMEM((ts, c_w), F32), pltpu.VMEM((ts, c_w), F32)],
        compiler_params=_cparams(("parallel", "arbitrary")),
        name="rglru",
    )(xg, xg, conv_w, conv_b.reshape(1, c_w), w_bd, b_cat.reshape(1, 2 * c_w), lru_lambda.reshape(1, c_w))


def _outproj_kernel(att_ref, rnn_ref, x_ref, wo_ref, nw_ref, wr_ref, br_ref,
                    x1_ref, hn_ref, route_ref, route_t_ref, n8_ref, *, att_w):
    y = jnp.dot(att_ref[...], wo_ref[:att_w, :], preferred_element_type=F32)
    y = y + jnp.dot(rnn_ref[...], wo_ref[att_w:, :], preferred_element_type=F32)
    x1 = x_ref[...] + y
    x1_ref[...] = x1
    hn = (x1 * lax.rsqrt(jnp.mean(x1 * x1, axis=-1, keepdims=True) + NORM_EPS) * nw_ref[...]).astype(BF16)
    hn_ref[...] = hn
    lg = jnp.dot(hn, wr_ref[...], preferred_element_type=F32) + br_ref[...]
    tm = lg.shape[0]

    col = lax.broadcasted_iota(jnp.int32, lg.shape, 1)
    colf = col.astype(F32)
    big = float(LANES)
    ninf = -jnp.inf
    is_g = col < N_GROUPS
    lgm = jnp.where(is_g, lg, ninf)
    mg = jnp.max(lgm, axis=1, keepdims=True)
    g_sel = jnp.min(jnp.where(lgm == mg, colf, big), axis=1, keepdims=True)
    pg = 1.0 / jnp.sum(jnp.where(is_g, jnp.exp(lgm - mg), 0.0), axis=1, keepdims=True)
    lo = N_GROUPS + EXPERTS_PER_GROUP * g_sel
    in_grp = (colf >= lo) & (colf < lo + EXPERTS_PER_GROUP)
    lem = jnp.where(in_grp, lg, ninf)
    v1 = jnp.max(lem, axis=1, keepdims=True)
    i1 = jnp.min(jnp.where(lem == v1, colf, big), axis=1, keepdims=True)
    lem2 = jnp.where(colf == i1, ninf, lem)
    v2 = jnp.max(lem2, axis=1, keepdims=True)
    i2 = jnp.min(jnp.where(lem2 == v2, colf, big), axis=1, keepdims=True)
    e2 = jnp.exp(v2 - v1)
    den = 1.0 + e2
    g1 = pg / den
    g2 = pg * e2 / den

    oh1 = jnp.where(colf == i1, 1.0, 0.0)
    oh2 = jnp.where(colf == i2, 1.0, 0.0)
    oh = oh1 + oh2
    earlier = (lax.broadcasted_iota(jnp.int32, (tm, tm), 1)
               < lax.broadcasted_iota(jnp.int32, (tm, tm), 0)).astype(BF16)
    pref = jnp.dot(earlier, oh.astype(BF16), preferred_element_type=F32)
    cnt = jnp.sum(oh, axis=0, keepdims=True)
    n8 = jnp.floor((cnt + (SUBLANES - 1)) * (1.0 / SUBLANES))
    before = (lax.broadcasted_iota(jnp.int32, (LANES, LANES), 0)
              < lax.broadcasted_iota(jnp.int32, (LANES, LANES), 1)).astype(BF16)
    loff8 = jnp.dot(jnp.broadcast_to(n8, (SUBLANES, LANES)).astype(BF16), before,
                    preferred_element_type=F32)[0:1]
    pos = SUBLANES * loff8 + pref
    lp1 = jnp.sum(oh1 * pos, axis=1, keepdims=True)
    lp2 = jnp.sum(oh2 * pos, axis=1, keepdims=True)
    route = jnp.where(col == 0, g1,
            jnp.where(col == 1, g2,
            jnp.where(col == 2, lp1,
            jnp.where(col == 3, lp2, 0.0))))
    route_ref[...] = route
    route_t_ref[0] = route.T[:SUBLANES]
    n8_ref[0] = n8


def _out_proj(att, rnn, xf, w_out_bf, norm_w, w_route_bf, b_route):
    t, d = xf.shape
    att_w = att.shape[1]
    tm = min(ROW_TILE, t)
    row = lambda i: (i, 0)
    fix = lambda i: (0, 0)
    return pl.pallas_call(
        functools.partial(_outproj_kernel, att_w=att_w),
        grid=(t // tm,),
        in_specs=[pl.BlockSpec((tm, att_w), row), pl.BlockSpec((tm, rnn.shape[1]), row),
                  pl.BlockSpec((tm, d), row), pl.BlockSpec(w_out_bf.shape, fix),
                  pl.BlockSpec((1, d), fix), pl.BlockSpec((d, LANES), fix), pl.BlockSpec((1, LANES), fix)],
        out_specs=[pl.BlockSpec((tm, d), row), pl.BlockSpec((tm, d), row), pl.BlockSpec((tm, LANES), row),
                   pl.BlockSpec((1, SUBLANES, tm), lambda i: (i, 0, 0)),
                   pl.BlockSpec((1, 1, LANES), lambda i: (i, 0, 0))],
        out_shape=[jax.ShapeDtypeStruct((t, d), F32), jax.ShapeDtypeStruct((t, d), BF16),
                   jax.ShapeDtypeStruct((t, LANES), F32),
                   jax.ShapeDtypeStruct((t // tm, SUBLANES, tm), F32),
                   jax.ShapeDtypeStruct((t // tm, 1, LANES), F32)],
        compiler_params=_cparams(("parallel",)),
        name="out_proj",
    )(att, rnn, xf, w_out_bf, norm_w.reshape(1, d), w_route_bf, b_route)


def _local_rows(tm):
    return -(-(TOP_K * tm + N_EXPERTS * (SUBLANES - 1)) // LANES) * LANES


def _segment_tables(n8_tiles, tm_moe, n_tiles):
    n8 = n8_tiles[:, 0, N_GROUPS:N_GROUPS + N_EXPERTS].astype(jnp.int32)
    c8 = n8 * SUBLANES
    loff = jnp.cumsum(c8, axis=1) - c8
    gtot = jnp.sum(c8, axis=0)
    gpad = (gtot + tm_moe - 1) // tm_moe * tm_moe
    gend = jnp.cumsum(gpad)
    gstart = gend - gpad
    gbase = gstart[None, :] + jnp.cumsum(c8, axis=0) - c8
    tile_row0 = jnp.arange(n_tiles, dtype=jnp.int32) * tm_moe
    tile_e = jnp.minimum(jnp.sum((gend[None, :] <= tile_row0[:, None]).astype(jnp.int32), axis=1),
                         N_EXPERTS - 1).astype(jnp.int32)
    n_used = (gend[-1] // tm_moe).astype(jnp.int32).reshape(1)
    tail_start = (gstart + gtot).astype(jnp.int32)
    tail_n8 = ((gpad - gtot) // SUBLANES).astype(jnp.int32)
    after = gend[tile_e] // tm_moe
    next_e = jnp.where(after < n_used[0], tile_e[jnp.minimum(after, n_tiles - 1)], -1).astype(jnp.int32)
    first = jnp.concatenate([jnp.ones((1,), jnp.int32), (tile_e[1:] != tile_e[:-1]).astype(jnp.int32)])
    w_slot = ((jnp.cumsum(first) - 1) % 2).astype(jnp.int32)
    return (n8.reshape(-1), loff.reshape(-1).astype(jnp.int32), gbase.reshape(-1).astype(jnp.int32),
            tile_e, n_used, tail_start, tail_n8, next_e, w_slot)


def _segment_copies(n8_ref, src_off_ref, dst_off_ref, src, dst, sem, tile, wait):
    def rows_of(e):
        return pl.multiple_of(n8_ref[tile * N_EXPERTS + e] * SUBLANES, SUBLANES)

    if wait:
        total = lax.fori_loop(0, N_EXPERTS, lambda e, acc: acc + rows_of(e), 0)
        total = pl.multiple_of(total, SUBLANES)
        pltpu.make_async_copy(src.at[pl.ds(0, total), :], dst.at[pl.ds(0, total), :], sem).wait()
        return

    def per_expert(e, c):
        k = tile * N_EXPERTS + e
        rows = rows_of(e)

        @pl.when(rows > 0)
        def _():
            pltpu.make_async_copy(
                src.at[pl.ds(pl.multiple_of(src_off_ref[k], SUBLANES), rows), :],
                dst.at[pl.ds(pl.multiple_of(dst_off_ref[k], SUBLANES), rows), :], sem).start()
        return c
    lax.fori_loop(0, N_EXPERTS, per_expert, 0)


def _pack_bf16_pairs(x):
    n = x.shape[1] // 2
    bits = lax.bitcast_convert_type(x, jnp.uint32)
    return (bits[:, :n] >> 16) | (bits[:, n:] & jnp.uint32(0xFFFF0000))


def _unpack_bf16_pairs(p):
    lo = lax.bitcast_convert_type(p << 16, F32)
    hi = lax.bitcast_convert_type(p & jnp.uint32(0xFFFF0000), F32)
    return jnp.concatenate([lo, hi], axis=1).astype(BF16)


def _dispatch_kernel(n8_ref, loff_ref, gbase_ref, tstart_ref, tn8_ref, nu_ref, hn_ref, route_ref, xs_hbm,
                     stage, zbuf, sem, zsem, *, lcap, n_tt):
    i = pl.program_id(0)
    slot = i % 2
    tm = hn_ref.shape[0]
    tm_moe = zbuf.shape[0]
    n_tiles = xs_hbm.shape[0] // tm_moe

    def tail_copies(wait):
        def go(cp):
            if wait:
                cp.wait()
            else:
                cp.start()

        def per_expert(e, c):
            rows = pl.multiple_of(tn8_ref[e] * SUBLANES, SUBLANES)

            @pl.when(rows > 0)
            def _():
                go(pltpu.make_async_copy(
                    zbuf.at[pl.ds(0, rows), :],
                    xs_hbm.at[pl.ds(pl.multiple_of(tstart_ref[e], SUBLANES), rows), :], zsem.at[0]))
            return c
        lax.fori_loop(0, N_EXPERTS, per_expert, 0)

        def per_unused_tile(j, c):
            go(pltpu.make_async_copy(zbuf, xs_hbm.at[pl.ds(pl.multiple_of(j * tm_moe, tm_moe), tm_moe), :],
                                     zsem.at[0]))
            return c
        lax.fori_loop(nu_ref[0], n_tiles, per_unused_tile, 0)

    @pl.when(i == 0)
    def _():
        zbuf[...] = jnp.zeros(zbuf.shape, zbuf.dtype)
        tail_copies(False)

    @pl.when(i >= 2)
    def _():
        _segment_copies(n8_ref, loff_ref, gbase_ref, stage.at[slot], xs_hbm, sem.at[slot], i - 2, True)

    lp1 = route_ref[0, 2:3, :]
    lp2 = route_ref[0, 3:4, :]
    rpos = lax.broadcasted_iota(jnp.int32, (lcap, tm), 0).astype(F32)
    sel = jnp.where((rpos == lp1) | (rpos == lp2), 1.0, 0.0).astype(BF16)
    stage[slot] = _pack_bf16_pairs(jnp.dot(sel, hn_ref[...], preferred_element_type=F32))
    _segment_copies(n8_ref, loff_ref, gbase_ref, stage.at[slot], xs_hbm, sem.at[slot], i, False)

    @pl.when(i == n_tt - 1)
    def _():
        _segment_copies(n8_ref, loff_ref, gbase_ref, stage.at[slot], xs_hbm, sem.at[slot], i, True)
        if n_tt > 1:
            _segment_copies(n8_ref, loff_ref, gbase_ref, stage.at[1 - slot], xs_hbm, sem.at[1 - slot],
                            i - 1, True)
        tail_copies(True)


def _dispatch(hn, route, tables, n_rows, tm_moe):
    t, d = hn.shape
    tm = min(ROW_TILE, t)
    n_tt = t // tm
    lcap = _local_rows(tm)
    n8, loff, gbase, _, n_used, tail_start, tail_n8 = tables[:7]
    grid_spec = pltpu.PrefetchScalarGridSpec(
        num_scalar_prefetch=6,
        grid=(n_tt,),
        in_specs=[pl.BlockSpec((tm, d), lambda i, *_: (i, 0)),
                  pl.BlockSpec((1, SUBLANES, tm), lambda i, *_: (i, 0, 0))],
        out_specs=pl.BlockSpec(memory_space=pl.ANY),
        scratch_shapes=[pltpu.VMEM((2, lcap, d // 2), jnp.uint32), pltpu.VMEM((tm_moe, d // 2), jnp.uint32),
                        pltpu.SemaphoreType.DMA((2,)), pltpu.SemaphoreType.DMA((1,))],
    )
    return pl.pallas_call(
        functools.partial(_dispatch_kernel, lcap=lcap, n_tt=n_tt),
        grid_spec=grid_spec,
        out_shape=jax.ShapeDtypeStruct((n_rows, d // 2), jnp.uint32),
        compiler_params=_cparams(("arbitrary",), has_side_effects=True),
        name="dispatch",
    )(n8, loff, gbase, tail_start, tail_n8, n_used, hn, route)


def _moe_kernel(te_ref, nu_ref, nxt_ref, wslot_ref, xs_ref, wg_hbm, wu_hbm, wd_hbm, y_ref,
                wgf, wuf, wdf, wgb, wub, wdb, wsem):
    i = pl.program_id(0)

    def weight_copies(e, sl):
        return (pltpu.make_async_copy(wg_hbm.at[e], wgf.at[sl], wsem.at[sl, 0]),
                pltpu.make_async_copy(wu_hbm.at[e], wuf.at[sl], wsem.at[sl, 1]),
                pltpu.make_async_copy(wd_hbm.at[e], wdf.at[sl], wsem.at[sl, 2]))

    @pl.when(i == 0)
    def _():
        for cp in weight_copies(te_ref[0], wslot_ref[0]):
            cp.start()

    @pl.when(i < nu_ref[0])
    def _():
        changed = jnp.logical_or(i == 0, te_ref[i] != te_ref[jnp.maximum(i - 1, 0)])

        @pl.when(changed)
        def _():
            sl = wslot_ref[i]
            for cp in weight_copies(te_ref[i], sl):
                cp.wait()
            wgb[...] = wgf[sl].astype(BF16)
            wub[...] = wuf[sl].astype(BF16)
            wdb[...] = wdf[sl].astype(BF16)

            @pl.when(nxt_ref[i] >= 0)
            def _():
                for cp in weight_copies(nxt_ref[i], 1 - sl):
                    cp.start()

        x = _unpack_bf16_pairs(xs_ref[...])
        g = jnp.dot(x, wgb[...], preferred_element_type=F32)
        u = jnp.dot(x, wub[...], preferred_element_type=F32)
        hdn = (g * jax.nn.sigmoid(g) * u).astype(BF16)
        y = jnp.dot(hdn, wdb[...], preferred_element_type=F32)
        y_ref[...] = _pack_bf16_pairs(y.astype(BF16).astype(F32))

    @pl.when(i >= nu_ref[0])
    def _():
        y_ref[...] = jnp.zeros(y_ref.shape, y_ref.dtype)


def _moe(xs, tile_e, n_used, next_e, w_slot, w_g, w_u, w_d, tm):
    n_rows = xs.shape[0]
    d = w_g.shape[1]
    dp = xs.shape[1]
    n_tiles = n_rows // tm
    ff = w_g.shape[2]
    row_blk = lambda i, te, nu, *_: (jnp.minimum(i, nu[0] - 1), 0)
    hbm = pl.BlockSpec(memory_space=pl.ANY)
    grid_spec = pltpu.PrefetchScalarGridSpec(
        num_scalar_prefetch=4,
        grid=(n_tiles,),
        in_specs=[pl.BlockSpec((tm, dp), row_blk), hbm, hbm, hbm],
        out_specs=pl.BlockSpec((tm, dp), lambda i, *_: (i, 0)),
        scratch_shapes=[pltpu.VMEM((2, d, ff), F32), pltpu.VMEM((2, d, ff), F32), pltpu.VMEM((2, ff, d), F32),
                        pltpu.VMEM((d, ff), BF16), pltpu.VMEM((d, ff), BF16), pltpu.VMEM((ff, d), BF16),
                        pltpu.SemaphoreType.DMA((2, 3))],
    )
    return pl.pallas_call(
        _moe_kernel,
        grid_spec=grid_spec,
        out_shape=jax.ShapeDtypeStruct((n_rows, dp), jnp.uint32),
        compiler_params=_cparams(("arbitrary",)),
        name="moe",
    )(tile_e, n_used, next_e, w_slot, xs, w_g, w_u, w_d)


def _combine_kernel(n8_ref, loff_ref, gbase_ref, x1_ref, route_ref, nw_ref, y_hbm, o_ref,
                    ybuf, sem, *, lcap, n_tt):
    i = pl.program_id(0)
    slot = i % 2
    tm = x1_ref.shape[0]

    def fetch(tile, sl, wait):
        _segment_copies(n8_ref, gbase_ref, loff_ref, y_hbm, ybuf.at[sl], sem.at[sl], tile, wait)

    @pl.when(i == 0)
    def _():
        ybuf[...] = jnp.zeros(ybuf.shape, ybuf.dtype)
        fetch(0, 0, False)

    @pl.when(i + 1 < n_tt)
    def _():
        fetch(i + 1, 1 - slot, False)

    fetch(i, slot, True)
    g1 = route_ref[:, 0:1]
    g2 = route_ref[:, 1:2]
    lp1 = route_ref[:, 2:3]
    lp2 = route_ref[:, 3:4]
    cpos = lax.broadcasted_iota(jnp.int32, (tm, lcap), 1).astype(F32)
    gsel = (jnp.where(cpos == lp1, g1, 0.0) + jnp.where(cpos == lp2, g2, 0.0)).astype(BF16)
    moe = jnp.dot(gsel, _unpack_bf16_pairs(ybuf[slot]), preferred_element_type=F32)
    x = x1_ref[...] + moe
    o_ref[...] = x * lax.rsqrt(jnp.mean(x * x, axis=-1, keepdims=True) + NORM_EPS) * nw_ref[...]


def _combine(x1, y, route, norm_w, tables):
    t, d = x1.shape
    tm = min(ROW_TILE, t)
    n_tt = t // tm
    lcap = _local_rows(tm)
    n8, loff, gbase = tables[:3]
    grid_spec = pltpu.PrefetchScalarGridSpec(
        num_scalar_prefetch=3,
        grid=(n_tt,),
        in_specs=[pl.BlockSpec((tm, d), lambda i, *_: (i, 0)),
                  pl.BlockSpec((tm, LANES), lambda i, *_: (i, 0)),
                  pl.BlockSpec((1, d), lambda i, *_: (0, 0)),
                  pl.BlockSpec(memory_space=pl.ANY)],
        out_specs=pl.BlockSpec((tm, d), lambda i, *_: (i, 0)),
        scratch_shapes=[pltpu.VMEM((2, lcap, d // 2), jnp.uint32), pltpu.SemaphoreType.DMA((2,))],
    )
    return pl.pallas_call(
        functools.partial(_combine_kernel, lcap=lcap, n_tt=n_tt),
        grid_spec=grid_spec,
        out_shape=jax.ShapeDtypeStruct((t, d), F32),
        compiler_params=_cparams(("arbitrary",)),
        name="combine",
    )(n8, loff, gbase, x1, route, norm_w.reshape(1, d), y)


def _block_diag(w):
    n, bi, bj = w.shape
    eye = jnp.eye(n, dtype=w.dtype)
    return jnp.einsum('nij,nm->nimj', w, eye).reshape(n * bi, n * bj)


def kernel(x, mix_norm_w, w_in, lambda_q1, lambda_k1, lambda_q2, lambda_k2, head_norm_w, conv_w, conv_b, w_rgate, b_rgate, w_igate, b_igate, lru_lambda, w_out, ffn_norm_w, w_router_group, b_router_group, w_router_expert, b_router_expert, w_exp_gate, w_exp_up, w_exp_down, final_norm_w):
    b, s, d = x.shape
    t = b * s
    assert w_in.shape[0] == 1, "single-layer stack only"
    att_w = N_ATT_HEADS * HEAD_DIM
    tm_moe = MOE_TILE
    xf = x.reshape(t, d)
    for l in range(1):
        lambda_init = 0.8 - 0.6 * math.exp(-0.3 * l)
        assert s % ATT_TILE == 0, "sequence length must be a multiple of the attention tile"
        qt, ka, vt, xg = _in_proj(xf, mix_norm_w[l], w_in[l].astype(BF16), att_w, s)
        lam_params = jnp.stack([lambda_q1[l], lambda_k1[l], lambda_q2[l], lambda_k2[l]]).astype(F32)
        att = _diff_attention(qt, ka, vt, lam_params, head_norm_w[l], lambda_init, b, s)
        w_bd = jnp.concatenate([_block_diag(w_rgate[l]), _block_diag(w_igate[l])], axis=1).astype(BF16)
        b_cat = jnp.concatenate([b_rgate[l], b_igate[l]])
        rnn = _rglru(xg.reshape(b, s, xg.shape[1]), conv_w[l], conv_b[l], w_bd, b_cat, lru_lambda[l])
        w_route = jnp.concatenate([w_router_group[l], w_router_expert[l]], axis=1)
        w_route = jnp.pad(w_route, ((0, 0), (0, LANES - w_route.shape[1]))).astype(BF16)
        b_route = jnp.concatenate([b_router_group[l], b_router_expert[l]])
        b_route = jnp.pad(b_route, (0, LANES - b_route.shape[0])).reshape(1, LANES).astype(F32)
        x1, hn, route, route_t, n8_tiles = _out_proj(att.reshape(t, att_w), rnn.reshape(t, -1), xf,
                                                     w_out[l].astype(BF16), ffn_norm_w[l], w_route, b_route)
        n_tt = n8_tiles.shape[0]
        max_rows = TOP_K * t + n_tt * N_EXPERTS * (SUBLANES - 1) + N_EXPERTS * (tm_moe - 1)
        n_tiles = -(-max_rows // tm_moe)
        tables = _segment_tables(n8_tiles, tm_moe, n_tiles)
        xs = _dispatch(hn, route_t, tables, n_tiles * tm_moe, tm_moe)
        y = _moe(xs, tables[3], tables[4], tables[7], tables[8], w_exp_gate[l], w_exp_up[l], w_exp_down[l], tm_moe)
        out = _combine(x1, y, route, final_norm_w, tables)
    return out.reshape(b, s, d)
```

```python
import functools
import math

import numpy as np
import jax
import jax.numpy as jnp
from jax import lax
from jax.experimental import pallas as pl
from jax.experimental.pallas import tpu as pltpu

F32 = jnp.float32
BF16 = jnp.bfloat16

N_ATT_HEADS = 4
HEAD_DIM = 128
QK_DIM = 64
N_RNN_BLOCKS = 8
CONV_WIDTH = 4
LRU_C = 8.0
N_GROUPS = 4
EXPERTS_PER_GROUP = 8
N_EXPERTS = N_GROUPS * EXPERTS_PER_GROUP
TOP_K = 2
NORM_EPS = 1e-6
HEAD_NORM_EPS = 1e-5
LANES = 128
SUBLANES = 8
NEG_BIG = -1e30

ROW_TILE = 512
ROUTE_ROWS = 48
ATT_TILE = 512
ATT_HEADS_PER_STEP = 4
V_ROWS = HEAD_DIM + 16
LRU_TILE = 512
LRU_CHUNK = 128
MOE_TILE = 512
VMEM_LIMIT = 48 * 1024 * 1024


def _cparams(sem, vmem=VMEM_LIMIT, **kw):
    return pltpu.CompilerParams(dimension_semantics=sem, vmem_limit_bytes=vmem, **kw)


def _inproj_kernel(slope_ref, x_ref, nw_ref, w_ref, qt_ref, ka_ref, vt_ref, xg_ref, *, att_w, s_len):
    i = pl.program_id(0)
    x = x_ref[...]
    tm = x.shape[0]
    ms = jnp.mean(x * x, axis=-1, keepdims=True)
    hn = (x * lax.rsqrt(ms + NORM_EPS) * nw_ref[...]).astype(BF16)
    c_w = (w_ref.shape[1] - 3 * att_w) // 2
    p_lru = jnp.dot(hn, w_ref[:, 3 * att_w:], preferred_element_type=F32)
    xg_ref[:, :c_w] = p_lru[:, :c_w]
    xg_ref[:, c_w:] = _gelu_tanh(p_lru[:, c_w:])
    p_k = jnp.dot(hn, w_ref[:, att_w:2 * att_w], preferred_element_type=F32)
    p_v = jnp.dot(hn, w_ref[:, 2 * att_w:3 * att_w], preferred_element_type=F32)
    p_q = jnp.dot(hn, w_ref[:, :att_w], preferred_element_type=F32)

    qt = (p_q * (QK_DIM ** -0.5)).T
    ones2 = jnp.where(lax.broadcasted_iota(jnp.int32, (QK_DIM, tm), 0) < 2, 1.0, 0.0)
    pieces = []
    for g in range(2 * N_ATT_HEADS):
        pieces += [qt[g * QK_DIM:(g + 1) * QK_DIM], ones2]
    qt_ref[0] = jnp.concatenate(pieces, axis=0).astype(BF16)

    lane = lax.broadcasted_iota(jnp.int32, (tm, HEAD_DIM), 1)
    j = (i * tm) % s_len + lax.broadcasted_iota(jnp.int32, (tm, HEAD_DIM), 0)
    j_lo = (j & 255).astype(F32)
    j_hi = (j - (j & 255)).astype(F32)
    vtt = p_v.T
    ones_rows = jnp.where(lax.broadcasted_iota(jnp.int32, (V_ROWS - HEAD_DIM, tm), 0) == 0, 1.0, 0.0)
    for h in range(N_ATT_HEADS):
        slope = slope_ref[h]
        kk = p_k[:, h * HEAD_DIM:(h + 1) * HEAD_DIM]
        aug = jnp.where(lane == QK_DIM, slope * j_hi, jnp.where(lane == QK_DIM + 1, slope * j_lo, 0.0))
        ka_ref[:, 2 * h * HEAD_DIM:(2 * h + 1) * HEAD_DIM] = jnp.where(lane < QK_DIM, kk, aug).astype(BF16)
        ka_ref[:, (2 * h + 1) * HEAD_DIM:(2 * h + 2) * HEAD_DIM] = jnp.where(
            lane < QK_DIM, pltpu.roll(kk, QK_DIM, axis=1), aug).astype(BF16)
        vt_ref[0, h * V_ROWS:h * V_ROWS + HEAD_DIM, :] = vtt[h * HEAD_DIM:(h + 1) * HEAD_DIM].astype(BF16)
        vt_ref[0, h * V_ROWS + HEAD_DIM:(h + 1) * V_ROWS, :] = ones_rows.astype(BF16)


def _alibi_slopes():
    nh = N_ATT_HEADS
    return jnp.asarray(np.array([2.0 ** (-8.0 * (i + 1) / nh) for i in range(nh)], dtype=np.float32))


def _in_proj(xf, norm_w, w_in_bf, att_w, s_len):
    t, d = xf.shape
    n = w_in_bf.shape[1]
    tm = min(ATT_TILE, t)
    nh = N_ATT_HEADS
    return pl.pallas_call(
        functools.partial(_inproj_kernel, att_w=att_w, s_len=s_len),
        grid=(t // tm,),
        in_specs=[pl.BlockSpec(memory_space=pltpu.SMEM),
                  pl.BlockSpec((tm, d), lambda i: (i, 0)),
                  pl.BlockSpec((1, d), lambda i: (0, 0)),
                  pl.BlockSpec((d, n), lambda i: (0, 0))],
        out_specs=[pl.BlockSpec((1, 2 * att_w, tm), lambda i: (i, 0, 0)),
                   pl.BlockSpec((tm, 2 * att_w), lambda i: (i, 0)),
                   pl.BlockSpec((1, nh * V_ROWS, tm), lambda i: (i, 0, 0)),
                   pl.BlockSpec((tm, n - 3 * att_w), lambda i: (i, 0))],
        out_shape=[jax.ShapeDtypeStruct((t // tm, 2 * att_w, tm), BF16),
                   jax.ShapeDtypeStruct((t, 2 * att_w), BF16),
                   jax.ShapeDtypeStruct((t // tm, nh * V_ROWS, tm), BF16),
                   jax.ShapeDtypeStruct((t, n - 3 * att_w), F32)],
        compiler_params=_cparams(("parallel",)),
        name="in_proj",
    )(_alibi_slopes(), xf, norm_w.reshape(1, d), w_in_bf)


def _attn_kernel(lam_ref, hw_ref, q_ref, k_ref, vt, o_ref, sb, mx, acc, *, tq, n_heads, lambda_init):
    qi = pl.program_id(2)
    n_maps = 2 * n_heads
    mx[...] = jnp.full(mx.shape, NEG_BIG, F32)
    acc[...] = jnp.zeros(acc.shape, F32)

    def values(c, n, lanes=slice(None)):
        return vt[c, (n // 2) * V_ROWS:(n // 2 + 1) * V_ROWS, lanes]

    def scores(c, slot):
        rows = pl.ds(pl.multiple_of(c * tq, tq), tq)
        for n in range(n_maps):
            sb[n, slot] = jnp.dot(k_ref[0, rows, n * HEAD_DIM:(n + 1) * HEAD_DIM],
                                  q_ref[0, n * HEAD_DIM:(n + 1) * HEAD_DIM, :],
                                  preferred_element_type=F32)

    def softmax_pv(c, slot):
        for n in range(n_maps):
            s = sb[n, slot]
            m_prev = mx[n]
            m_new = jnp.maximum(m_prev, jnp.max(s, axis=0, keepdims=True))
            p = jnp.exp(s - m_new).astype(BF16)
            acc[n] = jnp.exp(m_prev - m_new) * acc[n] + jnp.dot(values(c, n), p, preferred_element_type=F32)
            mx[n] = m_new

    def softmax_pv_diagonal(c, slot):
        hq = tq // 2
        keep_t = (lax.broadcasted_iota(jnp.int32, (hq, tq), 0) <= lax.broadcasted_iota(jnp.int32, (hq, tq), 1))
        keep_b = (lax.broadcasted_iota(jnp.int32, (hq, hq), 0) <= lax.broadcasted_iota(jnp.int32, (hq, hq), 1))
        for n in range(n_maps):
            top = jnp.where(keep_t, sb[n, slot, :hq, :], NEG_BIG)
            bot = jnp.where(keep_b, sb[n, slot, hq:, hq:], NEG_BIG)
            mt = jnp.max(top, axis=0, keepdims=True)
            mb = jnp.max(bot, axis=0, keepdims=True)
            m_prev = mx[n]
            m_new = jnp.maximum(m_prev, jnp.concatenate([mt[:, :hq], jnp.maximum(mt[:, hq:], mb)], axis=1))
            p_top = jnp.exp(top - m_new).astype(BF16)
            p_bot = jnp.exp(bot - m_new[:, hq:]).astype(BF16)
            acc[n] = (jnp.exp(m_prev - m_new) * acc[n]
                      + jnp.dot(values(c, n, slice(0, hq)), p_top, preferred_element_type=F32))
            acc[n, :, hq:] += jnp.dot(values(c, n, slice(hq, tq)), p_bot, preferred_element_type=F32)
            mx[n] = m_new

    scores(0, 0)

    def body(j, c):
        scores(2 * j + 1, 1)
        softmax_pv(2 * j, 0)
        scores(2 * j + 2, 0)
        softmax_pv(2 * j + 1, 1)
        return c

    lax.fori_loop(0, qi // 2, body, 0)

    @pl.when(qi % 2 == 0)
    def _():
        softmax_pv_diagonal(qi, 0)

    @pl.when(qi % 2 == 1)
    def _():
        scores(qi, 1)
        softmax_pv(qi - 1, 0)
        softmax_pv_diagonal(qi, 1)

    lam = (jnp.exp(jnp.sum(lam_ref[0:1, :] * lam_ref[1:2, :], axis=1, keepdims=True))
           - jnp.exp(jnp.sum(lam_ref[2:3, :] * lam_ref[3:4, :], axis=1, keepdims=True))
           + lambda_init)
    for hh in range(n_heads):
        o1 = acc[2 * hh, :HEAD_DIM, :] * (1.0 / acc[2 * hh, HEAD_DIM:HEAD_DIM + 1, :])
        o2 = acc[2 * hh + 1, :HEAD_DIM, :] * (1.0 / acc[2 * hh + 1, HEAD_DIM:HEAD_DIM + 1, :])
        o = o1 - lam * o2
        o = o * lax.rsqrt(jnp.mean(o * o, axis=0, keepdims=True) + HEAD_NORM_EPS)
        o_ref[0, :, hh * HEAD_DIM:(hh + 1) * HEAD_DIM] = (
            o.T * hw_ref[...] * (1.0 - lambda_init)).astype(o_ref.dtype)


def _diff_attention(qt, ka, vt, lam_params, head_norm_w, lambda_init, b, s):
    nh = N_ATT_HEADS
    hp = ATT_HEADS_PER_STEP
    tq = qt.shape[2]
    nq = s // tq
    return pl.pallas_call(
        functools.partial(_attn_kernel, tq=tq, n_heads=hp, lambda_init=lambda_init),
        grid=(b, nh // hp, nq),
        in_specs=[pl.BlockSpec((4, QK_DIM), lambda bi, hi, qi: (0, 0)),
                  pl.BlockSpec((1, HEAD_DIM), lambda bi, hi, qi: (0, 0)),
                  pl.BlockSpec((1, hp * 2 * HEAD_DIM, tq), lambda bi, hi, qi: (bi * nq + qi, hi, 0)),
                  pl.BlockSpec((1, s, hp * 2 * HEAD_DIM), lambda bi, hi, qi: (bi, 0, hi)),
                  pl.BlockSpec((nq, hp * V_ROWS, tq), lambda bi, hi, qi: (bi, hi, 0))],
        out_specs=pl.BlockSpec((1, tq, hp * HEAD_DIM), lambda bi, hi, qi: (bi, qi, hi)),
        out_shape=jax.ShapeDtypeStruct((b, s, nh * HEAD_DIM), BF16),
        scratch_shapes=[pltpu.VMEM((2 * hp, 2, tq, tq), F32), pltpu.VMEM((2 * hp, 1, tq), F32),
                        pltpu.VMEM((2 * hp, V_ROWS, tq), F32)],
        compiler_params=_cparams(("parallel", "parallel", "arbitrary"), vmem=56 * 1024 * 1024),
        name="diff_attn",
    )(lam_params, head_norm_w.reshape(1, HEAD_DIM), qt, ka.reshape(b, s, ka.shape[1]), vt)


def _gelu_tanh(x):
    return 0.5 * x * (1.0 + jnp.tanh(math.sqrt(2.0 / math.pi) * (x + 0.044715 * (x * x * x))))


def _rglru_kernel(xr_ref, gr_ref, cw_ref, cb_ref, w_ref, b_ref, lam_ref, o_ref,
                  xs, carry_h, a_s, u_s, *, ts, ch, c_w):
    si = pl.program_id(1)

    @pl.when(si == 0)
    def _():
        xs[0:8, :] = jnp.zeros((8, c_w), F32)
        carry_h[...] = jnp.zeros(carry_h.shape, F32)

    xs[8:, :] = xr_ref[0]
    neg_lam = -lam_ref[...]
    sp = jnp.maximum(neg_lam, 0.0) + jnp.log1p(jnp.exp(-jnp.abs(neg_lam)))
    cw = cw_ref[...]
    cb = cb_ref[...]
    bias = b_ref[...]
    r8 = lax.broadcasted_iota(jnp.int32, (ch // SUBLANES, SUBLANES, c_w), 1)

    def chunk(c, carry):
        r0 = pl.multiple_of(c * ch, ch)
        win = xs[pl.ds(r0, ch + 8), :]
        xc = cw[3:4, :] * win[8:] + cb
        for k in (1, 2, 3):
            xc = xc + cw[3 - k:4 - k, :] * pltpu.roll(win, k, axis=0)[8:]
        z = jnp.dot(xc.astype(BF16), w_ref[...], preferred_element_type=F32) + bias
        r = jax.nn.sigmoid(z[:, :c_w])
        ig = jax.nn.sigmoid(z[:, c_w:])
        log_a = (-LRU_C) * r * sp
        a = jnp.exp(log_a)
        w = jnp.tanh(-log_a) * (1.0 + a * a)
        u = jnp.where(w > 0.0, w * lax.rsqrt(w), 0.0) * ig * xc
        a = a.reshape(ch // SUBLANES, SUBLANES, c_w)
        u = u.reshape(ch // SUBLANES, SUBLANES, c_w)
        for k in (1, 2, 4):
            a_sh = pltpu.roll(a, k, axis=1)
            u_sh = pltpu.roll(u, k, axis=1)
            ok = r8 >= k
            u = jnp.where(ok, u + a * u_sh, u)
            a = jnp.where(ok, a * a_sh, a)
        a_s[pl.ds(r0, ch), :] = a.reshape(ch, c_w)
        u_s[pl.ds(r0, ch), :] = u.reshape(ch, c_w)
        return carry

    lax.fori_loop(0, ts // ch, chunk, 0)

    def grp(g, hprev):
        r0 = pl.multiple_of(g * 8, 8)
        hg = u_s[pl.ds(r0, 8), :] + a_s[pl.ds(r0, 8), :] * hprev
        u_s[pl.ds(r0, 8), :] = hg
        return hg[7:8, :]

    hlast = lax.fori_loop(0, ts // 8, grp, carry_h[0:1, :], unroll=8)
    carry_h[0:1, :] = hlast
    xs[0:8, :] = xs[ts:ts + 8, :]
    o_ref[0] = (u_s[...] * gr_ref[0]).astype(o_ref.dtype)


def _rglru(xg, conv_w, conv_b, w_bd, b_cat, lru_lambda):
    b, s, w2 = xg.shape
    c_w = w2 // 2
    ts = min(LRU_TILE, s)
    ch = min(LRU_CHUNK, ts)
    return pl.pallas_call(
        functools.partial(_rglru_kernel, ts=ts, ch=ch, c_w=c_w),
        grid=(b, s // ts),
        in_specs=[pl.BlockSpec((1, ts, c_w), lambda bi, si: (bi, si, 0)),
                  pl.BlockSpec((1, ts, c_w), lambda bi, si: (bi, si, 1)),
                  pl.BlockSpec((CONV_WIDTH, c_w), lambda bi, si: (0, 0)),
                  pl.BlockSpec((1, c_w), lambda bi, si: (0, 0)),
                  pl.BlockSpec((c_w, 2 * c_w), lambda bi, si: (0, 0)),
                  pl.BlockSpec((1, 2 * c_w), lambda bi, si: (0, 0)),
                  pl.BlockSpec((1, c_w), lambda bi, si: (0, 0))],
        out_specs=pl.BlockSpec((1, ts, c_w), lambda bi, si: (bi, si, 0)),
        out_shape=jax.ShapeDtypeStruct((b, s, c_w), BF16),
        scratch_shapes=[pltpu.VMEM((ts + 8, c_w), F32), pltpu.VMEM((8, c_w), F32),
                        pltpu.VMEM((ts, c_w), F32), pltpu.VMEM((ts, c_w), F32)],
        compiler_params=_cparams(("parallel", "arbitrary")),
        name="rglru",
    )(xg, xg, conv_w, conv_b.reshape(1, c_w), w_bd, b_cat.reshape(1, 2 * c_w), lru_lambda.reshape(1, c_w))


def _outproj_kernel(att_ref, rnn_ref, x_ref, wo_ref, nw_ref, wrt_ref, brc_ref,
                    x1_ref, hn_ref, route_ref, route_t_ref, n8_ref, *, att_w):
    y = jnp.dot(att_ref[...], wo_ref[:att_w, :], preferred_element_type=F32)
    y = y + jnp.dot(rnn_ref[...], wo_ref[att_w:, :], preferred_element_type=F32)
    x1 = x_ref[...] + y
    x1_ref[...] = x1
    hn = (x1 * lax.rsqrt(jnp.mean(x1 * x1, axis=-1, keepdims=True) + NORM_EPS) * nw_ref[...]).astype(BF16)
    hn_ref[...] = hn
    tm = hn.shape[0]

    lg = lax.dot_general(wrt_ref[...], hn, (((1,), (1,)), ((), ())), preferred_element_type=F32)
    lg = lg[:ROUTE_ROWS] + brc_ref[:ROUTE_ROWS, 0:1]
    rowf = lax.broadcasted_iota(jnp.int32, lg.shape, 0).astype(F32)
    big = float(LANES)
    ninf = -jnp.inf
    is_g = rowf < N_GROUPS
    lgm = jnp.where(is_g, lg, ninf)
    mg = jnp.max(lgm, axis=0, keepdims=True)
    g_sel = jnp.min(jnp.where(lgm == mg, rowf, big), axis=0, keepdims=True)
    pg = 1.0 / jnp.sum(jnp.where(is_g, jnp.exp(lgm - mg), 0.0), axis=0, keepdims=True)
    lo = N_GROUPS + EXPERTS_PER_GROUP * g_sel
    in_grp = (rowf >= lo) & (rowf < lo + EXPERTS_PER_GROUP)
    lem = jnp.where(in_grp, lg, ninf)
    v1 = jnp.max(lem, axis=0, keepdims=True)
    i1 = jnp.min(jnp.where(lem == v1, rowf, big), axis=0, keepdims=True)
    lem2 = jnp.where(rowf == i1, ninf, lem)
    v2 = jnp.max(lem2, axis=0, keepdims=True)
    i2 = jnp.min(jnp.where(lem2 == v2, rowf, big), axis=0, keepdims=True)
    e2 = jnp.exp(v2 - v1)
    den = 1.0 + e2
    g1 = pg / den
    g2 = pg * e2 / den

    oh1 = jnp.where(rowf == i1, 1.0, 0.0)
    oh2 = jnp.where(rowf == i2, 1.0, 0.0)
    oh = oh1 + oh2
    earlier = (lax.broadcasted_iota(jnp.int32, (tm, tm), 0)
               < lax.broadcasted_iota(jnp.int32, (tm, tm), 1)).astype(BF16)
    pref = jnp.dot(oh.astype(BF16), earlier, preferred_element_type=F32)
    cnt = jnp.sum(oh, axis=1, keepdims=True)
    n8 = jnp.floor((cnt + (SUBLANES - 1)) * (1.0 / SUBLANES))
    n8_b = jnp.broadcast_to(n8, (ROUTE_ROWS, LANES))
    before = (lax.broadcasted_iota(jnp.int32, (ROUTE_ROWS, ROUTE_ROWS), 1)
              < lax.broadcasted_iota(jnp.int32, (ROUTE_ROWS, ROUTE_ROWS), 0)).astype(BF16)
    loff8 = jnp.dot(before, n8_b.astype(BF16), preferred_element_type=F32)[:, 0:1]
    pos = SUBLANES * loff8 + pref
    lp1 = jnp.sum(oh1 * pos, axis=0, keepdims=True)
    lp2 = jnp.sum(oh2 * pos, axis=0, keepdims=True)
    route_t = jnp.concatenate([g1, g2, lp1, lp2, jnp.zeros((LANES - 4, tm), F32)], axis=0)
    route_t_ref[0] = route_t[:SUBLANES]
    route_ref[...] = route_t.T
    n8_ref[0] = n8_b


def _out_proj(att, rnn, xf, w_out_bf, norm_w, w_route_t_bf, b_route_col):
    t, d = xf.shape
    att_w = att.shape[1]
    tm = min(ROW_TILE, t)
    row = lambda i: (i, 0)
    fix = lambda i: (0, 0)
    return pl.pallas_call(
        functools.partial(_outproj_kernel, att_w=att_w),
        grid=(t // tm,),
        in_specs=[pl.BlockSpec((tm, att_w), row), pl.BlockSpec((tm, rnn.shape[1]), row),
                  pl.BlockSpec((tm, d), row), pl.BlockSpec(w_out_bf.shape, fix),
                  pl.BlockSpec((1, d), fix), pl.BlockSpec((LANES, d), fix), pl.BlockSpec((LANES, 1), fix)],
        out_specs=[pl.BlockSpec((tm, d), row), pl.BlockSpec((tm, d), row), pl.BlockSpec((tm, LANES), row),
                   pl.BlockSpec((1, SUBLANES, tm), lambda i: (i, 0, 0)),
                   pl.BlockSpec((1, ROUTE_ROWS, LANES), lambda i: (i, 0, 0))],
        out_shape=[jax.ShapeDtypeStruct((t, d), F32), jax.ShapeDtypeStruct((t, d), BF16),
                   jax.ShapeDtypeStruct((t, LANES), F32),
                   jax.ShapeDtypeStruct((t // tm, SUBLANES, tm), F32),
                   jax.ShapeDtypeStruct((t // tm, ROUTE_ROWS, LANES), F32)],
        compiler_params=_cparams(("parallel",)),
        name="out_proj",
    )(att, rnn, xf, w_out_bf, norm_w.reshape(1, d), w_route_t_bf, b_route_col)


def _local_rows(tm):
    return -(-(TOP_K * tm + N_EXPERTS * (SUBLANES - 1)) // LANES) * LANES


def _segment_tables(n8_tiles, tm_moe, n_tiles):
    n8 = n8_tiles[:, N_GROUPS:N_GROUPS + N_EXPERTS, 0].astype(jnp.int32)
    c8 = n8 * SUBLANES
    loff = jnp.cumsum(c8, axis=1) - c8
    gtot = jnp.sum(c8, axis=0)
    gpad = (gtot + tm_moe - 1) // tm_moe * tm_moe
    gend = jnp.cumsum(gpad)
    gstart = gend - gpad
    gbase = gstart[None, :] + jnp.cumsum(c8, axis=0) - c8
    tile_row0 = jnp.arange(n_tiles, dtype=jnp.int32) * tm_moe
    tile_e = jnp.minimum(jnp.sum((gend[None, :] <= tile_row0[:, None]).astype(jnp.int32), axis=1),
                         N_EXPERTS - 1).astype(jnp.int32)
    n_used = (gend[-1] // tm_moe).astype(jnp.int32).reshape(1)
    tail_start = (gstart + gtot).astype(jnp.int32)
    tail_n8 = ((gpad - gtot) // SUBLANES).astype(jnp.int32)
    after = gend[tile_e] // tm_moe
    next_e = jnp.where(after < n_used[0], tile_e[jnp.minimum(after, n_tiles - 1)], -1).astype(jnp.int32)
    first = jnp.concatenate([jnp.ones((1,), jnp.int32), (tile_e[1:] != tile_e[:-1]).astype(jnp.int32)])
    w_slot = ((jnp.cumsum(first) - 1) % 2).astype(jnp.int32)
    return (n8.reshape(-1), loff.reshape(-1).astype(jnp.int32), gbase.reshape(-1).astype(jnp.int32),
            tile_e, n_used, tail_start, tail_n8, next_e, w_slot)


def _segment_copies(n8_ref, src_off_ref, dst_off_ref, src, dst, sem, tile, wait):
    def rows_of(e):
        return pl.multiple_of(n8_ref[tile * N_EXPERTS + e] * SUBLANES, SUBLANES)

    if wait:
        total = lax.fori_loop(0, N_EXPERTS, lambda e, acc: acc + rows_of(e), 0)
        total = pl.multiple_of(total, SUBLANES)
        pltpu.make_async_copy(src.at[pl.ds(0, total), :], dst.at[pl.ds(0, total), :], sem).wait()
        return

    def per_expert(e, c):
        k = tile * N_EXPERTS + e
        rows = rows_of(e)

        @pl.when(rows > 0)
        def _():
            pltpu.make_async_copy(
                src.at[pl.ds(pl.multiple_of(src_off_ref[k], SUBLANES), rows), :],
                dst.at[pl.ds(pl.multiple_of(dst_off_ref[k], SUBLANES), rows), :], sem).start()
        return c
    lax.fori_loop(0, N_EXPERTS, per_expert, 0)


def _pack_bf16_pairs(x):
    n = x.shape[1] // 2
    bits = lax.bitcast_convert_type(x, jnp.uint32)
    return (bits[:, :n] >> 16) | (bits[:, n:] & jnp.uint32(0xFFFF0000))


def _unpack_bf16_pairs(p):
    lo = lax.bitcast_convert_type(p << 16, F32)
    hi = lax.bitcast_convert_type(p & jnp.uint32(0xFFFF0000), F32)
    return jnp.concatenate([lo, hi], axis=1).astype(BF16)


def _dispatch_kernel(n8_ref, loff_ref, gbase_ref, tstart_ref, tn8_ref, nu_ref, hn_ref, route_ref, xs_hbm,
                     stage, zbuf, sem, zsem, *, lcap, n_tt):
    i = pl.program_id(0)
    slot = i % 2
    tm = hn_ref.shape[0]
    tm_moe = zbuf.shape[0]
    n_tiles = xs_hbm.shape[0] // tm_moe

    def tail_copies(wait):
        def go(cp):
            if wait:
                cp.wait()
            else:
                cp.start()

        def per_expert(e, c):
            rows = pl.multiple_of(tn8_ref[e] * SUBLANES, SUBLANES)

            @pl.when(rows > 0)
            def _():
                go(pltpu.make_async_copy(
                    zbuf.at[pl.ds(0, rows), :],
                    xs_hbm.at[pl.ds(pl.multiple_of(tstart_ref[e], SUBLANES), rows), :], zsem.at[0]))
            return c
        lax.fori_loop(0, N_EXPERTS, per_expert, 0)

        def per_unused_tile(j, c):
            go(pltpu.make_async_copy(zbuf, xs_hbm.at[pl.ds(pl.multiple_of(j * tm_moe, tm_moe), tm_moe), :],
                                     zsem.at[0]))
            return c
        lax.fori_loop(nu_ref[0], n_tiles, per_unused_tile, 0)

    @pl.when(i == 0)
    def _():
        zbuf[...] = jnp.zeros(zbuf.shape, zbuf.dtype)
        tail_copies(False)

    @pl.when(i >= 2)
    def _():
        _segment_copies(n8_ref, loff_ref, gbase_ref, stage.at[slot], xs_hbm, sem.at[slot], i - 2, True)

    lp1 = route_ref[0, 2:3, :]
    lp2 = route_ref[0, 3:4, :]
    rpos = lax.broadcasted_iota(jnp.int32, (lcap, tm), 0).astype(F32)
    sel = jnp.where((rpos == lp1) | (rpos == lp2), 1.0, 0.0).astype(BF16)
    stage[slot] = _pack_bf16_pairs(jnp.dot(sel, hn_ref[...], preferred_element_type=F32))
    _segment_copies(n8_ref, loff_ref, gbase_ref, stage.at[slot], xs_hbm, sem.at[slot], i, False)

    @pl.when(i == n_tt - 1)
    def _():
        _segment_copies(n8_ref, loff_ref, gbase_ref, stage.at[slot], xs_hbm, sem.at[slot], i, True)
        if n_tt > 1:
            _segment_copies(n8_ref, loff_ref, gbase_ref, stage.at[1 - slot], xs_hbm, sem.at[1 - slot],
                            i - 1, True)
        tail_copies(True)


def _dispatch(hn, route, tables, n_rows, tm_moe):
    t, d = hn.shape
    tm = min(ROW_TILE, t)
    n_tt = t // tm
    lcap = _local_rows(tm)
    n8, loff, gbase, _, n_used, tail_start, tail_n8 = tables[:7]
    grid_spec = pltpu.PrefetchScalarGridSpec(
        num_scalar_prefetch=6,
        grid=(n_tt,),
        in_specs=[pl.BlockSpec((tm, d), lambda i, *_: (i, 0)),
                  pl.BlockSpec((1, SUBLANES, tm), lambda i, *_: (i, 0, 0))],
        out_specs=pl.BlockSpec(memory_space=pl.ANY),
        scratch_shapes=[pltpu.VMEM((2, lcap, d // 2), jnp.uint32), pltpu.VMEM((tm_moe, d // 2), jnp.uint32),
                        pltpu.SemaphoreType.DMA((2,)), pltpu.SemaphoreType.DMA((1,))],
    )
    return pl.pallas_call(
        functools.partial(_dispatch_kernel, lcap=lcap, n_tt=n_tt),
        grid_spec=grid_spec,
        out_shape=jax.ShapeDtypeStruct((n_rows, d // 2), jnp.uint32),
        compiler_params=_cparams(("arbitrary",), has_side_effects=True),
        name="dispatch",
    )(n8, loff, gbase, tail_start, tail_n8, n_used, hn, route)


def _moe_kernel(te_ref, nu_ref, nxt_ref, wslot_ref, xs_ref, wg_hbm, wu_hbm, wd_hbm, y_ref,
                wgf, wuf, wdf, wgb, wub, wdb, wsem):
    i = pl.program_id(0)

    def weight_copies(e, sl):
        return (pltpu.make_async_copy(wg_hbm.at[e], wgf.at[sl], wsem.at[sl, 0]),
                pltpu.make_async_copy(wu_hbm.at[e], wuf.at[sl], wsem.at[sl, 1]),
                pltpu.make_async_copy(wd_hbm.at[e], wdf.at[sl], wsem.at[sl, 2]))

    @pl.when(i == 0)
    def _():
        for cp in weight_copies(te_ref[0], wslot_ref[0]):
            cp.start()

    @pl.when(i < nu_ref[0])
    def _():
        changed = jnp.logical_or(i == 0, te_ref[i] != te_ref[jnp.maximum(i - 1, 0)])

        @pl.when(changed)
        def _():
            sl = wslot_ref[i]
            for cp in weight_copies(te_ref[i], sl):
                cp.wait()
            wgb[...] = wgf[sl].astype(BF16)
            wub[...] = wuf[sl].astype(BF16)
            wdb[...] = wdf[sl].astype(BF16)

            @pl.when(nxt_ref[i] >= 0)
            def _():
                for cp in weight_copies(nxt_ref[i], 1 - sl):
                    cp.start()

        x = _unpack_bf16_pairs(xs_ref[...])
        g = jnp.dot(x, wgb[...], preferred_element_type=F32)
        u = jnp.dot(x, wub[...], preferred_element_type=F32)
        hdn = (g * jax.nn.sigmoid(g) * u).astype(BF16)
        y = jnp.dot(hdn, wdb[...], preferred_element_type=F32)
        y_ref[...] = _pack_bf16_pairs(y.astype(BF16).astype(F32))

    @pl.when(i >= nu_ref[0])
    def _():
        y_ref[...] = jnp.zeros(y_ref.shape, y_ref.dtype)


def _moe(xs, tile_e, n_used, next_e, w_slot, w_g, w_u, w_d, tm):
    n_rows = xs.shape[0]
    d = w_g.shape[1]
    dp = xs.shape[1]
    n_tiles = n_rows // tm
    ff = w_g.shape[2]
    row_blk = lambda i, te, nu, *_: (jnp.minimum(i, nu[0] - 1), 0)
    hbm = pl.BlockSpec(memory_space=pl.ANY)
    grid_spec = pltpu.PrefetchScalarGridSpec(
        num_scalar_prefetch=4,
        grid=(n_tiles,),
        in_specs=[pl.BlockSpec((tm, dp), row_blk), hbm, hbm, hbm],
        out_specs=pl.BlockSpec((tm, dp), lambda i, *_: (i, 0)),
        scratch_shapes=[pltpu.VMEM((2, d, ff), F32), pltpu.VMEM((2, d, ff), F32), pltpu.VMEM((2, ff, d), F32),
                        pltpu.VMEM((d, ff), BF16), pltpu.VMEM((d, ff), BF16), pltpu.VMEM((ff, d), BF16),
                        pltpu.SemaphoreType.DMA((2, 3))],
    )
    return pl.pallas_call(
        _moe_kernel,
        grid_spec=grid_spec,
        out_shape=jax.ShapeDtypeStruct((n_rows, dp), jnp.uint32),
        compiler_params=_cparams(("arbitrary",)),
        name="moe",
    )(tile_e, n_used, next_e, w_slot, xs, w_g, w_u, w_d)


def _combine_kernel(n8_ref, loff_ref, gbase_ref, x1_ref, route_ref, nw_ref, y_hbm, o_ref,
                    ybuf, sem, *, lcap, n_tt):
    i = pl.program_id(0)
    slot = i % 2
    tm = x1_ref.shape[0]

    def fetch(tile, sl, wait):
        _segment_copies(n8_ref, gbase_ref, loff_ref, y_hbm, ybuf.at[sl], sem.at[sl], tile, wait)

    @pl.when(i == 0)
    def _():
        ybuf[...] = jnp.zeros(ybuf.shape, ybuf.dtype)
        fetch(0, 0, False)

    @pl.when(i + 1 < n_tt)
    def _():
        fetch(i + 1, 1 - slot, False)

    fetch(i, slot, True)
    g1 = route_ref[:, 0:1]
    g2 = route_ref[:, 1:2]
    lp1 = route_ref[:, 2:3]
    lp2 = route_ref[:, 3:4]
    cpos = lax.broadcasted_iota(jnp.int32, (tm, lcap), 1).astype(F32)
    gsel = (jnp.where(cpos == lp1, g1, 0.0) + jnp.where(cpos == lp2, g2, 0.0)).astype(BF16)
    moe = jnp.dot(gsel, _unpack_bf16_pairs(ybuf[slot]), preferred_element_type=F32)
    x = x1_ref[...] + moe
    o_ref[...] = x * lax.rsqrt(jnp.mean(x * x, axis=-1, keepdims=True) + NORM_EPS) * nw_ref[...]


def _combine(x1, y, route, norm_w, tables):
    t, d = x1.shape
    tm = min(ROW_TILE, t)
    n_tt = t // tm
    lcap = _local_rows(tm)
    n8, loff, gbase = tables[:3]
    grid_spec = pltpu.PrefetchScalarGridSpec(
        num_scalar_prefetch=3,
        grid=(n_tt,),
        in_specs=[pl.BlockSpec((tm, d), lambda i, *_: (i, 0)),
                  pl.BlockSpec((tm, LANES), lambda i, *_: (i, 0)),
                  pl.BlockSpec((1, d), lambda i, *_: (0, 0)),
                  pl.BlockSpec(memory_space=pl.ANY)],
        out_specs=pl.BlockSpec((tm, d), lambda i, *_: (i, 0)),
        scratch_shapes=[pltpu.VMEM((2, lcap, d // 2), jnp.uint32), pltpu.SemaphoreType.DMA((2,))],
    )
    return pl.pallas_call(
        functools.partial(_combine_kernel, lcap=lcap, n_tt=n_tt),
        grid_spec=grid_spec,
        out_shape=jax.ShapeDtypeStruct((t, d), F32),
        compiler_params=_cparams(("arbitrary",)),
        name="combine",
    )(n8, loff, gbase, x1, route, norm_w.reshape(1, d), y)


def _block_diag(w):
    n, bi, bj = w.shape
    eye = jnp.eye(n, dtype=w.dtype)
    return jnp.einsum('nij,nm->nimj', w, eye).reshape(n * bi, n * bj)


def kernel(x, mix_norm_w, w_in, lambda_q1, lambda_k1, lambda_q2, lambda_k2, head_norm_w, conv_w, conv_b, w_rgate, b_rgate, w_igate, b_igate, lru_lambda, w_out, ffn_norm_w, w_router_group, b_router_group, w_router_expert, b_router_expert, w_exp_gate, w_exp_up, w_exp_down, final_norm_w):
    b, s, d = x.shape
    t = b * s
    assert w_in.shape[0] == 1, "single-layer stack only"
    att_w = N_ATT_HEADS * HEAD_DIM
    tm_moe = MOE_TILE
    xf = x.reshape(t, d)
    for l in range(1):
        lambda_init = 0.8 - 0.6 * math.exp(-0.3 * l)
        assert s % ATT_TILE == 0, "sequence length must be a multiple of the attention tile"
        qt, ka, vt, xg = _in_proj(xf, mix_norm_w[l], w_in[l].astype(BF16), att_w, s)
        lam_params = jnp.stack([lambda_q1[l], lambda_k1[l], lambda_q2[l], lambda_k2[l]]).astype(F32)
        att = _diff_attention(qt, ka, vt, lam_params, head_norm_w[l], lambda_init, b, s)
        w_bd = jnp.concatenate([_block_diag(w_rgate[l]), _block_diag(w_igate[l])], axis=1).astype(BF16)
        b_cat = jnp.concatenate([b_rgate[l], b_igate[l]])
        rnn = _rglru(xg.reshape(b, s, xg.shape[1]), conv_w[l], conv_b[l], w_bd, b_cat, lru_lambda[l])
        w_route = jnp.concatenate([w_router_group[l], w_router_expert[l]], axis=1).T
        w_route = jnp.pad(w_route, ((0, LANES - w_route.shape[0]), (0, 0))).astype(BF16)
        b_route = jnp.concatenate([b_router_group[l], b_router_expert[l]])
        b_route = jnp.pad(b_route, (0, LANES - b_route.shape[0])).reshape(LANES, 1).astype(F32)
        x1, hn, route, route_t, n8_tiles = _out_proj(att.reshape(t, att_w), rnn.reshape(t, -1), xf,
                                                     w_out[l].astype(BF16), ffn_norm_w[l], w_route, b_route)
        n_tt = n8_tiles.shape[0]
        max_rows = TOP_K * t + n_tt * N_EXPERTS * (SUBLANES - 1) + N_EXPERTS * (tm_moe - 1)
        n_tiles = -(-max_rows // tm_moe)
        tables = _segment_tables(n8_tiles, tm_moe, n_tiles)
        xs = _dispatch(hn, route_t, tables, n_tiles * tm_moe, tm_moe)
        y = _moe(xs, tables[3], tables[4], tables[7], tables[8], w_exp_gate[l], w_exp_up[l], w_exp_down[l], tm_moe)
        out = _combine(x1, y, route, final_norm_w, tables)
    return out.reshape(b, s, d)
```

```python
import functools
import math

import numpy as np
import jax
import jax.numpy as jnp
from jax import lax
from jax.experimental import pallas as pl
from jax.experimental.pallas import tpu as pltpu

F32 = jnp.float32
BF16 = jnp.bfloat16

N_ATT_HEADS = 4
HEAD_DIM = 128
QK_DIM = 64
N_RNN_BLOCKS = 8
CONV_WIDTH = 4
LRU_C = 8.0
N_GROUPS = 4
EXPERTS_PER_GROUP = 8
N_EXPERTS = N_GROUPS * EXPERTS_PER_GROUP
TOP_K = 2
NORM_EPS = 1e-6
HEAD_NORM_EPS = 1e-5
LANES = 128
SUBLANES = 8
NEG_BIG = -1e30

ROW_TILE = 512
ROUTE_ROWS = 48
ATT_TILE = 512
ATT_HEADS_PER_STEP = 4
V_ROWS = HEAD_DIM + 16
LRU_TILE = 512
LRU_CHUNK = 128
MOE_TILE = 512
MOE_SUBTILE = 128
VMEM_LIMIT = 48 * 1024 * 1024


def _cparams(sem, vmem=VMEM_LIMIT, **kw):
    return pltpu.CompilerParams(dimension_semantics=sem, vmem_limit_bytes=vmem, **kw)


def _inproj_kernel(slope_ref, x_ref, nw_ref, w_ref, qt_ref, ka_ref, vt_ref, xg_ref, *, att_w, s_len):
    i = pl.program_id(0)
    x = x_ref[...]
    tm = x.shape[0]
    ms = jnp.mean(x * x, axis=-1, keepdims=True)
    hn = (x * lax.rsqrt(ms + NORM_EPS) * nw_ref[...]).astype(BF16)
    c_w = (w_ref.shape[1] - 3 * att_w) // 2
    xg_ref[:, c_w:] = _gelu_tanh(jnp.dot(hn, w_ref[:, 3 * att_w + c_w:], preferred_element_type=F32))
    p_q = jnp.dot(hn, w_ref[:, :att_w], preferred_element_type=F32)
    p_v = jnp.dot(hn, w_ref[:, 2 * att_w:3 * att_w], preferred_element_type=F32)
    p_k = jnp.dot(hn, w_ref[:, att_w:2 * att_w], preferred_element_type=F32)
    xg_ref[:, :c_w] = jnp.dot(hn, w_ref[:, 3 * att_w:3 * att_w + c_w], preferred_element_type=F32)

    qt = (p_q * (QK_DIM ** -0.5)).T
    ones2 = jnp.where(lax.broadcasted_iota(jnp.int32, (QK_DIM, tm), 0) < 2, 1.0, 0.0)
    pieces = []
    for g in range(2 * N_ATT_HEADS):
        pieces += [qt[g * QK_DIM:(g + 1) * QK_DIM], ones2]
    qt_ref[0] = jnp.concatenate(pieces, axis=0).astype(BF16)

    lane = lax.broadcasted_iota(jnp.int32, (tm, HEAD_DIM), 1)
    j = (i * tm) % s_len + lax.broadcasted_iota(jnp.int32, (tm, HEAD_DIM), 0)
    j_lo = (j & 255).astype(F32)
    j_hi = (j - (j & 255)).astype(F32)
    vtt = p_v.T
    ones_rows = jnp.where(lax.broadcasted_iota(jnp.int32, (V_ROWS - HEAD_DIM, tm), 0) == 0, 1.0, 0.0)
    for h in range(N_ATT_HEADS):
        slope = slope_ref[h]
        kk = p_k[:, h * HEAD_DIM:(h + 1) * HEAD_DIM]
        aug = jnp.where(lane == QK_DIM, slope * j_hi, jnp.where(lane == QK_DIM + 1, slope * j_lo, 0.0))
        ka_ref[:, 2 * h * HEAD_DIM:(2 * h + 1) * HEAD_DIM] = jnp.where(lane < QK_DIM, kk, aug).astype(BF16)
        ka_ref[:, (2 * h + 1) * HEAD_DIM:(2 * h + 2) * HEAD_DIM] = jnp.where(
            lane < QK_DIM, pltpu.roll(kk, QK_DIM, axis=1), aug).astype(BF16)
        vt_ref[0, h * V_ROWS:h * V_ROWS + HEAD_DIM, :] = vtt[h * HEAD_DIM:(h + 1) * HEAD_DIM].astype(BF16)
        vt_ref[0, h * V_ROWS + HEAD_DIM:(h + 1) * V_ROWS, :] = ones_rows.astype(BF16)


def _alibi_slopes():
    nh = N_ATT_HEADS
    return jnp.asarray(np.array([2.0 ** (-8.0 * (i + 1) / nh) for i in range(nh)], dtype=np.float32))


def _in_proj(xf, norm_w, w_in_bf, att_w, s_len):
    t, d = xf.shape
    n = w_in_bf.shape[1]
    tm = min(ATT_TILE, t)
    nh = N_ATT_HEADS
    return pl.pallas_call(
        functools.partial(_inproj_kernel, att_w=att_w, s_len=s_len),
        grid=(t // tm,),
        in_specs=[pl.BlockSpec(memory_space=pltpu.SMEM),
                  pl.BlockSpec((tm, d), lambda i: (i, 0)),
                  pl.BlockSpec((1, d), lambda i: (0, 0)),
                  pl.BlockSpec((d, n), lambda i: (0, 0))],
        out_specs=[pl.BlockSpec((1, 2 * att_w, tm), lambda i: (i, 0, 0)),
                   pl.BlockSpec((tm, 2 * att_w), lambda i: (i, 0)),
                   pl.BlockSpec((1, nh * V_ROWS, tm), lambda i: (i, 0, 0)),
                   pl.BlockSpec((tm, n - 3 * att_w), lambda i: (i, 0))],
        out_shape=[jax.ShapeDtypeStruct((t // tm, 2 * att_w, tm), BF16),
                   jax.ShapeDtypeStruct((t, 2 * att_w), BF16),
                   jax.ShapeDtypeStruct((t // tm, nh * V_ROWS, tm), BF16),
                   jax.ShapeDtypeStruct((t, n - 3 * att_w), F32)],
        compiler_params=_cparams(("parallel",)),
        name="in_proj",
    )(_alibi_slopes(), xf, norm_w.reshape(1, d), w_in_bf)


def _attn_kernel(lam_ref, hw_ref, q_ref, k_ref, vt, o_ref, sb, mx, acc, *, tq, n_heads, lambda_init):
    qi = pl.program_id(2)
    n_maps = 2 * n_heads
    mx[...] = jnp.full(mx.shape, NEG_BIG, F32)
    acc[...] = jnp.zeros(acc.shape, F32)

    def values(c, n, lanes=slice(None)):
        return vt[c, (n // 2) * V_ROWS:(n // 2 + 1) * V_ROWS, lanes]

    def scores(c, slot):
        rows = pl.ds(pl.multiple_of(c * tq, tq), tq)
        for n in range(n_maps):
            sb[n, slot] = jnp.dot(k_ref[0, rows, n * HEAD_DIM:(n + 1) * HEAD_DIM],
                                  q_ref[0, n * HEAD_DIM:(n + 1) * HEAD_DIM, :],
                                  preferred_element_type=F32)

    def softmax_pv(c, slot):
        for n in range(n_maps):
            s = sb[n, slot]
            m_prev = mx[n]
            m_new = jnp.maximum(m_prev, jnp.max(s, axis=0, keepdims=True))
            p = jnp.exp(s - m_new).astype(BF16)
            acc[n] = jnp.exp(m_prev - m_new) * acc[n] + jnp.dot(values(c, n), p, preferred_element_type=F32)
            mx[n] = m_new

    def softmax_pv_diagonal(c, slot):
        hq = tq // 2
        keep_t = (lax.broadcasted_iota(jnp.int32, (hq, tq), 0) <= lax.broadcasted_iota(jnp.int32, (hq, tq), 1))
        keep_b = (lax.broadcasted_iota(jnp.int32, (hq, hq), 0) <= lax.broadcasted_iota(jnp.int32, (hq, hq), 1))
        for n in range(n_maps):
            top = jnp.where(keep_t, sb[n, slot, :hq, :], NEG_BIG)
            bot = jnp.where(keep_b, sb[n, slot, hq:, hq:], NEG_BIG)
            mt = jnp.max(top, axis=0, keepdims=True)
            mb = jnp.max(bot, axis=0, keepdims=True)
            m_prev = mx[n]
            m_new = jnp.maximum(m_prev, jnp.concatenate([mt[:, :hq], jnp.maximum(mt[:, hq:], mb)], axis=1))
            p_top = jnp.exp(top - m_new).astype(BF16)
            p_bot = jnp.exp(bot - m_new[:, hq:]).astype(BF16)
            acc[n] = (jnp.exp(m_prev - m_new) * acc[n]
                      + jnp.dot(values(c, n, slice(0, hq)), p_top, preferred_element_type=F32))
            acc[n, :, hq:] += jnp.dot(values(c, n, slice(hq, tq)), p_bot, preferred_element_type=F32)
            mx[n] = m_new

    scores(0, 0)

    def body(j, c):
        scores(2 * j + 1, 1)
        softmax_pv(2 * j, 0)
        scores(2 * j + 2, 0)
        softmax_pv(2 * j + 1, 1)
        return c

    lax.fori_loop(0, qi // 2, body, 0)

    @pl.when(qi % 2 == 0)
    def _():
        softmax_pv_diagonal(qi, 0)

    @pl.when(qi % 2 == 1)
    def _():
        scores(qi, 1)
        softmax_pv(qi - 1, 0)
        softmax_pv_diagonal(qi, 1)

    lam = (jnp.exp(jnp.sum(lam_ref[0:1, :] * lam_ref[1:2, :], axis=1, keepdims=True))
           - jnp.exp(jnp.sum(lam_ref[2:3, :] * lam_ref[3:4, :], axis=1, keepdims=True))
           + lambda_init)
    for hh in range(n_heads):
        o1 = acc[2 * hh, :HEAD_DIM, :] * (1.0 / acc[2 * hh, HEAD_DIM:HEAD_DIM + 1, :])
        o2 = acc[2 * hh + 1, :HEAD_DIM, :] * (1.0 / acc[2 * hh + 1, HEAD_DIM:HEAD_DIM + 1, :])
        o = o1 - lam * o2
        o = o * lax.rsqrt(jnp.mean(o * o, axis=0, keepdims=True) + HEAD_NORM_EPS)
        o_ref[0, :, hh * HEAD_DIM:(hh + 1) * HEAD_DIM] = (
            o.T * hw_ref[...] * (1.0 - lambda_init)).astype(o_ref.dtype)


def _diff_attention(qt, ka, vt, lam_params, head_norm_w, lambda_init, b, s):
    nh = N_ATT_HEADS
    hp = ATT_HEADS_PER_STEP
    tq = qt.shape[2]
    nq = s // tq
    return pl.pallas_call(
        functools.partial(_attn_kernel, tq=tq, n_heads=hp, lambda_init=lambda_init),
        grid=(b, nh // hp, nq),
        in_specs=[pl.BlockSpec((4, QK_DIM), lambda bi, hi, qi: (0, 0)),
                  pl.BlockSpec((1, HEAD_DIM), lambda bi, hi, qi: (0, 0)),
                  pl.BlockSpec((1, hp * 2 * HEAD_DIM, tq), lambda bi, hi, qi: (bi * nq + qi, hi, 0)),
                  pl.BlockSpec((1, s, hp * 2 * HEAD_DIM), lambda bi, hi, qi: (bi, 0, hi)),
                  pl.BlockSpec((nq, hp * V_ROWS, tq), lambda bi, hi, qi: (bi, hi, 0))],
        out_specs=pl.BlockSpec((1, tq, hp * HEAD_DIM), lambda bi, hi, qi: (bi, qi, hi)),
        out_shape=jax.ShapeDtypeStruct((b, s, nh * HEAD_DIM), BF16),
        scratch_shapes=[pltpu.VMEM((2 * hp, 2, tq, tq), F32), pltpu.VMEM((2 * hp, 1, tq), F32),
                        pltpu.VMEM((2 * hp, V_ROWS, tq), F32)],
        compiler_params=_cparams(("parallel", "parallel", "arbitrary"), vmem=56 * 1024 * 1024),
        name="diff_attn",
    )(lam_params, head_norm_w.reshape(1, HEAD_DIM), qt, ka.reshape(b, s, ka.shape[1]), vt)


def _gelu_tanh(x):
    return 0.5 * x * (1.0 + jnp.tanh(math.sqrt(2.0 / math.pi) * (x + 0.044715 * (x * x * x))))


def _rglru_kernel(xr_ref, gr_ref, cw_ref, cb_ref, w_ref, b_ref, lam_ref, o_ref,
                  xs, carry_h, a_s, u_s, *, ts, ch, c_w):
    si = pl.program_id(1)

    @pl.when(si == 0)
    def _():
        xs[0:8, :] = jnp.zeros((8, c_w), F32)
        carry_h[...] = jnp.zeros(carry_h.shape, F32)

    xs[8:, :] = xr_ref[0]
    neg_lam = -lam_ref[...]
    sp = jnp.maximum(neg_lam, 0.0) + jnp.log1p(jnp.exp(-jnp.abs(neg_lam)))
    cw = cw_ref[...]
    cb = cb_ref[...]
    bias = b_ref[...]
    r8 = lax.broadcasted_iota(jnp.int32, (ch // SUBLANES, SUBLANES, c_w), 1)

    def chunk(c, carry):
        r0 = pl.multiple_of(c * ch, ch)
        win = xs[pl.ds(r0, ch + 8), :]
        xc = cw[3:4, :] * win[8:] + cb
        for k in (1, 2, 3):
            xc = xc + cw[3 - k:4 - k, :] * pltpu.roll(win, k, axis=0)[8:]
        z = jnp.dot(xc.astype(BF16), w_ref[...], preferred_element_type=F32) + bias
        r = jax.nn.sigmoid(z[:, :c_w])
        ig = jax.nn.sigmoid(z[:, c_w:])
        log_a = (-LRU_C) * r * sp
        a = jnp.exp(log_a)
        w = jnp.tanh(-log_a) * (1.0 + a * a)
        u = jnp.where(w > 0.0, w * lax.rsqrt(w), 0.0) * ig * xc
        a = a.reshape(ch // SUBLANES, SUBLANES, c_w)
        u = u.reshape(ch // SUBLANES, SUBLANES, c_w)
        for k in (1, 2, 4):
            a_sh = pltpu.roll(a, k, axis=1)
            u_sh = pltpu.roll(u, k, axis=1)
            ok = r8 >= k
            u = jnp.where(ok, u + a * u_sh, u)
            a = jnp.where(ok, a * a_sh, a)
        a_s[pl.ds(r0, ch), :] = a.reshape(ch, c_w)
        u_s[pl.ds(r0, ch), :] = u.reshape(ch, c_w)
        return carry

    lax.fori_loop(0, ts // ch, chunk, 0)

    def grp(g, hprev):
        r0 = pl.multiple_of(g * 8, 8)
        hg = u_s[pl.ds(r0, 8), :] + a_s[pl.ds(r0, 8), :] * hprev
        u_s[pl.ds(r0, 8), :] = hg
        return hg[7:8, :]

    hlast = lax.fori_loop(0, ts // 8, grp, carry_h[0:1, :], unroll=8)
    carry_h[0:1, :] = hlast
    xs[0:8, :] = xs[ts:ts + 8, :]
    o_ref[0] = (u_s[...] * gr_ref[0]).astype(o_ref.dtype)


def _rglru(xg, conv_w, conv_b, w_bd, b_cat, lru_lambda):
    b, s, w2 = xg.shape
    c_w = w2 // 2
    ts = min(LRU_TILE, s)
    ch = min(LRU_CHUNK, ts)
    return pl.pallas_call(
        functools.partial(_rglru_kernel, ts=ts, ch=ch, c_w=c_w),
        grid=(b, s // ts),
        in_specs=[pl.BlockSpec((1, ts, c_w), lambda bi, si: (bi, si, 0)),
                  pl.BlockSpec((1, ts, c_w), lambda bi, si: (bi, si, 1)),
                  pl.BlockSpec((CONV_WIDTH, c_w), lambda bi, si: (0, 0)),
                  pl.BlockSpec((1, c_w), lambda bi, si: (0, 0)),
                  pl.BlockSpec((c_w, 2 * c_w), lambda bi, si: (0, 0)),
                  pl.BlockSpec((1, 2 * c_w), lambda bi, si: (0, 0)),
                  pl.BlockSpec((1, c_w), lambda bi, si: (0, 0))],
        out_specs=pl.BlockSpec((1, ts, c_w), lambda bi, si: (bi, si, 0)),
        out_shape=jax.ShapeDtypeStruct((b, s, c_w), BF16),
        scratch_shapes=[pltpu.VMEM((ts + 8, c_w), F32), pltpu.VMEM((8, c_w), F32),
                        pltpu.VMEM((ts, c_w), F32), pltpu.VMEM((ts, c_w), F32)],
        compiler_params=_cparams(("parallel", "arbitrary")),
        name="rglru",
    )(xg, xg, conv_w, conv_b.reshape(1, c_w), w_bd, b_cat.reshape(1, 2 * c_w), lru_lambda.reshape(1, c_w))


def _outproj_kernel(att_ref, rnn_ref, x_ref, wo_ref, nw_ref, wrt_ref, brc_ref,
                    x1_ref, hn_ref, route_ref, route_t_ref, n8_ref, *, att_w):
    y = jnp.dot(att_ref[...], wo_ref[:att_w, :], preferred_element_type=F32)
    y = y + jnp.dot(rnn_ref[...], wo_ref[att_w:, :], preferred_element_type=F32)
    x1 = x_ref[...] + y
    x1_ref[...] = x1
    hn = (x1 * lax.rsqrt(jnp.mean(x1 * x1, axis=-1, keepdims=True) + NORM_EPS) * nw_ref[...]).astype(BF16)
    hn_ref[...] = hn
    tm = hn.shape[0]

    lg = lax.dot_general(wrt_ref[...], hn, (((1,), (1,)), ((), ())), preferred_element_type=F32)
    lg = lg[:ROUTE_ROWS] + brc_ref[:ROUTE_ROWS, 0:1]
    rowf = lax.broadcasted_iota(jnp.int32, lg.shape, 0).astype(F32)
    big = float(LANES)
    ninf = -jnp.inf
    is_g = rowf < N_GROUPS
    lgm = jnp.where(is_g, lg, ninf)
    mg = jnp.max(lgm, axis=0, keepdims=True)
    g_sel = jnp.min(jnp.where(lgm == mg, rowf, big), axis=0, keepdims=True)
    pg = 1.0 / jnp.sum(jnp.where(is_g, jnp.exp(lgm - mg), 0.0), axis=0, keepdims=True)
    lo = N_GROUPS + EXPERTS_PER_GROUP * g_sel
    in_grp = (rowf >= lo) & (rowf < lo + EXPERTS_PER_GROUP)
    lem = jnp.where(in_grp, lg, ninf)
    v1 = jnp.max(lem, axis=0, keepdims=True)
    i1 = jnp.min(jnp.where(lem == v1, rowf, big), axis=0, keepdims=True)
    lem2 = jnp.where(rowf == i1, ninf, lem)
    v2 = jnp.max(lem2, axis=0, keepdims=True)
    i2 = jnp.min(jnp.where(lem2 == v2, rowf, big), axis=0, keepdims=True)
    e2 = jnp.exp(v2 - v1)
    den = 1.0 + e2
    g1 = pg / den
    g2 = pg * e2 / den

    oh1 = jnp.where(rowf == i1, 1.0, 0.0)
    oh2 = jnp.where(rowf == i2, 1.0, 0.0)
    oh = oh1 + oh2
    earlier = (lax.broadcasted_iota(jnp.int32, (tm, tm), 0)
               < lax.broadcasted_iota(jnp.int32, (tm, tm), 1)).astype(BF16)
    pref = jnp.dot(oh.astype(BF16), earlier, preferred_element_type=F32)
    cnt = jnp.sum(oh, axis=1, keepdims=True)
    n8 = jnp.floor((cnt + (SUBLANES - 1)) * (1.0 / SUBLANES))
    n8_b = jnp.broadcast_to(n8, (ROUTE_ROWS, LANES))
    before = (lax.broadcasted_iota(jnp.int32, (ROUTE_ROWS, ROUTE_ROWS), 1)
              < lax.broadcasted_iota(jnp.int32, (ROUTE_ROWS, ROUTE_ROWS), 0)).astype(BF16)
    loff8 = jnp.dot(before, n8_b.astype(BF16), preferred_element_type=F32)[:, 0:1]
    pos = SUBLANES * loff8 + pref
    lp1 = jnp.sum(oh1 * pos, axis=0, keepdims=True)
    lp2 = jnp.sum(oh2 * pos, axis=0, keepdims=True)
    route_t = jnp.concatenate([g1, g2, lp1, lp2, jnp.zeros((LANES - 4, tm), F32)], axis=0)
    route_t_ref[0] = route_t[:SUBLANES]
    route_ref[...] = route_t.T
    n8_ref[0] = n8_b


def _out_proj(att, rnn, xf, w_out_bf, norm_w, w_route_t_bf, b_route_col):
    t, d = xf.shape
    att_w = att.shape[1]
    tm = min(ROW_TILE, t)
    row = lambda i: (i, 0)
    fix = lambda i: (0, 0)
    return pl.pallas_call(
        functools.partial(_outproj_kernel, att_w=att_w),
        grid=(t // tm,),
        in_specs=[pl.BlockSpec((tm, att_w), row), pl.BlockSpec((tm, rnn.shape[1]), row),
                  pl.BlockSpec((tm, d), row), pl.BlockSpec(w_out_bf.shape, fix),
                  pl.BlockSpec((1, d), fix), pl.BlockSpec((LANES, d), fix), pl.BlockSpec((LANES, 1), fix)],
        out_specs=[pl.BlockSpec((tm, d), row), pl.BlockSpec((tm, d), row), pl.BlockSpec((tm, LANES), row),
                   pl.BlockSpec((1, SUBLANES, tm), lambda i: (i, 0, 0)),
                   pl.BlockSpec((1, ROUTE_ROWS, LANES), lambda i: (i, 0, 0))],
        out_shape=[jax.ShapeDtypeStruct((t, d), F32), jax.ShapeDtypeStruct((t, d), BF16),
                   jax.ShapeDtypeStruct((t, LANES), F32),
                   jax.ShapeDtypeStruct((t // tm, SUBLANES, tm), F32),
                   jax.ShapeDtypeStruct((t // tm, ROUTE_ROWS, LANES), F32)],
        compiler_params=_cparams(("parallel",)),
        name="out_proj",
    )(att, rnn, xf, w_out_bf, norm_w.reshape(1, d), w_route_t_bf, b_route_col)


def _local_rows(tm):
    return -(-(TOP_K * tm + N_EXPERTS * (SUBLANES - 1)) // LANES) * LANES


def _segment_tables(n8_tiles, tm_moe, n_tiles):
    n8 = n8_tiles[:, N_GROUPS:N_GROUPS + N_EXPERTS, 0].astype(jnp.int32)
    c8 = n8 * SUBLANES
    loff = jnp.cumsum(c8, axis=1) - c8
    gtot = jnp.sum(c8, axis=0)
    gpad = (gtot + tm_moe - 1) // tm_moe * tm_moe
    gend = jnp.cumsum(gpad)
    gstart = gend - gpad
    gbase = gstart[None, :] + jnp.cumsum(c8, axis=0) - c8
    tile_row0 = jnp.arange(n_tiles, dtype=jnp.int32) * tm_moe
    tile_e = jnp.minimum(jnp.sum((gend[None, :] <= tile_row0[:, None]).astype(jnp.int32), axis=1),
                         N_EXPERTS - 1).astype(jnp.int32)
    n_used = (gend[-1] // tm_moe).astype(jnp.int32).reshape(1)
    tail_start = (gstart + gtot).astype(jnp.int32)
    tail_n8 = ((gpad - gtot) // SUBLANES).astype(jnp.int32)
    after = gend[tile_e] // tm_moe
    next_e = jnp.where(after < n_used[0], tile_e[jnp.minimum(after, n_tiles - 1)], -1).astype(jnp.int32)
    first = jnp.concatenate([jnp.ones((1,), jnp.int32), (tile_e[1:] != tile_e[:-1]).astype(jnp.int32)])
    w_slot = ((jnp.cumsum(first) - 1) % 2).astype(jnp.int32)
    rows_in_tile = jnp.clip((gstart + gtot)[tile_e] - tile_row0, 1, tm_moe)
    tile_sub = ((rows_in_tile + MOE_SUBTILE - 1) // MOE_SUBTILE).astype(jnp.int32)
    return (n8.reshape(-1), loff.reshape(-1).astype(jnp.int32), gbase.reshape(-1).astype(jnp.int32),
            tile_e, n_used, tail_start, tail_n8, next_e, w_slot, tile_sub)


def _segment_copies(n8_ref, src_off_ref, dst_off_ref, src, dst, sem, tile, wait):
    def rows_of(e):
        return pl.multiple_of(n8_ref[tile * N_EXPERTS + e] * SUBLANES, SUBLANES)

    if wait:
        total = lax.fori_loop(0, N_EXPERTS, lambda e, acc: acc + rows_of(e), 0)
        total = pl.multiple_of(total, SUBLANES)
        pltpu.make_async_copy(src.at[pl.ds(0, total), :], dst.at[pl.ds(0, total), :], sem).wait()
        return

    def per_expert(e, c):
        k = tile * N_EXPERTS + e
        rows = rows_of(e)

        @pl.when(rows > 0)
        def _():
            pltpu.make_async_copy(
                src.at[pl.ds(pl.multiple_of(src_off_ref[k], SUBLANES), rows), :],
                dst.at[pl.ds(pl.multiple_of(dst_off_ref[k], SUBLANES), rows), :], sem).start()
        return c
    lax.fori_loop(0, N_EXPERTS, per_expert, 0)


def _pack_bf16_pairs(x):
    n = x.shape[1] // 2
    bits = lax.bitcast_convert_type(x, jnp.uint32)
    return (bits[:, :n] >> 16) | (bits[:, n:] & jnp.uint32(0xFFFF0000))


def _unpack_bf16_pairs(p):
    lo = lax.bitcast_convert_type(p << 16, F32)
    hi = lax.bitcast_convert_type(p & jnp.uint32(0xFFFF0000), F32)
    return jnp.concatenate([lo, hi], axis=1).astype(BF16)


def _dispatch_kernel(n8_ref, loff_ref, gbase_ref, tstart_ref, tn8_ref, nu_ref, hn_ref, route_ref, xs_hbm,
                     stage, zbuf, sem, zsem, *, lcap, n_tt):
    i = pl.program_id(0)
    slot = i % 2
    tm = hn_ref.shape[0]
    tm_moe = zbuf.shape[0]
    n_tiles = xs_hbm.shape[0] // tm_moe

    def tail_copies(wait):
        def go(cp):
            if wait:
                cp.wait()
            else:
                cp.start()

        def per_expert(e, c):
            rows = pl.multiple_of(tn8_ref[e] * SUBLANES, SUBLANES)

            @pl.when(rows > 0)
            def _():
                go(pltpu.make_async_copy(
                    zbuf.at[pl.ds(0, rows), :],
                    xs_hbm.at[pl.ds(pl.multiple_of(tstart_ref[e], SUBLANES), rows), :], zsem.at[0]))
            return c
        lax.fori_loop(0, N_EXPERTS, per_expert, 0)

        def per_unused_tile(j, c):
            go(pltpu.make_async_copy(zbuf, xs_hbm.at[pl.ds(pl.multiple_of(j * tm_moe, tm_moe), tm_moe), :],
                                     zsem.at[0]))
            return c
        lax.fori_loop(nu_ref[0], n_tiles, per_unused_tile, 0)

    @pl.when(i == 0)
    def _():
        zbuf[...] = jnp.zeros(zbuf.shape, zbuf.dtype)
        tail_copies(False)

    @pl.when(i >= 2)
    def _():
        _segment_copies(n8_ref, loff_ref, gbase_ref, stage.at[slot], xs_hbm, sem.at[slot], i - 2, True)

    lp1 = route_ref[0, 2:3, :]
    lp2 = route_ref[0, 3:4, :]
    rpos = lax.broadcasted_iota(jnp.int32, (lcap, tm), 0).astype(F32)
    sel = jnp.where((rpos == lp1) | (rpos == lp2), 1.0, 0.0).astype(BF16)
    stage[slot] = _pack_bf16_pairs(jnp.dot(sel, hn_ref[...], preferred_element_type=F32))
    _segment_copies(n8_ref, loff_ref, gbase_ref, stage.at[slot], xs_hbm, sem.at[slot], i, False)

    @pl.when(i == n_tt - 1)
    def _():
        _segment_copies(n8_ref, loff_ref, gbase_ref, stage.at[slot], xs_hbm, sem.at[slot], i, True)
        if n_tt > 1:
            _segment_copies(n8_ref, loff_ref, gbase_ref, stage.at[1 - slot], xs_hbm, sem.at[1 - slot],
                            i - 1, True)
        tail_copies(True)


def _dispatch(hn, route, tables, n_rows, tm_moe):
    t, d = hn.shape
    tm = min(ROW_TILE, t)
    n_tt = t // tm
    lcap = _local_rows(tm)
    n8, loff, gbase, _, n_used, tail_start, tail_n8 = tables[:7]
    grid_spec = pltpu.PrefetchScalarGridSpec(
        num_scalar_prefetch=6,
        grid=(n_tt,),
        in_specs=[pl.BlockSpec((tm, d), lambda i, *_: (i, 0)),
                  pl.BlockSpec((1, SUBLANES, tm), lambda i, *_: (i, 0, 0))],
        out_specs=pl.BlockSpec(memory_space=pl.ANY),
        scratch_shapes=[pltpu.VMEM((2, lcap, d // 2), jnp.uint32), pltpu.VMEM((tm_moe, d // 2), jnp.uint32),
                        pltpu.SemaphoreType.DMA((2,)), pltpu.SemaphoreType.DMA((1,))],
    )
    return pl.pallas_call(
        functools.partial(_dispatch_kernel, lcap=lcap, n_tt=n_tt),
        grid_spec=grid_spec,
        out_shape=jax.ShapeDtypeStruct((n_rows, d // 2), jnp.uint32),
        compiler_params=_cparams(("arbitrary",), has_side_effects=True),
        name="dispatch",
    )(n8, loff, gbase, tail_start, tail_n8, n_used, hn, route)


def _moe_kernel(te_ref, nu_ref, nxt_ref, wslot_ref, nsub_ref, xs_ref, wg_hbm, wu_hbm, wd_hbm, y_ref,
                wgf, wuf, wdf, wgb, wub, wdb, wsem):
    i = pl.program_id(0)

    def weight_copies(e, sl):
        return (pltpu.make_async_copy(wg_hbm.at[e], wgf.at[sl], wsem.at[sl, 0]),
                pltpu.make_async_copy(wu_hbm.at[e], wuf.at[sl], wsem.at[sl, 1]),
                pltpu.make_async_copy(wd_hbm.at[e], wdf.at[sl], wsem.at[sl, 2]))

    @pl.when(i == 0)
    def _():
        for cp in weight_copies(te_ref[0], wslot_ref[0]):
            cp.start()

    @pl.when(i < nu_ref[0])
    def _():
        changed = jnp.logical_or(i == 0, te_ref[i] != te_ref[jnp.maximum(i - 1, 0)])

        @pl.when(changed)
        def _():
            sl = wslot_ref[i]
            for cp in weight_copies(te_ref[i], sl):
                cp.wait()
            wgb[...] = wgf[sl].astype(BF16)
            wub[...] = wuf[sl].astype(BF16)
            wdb[...] = wdf[sl].astype(BF16)

            @pl.when(nxt_ref[i] >= 0)
            def _():
                for cp in weight_copies(nxt_ref[i], 1 - sl):
                    cp.start()

        tm = xs_ref.shape[0]
        n_sub = tm // MOE_SUBTILE
        filled = nsub_ref[i]
        for k in range(1, n_sub + 1):
            @pl.when(filled == k)
            def _(k=k):
                r = k * MOE_SUBTILE
                x = _unpack_bf16_pairs(xs_ref[:r, :])
                g = jnp.dot(x, wgb[...], preferred_element_type=F32)
                u = jnp.dot(x, wub[...], preferred_element_type=F32)
                hdn = (g * jax.nn.sigmoid(g) * u).astype(BF16)
                y = jnp.dot(hdn, wdb[...], preferred_element_type=F32)
                y_ref[:r, :] = _pack_bf16_pairs(y.astype(BF16).astype(F32))
                if r < tm:
                    y_ref[r:, :] = jnp.zeros((tm - r, y_ref.shape[1]), y_ref.dtype)

    @pl.when(i >= nu_ref[0])
    def _():
        y_ref[...] = jnp.zeros(y_ref.shape, y_ref.dtype)


def _moe(xs, tile_e, n_used, next_e, w_slot, tile_sub, w_g, w_u, w_d, tm):
    n_rows = xs.shape[0]
    d = w_g.shape[1]
    dp = xs.shape[1]
    n_tiles = n_rows // tm
    ff = w_g.shape[2]
    row_blk = lambda i, te, nu, *_: (jnp.minimum(i, nu[0] - 1), 0)
    hbm = pl.BlockSpec(memory_space=pl.ANY)
    grid_spec = pltpu.PrefetchScalarGridSpec(
        num_scalar_prefetch=5,
        grid=(n_tiles,),
        in_specs=[pl.BlockSpec((tm, dp), row_blk), hbm, hbm, hbm],
        out_specs=pl.BlockSpec((tm, dp), lambda i, *_: (i, 0)),
        scratch_shapes=[pltpu.VMEM((2, d, ff), F32), pltpu.VMEM((2, d, ff), F32), pltpu.VMEM((2, ff, d), F32),
                        pltpu.VMEM((d, ff), BF16), pltpu.VMEM((d, ff), BF16), pltpu.VMEM((ff, d), BF16),
                        pltpu.SemaphoreType.DMA((2, 3))],
    )
    return pl.pallas_call(
        _moe_kernel,
        grid_spec=grid_spec,
        out_shape=jax.ShapeDtypeStruct((n_rows, dp), jnp.uint32),
        compiler_params=_cparams(("arbitrary",)),
        name="moe",
    )(tile_e, n_used, next_e, w_slot, tile_sub, xs, w_g, w_u, w_d)


def _combine_kernel(n8_ref, loff_ref, gbase_ref, x1_ref, route_ref, nw_ref, y_hbm, o_ref,
                    ybuf, sem, *, lcap, n_tt):
    i = pl.program_id(0)
    slot = i % 2
    tm = x1_ref.shape[0]

    def fetch(tile, sl, wait):
        _segment_copies(n8_ref, gbase_ref, loff_ref, y_hbm, ybuf.at[sl], sem.at[sl], tile, wait)

    @pl.when(i == 0)
    def _():
        ybuf[...] = jnp.zeros(ybuf.shape, ybuf.dtype)
        fetch(0, 0, False)

    @pl.when(i + 1 < n_tt)
    def _():
        fetch(i + 1, 1 - slot, False)

    fetch(i, slot, True)
    g1 = route_ref[:, 0:1]
    g2 = route_ref[:, 1:2]
    lp1 = route_ref[:, 2:3]
    lp2 = route_ref[:, 3:4]
    cpos = lax.broadcasted_iota(jnp.int32, (tm, lcap), 1).astype(F32)
    gsel = (jnp.where(cpos == lp1, g1, 0.0) + jnp.where(cpos == lp2, g2, 0.0)).astype(BF16)
    moe = jnp.dot(gsel, _unpack_bf16_pairs(ybuf[slot]), preferred_element_type=F32)
    x = x1_ref[...] + moe
    o_ref[...] = x * lax.rsqrt(jnp.mean(x * x, axis=-1, keepdims=True) + NORM_EPS) * nw_ref[...]


def _combine(x1, y, route, norm_w, tables):
    t, d = x1.shape
    tm = min(ROW_TILE, t)
    n_tt = t // tm
    lcap = _local_rows(tm)
    n8, loff, gbase = tables[:3]
    grid_spec = pltpu.PrefetchScalarGridSpec(
        num_scalar_prefetch=3,
        grid=(n_tt,),
        in_specs=[pl.BlockSpec((tm, d), lambda i, *_: (i, 0)),
                  pl.BlockSpec((tm, LANES), lambda i, *_: (i, 0)),
                  pl.BlockSpec((1, d), lambda i, *_: (0, 0)),
                  pl.BlockSpec(memory_space=pl.ANY)],
        out_specs=pl.BlockSpec((tm, d), lambda i, *_: (i, 0)),
        scratch_shapes=[pltpu.VMEM((2, lcap, d // 2), jnp.uint32), pltpu.SemaphoreType.DMA((2,))],
    )
    return pl.pallas_call(
        functools.partial(_combine_kernel, lcap=lcap, n_tt=n_tt),
        grid_spec=grid_spec,
        out_shape=jax.ShapeDtypeStruct((t, d), F32),
        compiler_params=_cparams(("arbitrary",)),
        name="combine",
    )(n8, loff, gbase, x1, route, norm_w.reshape(1, d), y)


def _block_diag(w):
    n, bi, bj = w.shape
    eye = jnp.eye(n, dtype=w.dtype)
    return jnp.einsum('nij,nm->nimj', w, eye).reshape(n * bi, n * bj)


def kernel(x, mix_norm_w, w_in, lambda_q1, lambda_k1, lambda_q2, lambda_k2, head_norm_w, conv_w, conv_b, w_rgate, b_rgate, w_igate, b_igate, lru_lambda, w_out, ffn_norm_w, w_router_group, b_router_group, w_router_expert, b_router_expert, w_exp_gate, w_exp_up, w_exp_down, final_norm_w):
    b, s, d = x.shape
    t = b * s
    assert w_in.shape[0] == 1, "single-layer stack only"
    att_w = N_ATT_HEADS * HEAD_DIM
    tm_moe = MOE_TILE
    xf = x.reshape(t, d)
    for l in range(1):
        lambda_init = 0.8 - 0.6 * math.exp(-0.3 * l)
        assert s % ATT_TILE == 0, "sequence length must be a multiple of the attention tile"
        qt, ka, vt, xg = _in_proj(xf, mix_norm_w[l], w_in[l].astype(BF16), att_w, s)
        lam_params = jnp.stack([lambda_q1[l], lambda_k1[l], lambda_q2[l], lambda_k2[l]]).astype(F32)
        att = _diff_attention(qt, ka, vt, lam_params, head_norm_w[l], lambda_init, b, s)
        w_bd = jnp.concatenate([_block_diag(w_rgate[l]), _block_diag(w_igate[l])], axis=1).astype(BF16)
        b_cat = jnp.concatenate([b_rgate[l], b_igate[l]])
        rnn = _rglru(xg.reshape(b, s, xg.shape[1]), conv_w[l], conv_b[l], w_bd, b_cat, lru_lambda[l])
        w_route = jnp.concatenate([w_router_group[l], w_router_expert[l]], axis=1).T
        w_route = jnp.pad(w_route, ((0, LANES - w_route.shape[0]), (0, 0))).astype(BF16)
        b_route = jnp.concatenate([b_router_group[l], b_router_expert[l]])
        b_route = jnp.pad(b_route, (0, LANES - b_route.shape[0])).reshape(LANES, 1).astype(F32)
        x1, hn, route, route_t, n8_tiles = _out_proj(att.reshape(t, att_w), rnn.reshape(t, -1), xf,
                                                     w_out[l].astype(BF16), ffn_norm_w[l], w_route, b_route)
        n_tt = n8_tiles.shape[0]
        max_rows = TOP_K * t + n_tt * N_EXPERTS * (SUBLANES - 1) + N_EXPERTS * (tm_moe - 1)
        n_tiles = -(-max_rows // tm_moe)
        tables = _segment_tables(n8_tiles, tm_moe, n_tiles)
        xs = _dispatch(hn, route_t, tables, n_tiles * tm_moe, tm_moe)
        y = _moe(xs, tables[3], tables[4], tables[7], tables[8], tables[9],
                 w_exp_gate[l], w_exp_up[l], w_exp_down[l], tm_moe)
        out = _combine(x1, y, route, final_norm_w, tables)
    return out.reshape(b, s, d)
```

```python
import functools
import math

import numpy as np
import jax
import jax.numpy as jnp
from jax import lax
from jax.experimental import pallas as pl
from jax.experimental.pallas import tpu as pltpu

F32 = jnp.float32
BF16 = jnp.bfloat16

N_ATT_HEADS = 4
HEAD_DIM = 128
QK_DIM = 64
N_RNN_BLOCKS = 8
CONV_WIDTH = 4
LRU_C = 8.0
N_GROUPS = 4
EXPERTS_PER_GROUP = 8
N_EXPERTS = N_GROUPS * EXPERTS_PER_GROUP
TOP_K = 2
NORM_EPS = 1e-6
HEAD_NORM_EPS = 1e-5
LANES = 128
SUBLANES = 8
NEG_BIG = -1e30

ROW_TILE = 512
ROUTE_ROWS = 48
ATT_TILE = 512
ATT_HEADS_PER_STEP = 4
V_ROWS = HEAD_DIM + 16
LRU_CHUNK = 128
MOE_TILE = 512
MOE_SUBTILE = 128
VMEM_LIMIT = 48 * 1024 * 1024


def _cparams(sem, vmem=VMEM_LIMIT, **kw):
    return pltpu.CompilerParams(dimension_semantics=sem, vmem_limit_bytes=vmem, **kw)


def _inproj_kernel(slope_ref, x_ref, nw_ref, w_ref, cw_ref, cb_ref, wg_ref, bg_ref, lam_ref,
                   qt_ref, ka_ref, vt_ref, rnn_ref, xs, carry_h, a_s, u_s, *, att_w, s_len, ch):
    i = pl.program_id(0)
    x = x_ref[...]
    tm = x.shape[0]
    c_w = (w_ref.shape[1] - 3 * att_w) // 2

    @pl.when((i * tm) % s_len == 0)
    def _():
        xs[0:SUBLANES, :] = jnp.zeros((SUBLANES, c_w), F32)
        carry_h[...] = jnp.zeros(carry_h.shape, F32)

    ms = jnp.mean(x * x, axis=-1, keepdims=True)
    hn = (x * lax.rsqrt(ms + NORM_EPS) * nw_ref[...]).astype(BF16)
    p_lru = jnp.dot(hn, w_ref[:, 3 * att_w:], preferred_element_type=F32)

    xs[SUBLANES:, :] = p_lru[:, :c_w]
    neg_lam = -lam_ref[...]
    sp = jnp.maximum(neg_lam, 0.0) + jnp.log1p(jnp.exp(-jnp.abs(neg_lam)))
    cw = cw_ref[...]
    cb = cb_ref[...]
    bias = bg_ref[...]
    r8 = lax.broadcasted_iota(jnp.int32, (ch // SUBLANES, SUBLANES, LANES), 1)
    for c in range(tm // ch):
        r0 = c * ch
        win = xs[r0:r0 + ch + SUBLANES, :]
        xc = cw[3:4, :] * win[SUBLANES:] + cb
        for k in (1, 2, 3):
            xc = xc + cw[3 - k:4 - k, :] * pltpu.roll(win, k, axis=0)[SUBLANES:]
        z = jnp.dot(xc.astype(BF16), wg_ref[...], preferred_element_type=F32) + bias
        for l0 in range(0, c_w, LANES):
            ls = slice(l0, l0 + LANES)
            r = jax.nn.sigmoid(z[:, l0:l0 + LANES])
            ig = jax.nn.sigmoid(z[:, c_w + l0:c_w + l0 + LANES])
            log_a = (-LRU_C) * r * sp[:, ls]
            a = jnp.exp(log_a)
            w = jnp.tanh(-log_a) * (1.0 + a * a)
            u = jnp.where(w > 0.0, w * lax.rsqrt(w), 0.0) * ig * xc[:, ls]
            a = a.reshape(ch // SUBLANES, SUBLANES, LANES)
            u = u.reshape(ch // SUBLANES, SUBLANES, LANES)
            for k in (1, 2, 4):
                a_sh = pltpu.roll(a, k, axis=1)
                u_sh = pltpu.roll(u, k, axis=1)
                ok = r8 >= k
                u = jnp.where(ok, u + a * u_sh, u)
                a = jnp.where(ok, a * a_sh, a)
            a_s[r0:r0 + ch, ls] = a.reshape(ch, LANES)
            u_s[r0:r0 + ch, ls] = u.reshape(ch, LANES)
    xs[0:SUBLANES, :] = xs[tm:tm + SUBLANES, :]

    p_q = jnp.dot(hn, w_ref[:, :att_w], preferred_element_type=F32)
    p_v = jnp.dot(hn, w_ref[:, 2 * att_w:3 * att_w], preferred_element_type=F32)
    p_k = jnp.dot(hn, w_ref[:, att_w:2 * att_w], preferred_element_type=F32)
    qt = (p_q * (QK_DIM ** -0.5)).T
    ones2 = jnp.where(lax.broadcasted_iota(jnp.int32, (QK_DIM, tm), 0) < 2, 1.0, 0.0)
    pieces = []
    for g in range(2 * N_ATT_HEADS):
        pieces += [qt[g * QK_DIM:(g + 1) * QK_DIM], ones2]
    qt_ref[0] = jnp.concatenate(pieces, axis=0).astype(BF16)

    lane = lax.broadcasted_iota(jnp.int32, (tm, HEAD_DIM), 1)
    j = (i * tm) % s_len + lax.broadcasted_iota(jnp.int32, (tm, HEAD_DIM), 0)
    j_lo = (j & 255).astype(F32)
    j_hi = (j - (j & 255)).astype(F32)
    vtt = p_v.T
    ones_rows = jnp.where(lax.broadcasted_iota(jnp.int32, (V_ROWS - HEAD_DIM, tm), 0) == 0, 1.0, 0.0)
    for h in range(N_ATT_HEADS):
        slope = slope_ref[h]
        kk = p_k[:, h * HEAD_DIM:(h + 1) * HEAD_DIM]
        aug = jnp.where(lane == QK_DIM, slope * j_hi, jnp.where(lane == QK_DIM + 1, slope * j_lo, 0.0))
        ka_ref[:, 2 * h * HEAD_DIM:(2 * h + 1) * HEAD_DIM] = jnp.where(lane < QK_DIM, kk, aug).astype(BF16)
        ka_ref[:, (2 * h + 1) * HEAD_DIM:(2 * h + 2) * HEAD_DIM] = jnp.where(
            lane < QK_DIM, pltpu.roll(kk, QK_DIM, axis=1), aug).astype(BF16)
        vt_ref[0, h * V_ROWS:h * V_ROWS + HEAD_DIM, :] = vtt[h * HEAD_DIM:(h + 1) * HEAD_DIM].astype(BF16)
        vt_ref[0, h * V_ROWS + HEAD_DIM:(h + 1) * V_ROWS, :] = ones_rows.astype(BF16)

    def grp(g, hprev):
        r0 = pl.multiple_of(g * SUBLANES, SUBLANES)
        hg = u_s[pl.ds(r0, SUBLANES), :] + a_s[pl.ds(r0, SUBLANES), :] * hprev
        u_s[pl.ds(r0, SUBLANES), :] = hg
        return hg[SUBLANES - 1:SUBLANES, :]

    hlast = lax.fori_loop(0, tm // SUBLANES, grp, carry_h[0:1, :], unroll=8)
    carry_h[0:1, :] = hlast
    rnn_ref[...] = (u_s[...] * _gelu_tanh(p_lru[:, c_w:])).astype(rnn_ref.dtype)


def _gelu_tanh(x):
    return 0.5 * x * (1.0 + jnp.tanh(math.sqrt(2.0 / math.pi) * (x + 0.044715 * (x * x * x))))


def _alibi_slopes():
    nh = N_ATT_HEADS
    return jnp.asarray(np.array([2.0 ** (-8.0 * (i + 1) / nh) for i in range(nh)], dtype=np.float32))


def _in_proj(xf, norm_w, w_in_bf, att_w, s_len, conv_w, conv_b, w_gates_bf, b_gates, lru_lambda):
    t, d = xf.shape
    n = w_in_bf.shape[1]
    tm = min(ATT_TILE, t)
    ch = min(LRU_CHUNK, tm)
    nh = N_ATT_HEADS
    c_w = (n - 3 * att_w) // 2
    fix = lambda i: (0, 0)
    return pl.pallas_call(
        functools.partial(_inproj_kernel, att_w=att_w, s_len=s_len, ch=ch),
        grid=(t // tm,),
        in_specs=[pl.BlockSpec(memory_space=pltpu.SMEM),
                  pl.BlockSpec((tm, d), lambda i: (i, 0)),
                  pl.BlockSpec((1, d), fix),
                  pl.BlockSpec((d, n), fix),
                  pl.BlockSpec((CONV_WIDTH, c_w), fix),
                  pl.BlockSpec((1, c_w), fix),
                  pl.BlockSpec((c_w, 2 * c_w), fix),
                  pl.BlockSpec((1, 2 * c_w), fix),
                  pl.BlockSpec((1, c_w), fix)],
        out_specs=[pl.BlockSpec((1, 2 * att_w, tm), lambda i: (i, 0, 0)),
                   pl.BlockSpec((tm, 2 * att_w), lambda i: (i, 0)),
                   pl.BlockSpec((1, nh * V_ROWS, tm), lambda i: (i, 0, 0)),
                   pl.BlockSpec((tm, c_w), lambda i: (i, 0))],
        out_shape=[jax.ShapeDtypeStruct((t // tm, 2 * att_w, tm), BF16),
                   jax.ShapeDtypeStruct((t, 2 * att_w), BF16),
                   jax.ShapeDtypeStruct((t // tm, nh * V_ROWS, tm), BF16),
                   jax.ShapeDtypeStruct((t, c_w), BF16)],
        scratch_shapes=[pltpu.VMEM((tm + SUBLANES, c_w), F32), pltpu.VMEM((SUBLANES, c_w), F32),
                        pltpu.VMEM((tm, c_w), F32), pltpu.VMEM((tm, c_w), F32)],
        compiler_params=_cparams(("arbitrary",)),
        name="in_proj",
    )(_alibi_slopes(), xf, norm_w.reshape(1, d), w_in_bf, conv_w, conv_b.reshape(1, c_w), w_gates_bf,
      b_gates.reshape(1, 2 * c_w), lru_lambda.reshape(1, c_w))


def _attn_kernel(lam_ref, hw_ref, q_ref, k_ref, vt, o_ref, sb, mx, acc, *, tq, n_heads, lambda_init):
    qi = pl.program_id(2)
    n_maps = 2 * n_heads
    mx[...] = jnp.full(mx.shape, NEG_BIG, F32)
    acc[...] = jnp.zeros(acc.shape, F32)

    def values(c, n, lanes=slice(None)):
        return vt[c, (n // 2) * V_ROWS:(n // 2 + 1) * V_ROWS, lanes]

    def scores(c, slot):
        rows = pl.ds(pl.multiple_of(c * tq, tq), tq)
        for n in range(n_maps):
            sb[n, slot] = jnp.dot(k_ref[0, rows, n * HEAD_DIM:(n + 1) * HEAD_DIM],
                                  q_ref[0, n * HEAD_DIM:(n + 1) * HEAD_DIM, :],
                                  preferred_element_type=F32)

    def softmax_pv(c, slot):
        for n in range(n_maps):
            s = sb[n, slot]
            m_prev = mx[n]
            m_new = jnp.maximum(m_prev, jnp.max(s, axis=0, keepdims=True))
            p = jnp.exp(s - m_new).astype(BF16)
            acc[n] = jnp.exp(m_prev - m_new) * acc[n] + jnp.dot(values(c, n), p, preferred_element_type=F32)
            mx[n] = m_new

    def softmax_pv_diagonal(c, slot):
        hq = tq // 2
        keep_t = (lax.broadcasted_iota(jnp.int32, (hq, tq), 0) <= lax.broadcasted_iota(jnp.int32, (hq, tq), 1))
        keep_b = (lax.broadcasted_iota(jnp.int32, (hq, hq), 0) <= lax.broadcasted_iota(jnp.int32, (hq, hq), 1))
        for n in range(n_maps):
            top = jnp.where(keep_t, sb[n, slot, :hq, :], NEG_BIG)
            bot = jnp.where(keep_b, sb[n, slot, hq:, hq:], NEG_BIG)
            mt = jnp.max(top, axis=0, keepdims=True)
            mb = jnp.max(bot, axis=0, keepdims=True)
            m_prev = mx[n]
            m_new = jnp.maximum(m_prev, jnp.concatenate([mt[:, :hq], jnp.maximum(mt[:, hq:], mb)], axis=1))
            p_top = jnp.exp(top - m_new).astype(BF16)
            p_bot = jnp.exp(bot - m_new[:, hq:]).astype(BF16)
            acc[n] = (jnp.exp(m_prev - m_new) * acc[n]
                      + jnp.dot(values(c, n, slice(0, hq)), p_top, preferred_element_type=F32))
            acc[n, :, hq:] += jnp.dot(values(c, n, slice(hq, tq)), p_bot, preferred_element_type=F32)
            mx[n] = m_new

    scores(0, 0)

    def body(j, c):
        scores(2 * j + 1, 1)
        softmax_pv(2 * j, 0)
        scores(2 * j + 2, 0)
        softmax_pv(2 * j + 1, 1)
        return c

    lax.fori_loop(0, qi // 2, body, 0)

    @pl.when(qi % 2 == 0)
    def _():
        softmax_pv_diagonal(qi, 0)

    @pl.when(qi % 2 == 1)
    def _():
        scores(qi, 1)
        softmax_pv(qi - 1, 0)
        softmax_pv_diagonal(qi, 1)

    lam = (jnp.exp(jnp.sum(lam_ref[0:1, :] * lam_ref[1:2, :], axis=1, keepdims=True))
           - jnp.exp(jnp.sum(lam_ref[2:3, :] * lam_ref[3:4, :], axis=1, keepdims=True))
           + lambda_init)
    for hh in range(n_heads):
        o1 = acc[2 * hh, :HEAD_DIM, :] * (1.0 / acc[2 * hh, HEAD_DIM:HEAD_DIM + 1, :])
        o2 = acc[2 * hh + 1, :HEAD_DIM, :] * (1.0 / acc[2 * hh + 1, HEAD_DIM:HEAD_DIM + 1, :])
        o = o1 - lam * o2
        o = o * lax.rsqrt(jnp.mean(o * o, axis=0, keepdims=True) + HEAD_NORM_EPS)
        o_ref[0, :, hh * HEAD_DIM:(hh + 1) * HEAD_DIM] = (
            o.T * hw_ref[...] * (1.0 - lambda_init)).astype(o_ref.dtype)


def _diff_attention(qt, ka, vt, lam_params, head_norm_w, lambda_init, b, s):
    nh = N_ATT_HEADS
    hp = ATT_HEADS_PER_STEP
    tq = qt.shape[2]
    nq = s // tq
    return pl.pallas_call(
        functools.partial(_attn_kernel, tq=tq, n_heads=hp, lambda_init=lambda_init),
        grid=(b, nh // hp, nq),
        in_specs=[pl.BlockSpec((4, QK_DIM), lambda bi, hi, qi: (0, 0)),
                  pl.BlockSpec((1, HEAD_DIM), lambda bi, hi, qi: (0, 0)),
                  pl.BlockSpec((1, hp * 2 * HEAD_DIM, tq), lambda bi, hi, qi: (bi * nq + qi, hi, 0)),
                  pl.BlockSpec((1, s, hp * 2 * HEAD_DIM), lambda bi, hi, qi: (bi, 0, hi)),
                  pl.BlockSpec((nq, hp * V_ROWS, tq), lambda bi, hi, qi: (bi, hi, 0))],
        out_specs=pl.BlockSpec((1, tq, hp * HEAD_DIM), lambda bi, hi, qi: (bi, qi, hi)),
        out_shape=jax.ShapeDtypeStruct((b, s, nh * HEAD_DIM), BF16),
        scratch_shapes=[pltpu.VMEM((2 * hp, 2, tq, tq), F32), pltpu.VMEM((2 * hp, 1, tq), F32),
                        pltpu.VMEM((2 * hp, V_ROWS, tq), F32)],
        compiler_params=_cparams(("parallel", "parallel", "arbitrary"), vmem=56 * 1024 * 1024),
        name="diff_attn",
    )(lam_params, head_norm_w.reshape(1, HEAD_DIM), qt, ka.reshape(b, s, ka.shape[1]), vt)


def _outproj_kernel(att_ref, rnn_ref, x_ref, wo_ref, nw_ref, wrt_ref, brc_ref,
                    x1_ref, hn_ref, route_ref, route_t_ref, n8_ref, *, att_w):
    y = jnp.dot(att_ref[...], wo_ref[:att_w, :], preferred_element_type=F32)
    y = y + jnp.dot(rnn_ref[...], wo_ref[att_w:, :], preferred_element_type=F32)
    x1 = x_ref[...] + y
    x1_ref[...] = x1
    hn = (x1 * lax.rsqrt(jnp.mean(x1 * x1, axis=-1, keepdims=True) + NORM_EPS) * nw_ref[...]).astype(BF16)
    hn_ref[...] = hn
    tm = hn.shape[0]

    lg = lax.dot_general(wrt_ref[...], hn, (((1,), (1,)), ((), ())), preferred_element_type=F32)
    lg = lg[:ROUTE_ROWS] + brc_ref[:ROUTE_ROWS, 0:1]
    rowf = lax.broadcasted_iota(jnp.int32, lg.shape, 0).astype(F32)
    big = float(LANES)
    ninf = -jnp.inf
    is_g = rowf < N_GROUPS
    lgm = jnp.where(is_g, lg, ninf)
    mg = jnp.max(lgm, axis=0, keepdims=True)
    g_sel = jnp.min(jnp.where(lgm == mg, rowf, big), axis=0, keepdims=True)
    pg = 1.0 / jnp.sum(jnp.where(is_g, jnp.exp(lgm - mg), 0.0), axis=0, keepdims=True)
    lo = N_GROUPS + EXPERTS_PER_GROUP * g_sel
    in_grp = (rowf >= lo) & (rowf < lo + EXPERTS_PER_GROUP)
    lem = jnp.where(in_grp, lg, ninf)
    v1 = jnp.max(lem, axis=0, keepdims=True)
    i1 = jnp.min(jnp.where(lem == v1, rowf, big), axis=0, keepdims=True)
    lem2 = jnp.where(rowf == i1, ninf, lem)
    v2 = jnp.max(lem2, axis=0, keepdims=True)
    i2 = jnp.min(jnp.where(lem2 == v2, rowf, big), axis=0, keepdims=True)
    e2 = jnp.exp(v2 - v1)
    den = 1.0 + e2
    g1 = pg / den
    g2 = pg * e2 / den

    oh1 = jnp.where(rowf == i1, 1.0, 0.0)
    oh2 = jnp.where(rowf == i2, 1.0, 0.0)
    oh = oh1 + oh2
    earlier = (lax.broadcasted_iota(jnp.int32, (tm, tm), 0)
               < lax.broadcasted_iota(jnp.int32, (tm, tm), 1)).astype(BF16)
    pref = jnp.dot(oh.astype(BF16), earlier, preferred_element_type=F32)
    cnt = jnp.sum(oh, axis=1, keepdims=True)
    n8 = jnp.floor((cnt + (SUBLANES - 1)) * (1.0 / SUBLANES))
    n8_b = jnp.broadcast_to(n8, (ROUTE_ROWS, LANES))
    before = (lax.broadcasted_iota(jnp.int32, (ROUTE_ROWS, ROUTE_ROWS), 1)
              < lax.broadcasted_iota(jnp.int32, (ROUTE_ROWS, ROUTE_ROWS), 0)).astype(BF16)
    loff8 = jnp.dot(before, n8_b.astype(BF16), preferred_element_type=F32)[:, 0:1]
    pos = SUBLANES * loff8 + pref
    lp1 = jnp.sum(oh1 * pos, axis=0, keepdims=True)
    lp2 = jnp.sum(oh2 * pos, axis=0, keepdims=True)
    route_t = jnp.concatenate([g1, g2, lp1, lp2, jnp.zeros((LANES - 4, tm), F32)], axis=0)
    route_t_ref[0] = route_t[:SUBLANES]
    route_ref[...] = route_t.T
    n8_ref[0] = n8_b


def _out_proj(att, rnn, xf, w_out_bf, norm_w, w_route_t_bf, b_route_col):
    t, d = xf.shape
    att_w = att.shape[1]
    tm = min(ROW_TILE, t)
    row = lambda i: (i, 0)
    fix = lambda i: (0, 0)
    return pl.pallas_call(
        functools.partial(_outproj_kernel, att_w=att_w),
        grid=(t // tm,),
        in_specs=[pl.BlockSpec((tm, att_w), row), pl.BlockSpec((tm, rnn.shape[1]), row),
                  pl.BlockSpec((tm, d), row), pl.BlockSpec(w_out_bf.shape, fix),
                  pl.BlockSpec((1, d), fix), pl.BlockSpec((LANES, d), fix), pl.BlockSpec((LANES, 1), fix)],
        out_specs=[pl.BlockSpec((tm, d), row), pl.BlockSpec((tm, d), row), pl.BlockSpec((tm, LANES), row),
                   pl.BlockSpec((1, SUBLANES, tm), lambda i: (i, 0, 0)),
                   pl.BlockSpec((1, ROUTE_ROWS, LANES), lambda i: (i, 0, 0))],
        out_shape=[jax.ShapeDtypeStruct((t, d), F32), jax.ShapeDtypeStruct((t, d), BF16),
                   jax.ShapeDtypeStruct((t, LANES), F32),
                   jax.ShapeDtypeStruct((t // tm, SUBLANES, tm), F32),
                   jax.ShapeDtypeStruct((t // tm, ROUTE_ROWS, LANES), F32)],
        compiler_params=_cparams(("parallel",)),
        name="out_proj",
    )(att, rnn, xf, w_out_bf, norm_w.reshape(1, d), w_route_t_bf, b_route_col)


def _local_rows(tm):
    return -(-(TOP_K * tm + N_EXPERTS * (SUBLANES - 1)) // LANES) * LANES


def _segment_tables(n8_tiles, tm_moe, n_tiles):
    n8 = n8_tiles[:, N_GROUPS:N_GROUPS + N_EXPERTS, 0].astype(jnp.int32)
    c8 = n8 * SUBLANES
    loff = jnp.cumsum(c8, axis=1) - c8
    gtot = jnp.sum(c8, axis=0)
    gpad = (gtot + tm_moe - 1) // tm_moe * tm_moe
    gend = jnp.cumsum(gpad)
    gstart = gend - gpad
    gbase = gstart[None, :] + jnp.cumsum(c8, axis=0) - c8
    tile_row0 = jnp.arange(n_tiles, dtype=jnp.int32) * tm_moe
    tile_e = jnp.minimum(jnp.sum((gend[None, :] <= tile_row0[:, None]).astype(jnp.int32), axis=1),
                         N_EXPERTS - 1).astype(jnp.int32)
    n_used = (gend[-1] // tm_moe).astype(jnp.int32).reshape(1)
    tail_start = (gstart + gtot).astype(jnp.int32)
    tail_n8 = ((gpad - gtot) // SUBLANES).astype(jnp.int32)
    after = gend[tile_e] // tm_moe
    next_e = jnp.where(after < n_used[0], tile_e[jnp.minimum(after, n_tiles - 1)], -1).astype(jnp.int32)
    first = jnp.concatenate([jnp.ones((1,), jnp.int32), (tile_e[1:] != tile_e[:-1]).astype(jnp.int32)])
    w_slot = ((jnp.cumsum(first) - 1) % 2).astype(jnp.int32)
    rows_in_tile = jnp.clip((gstart + gtot)[tile_e] - tile_row0, 1, tm_moe)
    tile_sub = ((rows_in_tile + MOE_SUBTILE - 1) // MOE_SUBTILE).astype(jnp.int32)
    return (n8.reshape(-1), loff.reshape(-1).astype(jnp.int32), gbase.reshape(-1).astype(jnp.int32),
            tile_e, n_used, tail_start, tail_n8, next_e, w_slot, tile_sub)


def _segment_copies(n8_ref, src_off_ref, dst_off_ref, src, dst, sem, tile, wait):
    def rows_of(e):
        return pl.multiple_of(n8_ref[tile * N_EXPERTS + e] * SUBLANES, SUBLANES)

    if wait:
        total = lax.fori_loop(0, N_EXPERTS, lambda e, acc: acc + rows_of(e), 0)
        total = pl.multiple_of(total, SUBLANES)
        pltpu.make_async_copy(src.at[pl.ds(0, total), :], dst.at[pl.ds(0, total), :], sem).wait()
        return

    def per_expert(e, c):
        k = tile * N_EXPERTS + e
        rows = rows_of(e)

        @pl.when(rows > 0)
        def _():
            pltpu.make_async_copy(
                src.at[pl.ds(pl.multiple_of(src_off_ref[k], SUBLANES), rows), :],
                dst.at[pl.ds(pl.multiple_of(dst_off_ref[k], SUBLANES), rows), :], sem).start()
        return c
    lax.fori_loop(0, N_EXPERTS, per_expert, 0)


def _pack_bf16_pairs(x):
    n = x.shape[1] // 2
    bits = lax.bitcast_convert_type(x, jnp.uint32)
    return (bits[:, :n] >> 16) | (bits[:, n:] & jnp.uint32(0xFFFF0000))


def _unpack_bf16_pairs(p):
    lo = lax.bitcast_convert_type(p << 16, F32)
    hi = lax.bitcast_convert_type(p & jnp.uint32(0xFFFF0000), F32)
    return jnp.concatenate([lo, hi], axis=1).astype(BF16)


def _dispatch_kernel(n8_ref, loff_ref, gbase_ref, tstart_ref, tn8_ref, nu_ref, hn_ref, route_ref, xs_hbm,
                     stage, zbuf, sem, zsem, *, lcap, n_tt):
    i = pl.program_id(0)
    slot = i % 2
    tm = hn_ref.shape[0]
    tm_moe = zbuf.shape[0]
    n_tiles = xs_hbm.shape[0] // tm_moe

    def tail_copies(wait):
        def go(cp):
            if wait:
                cp.wait()
            else:
                cp.start()

        def per_expert(e, c):
            rows = pl.multiple_of(tn8_ref[e] * SUBLANES, SUBLANES)

            @pl.when(rows > 0)
            def _():
                go(pltpu.make_async_copy(
                    zbuf.at[pl.ds(0, rows), :],
                    xs_hbm.at[pl.ds(pl.multiple_of(tstart_ref[e], SUBLANES), rows), :], zsem.at[0]))
            return c
        lax.fori_loop(0, N_EXPERTS, per_expert, 0)

        def per_unused_tile(j, c):
            go(pltpu.make_async_copy(zbuf, xs_hbm.at[pl.ds(pl.multiple_of(j * tm_moe, tm_moe), tm_moe), :],
                                     zsem.at[0]))
            return c
        lax.fori_loop(nu_ref[0], n_tiles, per_unused_tile, 0)

    @pl.when(i == 0)
    def _():
        zbuf[...] = jnp.zeros(zbuf.shape, zbuf.dtype)
        tail_copies(False)

    @pl.when(i >= 2)
    def _():
        _segment_copies(n8_ref, loff_ref, gbase_ref, stage.at[slot], xs_hbm, sem.at[slot], i - 2, True)

    lp1 = route_ref[0, 2:3, :]
    lp2 = route_ref[0, 3:4, :]
    rpos = lax.broadcasted_iota(jnp.int32, (lcap, tm), 0).astype(F32)
    sel = jnp.where((rpos == lp1) | (rpos == lp2), 1.0, 0.0).astype(BF16)
    stage[slot] = _pack_bf16_pairs(jnp.dot(sel, hn_ref[...], preferred_element_type=F32))
    _segment_copies(n8_ref, loff_ref, gbase_ref, stage.at[slot], xs_hbm, sem.at[slot], i, False)

    @pl.when(i == n_tt - 1)
    def _():
        _segment_copies(n8_ref, loff_ref, gbase_ref, stage.at[slot], xs_hbm, sem.at[slot], i, True)
        if n_tt > 1:
            _segment_copies(n8_ref, loff_ref, gbase_ref, stage.at[1 - slot], xs_hbm, sem.at[1 - slot],
                            i - 1, True)
        tail_copies(True)


def _dispatch(hn, route, tables, n_rows, tm_moe):
    t, d = hn.shape
    tm = min(ROW_TILE, t)
    n_tt = t // tm
    lcap = _local_rows(tm)
    n8, loff, gbase, _, n_used, tail_start, tail_n8 = tables[:7]
    grid_spec = pltpu.PrefetchScalarGridSpec(
        num_scalar_prefetch=6,
        grid=(n_tt,),
        in_specs=[pl.BlockSpec((tm, d), lambda i, *_: (i, 0)),
                  pl.BlockSpec((1, SUBLANES, tm), lambda i, *_: (i, 0, 0))],
        out_specs=pl.BlockSpec(memory_space=pl.ANY),
        scratch_shapes=[pltpu.VMEM((2, lcap, d // 2), jnp.uint32), pltpu.VMEM((tm_moe, d // 2), jnp.uint32),
                        pltpu.SemaphoreType.DMA((2,)), pltpu.SemaphoreType.DMA((1,))],
    )
    return pl.pallas_call(
        functools.partial(_dispatch_kernel, lcap=lcap, n_tt=n_tt),
        grid_spec=grid_spec,
        out_shape=jax.ShapeDtypeStruct((n_rows, d // 2), jnp.uint32),
        compiler_params=_cparams(("arbitrary",), has_side_effects=True),
        name="dispatch",
    )(n8, loff, gbase, tail_start, tail_n8, n_used, hn, route)


def _moe_kernel(te_ref, nu_ref, nxt_ref, wslot_ref, nsub_ref, xs_ref, wg_hbm, wu_hbm, wd_hbm, y_ref,
                wgf, wuf, wdf, wgb, wub, wdb, wsem):
    i = pl.program_id(0)

    def weight_copies(e, sl):
        return (pltpu.make_async_copy(wg_hbm.at[e], wgf.at[sl], wsem.at[sl, 0]),
                pltpu.make_async_copy(wu_hbm.at[e], wuf.at[sl], wsem.at[sl, 1]),
                pltpu.make_async_copy(wd_hbm.at[e], wdf.at[sl], wsem.at[sl, 2]))

    @pl.when(i == 0)
    def _():
        for cp in weight_copies(te_ref[0], wslot_ref[0]):
            cp.start()

    @pl.when(i < nu_ref[0])
    def _():
        changed = jnp.logical_or(i == 0, te_ref[i] != te_ref[jnp.maximum(i - 1, 0)])

        @pl.when(changed)
        def _():
            sl = wslot_ref[i]
            for cp in weight_copies(te_ref[i], sl):
                cp.wait()
            wgb[...] = wgf[sl].astype(BF16)
            wub[...] = wuf[sl].astype(BF16)
            wdb[...] = wdf[sl].astype(BF16)

            @pl.when(nxt_ref[i] >= 0)
            def _():
                for cp in weight_copies(nxt_ref[i], 1 - sl):
                    cp.start()

        tm = xs_ref.shape[0]
        n_sub = tm // MOE_SUBTILE
        filled = nsub_ref[i]
        for k in range(1, n_sub + 1):
            @pl.when(filled == k)
            def _(k=k):
                r = k * MOE_SUBTILE
                x = _unpack_bf16_pairs(xs_ref[:r, :])
                g = jnp.dot(x, wgb[...], preferred_element_type=F32)
                u = jnp.dot(x, wub[...], preferred_element_type=F32)
                hdn = (g * jax.nn.sigmoid(g) * u).astype(BF16)
                y = jnp.dot(hdn, wdb[...], preferred_element_type=F32)
                y_ref[:r, :] = _pack_bf16_pairs(y.astype(BF16).astype(F32))
                if r < tm:
                    y_ref[r:, :] = jnp.zeros((tm - r, y_ref.shape[1]), y_ref.dtype)

    @pl.when(i >= nu_ref[0])
    def _():
        y_ref[...] = jnp.zeros(y_ref.shape, y_ref.dtype)


def _moe(xs, tile_e, n_used, next_e, w_slot, tile_sub, w_g, w_u, w_d, tm):
    n_rows = xs.shape[0]
    d = w_g.shape[1]
    dp = xs.shape[1]
    n_tiles = n_rows // tm
    ff = w_g.shape[2]
    row_blk = lambda i, te, nu, *_: (jnp.minimum(i, nu[0] - 1), 0)
    hbm = pl.BlockSpec(memory_space=pl.ANY)
    grid_spec = pltpu.PrefetchScalarGridSpec(
        num_scalar_prefetch=5,
        grid=(n_tiles,),
        in_specs=[pl.BlockSpec((tm, dp), row_blk), hbm, hbm, hbm],
        out_specs=pl.BlockSpec((tm, dp), lambda i, *_: (i, 0)),
        scratch_shapes=[pltpu.VMEM((2, d, ff), F32), pltpu.VMEM((2, d, ff), F32), pltpu.VMEM((2, ff, d), F32),
                        pltpu.VMEM((d, ff), BF16), pltpu.VMEM((d, ff), BF16), pltpu.VMEM((ff, d), BF16),
                        pltpu.SemaphoreType.DMA((2, 3))],
    )
    return pl.pallas_call(
        _moe_kernel,
        grid_spec=grid_spec,
        out_shape=jax.ShapeDtypeStruct((n_rows, dp), jnp.uint32),
        compiler_params=_cparams(("arbitrary",)),
        name="moe",
    )(tile_e, n_used, next_e, w_slot, tile_sub, xs, w_g, w_u, w_d)


def _combine_kernel(n8_ref, loff_ref, gbase_ref, x1_ref, route_ref, nw_ref, y_hbm, o_ref,
                    ybuf, sem, *, lcap, n_tt):
    i = pl.program_id(0)
    slot = i % 2
    tm = x1_ref.shape[0]

    def fetch(tile, sl, wait):
        _segment_copies(n8_ref, gbase_ref, loff_ref, y_hbm, ybuf.at[sl], sem.at[sl], tile, wait)

    @pl.when(i == 0)
    def _():
        ybuf[...] = jnp.zeros(ybuf.shape, ybuf.dtype)
        fetch(0, 0, False)

    @pl.when(i + 1 < n_tt)
    def _():
        fetch(i + 1, 1 - slot, False)

    fetch(i, slot, True)
    g1 = route_ref[:, 0:1]
    g2 = route_ref[:, 1:2]
    lp1 = route_ref[:, 2:3]
    lp2 = route_ref[:, 3:4]
    cpos = lax.broadcasted_iota(jnp.int32, (tm, lcap), 1).astype(F32)
    gsel = (jnp.where(cpos == lp1, g1, 0.0) + jnp.where(cpos == lp2, g2, 0.0)).astype(BF16)
    moe = jnp.dot(gsel, _unpack_bf16_pairs(ybuf[slot]), preferred_element_type=F32)
    x = x1_ref[...] + moe
    o_ref[...] = x * lax.rsqrt(jnp.mean(x * x, axis=-1, keepdims=True) + NORM_EPS) * nw_ref[...]


def _combine(x1, y, route, norm_w, tables):
    t, d = x1.shape
    tm = min(ROW_TILE, t)
    n_tt = t // tm
    lcap = _local_rows(tm)
    n8, loff, gbase = tables[:3]
    grid_spec = pltpu.PrefetchScalarGridSpec(
        num_scalar_prefetch=3,
        grid=(n_tt,),
        in_specs=[pl.BlockSpec((tm, d), lambda i, *_: (i, 0)),
                  pl.BlockSpec((tm, LANES), lambda i, *_: (i, 0)),
                  pl.BlockSpec((1, d), lambda i, *_: (0, 0)),
                  pl.BlockSpec(memory_space=pl.ANY)],
        out_specs=pl.BlockSpec((tm, d), lambda i, *_: (i, 0)),
        scratch_shapes=[pltpu.VMEM((2, lcap, d // 2), jnp.uint32), pltpu.SemaphoreType.DMA((2,))],
    )
    return pl.pallas_call(
        functools.partial(_combine_kernel, lcap=lcap, n_tt=n_tt),
        grid_spec=grid_spec,
        out_shape=jax.ShapeDtypeStruct((t, d), F32),
        compiler_params=_cparams(("arbitrary",)),
        name="combine",
    )(n8, loff, gbase, x1, route, norm_w.reshape(1, d), y)


def _block_diag(w):
    n, bi, bj = w.shape
    eye = jnp.eye(n, dtype=w.dtype)
    return jnp.einsum('nij,nm->nimj', w, eye).reshape(n * bi, n * bj)


def kernel(x, mix_norm_w, w_in, lambda_q1, lambda_k1, lambda_q2, lambda_k2, head_norm_w, conv_w, conv_b, w_rgate, b_rgate, w_igate, b_igate, lru_lambda, w_out, ffn_norm_w, w_router_group, b_router_group, w_router_expert, b_router_expert, w_exp_gate, w_exp_up, w_exp_down, final_norm_w):
    b, s, d = x.shape
    t = b * s
    assert w_in.shape[0] == 1, "single-layer stack only"
    att_w = N_ATT_HEADS * HEAD_DIM
    tm_moe = MOE_TILE
    xf = x.reshape(t, d)
    for l in range(1):
        lambda_init = 0.8 - 0.6 * math.exp(-0.3 * l)
        assert s % ATT_TILE == 0, "sequence length must be a multiple of the attention tile"
        w_bd = jnp.concatenate([_block_diag(w_rgate[l]), _block_diag(w_igate[l])], axis=1).astype(BF16)
        b_cat = jnp.concatenate([b_rgate[l], b_igate[l]])
        qt, ka, vt, rnn = _in_proj(xf, mix_norm_w[l], w_in[l].astype(BF16), att_w, s,
                                   conv_w[l], conv_b[l], w_bd, b_cat, lru_lambda[l])
        lam_params = jnp.stack([lambda_q1[l], lambda_k1[l], lambda_q2[l], lambda_k2[l]]).astype(F32)
        att = _diff_attention(qt, ka, vt, lam_params, head_norm_w[l], lambda_init, b, s)
        w_route = jnp.concatenate([w_router_group[l], w_router_expert[l]], axis=1).T
        w_route = jnp.pad(w_route, ((0, LANES - w_route.shape[0]), (0, 0))).astype(BF16)
        b_route = jnp.concatenate([b_router_group[l], b_router_expert[l]])
        b_route = jnp.pad(b_route, (0, LANES - b_route.shape[0])).reshape(LANES, 1).astype(F32)
        x1, hn, route, route_t, n8_tiles = _out_proj(att.reshape(t, att_w), rnn.reshape(t, -1), xf,
                                                     w_out[l].astype(BF16), ffn_norm_w[l], w_route, b_route)
        n_tt = n8_tiles.shape[0]
        max_rows = TOP_K * t + n_tt * N_EXPERTS * (SUBLANES - 1) + N_EXPERTS * (tm_moe - 1)
        n_tiles = -(-max_rows // tm_moe)
        tables = _segment_tables(n8_tiles, tm_moe, n_tiles)
        xs = _dispatch(hn, route_t, tables, n_tiles * tm_moe, tm_moe)
        y = _moe(xs, tables[3], tables[4], tables[7], tables[8], tables[9],
                 w_exp_gate[l], w_exp_up[l], w_exp_down[l], tm_moe)
        out = _combine(x1, y, route, final_norm_w, tables)
    return out.reshape(b, s, d)
```

```python
import functools
import math

import numpy as np
import jax
import jax.numpy as jnp
from jax import lax
from jax.experimental import pallas as pl
from jax.experimental.pallas import tpu as pltpu

F32 = jnp.float32
BF16 = jnp.bfloat16

N_ATT_HEADS = 4
HEAD_DIM = 128
QK_DIM = 64
N_RNN_BLOCKS = 8
CONV_WIDTH = 4
LRU_C = 8.0
N_GROUPS = 4
EXPERTS_PER_GROUP = 8
N_EXPERTS = N_GROUPS * EXPERTS_PER_GROUP
TOP_K = 2
NORM_EPS = 1e-6
HEAD_NORM_EPS = 1e-5
LANES = 128
SUBLANES = 8
NEG_BIG = -1e30

ROW_TILE = 512
ROUTE_ROWS = 48
ATT_TILE = 512
ATT_HEADS_PER_STEP = 4
V_ROWS = HEAD_DIM + 16
LRU_CHUNK = 128
MOE_TILE = 512
MOE_SUBTILE = 128
VMEM_LIMIT = 48 * 1024 * 1024


def _cparams(sem, vmem=VMEM_LIMIT, **kw):
    return pltpu.CompilerParams(dimension_semantics=sem, vmem_limit_bytes=vmem, **kw)


def _inproj_kernel(slope_ref, x_ref, nw_ref, w_ref, cw_ref, cb_ref, wg_ref, bg_ref, lam_ref,
                   qt_ref, ka_ref, vt_ref, rnn_ref, xs, carry_h, a_s, u_s, *, att_w, s_len, ch):
    i = pl.program_id(0)
    x = x_ref[...]
    tm = x.shape[0]
    c_w = (w_ref.shape[1] - 3 * att_w) // 2

    @pl.when((i * tm) % s_len == 0)
    def _():
        xs[0:SUBLANES, :] = jnp.zeros((SUBLANES, c_w), F32)
        carry_h[...] = jnp.zeros(carry_h.shape, F32)

    ms = jnp.mean(x * x, axis=-1, keepdims=True)
    hn = (x * lax.rsqrt(ms + NORM_EPS) * nw_ref[...]).astype(BF16)
    p_lru = jnp.dot(hn, w_ref[:, 3 * att_w:], preferred_element_type=F32)

    xs[SUBLANES:, :] = p_lru[:, :c_w]
    neg_lam = -lam_ref[...]
    sp = jnp.maximum(neg_lam, 0.0) + jnp.log1p(jnp.exp(-jnp.abs(neg_lam)))
    cw = cw_ref[...]
    cb = cb_ref[...]
    bias = bg_ref[...]
    r8 = lax.broadcasted_iota(jnp.int32, (ch // SUBLANES, SUBLANES, LANES), 1)
    for c in range(tm // ch):
        r0 = c * ch
        win = xs[r0:r0 + ch + SUBLANES, :]
        xc = cw[3:4, :] * win[SUBLANES:] + cb
        for k in (1, 2, 3):
            xc = xc + cw[3 - k:4 - k, :] * pltpu.roll(win, k, axis=0)[SUBLANES:]
        z = jnp.dot(xc.astype(BF16), wg_ref[...], preferred_element_type=F32) + bias
        for l0 in range(0, c_w, LANES):
            ls = slice(l0, l0 + LANES)
            r = jax.nn.sigmoid(z[:, l0:l0 + LANES])
            ig = jax.nn.sigmoid(z[:, c_w + l0:c_w + l0 + LANES])
            log_a = (-LRU_C) * r * sp[:, ls]
            a = jnp.exp(log_a)
            w = jnp.tanh(-log_a) * (1.0 + a * a)
            u = jnp.where(w > 0.0, w * lax.rsqrt(w), 0.0) * ig * xc[:, ls]
            a = a.reshape(ch // SUBLANES, SUBLANES, LANES)
            u = u.reshape(ch // SUBLANES, SUBLANES, LANES)
            for k in (1, 2, 4):
                a_sh = pltpu.roll(a, k, axis=1)
                u_sh = pltpu.roll(u, k, axis=1)
                ok = r8 >= k
                u = jnp.where(ok, u + a * u_sh, u)
                a = jnp.where(ok, a * a_sh, a)
            a_s[r0:r0 + ch, ls] = a.reshape(ch, LANES)
            u_s[r0:r0 + ch, ls] = u.reshape(ch, LANES)
    xs[0:SUBLANES, :] = xs[tm:tm + SUBLANES, :]

    p_q = jnp.dot(hn, w_ref[:, :att_w], preferred_element_type=F32)
    p_v = jnp.dot(hn, w_ref[:, 2 * att_w:3 * att_w], preferred_element_type=F32)
    p_k = jnp.dot(hn, w_ref[:, att_w:2 * att_w], preferred_element_type=F32)
    qt = (p_q * (QK_DIM ** -0.5)).T
    ones2 = jnp.where(lax.broadcasted_iota(jnp.int32, (QK_DIM, tm), 0) < 2, 1.0, 0.0)
    pieces = []
    for g in range(2 * N_ATT_HEADS):
        pieces += [qt[g * QK_DIM:(g + 1) * QK_DIM], ones2]
    qt_ref[0] = jnp.concatenate(pieces, axis=0).astype(BF16)

    lane = lax.broadcasted_iota(jnp.int32, (tm, HEAD_DIM), 1)
    j = (i * tm) % s_len + lax.broadcasted_iota(jnp.int32, (tm, HEAD_DIM), 0)
    j_lo = (j & 255).astype(F32)
    j_hi = (j - (j & 255)).astype(F32)
    vtt = p_v.T
    ones_rows = jnp.where(lax.broadcasted_iota(jnp.int32, (V_ROWS - HEAD_DIM, tm), 0) == 0, 1.0, 0.0)
    for h in range(N_ATT_HEADS):
        slope = slope_ref[h]
        kk = p_k[:, h * HEAD_DIM:(h + 1) * HEAD_DIM]
        aug = jnp.where(lane == QK_DIM, slope * j_hi, jnp.where(lane == QK_DIM + 1, slope * j_lo, 0.0))
        ka_ref[:, 2 * h * HEAD_DIM:(2 * h + 1) * HEAD_DIM] = jnp.where(lane < QK_DIM, kk, aug).astype(BF16)
        ka_ref[:, (2 * h + 1) * HEAD_DIM:(2 * h + 2) * HEAD_DIM] = jnp.where(
            lane < QK_DIM, pltpu.roll(kk, QK_DIM, axis=1), aug).astype(BF16)
        vt_ref[0, h * V_ROWS:h * V_ROWS + HEAD_DIM, :] = vtt[h * HEAD_DIM:(h + 1) * HEAD_DIM].astype(BF16)
        vt_ref[0, h * V_ROWS + HEAD_DIM:(h + 1) * V_ROWS, :] = ones_rows.astype(BF16)

    def grp(g, hprev):
        r0 = pl.multiple_of(g * SUBLANES, SUBLANES)
        hg = u_s[pl.ds(r0, SUBLANES), :] + a_s[pl.ds(r0, SUBLANES), :] * hprev
        u_s[pl.ds(r0, SUBLANES), :] = hg
        return hg[SUBLANES - 1:SUBLANES, :]

    hlast = lax.fori_loop(0, tm // SUBLANES, grp, carry_h[0:1, :], unroll=8)
    carry_h[0:1, :] = hlast
    rnn_ref[...] = (u_s[...] * _gelu_tanh(p_lru[:, c_w:])).astype(rnn_ref.dtype)


def _gelu_tanh(x):
    return 0.5 * x * (1.0 + jnp.tanh(math.sqrt(2.0 / math.pi) * (x + 0.044715 * (x * x * x))))


def _alibi_slopes():
    nh = N_ATT_HEADS
    return jnp.asarray(np.array([2.0 ** (-8.0 * (i + 1) / nh) for i in range(nh)], dtype=np.float32))


def _in_proj(xf, norm_w, w_in_bf, att_w, s_len, conv_w, conv_b, w_gates_bf, b_gates, lru_lambda):
    t, d = xf.shape
    n = w_in_bf.shape[1]
    tm = min(ATT_TILE, t)
    ch = min(LRU_CHUNK, tm)
    nh = N_ATT_HEADS
    c_w = (n - 3 * att_w) // 2
    fix = lambda i: (0, 0)
    return pl.pallas_call(
        functools.partial(_inproj_kernel, att_w=att_w, s_len=s_len, ch=ch),
        grid=(t // tm,),
        in_specs=[pl.BlockSpec(memory_space=pltpu.SMEM),
                  pl.BlockSpec((tm, d), lambda i: (i, 0)),
                  pl.BlockSpec((1, d), fix),
                  pl.BlockSpec((d, n), fix),
                  pl.BlockSpec((CONV_WIDTH, c_w), fix),
                  pl.BlockSpec((1, c_w), fix),
                  pl.BlockSpec((c_w, 2 * c_w), fix),
                  pl.BlockSpec((1, 2 * c_w), fix),
                  pl.BlockSpec((1, c_w), fix)],
        out_specs=[pl.BlockSpec((1, 2 * att_w, tm), lambda i: (i, 0, 0)),
                   pl.BlockSpec((tm, 2 * att_w), lambda i: (i, 0)),
                   pl.BlockSpec((1, nh * V_ROWS, tm), lambda i: (i, 0, 0)),
                   pl.BlockSpec((tm, c_w), lambda i: (i, 0))],
        out_shape=[jax.ShapeDtypeStruct((t // tm, 2 * att_w, tm), BF16),
                   jax.ShapeDtypeStruct((t, 2 * att_w), BF16),
                   jax.ShapeDtypeStruct((t // tm, nh * V_ROWS, tm), BF16),
                   jax.ShapeDtypeStruct((t, c_w), BF16)],
        scratch_shapes=[pltpu.VMEM((tm + SUBLANES, c_w), F32), pltpu.VMEM((SUBLANES, c_w), F32),
                        pltpu.VMEM((tm, c_w), F32), pltpu.VMEM((tm, c_w), F32)],
        compiler_params=_cparams(("arbitrary",)),
        name="in_proj",
    )(_alibi_slopes(), xf, norm_w.reshape(1, d), w_in_bf, conv_w, conv_b.reshape(1, c_w), w_gates_bf,
      b_gates.reshape(1, 2 * c_w), lru_lambda.reshape(1, c_w))


def _attn_kernel(lam_ref, hw_ref, q_ref, k_ref, vt, o_ref, sb, mx, acc, *, tq, n_heads, lambda_init):
    g = pl.program_id(2)
    n_maps = 2 * n_heads
    mx[...] = jnp.full(mx.shape, NEG_BIG, F32)
    acc[...] = jnp.zeros(acc.shape, F32)

    def values(c, n, lanes=slice(None)):
        return vt[c, (n // 2) * V_ROWS:(n // 2 + 1) * V_ROWS, lanes]

    def scores(t, c, slot):
        rows = pl.ds(pl.multiple_of(c * tq, tq), tq)
        for n in range(n_maps):
            sb[n, slot] = jnp.dot(k_ref[0, rows, n * HEAD_DIM:(n + 1) * HEAD_DIM],
                                  q_ref[t, n * HEAD_DIM:(n + 1) * HEAD_DIM, :],
                                  preferred_element_type=F32)

    def softmax_pv(t, c, slot):
        for n in range(n_maps):
            s = sb[n, slot]
            m_prev = mx[t, n]
            m_new = jnp.maximum(m_prev, jnp.max(s, axis=0, keepdims=True))
            p = jnp.exp(s - m_new).astype(BF16)
            acc[t, n] = (jnp.exp(m_prev - m_new) * acc[t, n]
                         + jnp.dot(values(c, n), p, preferred_element_type=F32))
            mx[t, n] = m_new

    def softmax_pv_diagonal(t, c, slot):
        hq = tq // 2
        keep_t = (lax.broadcasted_iota(jnp.int32, (hq, tq), 0) <= lax.broadcasted_iota(jnp.int32, (hq, tq), 1))
        keep_b = (lax.broadcasted_iota(jnp.int32, (hq, hq), 0) <= lax.broadcasted_iota(jnp.int32, (hq, hq), 1))
        for n in range(n_maps):
            top = jnp.where(keep_t, sb[n, slot, :hq, :], NEG_BIG)
            bot = jnp.where(keep_b, sb[n, slot, hq:, hq:], NEG_BIG)
            mt = jnp.max(top, axis=0, keepdims=True)
            mb = jnp.max(bot, axis=0, keepdims=True)
            m_prev = mx[t, n]
            m_new = jnp.maximum(m_prev, jnp.concatenate([mt[:, :hq], jnp.maximum(mt[:, hq:], mb)], axis=1))
            p_top = jnp.exp(top - m_new).astype(BF16)
            p_bot = jnp.exp(bot - m_new[:, hq:]).astype(BF16)
            acc[t, n] = (jnp.exp(m_prev - m_new) * acc[t, n]
                         + jnp.dot(values(c, n, slice(0, hq)), p_top, preferred_element_type=F32))
            acc[t, n, :, hq:] += jnp.dot(values(c, n, slice(hq, tq)), p_bot, preferred_element_type=F32)
            mx[t, n] = m_new

    lam = (jnp.exp(jnp.sum(lam_ref[0:1, :] * lam_ref[1:2, :], axis=1, keepdims=True))
           - jnp.exp(jnp.sum(lam_ref[2:3, :] * lam_ref[3:4, :], axis=1, keepdims=True))
           + lambda_init)

    def finish(t):
        for hh in range(n_heads):
            o1 = acc[t, 2 * hh, :HEAD_DIM, :] * (1.0 / acc[t, 2 * hh, HEAD_DIM:HEAD_DIM + 1, :])
            o2 = acc[t, 2 * hh + 1, :HEAD_DIM, :] * (1.0 / acc[t, 2 * hh + 1, HEAD_DIM:HEAD_DIM + 1, :])
            o = o1 - lam * o2
            o = o * lax.rsqrt(jnp.mean(o * o, axis=0, keepdims=True) + HEAD_NORM_EPS)
            o_ref[0, t * tq:(t + 1) * tq, hh * HEAD_DIM:(hh + 1) * HEAD_DIM] = (
                o.T * hw_ref[...] * (1.0 - lambda_init)).astype(o_ref.dtype)

    scores(0, 0, 0)

    def body_even(j, c):
        scores(0, 2 * j + 1, 1)
        softmax_pv(0, 2 * j, 0)
        scores(0, 2 * j + 2, 0)
        softmax_pv(0, 2 * j + 1, 1)
        return c

    lax.fori_loop(0, g, body_even, 0)
    scores(1, 0, 1)
    softmax_pv_diagonal(0, 2 * g, 0)
    finish(0)

    def body_odd(j, c):
        scores(1, 2 * j + 1, 0)
        softmax_pv(1, 2 * j, 1)
        scores(1, 2 * j + 2, 1)
        softmax_pv(1, 2 * j + 1, 0)
        return c

    lax.fori_loop(0, g, body_odd, 0)
    scores(1, 2 * g + 1, 0)
    softmax_pv(1, 2 * g, 1)
    softmax_pv_diagonal(1, 2 * g + 1, 0)
    finish(1)


def _diff_attention(qt, ka, vt, lam_params, head_norm_w, lambda_init, b, s):
    nh = N_ATT_HEADS
    hp = ATT_HEADS_PER_STEP
    tq = qt.shape[2]
    nq = s // tq
    assert nq % 2 == 0, "query tiles are processed in (even, odd) pairs"
    npair = nq // 2
    return pl.pallas_call(
        functools.partial(_attn_kernel, tq=tq, n_heads=hp, lambda_init=lambda_init),
        grid=(b, nh // hp, npair),
        in_specs=[pl.BlockSpec((4, QK_DIM), lambda bi, hi, gi: (0, 0)),
                  pl.BlockSpec((1, HEAD_DIM), lambda bi, hi, gi: (0, 0)),
                  pl.BlockSpec((2, hp * 2 * HEAD_DIM, tq), lambda bi, hi, gi: (bi * npair + gi, hi, 0)),
                  pl.BlockSpec((1, s, hp * 2 * HEAD_DIM), lambda bi, hi, gi: (bi, 0, hi),
                               pipeline_mode=pl.Buffered(1)),
                  pl.BlockSpec((nq, hp * V_ROWS, tq), lambda bi, hi, gi: (bi, hi, 0),
                               pipeline_mode=pl.Buffered(1))],
        out_specs=pl.BlockSpec((1, 2 * tq, hp * HEAD_DIM), lambda bi, hi, gi: (bi, gi, hi)),
        out_shape=jax.ShapeDtypeStruct((b, s, nh * HEAD_DIM), BF16),
        scratch_shapes=[pltpu.VMEM((2 * hp, 2, tq, tq), F32), pltpu.VMEM((2, 2 * hp, 1, tq), F32),
                        pltpu.VMEM((2, 2 * hp, V_ROWS, tq), F32)],
        compiler_params=_cparams(("parallel", "parallel", "arbitrary"), vmem=56 * 1024 * 1024),
        name="diff_attn",
    )(lam_params, head_norm_w.reshape(1, HEAD_DIM), qt, ka.reshape(b, s, ka.shape[1]), vt)


def _outproj_kernel(att_ref, rnn_ref, x_ref, wo_ref, nw_ref, wrt_ref, brc_ref,
                    x1_ref, hn_ref, route_ref, route_t_ref, n8_ref, *, att_w):
    y = jnp.dot(att_ref[...], wo_ref[:att_w, :], preferred_element_type=F32)
    y = y + jnp.dot(rnn_ref[...], wo_ref[att_w:, :], preferred_element_type=F32)
    x1 = x_ref[...] + y
    x1_ref[...] = x1
    hn = (x1 * lax.rsqrt(jnp.mean(x1 * x1, axis=-1, keepdims=True) + NORM_EPS) * nw_ref[...]).astype(BF16)
    hn_ref[...] = hn
    tm = hn.shape[0]

    lg = lax.dot_general(wrt_ref[...], hn, (((1,), (1,)), ((), ())), preferred_element_type=F32)
    lg = lg[:ROUTE_ROWS] + brc_ref[:ROUTE_ROWS, 0:1]
    rowf = lax.broadcasted_iota(jnp.int32, lg.shape, 0).astype(F32)
    big = float(LANES)
    ninf = -jnp.inf
    is_g = rowf < N_GROUPS
    lgm = jnp.where(is_g, lg, ninf)
    mg = jnp.max(lgm, axis=0, keepdims=True)
    g_sel = jnp.min(jnp.where(lgm == mg, rowf, big), axis=0, keepdims=True)
    pg = 1.0 / jnp.sum(jnp.where(is_g, jnp.exp(lgm - mg), 0.0), axis=0, keepdims=True)
    lo = N_GROUPS + EXPERTS_PER_GROUP * g_sel
    in_grp = (rowf >= lo) & (rowf < lo + EXPERTS_PER_GROUP)
    lem = jnp.where(in_grp, lg, ninf)
    v1 = jnp.max(lem, axis=0, keepdims=True)
    i1 = jnp.min(jnp.where(lem == v1, rowf, big), axis=0, keepdims=True)
    lem2 = jnp.where(rowf == i1, ninf, lem)
    v2 = jnp.max(lem2, axis=0, keepdims=True)
    i2 = jnp.min(jnp.where(lem2 == v2, rowf, big), axis=0, keepdims=True)
    e2 = jnp.exp(v2 - v1)
    den = 1.0 + e2
    g1 = pg / den
    g2 = pg * e2 / den

    oh1 = jnp.where(rowf == i1, 1.0, 0.0)
    oh2 = jnp.where(rowf == i2, 1.0, 0.0)
    oh = oh1 + oh2
    earlier = (lax.broadcasted_iota(jnp.int32, (tm, tm), 0)
               < lax.broadcasted_iota(jnp.int32, (tm, tm), 1)).astype(BF16)
    pref = jnp.dot(oh.astype(BF16), earlier, preferred_element_type=F32)
    cnt = jnp.sum(oh, axis=1, keepdims=True)
    n8 = jnp.floor((cnt + (SUBLANES - 1)) * (1.0 / SUBLANES))
    n8_b = jnp.broadcast_to(n8, (ROUTE_ROWS, LANES))
    before = (lax.broadcasted_iota(jnp.int32, (ROUTE_ROWS, ROUTE_ROWS), 1)
              < lax.broadcasted_iota(jnp.int32, (ROUTE_ROWS, ROUTE_ROWS), 0)).astype(BF16)
    loff8 = jnp.dot(before, n8_b.astype(BF16), preferred_element_type=F32)[:, 0:1]
    pos = SUBLANES * loff8 + pref
    lp1 = jnp.sum(oh1 * pos, axis=0, keepdims=True)
    lp2 = jnp.sum(oh2 * pos, axis=0, keepdims=True)
    route_t = jnp.concatenate([g1, g2, lp1, lp2, jnp.zeros((LANES - 4, tm), F32)], axis=0)
    route_t_ref[0] = route_t[:SUBLANES]
    route_ref[...] = route_t.T
    n8_ref[0] = n8_b


def _out_proj(att, rnn, xf, w_out_bf, norm_w, w_route_t_bf, b_route_col):
    t, d = xf.shape
    att_w = att.shape[1]
    tm = min(ROW_TILE, t)
    row = lambda i: (i, 0)
    fix = lambda i: (0, 0)
    return pl.pallas_call(
        functools.partial(_outproj_kernel, att_w=att_w),
        grid=(t // tm,),
        in_specs=[pl.BlockSpec((tm, att_w), row), pl.BlockSpec((tm, rnn.shape[1]), row),
                  pl.BlockSpec((tm, d), row), pl.BlockSpec(w_out_bf.shape, fix),
                  pl.BlockSpec((1, d), fix), pl.BlockSpec((LANES, d), fix), pl.BlockSpec((LANES, 1), fix)],
        out_specs=[pl.BlockSpec((tm, d), row), pl.BlockSpec((tm, d), row), pl.BlockSpec((tm, LANES), row),
                   pl.BlockSpec((1, SUBLANES, tm), lambda i: (i, 0, 0)),
                   pl.BlockSpec((1, ROUTE_ROWS, LANES), lambda i: (i, 0, 0))],
        out_shape=[jax.ShapeDtypeStruct((t, d), F32), jax.ShapeDtypeStruct((t, d), BF16),
                   jax.ShapeDtypeStruct((t, LANES), F32),
                   jax.ShapeDtypeStruct((t // tm, SUBLANES, tm), F32),
                   jax.ShapeDtypeStruct((t // tm, ROUTE_ROWS, LANES), F32)],
        compiler_params=_cparams(("parallel",)),
        name="out_proj",
    )(att, rnn, xf, w_out_bf, norm_w.reshape(1, d), w_route_t_bf, b_route_col)


def _local_rows(tm):
    return -(-(TOP_K * tm + N_EXPERTS * (SUBLANES - 1)) // LANES) * LANES


def _segment_tables(n8_tiles, tm_moe, n_tiles):
    n8 = n8_tiles[:, N_GROUPS:N_GROUPS + N_EXPERTS, 0].astype(jnp.int32)
    c8 = n8 * SUBLANES
    loff = jnp.cumsum(c8, axis=1) - c8
    gtot = jnp.sum(c8, axis=0)
    gpad = (gtot + tm_moe - 1) // tm_moe * tm_moe
    gend = jnp.cumsum(gpad)
    gstart = gend - gpad
    gbase = gstart[None, :] + jnp.cumsum(c8, axis=0) - c8
    tile_row0 = jnp.arange(n_tiles, dtype=jnp.int32) * tm_moe
    tile_e = jnp.minimum(jnp.sum((gend[None, :] <= tile_row0[:, None]).astype(jnp.int32), axis=1),
                         N_EXPERTS - 1).astype(jnp.int32)
    n_used = (gend[-1] // tm_moe).astype(jnp.int32).reshape(1)
    tail_start = (gstart + gtot).astype(jnp.int32)
    tail_n8 = ((gpad - gtot) // SUBLANES).astype(jnp.int32)
    after = gend[tile_e] // tm_moe
    next_e = jnp.where(after < n_used[0], tile_e[jnp.minimum(after, n_tiles - 1)], -1).astype(jnp.int32)
    first = jnp.concatenate([jnp.ones((1,), jnp.int32), (tile_e[1:] != tile_e[:-1]).astype(jnp.int32)])
    w_slot = ((jnp.cumsum(first) - 1) % 2).astype(jnp.int32)
    rows_in_tile = jnp.clip((gstart + gtot)[tile_e] - tile_row0, 1, tm_moe)
    tile_sub = ((rows_in_tile + MOE_SUBTILE - 1) // MOE_SUBTILE).astype(jnp.int32)
    return (n8.reshape(-1), loff.reshape(-1).astype(jnp.int32), gbase.reshape(-1).astype(jnp.int32),
            tile_e, n_used, tail_start, tail_n8, next_e, w_slot, tile_sub)


def _segment_copies(n8_ref, src_off_ref, dst_off_ref, src, dst, sem, tile, wait):
    def rows_of(e):
        return pl.multiple_of(n8_ref[tile * N_EXPERTS + e] * SUBLANES, SUBLANES)

    if wait:
        total = lax.fori_loop(0, N_EXPERTS, lambda e, acc: acc + rows_of(e), 0)
        total = pl.multiple_of(total, SUBLANES)
        pltpu.make_async_copy(src.at[pl.ds(0, total), :], dst.at[pl.ds(0, total), :], sem).wait()
        return

    def per_expert(e, c):
        k = tile * N_EXPERTS + e
        rows = rows_of(e)

        @pl.when(rows > 0)
        def _():
            pltpu.make_async_copy(
                src.at[pl.ds(pl.multiple_of(src_off_ref[k], SUBLANES), rows), :],
                dst.at[pl.ds(pl.multiple_of(dst_off_ref[k], SUBLANES), rows), :], sem).start()
        return c
    lax.fori_loop(0, N_EXPERTS, per_expert, 0)


def _pack_bf16_pairs(x):
    n = x.shape[1] // 2
    bits = lax.bitcast_convert_type(x, jnp.uint32)
    return (bits[:, :n] >> 16) | (bits[:, n:] & jnp.uint32(0xFFFF0000))


def _unpack_bf16_pairs(p):
    lo = lax.bitcast_convert_type(p << 16, F32)
    hi = lax.bitcast_convert_type(p & jnp.uint32(0xFFFF0000), F32)
    return jnp.concatenate([lo, hi], axis=1).astype(BF16)


def _dispatch_kernel(n8_ref, loff_ref, gbase_ref, tstart_ref, tn8_ref, nu_ref, hn_ref, route_ref, xs_hbm,
                     stage, zbuf, sem, zsem, *, lcap, n_tt):
    i = pl.program_id(0)
    slot = i % 2
    tm = hn_ref.shape[0]
    tm_moe = zbuf.shape[0]
    n_tiles = xs_hbm.shape[0] // tm_moe

    def tail_copies(wait):
        def go(cp):
            if wait:
                cp.wait()
            else:
                cp.start()

        def per_expert(e, c):
            rows = pl.multiple_of(tn8_ref[e] * SUBLANES, SUBLANES)

            @pl.when(rows > 0)
            def _():
                go(pltpu.make_async_copy(
                    zbuf.at[pl.ds(0, rows), :],
                    xs_hbm.at[pl.ds(pl.multiple_of(tstart_ref[e], SUBLANES), rows), :], zsem.at[0]))
            return c
        lax.fori_loop(0, N_EXPERTS, per_expert, 0)

        def per_unused_tile(j, c):
            go(pltpu.make_async_copy(zbuf, xs_hbm.at[pl.ds(pl.multiple_of(j * tm_moe, tm_moe), tm_moe), :],
                                     zsem.at[0]))
            return c
        lax.fori_loop(nu_ref[0], n_tiles, per_unused_tile, 0)

    @pl.when(i == 0)
    def _():
        zbuf[...] = jnp.zeros(zbuf.shape, zbuf.dtype)
        tail_copies(False)

    @pl.when(i >= 2)
    def _():
        _segment_copies(n8_ref, loff_ref, gbase_ref, stage.at[slot], xs_hbm, sem.at[slot], i - 2, True)

    lp1 = route_ref[0, 2:3, :]
    lp2 = route_ref[0, 3:4, :]
    rpos = lax.broadcasted_iota(jnp.int32, (lcap, tm), 0).astype(F32)
    sel = jnp.where((rpos == lp1) | (rpos == lp2), 1.0, 0.0).astype(BF16)
    stage[slot] = _pack_bf16_pairs(jnp.dot(sel, hn_ref[...], preferred_element_type=F32))
    _segment_copies(n8_ref, loff_ref, gbase_ref, stage.at[slot], xs_hbm, sem.at[slot], i, False)

    @pl.when(i == n_tt - 1)
    def _():
        _segment_copies(n8_ref, loff_ref, gbase_ref, stage.at[slot], xs_hbm, sem.at[slot], i, True)
        if n_tt > 1:
            _segment_copies(n8_ref, loff_ref, gbase_ref, stage.at[1 - slot], xs_hbm, sem.at[1 - slot],
                            i - 1, True)
        tail_copies(True)


def _dispatch(hn, route, tables, n_rows, tm_moe):
    t, d = hn.shape
    tm = min(ROW_TILE, t)
    n_tt = t // tm
    lcap = _local_rows(tm)
    n8, loff, gbase, _, n_used, tail_start, tail_n8 = tables[:7]
    grid_spec = pltpu.PrefetchScalarGridSpec(
        num_scalar_prefetch=6,
        grid=(n_tt,),
        in_specs=[pl.BlockSpec((tm, d), lambda i, *_: (i, 0)),
                  pl.BlockSpec((1, SUBLANES, tm), lambda i, *_: (i, 0, 0))],
        out_specs=pl.BlockSpec(memory_space=pl.ANY),
        scratch_shapes=[pltpu.VMEM((2, lcap, d // 2), jnp.uint32), pltpu.VMEM((tm_moe, d // 2), jnp.uint32),
                        pltpu.SemaphoreType.DMA((2,)), pltpu.SemaphoreType.DMA((1,))],
    )
    return pl.pallas_call(
        functools.partial(_dispatch_kernel, lcap=lcap, n_tt=n_tt),
        grid_spec=grid_spec,
        out_shape=jax.ShapeDtypeStruct((n_rows, d // 2), jnp.uint32),
        compiler_params=_cparams(("arbitrary",), has_side_effects=True),
        name="dispatch",
    )(n8, loff, gbase, tail_start, tail_n8, n_used, hn, route)


def _moe_kernel(te_ref, nu_ref, nxt_ref, wslot_ref, nsub_ref, xs_ref, wg_hbm, wu_hbm, wd_hbm, y_ref,
                wgf, wuf, wdf, wgb, wub, wdb, wsem):
    i = pl.program_id(0)

    def weight_copies(e, sl):
        return (pltpu.make_async_copy(wg_hbm.at[e], wgf.at[sl], wsem.at[sl, 0]),
                pltpu.make_async_copy(wu_hbm.at[e], wuf.at[sl], wsem.at[sl, 1]),
                pltpu.make_async_copy(wd_hbm.at[e], wdf.at[sl], wsem.at[sl, 2]))

    @pl.when(i == 0)
    def _():
        for cp in weight_copies(te_ref[0], wslot_ref[0]):
            cp.start()

    @pl.when(i < nu_ref[0])
    def _():
        changed = jnp.logical_or(i == 0, te_ref[i] != te_ref[jnp.maximum(i - 1, 0)])

        @pl.when(changed)
        def _():
            sl = wslot_ref[i]
            for cp in weight_copies(te_ref[i], sl):
                cp.wait()
            wgb[...] = wgf[sl].astype(BF16)
            wub[...] = wuf[sl].astype(BF16)
            wdb[...] = wdf[sl].astype(BF16)

            @pl.when(nxt_ref[i] >= 0)
            def _():
                for cp in weight_copies(nxt_ref[i], 1 - sl):
                    cp.start()

        tm = xs_ref.shape[0]
        n_sub = tm // MOE_SUBTILE
        filled = nsub_ref[i]
        for k in range(1, n_sub + 1):
            @pl.when(filled == k)
            def _(k=k):
                r = k * MOE_SUBTILE
                x = _unpack_bf16_pairs(xs_ref[:r, :])
                g = jnp.dot(x, wgb[...], preferred_element_type=F32)
                u = jnp.dot(x, wub[...], preferred_element_type=F32)
                hdn = (g * jax.nn.sigmoid(g) * u).astype(BF16)
                y = jnp.dot(hdn, wdb[...], preferred_element_type=F32)
                y_ref[:r, :] = _pack_bf16_pairs(y.astype(BF16).astype(F32))
                if r < tm:
                    y_ref[r:, :] = jnp.zeros((tm - r, y_ref.shape[1]), y_ref.dtype)

    @pl.when(i >= nu_ref[0])
    def _():
        y_ref[...] = jnp.zeros(y_ref.shape, y_ref.dtype)


def _moe(xs, tile_e, n_used, next_e, w_slot, tile_sub, w_g, w_u, w_d, tm):
    n_rows = xs.shape[0]
    d = w_g.shape[1]
    dp = xs.shape[1]
    n_tiles = n_rows // tm
    ff = w_g.shape[2]
    row_blk = lambda i, te, nu, *_: (jnp.minimum(i, nu[0] - 1), 0)
    hbm = pl.BlockSpec(memory_space=pl.ANY)
    grid_spec = pltpu.PrefetchScalarGridSpec(
        num_scalar_prefetch=5,
        grid=(n_tiles,),
        in_specs=[pl.BlockSpec((tm, dp), row_blk), hbm, hbm, hbm],
        out_specs=pl.BlockSpec((tm, dp), lambda i, *_: (i, 0)),
        scratch_shapes=[pltpu.VMEM((2, d, ff), F32), pltpu.VMEM((2, d, ff), F32), pltpu.VMEM((2, ff, d), F32),
                        pltpu.VMEM((d, ff), BF16), pltpu.VMEM((d, ff), BF16), pltpu.VMEM((ff, d), BF16),
                        pltpu.SemaphoreType.DMA((2, 3))],
    )
    return pl.pallas_call(
        _moe_kernel,
        grid_spec=grid_spec,
        out_shape=jax.ShapeDtypeStruct((n_rows, dp), jnp.uint32),
        compiler_params=_cparams(("arbitrary",)),
        name="moe",
    )(tile_e, n_used, next_e, w_slot, tile_sub, xs, w_g, w_u, w_d)


def _combine_kernel(n8_ref, loff_ref, gbase_ref, x1_ref, route_ref, nw_ref, y_hbm, o_ref,
                    ybuf, sem, *, lcap, n_tt):
    i = pl.program_id(0)
    slot = i % 2
    tm = x1_ref.shape[0]

    def fetch(tile, sl, wait):
        _segment_copies(n8_ref, gbase_ref, loff_ref, y_hbm, ybuf.at[sl], sem.at[sl], tile, wait)

    @pl.when(i == 0)
    def _():
        ybuf[...] = jnp.zeros(ybuf.shape, ybuf.dtype)
        fetch(0, 0, False)

    @pl.when(i + 1 < n_tt)
    def _():
        fetch(i + 1, 1 - slot, False)

    fetch(i, slot, True)
    g1 = route_ref[:, 0:1]
    g2 = route_ref[:, 1:2]
    lp1 = route_ref[:, 2:3]
    lp2 = route_ref[:, 3:4]
    cpos = lax.broadcasted_iota(jnp.int32, (tm, lcap), 1).astype(F32)
    gsel = (jnp.where(cpos == lp1, g1, 0.0) + jnp.where(cpos == lp2, g2, 0.0)).astype(BF16)
    moe = jnp.dot(gsel, _unpack_bf16_pairs(ybuf[slot]), preferred_element_type=F32)
    x = x1_ref[...] + moe
    o_ref[...] = x * lax.rsqrt(jnp.mean(x * x, axis=-1, keepdims=True) + NORM_EPS) * nw_ref[...]


def _combine(x1, y, route, norm_w, tables):
    t, d = x1.shape
    tm = min(ROW_TILE, t)
    n_tt = t // tm
    lcap = _local_rows(tm)
    n8, loff, gbase = tables[:3]
    grid_spec = pltpu.PrefetchScalarGridSpec(
        num_scalar_prefetch=3,
        grid=(n_tt,),
        in_specs=[pl.BlockSpec((tm, d), lambda i, *_: (i, 0)),
                  pl.BlockSpec((tm, LANES), lambda i, *_: (i, 0)),
                  pl.BlockSpec((1, d), lambda i, *_: (0, 0)),
                  pl.BlockSpec(memory_space=pl.ANY)],
        out_specs=pl.BlockSpec((tm, d), lambda i, *_: (i, 0)),
        scratch_shapes=[pltpu.VMEM((2, lcap, d // 2), jnp.uint32), pltpu.SemaphoreType.DMA((2,))],
    )
    return pl.pallas_call(
        functools.partial(_combine_kernel, lcap=lcap, n_tt=n_tt),
        grid_spec=grid_spec,
        out_shape=jax.ShapeDtypeStruct((t, d), F32),
        compiler_params=_cparams(("arbitrary",)),
        name="combine",
    )(n8, loff, gbase, x1, route, norm_w.reshape(1, d), y)


def _block_diag(w):
    n, bi, bj = w.shape
    eye = jnp.eye(n, dtype=w.dtype)
    return jnp.einsum('nij,nm->nimj', w, eye).reshape(n * bi, n * bj)


def kernel(x, mix_norm_w, w_in, lambda_q1, lambda_k1, lambda_q2, lambda_k2, head_norm_w, conv_w, conv_b, w_rgate, b_rgate, w_igate, b_igate, lru_lambda, w_out, ffn_norm_w, w_router_group, b_router_group, w_router_expert, b_router_expert, w_exp_gate, w_exp_up, w_exp_down, final_norm_w):
    b, s, d = x.shape
    t = b * s
    assert w_in.shape[0] == 1, "single-layer stack only"
    att_w = N_ATT_HEADS * HEAD_DIM
    tm_moe = MOE_TILE
    xf = x.reshape(t, d)
    for l in range(1):
        lambda_init = 0.8 - 0.6 * math.exp(-0.3 * l)
        assert s % ATT_TILE == 0, "sequence length must be a multiple of the attention tile"
        w_bd = jnp.concatenate([_block_diag(w_rgate[l]), _block_diag(w_igate[l])], axis=1).astype(BF16)
        b_cat = jnp.concatenate([b_rgate[l], b_igate[l]])
        qt, ka, vt, rnn = _in_proj(xf, mix_norm_w[l], w_in[l].astype(BF16), att_w, s,
                                   conv_w[l], conv_b[l], w_bd, b_cat, lru_lambda[l])
        lam_params = jnp.stack([lambda_q1[l], lambda_k1[l], lambda_q2[l], lambda_k2[l]]).astype(F32)
        att = _diff_attention(qt, ka, vt, lam_params, head_norm_w[l], lambda_init, b, s)
        w_route = jnp.concatenate([w_router_group[l], w_router_expert[l]], axis=1).T
        w_route = jnp.pad(w_route, ((0, LANES - w_route.shape[0]), (0, 0))).astype(BF16)
        b_route = jnp.concatenate([b_router_group[l], b_router_expert[l]])
        b_route = jnp.pad(b_route, (0, LANES - b_route.shape[0])).reshape(LANES, 1).astype(F32)
        x1, hn, route, route_t, n8_tiles = _out_proj(att.reshape(t, att_w), rnn.reshape(t, -1), xf,
                                                     w_out[l].astype(BF16), ffn_norm_w[l], w_route, b_route)
        n_tt = n8_tiles.shape[0]
        max_rows = TOP_K * t + n_tt * N_EXPERTS * (SUBLANES - 1) + N_EXPERTS * (tm_moe - 1)
        n_tiles = -(-max_rows // tm_moe)
        tables = _segment_tables(n8_tiles, tm_moe, n_tiles)
        xs = _dispatch(hn, route_t, tables, n_tiles * tm_moe, tm_moe)
        y = _moe(xs, tables[3], tables[4], tables[7], tables[8], tables[9],
                 w_exp_gate[l], w_exp_up[l], w_exp_down[l], tm_moe)
        out = _combine(x1, y, route, final_norm_w, tables)
    return out.reshape(b, s, d)
```

```python
import functools
import math

import numpy as np
import jax
import jax.numpy as jnp
from jax import lax
from jax.experimental import pallas as pl
from jax.experimental.pallas import tpu as pltpu

F32 = jnp.float32
BF16 = jnp.bfloat16

N_ATT_HEADS = 4
HEAD_DIM = 128
QK_DIM = 64
N_RNN_BLOCKS = 8
CONV_WIDTH = 4
LRU_C = 8.0
N_GROUPS = 4
EXPERTS_PER_GROUP = 8
N_EXPERTS = N_GROUPS * EXPERTS_PER_GROUP
TOP_K = 2
NORM_EPS = 1e-6
HEAD_NORM_EPS = 1e-5
LANES = 128
SUBLANES = 8
NEG_BIG = -1e30

ROW_TILE = 512
ROUTE_ROWS = 48
ATT_TILE = 512
ATT_HEADS_PER_STEP = 4
V_ROWS = HEAD_DIM + 16
LRU_CHUNK = 128
MOE_TILE = 512
MOE_SUBTILE = 128
VMEM_LIMIT = 48 * 1024 * 1024


def _cparams(sem, vmem=VMEM_LIMIT, **kw):
    return pltpu.CompilerParams(dimension_semantics=sem, vmem_limit_bytes=vmem, **kw)


def _inproj_kernel(slope_ref, x_ref, nw_ref, w_ref, cw_ref, cb_ref, wg_ref, bg_ref, lam_ref,
                   qt_ref, ka_ref, vt_ref, rnn_ref, xs, carry_h, a_s, u_s, *, att_w, s_len, ch):
    i = pl.program_id(0)
    x = x_ref[...]
    tm = x.shape[0]
    c_w = (w_ref.shape[1] - 3 * att_w) // 2

    @pl.when((i * tm) % s_len == 0)
    def _():
        xs[0:SUBLANES, :] = jnp.zeros((SUBLANES, c_w), F32)
        carry_h[...] = jnp.zeros(carry_h.shape, F32)

    ms = jnp.mean(x * x, axis=-1, keepdims=True)
    hn = (x * lax.rsqrt(ms + NORM_EPS) * nw_ref[...]).astype(BF16)
    p_lru = jnp.dot(hn, w_ref[:, 3 * att_w:], preferred_element_type=F32)

    xs[SUBLANES:, :] = p_lru[:, :c_w]
    neg_lam = -lam_ref[...]
    sp = jnp.maximum(neg_lam, 0.0) + jnp.log1p(jnp.exp(-jnp.abs(neg_lam)))
    cw = cw_ref[...]
    cb = cb_ref[...]
    bias = bg_ref[...]
    r8 = lax.broadcasted_iota(jnp.int32, (ch // SUBLANES, SUBLANES, c_w), 1)
    xcs, zs = [], []
    for c in range(tm // ch):
        r0 = c * ch
        win = xs[r0:r0 + ch + SUBLANES, :]
        xc = cw[3:4, :] * win[SUBLANES:] + cb
        for k in (1, 2, 3):
            xc = xc + cw[3 - k:4 - k, :] * pltpu.roll(win, k, axis=0)[SUBLANES:]
        xcs.append(xc)
        zs.append(jnp.dot(xc.astype(BF16), wg_ref[...], preferred_element_type=F32) + bias)
    xs[0:SUBLANES, :] = xs[tm:tm + SUBLANES, :]

    p_q = jnp.dot(hn, w_ref[:, :att_w], preferred_element_type=F32)
    p_v = jnp.dot(hn, w_ref[:, 2 * att_w:3 * att_w], preferred_element_type=F32)
    p_k = jnp.dot(hn, w_ref[:, att_w:2 * att_w], preferred_element_type=F32)

    for c in range(tm // ch):
        r0 = c * ch
        xc, z = xcs[c], zs[c]
        r = jax.nn.sigmoid(z[:, :c_w])
        ig = jax.nn.sigmoid(z[:, c_w:])
        log_a = (-LRU_C) * r * sp
        a = jnp.exp(log_a)
        w = jnp.tanh(-log_a) * (1.0 + a * a)
        u = jnp.where(w > 0.0, w * lax.rsqrt(w), 0.0) * ig * xc
        a = a.reshape(ch // SUBLANES, SUBLANES, c_w)
        u = u.reshape(ch // SUBLANES, SUBLANES, c_w)
        for k in (1, 2, 4):
            a_sh = pltpu.roll(a, k, axis=1)
            u_sh = pltpu.roll(u, k, axis=1)
            ok = r8 >= k
            u = jnp.where(ok, u + a * u_sh, u)
            a = jnp.where(ok, a * a_sh, a)
        a_s[r0:r0 + ch, :] = a.reshape(ch, c_w)
        u_s[r0:r0 + ch, :] = u.reshape(ch, c_w)

    qt = (p_q * (QK_DIM ** -0.5)).T
    ones2 = jnp.where(lax.broadcasted_iota(jnp.int32, (QK_DIM, tm), 0) < 2, 1.0, 0.0)
    pieces = []
    for g in range(2 * N_ATT_HEADS):
        pieces += [qt[g * QK_DIM:(g + 1) * QK_DIM], ones2]
    qt_ref[0] = jnp.concatenate(pieces, axis=0).astype(BF16)

    lane = lax.broadcasted_iota(jnp.int32, (tm, HEAD_DIM), 1)
    j = (i * tm) % s_len + lax.broadcasted_iota(jnp.int32, (tm, HEAD_DIM), 0)
    j_lo = (j & 255).astype(F32)
    j_hi = (j - (j & 255)).astype(F32)
    vtt = p_v.T
    ones_rows = jnp.where(lax.broadcasted_iota(jnp.int32, (V_ROWS - HEAD_DIM, tm), 0) == 0, 1.0, 0.0)
    for h in range(N_ATT_HEADS):
        slope = slope_ref[h]
        kk = p_k[:, h * HEAD_DIM:(h + 1) * HEAD_DIM]
        aug = jnp.where(lane == QK_DIM, slope * j_hi, jnp.where(lane == QK_DIM + 1, slope * j_lo, 0.0))
        ka_ref[:, 2 * h * HEAD_DIM:(2 * h + 1) * HEAD_DIM] = jnp.where(lane < QK_DIM, kk, aug).astype(BF16)
        ka_ref[:, (2 * h + 1) * HEAD_DIM:(2 * h + 2) * HEAD_DIM] = jnp.where(
            lane < QK_DIM, pltpu.roll(kk, QK_DIM, axis=1), aug).astype(BF16)
        vt_ref[0, h * V_ROWS:h * V_ROWS + HEAD_DIM, :] = vtt[h * HEAD_DIM:(h + 1) * HEAD_DIM].astype(BF16)
        vt_ref[0, h * V_ROWS + HEAD_DIM:(h + 1) * V_ROWS, :] = ones_rows.astype(BF16)

    def grp(g, hprev):
        r0 = pl.multiple_of(g * SUBLANES, SUBLANES)
        hg = u_s[pl.ds(r0, SUBLANES), :] + a_s[pl.ds(r0, SUBLANES), :] * hprev
        u_s[pl.ds(r0, SUBLANES), :] = hg
        return hg[SUBLANES - 1:SUBLANES, :]

    hlast = lax.fori_loop(0, tm // SUBLANES, grp, carry_h[0:1, :], unroll=8)
    carry_h[0:1, :] = hlast
    rnn_ref[...] = (u_s[...] * _gelu_tanh(p_lru[:, c_w:])).astype(rnn_ref.dtype)


def _gelu_tanh(x):
    return 0.5 * x * (1.0 + jnp.tanh(math.sqrt(2.0 / math.pi) * (x + 0.044715 * (x * x * x))))


def _alibi_slopes():
    nh = N_ATT_HEADS
    return jnp.asarray(np.array([2.0 ** (-8.0 * (i + 1) / nh) for i in range(nh)], dtype=np.float32))


def _in_proj(xf, norm_w, w_in_bf, att_w, s_len, conv_w, conv_b, w_gates_bf, b_gates, lru_lambda):
    t, d = xf.shape
    n = w_in_bf.shape[1]
    tm = min(ATT_TILE, t)
    ch = min(LRU_CHUNK, tm)
    nh = N_ATT_HEADS
    c_w = (n - 3 * att_w) // 2
    fix = lambda i: (0, 0)
    return pl.pallas_call(
        functools.partial(_inproj_kernel, att_w=att_w, s_len=s_len, ch=ch),
        grid=(t // tm,),
        in_specs=[pl.BlockSpec(memory_space=pltpu.SMEM),
                  pl.BlockSpec((tm, d), lambda i: (i, 0)),
                  pl.BlockSpec((1, d), fix),
                  pl.BlockSpec((d, n), fix),
                  pl.BlockSpec((CONV_WIDTH, c_w), fix),
                  pl.BlockSpec((1, c_w), fix),
                  pl.BlockSpec((c_w, 2 * c_w), fix),
                  pl.BlockSpec((1, 2 * c_w), fix),
                  pl.BlockSpec((1, c_w), fix)],
        out_specs=[pl.BlockSpec((1, 2 * att_w, tm), lambda i: (i, 0, 0)),
                   pl.BlockSpec((tm, 2 * att_w), lambda i: (i, 0)),
                   pl.BlockSpec((1, nh * V_ROWS, tm), lambda i: (i, 0, 0)),
                   pl.BlockSpec((tm, c_w), lambda i: (i, 0))],
        out_shape=[jax.ShapeDtypeStruct((t // tm, 2 * att_w, tm), BF16),
                   jax.ShapeDtypeStruct((t, 2 * att_w), BF16),
                   jax.ShapeDtypeStruct((t // tm, nh * V_ROWS, tm), BF16),
                   jax.ShapeDtypeStruct((t, c_w), BF16)],
        scratch_shapes=[pltpu.VMEM((tm + SUBLANES, c_w), F32), pltpu.VMEM((SUBLANES, c_w), F32),
                        pltpu.VMEM((tm, c_w), F32), pltpu.VMEM((tm, c_w), F32)],
        compiler_params=_cparams(("arbitrary",)),
        name="in_proj",
    )(_alibi_slopes(), xf, norm_w.reshape(1, d), w_in_bf, conv_w, conv_b.reshape(1, c_w), w_gates_bf,
      b_gates.reshape(1, 2 * c_w), lru_lambda.reshape(1, c_w))


def _attn_kernel(lam_ref, hw_ref, q_ref, k_ref, vt, o_ref, sb, mx, acc, *, tq, n_heads, lambda_init):
    qi = pl.program_id(2)
    n_maps = 2 * n_heads
    mx[...] = jnp.full(mx.shape, NEG_BIG, F32)
    acc[...] = jnp.zeros(acc.shape, F32)

    def values(c, n, lanes=slice(None)):
        return vt[c, (n // 2) * V_ROWS:(n // 2 + 1) * V_ROWS, lanes]

    def scores(c, slot):
        rows = pl.ds(pl.multiple_of(c * tq, tq), tq)
        for n in range(n_maps):
            sb[n, slot] = jnp.dot(k_ref[0, rows, n * HEAD_DIM:(n + 1) * HEAD_DIM],
                                  q_ref[0, n * HEAD_DIM:(n + 1) * HEAD_DIM, :],
                                  preferred_element_type=F32)

    def softmax_pv(c, slot):
        for n in range(n_maps):
            s = sb[n, slot]
            m_prev = mx[n]
            m_new = jnp.maximum(m_prev, jnp.max(s, axis=0, keepdims=True))
            p = jnp.exp(s - m_new).astype(BF16)
            acc[n] = jnp.exp(m_prev - m_new) * acc[n] + jnp.dot(values(c, n), p, preferred_element_type=F32)
            mx[n] = m_new

    def softmax_pv_diagonal(c, slot):
        hq = tq // 2
        keep_t = (lax.broadcasted_iota(jnp.int32, (hq, tq), 0) <= lax.broadcasted_iota(jnp.int32, (hq, tq), 1))
        keep_b = (lax.broadcasted_iota(jnp.int32, (hq, hq), 0) <= lax.broadcasted_iota(jnp.int32, (hq, hq), 1))
        for n in range(n_maps):
            top = jnp.where(keep_t, sb[n, slot, :hq, :], NEG_BIG)
            bot = jnp.where(keep_b, sb[n, slot, hq:, hq:], NEG_BIG)
            mt = jnp.max(top, axis=0, keepdims=True)
            mb = jnp.max(bot, axis=0, keepdims=True)
            m_prev = mx[n]
            m_new = jnp.maximum(m_prev, jnp.concatenate([mt[:, :hq], jnp.maximum(mt[:, hq:], mb)], axis=1))
            p_top = jnp.exp(top - m_new).astype(BF16)
            p_bot = jnp.exp(bot - m_new[:, hq:]).astype(BF16)
            acc[n] = (jnp.exp(m_prev - m_new) * acc[n]
                      + jnp.dot(values(c, n, slice(0, hq)), p_top, preferred_element_type=F32))
            acc[n, :, hq:] += jnp.dot(values(c, n, slice(hq, tq)), p_bot, preferred_element_type=F32)
            mx[n] = m_new

    scores(0, 0)

    def body(j, c):
        scores(2 * j + 1, 1)
        softmax_pv(2 * j, 0)
        scores(2 * j + 2, 0)
        softmax_pv(2 * j + 1, 1)
        return c

    lax.fori_loop(0, qi // 2, body, 0)

    @pl.when(qi % 2 == 0)
    def _():
        softmax_pv_diagonal(qi, 0)

    @pl.when(qi % 2 == 1)
    def _():
        scores(qi, 1)
        softmax_pv(qi - 1, 0)
        softmax_pv_diagonal(qi, 1)

    lam = (jnp.exp(jnp.sum(lam_ref[0:1, :] * lam_ref[1:2, :], axis=1, keepdims=True))
           - jnp.exp(jnp.sum(lam_ref[2:3, :] * lam_ref[3:4, :], axis=1, keepdims=True))
           + lambda_init)
    for hh in range(n_heads):
        o1 = acc[2 * hh, :HEAD_DIM, :] * (1.0 / acc[2 * hh, HEAD_DIM:HEAD_DIM + 1, :])
        o2 = acc[2 * hh + 1, :HEAD_DIM, :] * (1.0 / acc[2 * hh + 1, HEAD_DIM:HEAD_DIM + 1, :])
        o = o1 - lam * o2
        o = o * lax.rsqrt(jnp.mean(o * o, axis=0, keepdims=True) + HEAD_NORM_EPS)
        o_ref[0, :, hh * HEAD_DIM:(hh + 1) * HEAD_DIM] = (
            o.T * hw_ref[...] * (1.0 - lambda_init)).astype(o_ref.dtype)


def _diff_attention(qt, ka, vt, lam_params, head_norm_w, lambda_init, b, s):
    nh = N_ATT_HEADS
    hp = ATT_HEADS_PER_STEP
    tq = qt.shape[2]
    nq = s // tq
    return pl.pallas_call(
        functools.partial(_attn_kernel, tq=tq, n_heads=hp, lambda_init=lambda_init),
        grid=(b, nh // hp, nq),
        in_specs=[pl.BlockSpec((4, QK_DIM), lambda bi, hi, qi: (0, 0)),
                  pl.BlockSpec((1, HEAD_DIM), lambda bi, hi, qi: (0, 0)),
                  pl.BlockSpec((1, hp * 2 * HEAD_DIM, tq), lambda bi, hi, qi: (bi * nq + qi, hi, 0)),
                  pl.BlockSpec((1, s, hp * 2 * HEAD_DIM), lambda bi, hi, qi: (bi, 0, hi)),
                  pl.BlockSpec((nq, hp * V_ROWS, tq), lambda bi, hi, qi: (bi, hi, 0))],
        out_specs=pl.BlockSpec((1, tq, hp * HEAD_DIM), lambda bi, hi, qi: (bi, qi, hi)),
        out_shape=jax.ShapeDtypeStruct((b, s, nh * HEAD_DIM), BF16),
        scratch_shapes=[pltpu.VMEM((2 * hp, 2, tq, tq), F32), pltpu.VMEM((2 * hp, 1, tq), F32),
                        pltpu.VMEM((2 * hp, V_ROWS, tq), F32)],
        compiler_params=_cparams(("parallel", "parallel", "arbitrary"), vmem=56 * 1024 * 1024),
        name="diff_attn",
    )(lam_params, head_norm_w.reshape(1, HEAD_DIM), qt, ka.reshape(b, s, ka.shape[1]), vt)


def _outproj_kernel(att_ref, rnn_ref, x_ref, wo_ref, nw_ref, wrt_ref, brc_ref,
                    x1_ref, hn_ref, route_ref, route_t_ref, n8_ref, *, att_w):
    y = jnp.dot(att_ref[...], wo_ref[:att_w, :], preferred_element_type=F32)
    y = y + jnp.dot(rnn_ref[...], wo_ref[att_w:, :], preferred_element_type=F32)
    x1 = x_ref[...] + y
    x1_ref[...] = x1
    hn = (x1 * lax.rsqrt(jnp.mean(x1 * x1, axis=-1, keepdims=True) + NORM_EPS) * nw_ref[...]).astype(BF16)
    hn_ref[...] = hn
    tm = hn.shape[0]

    lg = lax.dot_general(wrt_ref[...], hn, (((1,), (1,)), ((), ())), preferred_element_type=F32)
    lg = lg[:ROUTE_ROWS] + brc_ref[:ROUTE_ROWS, 0:1]
    rowf = lax.broadcasted_iota(jnp.int32, lg.shape, 0).astype(F32)
    big = float(LANES)
    ninf = -jnp.inf
    is_g = rowf < N_GROUPS
    lgm = jnp.where(is_g, lg, ninf)
    mg = jnp.max(lgm, axis=0, keepdims=True)
    g_sel = jnp.min(jnp.where(lgm == mg, rowf, big), axis=0, keepdims=True)
    pg = 1.0 / jnp.sum(jnp.where(is_g, jnp.exp(lgm - mg), 0.0), axis=0, keepdims=True)
    lo = N_GROUPS + EXPERTS_PER_GROUP * g_sel
    in_grp = (rowf >= lo) & (rowf < lo + EXPERTS_PER_GROUP)
    lem = jnp.where(in_grp, lg, ninf)
    v1 = jnp.max(lem, axis=0, keepdims=True)
    i1 = jnp.min(jnp.where(lem == v1, rowf, big), axis=0, keepdims=True)
    lem2 = jnp.where(rowf == i1, ninf, lem)
    v2 = jnp.max(lem2, axis=0, keepdims=True)
    i2 = jnp.min(jnp.where(lem2 == v2, rowf, big), axis=0, keepdims=True)
    e2 = jnp.exp(v2 - v1)
    den = 1.0 + e2
    g1 = pg / den
    g2 = pg * e2 / den

    oh1 = jnp.where(rowf == i1, 1.0, 0.0)
    oh2 = jnp.where(rowf == i2, 1.0, 0.0)
    oh = oh1 + oh2
    earlier = (lax.broadcasted_iota(jnp.int32, (tm, tm), 0)
               < lax.broadcasted_iota(jnp.int32, (tm, tm), 1)).astype(BF16)
    pref = jnp.dot(oh.astype(BF16), earlier, preferred_element_type=F32)
    cnt = jnp.sum(oh, axis=1, keepdims=True)
    n8 = jnp.floor((cnt + (SUBLANES - 1)) * (1.0 / SUBLANES))
    n8_b = jnp.broadcast_to(n8, (ROUTE_ROWS, LANES))
    before = (lax.broadcasted_iota(jnp.int32, (ROUTE_ROWS, ROUTE_ROWS), 1)
              < lax.broadcasted_iota(jnp.int32, (ROUTE_ROWS, ROUTE_ROWS), 0)).astype(BF16)
    loff8 = jnp.dot(before, n8_b.astype(BF16), preferred_element_type=F32)[:, 0:1]
    pos = SUBLANES * loff8 + pref
    lp1 = jnp.sum(oh1 * pos, axis=0, keepdims=True)
    lp2 = jnp.sum(oh2 * pos, axis=0, keepdims=True)
    route_t = jnp.concatenate([g1, g2, lp1, lp2, jnp.zeros((LANES - 4, tm), F32)], axis=0)
    route_t_ref[0] = route_t[:SUBLANES]
    route_ref[...] = route_t.T
    n8_ref[0] = n8_b


def _out_proj(att, rnn, xf, w_out_bf, norm_w, w_route_t_bf, b_route_col):
    t, d = xf.shape
    att_w = att.shape[1]
    tm = min(ROW_TILE, t)
    row = lambda i: (i, 0)
    fix = lambda i: (0, 0)
    return pl.pallas_call(
        functools.partial(_outproj_kernel, att_w=att_w),
        grid=(t // tm,),
        in_specs=[pl.BlockSpec((tm, att_w), row), pl.BlockSpec((tm, rnn.shape[1]), row),
                  pl.BlockSpec((tm, d), row), pl.BlockSpec(w_out_bf.shape, fix),
                  pl.BlockSpec((1, d), fix), pl.BlockSpec((LANES, d), fix), pl.BlockSpec((LANES, 1), fix)],
        out_specs=[pl.BlockSpec((tm, d), row), pl.BlockSpec((tm, d), row), pl.BlockSpec((tm, LANES), row),
                   pl.BlockSpec((1, SUBLANES, tm), lambda i: (i, 0, 0)),
                   pl.BlockSpec((1, ROUTE_ROWS, LANES), lambda i: (i, 0, 0))],
        out_shape=[jax.ShapeDtypeStruct((t, d), F32), jax.ShapeDtypeStruct((t, d), BF16),
                   jax.ShapeDtypeStruct((t, LANES), F32),
                   jax.ShapeDtypeStruct((t // tm, SUBLANES, tm), F32),
                   jax.ShapeDtypeStruct((t // tm, ROUTE_ROWS, LANES), F32)],
        compiler_params=_cparams(("parallel",)),
        name="out_proj",
    )(att, rnn, xf, w_out_bf, norm_w.reshape(1, d), w_route_t_bf, b_route_col)


def _local_rows(tm):
    return -(-(TOP_K * tm + N_EXPERTS * (SUBLANES - 1)) // LANES) * LANES


def _segment_tables(n8_tiles, tm_moe, n_tiles):
    n8 = n8_tiles[:, N_GROUPS:N_GROUPS + N_EXPERTS, 0].astype(jnp.int32)
    c8 = n8 * SUBLANES
    loff = jnp.cumsum(c8, axis=1) - c8
    gtot = jnp.sum(c8, axis=0)
    gpad = (gtot + tm_moe - 1) // tm_moe * tm_moe
    gend = jnp.cumsum(gpad)
    gstart = gend - gpad
    gbase = gstart[None, :] + jnp.cumsum(c8, axis=0) - c8
    tile_row0 = jnp.arange(n_tiles, dtype=jnp.int32) * tm_moe
    tile_e = jnp.minimum(jnp.sum((gend[None, :] <= tile_row0[:, None]).astype(jnp.int32), axis=1),
                         N_EXPERTS - 1).astype(jnp.int32)
    n_used = (gend[-1] // tm_moe).astype(jnp.int32).reshape(1)
    tail_start = (gstart + gtot).astype(jnp.int32)
    tail_n8 = ((gpad - gtot) // SUBLANES).astype(jnp.int32)
    after = gend[tile_e] // tm_moe
    next_e = jnp.where(after < n_used[0], tile_e[jnp.minimum(after, n_tiles - 1)], -1).astype(jnp.int32)
    first = jnp.concatenate([jnp.ones((1,), jnp.int32), (tile_e[1:] != tile_e[:-1]).astype(jnp.int32)])
    w_slot = ((jnp.cumsum(first) - 1) % 2).astype(jnp.int32)
    rows_in_tile = jnp.clip((gstart + gtot)[tile_e] - tile_row0, 1, tm_moe)
    tile_sub = ((rows_in_tile + MOE_SUBTILE - 1) // MOE_SUBTILE).astype(jnp.int32)
    return (n8.reshape(-1), loff.reshape(-1).astype(jnp.int32), gbase.reshape(-1).astype(jnp.int32),
            tile_e, n_used, tail_start, tail_n8, next_e, w_slot, tile_sub)


def _segment_copies(n8_ref, src_off_ref, dst_off_ref, src, dst, sem, tile, wait):
    def rows_of(e):
        return pl.multiple_of(n8_ref[tile * N_EXPERTS + e] * SUBLANES, SUBLANES)

    if wait:
        total = lax.fori_loop(0, N_EXPERTS, lambda e, acc: acc + rows_of(e), 0)
        total = pl.multiple_of(total, SUBLANES)
        pltpu.make_async_copy(src.at[pl.ds(0, total), :], dst.at[pl.ds(0, total), :], sem).wait()
        return

    def per_expert(e, c):
        k = tile * N_EXPERTS + e
        rows = rows_of(e)

        @pl.when(rows > 0)
        def _():
            pltpu.make_async_copy(
                src.at[pl.ds(pl.multiple_of(src_off_ref[k], SUBLANES), rows), :],
                dst.at[pl.ds(pl.multiple_of(dst_off_ref[k], SUBLANES), rows), :], sem).start()
        return c
    lax.fori_loop(0, N_EXPERTS, per_expert, 0)


def _pack_bf16_pairs(x):
    n = x.shape[1] // 2
    bits = lax.bitcast_convert_type(x, jnp.uint32)
    return (bits[:, :n] >> 16) | (bits[:, n:] & jnp.uint32(0xFFFF0000))


def _unpack_bf16_pairs(p):
    lo = lax.bitcast_convert_type(p << 16, F32)
    hi = lax.bitcast_convert_type(p & jnp.uint32(0xFFFF0000), F32)
    return jnp.concatenate([lo, hi], axis=1).astype(BF16)


def _dispatch_kernel(n8_ref, loff_ref, gbase_ref, tstart_ref, tn8_ref, nu_ref, hn_ref, route_ref, xs_hbm,
                     stage, zbuf, sem, zsem, *, lcap, n_tt):
    i = pl.program_id(0)
    slot = i % 2
    tm = hn_ref.shape[0]
    tm_moe = zbuf.shape[0]
    n_tiles = xs_hbm.shape[0] // tm_moe

    def tail_copies(wait):
        def go(cp):
            if wait:
                cp.wait()
            else:
                cp.start()

        def per_expert(e, c):
            rows = pl.multiple_of(tn8_ref[e] * SUBLANES, SUBLANES)

            @pl.when(rows > 0)
            def _():
                go(pltpu.make_async_copy(
                    zbuf.at[pl.ds(0, rows), :],
                    xs_hbm.at[pl.ds(pl.multiple_of(tstart_ref[e], SUBLANES), rows), :], zsem.at[0]))
            return c
        lax.fori_loop(0, N_EXPERTS, per_expert, 0)

        def per_unused_tile(j, c):
            go(pltpu.make_async_copy(zbuf, xs_hbm.at[pl.ds(pl.multiple_of(j * tm_moe, tm_moe), tm_moe), :],
                                     zsem.at[0]))
            return c
        lax.fori_loop(nu_ref[0], n_tiles, per_unused_tile, 0)

    @pl.when(i == 0)
    def _():
        zbuf[...] = jnp.zeros(zbuf.shape, zbuf.dtype)
        tail_copies(False)

    @pl.when(i >= 2)
    def _():
        _segment_copies(n8_ref, loff_ref, gbase_ref, stage.at[slot], xs_hbm, sem.at[slot], i - 2, True)

    lp1 = route_ref[0, 2:3, :]
    lp2 = route_ref[0, 3:4, :]
    rpos = lax.broadcasted_iota(jnp.int32, (lcap, tm), 0).astype(F32)
    sel = jnp.where((rpos == lp1) | (rpos == lp2), 1.0, 0.0).astype(BF16)
    stage[slot] = _pack_bf16_pairs(jnp.dot(sel, hn_ref[...], preferred_element_type=F32))
    _segment_copies(n8_ref, loff_ref, gbase_ref, stage.at[slot], xs_hbm, sem.at[slot], i, False)

    @pl.when(i == n_tt - 1)
    def _():
        _segment_copies(n8_ref, loff_ref, gbase_ref, stage.at[slot], xs_hbm, sem.at[slot], i, True)
        if n_tt > 1:
            _segment_copies(n8_ref, loff_ref, gbase_ref, stage.at[1 - slot], xs_hbm, sem.at[1 - slot],
                            i - 1, True)
        tail_copies(True)


def _dispatch(hn, route, tables, n_rows, tm_moe):
    t, d = hn.shape
    tm = min(ROW_TILE, t)
    n_tt = t // tm
    lcap = _local_rows(tm)
    n8, loff, gbase, _, n_used, tail_start, tail_n8 = tables[:7]
    grid_spec = pltpu.PrefetchScalarGridSpec(
        num_scalar_prefetch=6,
        grid=(n_tt,),
        in_specs=[pl.BlockSpec((tm, d), lambda i, *_: (i, 0)),
                  pl.BlockSpec((1, SUBLANES, tm), lambda i, *_: (i, 0, 0))],
        out_specs=pl.BlockSpec(memory_space=pl.ANY),
        scratch_shapes=[pltpu.VMEM((2, lcap, d // 2), jnp.uint32), pltpu.VMEM((tm_moe, d // 2), jnp.uint32),
                        pltpu.SemaphoreType.DMA((2,)), pltpu.SemaphoreType.DMA((1,))],
    )
    return pl.pallas_call(
        functools.partial(_dispatch_kernel, lcap=lcap, n_tt=n_tt),
        grid_spec=grid_spec,
        out_shape=jax.ShapeDtypeStruct((n_rows, d // 2), jnp.uint32),
        compiler_params=_cparams(("arbitrary",), has_side_effects=True),
        name="dispatch",
    )(n8, loff, gbase, tail_start, tail_n8, n_used, hn, route)


def _moe_kernel(te_ref, nu_ref, nxt_ref, wslot_ref, nsub_ref, xs_ref, wg_hbm, wu_hbm, wd_hbm, y_ref,
                wgf, wuf, wdf, wgb, wub, wdb, wsem):
    i = pl.program_id(0)

    def weight_copies(e, sl):
        return (pltpu.make_async_copy(wg_hbm.at[e], wgf.at[sl], wsem.at[sl, 0]),
                pltpu.make_async_copy(wu_hbm.at[e], wuf.at[sl], wsem.at[sl, 1]),
                pltpu.make_async_copy(wd_hbm.at[e], wdf.at[sl], wsem.at[sl, 2]))

    @pl.when(i == 0)
    def _():
        for cp in weight_copies(te_ref[0], wslot_ref[0]):
            cp.start()

    @pl.when(i < nu_ref[0])
    def _():
        changed = jnp.logical_or(i == 0, te_ref[i] != te_ref[jnp.maximum(i - 1, 0)])

        @pl.when(changed)
        def _():
            sl = wslot_ref[i]
            for cp in weight_copies(te_ref[i], sl):
                cp.wait()
            wgb[...] = wgf[sl].astype(BF16)
            wub[...] = wuf[sl].astype(BF16)
            wdb[...] = wdf[sl].astype(BF16)

            @pl.when(nxt_ref[i] >= 0)
            def _():
                for cp in weight_copies(nxt_ref[i], 1 - sl):
                    cp.start()

        tm = xs_ref.shape[0]
        n_sub = tm // MOE_SUBTILE
        filled = nsub_ref[i]
        for k in range(1, n_sub + 1):
            @pl.when(filled == k)
            def _(k=k):
                r = k * MOE_SUBTILE
                x = _unpack_bf16_pairs(xs_ref[:r, :])
                g = jnp.dot(x, wgb[...], preferred_element_type=F32)
                u = jnp.dot(x, wub[...], preferred_element_type=F32)
                hdn = (g * jax.nn.sigmoid(g) * u).astype(BF16)
                y = jnp.dot(hdn, wdb[...], preferred_element_type=F32)
                y_ref[:r, :] = _pack_bf16_pairs(y.astype(BF16).astype(F32))
                if r < tm:
                    y_ref[r:, :] = jnp.zeros((tm - r, y_ref.shape[1]), y_ref.dtype)

    @pl.when(i >= nu_ref[0])
    def _():
        y_ref[...] = jnp.zeros(y_ref.shape, y_ref.dtype)


def _moe(xs, tile_e, n_used, next_e, w_slot, tile_sub, w_g, w_u, w_d, tm):
    n_rows = xs.shape[0]
    d = w_g.shape[1]
    dp = xs.shape[1]
    n_tiles = n_rows // tm
    ff = w_g.shape[2]
    row_blk = lambda i, te, nu, *_: (jnp.minimum(i, nu[0] - 1), 0)
    hbm = pl.BlockSpec(memory_space=pl.ANY)
    grid_spec = pltpu.PrefetchScalarGridSpec(
        num_scalar_prefetch=5,
        grid=(n_tiles,),
        in_specs=[pl.BlockSpec((tm, dp), row_blk), hbm, hbm, hbm],
        out_specs=pl.BlockSpec((tm, dp), lambda i, *_: (i, 0)),
        scratch_shapes=[pltpu.VMEM((2, d, ff), F32), pltpu.VMEM((2, d, ff), F32), pltpu.VMEM((2, ff, d), F32),
                        pltpu.VMEM((d, ff), BF16), pltpu.VMEM((d, ff), BF16), pltpu.VMEM((ff, d), BF16),
                        pltpu.SemaphoreType.DMA((2, 3))],
    )
    return pl.pallas_call(
        _moe_kernel,
        grid_spec=grid_spec,
        out_shape=jax.ShapeDtypeStruct((n_rows, dp), jnp.uint32),
        compiler_params=_cparams(("arbitrary",)),
        name="moe",
    )(tile_e, n_used, next_e, w_slot, tile_sub, xs, w_g, w_u, w_d)


def _combine_kernel(n8_ref, loff_ref, gbase_ref, x1_ref, route_ref, nw_ref, y_hbm, o_ref,
                    ybuf, sem, *, lcap, n_tt):
    i = pl.program_id(0)
    slot = i % 2
    tm = x1_ref.shape[0]

    def fetch(tile, sl, wait):
        _segment_copies(n8_ref, gbase_ref, loff_ref, y_hbm, ybuf.at[sl], sem.at[sl], tile, wait)

    @pl.when(i == 0)
    def _():
        ybuf[...] = jnp.zeros(ybuf.shape, ybuf.dtype)
        fetch(0, 0, False)

    @pl.when(i + 1 < n_tt)
    def _():
        fetch(i + 1, 1 - slot, False)

    fetch(i, slot, True)
    g1 = route_ref[:, 0:1]
    g2 = route_ref[:, 1:2]
    lp1 = route_ref[:, 2:3]
    lp2 = route_ref[:, 3:4]
    cpos = lax.broadcasted_iota(jnp.int32, (tm, lcap), 1).astype(F32)
    gsel = (jnp.where(cpos == lp1, g1, 0.0) + jnp.where(cpos == lp2, g2, 0.0)).astype(BF16)
    moe = jnp.dot(gsel, _unpack_bf16_pairs(ybuf[slot]), preferred_element_type=F32)
    x = x1_ref[...] + moe
    o_ref[...] = x * lax.rsqrt(jnp.mean(x * x, axis=-1, keepdims=True) + NORM_EPS) * nw_ref[...]


def _combine(x1, y, route, norm_w, tables):
    t, d = x1.shape
    tm = min(ROW_TILE, t)
    n_tt = t // tm
    lcap = _local_rows(tm)
    n8, loff, gbase = tables[:3]
    grid_spec = pltpu.PrefetchScalarGridSpec(
        num_scalar_prefetch=3,
        grid=(n_tt,),
        in_specs=[pl.BlockSpec((tm, d), lambda i, *_: (i, 0)),
                  pl.BlockSpec((tm, LANES), lambda i, *_: (i, 0)),
                  pl.BlockSpec((1, d), lambda i, *_: (0, 0)),
                  pl.BlockSpec(memory_space=pl.ANY)],
        out_specs=pl.BlockSpec((tm, d), lambda i, *_: (i, 0)),
        scratch_shapes=[pltpu.VMEM((2, lcap, d // 2), jnp.uint32), pltpu.SemaphoreType.DMA((2,))],
    )
    return pl.pallas_call(
        functools.partial(_combine_kernel, lcap=lcap, n_tt=n_tt),
        grid_spec=grid_spec,
        out_shape=jax.ShapeDtypeStruct((t, d), F32),
        compiler_params=_cparams(("arbitrary",)),
        name="combine",
    )(n8, loff, gbase, x1, route, norm_w.reshape(1, d), y)


def _block_diag(w):
    n, bi, bj = w.shape
    eye = jnp.eye(n, dtype=w.dtype)
    return jnp.einsum('nij,nm->nimj', w, eye).reshape(n * bi, n * bj)


def kernel(x, mix_norm_w, w_in, lambda_q1, lambda_k1, lambda_q2, lambda_k2, head_norm_w, conv_w, conv_b, w_rgate, b_rgate, w_igate, b_igate, lru_lambda, w_out, ffn_norm_w, w_router_group, b_router_group, w_router_expert, b_router_expert, w_exp_gate, w_exp_up, w_exp_down, final_norm_w):
    b, s, d = x.shape
    t = b * s
    assert w_in.shape[0] == 1, "single-layer stack only"
    att_w = N_ATT_HEADS * HEAD_DIM
    tm_moe = MOE_TILE
    xf = x.reshape(t, d)
    for l in range(1):
        lambda_init = 0.8 - 0.6 * math.exp(-0.3 * l)
        assert s % ATT_TILE == 0, "sequence length must be a multiple of the attention tile"
        w_bd = jnp.concatenate([_block_diag(w_rgate[l]), _block_diag(w_igate[l])], axis=1).astype(BF16)
        b_cat = jnp.concatenate([b_rgate[l], b_igate[l]])
        qt, ka, vt, rnn = _in_proj(xf, mix_norm_w[l], w_in[l].astype(BF16), att_w, s,
                                   conv_w[l], conv_b[l], w_bd, b_cat, lru_lambda[l])
        lam_params = jnp.stack([lambda_q1[l], lambda_k1[l], lambda_q2[l], lambda_k2[l]]).astype(F32)
        att = _diff_attention(qt, ka, vt, lam_params, head_norm_w[l], lambda_init, b, s)
        w_route = jnp.concatenate([w_router_group[l], w_router_expert[l]], axis=1).T
        w_route = jnp.pad(w_route, ((0, LANES - w_route.shape[0]), (0, 0))).astype(BF16)
        b_route = jnp.concatenate([b_router_group[l], b_router_expert[l]])
        b_route = jnp.pad(b_route, (0, LANES - b_route.shape[0])).reshape(LANES, 1).astype(F32)
        x1, hn, route, route_t, n8_tiles = _out_proj(att.reshape(t, att_w), rnn.reshape(t, -1), xf,
                                                     w_out[l].astype(BF16), ffn_norm_w[l], w_route, b_route)
        n_tt = n8_tiles.shape[0]
        max_rows = TOP_K * t + n_tt * N_EXPERTS * (SUBLANES - 1) + N_EXPERTS * (tm_moe - 1)
        n_tiles = -(-max_rows // tm_moe)
        tables = _segment_tables(n8_tiles, tm_moe, n_tiles)
        xs = _dispatch(hn, route_t, tables, n_tiles * tm_moe, tm_moe)
        y = _moe(xs, tables[3], tables[4], tables[7], tables[8], tables[9],
                 w_exp_gate[l], w_exp_up[l], w_exp_down[l], tm_moe)
        out = _combine(x1, y, route, final_norm_w, tables)
    return out.reshape(b, s, d)
```

```python
import functools
import math

import numpy as np
import jax
import jax.numpy as jnp
from jax import lax
from jax.experimental import pallas as pl
from jax.experimental.pallas import tpu as pltpu

F32 = jnp.float32
BF16 = jnp.bfloat16

N_ATT_HEADS = 4
HEAD_DIM = 128
QK_DIM = 64
N_RNN_BLOCKS = 8
CONV_WIDTH = 4
LRU_C = 8.0
N_GROUPS = 4
EXPERTS_PER_GROUP = 8
N_EXPERTS = N_GROUPS * EXPERTS_PER_GROUP
TOP_K = 2
NORM_EPS = 1e-6
HEAD_NORM_EPS = 1e-5
LOG2E = math.log2(math.e)


def _bf16_terms(x, n):
    terms = []
    for _ in range(n):
        t = float(np.float32(x).astype(jnp.bfloat16))
        terms.append(t)
        x -= t
    return tuple(terms)


LOG2E_TERMS = _bf16_terms(LOG2E, 3)
LANES = 128
SUBLANES = 8
NEG_BIG = -1e30

ROW_TILE = 512
ROUTE_ROWS = 48
ATT_TILE = 512
ATT_HEADS_PER_STEP = 4
V_ROWS = HEAD_DIM + 16
LRU_CHUNK = 128
MOE_TILE = 512
MOE_SUBTILE = 128
VMEM_LIMIT = 48 * 1024 * 1024


def _cparams(sem, vmem=VMEM_LIMIT, **kw):
    return pltpu.CompilerParams(dimension_semantics=sem, vmem_limit_bytes=vmem, **kw)


def _inproj_kernel(slope_ref, x_ref, nw_ref, w_ref, cw_ref, cb_ref, wg_ref, bg_ref, lam_ref,
                   qt_ref, ka_ref, vt_ref, rnn_ref, xs, carry_h, a_s, u_s, *, att_w, s_len, ch):
    i = pl.program_id(0)
    x = x_ref[...]
    tm = x.shape[0]
    c_w = (w_ref.shape[1] - 3 * att_w) // 2

    @pl.when((i * tm) % s_len == 0)
    def _():
        xs[0:SUBLANES, :] = jnp.zeros((SUBLANES, c_w), F32)
        carry_h[...] = jnp.zeros(carry_h.shape, F32)

    ms = jnp.mean(x * x, axis=-1, keepdims=True)
    hn = (x * lax.rsqrt(ms + NORM_EPS) * nw_ref[...]).astype(BF16)
    p_lru = jnp.dot(hn, w_ref[:, 3 * att_w:], preferred_element_type=F32)

    xs[SUBLANES:, :] = p_lru[:, :c_w]
    neg_lam = -lam_ref[...]
    sp = jnp.maximum(neg_lam, 0.0) + jnp.log1p(jnp.exp(-jnp.abs(neg_lam)))
    cw = cw_ref[...]
    cb = cb_ref[...]
    bias = bg_ref[...]
    r8 = lax.broadcasted_iota(jnp.int32, (ch // SUBLANES, SUBLANES, c_w), 1)
    xcs, zs = [], []
    for c in range(tm // ch):
        r0 = c * ch
        win = xs[r0:r0 + ch + SUBLANES, :]
        xc = cw[3:4, :] * win[SUBLANES:] + cb
        for k in (1, 2, 3):
            xc = xc + cw[3 - k:4 - k, :] * pltpu.roll(win, k, axis=0)[SUBLANES:]
        xcs.append(xc)
        zs.append(jnp.dot(xc.astype(BF16), wg_ref[...], preferred_element_type=F32) + bias)
    xs[0:SUBLANES, :] = xs[tm:tm + SUBLANES, :]

    p_q = jnp.dot(hn, w_ref[:, :att_w], preferred_element_type=F32)
    p_v = jnp.dot(hn, w_ref[:, 2 * att_w:3 * att_w], preferred_element_type=F32)
    p_k = jnp.dot(hn, w_ref[:, att_w:2 * att_w], preferred_element_type=F32)

    for c in range(tm // ch):
        r0 = c * ch
        xc, z = xcs[c], zs[c]
        r = jax.nn.sigmoid(z[:, :c_w])
        ig = jax.nn.sigmoid(z[:, c_w:])
        log_a = (-LRU_C) * r * sp
        a = jnp.exp(log_a)
        w = jnp.tanh(-log_a) * (1.0 + a * a)
        u = jnp.where(w > 0.0, w * lax.rsqrt(w), 0.0) * ig * xc
        a = a.reshape(ch // SUBLANES, SUBLANES, c_w)
        u = u.reshape(ch // SUBLANES, SUBLANES, c_w)
        for k in (1, 2, 4):
            a_sh = pltpu.roll(a, k, axis=1)
            u_sh = pltpu.roll(u, k, axis=1)
            ok = r8 >= k
            u = jnp.where(ok, u + a * u_sh, u)
            a = jnp.where(ok, a * a_sh, a)
        a_s[r0:r0 + ch, :] = a.reshape(ch, c_w)
        u_s[r0:r0 + ch, :] = u.reshape(ch, c_w)

    qt = (p_q * (QK_DIM ** -0.5 * LOG2E)).T
    arow = lax.broadcasted_iota(jnp.int32, (QK_DIM, tm), 0)
    ones2 = jnp.where(arow < 2, LOG2E_TERMS[0], jnp.where(arow < 4, LOG2E_TERMS[1],
                                                          jnp.where(arow < 6, LOG2E_TERMS[2], 0.0)))
    pieces = []
    for g in range(2 * N_ATT_HEADS):
        pieces += [qt[g * QK_DIM:(g + 1) * QK_DIM], ones2]
    qt_ref[0] = jnp.concatenate(pieces, axis=0).astype(BF16)

    lane = lax.broadcasted_iota(jnp.int32, (tm, HEAD_DIM), 1)
    j = (i * tm) % s_len + lax.broadcasted_iota(jnp.int32, (tm, HEAD_DIM), 0)
    j_lo = (j & 255).astype(F32)
    j_hi = (j - (j & 255)).astype(F32)
    vtt = p_v.T
    ones_rows = jnp.where(lax.broadcasted_iota(jnp.int32, (V_ROWS - HEAD_DIM, tm), 0) == 0, 1.0, 0.0)
    for h in range(N_ATT_HEADS):
        slope = slope_ref[h]
        kk = p_k[:, h * HEAD_DIM:(h + 1) * HEAD_DIM]
        in_aug = (lane >= QK_DIM) & (lane < QK_DIM + 2 * len(LOG2E_TERMS))
        aug = jnp.where(in_aug, jnp.where((lane & 1) == 0, slope * j_hi, slope * j_lo), 0.0)
        ka_ref[:, 2 * h * HEAD_DIM:(2 * h + 1) * HEAD_DIM] = jnp.where(lane < QK_DIM, kk, aug).astype(BF16)
        ka_ref[:, (2 * h + 1) * HEAD_DIM:(2 * h + 2) * HEAD_DIM] = jnp.where(
            lane < QK_DIM, pltpu.roll(kk, QK_DIM, axis=1), aug).astype(BF16)
        vt_ref[0, h * V_ROWS:h * V_ROWS + HEAD_DIM, :] = vtt[h * HEAD_DIM:(h + 1) * HEAD_DIM].astype(BF16)
        vt_ref[0, h * V_ROWS + HEAD_DIM:(h + 1) * V_ROWS, :] = ones_rows.astype(BF16)

    hprev = carry_h[0:1, :]
    for r0 in range(0, tm, SUBLANES):
        hg = u_s[r0:r0 + SUBLANES, :] + a_s[r0:r0 + SUBLANES, :] * hprev
        u_s[r0:r0 + SUBLANES, :] = hg
        hprev = hg[SUBLANES - 1:SUBLANES, :]
    carry_h[0:1, :] = hprev
    rnn_ref[...] = (u_s[...] * _gelu_tanh(p_lru[:, c_w:])).astype(rnn_ref.dtype)


def _gelu_tanh(x):
    return 0.5 * x * (1.0 + jnp.tanh(math.sqrt(2.0 / math.pi) * (x + 0.044715 * (x * x * x))))


def _alibi_slopes():
    nh = N_ATT_HEADS
    return jnp.asarray(np.array([2.0 ** (-8.0 * (i + 1) / nh) for i in range(nh)], dtype=np.float32))


def _in_proj(xf, norm_w, w_in_bf, att_w, s_len, conv_w, conv_b, w_gates_bf, b_gates, lru_lambda):
    t, d = xf.shape
    n = w_in_bf.shape[1]
    tm = min(ATT_TILE, t)
    ch = min(LRU_CHUNK, tm)
    nh = N_ATT_HEADS
    c_w = (n - 3 * att_w) // 2
    fix = lambda i: (0, 0)
    return pl.pallas_call(
        functools.partial(_inproj_kernel, att_w=att_w, s_len=s_len, ch=ch),
        grid=(t // tm,),
        in_specs=[pl.BlockSpec(memory_space=pltpu.SMEM),
                  pl.BlockSpec((tm, d), lambda i: (i, 0)),
                  pl.BlockSpec((1, d), fix),
                  pl.BlockSpec((d, n), fix),
                  pl.BlockSpec((CONV_WIDTH, c_w), fix),
                  pl.BlockSpec((1, c_w), fix),
                  pl.BlockSpec((c_w, 2 * c_w), fix),
                  pl.BlockSpec((1, 2 * c_w), fix),
                  pl.BlockSpec((1, c_w), fix)],
        out_specs=[pl.BlockSpec((1, 2 * att_w, tm), lambda i: (i, 0, 0)),
                   pl.BlockSpec((tm, 2 * att_w), lambda i: (i, 0)),
                   pl.BlockSpec((1, nh * V_ROWS, tm), lambda i: (i, 0, 0)),
                   pl.BlockSpec((tm, c_w), lambda i: (i, 0))],
        out_shape=[jax.ShapeDtypeStruct((t // tm, 2 * att_w, tm), BF16),
                   jax.ShapeDtypeStruct((t, 2 * att_w), BF16),
                   jax.ShapeDtypeStruct((t // tm, nh * V_ROWS, tm), BF16),
                   jax.ShapeDtypeStruct((t, c_w), BF16)],
        scratch_shapes=[pltpu.VMEM((tm + SUBLANES, c_w), F32), pltpu.VMEM((SUBLANES, c_w), F32),
                        pltpu.VMEM((tm, c_w), F32), pltpu.VMEM((tm, c_w), F32)],
        compiler_params=_cparams(("arbitrary",)),
        name="in_proj",
    )(_alibi_slopes(), xf, norm_w.reshape(1, d), w_in_bf, conv_w, conv_b.reshape(1, c_w), w_gates_bf,
      b_gates.reshape(1, 2 * c_w), lru_lambda.reshape(1, c_w))


def _attn_kernel(lam_ref, hw_ref, q_ref, k_ref, vt, o_ref, sb, mx, acc, *, tq, n_heads, lambda_init):
    qi = pl.program_id(2)
    n_maps = 2 * n_heads
    mx[...] = jnp.full(mx.shape, NEG_BIG, F32)
    acc[...] = jnp.zeros(acc.shape, F32)

    def values(c, n, lanes=slice(None)):
        return vt[c, (n // 2) * V_ROWS:(n // 2 + 1) * V_ROWS, lanes]

    def scores(c, slot):
        rows = pl.ds(pl.multiple_of(c * tq, tq), tq)
        for n in range(n_maps):
            sb[n, slot] = jnp.dot(k_ref[0, rows, n * HEAD_DIM:(n + 1) * HEAD_DIM],
                                  q_ref[0, n * HEAD_DIM:(n + 1) * HEAD_DIM, :],
                                  preferred_element_type=F32)

    def softmax_pv(c, slot):
        for n in range(n_maps):
            s = sb[n, slot]
            m_prev = mx[n]
            m_new = jnp.maximum(m_prev, jnp.max(s, axis=0, keepdims=True))
            p = jnp.exp2(s - m_new).astype(BF16)
            acc[n] = jnp.exp2(m_prev - m_new) * acc[n] + jnp.dot(values(c, n), p, preferred_element_type=F32)
            mx[n] = m_new

    def softmax_pv_diagonal(c, slot):
        hq = tq // 2
        keep_t = (lax.broadcasted_iota(jnp.int32, (hq, tq), 0) <= lax.broadcasted_iota(jnp.int32, (hq, tq), 1))
        keep_b = (lax.broadcasted_iota(jnp.int32, (hq, hq), 0) <= lax.broadcasted_iota(jnp.int32, (hq, hq), 1))
        for n in range(n_maps):
            top = jnp.where(keep_t, sb[n, slot, :hq, :], NEG_BIG)
            bot = jnp.where(keep_b, sb[n, slot, hq:, hq:], NEG_BIG)
            mt = jnp.max(top, axis=0, keepdims=True)
            mb = jnp.max(bot, axis=0, keepdims=True)
            m_prev = mx[n]
            m_new = jnp.maximum(m_prev, jnp.concatenate([mt[:, :hq], jnp.maximum(mt[:, hq:], mb)], axis=1))
            p_top = jnp.exp2(top - m_new).astype(BF16)
            p_bot = jnp.exp2(bot - m_new[:, hq:]).astype(BF16)
            acc[n] = (jnp.exp2(m_prev - m_new) * acc[n]
                      + jnp.dot(values(c, n, slice(0, hq)), p_top, preferred_element_type=F32))
            acc[n, :, hq:] += jnp.dot(values(c, n, slice(hq, tq)), p_bot, preferred_element_type=F32)
            mx[n] = m_new

    scores(0, 0)

    def body(j, c):
        scores(2 * j + 1, 1)
        softmax_pv(2 * j, 0)
        scores(2 * j + 2, 0)
        softmax_pv(2 * j + 1, 1)
        return c

    lax.fori_loop(0, qi // 2, body, 0)

    @pl.when(qi % 2 == 0)
    def _():
        softmax_pv_diagonal(qi, 0)

    @pl.when(qi % 2 == 1)
    def _():
        scores(qi, 1)
        softmax_pv(qi - 1, 0)
        softmax_pv_diagonal(qi, 1)

    lam = (jnp.exp(jnp.sum(lam_ref[0:1, :] * lam_ref[1:2, :], axis=1, keepdims=True))
           - jnp.exp(jnp.sum(lam_ref[2:3, :] * lam_ref[3:4, :], axis=1, keepdims=True))
           + lambda_init)
    for hh in range(n_heads):
        o1 = acc[2 * hh, :HEAD_DIM, :] * (1.0 / acc[2 * hh, HEAD_DIM:HEAD_DIM + 1, :])
        o2 = acc[2 * hh + 1, :HEAD_DIM, :] * (1.0 / acc[2 * hh + 1, HEAD_DIM:HEAD_DIM + 1, :])
        o = o1 - lam * o2
        o = o * lax.rsqrt(jnp.mean(o * o, axis=0, keepdims=True) + HEAD_NORM_EPS)
        o_ref[0, :, hh * HEAD_DIM:(hh + 1) * HEAD_DIM] = (
            o.T * hw_ref[...] * (1.0 - lambda_init)).astype(o_ref.dtype)


def _diff_attention(qt, ka, vt, lam_params, head_norm_w, lambda_init, b, s):
    nh = N_ATT_HEADS
    hp = ATT_HEADS_PER_STEP
    tq = qt.shape[2]
    nq = s // tq
    return pl.pallas_call(
        functools.partial(_attn_kernel, tq=tq, n_heads=hp, lambda_init=lambda_init),
        grid=(b, nh // hp, nq),
        in_specs=[pl.BlockSpec((4, QK_DIM), lambda bi, hi, qi: (0, 0)),
                  pl.BlockSpec((1, HEAD_DIM), lambda bi, hi, qi: (0, 0)),
                  pl.BlockSpec((1, hp * 2 * HEAD_DIM, tq), lambda bi, hi, qi: (bi * nq + qi, hi, 0)),
                  pl.BlockSpec((1, s, hp * 2 * HEAD_DIM), lambda bi, hi, qi: (bi, 0, hi)),
                  pl.BlockSpec((nq, hp * V_ROWS, tq), lambda bi, hi, qi: (bi, hi, 0))],
        out_specs=pl.BlockSpec((1, tq, hp * HEAD_DIM), lambda bi, hi, qi: (bi, qi, hi)),
        out_shape=jax.ShapeDtypeStruct((b, s, nh * HEAD_DIM), BF16),
        scratch_shapes=[pltpu.VMEM((2 * hp, 2, tq, tq), F32), pltpu.VMEM((2 * hp, 1, tq), F32),
                        pltpu.VMEM((2 * hp, V_ROWS, tq), F32)],
        compiler_params=_cparams(("parallel", "parallel", "arbitrary"), vmem=56 * 1024 * 1024),
        name="diff_attn",
    )(lam_params, head_norm_w.reshape(1, HEAD_DIM), qt, ka.reshape(b, s, ka.shape[1]), vt)


def _outproj_kernel(att_ref, rnn_ref, x_ref, wo_ref, nw_ref, wrt_ref, brc_ref,
                    x1_ref, hn_ref, route_ref, route_t_ref, n8_ref, *, att_w):
    y = jnp.dot(att_ref[...], wo_ref[:att_w, :], preferred_element_type=F32)
    y = y + jnp.dot(rnn_ref[...], wo_ref[att_w:, :], preferred_element_type=F32)
    x1 = x_ref[...] + y
    x1_ref[...] = x1
    hn = (x1 * lax.rsqrt(jnp.mean(x1 * x1, axis=-1, keepdims=True) + NORM_EPS) * nw_ref[...]).astype(BF16)
    hn_ref[...] = hn
    tm = hn.shape[0]

    lg = lax.dot_general(wrt_ref[...], hn, (((1,), (1,)), ((), ())), preferred_element_type=F32)
    lg = lg[:ROUTE_ROWS] + brc_ref[:ROUTE_ROWS, 0:1]
    rowf = lax.broadcasted_iota(jnp.int32, lg.shape, 0).astype(F32)
    big = float(LANES)
    ninf = -jnp.inf
    is_g = rowf < N_GROUPS
    lgm = jnp.where(is_g, lg, ninf)
    mg = jnp.max(lgm, axis=0, keepdims=True)
    g_sel = jnp.min(jnp.where(lgm == mg, rowf, big), axis=0, keepdims=True)
    pg = 1.0 / jnp.sum(jnp.where(is_g, jnp.exp(lgm - mg), 0.0), axis=0, keepdims=True)
    lo = N_GROUPS + EXPERTS_PER_GROUP * g_sel
    in_grp = (rowf >= lo) & (rowf < lo + EXPERTS_PER_GROUP)
    lem = jnp.where(in_grp, lg, ninf)
    v1 = jnp.max(lem, axis=0, keepdims=True)
    i1 = jnp.min(jnp.where(lem == v1, rowf, big), axis=0, keepdims=True)
    lem2 = jnp.where(rowf == i1, ninf, lem)
    v2 = jnp.max(lem2, axis=0, keepdims=True)
    i2 = jnp.min(jnp.where(lem2 == v2, rowf, big), axis=0, keepdims=True)
    e2 = jnp.exp(v2 - v1)
    den = 1.0 + e2
    g1 = pg / den
    g2 = pg * e2 / den

    oh1 = jnp.where(rowf == i1, 1.0, 0.0)
    oh2 = jnp.where(rowf == i2, 1.0, 0.0)
    oh = oh1 + oh2
    earlier = (lax.broadcasted_iota(jnp.int32, (tm, tm), 0)
               < lax.broadcasted_iota(jnp.int32, (tm, tm), 1)).astype(BF16)
    pref = jnp.dot(oh.astype(BF16), earlier, preferred_element_type=F32)
    cnt = jnp.sum(oh, axis=1, keepdims=True)
    n8 = jnp.floor((cnt + (SUBLANES - 1)) * (1.0 / SUBLANES))
    n8_b = jnp.broadcast_to(n8, (ROUTE_ROWS, LANES))
    before = (lax.broadcasted_iota(jnp.int32, (ROUTE_ROWS, ROUTE_ROWS), 1)
              < lax.broadcasted_iota(jnp.int32, (ROUTE_ROWS, ROUTE_ROWS), 0)).astype(BF16)
    loff8 = jnp.dot(before, n8_b.astype(BF16), preferred_element_type=F32)[:, 0:1]
    pos = SUBLANES * loff8 + pref
    lp1 = jnp.sum(oh1 * pos, axis=0, keepdims=True)
    lp2 = jnp.sum(oh2 * pos, axis=0, keepdims=True)
    route_t = jnp.concatenate([g1, g2, lp1, lp2, jnp.zeros((LANES - 4, tm), F32)], axis=0)
    route_t_ref[0] = route_t[:SUBLANES]
    route_ref[...] = route_t.T
    n8_ref[0] = n8_b


def _out_proj(att, rnn, xf, w_out_bf, norm_w, w_route_t_bf, b_route_col):
    t, d = xf.shape
    att_w = att.shape[1]
    tm = min(ROW_TILE, t)
    row = lambda i: (i, 0)
    fix = lambda i: (0, 0)
    return pl.pallas_call(
        functools.partial(_outproj_kernel, att_w=att_w),
        grid=(t // tm,),
        in_specs=[pl.BlockSpec((tm, att_w), row), pl.BlockSpec((tm, rnn.shape[1]), row),
                  pl.BlockSpec((tm, d), row), pl.BlockSpec(w_out_bf.shape, fix),
                  pl.BlockSpec((1, d), fix), pl.BlockSpec((LANES, d), fix), pl.BlockSpec((LANES, 1), fix)],
        out_specs=[pl.BlockSpec((tm, d), row), pl.BlockSpec((tm, d), row), pl.BlockSpec((tm, LANES), row),
                   pl.BlockSpec((1, SUBLANES, tm), lambda i: (i, 0, 0)),
                   pl.BlockSpec((1, ROUTE_ROWS, LANES), lambda i: (i, 0, 0))],
        out_shape=[jax.ShapeDtypeStruct((t, d), F32), jax.ShapeDtypeStruct((t, d), BF16),
                   jax.ShapeDtypeStruct((t, LANES), F32),
                   jax.ShapeDtypeStruct((t // tm, SUBLANES, tm), F32),
                   jax.ShapeDtypeStruct((t // tm, ROUTE_ROWS, LANES), F32)],
        compiler_params=_cparams(("parallel",)),
        name="out_proj",
    )(att, rnn, xf, w_out_bf, norm_w.reshape(1, d), w_route_t_bf, b_route_col)


def _local_rows(tm):
    return -(-(TOP_K * tm + N_EXPERTS * (SUBLANES - 1)) // LANES) * LANES


def _segment_tables(n8_tiles, tm_moe, n_tiles):
    n8 = n8_tiles[:, N_GROUPS:N_GROUPS + N_EXPERTS, 0].astype(jnp.int32)
    c8 = n8 * SUBLANES
    loff = jnp.cumsum(c8, axis=1) - c8
    gtot = jnp.sum(c8, axis=0)
    gpad = (gtot + tm_moe - 1) // tm_moe * tm_moe
    gend = jnp.cumsum(gpad)
    gstart = gend - gpad
    gbase = gstart[None, :] + jnp.cumsum(c8, axis=0) - c8
    tile_row0 = jnp.arange(n_tiles, dtype=jnp.int32) * tm_moe
    tile_e = jnp.minimum(jnp.sum((gend[None, :] <= tile_row0[:, None]).astype(jnp.int32), axis=1),
                         N_EXPERTS - 1).astype(jnp.int32)
    n_used = (gend[-1] // tm_moe).astype(jnp.int32).reshape(1)
    tail_start = (gstart + gtot).astype(jnp.int32)
    tail_n8 = ((gpad - gtot) // SUBLANES).astype(jnp.int32)
    after = gend[tile_e] // tm_moe
    next_e = jnp.where(after < n_used[0], tile_e[jnp.minimum(after, n_tiles - 1)], -1).astype(jnp.int32)
    first = jnp.concatenate([jnp.ones((1,), jnp.int32), (tile_e[1:] != tile_e[:-1]).astype(jnp.int32)])
    w_slot = ((jnp.cumsum(first) - 1) % 2).astype(jnp.int32)
    rows_in_tile = jnp.clip((gstart + gtot)[tile_e] - tile_row0, 1, tm_moe)
    tile_sub = ((rows_in_tile + MOE_SUBTILE - 1) // MOE_SUBTILE).astype(jnp.int32)
    return (n8.reshape(-1), loff.reshape(-1).astype(jnp.int32), gbase.reshape(-1).astype(jnp.int32),
            tile_e, n_used, tail_start, tail_n8, next_e, w_slot, tile_sub)


def _segment_copies(n8_ref, src_off_ref, dst_off_ref, src, dst, sem, tile, wait):
    def rows_of(e):
        return pl.multiple_of(n8_ref[tile * N_EXPERTS + e] * SUBLANES, SUBLANES)

    if wait:
        total = lax.fori_loop(0, N_EXPERTS, lambda e, acc: acc + rows_of(e), 0)
        total = pl.multiple_of(total, SUBLANES)
        pltpu.make_async_copy(src.at[pl.ds(0, total), :], dst.at[pl.ds(0, total), :], sem).wait()
        return

    def per_expert(e, c):
        k = tile * N_EXPERTS + e
        rows = rows_of(e)

        @pl.when(rows > 0)
        def _():
            pltpu.make_async_copy(
                src.at[pl.ds(pl.multiple_of(src_off_ref[k], SUBLANES), rows), :],
                dst.at[pl.ds(pl.multiple_of(dst_off_ref[k], SUBLANES), rows), :], sem).start()
        return c
    lax.fori_loop(0, N_EXPERTS, per_expert, 0)


def _pack_bf16_pairs(x):
    n = x.shape[1] // 2
    bits = lax.bitcast_convert_type(x, jnp.uint32)
    return (bits[:, :n] >> 16) | (bits[:, n:] & jnp.uint32(0xFFFF0000))


def _unpack_bf16_pairs(p):
    lo = lax.bitcast_convert_type(p << 16, F32)
    hi = lax.bitcast_convert_type(p & jnp.uint32(0xFFFF0000), F32)
    return jnp.concatenate([lo, hi], axis=1).astype(BF16)


def _dispatch_kernel(n8_ref, loff_ref, gbase_ref, tstart_ref, tn8_ref, nu_ref, hn_ref, route_ref, xs_hbm,
                     stage, zbuf, sem, zsem, *, lcap, n_tt):
    i = pl.program_id(0)
    slot = i % 2
    tm = hn_ref.shape[0]
    tm_moe = zbuf.shape[0]
    n_tiles = xs_hbm.shape[0] // tm_moe

    def tail_copies(wait):
        def go(cp):
            if wait:
                cp.wait()
            else:
                cp.start()

        def per_expert(e, c):
            rows = pl.multiple_of(tn8_ref[e] * SUBLANES, SUBLANES)

            @pl.when(rows > 0)
            def _():
                go(pltpu.make_async_copy(
                    zbuf.at[pl.ds(0, rows), :],
                    xs_hbm.at[pl.ds(pl.multiple_of(tstart_ref[e], SUBLANES), rows), :], zsem.at[0]))
            return c
        lax.fori_loop(0, N_EXPERTS, per_expert, 0)

        def per_unused_tile(j, c):
            go(pltpu.make_async_copy(zbuf, xs_hbm.at[pl.ds(pl.multiple_of(j * tm_moe, tm_moe), tm_moe), :],
                                     zsem.at[0]))
            return c
        lax.fori_loop(nu_ref[0], n_tiles, per_unused_tile, 0)

    @pl.when(i == 0)
    def _():
        zbuf[...] = jnp.zeros(zbuf.shape, zbuf.dtype)
        tail_copies(False)

    @pl.when(i >= 2)
    def _():
        _segment_copies(n8_ref, loff_ref, gbase_ref, stage.at[slot], xs_hbm, sem.at[slot], i - 2, True)

    lp1 = route_ref[0, 2:3, :]
    lp2 = route_ref[0, 3:4, :]
    rpos = lax.broadcasted_iota(jnp.int32, (lcap, tm), 0).astype(F32)
    sel = jnp.where((rpos == lp1) | (rpos == lp2), 1.0, 0.0).astype(BF16)
    stage[slot] = _pack_bf16_pairs(jnp.dot(sel, hn_ref[...], preferred_element_type=F32))
    _segment_copies(n8_ref, loff_ref, gbase_ref, stage.at[slot], xs_hbm, sem.at[slot], i, False)

    @pl.when(i == n_tt - 1)
    def _():
        _segment_copies(n8_ref, loff_ref, gbase_ref, stage.at[slot], xs_hbm, sem.at[slot], i, True)
        if n_tt > 1:
            _segment_copies(n8_ref, loff_ref, gbase_ref, stage.at[1 - slot], xs_hbm, sem.at[1 - slot],
                            i - 1, True)
        tail_copies(True)


def _dispatch(hn, route, tables, n_rows, tm_moe):
    t, d = hn.shape
    tm = min(ROW_TILE, t)
    n_tt = t // tm
    lcap = _local_rows(tm)
    n8, loff, gbase, _, n_used, tail_start, tail_n8 = tables[:7]
    grid_spec = pltpu.PrefetchScalarGridSpec(
        num_scalar_prefetch=6,
        grid=(n_tt,),
        in_specs=[pl.BlockSpec((tm, d), lambda i, *_: (i, 0)),
                  pl.BlockSpec((1, SUBLANES, tm), lambda i, *_: (i, 0, 0))],
        out_specs=pl.BlockSpec(memory_space=pl.ANY),
        scratch_shapes=[pltpu.VMEM((2, lcap, d // 2), jnp.uint32), pltpu.VMEM((tm_moe, d // 2), jnp.uint32),
                        pltpu.SemaphoreType.DMA((2,)), pltpu.SemaphoreType.DMA((1,))],
    )
    return pl.pallas_call(
        functools.partial(_dispatch_kernel, lcap=lcap, n_tt=n_tt),
        grid_spec=grid_spec,
        out_shape=jax.ShapeDtypeStruct((n_rows, d // 2), jnp.uint32),
        compiler_params=_cparams(("arbitrary",), has_side_effects=True),
        name="dispatch",
    )(n8, loff, gbase, tail_start, tail_n8, n_used, hn, route)


def _moe_kernel(te_ref, nu_ref, nxt_ref, wslot_ref, nsub_ref, xs_ref, wg_hbm, wu_hbm, wd_hbm, y_ref,
                wgf, wuf, wdf, wgb, wub, wdb, wsem):
    i = pl.program_id(0)

    def weight_copies(e, sl):
        return (pltpu.make_async_copy(wg_hbm.at[e], wgf.at[sl], wsem.at[sl, 0]),
                pltpu.make_async_copy(wu_hbm.at[e], wuf.at[sl], wsem.at[sl, 1]),
                pltpu.make_async_copy(wd_hbm.at[e], wdf.at[sl], wsem.at[sl, 2]))

    @pl.when(i == 0)
    def _():
        for cp in weight_copies(te_ref[0], wslot_ref[0]):
            cp.start()

    @pl.when(i < nu_ref[0])
    def _():
        changed = jnp.logical_or(i == 0, te_ref[i] != te_ref[jnp.maximum(i - 1, 0)])

        @pl.when(changed)
        def _():
            sl = wslot_ref[i]
            for cp in weight_copies(te_ref[i], sl):
                cp.wait()
            wgb[...] = wgf[sl].astype(BF16)
            wub[...] = wuf[sl].astype(BF16)
            wdb[...] = wdf[sl].astype(BF16)

            @pl.when(nxt_ref[i] >= 0)
            def _():
                for cp in weight_copies(nxt_ref[i], 1 - sl):
                    cp.start()

        tm = xs_ref.shape[0]
        n_sub = tm // MOE_SUBTILE
        filled = nsub_ref[i]
        for k in range(1, n_sub + 1):
            @pl.when(filled == k)
            def _(k=k):
                r = k * MOE_SUBTILE
                x = _unpack_bf16_pairs(xs_ref[:r, :])
                g = jnp.dot(x, wgb[...], preferred_element_type=F32)
                u = jnp.dot(x, wub[...], preferred_element_type=F32)
                hdn = (g * jax.nn.sigmoid(g) * u).astype(BF16)
                y = jnp.dot(hdn, wdb[...], preferred_element_type=F32)
                y_ref[:r, :] = _pack_bf16_pairs(y.astype(BF16).astype(F32))
                if r < tm:
                    y_ref[r:, :] = jnp.zeros((tm - r, y_ref.shape[1]), y_ref.dtype)

    @pl.when(i >= nu_ref[0])
    def _():
        y_ref[...] = jnp.zeros(y_ref.shape, y_ref.dtype)


def _moe(xs, tile_e, n_used, next_e, w_slot, tile_sub, w_g, w_u, w_d, tm):
    n_rows = xs.shape[0]
    d = w_g.shape[1]
    dp = xs.shape[1]
    n_tiles = n_rows // tm
    ff = w_g.shape[2]
    row_blk = lambda i, te, nu, *_: (jnp.minimum(i, nu[0] - 1), 0)
    hbm = pl.BlockSpec(memory_space=pl.ANY)
    grid_spec = pltpu.PrefetchScalarGridSpec(
        num_scalar_prefetch=5,
        grid=(n_tiles,),
        in_specs=[pl.BlockSpec((tm, dp), row_blk), hbm, hbm, hbm],
        out_specs=pl.BlockSpec((tm, dp), lambda i, *_: (i, 0)),
        scratch_shapes=[pltpu.VMEM((2, d, ff), F32), pltpu.VMEM((2, d, ff), F32), pltpu.VMEM((2, ff, d), F32),
                        pltpu.VMEM((d, ff), BF16), pltpu.VMEM((d, ff), BF16), pltpu.VMEM((ff, d), BF16),
                        pltpu.SemaphoreType.DMA((2, 3))],
    )
    return pl.pallas_call(
        _moe_kernel,
        grid_spec=grid_spec,
        out_shape=jax.ShapeDtypeStruct((n_rows, dp), jnp.uint32),
        compiler_params=_cparams(("arbitrary",)),
        name="moe",
    )(tile_e, n_used, next_e, w_slot, tile_sub, xs, w_g, w_u, w_d)


def _combine_kernel(n8_ref, loff_ref, gbase_ref, x1_ref, route_ref, nw_ref, y_hbm, o_ref,
                    ybuf, sem, *, lcap, n_tt):
    i = pl.program_id(0)
    slot = i % 2
    tm = x1_ref.shape[0]

    def fetch(tile, sl, wait):
        _segment_copies(n8_ref, gbase_ref, loff_ref, y_hbm, ybuf.at[sl], sem.at[sl], tile, wait)

    @pl.when(i == 0)
    def _():
        ybuf[...] = jnp.zeros(ybuf.shape, ybuf.dtype)
        fetch(0, 0, False)

    @pl.when(i + 1 < n_tt)
    def _():
        fetch(i + 1, 1 - slot, False)

    fetch(i, slot, True)
    yb = _unpack_bf16_pairs(ybuf[slot])
    half = tm // 2
    for r0 in (0, half):
        rows = slice(r0, r0 + half)
        g1 = route_ref[rows, 0:1]
        g2 = route_ref[rows, 1:2]
        lp1 = route_ref[rows, 2:3]
        lp2 = route_ref[rows, 3:4]
        cpos = lax.broadcasted_iota(jnp.int32, (half, lcap), 1).astype(F32)
        gsel = (jnp.where(cpos == lp1, g1, 0.0) + jnp.where(cpos == lp2, g2, 0.0)).astype(BF16)
        x = x1_ref[rows, :] + jnp.dot(gsel, yb, preferred_element_type=F32)
        o_ref[rows, :] = x * lax.rsqrt(jnp.mean(x * x, axis=-1, keepdims=True) + NORM_EPS) * nw_ref[...]


def _combine(x1, y, route, norm_w, tables):
    t, d = x1.shape
    tm = min(ROW_TILE, t)
    n_tt = t // tm
    lcap = _local_rows(tm)
    n8, loff, gbase = tables[:3]
    grid_spec = pltpu.PrefetchScalarGridSpec(
        num_scalar_prefetch=3,
        grid=(n_tt,),
        in_specs=[pl.BlockSpec((tm, d), lambda i, *_: (i, 0)),
                  pl.BlockSpec((tm, LANES), lambda i, *_: (i, 0)),
                  pl.BlockSpec((1, d), lambda i, *_: (0, 0)),
                  pl.BlockSpec(memory_space=pl.ANY)],
        out_specs=pl.BlockSpec((tm, d), lambda i, *_: (i, 0)),
        scratch_shapes=[pltpu.VMEM((2, lcap, d // 2), jnp.uint32), pltpu.SemaphoreType.DMA((2,))],
    )
    return pl.pallas_call(
        functools.partial(_combine_kernel, lcap=lcap, n_tt=n_tt),
        grid_spec=grid_spec,
        out_shape=jax.ShapeDtypeStruct((t, d), F32),
        compiler_params=_cparams(("arbitrary",)),
        name="combine",
    )(n8, loff, gbase, x1, route, norm_w.reshape(1, d), y)


def _block_diag(w):
    n, bi, bj = w.shape
    eye = jnp.eye(n, dtype=w.dtype)
    return jnp.einsum('nij,nm->nimj', w, eye).reshape(n * bi, n * bj)


def kernel(x, mix_norm_w, w_in, lambda_q1, lambda_k1, lambda_q2, lambda_k2, head_norm_w, conv_w, conv_b, w_rgate, b_rgate, w_igate, b_igate, lru_lambda, w_out, ffn_norm_w, w_router_group, b_router_group, w_router_expert, b_router_expert, w_exp_gate, w_exp_up, w_exp_down, final_norm_w):
    b, s, d = x.shape
    t = b * s
    assert w_in.shape[0] == 1, "single-layer stack only"
    att_w = N_ATT_HEADS * HEAD_DIM
    tm_moe = MOE_TILE
    xf = x.reshape(t, d)
    for l in range(1):
        lambda_init = 0.8 - 0.6 * math.exp(-0.3 * l)
        assert s % ATT_TILE == 0, "sequence length must be a multiple of the attention tile"
        w_bd = jnp.concatenate([_block_diag(w_rgate[l]), _block_diag(w_igate[l])], axis=1).astype(BF16)
        b_cat = jnp.concatenate([b_rgate[l], b_igate[l]])
        qt, ka, vt, rnn = _in_proj(xf, mix_norm_w[l], w_in[l].astype(BF16), att_w, s,
                                   conv_w[l], conv_b[l], w_bd, b_cat, lru_lambda[l])
        lam_params = jnp.stack([lambda_q1[l], lambda_k1[l], lambda_q2[l], lambda_k2[l]]).astype(F32)
        att = _diff_attention(qt, ka, vt, lam_params, head_norm_w[l], lambda_init, b, s)
        w_route = jnp.concatenate([w_router_group[l], w_router_expert[l]], axis=1).T
        w_route = jnp.pad(w_route, ((0, LANES - w_route.shape[0]), (0, 0))).astype(BF16)
        b_route = jnp.concatenate([b_router_group[l], b_router_expert[l]])
        b_route = jnp.pad(b_route, (0, LANES - b_route.shape[0])).reshape(LANES, 1).astype(F32)
        x1, hn, route, route_t, n8_tiles = _out_proj(att.reshape(t, att_w), rnn.reshape(t, -1), xf,
                                                     w_out[l].astype(BF16), ffn_norm_w[l], w_route, b_route)
        n_tt = n8_tiles.shape[0]
        max_rows = TOP_K * t + n_tt * N_EXPERTS * (SUBLANES - 1) + N_EXPERTS * (tm_moe - 1)
        n_tiles = -(-max_rows // tm_moe)
        tables = _segment_tables(n8_tiles, tm_moe, n_tiles)
        xs = _dispatch(hn, route_t, tables, n_tiles * tm_moe, tm_moe)
        y = _moe(xs, tables[3], tables[4], tables[7], tables[8], tables[9],
                 w_exp_gate[l], w_exp_up[l], w_exp_down[l], tm_moe)
        out = _combine(x1, y, route, final_norm_w, tables)
    return out.reshape(b, s, d)
```

```python
import functools
import math

import numpy as np
import jax
import jax.numpy as jnp
from jax import lax
from jax.experimental import pallas as pl
from jax.experimental.pallas import tpu as pltpu

F32 = jnp.float32
BF16 = jnp.bfloat16

N_ATT_HEADS = 4
HEAD_DIM = 128
QK_DIM = 64
N_RNN_BLOCKS = 8
CONV_WIDTH = 4
LRU_C = 8.0
N_GROUPS = 4
EXPERTS_PER_GROUP = 8
N_EXPERTS = N_GROUPS * EXPERTS_PER_GROUP
TOP_K = 2
NORM_EPS = 1e-6
HEAD_NORM_EPS = 1e-5
LOG2E = math.log2(math.e)


def _bf16_terms(x, n):
    terms = []
    for _ in range(n):
        t = float(np.float32(x).astype(jnp.bfloat16))
        terms.append(t)
        x -= t
    return tuple(terms)


LOG2E_TERMS = _bf16_terms(LOG2E, 3)
LANES = 128
SUBLANES = 8
NEG_BIG = -1e30

ROW_TILE = 512
ROUTE_ROWS = 48
ATT_TILE = 512
ATT_HEADS_PER_STEP = 4
V_ROWS = HEAD_DIM + 16
LRU_CHUNK = 128
MOE_TILE = 1024
MOE_SUBTILE = 256
VMEM_LIMIT = 48 * 1024 * 1024


def _cparams(sem, vmem=VMEM_LIMIT, **kw):
    return pltpu.CompilerParams(dimension_semantics=sem, vmem_limit_bytes=vmem, **kw)


def _inproj_kernel(slope_ref, x_ref, nw_ref, w_ref, cw_ref, cb_ref, wg_ref, bg_ref, lam_ref,
                   qt_ref, ka_ref, vt_ref, rnn_ref, xs, carry_h, a_s, u_s, *, att_w, s_len, ch):
    i = pl.program_id(0)
    x = x_ref[...]
    tm = x.shape[0]
    c_w = (w_ref.shape[1] - 3 * att_w) // 2

    @pl.when((i * tm) % s_len == 0)
    def _():
        xs[0:SUBLANES, :] = jnp.zeros((SUBLANES, c_w), F32)
        carry_h[...] = jnp.zeros(carry_h.shape, F32)

    ms = jnp.mean(x * x, axis=-1, keepdims=True)
    hn = (x * lax.rsqrt(ms + NORM_EPS) * nw_ref[...]).astype(BF16)
    p_lru = jnp.dot(hn, w_ref[:, 3 * att_w:], preferred_element_type=F32)

    xs[SUBLANES:, :] = p_lru[:, :c_w]
    neg_lam = -lam_ref[...]
    sp = jnp.maximum(neg_lam, 0.0) + jnp.log1p(jnp.exp(-jnp.abs(neg_lam)))
    cw = cw_ref[...]
    cb = cb_ref[...]
    bias = bg_ref[...]
    r8 = lax.broadcasted_iota(jnp.int32, (ch // SUBLANES, SUBLANES, c_w), 1)
    xcs, zs = [], []
    for c in range(tm // ch):
        r0 = c * ch
        win = xs[r0:r0 + ch + SUBLANES, :]
        xc = cw[3:4, :] * win[SUBLANES:] + cb
        for k in (1, 2, 3):
            xc = xc + cw[3 - k:4 - k, :] * pltpu.roll(win, k, axis=0)[SUBLANES:]
        xcs.append(xc)
        zs.append(jnp.dot(xc.astype(BF16), wg_ref[...], preferred_element_type=F32) + bias)
    xs[0:SUBLANES, :] = xs[tm:tm + SUBLANES, :]

    p_q = jnp.dot(hn, w_ref[:, :att_w], preferred_element_type=F32)
    p_v = jnp.dot(hn, w_ref[:, 2 * att_w:3 * att_w], preferred_element_type=F32)
    p_k = jnp.dot(hn, w_ref[:, att_w:2 * att_w], preferred_element_type=F32)

    for c in range(tm // ch):
        r0 = c * ch
        xc, z = xcs[c], zs[c]
        r = jax.nn.sigmoid(z[:, :c_w])
        ig = jax.nn.sigmoid(z[:, c_w:])
        log_a = (-LRU_C) * r * sp
        a = jnp.exp(log_a)
        w = jnp.tanh(-log_a) * (1.0 + a * a)
        u = jnp.where(w > 0.0, w * lax.rsqrt(w), 0.0) * ig * xc
        a = a.reshape(ch // SUBLANES, SUBLANES, c_w)
        u = u.reshape(ch // SUBLANES, SUBLANES, c_w)
        for k in (1, 2, 4):
            a_sh = pltpu.roll(a, k, axis=1)
            u_sh = pltpu.roll(u, k, axis=1)
            ok = r8 >= k
            u = jnp.where(ok, u + a * u_sh, u)
            a = jnp.where(ok, a * a_sh, a)
        a_s[r0:r0 + ch, :] = a.reshape(ch, c_w)
        u_s[r0:r0 + ch, :] = u.reshape(ch, c_w)

    qt = (p_q * (QK_DIM ** -0.5 * LOG2E)).T
    arow = lax.broadcasted_iota(jnp.int32, (QK_DIM, tm), 0)
    ones2 = jnp.where(arow < 2, LOG2E_TERMS[0], jnp.where(arow < 4, LOG2E_TERMS[1],
                                                          jnp.where(arow < 6, LOG2E_TERMS[2], 0.0)))
    pieces = []
    for g in range(2 * N_ATT_HEADS):
        pieces += [qt[g * QK_DIM:(g + 1) * QK_DIM], ones2]
    qt_ref[0] = jnp.concatenate(pieces, axis=0).astype(BF16)

    lane = lax.broadcasted_iota(jnp.int32, (tm, HEAD_DIM), 1)
    j = (i * tm) % s_len + lax.broadcasted_iota(jnp.int32, (tm, HEAD_DIM), 0)
    j_lo = (j & 255).astype(F32)
    j_hi = (j - (j & 255)).astype(F32)
    vtt = p_v.T
    ones_rows = jnp.where(lax.broadcasted_iota(jnp.int32, (V_ROWS - HEAD_DIM, tm), 0) == 0, 1.0, 0.0)
    for h in range(N_ATT_HEADS):
        slope = slope_ref[h]
        kk = p_k[:, h * HEAD_DIM:(h + 1) * HEAD_DIM]
        in_aug = (lane >= QK_DIM) & (lane < QK_DIM + 2 * len(LOG2E_TERMS))
        aug = jnp.where(in_aug, jnp.where((lane & 1) == 0, slope * j_hi, slope * j_lo), 0.0)
        ka_ref[:, 2 * h * HEAD_DIM:(2 * h + 1) * HEAD_DIM] = jnp.where(lane < QK_DIM, kk, aug).astype(BF16)
        ka_ref[:, (2 * h + 1) * HEAD_DIM:(2 * h + 2) * HEAD_DIM] = jnp.where(
            lane < QK_DIM, pltpu.roll(kk, QK_DIM, axis=1), aug).astype(BF16)
        vt_ref[0, h * V_ROWS:h * V_ROWS + HEAD_DIM, :] = vtt[h * HEAD_DIM:(h + 1) * HEAD_DIM].astype(BF16)
        vt_ref[0, h * V_ROWS + HEAD_DIM:(h + 1) * V_ROWS, :] = ones_rows.astype(BF16)

    hprev = carry_h[0:1, :]
    for r0 in range(0, tm, SUBLANES):
        hg = u_s[r0:r0 + SUBLANES, :] + a_s[r0:r0 + SUBLANES, :] * hprev
        u_s[r0:r0 + SUBLANES, :] = hg
        hprev = hg[SUBLANES - 1:SUBLANES, :]
    carry_h[0:1, :] = hprev
    rnn_ref[...] = (u_s[...] * _gelu_tanh(p_lru[:, c_w:])).astype(rnn_ref.dtype)


def _gelu_tanh(x):
    return 0.5 * x * (1.0 + jnp.tanh(math.sqrt(2.0 / math.pi) * (x + 0.044715 * (x * x * x))))


def _alibi_slopes():
    nh = N_ATT_HEADS
    return jnp.asarray(np.array([2.0 ** (-8.0 * (i + 1) / nh) for i in range(nh)], dtype=np.float32))


def _in_proj(xf, norm_w, w_in_bf, att_w, s_len, conv_w, conv_b, w_gates_bf, b_gates, lru_lambda):
    t, d = xf.shape
    n = w_in_bf.shape[1]
    tm = min(ATT_TILE, t)
    ch = min(LRU_CHUNK, tm)
    nh = N_ATT_HEADS
    c_w = (n - 3 * att_w) // 2
    fix = lambda i: (0, 0)
    return pl.pallas_call(
        functools.partial(_inproj_kernel, att_w=att_w, s_len=s_len, ch=ch),
        grid=(t // tm,),
        in_specs=[pl.BlockSpec(memory_space=pltpu.SMEM),
                  pl.BlockSpec((tm, d), lambda i: (i, 0)),
                  pl.BlockSpec((1, d), fix),
                  pl.BlockSpec((d, n), fix),
                  pl.BlockSpec((CONV_WIDTH, c_w), fix),
                  pl.BlockSpec((1, c_w), fix),
                  pl.BlockSpec((c_w, 2 * c_w), fix),
                  pl.BlockSpec((1, 2 * c_w), fix),
                  pl.BlockSpec((1, c_w), fix)],
        out_specs=[pl.BlockSpec((1, 2 * att_w, tm), lambda i: (i, 0, 0)),
                   pl.BlockSpec((tm, 2 * att_w), lambda i: (i, 0)),
                   pl.BlockSpec((1, nh * V_ROWS, tm), lambda i: (i, 0, 0)),
                   pl.BlockSpec((tm, c_w), lambda i: (i, 0))],
        out_shape=[jax.ShapeDtypeStruct((t // tm, 2 * att_w, tm), BF16),
                   jax.ShapeDtypeStruct((t, 2 * att_w), BF16),
                   jax.ShapeDtypeStruct((t // tm, nh * V_ROWS, tm), BF16),
                   jax.ShapeDtypeStruct((t, c_w), BF16)],
        scratch_shapes=[pltpu.VMEM((tm + SUBLANES, c_w), F32), pltpu.VMEM((SUBLANES, c_w), F32),
                        pltpu.VMEM((tm, c_w), F32), pltpu.VMEM((tm, c_w), F32)],
        compiler_params=_cparams(("arbitrary",)),
        name="in_proj",
    )(_alibi_slopes(), xf, norm_w.reshape(1, d), w_in_bf, conv_w, conv_b.reshape(1, c_w), w_gates_bf,
      b_gates.reshape(1, 2 * c_w), lru_lambda.reshape(1, c_w))


def _attn_kernel(lam_ref, hw_ref, q_ref, k_ref, vt, o_ref, sb, mx, acc, *, tq, n_heads, lambda_init):
    qi = pl.program_id(2)
    n_maps = 2 * n_heads
    mx[...] = jnp.full(mx.shape, NEG_BIG, F32)
    acc[...] = jnp.zeros(acc.shape, F32)

    def values(c, n, lanes=slice(None)):
        return vt[c, (n // 2) * V_ROWS:(n // 2 + 1) * V_ROWS, lanes]

    def scores(c, slot):
        rows = pl.ds(pl.multiple_of(c * tq, tq), tq)
        for n in range(n_maps):
            sb[n, slot] = jnp.dot(k_ref[0, rows, n * HEAD_DIM:(n + 1) * HEAD_DIM],
                                  q_ref[0, n * HEAD_DIM:(n + 1) * HEAD_DIM, :],
                                  preferred_element_type=F32)

    def softmax_pv(c, slot):
        for n in range(n_maps):
            s = sb[n, slot]
            m_prev = mx[n]
            m_new = jnp.maximum(m_prev, jnp.max(s, axis=0, keepdims=True))
            p = jnp.exp2(s - m_new).astype(BF16)
            acc[n] = jnp.exp2(m_prev - m_new) * acc[n] + jnp.dot(values(c, n), p, preferred_element_type=F32)
            mx[n] = m_new

    def softmax_pv_diagonal(c, slot):
        hq = tq // 2
        keep_t = (lax.broadcasted_iota(jnp.int32, (hq, tq), 0) <= lax.broadcasted_iota(jnp.int32, (hq, tq), 1))
        keep_b = (lax.broadcasted_iota(jnp.int32, (hq, hq), 0) <= lax.broadcasted_iota(jnp.int32, (hq, hq), 1))
        for n in range(n_maps):
            top = jnp.where(keep_t, sb[n, slot, :hq, :], NEG_BIG)
            bot = jnp.where(keep_b, sb[n, slot, hq:, hq:], NEG_BIG)
            mt = jnp.max(top, axis=0, keepdims=True)
            mb = jnp.max(bot, axis=0, keepdims=True)
            m_prev = mx[n]
            m_new = jnp.maximum(m_prev, jnp.concatenate([mt[:, :hq], jnp.maximum(mt[:, hq:], mb)], axis=1))
            p_top = jnp.exp2(top - m_new).astype(BF16)
            p_bot = jnp.exp2(bot - m_new[:, hq:]).astype(BF16)
            acc[n] = (jnp.exp2(m_prev - m_new) * acc[n]
                      + jnp.dot(values(c, n, slice(0, hq)), p_top, preferred_element_type=F32))
            acc[n, :, hq:] += jnp.dot(values(c, n, slice(hq, tq)), p_bot, preferred_element_type=F32)
            mx[n] = m_new

    scores(0, 0)

    def body(j, c):
        scores(2 * j + 1, 1)
        softmax_pv(2 * j, 0)
        scores(2 * j + 2, 0)
        softmax_pv(2 * j + 1, 1)
        return c

    lax.fori_loop(0, qi // 2, body, 0)

    @pl.when(qi % 2 == 0)
    def _():
        softmax_pv_diagonal(qi, 0)

    @pl.when(qi % 2 == 1)
    def _():
        scores(qi, 1)
        softmax_pv(qi - 1, 0)
        softmax_pv_diagonal(qi, 1)

    lam = (jnp.exp(jnp.sum(lam_ref[0:1, :] * lam_ref[1:2, :], axis=1, keepdims=True))
           - jnp.exp(jnp.sum(lam_ref[2:3, :] * lam_ref[3:4, :], axis=1, keepdims=True))
           + lambda_init)
    for hh in range(n_heads):
        o1 = acc[2 * hh, :HEAD_DIM, :] * (1.0 / acc[2 * hh, HEAD_DIM:HEAD_DIM + 1, :])
        o2 = acc[2 * hh + 1, :HEAD_DIM, :] * (1.0 / acc[2 * hh + 1, HEAD_DIM:HEAD_DIM + 1, :])
        o = o1 - lam * o2
        o = o * lax.rsqrt(jnp.mean(o * o, axis=0, keepdims=True) + HEAD_NORM_EPS)
        o_ref[0, :, hh * HEAD_DIM:(hh + 1) * HEAD_DIM] = (
            o.T * hw_ref[...] * (1.0 - lambda_init)).astype(o_ref.dtype)


def _diff_attention(qt, ka, vt, lam_params, head_norm_w, lambda_init, b, s):
    nh = N_ATT_HEADS
    hp = ATT_HEADS_PER_STEP
    tq = qt.shape[2]
    nq = s // tq
    return pl.pallas_call(
        functools.partial(_attn_kernel, tq=tq, n_heads=hp, lambda_init=lambda_init),
        grid=(b, nh // hp, nq),
        in_specs=[pl.BlockSpec((4, QK_DIM), lambda bi, hi, qi: (0, 0)),
                  pl.BlockSpec((1, HEAD_DIM), lambda bi, hi, qi: (0, 0)),
                  pl.BlockSpec((1, hp * 2 * HEAD_DIM, tq), lambda bi, hi, qi: (bi * nq + qi, hi, 0)),
                  pl.BlockSpec((1, s, hp * 2 * HEAD_DIM), lambda bi, hi, qi: (bi, 0, hi)),
                  pl.BlockSpec((nq, hp * V_ROWS, tq), lambda bi, hi, qi: (bi, hi, 0))],
        out_specs=pl.BlockSpec((1, tq, hp * HEAD_DIM), lambda bi, hi, qi: (bi, qi, hi)),
        out_shape=jax.ShapeDtypeStruct((b, s, nh * HEAD_DIM), BF16),
        scratch_shapes=[pltpu.VMEM((2 * hp, 2, tq, tq), F32), pltpu.VMEM((2 * hp, 1, tq), F32),
                        pltpu.VMEM((2 * hp, V_ROWS, tq), F32)],
        compiler_params=_cparams(("parallel", "parallel", "arbitrary"), vmem=56 * 1024 * 1024),
        name="diff_attn",
    )(lam_params, head_norm_w.reshape(1, HEAD_DIM), qt, ka.reshape(b, s, ka.shape[1]), vt)


def _outproj_kernel(att_ref, rnn_ref, x_ref, wo_ref, nw_ref, wrt_ref, brc_ref,
                    x1_ref, hn_ref, route_ref, route_t_ref, n8_ref, *, att_w):
    y = jnp.dot(att_ref[...], wo_ref[:att_w, :], preferred_element_type=F32)
    y = y + jnp.dot(rnn_ref[...], wo_ref[att_w:, :], preferred_element_type=F32)
    x1 = x_ref[...] + y
    x1_ref[...] = x1
    hn = (x1 * lax.rsqrt(jnp.mean(x1 * x1, axis=-1, keepdims=True) + NORM_EPS) * nw_ref[...]).astype(BF16)
    hn_ref[...] = hn
    tm = hn.shape[0]

    lg = lax.dot_general(wrt_ref[...], hn, (((1,), (1,)), ((), ())), preferred_element_type=F32)
    lg = lg[:ROUTE_ROWS] + brc_ref[:ROUTE_ROWS, 0:1]
    rowf = lax.broadcasted_iota(jnp.int32, lg.shape, 0).astype(F32)
    big = float(LANES)
    ninf = -jnp.inf
    is_g = rowf < N_GROUPS
    lgm = jnp.where(is_g, lg, ninf)
    mg = jnp.max(lgm, axis=0, keepdims=True)
    g_sel = jnp.min(jnp.where(lgm == mg, rowf, big), axis=0, keepdims=True)
    pg = 1.0 / jnp.sum(jnp.where(is_g, jnp.exp(lgm - mg), 0.0), axis=0, keepdims=True)
    lo = N_GROUPS + EXPERTS_PER_GROUP * g_sel
    in_grp = (rowf >= lo) & (rowf < lo + EXPERTS_PER_GROUP)
    lem = jnp.where(in_grp, lg, ninf)
    v1 = jnp.max(lem, axis=0, keepdims=True)
    i1 = jnp.min(jnp.where(lem == v1, rowf, big), axis=0, keepdims=True)
    lem2 = jnp.where(rowf == i1, ninf, lem)
    v2 = jnp.max(lem2, axis=0, keepdims=True)
    i2 = jnp.min(jnp.where(lem2 == v2, rowf, big), axis=0, keepdims=True)
    e2 = jnp.exp(v2 - v1)
    den = 1.0 + e2
    g1 = pg / den
    g2 = pg * e2 / den

    oh1 = jnp.where(rowf == i1, 1.0, 0.0)
    oh2 = jnp.where(rowf == i2, 1.0, 0.0)
    oh = oh1 + oh2
    earlier = (lax.broadcasted_iota(jnp.int32, (tm, tm), 0)
               < lax.broadcasted_iota(jnp.int32, (tm, tm), 1)).astype(BF16)
    pref = jnp.dot(oh.astype(BF16), earlier, preferred_element_type=F32)
    cnt = jnp.sum(oh, axis=1, keepdims=True)
    n8 = jnp.floor((cnt + (SUBLANES - 1)) * (1.0 / SUBLANES))
    n8_b = jnp.broadcast_to(n8, (ROUTE_ROWS, LANES))
    before = (lax.broadcasted_iota(jnp.int32, (ROUTE_ROWS, ROUTE_ROWS), 1)
              < lax.broadcasted_iota(jnp.int32, (ROUTE_ROWS, ROUTE_ROWS), 0)).astype(BF16)
    loff8 = jnp.dot(before, n8_b.astype(BF16), preferred_element_type=F32)[:, 0:1]
    pos = SUBLANES * loff8 + pref
    lp1 = jnp.sum(oh1 * pos, axis=0, keepdims=True)
    lp2 = jnp.sum(oh2 * pos, axis=0, keepdims=True)
    route_t = jnp.concatenate([g1, g2, lp1, lp2, jnp.zeros((LANES - 4, tm), F32)], axis=0)
    route_t_ref[0] = route_t[:SUBLANES]
    route_ref[...] = route_t.T
    n8_ref[0] = n8_b


def _out_proj(att, rnn, xf, w_out_bf, norm_w, w_route_t_bf, b_route_col):
    t, d = xf.shape
    att_w = att.shape[1]
    tm = min(ROW_TILE, t)
    row = lambda i: (i, 0)
    fix = lambda i: (0, 0)
    return pl.pallas_call(
        functools.partial(_outproj_kernel, att_w=att_w),
        grid=(t // tm,),
        in_specs=[pl.BlockSpec((tm, att_w), row), pl.BlockSpec((tm, rnn.shape[1]), row),
                  pl.BlockSpec((tm, d), row), pl.BlockSpec(w_out_bf.shape, fix),
                  pl.BlockSpec((1, d), fix), pl.BlockSpec((LANES, d), fix), pl.BlockSpec((LANES, 1), fix)],
        out_specs=[pl.BlockSpec((tm, d), row), pl.BlockSpec((tm, d), row), pl.BlockSpec((tm, LANES), row),
                   pl.BlockSpec((1, SUBLANES, tm), lambda i: (i, 0, 0)),
                   pl.BlockSpec((1, ROUTE_ROWS, LANES), lambda i: (i, 0, 0))],
        out_shape=[jax.ShapeDtypeStruct((t, d), F32), jax.ShapeDtypeStruct((t, d), BF16),
                   jax.ShapeDtypeStruct((t, LANES), F32),
                   jax.ShapeDtypeStruct((t // tm, SUBLANES, tm), F32),
                   jax.ShapeDtypeStruct((t // tm, ROUTE_ROWS, LANES), F32)],
        compiler_params=_cparams(("parallel",)),
        name="out_proj",
    )(att, rnn, xf, w_out_bf, norm_w.reshape(1, d), w_route_t_bf, b_route_col)


def _local_rows(tm):
    return -(-(TOP_K * tm + N_EXPERTS * (SUBLANES - 1)) // LANES) * LANES


def _segment_tables(n8_tiles, tm_moe, n_tiles):
    n8 = n8_tiles[:, N_GROUPS:N_GROUPS + N_EXPERTS, 0].astype(jnp.int32)
    c8 = n8 * SUBLANES
    loff = jnp.cumsum(c8, axis=1) - c8
    gtot = jnp.sum(c8, axis=0)
    gpad = (gtot + tm_moe - 1) // tm_moe * tm_moe
    gend = jnp.cumsum(gpad)
    gstart = gend - gpad
    gbase = gstart[None, :] + jnp.cumsum(c8, axis=0) - c8
    tile_row0 = jnp.arange(n_tiles, dtype=jnp.int32) * tm_moe
    tile_e = jnp.minimum(jnp.sum((gend[None, :] <= tile_row0[:, None]).astype(jnp.int32), axis=1),
                         N_EXPERTS - 1).astype(jnp.int32)
    n_used = (gend[-1] // tm_moe).astype(jnp.int32).reshape(1)
    tail_start = (gstart + gtot).astype(jnp.int32)
    tail_n8 = ((gpad - gtot) // SUBLANES).astype(jnp.int32)
    after = gend[tile_e] // tm_moe
    next_e = jnp.where(after < n_used[0], tile_e[jnp.minimum(after, n_tiles - 1)], -1).astype(jnp.int32)
    first = jnp.concatenate([jnp.ones((1,), jnp.int32), (tile_e[1:] != tile_e[:-1]).astype(jnp.int32)])
    w_slot = ((jnp.cumsum(first) - 1) % 2).astype(jnp.int32)
    rows_in_tile = jnp.clip((gstart + gtot)[tile_e] - tile_row0, 1, tm_moe)
    tile_sub = ((rows_in_tile + MOE_SUBTILE - 1) // MOE_SUBTILE).astype(jnp.int32)
    return (n8.reshape(-1), loff.reshape(-1).astype(jnp.int32), gbase.reshape(-1).astype(jnp.int32),
            tile_e, n_used, tail_start, tail_n8, next_e, w_slot, tile_sub)


def _segment_copies(n8_ref, src_off_ref, dst_off_ref, src, dst, sem, tile, wait):
    def rows_of(e):
        return pl.multiple_of(n8_ref[tile * N_EXPERTS + e] * SUBLANES, SUBLANES)

    if wait:
        total = lax.fori_loop(0, N_EXPERTS, lambda e, acc: acc + rows_of(e), 0)
        total = pl.multiple_of(total, SUBLANES)
        pltpu.make_async_copy(src.at[pl.ds(0, total), :], dst.at[pl.ds(0, total), :], sem).wait()
        return

    def per_expert(e, c):
        k = tile * N_EXPERTS + e
        rows = rows_of(e)

        @pl.when(rows > 0)
        def _():
            pltpu.make_async_copy(
                src.at[pl.ds(pl.multiple_of(src_off_ref[k], SUBLANES), rows), :],
                dst.at[pl.ds(pl.multiple_of(dst_off_ref[k], SUBLANES), rows), :], sem).start()
        return c
    lax.fori_loop(0, N_EXPERTS, per_expert, 0)


def _pack_bf16_pairs(x):
    n = x.shape[1] // 2
    bits = lax.bitcast_convert_type(x, jnp.uint32)
    return (bits[:, :n] >> 16) | (bits[:, n:] & jnp.uint32(0xFFFF0000))


def _unpack_bf16_pairs(p):
    lo = lax.bitcast_convert_type(p << 16, F32)
    hi = lax.bitcast_convert_type(p & jnp.uint32(0xFFFF0000), F32)
    return jnp.concatenate([lo, hi], axis=1).astype(BF16)


def _dispatch_kernel(n8_ref, loff_ref, gbase_ref, tstart_ref, tn8_ref, nu_ref, hn_ref, route_ref, xs_hbm,
                     stage, zbuf, sem, zsem, *, lcap, n_tt):
    i = pl.program_id(0)
    slot = i % 2
    tm = hn_ref.shape[0]
    tm_moe = zbuf.shape[0]
    n_tiles = xs_hbm.shape[0] // tm_moe

    def tail_copies(wait):
        def go(cp):
            if wait:
                cp.wait()
            else:
                cp.start()

        def per_expert(e, c):
            rows = pl.multiple_of(tn8_ref[e] * SUBLANES, SUBLANES)

            @pl.when(rows > 0)
            def _():
                go(pltpu.make_async_copy(
                    zbuf.at[pl.ds(0, rows), :],
                    xs_hbm.at[pl.ds(pl.multiple_of(tstart_ref[e], SUBLANES), rows), :], zsem.at[0]))
            return c
        lax.fori_loop(0, N_EXPERTS, per_expert, 0)

        def per_unused_tile(j, c):
            go(pltpu.make_async_copy(zbuf, xs_hbm.at[pl.ds(pl.multiple_of(j * tm_moe, tm_moe), tm_moe), :],
                                     zsem.at[0]))
            return c
        lax.fori_loop(nu_ref[0], n_tiles, per_unused_tile, 0)

    @pl.when(i == 0)
    def _():
        zbuf[...] = jnp.zeros(zbuf.shape, zbuf.dtype)
        tail_copies(False)

    @pl.when(i >= 2)
    def _():
        _segment_copies(n8_ref, loff_ref, gbase_ref, stage.at[slot], xs_hbm, sem.at[slot], i - 2, True)

    lp1 = route_ref[0, 2:3, :]
    lp2 = route_ref[0, 3:4, :]
    rpos = lax.broadcasted_iota(jnp.int32, (lcap, tm), 0).astype(F32)
    sel = jnp.where((rpos == lp1) | (rpos == lp2), 1.0, 0.0).astype(BF16)
    stage[slot] = _pack_bf16_pairs(jnp.dot(sel, hn_ref[...], preferred_element_type=F32))
    _segment_copies(n8_ref, loff_ref, gbase_ref, stage.at[slot], xs_hbm, sem.at[slot], i, False)

    @pl.when(i == n_tt - 1)
    def _():
        _segment_copies(n8_ref, loff_ref, gbase_ref, stage.at[slot], xs_hbm, sem.at[slot], i, True)
        if n_tt > 1:
            _segment_copies(n8_ref, loff_ref, gbase_ref, stage.at[1 - slot], xs_hbm, sem.at[1 - slot],
                            i - 1, True)
        tail_copies(True)


def _dispatch(hn, route, tables, n_rows, tm_moe):
    t, d = hn.shape
    tm = min(ROW_TILE, t)
    n_tt = t // tm
    lcap = _local_rows(tm)
    n8, loff, gbase, _, n_used, tail_start, tail_n8 = tables[:7]
    grid_spec = pltpu.PrefetchScalarGridSpec(
        num_scalar_prefetch=6,
        grid=(n_tt,),
        in_specs=[pl.BlockSpec((tm, d), lambda i, *_: (i, 0)),
                  pl.BlockSpec((1, SUBLANES, tm), lambda i, *_: (i, 0, 0))],
        out_specs=pl.BlockSpec(memory_space=pl.ANY),
        scratch_shapes=[pltpu.VMEM((2, lcap, d // 2), jnp.uint32), pltpu.VMEM((tm_moe, d // 2), jnp.uint32),
                        pltpu.SemaphoreType.DMA((2,)), pltpu.SemaphoreType.DMA((1,))],
    )
    return pl.pallas_call(
        functools.partial(_dispatch_kernel, lcap=lcap, n_tt=n_tt),
        grid_spec=grid_spec,
        out_shape=jax.ShapeDtypeStruct((n_rows, d // 2), jnp.uint32),
        compiler_params=_cparams(("arbitrary",), has_side_effects=True),
        name="dispatch",
    )(n8, loff, gbase, tail_start, tail_n8, n_used, hn, route)


def _moe_kernel(te_ref, nu_ref, nxt_ref, wslot_ref, nsub_ref, xs_ref, wg_hbm, wu_hbm, wd_hbm, y_ref,
                wgf, wuf, wdf, wgb, wub, wdb, wsem):
    i = pl.program_id(0)

    def weight_copies(e, sl):
        return (pltpu.make_async_copy(wg_hbm.at[e], wgf.at[sl], wsem.at[sl, 0]),
                pltpu.make_async_copy(wu_hbm.at[e], wuf.at[sl], wsem.at[sl, 1]),
                pltpu.make_async_copy(wd_hbm.at[e], wdf.at[sl], wsem.at[sl, 2]))

    @pl.when(i == 0)
    def _():
        for cp in weight_copies(te_ref[0], wslot_ref[0]):
            cp.start()

    @pl.when(i < nu_ref[0])
    def _():
        changed = jnp.logical_or(i == 0, te_ref[i] != te_ref[jnp.maximum(i - 1, 0)])

        @pl.when(changed)
        def _():
            sl = wslot_ref[i]
            for cp in weight_copies(te_ref[i], sl):
                cp.wait()
            wgb[...] = wgf[sl].astype(BF16)
            wub[...] = wuf[sl].astype(BF16)
            wdb[...] = wdf[sl].astype(BF16)

            @pl.when(nxt_ref[i] >= 0)
            def _():
                for cp in weight_copies(nxt_ref[i], 1 - sl):
                    cp.start()

        tm = xs_ref.shape[0]
        n_sub = tm // MOE_SUBTILE
        filled = nsub_ref[i]
        for k in range(1, n_sub + 1):
            @pl.when(filled == k)
            def _(k=k):
                r = k * MOE_SUBTILE
                x = _unpack_bf16_pairs(xs_ref[:r, :])
                g = jnp.dot(x, wgb[...], preferred_element_type=F32)
                u = jnp.dot(x, wub[...], preferred_element_type=F32)
                hdn = (g * jax.nn.sigmoid(g) * u).astype(BF16)
                y = jnp.dot(hdn, wdb[...], preferred_element_type=F32)
                y_ref[:r, :] = _pack_bf16_pairs(y.astype(BF16).astype(F32))
                if r < tm:
                    y_ref[r:, :] = jnp.zeros((tm - r, y_ref.shape[1]), y_ref.dtype)

    @pl.when(i >= nu_ref[0])
    def _():
        y_ref[...] = jnp.zeros(y_ref.shape, y_ref.dtype)


def _moe(xs, tile_e, n_used, next_e, w_slot, tile_sub, w_g, w_u, w_d, tm):
    n_rows = xs.shape[0]
    d = w_g.shape[1]
    dp = xs.shape[1]
    n_tiles = n_rows // tm
    ff = w_g.shape[2]
    row_blk = lambda i, te, nu, *_: (jnp.minimum(i, nu[0] - 1), 0)
    hbm = pl.BlockSpec(memory_space=pl.ANY)
    grid_spec = pltpu.PrefetchScalarGridSpec(
        num_scalar_prefetch=5,
        grid=(n_tiles,),
        in_specs=[pl.BlockSpec((tm, dp), row_blk), hbm, hbm, hbm],
        out_specs=pl.BlockSpec((tm, dp), lambda i, *_: (i, 0)),
        scratch_shapes=[pltpu.VMEM((2, d, ff), F32), pltpu.VMEM((2, d, ff), F32), pltpu.VMEM((2, ff, d), F32),
                        pltpu.VMEM((d, ff), BF16), pltpu.VMEM((d, ff), BF16), pltpu.VMEM((ff, d), BF16),
                        pltpu.SemaphoreType.DMA((2, 3))],
    )
    return pl.pallas_call(
        _moe_kernel,
        grid_spec=grid_spec,
        out_shape=jax.ShapeDtypeStruct((n_rows, dp), jnp.uint32),
        compiler_params=_cparams(("arbitrary",)),
        name="moe",
    )(tile_e, n_used, next_e, w_slot, tile_sub, xs, w_g, w_u, w_d)


def _combine_kernel(n8_ref, loff_ref, gbase_ref, x1_ref, route_ref, nw_ref, y_hbm, o_ref,
                    ybuf, sem, *, lcap, n_tt):
    i = pl.program_id(0)
    slot = i % 2
    tm = x1_ref.shape[0]

    def fetch(tile, sl, wait):
        _segment_copies(n8_ref, gbase_ref, loff_ref, y_hbm, ybuf.at[sl], sem.at[sl], tile, wait)

    @pl.when(i == 0)
    def _():
        ybuf[...] = jnp.zeros(ybuf.shape, ybuf.dtype)
        fetch(0, 0, False)

    @pl.when(i + 1 < n_tt)
    def _():
        fetch(i + 1, 1 - slot, False)

    fetch(i, slot, True)
    yb = _unpack_bf16_pairs(ybuf[slot])
    half = tm // 2
    for r0 in (0, half):
        rows = slice(r0, r0 + half)
        g1 = route_ref[rows, 0:1]
        g2 = route_ref[rows, 1:2]
        lp1 = route_ref[rows, 2:3]
        lp2 = route_ref[rows, 3:4]
        cpos = lax.broadcasted_iota(jnp.int32, (half, lcap), 1).astype(F32)
        gsel = (jnp.where(cpos == lp1, g1, 0.0) + jnp.where(cpos == lp2, g2, 0.0)).astype(BF16)
        x = x1_ref[rows, :] + jnp.dot(gsel, yb, preferred_element_type=F32)
        o_ref[rows, :] = x * lax.rsqrt(jnp.mean(x * x, axis=-1, keepdims=True) + NORM_EPS) * nw_ref[...]


def _combine(x1, y, route, norm_w, tables):
    t, d = x1.shape
    tm = min(ROW_TILE, t)
    n_tt = t // tm
    lcap = _local_rows(tm)
    n8, loff, gbase = tables[:3]
    grid_spec = pltpu.PrefetchScalarGridSpec(
        num_scalar_prefetch=3,
        grid=(n_tt,),
        in_specs=[pl.BlockSpec((tm, d), lambda i, *_: (i, 0)),
                  pl.BlockSpec((tm, LANES), lambda i, *_: (i, 0)),
                  pl.BlockSpec((1, d), lambda i, *_: (0, 0)),
                  pl.BlockSpec(memory_space=pl.ANY)],
        out_specs=pl.BlockSpec((tm, d), lambda i, *_: (i, 0)),
        scratch_shapes=[pltpu.VMEM((2, lcap, d // 2), jnp.uint32), pltpu.SemaphoreType.DMA((2,))],
    )
    return pl.pallas_call(
        functools.partial(_combine_kernel, lcap=lcap, n_tt=n_tt),
        grid_spec=grid_spec,
        out_shape=jax.ShapeDtypeStruct((t, d), F32),
        compiler_params=_cparams(("arbitrary",)),
        name="combine",
    )(n8, loff, gbase, x1, route, norm_w.reshape(1, d), y)


def _block_diag(w):
    n, bi, bj = w.shape
    eye = jnp.eye(n, dtype=w.dtype)
    return jnp.einsum('nij,nm->nimj', w, eye).reshape(n * bi, n * bj)


def kernel(x, mix_norm_w, w_in, lambda_q1, lambda_k1, lambda_q2, lambda_k2, head_norm_w, conv_w, conv_b, w_rgate, b_rgate, w_igate, b_igate, lru_lambda, w_out, ffn_norm_w, w_router_group, b_router_group, w_router_expert, b_router_expert, w_exp_gate, w_exp_up, w_exp_down, final_norm_w):
    b, s, d = x.shape
    t = b * s
    assert w_in.shape[0] == 1, "single-layer stack only"
    att_w = N_ATT_HEADS * HEAD_DIM
    tm_moe = MOE_TILE
    xf = x.reshape(t, d)
    for l in range(1):
        lambda_init = 0.8 - 0.6 * math.exp(-0.3 * l)
        assert s % ATT_TILE == 0, "sequence length must be a multiple of the attention tile"
        w_bd = jnp.concatenate([_block_diag(w_rgate[l]), _block_diag(w_igate[l])], axis=1).astype(BF16)
        b_cat = jnp.concatenate([b_rgate[l], b_igate[l]])
        qt, ka, vt, rnn = _in_proj(xf, mix_norm_w[l], w_in[l].astype(BF16), att_w, s,
                                   conv_w[l], conv_b[l], w_bd, b_cat, lru_lambda[l])
        lam_params = jnp.stack([lambda_q1[l], lambda_k1[l], lambda_q2[l], lambda_k2[l]]).astype(F32)
        att = _diff_attention(qt, ka, vt, lam_params, head_norm_w[l], lambda_init, b, s)
        w_route = jnp.concatenate([w_router_group[l], w_router_expert[l]], axis=1).T
        w_route = jnp.pad(w_route, ((0, LANES - w_route.shape[0]), (0, 0))).astype(BF16)
        b_route = jnp.concatenate([b_router_group[l], b_router_expert[l]])
        b_route = jnp.pad(b_route, (0, LANES - b_route.shape[0])).reshape(LANES, 1).astype(F32)
        x1, hn, route, route_t, n8_tiles = _out_proj(att.reshape(t, att_w), rnn.reshape(t, -1), xf,
                                                     w_out[l].astype(BF16), ffn_norm_w[l], w_route, b_route)
        n_tt = n8_tiles.shape[0]
        max_rows = TOP_K * t + n_tt * N_EXPERTS * (SUBLANES - 1) + N_EXPERTS * (tm_moe - 1)
        n_tiles = -(-max_rows // tm_moe)
        tables = _segment_tables(n8_tiles, tm_moe, n_tiles)
        xs = _dispatch(hn, route_t, tables, n_tiles * tm_moe, tm_moe)
        y = _moe(xs, tables[3], tables[4], tables[7], tables[8], tables[9],
                 w_exp_gate[l], w_exp_up[l], w_exp_down[l], tm_moe)
        out = _combine(x1, y, route, final_norm_w, tables)
    return out.reshape(b, s, d)
```

```python
import functools
import math

import numpy as np
import jax
import jax.numpy as jnp
from jax import lax
from jax.experimental import pallas as pl
from jax.experimental.pallas import tpu as pltpu

F32 = jnp.float32
BF16 = jnp.bfloat16

N_ATT_HEADS = 4
HEAD_DIM = 128
QK_DIM = 64
N_RNN_BLOCKS = 8
CONV_WIDTH = 4
LRU_C = 8.0
N_GROUPS = 4
EXPERTS_PER_GROUP = 8
N_EXPERTS = N_GROUPS * EXPERTS_PER_GROUP
TOP_K = 2
NORM_EPS = 1e-6
HEAD_NORM_EPS = 1e-5
LOG2E = math.log2(math.e)


def _bf16_terms(x, n):
    terms = []
    for _ in range(n):
        t = float(np.float32(x).astype(jnp.bfloat16))
        terms.append(t)
        x -= t
    return tuple(terms)


LOG2E_TERMS = _bf16_terms(LOG2E, 3)
LANES = 128
SUBLANES = 8
SEG_ROWS = SUBLANES
NEG_BIG = -1e30

ROW_TILE = 512
ROUTE_ROWS = 48
ATT_TILE = 512
ATT_HEADS_PER_STEP = 4
V_ROWS = HEAD_DIM + 16
LRU_CHUNK = 128
MOE_TILE = 512
MOE_SUBTILE = 128
VMEM_LIMIT = 48 * 1024 * 1024


def _cparams(sem, vmem=VMEM_LIMIT, **kw):
    return pltpu.CompilerParams(dimension_semantics=sem, vmem_limit_bytes=vmem, **kw)


def _inproj_kernel(slope_ref, x_ref, nw_ref, w_ref, cw_ref, cb_ref, wg_ref, bg_ref, lam_ref,
                   qt_ref, ka_ref, vt_ref, rnn_ref, xs, carry_h, a_s, u_s, *, att_w, s_len, ch):
    i = pl.program_id(0)
    x = x_ref[...]
    tm = x.shape[0]
    c_w = (w_ref.shape[1] - 3 * att_w) // 2

    @pl.when((i * tm) % s_len == 0)
    def _():
        xs[0:SUBLANES, :] = jnp.zeros((SUBLANES, c_w), F32)
        carry_h[...] = jnp.zeros(carry_h.shape, F32)

    ms = jnp.mean(x * x, axis=-1, keepdims=True)
    hn = (x * lax.rsqrt(ms + NORM_EPS) * nw_ref[...]).astype(BF16)
    p_lru = jnp.dot(hn, w_ref[:, 3 * att_w:], preferred_element_type=F32)

    xs[SUBLANES:, :] = p_lru[:, :c_w]
    neg_lam = -lam_ref[...]
    sp = jnp.maximum(neg_lam, 0.0) + jnp.log1p(jnp.exp(-jnp.abs(neg_lam)))
    cw = cw_ref[...]
    cb = cb_ref[...]
    bias = bg_ref[...]
    r8 = lax.broadcasted_iota(jnp.int32, (ch // SUBLANES, SUBLANES, c_w), 1)
    xcs, zs = [], []
    for c in range(tm // ch):
        r0 = c * ch
        win = xs[r0:r0 + ch + SUBLANES, :]
        xc = cw[3:4, :] * win[SUBLANES:] + cb
        for k in (1, 2, 3):
            xc = xc + cw[3 - k:4 - k, :] * pltpu.roll(win, k, axis=0)[SUBLANES:]
        xcs.append(xc)
        zs.append(jnp.dot(xc.astype(BF16), wg_ref[...], preferred_element_type=F32) + bias)
    xs[0:SUBLANES, :] = xs[tm:tm + SUBLANES, :]

    p_q = jnp.dot(hn, w_ref[:, :att_w], preferred_element_type=F32)
    p_v = jnp.dot(hn, w_ref[:, 2 * att_w:3 * att_w], preferred_element_type=F32)
    p_k = jnp.dot(hn, w_ref[:, att_w:2 * att_w], preferred_element_type=F32)

    for c in range(tm // ch):
        r0 = c * ch
        xc, z = xcs[c], zs[c]
        r = _sigmoid(z[:, :c_w])
        ig = _sigmoid(z[:, c_w:])
        log_a = (-LRU_C) * r * sp
        a = jnp.exp(log_a)
        w = jnp.tanh(-log_a) * (1.0 + a * a)
        u = jnp.where(w > 0.0, w * lax.rsqrt(w), 0.0) * ig * xc
        a = a.reshape(ch // SUBLANES, SUBLANES, c_w)
        u = u.reshape(ch // SUBLANES, SUBLANES, c_w)
        for k in (1, 2, 4):
            a_sh = pltpu.roll(a, k, axis=1)
            u_sh = pltpu.roll(u, k, axis=1)
            ok = r8 >= k
            u = jnp.where(ok, u + a * u_sh, u)
            a = jnp.where(ok, a * a_sh, a)
        a_s[r0:r0 + ch, :] = a.reshape(ch, c_w)
        u_s[r0:r0 + ch, :] = u.reshape(ch, c_w)

    qt = (p_q * (QK_DIM ** -0.5 * LOG2E)).T
    arow = lax.broadcasted_iota(jnp.int32, (QK_DIM, tm), 0)
    ones2 = jnp.where(arow < 2, LOG2E_TERMS[0], jnp.where(arow < 4, LOG2E_TERMS[1],
                                                          jnp.where(arow < 6, LOG2E_TERMS[2], 0.0)))
    pieces = []
    for g in range(2 * N_ATT_HEADS):
        pieces += [qt[g * QK_DIM:(g + 1) * QK_DIM], ones2]
    qt_ref[0] = jnp.concatenate(pieces, axis=0).astype(BF16)

    lane = lax.broadcasted_iota(jnp.int32, (tm, HEAD_DIM), 1)
    j = (i * tm) % s_len + lax.broadcasted_iota(jnp.int32, (tm, HEAD_DIM), 0)
    j_lo = (j & 255).astype(F32)
    j_hi = (j - (j & 255)).astype(F32)
    vtt = p_v.T
    ones_rows = jnp.where(lax.broadcasted_iota(jnp.int32, (V_ROWS - HEAD_DIM, tm), 0) == 0, 1.0, 0.0)
    for h in range(N_ATT_HEADS):
        slope = slope_ref[h]
        kk = p_k[:, h * HEAD_DIM:(h + 1) * HEAD_DIM]
        in_aug = (lane >= QK_DIM) & (lane < QK_DIM + 2 * len(LOG2E_TERMS))
        aug = jnp.where(in_aug, jnp.where((lane & 1) == 0, slope * j_hi, slope * j_lo), 0.0)
        ka_ref[:, 2 * h * HEAD_DIM:(2 * h + 1) * HEAD_DIM] = jnp.where(lane < QK_DIM, kk, aug).astype(BF16)
        ka_ref[:, (2 * h + 1) * HEAD_DIM:(2 * h + 2) * HEAD_DIM] = jnp.where(
            lane < QK_DIM, pltpu.roll(kk, QK_DIM, axis=1), aug).astype(BF16)
        vt_ref[0, h * V_ROWS:h * V_ROWS + HEAD_DIM, :] = vtt[h * HEAD_DIM:(h + 1) * HEAD_DIM].astype(BF16)
        vt_ref[0, h * V_ROWS + HEAD_DIM:(h + 1) * V_ROWS, :] = ones_rows.astype(BF16)

    hprev = carry_h[0:1, :]
    for r0 in range(0, tm, SUBLANES):
        hg = u_s[r0:r0 + SUBLANES, :] + a_s[r0:r0 + SUBLANES, :] * hprev
        u_s[r0:r0 + SUBLANES, :] = hg
        hprev = hg[SUBLANES - 1:SUBLANES, :]
    carry_h[0:1, :] = hprev
    rnn_ref[...] = (u_s[...] * _gelu_tanh(p_lru[:, c_w:])).astype(rnn_ref.dtype)


def _gelu_tanh(x):
    k1 = -2.0 * math.sqrt(2.0 / math.pi) * LOG2E
    return x / (1.0 + jnp.exp2(x * (k1 + (k1 * 0.044715) * (x * x))))


def _sigmoid(x):
    return 0.5 * jnp.tanh(0.5 * x) + 0.5


def _alibi_slopes():
    nh = N_ATT_HEADS
    return jnp.asarray(np.array([2.0 ** (-8.0 * (i + 1) / nh) for i in range(nh)], dtype=np.float32))


def _in_proj(xf, norm_w, w_in_bf, att_w, s_len, conv_w, conv_b, w_gates_bf, b_gates, lru_lambda):
    t, d = xf.shape
    n = w_in_bf.shape[1]
    tm = min(ATT_TILE, t)
    ch = min(LRU_CHUNK, tm)
    nh = N_ATT_HEADS
    c_w = (n - 3 * att_w) // 2
    fix = lambda i: (0, 0)
    return pl.pallas_call(
        functools.partial(_inproj_kernel, att_w=att_w, s_len=s_len, ch=ch),
        grid=(t // tm,),
        in_specs=[pl.BlockSpec(memory_space=pltpu.SMEM),
                  pl.BlockSpec((tm, d), lambda i: (i, 0)),
                  pl.BlockSpec((1, d), fix),
                  pl.BlockSpec((d, n), fix),
                  pl.BlockSpec((CONV_WIDTH, c_w), fix),
                  pl.BlockSpec((1, c_w), fix),
                  pl.BlockSpec((c_w, 2 * c_w), fix),
                  pl.BlockSpec((1, 2 * c_w), fix),
                  pl.BlockSpec((1, c_w), fix)],
        out_specs=[pl.BlockSpec((1, 2 * att_w, tm), lambda i: (i, 0, 0)),
                   pl.BlockSpec((tm, 2 * att_w), lambda i: (i, 0)),
                   pl.BlockSpec((1, nh * V_ROWS, tm), lambda i: (i, 0, 0)),
                   pl.BlockSpec((tm, c_w), lambda i: (i, 0))],
        out_shape=[jax.ShapeDtypeStruct((t // tm, 2 * att_w, tm), BF16),
                   jax.ShapeDtypeStruct((t, 2 * att_w), BF16),
                   jax.ShapeDtypeStruct((t // tm, nh * V_ROWS, tm), BF16),
                   jax.ShapeDtypeStruct((t, c_w), BF16)],
        scratch_shapes=[pltpu.VMEM((tm + SUBLANES, c_w), F32), pltpu.VMEM((SUBLANES, c_w), F32),
                        pltpu.VMEM((tm, c_w), F32), pltpu.VMEM((tm, c_w), F32)],
        compiler_params=_cparams(("arbitrary",)),
        name="in_proj",
    )(_alibi_slopes(), xf, norm_w.reshape(1, d), w_in_bf, conv_w, conv_b.reshape(1, c_w), w_gates_bf,
      b_gates.reshape(1, 2 * c_w), lru_lambda.reshape(1, c_w))


def _attn_kernel(lam_ref, hw_ref, q_ref, k_ref, vt, o_ref, sb, mx, acc, *, tq, n_heads, lambda_init):
    qi = pl.program_id(2)
    n_maps = 2 * n_heads
    mx[...] = jnp.full(mx.shape, NEG_BIG, F32)
    acc[...] = jnp.zeros(acc.shape, F32)

    def values(c, n, lanes=slice(None)):
        return vt[c, (n // 2) * V_ROWS:(n // 2 + 1) * V_ROWS, lanes]

    def scores(c, slot):
        rows = pl.ds(pl.multiple_of(c * tq, tq), tq)
        for n in range(n_maps):
            sb[n, slot] = jnp.dot(k_ref[0, rows, n * HEAD_DIM:(n + 1) * HEAD_DIM],
                                  q_ref[0, n * HEAD_DIM:(n + 1) * HEAD_DIM, :],
                                  preferred_element_type=F32)

    def softmax_pv(c, slot):
        for n in range(n_maps):
            s = sb[n, slot]
            m_prev = mx[n]
            m_new = jnp.maximum(m_prev, jnp.max(s, axis=0, keepdims=True))
            p = jnp.exp2(s - m_new).astype(BF16)
            acc[n] = jnp.exp2(m_prev - m_new) * acc[n] + jnp.dot(values(c, n), p, preferred_element_type=F32)
            mx[n] = m_new

    def softmax_pv_diagonal(c, slot):
        hq = tq // 2
        keep_t = (lax.broadcasted_iota(jnp.int32, (hq, tq), 0) <= lax.broadcasted_iota(jnp.int32, (hq, tq), 1))
        keep_b = (lax.broadcasted_iota(jnp.int32, (hq, hq), 0) <= lax.broadcasted_iota(jnp.int32, (hq, hq), 1))
        for n in range(n_maps):
            top = jnp.where(keep_t, sb[n, slot, :hq, :], NEG_BIG)
            bot = jnp.where(keep_b, sb[n, slot, hq:, hq:], NEG_BIG)
            mt = jnp.max(top, axis=0, keepdims=True)
            mb = jnp.max(bot, axis=0, keepdims=True)
            m_prev = mx[n]
            m_new = jnp.maximum(m_prev, jnp.concatenate([mt[:, :hq], jnp.maximum(mt[:, hq:], mb)], axis=1))
            p_top = jnp.exp2(top - m_new).astype(BF16)
            p_bot = jnp.exp2(bot - m_new[:, hq:]).astype(BF16)
            acc[n] = (jnp.exp2(m_prev - m_new) * acc[n]
                      + jnp.dot(values(c, n, slice(0, hq)), p_top, preferred_element_type=F32))
            acc[n, :, hq:] += jnp.dot(values(c, n, slice(hq, tq)), p_bot, preferred_element_type=F32)
            mx[n] = m_new

    scores(0, 0)

    def body(j, c):
        scores(2 * j + 1, 1)
        softmax_pv(2 * j, 0)
        scores(2 * j + 2, 0)
        softmax_pv(2 * j + 1, 1)
        return c

    lax.fori_loop(0, qi // 2, body, 0)

    @pl.when(qi % 2 == 0)
    def _():
        softmax_pv_diagonal(qi, 0)

    @pl.when(qi % 2 == 1)
    def _():
        scores(qi, 1)
        softmax_pv(qi - 1, 0)
        softmax_pv_diagonal(qi, 1)

    lam = (jnp.exp(jnp.sum(lam_ref[0:1, :] * lam_ref[1:2, :], axis=1, keepdims=True))
           - jnp.exp(jnp.sum(lam_ref[2:3, :] * lam_ref[3:4, :], axis=1, keepdims=True))
           + lambda_init)
    for hh in range(n_heads):
        o1 = acc[2 * hh, :HEAD_DIM, :] * (1.0 / acc[2 * hh, HEAD_DIM:HEAD_DIM + 1, :])
        o2 = acc[2 * hh + 1, :HEAD_DIM, :] * (1.0 / acc[2 * hh + 1, HEAD_DIM:HEAD_DIM + 1, :])
        o = o1 - lam * o2
        o = o * lax.rsqrt(jnp.mean(o * o, axis=0, keepdims=True) + HEAD_NORM_EPS)
        o_ref[0, :, hh * HEAD_DIM:(hh + 1) * HEAD_DIM] = (
            o.T * hw_ref[...] * (1.0 - lambda_init)).astype(o_ref.dtype)


def _diff_attention(qt, ka, vt, lam_params, head_norm_w, lambda_init, b, s):
    nh = N_ATT_HEADS
    hp = ATT_HEADS_PER_STEP
    tq = qt.shape[2]
    nq = s // tq
    return pl.pallas_call(
        functools.partial(_attn_kernel, tq=tq, n_heads=hp, lambda_init=lambda_init),
        grid=(b, nh // hp, nq),
        in_specs=[pl.BlockSpec((4, QK_DIM), lambda bi, hi, qi: (0, 0)),
                  pl.BlockSpec((1, HEAD_DIM), lambda bi, hi, qi: (0, 0)),
                  pl.BlockSpec((1, hp * 2 * HEAD_DIM, tq), lambda bi, hi, qi: (bi * nq + qi, hi, 0)),
                  pl.BlockSpec((1, s, hp * 2 * HEAD_DIM), lambda bi, hi, qi: (bi, 0, hi)),
                  pl.BlockSpec((nq, hp * V_ROWS, tq), lambda bi, hi, qi: (bi, hi, 0))],
        out_specs=pl.BlockSpec((1, tq, hp * HEAD_DIM), lambda bi, hi, qi: (bi, qi, hi)),
        out_shape=jax.ShapeDtypeStruct((b, s, nh * HEAD_DIM), BF16),
        scratch_shapes=[pltpu.VMEM((2 * hp, 2, tq, tq), F32), pltpu.VMEM((2 * hp, 1, tq), F32),
                        pltpu.VMEM((2 * hp, V_ROWS, tq), F32)],
        compiler_params=_cparams(("parallel", "parallel", "arbitrary"), vmem=56 * 1024 * 1024),
        name="diff_attn",
    )(lam_params, head_norm_w.reshape(1, HEAD_DIM), qt, ka.reshape(b, s, ka.shape[1]), vt)


def _outproj_kernel(att_ref, rnn_ref, x_ref, wo_ref, nw_ref, wrt_ref, brc_ref,
                    x1_ref, hn_ref, route_ref, route_t_ref, n8_ref, *, att_w):
    y = jnp.dot(att_ref[...], wo_ref[:att_w, :], preferred_element_type=F32)
    y = y + jnp.dot(rnn_ref[...], wo_ref[att_w:, :], preferred_element_type=F32)
    x1 = x_ref[...] + y
    x1_ref[...] = x1
    hn = (x1 * lax.rsqrt(jnp.mean(x1 * x1, axis=-1, keepdims=True) + NORM_EPS) * nw_ref[...]).astype(BF16)
    hn_ref[...] = hn
    tm = hn.shape[0]

    lg = lax.dot_general(wrt_ref[...], hn, (((1,), (1,)), ((), ())), preferred_element_type=F32)
    lg = lg[:ROUTE_ROWS] + brc_ref[:ROUTE_ROWS, 0:1]
    rowf = lax.broadcasted_iota(jnp.int32, lg.shape, 0).astype(F32)
    big = float(LANES)
    ninf = -jnp.inf
    is_g = rowf < N_GROUPS
    lgm = jnp.where(is_g, lg, ninf)
    mg = jnp.max(lgm, axis=0, keepdims=True)
    g_sel = jnp.min(jnp.where(lgm == mg, rowf, big), axis=0, keepdims=True)
    pg = 1.0 / jnp.sum(jnp.where(is_g, jnp.exp(lgm - mg), 0.0), axis=0, keepdims=True)
    lo = N_GROUPS + EXPERTS_PER_GROUP * g_sel
    in_grp = (rowf >= lo) & (rowf < lo + EXPERTS_PER_GROUP)
    lem = jnp.where(in_grp, lg, ninf)
    v1 = jnp.max(lem, axis=0, keepdims=True)
    i1 = jnp.min(jnp.where(lem == v1, rowf, big), axis=0, keepdims=True)
    lem2 = jnp.where(rowf == i1, ninf, lem)
    v2 = jnp.max(lem2, axis=0, keepdims=True)
    i2 = jnp.min(jnp.where(lem2 == v2, rowf, big), axis=0, keepdims=True)
    e2 = jnp.exp(v2 - v1)
    den = 1.0 + e2
    g1 = pg / den
    g2 = pg * e2 / den

    oh1 = jnp.where(rowf == i1, 1.0, 0.0)
    oh2 = jnp.where(rowf == i2, 1.0, 0.0)
    oh = oh1 + oh2
    earlier = (lax.broadcasted_iota(jnp.int32, (tm, tm), 0)
               < lax.broadcasted_iota(jnp.int32, (tm, tm), 1)).astype(BF16)
    pref = jnp.dot(oh.astype(BF16), earlier, preferred_element_type=F32)
    cnt = jnp.sum(oh, axis=1, keepdims=True)
    n8 = jnp.floor((cnt + (SEG_ROWS - 1)) * (1.0 / SEG_ROWS))
    n8_b = jnp.broadcast_to(n8, (ROUTE_ROWS, LANES))
    before = (lax.broadcasted_iota(jnp.int32, (ROUTE_ROWS, ROUTE_ROWS), 1)
              < lax.broadcasted_iota(jnp.int32, (ROUTE_ROWS, ROUTE_ROWS), 0)).astype(BF16)
    loff8 = jnp.dot(before, n8_b.astype(BF16), preferred_element_type=F32)[:, 0:1]
    pos = SEG_ROWS * loff8 + pref
    lp1 = jnp.sum(oh1 * pos, axis=0, keepdims=True)
    lp2 = jnp.sum(oh2 * pos, axis=0, keepdims=True)
    route_t = jnp.concatenate([g1, g2, lp1, lp2, jnp.zeros((LANES - 4, tm), F32)], axis=0)
    route_t_ref[0] = route_t[:SUBLANES]
    route_ref[...] = route_t.T
    n8_ref[0] = n8_b


def _out_proj(att, rnn, xf, w_out_bf, norm_w, w_route_t_bf, b_route_col):
    t, d = xf.shape
    att_w = att.shape[1]
    tm = min(ROW_TILE, t)
    row = lambda i: (i, 0)
    fix = lambda i: (0, 0)
    return pl.pallas_call(
        functools.partial(_outproj_kernel, att_w=att_w),
        grid=(t // tm,),
        in_specs=[pl.BlockSpec((tm, att_w), row), pl.BlockSpec((tm, rnn.shape[1]), row),
                  pl.BlockSpec((tm, d), row), pl.BlockSpec(w_out_bf.shape, fix),
                  pl.BlockSpec((1, d), fix), pl.BlockSpec((LANES, d), fix), pl.BlockSpec((LANES, 1), fix)],
        out_specs=[pl.BlockSpec((tm, d), row), pl.BlockSpec((tm, d), row), pl.BlockSpec((tm, LANES), row),
                   pl.BlockSpec((1, SUBLANES, tm), lambda i: (i, 0, 0)),
                   pl.BlockSpec((1, ROUTE_ROWS, LANES), lambda i: (i, 0, 0))],
        out_shape=[jax.ShapeDtypeStruct((t, d), F32), jax.ShapeDtypeStruct((t, d), BF16),
                   jax.ShapeDtypeStruct((t, LANES), F32),
                   jax.ShapeDtypeStruct((t // tm, SUBLANES, tm), F32),
                   jax.ShapeDtypeStruct((t // tm, ROUTE_ROWS, LANES), F32)],
        compiler_params=_cparams(("parallel",)),
        name="out_proj",
    )(att, rnn, xf, w_out_bf, norm_w.reshape(1, d), w_route_t_bf, b_route_col)


def _local_rows(tm):
    return -(-(TOP_K * tm + N_EXPERTS * (SEG_ROWS - 1)) // LANES) * LANES


def _segment_tables(n8_tiles, tm_moe, n_tiles):
    n8 = n8_tiles[:, N_GROUPS:N_GROUPS + N_EXPERTS, 0].astype(jnp.int32)
    c8 = n8 * SEG_ROWS
    loff = jnp.cumsum(c8, axis=1) - c8
    gtot = jnp.sum(c8, axis=0)
    gpad = (gtot + tm_moe - 1) // tm_moe * tm_moe
    gend = jnp.cumsum(gpad)
    gstart = gend - gpad
    gbase = gstart[None, :] + jnp.cumsum(c8, axis=0) - c8
    tile_row0 = jnp.arange(n_tiles, dtype=jnp.int32) * tm_moe
    tile_e = jnp.minimum(jnp.sum((gend[None, :] <= tile_row0[:, None]).astype(jnp.int32), axis=1),
                         N_EXPERTS - 1).astype(jnp.int32)
    n_used = (gend[-1] // tm_moe).astype(jnp.int32).reshape(1)
    tail_start = (gstart + gtot).astype(jnp.int32)
    tail_n8 = ((gpad - gtot) // SEG_ROWS).astype(jnp.int32)
    after = gend[tile_e] // tm_moe
    next_e = jnp.where(after < n_used[0], tile_e[jnp.minimum(after, n_tiles - 1)], -1).astype(jnp.int32)
    first = jnp.concatenate([jnp.ones((1,), jnp.int32), (tile_e[1:] != tile_e[:-1]).astype(jnp.int32)])
    w_slot = ((jnp.cumsum(first) - 1) % 2).astype(jnp.int32)
    rows_in_tile = jnp.clip((gstart + gtot)[tile_e] - tile_row0, 1, tm_moe)
    tile_sub = ((rows_in_tile + MOE_SUBTILE - 1) // MOE_SUBTILE).astype(jnp.int32)
    return (n8.reshape(-1), loff.reshape(-1).astype(jnp.int32), gbase.reshape(-1).astype(jnp.int32),
            tile_e, n_used, tail_start, tail_n8, next_e, w_slot, tile_sub)


def _seg_aligned(rows):
    return pl.multiple_of(rows, SEG_ROWS) if SEG_ROWS > 1 else rows


def _segment_copies(n8_ref, src_off_ref, dst_off_ref, src, dst, sem, tile, wait):
    def rows_of(e):
        return _seg_aligned(n8_ref[tile * N_EXPERTS + e] * SEG_ROWS)

    if wait:
        total = lax.fori_loop(0, N_EXPERTS, lambda e, acc: acc + rows_of(e), 0)
        total = _seg_aligned(total)
        pltpu.make_async_copy(src.at[pl.ds(0, total), :], dst.at[pl.ds(0, total), :], sem).wait()
        return

    def per_expert(e, c):
        k = tile * N_EXPERTS + e
        rows = rows_of(e)

        @pl.when(rows > 0)
        def _():
            pltpu.make_async_copy(
                src.at[pl.ds(_seg_aligned(src_off_ref[k]), rows), :],
                dst.at[pl.ds(_seg_aligned(dst_off_ref[k]), rows), :], sem).start()
        return c
    lax.fori_loop(0, N_EXPERTS, per_expert, 0)


def _pack_bf16_pairs(x):
    n = x.shape[1] // 2
    bits = lax.bitcast_convert_type(x, jnp.uint32)
    return (bits[:, :n] >> 16) | (bits[:, n:] & jnp.uint32(0xFFFF0000))


def _unpack_bf16_pairs(p):
    lo = lax.bitcast_convert_type(p << 16, F32)
    hi = lax.bitcast_convert_type(p & jnp.uint32(0xFFFF0000), F32)
    return jnp.concatenate([lo, hi], axis=1).astype(BF16)


def _dispatch_kernel(n8_ref, loff_ref, gbase_ref, tstart_ref, tn8_ref, nu_ref, hn_ref, route_ref, xs_hbm,
                     stage, zbuf, sem, zsem, *, lcap, n_tt):
    i = pl.program_id(0)
    slot = i % 2
    tm = hn_ref.shape[0]
    tm_moe = zbuf.shape[0]
    n_tiles = xs_hbm.shape[0] // tm_moe

    def tail_copies(wait):
        def go(cp):
            if wait:
                cp.wait()
            else:
                cp.start()

        def per_expert(e, c):
            rows = _seg_aligned(tn8_ref[e] * SEG_ROWS)

            @pl.when(rows > 0)
            def _():
                go(pltpu.make_async_copy(
                    zbuf.at[pl.ds(0, rows), :],
                    xs_hbm.at[pl.ds(_seg_aligned(tstart_ref[e]), rows), :], zsem.at[0]))
            return c
        lax.fori_loop(0, N_EXPERTS, per_expert, 0)

        def per_unused_tile(j, c):
            go(pltpu.make_async_copy(zbuf, xs_hbm.at[pl.ds(pl.multiple_of(j * tm_moe, tm_moe), tm_moe), :],
                                     zsem.at[0]))
            return c
        lax.fori_loop(nu_ref[0], n_tiles, per_unused_tile, 0)

    @pl.when(i == 0)
    def _():
        zbuf[...] = jnp.zeros(zbuf.shape, zbuf.dtype)
        tail_copies(False)

    @pl.when(i >= 2)
    def _():
        _segment_copies(n8_ref, loff_ref, gbase_ref, stage.at[slot], xs_hbm, sem.at[slot], i - 2, True)

    lp1 = route_ref[0, 2:3, :]
    lp2 = route_ref[0, 3:4, :]
    rpos = lax.broadcasted_iota(jnp.int32, (lcap, tm), 0).astype(F32)
    sel = jnp.where((rpos == lp1) | (rpos == lp2), 1.0, 0.0).astype(BF16)
    stage[slot] = _pack_bf16_pairs(jnp.dot(sel, hn_ref[...], preferred_element_type=F32))
    _segment_copies(n8_ref, loff_ref, gbase_ref, stage.at[slot], xs_hbm, sem.at[slot], i, False)

    @pl.when(i == n_tt - 1)
    def _():
        _segment_copies(n8_ref, loff_ref, gbase_ref, stage.at[slot], xs_hbm, sem.at[slot], i, True)
        if n_tt > 1:
            _segment_copies(n8_ref, loff_ref, gbase_ref, stage.at[1 - slot], xs_hbm, sem.at[1 - slot],
                            i - 1, True)
        tail_copies(True)


def _dispatch(hn, route, tables, n_rows, tm_moe):
    t, d = hn.shape
    tm = min(ROW_TILE, t)
    n_tt = t // tm
    lcap = _local_rows(tm)
    n8, loff, gbase, _, n_used, tail_start, tail_n8 = tables[:7]
    grid_spec = pltpu.PrefetchScalarGridSpec(
        num_scalar_prefetch=6,
        grid=(n_tt,),
        in_specs=[pl.BlockSpec((tm, d), lambda i, *_: (i, 0)),
                  pl.BlockSpec((1, SUBLANES, tm), lambda i, *_: (i, 0, 0))],
        out_specs=pl.BlockSpec(memory_space=pl.ANY),
        scratch_shapes=[pltpu.VMEM((2, lcap, d // 2), jnp.uint32), pltpu.VMEM((tm_moe, d // 2), jnp.uint32),
                        pltpu.SemaphoreType.DMA((2,)), pltpu.SemaphoreType.DMA((1,))],
    )
    return pl.pallas_call(
        functools.partial(_dispatch_kernel, lcap=lcap, n_tt=n_tt),
        grid_spec=grid_spec,
        out_shape=jax.ShapeDtypeStruct((n_rows, d // 2), jnp.uint32),
        compiler_params=_cparams(("arbitrary",), has_side_effects=True),
        name="dispatch",
    )(n8, loff, gbase, tail_start, tail_n8, n_used, hn, route)


def _moe_kernel(te_ref, nu_ref, nxt_ref, wslot_ref, nsub_ref, xs_ref, wg_hbm, wu_hbm, wd_hbm, y_ref,
                wgf, wuf, wdf, wgb, wub, wdb, wsem):
    i = pl.program_id(0)

    def weight_copies(e, sl):
        return (pltpu.make_async_copy(wg_hbm.at[e], wgf.at[sl], wsem.at[sl, 0]),
                pltpu.make_async_copy(wu_hbm.at[e], wuf.at[sl], wsem.at[sl, 1]),
                pltpu.make_async_copy(wd_hbm.at[e], wdf.at[sl], wsem.at[sl, 2]))

    @pl.when(i == 0)
    def _():
        for cp in weight_copies(te_ref[0], wslot_ref[0]):
            cp.start()

    @pl.when(i < nu_ref[0])
    def _():
        changed = jnp.logical_or(i == 0, te_ref[i] != te_ref[jnp.maximum(i - 1, 0)])

        tm = xs_ref.shape[0]
        n_sub = tm // MOE_SUBTILE
        filled = nsub_ref[i]

        @pl.when(jnp.logical_and(changed, nxt_ref[i] >= 0))
        def _():
            for cp in weight_copies(nxt_ref[i], 1 - wslot_ref[i]):
                cp.start()

        def convert_weights():
            sl = wslot_ref[i]
            for cp in weight_copies(te_ref[i], sl):
                cp.wait()
            wgb[...] = wgf[sl].astype(BF16)
            wub[...] = wuf[sl].astype(BF16)
            wdb[...] = wdf[sl].astype(BF16)

        def expert_mlp(k):
            r = k * MOE_SUBTILE
            x = _unpack_bf16_pairs(xs_ref[:r, :])
            g = jnp.dot(x, wgb[...], preferred_element_type=F32)
            u = jnp.dot(x, wub[...], preferred_element_type=F32)
            hdn = (g * jax.nn.sigmoid(g) * u).astype(BF16)
            y = jnp.dot(hdn, wdb[...], preferred_element_type=F32)
            y_ref[:r, :] = _pack_bf16_pairs(y.astype(BF16).astype(F32))
            if r < tm:
                y_ref[r:, :] = jnp.zeros((tm - r, y_ref.shape[1]), y_ref.dtype)

        @pl.when(jnp.logical_and(changed, filled < n_sub))
        def _():
            convert_weights()

        @pl.when(jnp.logical_and(changed, filled == n_sub))
        def _():
            convert_weights()
            expert_mlp(n_sub)

        for k in range(1, n_sub):
            @pl.when(filled == k)
            def _(k=k):
                expert_mlp(k)

        @pl.when(jnp.logical_and(jnp.logical_not(changed), filled == n_sub))
        def _():
            expert_mlp(n_sub)

    @pl.when(i >= nu_ref[0])
    def _():
        y_ref[...] = jnp.zeros(y_ref.shape, y_ref.dtype)


def _moe(xs, tile_e, n_used, next_e, w_slot, tile_sub, w_g, w_u, w_d, tm):
    n_rows = xs.shape[0]
    d = w_g.shape[1]
    dp = xs.shape[1]
    n_tiles = n_rows // tm
    ff = w_g.shape[2]
    row_blk = lambda i, te, nu, *_: (jnp.minimum(i, nu[0] - 1), 0)
    hbm = pl.BlockSpec(memory_space=pl.ANY)
    grid_spec = pltpu.PrefetchScalarGridSpec(
        num_scalar_prefetch=5,
        grid=(n_tiles,),
        in_specs=[pl.BlockSpec((tm, dp), row_blk), hbm, hbm, hbm],
        out_specs=pl.BlockSpec((tm, dp), lambda i, *_: (i, 0)),
        scratch_shapes=[pltpu.VMEM((2, d, ff), F32), pltpu.VMEM((2, d, ff), F32), pltpu.VMEM((2, ff, d), F32),
                        pltpu.VMEM((d, ff), BF16), pltpu.VMEM((d, ff), BF16), pltpu.VMEM((ff, d), BF16),
                        pltpu.SemaphoreType.DMA((2, 3))],
    )
    return pl.pallas_call(
        _moe_kernel,
        grid_spec=grid_spec,
        out_shape=jax.ShapeDtypeStruct((n_rows, dp), jnp.uint32),
        compiler_params=_cparams(("arbitrary",)),
        name="moe",
    )(tile_e, n_used, next_e, w_slot, tile_sub, xs, w_g, w_u, w_d)


def _combine_kernel(n8_ref, loff_ref, gbase_ref, x1_ref, route_ref, nw_ref, y_hbm, o_ref,
                    ybuf, sem, *, lcap, n_tt):
    i = pl.program_id(0)
    slot = i % 2
    tm = x1_ref.shape[0]

    def fetch(tile, sl, wait):
        _segment_copies(n8_ref, gbase_ref, loff_ref, y_hbm, ybuf.at[sl], sem.at[sl], tile, wait)

    @pl.when(i == 0)
    def _():
        ybuf[...] = jnp.zeros(ybuf.shape, ybuf.dtype)
        fetch(0, 0, False)

    @pl.when(i + 1 < n_tt)
    def _():
        fetch(i + 1, 1 - slot, False)

    fetch(i, slot, True)
    yb = _unpack_bf16_pairs(ybuf[slot])
    half = tm // 2
    for r0 in (0, half):
        rows = slice(r0, r0 + half)
        g1 = route_ref[rows, 0:1]
        g2 = route_ref[rows, 1:2]
        lp1 = route_ref[rows, 2:3]
        lp2 = route_ref[rows, 3:4]
        cpos = lax.broadcasted_iota(jnp.int32, (half, lcap), 1).astype(F32)
        gsel = (jnp.where(cpos == lp1, g1, 0.0) + jnp.where(cpos == lp2, g2, 0.0)).astype(BF16)
        x = x1_ref[rows, :] + jnp.dot(gsel, yb, preferred_element_type=F32)
        o_ref[rows, :] = x * lax.rsqrt(jnp.mean(x * x, axis=-1, keepdims=True) + NORM_EPS) * nw_ref[...]


def _combine(x1, y, route, norm_w, tables):
    t, d = x1.shape
    tm = min(ROW_TILE, t)
    n_tt = t // tm
    lcap = _local_rows(tm)
    n8, loff, gbase = tables[:3]
    grid_spec = pltpu.PrefetchScalarGridSpec(
        num_scalar_prefetch=3,
        grid=(n_tt,),
        in_specs=[pl.BlockSpec((tm, d), lambda i, *_: (i, 0)),
                  pl.BlockSpec((tm, LANES), lambda i, *_: (i, 0)),
                  pl.BlockSpec((1, d), lambda i, *_: (0, 0)),
                  pl.BlockSpec(memory_space=pl.ANY)],
        out_specs=pl.BlockSpec((tm, d), lambda i, *_: (i, 0)),
        scratch_shapes=[pltpu.VMEM((2, lcap, d // 2), jnp.uint32), pltpu.SemaphoreType.DMA((2,))],
    )
    return pl.pallas_call(
        functools.partial(_combine_kernel, lcap=lcap, n_tt=n_tt),
        grid_spec=grid_spec,
        out_shape=jax.ShapeDtypeStruct((t, d), F32),
        compiler_params=_cparams(("arbitrary",)),
        name="combine",
    )(n8, loff, gbase, x1, route, norm_w.reshape(1, d), y)


def _block_diag(w):
    n, bi, bj = w.shape
    eye = jnp.eye(n, dtype=w.dtype)
    return jnp.einsum('nij,nm->nimj', w, eye).reshape(n * bi, n * bj)


def kernel(x, mix_norm_w, w_in, lambda_q1, lambda_k1, lambda_q2, lambda_k2, head_norm_w, conv_w, conv_b, w_rgate, b_rgate, w_igate, b_igate, lru_lambda, w_out, ffn_norm_w, w_router_group, b_router_group, w_router_expert, b_router_expert, w_exp_gate, w_exp_up, w_exp_down, final_norm_w):
    b, s, d = x.shape
    t = b * s
    assert w_in.shape[0] == 1, "single-layer stack only"
    att_w = N_ATT_HEADS * HEAD_DIM
    tm_moe = MOE_TILE
    xf = x.reshape(t, d)
    for l in range(1):
        lambda_init = 0.8 - 0.6 * math.exp(-0.3 * l)
        assert s % ATT_TILE == 0, "sequence length must be a multiple of the attention tile"
        w_bd = jnp.concatenate([_block_diag(w_rgate[l]), _block_diag(w_igate[l])], axis=1).astype(BF16)
        b_cat = jnp.concatenate([b_rgate[l], b_igate[l]])
        qt, ka, vt, rnn = _in_proj(xf, mix_norm_w[l], w_in[l].astype(BF16), att_w, s,
                                   conv_w[l], conv_b[l], w_bd, b_cat, lru_lambda[l])
        lam_params = jnp.stack([lambda_q1[l], lambda_k1[l], lambda_q2[l], lambda_k2[l]]).astype(F32)
        att = _diff_attention(qt, ka, vt, lam_params, head_norm_w[l], lambda_init, b, s)
        w_route = jnp.concatenate([w_router_group[l], w_router_expert[l]], axis=1).T
        w_route = jnp.pad(w_route, ((0, LANES - w_route.shape[0]), (0, 0))).astype(BF16)
        b_route = jnp.concatenate([b_router_group[l], b_router_expert[l]])
        b_route = jnp.pad(b_route, (0, LANES - b_route.shape[0])).reshape(LANES, 1).astype(F32)
        x1, hn, route, route_t, n8_tiles = _out_proj(att.reshape(t, att_w), rnn.reshape(t, -1), xf,
                                                     w_out[l].astype(BF16), ffn_norm_w[l], w_route, b_route)
        n_tt = n8_tiles.shape[0]
        max_rows = TOP_K * t + n_tt * N_EXPERTS * (SEG_ROWS - 1) + N_EXPERTS * (tm_moe - 1)
        n_tiles = -(-max_rows // tm_moe)
        tables = _segment_tables(n8_tiles, tm_moe, n_tiles)
        xs = _dispatch(hn, route_t, tables, n_tiles * tm_moe, tm_moe)
        y = _moe(xs, tables[3], tables[4], tables[7], tables[8], tables[9],
                 w_exp_gate[l], w_exp_up[l], w_exp_down[l], tm_moe)
        out = _combine(x1, y, route, final_norm_w, tables)
    return out.reshape(b, s, d)
```

```python
import functools
import math

import numpy as np
import jax
import jax.numpy as jnp
from jax import lax
from jax.experimental import pallas as pl
from jax.experimental.pallas import tpu as pltpu

F32 = jnp.float32
BF16 = jnp.bfloat16

N_ATT_HEADS = 4
HEAD_DIM = 128
QK_DIM = 64
N_RNN_BLOCKS = 8
CONV_WIDTH = 4
LRU_C = 8.0
N_GROUPS = 4
EXPERTS_PER_GROUP = 8
N_EXPERTS = N_GROUPS * EXPERTS_PER_GROUP
TOP_K = 2
NORM_EPS = 1e-6
HEAD_NORM_EPS = 1e-5
LOG2E = math.log2(math.e)


def _bf16_terms(x, n):
    terms = []
    for _ in range(n):
        t = float(np.float32(x).astype(jnp.bfloat16))
        terms.append(t)
        x -= t
    return tuple(terms)


LOG2E_TERMS = _bf16_terms(LOG2E, 3)
LANES = 128
SUBLANES = 8
SEG_ROWS = SUBLANES
NEG_BIG = -1e30

ROW_TILE = 512
ROUTE_ROWS = 48
ATT_TILE = 512
ATT_HEADS_PER_STEP = 4
V_ROWS = HEAD_DIM + 16
LRU_CHUNK = 128
MOE_TILE = 512
MOE_SUBTILE = 128
VMEM_LIMIT = 48 * 1024 * 1024


def _cparams(sem, vmem=VMEM_LIMIT, **kw):
    return pltpu.CompilerParams(dimension_semantics=sem, vmem_limit_bytes=vmem, **kw)


def _inproj_kernel(slope_ref, x_ref, nw_ref, w_ref, cw_ref, cb_ref, wg_ref, bg_ref, lam_ref,
                   qt_ref, ka_ref, vt_ref, rnn_ref, wb, xs, carry_h, a_s, u_s, *, att_w, s_len, ch):
    i = pl.program_id(0)
    x = x_ref[...]
    tm = x.shape[0]
    c_w = (w_ref.shape[1] - 3 * att_w) // 2

    @pl.when(i == 0)
    def _():
        wb[...] = w_ref[...].astype(BF16)

    @pl.when((i * tm) % s_len == 0)
    def _():
        xs[0:SUBLANES, :] = jnp.zeros((SUBLANES, c_w), F32)
        carry_h[...] = jnp.zeros(carry_h.shape, F32)

    ms = jnp.mean(x * x, axis=-1, keepdims=True)
    hn = (x * lax.rsqrt(ms + NORM_EPS) * nw_ref[...]).astype(BF16)
    p_lru = jnp.dot(hn, wb[:, 3 * att_w:], preferred_element_type=F32)

    xs[SUBLANES:, :] = p_lru[:, :c_w]
    neg_lam = -lam_ref[...]
    sp = jnp.maximum(neg_lam, 0.0) + jnp.log1p(jnp.exp(-jnp.abs(neg_lam)))
    cw = cw_ref[...]
    cb = cb_ref[...]
    bias = bg_ref[...]
    r8 = lax.broadcasted_iota(jnp.int32, (ch // SUBLANES, SUBLANES, c_w), 1)
    xcs, zs = [], []
    for c in range(tm // ch):
        r0 = c * ch
        win = xs[r0:r0 + ch + SUBLANES, :]
        xc = cw[3:4, :] * win[SUBLANES:] + cb
        for k in (1, 2, 3):
            xc = xc + cw[3 - k:4 - k, :] * pltpu.roll(win, k, axis=0)[SUBLANES:]
        xcs.append(xc)
        zs.append(jnp.dot(xc.astype(BF16), wg_ref[...], preferred_element_type=F32) + bias)
    xs[0:SUBLANES, :] = xs[tm:tm + SUBLANES, :]

    p_q = jnp.dot(hn, wb[:, :att_w], preferred_element_type=F32)
    p_v = jnp.dot(hn, wb[:, 2 * att_w:3 * att_w], preferred_element_type=F32)
    p_k = jnp.dot(hn, wb[:, att_w:2 * att_w], preferred_element_type=F32)

    for c in range(tm // ch):
        r0 = c * ch
        xc, z = xcs[c], zs[c]
        r = _sigmoid(z[:, :c_w])
        ig = _sigmoid(z[:, c_w:])
        log_a = (-LRU_C) * r * sp
        a = jnp.exp(log_a)
        w = jnp.tanh(-log_a) * (1.0 + a * a)
        u = jnp.where(w > 0.0, w * lax.rsqrt(w), 0.0) * ig * xc
        a = a.reshape(ch // SUBLANES, SUBLANES, c_w)
        u = u.reshape(ch // SUBLANES, SUBLANES, c_w)
        for k in (1, 2, 4):
            a_sh = pltpu.roll(a, k, axis=1)
            u_sh = pltpu.roll(u, k, axis=1)
            ok = r8 >= k
            u = jnp.where(ok, u + a * u_sh, u)
            a = jnp.where(ok, a * a_sh, a)
        a_s[r0:r0 + ch, :] = a.reshape(ch, c_w)
        u_s[r0:r0 + ch, :] = u.reshape(ch, c_w)

    qt = (p_q * (QK_DIM ** -0.5 * LOG2E)).T
    arow = lax.broadcasted_iota(jnp.int32, (QK_DIM, tm), 0)
    ones2 = jnp.where(arow < 2, LOG2E_TERMS[0], jnp.where(arow < 4, LOG2E_TERMS[1],
                                                          jnp.where(arow < 6, LOG2E_TERMS[2], 0.0)))
    pieces = []
    for g in range(2 * N_ATT_HEADS):
        pieces += [qt[g * QK_DIM:(g + 1) * QK_DIM], ones2]
    qt_ref[0] = jnp.concatenate(pieces, axis=0).astype(BF16)

    lane = lax.broadcasted_iota(jnp.int32, (tm, HEAD_DIM), 1)
    j = (i * tm) % s_len + lax.broadcasted_iota(jnp.int32, (tm, HEAD_DIM), 0)
    j_lo = (j & 255).astype(F32)
    j_hi = (j - (j & 255)).astype(F32)
    vtt = p_v.T
    ones_rows = jnp.where(lax.broadcasted_iota(jnp.int32, (V_ROWS - HEAD_DIM, tm), 0) == 0, 1.0, 0.0)
    for h in range(N_ATT_HEADS):
        slope = slope_ref[h]
        kk = p_k[:, h * HEAD_DIM:(h + 1) * HEAD_DIM]
        in_aug = (lane >= QK_DIM) & (lane < QK_DIM + 2 * len(LOG2E_TERMS))
        aug = jnp.where(in_aug, jnp.where((lane & 1) == 0, slope * j_hi, slope * j_lo), 0.0)
        ka_ref[:, 2 * h * HEAD_DIM:(2 * h + 1) * HEAD_DIM] = jnp.where(lane < QK_DIM, kk, aug).astype(BF16)
        ka_ref[:, (2 * h + 1) * HEAD_DIM:(2 * h + 2) * HEAD_DIM] = jnp.where(
            lane < QK_DIM, pltpu.roll(kk, QK_DIM, axis=1), aug).astype(BF16)
        vt_ref[0, h * V_ROWS:h * V_ROWS + HEAD_DIM, :] = vtt[h * HEAD_DIM:(h + 1) * HEAD_DIM].astype(BF16)
        vt_ref[0, h * V_ROWS + HEAD_DIM:(h + 1) * V_ROWS, :] = ones_rows.astype(BF16)

    hprev = carry_h[0:1, :]
    for r0 in range(0, tm, SUBLANES):
        hg = u_s[r0:r0 + SUBLANES, :] + a_s[r0:r0 + SUBLANES, :] * hprev
        u_s[r0:r0 + SUBLANES, :] = hg
        hprev = hg[SUBLANES - 1:SUBLANES, :]
    carry_h[0:1, :] = hprev
    rnn_ref[...] = (u_s[...] * _gelu_tanh(p_lru[:, c_w:])).astype(rnn_ref.dtype)


def _gelu_tanh(x):
    k1 = -2.0 * math.sqrt(2.0 / math.pi) * LOG2E
    return x / (1.0 + jnp.exp2(x * (k1 + (k1 * 0.044715) * (x * x))))


def _sigmoid(x):
    return 0.5 * jnp.tanh(0.5 * x) + 0.5


def _alibi_slopes():
    nh = N_ATT_HEADS
    return jnp.asarray(np.array([2.0 ** (-8.0 * (i + 1) / nh) for i in range(nh)], dtype=np.float32))


def _in_proj(xf, norm_w, w_in, att_w, s_len, conv_w, conv_b, w_gates_bf, b_gates, lru_lambda):
    t, d = xf.shape
    n = w_in.shape[1]
    tm = min(ATT_TILE, t)
    ch = min(LRU_CHUNK, tm)
    nh = N_ATT_HEADS
    c_w = (n - 3 * att_w) // 2
    fix = lambda i: (0, 0)
    return pl.pallas_call(
        functools.partial(_inproj_kernel, att_w=att_w, s_len=s_len, ch=ch),
        grid=(t // tm,),
        in_specs=[pl.BlockSpec(memory_space=pltpu.SMEM),
                  pl.BlockSpec((tm, d), lambda i: (i, 0)),
                  pl.BlockSpec((1, d), fix),
                  pl.BlockSpec((d, n), fix, pipeline_mode=pl.Buffered(1)),
                  pl.BlockSpec((CONV_WIDTH, c_w), fix),
                  pl.BlockSpec((1, c_w), fix),
                  pl.BlockSpec((c_w, 2 * c_w), fix),
                  pl.BlockSpec((1, 2 * c_w), fix),
                  pl.BlockSpec((1, c_w), fix)],
        out_specs=[pl.BlockSpec((1, 2 * att_w, tm), lambda i: (i, 0, 0)),
                   pl.BlockSpec((tm, 2 * att_w), lambda i: (i, 0)),
                   pl.BlockSpec((1, nh * V_ROWS, tm), lambda i: (i, 0, 0)),
                   pl.BlockSpec((tm, c_w), lambda i: (i, 0))],
        out_shape=[jax.ShapeDtypeStruct((t // tm, 2 * att_w, tm), BF16),
                   jax.ShapeDtypeStruct((t, 2 * att_w), BF16),
                   jax.ShapeDtypeStruct((t // tm, nh * V_ROWS, tm), BF16),
                   jax.ShapeDtypeStruct((t, c_w), BF16)],
        scratch_shapes=[pltpu.VMEM((d, n), BF16),
                        pltpu.VMEM((tm + SUBLANES, c_w), F32), pltpu.VMEM((SUBLANES, c_w), F32),
                        pltpu.VMEM((tm, c_w), F32), pltpu.VMEM((tm, c_w), F32)],
        compiler_params=_cparams(("arbitrary",)),
        name="in_proj",
    )(_alibi_slopes(), xf, norm_w.reshape(1, d), w_in, conv_w, conv_b.reshape(1, c_w), w_gates_bf,
      b_gates.reshape(1, 2 * c_w), lru_lambda.reshape(1, c_w))


def _attn_kernel(lam_ref, hw_ref, q_ref, k_ref, vt, o_ref, sb, mx, acc, *, tq, n_heads, lambda_init):
    qi = pl.program_id(2)
    n_maps = 2 * n_heads
    mx[...] = jnp.full(mx.shape, NEG_BIG, F32)
    acc[...] = jnp.zeros(acc.shape, F32)

    def values(c, n, lanes=slice(None)):
        return vt[c, (n // 2) * V_ROWS:(n // 2 + 1) * V_ROWS, lanes]

    def scores(c, slot):
        rows = pl.ds(pl.multiple_of(c * tq, tq), tq)
        for n in range(n_maps):
            sb[n, slot] = jnp.dot(k_ref[0, rows, n * HEAD_DIM:(n + 1) * HEAD_DIM],
                                  q_ref[0, n * HEAD_DIM:(n + 1) * HEAD_DIM, :],
                                  preferred_element_type=F32)

    def softmax_pv(c, slot):
        for n in range(n_maps):
            s = sb[n, slot]
            m_prev = mx[n]
            m_new = jnp.maximum(m_prev, jnp.max(s, axis=0, keepdims=True))
            p = jnp.exp2(s - m_new).astype(BF16)
            acc[n] = jnp.exp2(m_prev - m_new) * acc[n] + jnp.dot(values(c, n), p, preferred_element_type=F32)
            mx[n] = m_new

    def softmax_pv_diagonal(c, slot):
        hq = tq // 2
        keep_t = (lax.broadcasted_iota(jnp.int32, (hq, tq), 0) <= lax.broadcasted_iota(jnp.int32, (hq, tq), 1))
        keep_b = (lax.broadcasted_iota(jnp.int32, (hq, hq), 0) <= lax.broadcasted_iota(jnp.int32, (hq, hq), 1))
        for n in range(n_maps):
            top = jnp.where(keep_t, sb[n, slot, :hq, :], NEG_BIG)
            bot = jnp.where(keep_b, sb[n, slot, hq:, hq:], NEG_BIG)
            mt = jnp.max(top, axis=0, keepdims=True)
            mb = jnp.max(bot, axis=0, keepdims=True)
            m_prev = mx[n]
            m_new = jnp.maximum(m_prev, jnp.concatenate([mt[:, :hq], jnp.maximum(mt[:, hq:], mb)], axis=1))
            p_top = jnp.exp2(top - m_new).astype(BF16)
            p_bot = jnp.exp2(bot - m_new[:, hq:]).astype(BF16)
            acc[n] = (jnp.exp2(m_prev - m_new) * acc[n]
                      + jnp.dot(values(c, n, slice(0, hq)), p_top, preferred_element_type=F32))
            acc[n, :, hq:] += jnp.dot(values(c, n, slice(hq, tq)), p_bot, preferred_element_type=F32)
            mx[n] = m_new

    scores(0, 0)

    def body(j, c):
        scores(2 * j + 1, 1)
        softmax_pv(2 * j, 0)
        scores(2 * j + 2, 0)
        softmax_pv(2 * j + 1, 1)
        return c

    lax.fori_loop(0, qi // 2, body, 0)

    @pl.when(qi % 2 == 0)
    def _():
        softmax_pv_diagonal(qi, 0)

    @pl.when(qi % 2 == 1)
    def _():
        scores(qi, 1)
        softmax_pv(qi - 1, 0)
        softmax_pv_diagonal(qi, 1)

    lam = (jnp.exp(jnp.sum(lam_ref[0:1, :] * lam_ref[1:2, :], axis=1, keepdims=True))
           - jnp.exp(jnp.sum(lam_ref[2:3, :] * lam_ref[3:4, :], axis=1, keepdims=True))
           + lambda_init)
    for hh in range(n_heads):
        o1 = acc[2 * hh, :HEAD_DIM, :] * (1.0 / acc[2 * hh, HEAD_DIM:HEAD_DIM + 1, :])
        o2 = acc[2 * hh + 1, :HEAD_DIM, :] * (1.0 / acc[2 * hh + 1, HEAD_DIM:HEAD_DIM + 1, :])
        o = o1 - lam * o2
        o = o * lax.rsqrt(jnp.mean(o * o, axis=0, keepdims=True) + HEAD_NORM_EPS)
        o_ref[0, :, hh * HEAD_DIM:(hh + 1) * HEAD_DIM] = (
            o.T * hw_ref[...] * (1.0 - lambda_init)).astype(o_ref.dtype)


def _diff_attention(qt, ka, vt, lam_params, head_norm_w, lambda_init, b, s):
    nh = N_ATT_HEADS
    hp = ATT_HEADS_PER_STEP
    tq = qt.shape[2]
    nq = s // tq
    return pl.pallas_call(
        functools.partial(_attn_kernel, tq=tq, n_heads=hp, lambda_init=lambda_init),
        grid=(b, nh // hp, nq),
        in_specs=[pl.BlockSpec((4, QK_DIM), lambda bi, hi, qi: (0, 0)),
                  pl.BlockSpec((1, HEAD_DIM), lambda bi, hi, qi: (0, 0)),
                  pl.BlockSpec((1, hp * 2 * HEAD_DIM, tq), lambda bi, hi, qi: (bi * nq + qi, hi, 0)),
                  pl.BlockSpec((1, s, hp * 2 * HEAD_DIM), lambda bi, hi, qi: (bi, 0, hi)),
                  pl.BlockSpec((nq, hp * V_ROWS, tq), lambda bi, hi, qi: (bi, hi, 0))],
        out_specs=pl.BlockSpec((1, tq, hp * HEAD_DIM), lambda bi, hi, qi: (bi, qi, hi)),
        out_shape=jax.ShapeDtypeStruct((b, s, nh * HEAD_DIM), BF16),
        scratch_shapes=[pltpu.VMEM((2 * hp, 2, tq, tq), F32), pltpu.VMEM((2 * hp, 1, tq), F32),
                        pltpu.VMEM((2 * hp, V_ROWS, tq), F32)],
        compiler_params=_cparams(("parallel", "parallel", "arbitrary"), vmem=56 * 1024 * 1024),
        name="diff_attn",
    )(lam_params, head_norm_w.reshape(1, HEAD_DIM), qt, ka.reshape(b, s, ka.shape[1]), vt)


def _outproj_kernel(att_ref, rnn_ref, x_ref, wo_ref, nw_ref, wrt_ref, brc_ref,
                    x1_ref, hn_ref, route_ref, route_t_ref, n8_ref, wob, *, att_w):
    @pl.when(pl.program_id(0) == 0)
    def _():
        wob[...] = wo_ref[...].astype(BF16)

    y = jnp.dot(att_ref[...], wob[:att_w, :], preferred_element_type=F32)
    y = y + jnp.dot(rnn_ref[...], wob[att_w:, :], preferred_element_type=F32)
    x1 = x_ref[...] + y
    x1_ref[...] = x1
    hn = (x1 * lax.rsqrt(jnp.mean(x1 * x1, axis=-1, keepdims=True) + NORM_EPS) * nw_ref[...]).astype(BF16)
    hn_ref[...] = hn
    tm = hn.shape[0]

    lg = lax.dot_general(wrt_ref[...], hn, (((1,), (1,)), ((), ())), preferred_element_type=F32)
    lg = lg[:ROUTE_ROWS] + brc_ref[:ROUTE_ROWS, 0:1]
    rowf = lax.broadcasted_iota(jnp.int32, lg.shape, 0).astype(F32)
    big = float(LANES)
    ninf = -jnp.inf
    is_g = rowf < N_GROUPS
    lgm = jnp.where(is_g, lg, ninf)
    mg = jnp.max(lgm, axis=0, keepdims=True)
    g_sel = jnp.min(jnp.where(lgm == mg, rowf, big), axis=0, keepdims=True)
    pg = 1.0 / jnp.sum(jnp.where(is_g, jnp.exp(lgm - mg), 0.0), axis=0, keepdims=True)
    lo = N_GROUPS + EXPERTS_PER_GROUP * g_sel
    in_grp = (rowf >= lo) & (rowf < lo + EXPERTS_PER_GROUP)
    lem = jnp.where(in_grp, lg, ninf)
    v1 = jnp.max(lem, axis=0, keepdims=True)
    i1 = jnp.min(jnp.where(lem == v1, rowf, big), axis=0, keepdims=True)
    lem2 = jnp.where(rowf == i1, ninf, lem)
    v2 = jnp.max(lem2, axis=0, keepdims=True)
    i2 = jnp.min(jnp.where(lem2 == v2, rowf, big), axis=0, keepdims=True)
    e2 = jnp.exp(v2 - v1)
    den = 1.0 + e2
    g1 = pg / den
    g2 = pg * e2 / den

    oh1 = jnp.where(rowf == i1, 1.0, 0.0)
    oh2 = jnp.where(rowf == i2, 1.0, 0.0)
    oh = oh1 + oh2
    earlier = (lax.broadcasted_iota(jnp.int32, (tm, tm), 0)
               < lax.broadcasted_iota(jnp.int32, (tm, tm), 1)).astype(BF16)
    pref = jnp.dot(oh.astype(BF16), earlier, preferred_element_type=F32)
    cnt = jnp.sum(oh, axis=1, keepdims=True)
    n8 = jnp.floor((cnt + (SEG_ROWS - 1)) * (1.0 / SEG_ROWS))
    n8_b = jnp.broadcast_to(n8, (ROUTE_ROWS, LANES))
    before = (lax.broadcasted_iota(jnp.int32, (ROUTE_ROWS, ROUTE_ROWS), 1)
              < lax.broadcasted_iota(jnp.int32, (ROUTE_ROWS, ROUTE_ROWS), 0)).astype(BF16)
    loff8 = jnp.dot(before, n8_b.astype(BF16), preferred_element_type=F32)[:, 0:1]
    pos = SEG_ROWS * loff8 + pref
    lp1 = jnp.sum(oh1 * pos, axis=0, keepdims=True)
    lp2 = jnp.sum(oh2 * pos, axis=0, keepdims=True)
    route_t = jnp.concatenate([g1, g2, lp1, lp2, jnp.zeros((LANES - 4, tm), F32)], axis=0)
    route_t_ref[0] = route_t[:SUBLANES]
    route_ref[...] = route_t.T
    n8_ref[0] = n8_b


def _out_proj(att, rnn, xf, w_out, norm_w, w_route_t_bf, b_route_col):
    t, d = xf.shape
    att_w = att.shape[1]
    tm = min(ROW_TILE, t)
    row = lambda i: (i, 0)
    fix = lambda i: (0, 0)
    return pl.pallas_call(
        functools.partial(_outproj_kernel, att_w=att_w),
        grid=(t // tm,),
        in_specs=[pl.BlockSpec((tm, att_w), row), pl.BlockSpec((tm, rnn.shape[1]), row),
                  pl.BlockSpec((tm, d), row), pl.BlockSpec(w_out.shape, fix, pipeline_mode=pl.Buffered(1)),
                  pl.BlockSpec((1, d), fix), pl.BlockSpec((LANES, d), fix), pl.BlockSpec((LANES, 1), fix)],
        out_specs=[pl.BlockSpec((tm, d), row), pl.BlockSpec((tm, d), row), pl.BlockSpec((tm, LANES), row),
                   pl.BlockSpec((1, SUBLANES, tm), lambda i: (i, 0, 0)),
                   pl.BlockSpec((1, ROUTE_ROWS, LANES), lambda i: (i, 0, 0))],
        out_shape=[jax.ShapeDtypeStruct((t, d), F32), jax.ShapeDtypeStruct((t, d), BF16),
                   jax.ShapeDtypeStruct((t, LANES), F32),
                   jax.ShapeDtypeStruct((t // tm, SUBLANES, tm), F32),
                   jax.ShapeDtypeStruct((t // tm, ROUTE_ROWS, LANES), F32)],
        scratch_shapes=[pltpu.VMEM(w_out.shape, BF16)],
        compiler_params=_cparams(("arbitrary",)),
        name="out_proj",
    )(att, rnn, xf, w_out, norm_w.reshape(1, d), w_route_t_bf, b_route_col)


def _local_rows(tm):
    return -(-(TOP_K * tm + N_EXPERTS * (SEG_ROWS - 1)) // LANES) * LANES


def _segment_tables(n8_tiles, tm_moe, n_tiles):
    n8 = n8_tiles[:, N_GROUPS:N_GROUPS + N_EXPERTS, 0].astype(jnp.int32)
    c8 = n8 * SEG_ROWS
    loff = jnp.cumsum(c8, axis=1) - c8
    gtot = jnp.sum(c8, axis=0)
    gpad = (gtot + tm_moe - 1) // tm_moe * tm_moe
    gend = jnp.cumsum(gpad)
    gstart = gend - gpad
    gbase = gstart[None, :] + jnp.cumsum(c8, axis=0) - c8
    tile_row0 = jnp.arange(n_tiles, dtype=jnp.int32) * tm_moe
    tile_e = jnp.minimum(jnp.sum((gend[None, :] <= tile_row0[:, None]).astype(jnp.int32), axis=1),
                         N_EXPERTS - 1).astype(jnp.int32)
    n_used = (gend[-1] // tm_moe).astype(jnp.int32).reshape(1)
    tail_start = (gstart + gtot).astype(jnp.int32)
    tail_n8 = ((gpad - gtot) // SEG_ROWS).astype(jnp.int32)
    after = gend[tile_e] // tm_moe
    next_e = jnp.where(after < n_used[0], tile_e[jnp.minimum(after, n_tiles - 1)], -1).astype(jnp.int32)
    first = jnp.concatenate([jnp.ones((1,), jnp.int32), (tile_e[1:] != tile_e[:-1]).astype(jnp.int32)])
    w_slot = ((jnp.cumsum(first) - 1) % 2).astype(jnp.int32)
    rows_in_tile = jnp.clip((gstart + gtot)[tile_e] - tile_row0, 1, tm_moe)
    tile_sub = ((rows_in_tile + MOE_SUBTILE - 1) // MOE_SUBTILE).astype(jnp.int32)
    return (n8.reshape(-1), loff.reshape(-1).astype(jnp.int32), gbase.reshape(-1).astype(jnp.int32),
            tile_e, n_used, tail_start, tail_n8, next_e, w_slot, tile_sub)


def _seg_aligned(rows):
    return pl.multiple_of(rows, SEG_ROWS) if SEG_ROWS > 1 else rows


def _segment_copies(n8_ref, src_off_ref, dst_off_ref, src, dst, sem, tile, wait):
    def rows_of(e):
        return _seg_aligned(n8_ref[tile * N_EXPERTS + e] * SEG_ROWS)

    if wait:
        total = lax.fori_loop(0, N_EXPERTS, lambda e, acc: acc + rows_of(e), 0)
        total = _seg_aligned(total)
        pltpu.make_async_copy(src.at[pl.ds(0, total), :], dst.at[pl.ds(0, total), :], sem).wait()
        return

    def per_expert(e, c):
        k = tile * N_EXPERTS + e
        rows = rows_of(e)

        @pl.when(rows > 0)
        def _():
            pltpu.make_async_copy(
                src.at[pl.ds(_seg_aligned(src_off_ref[k]), rows), :],
                dst.at[pl.ds(_seg_aligned(dst_off_ref[k]), rows), :], sem).start()
        return c
    lax.fori_loop(0, N_EXPERTS, per_expert, 0)


def _pack_bf16_pairs(x):
    n = x.shape[1] // 2
    bits = lax.bitcast_convert_type(x, jnp.uint32)
    return (bits[:, :n] >> 16) | (bits[:, n:] & jnp.uint32(0xFFFF0000))


def _unpack_bf16_pairs(p):
    lo = lax.bitcast_convert_type(p << 16, F32)
    hi = lax.bitcast_convert_type(p & jnp.uint32(0xFFFF0000), F32)
    return jnp.concatenate([lo, hi], axis=1).astype(BF16)


def _dispatch_kernel(n8_ref, loff_ref, gbase_ref, tstart_ref, tn8_ref, nu_ref, hn_ref, route_ref, xs_hbm,
                     stage, zbuf, sem, zsem, *, lcap, n_tt):
    i = pl.program_id(0)
    slot = i % 2
    tm = hn_ref.shape[0]
    tm_moe = zbuf.shape[0]
    n_tiles = xs_hbm.shape[0] // tm_moe

    def tail_copies(wait):
        def go(cp):
            if wait:
                cp.wait()
            else:
                cp.start()

        def per_expert(e, c):
            rows = _seg_aligned(tn8_ref[e] * SEG_ROWS)

            @pl.when(rows > 0)
            def _():
                go(pltpu.make_async_copy(
                    zbuf.at[pl.ds(0, rows), :],
                    xs_hbm.at[pl.ds(_seg_aligned(tstart_ref[e]), rows), :], zsem.at[0]))
            return c
        lax.fori_loop(0, N_EXPERTS, per_expert, 0)

        def per_unused_tile(j, c):
            go(pltpu.make_async_copy(zbuf, xs_hbm.at[pl.ds(pl.multiple_of(j * tm_moe, tm_moe), tm_moe), :],
                                     zsem.at[0]))
            return c
        lax.fori_loop(nu_ref[0], n_tiles, per_unused_tile, 0)

    @pl.when(i == 0)
    def _():
        zbuf[...] = jnp.zeros(zbuf.shape, zbuf.dtype)
        tail_copies(False)

    @pl.when(i >= 2)
    def _():
        _segment_copies(n8_ref, loff_ref, gbase_ref, stage.at[slot], xs_hbm, sem.at[slot], i - 2, True)

    lp1 = route_ref[0, 2:3, :]
    lp2 = route_ref[0, 3:4, :]
    rpos = lax.broadcasted_iota(jnp.int32, (lcap, tm), 0).astype(F32)
    sel = jnp.where((rpos == lp1) | (rpos == lp2), 1.0, 0.0).astype(BF16)
    stage[slot] = _pack_bf16_pairs(jnp.dot(sel, hn_ref[...], preferred_element_type=F32))
    _segment_copies(n8_ref, loff_ref, gbase_ref, stage.at[slot], xs_hbm, sem.at[slot], i, False)

    @pl.when(i == n_tt - 1)
    def _():
        _segment_copies(n8_ref, loff_ref, gbase_ref, stage.at[slot], xs_hbm, sem.at[slot], i, True)
        if n_tt > 1:
            _segment_copies(n8_ref, loff_ref, gbase_ref, stage.at[1 - slot], xs_hbm, sem.at[1 - slot],
                            i - 1, True)
        tail_copies(True)


def _dispatch(hn, route, tables, n_rows, tm_moe):
    t, d = hn.shape
    tm = min(ROW_TILE, t)
    n_tt = t // tm
    lcap = _local_rows(tm)
    n8, loff, gbase, _, n_used, tail_start, tail_n8 = tables[:7]
    grid_spec = pltpu.PrefetchScalarGridSpec(
        num_scalar_prefetch=6,
        grid=(n_tt,),
        in_specs=[pl.BlockSpec((tm, d), lambda i, *_: (i, 0)),
                  pl.BlockSpec((1, SUBLANES, tm), lambda i, *_: (i, 0, 0))],
        out_specs=pl.BlockSpec(memory_space=pl.ANY),
        scratch_shapes=[pltpu.VMEM((2, lcap, d // 2), jnp.uint32), pltpu.VMEM((tm_moe, d // 2), jnp.uint32),
                        pltpu.SemaphoreType.DMA((2,)), pltpu.SemaphoreType.DMA((1,))],
    )
    return pl.pallas_call(
        functools.partial(_dispatch_kernel, lcap=lcap, n_tt=n_tt),
        grid_spec=grid_spec,
        out_shape=jax.ShapeDtypeStruct((n_rows, d // 2), jnp.uint32),
        compiler_params=_cparams(("arbitrary",), has_side_effects=True),
        name="dispatch",
    )(n8, loff, gbase, tail_start, tail_n8, n_used, hn, route)


def _moe_kernel(te_ref, nu_ref, nxt_ref, wslot_ref, nsub_ref, xs_hbm, wg_hbm, wu_hbm, wd_hbm, y_hbm,
                xbuf, ybuf, zbuf, wgf, wuf, wdf, wgb, wub, wdb, xsem, ysem, zsem, wsem):
    tm = xbuf.shape[1]
    n_sub = tm // MOE_SUBTILE
    n_tiles = y_hbm.shape[0] // tm
    nu = nu_ref[0]

    def tile_rows(t):
        return pl.ds(pl.multiple_of(t * tm, tm), tm)

    def x_copy(t, sl):
        return pltpu.make_async_copy(xs_hbm.at[tile_rows(t), :], xbuf.at[sl], xsem.at[sl])

    def y_copy(t, sl):
        return pltpu.make_async_copy(ybuf.at[sl], y_hbm.at[tile_rows(t), :], ysem.at[sl])

    def zero_copy(t):
        return pltpu.make_async_copy(zbuf, y_hbm.at[tile_rows(t), :], zsem.at[0])

    def weight_copies(e, sl):
        return (pltpu.make_async_copy(wg_hbm.at[e], wgf.at[sl], wsem.at[sl, 0]),
                pltpu.make_async_copy(wu_hbm.at[e], wuf.at[sl], wsem.at[sl, 1]),
                pltpu.make_async_copy(wd_hbm.at[e], wdf.at[sl], wsem.at[sl, 2]))

    zbuf[...] = jnp.zeros(zbuf.shape, zbuf.dtype)

    def start_zero(t, c):
        zero_copy(t).start()
        return c
    lax.fori_loop(nu, n_tiles, start_zero, 0)

    for cp in weight_copies(te_ref[0], wslot_ref[0]):
        cp.start()
    x_copy(0, 0).start()

    def tile(i, c):
        sl = i % 2
        x_copy(i, sl).wait()

        @pl.when(i + 1 < nu)
        def _():
            x_copy(i + 1, 1 - sl).start()

        @pl.when(i >= 2)
        def _():
            y_copy(i - 2, sl).wait()

        changed = jnp.logical_or(i == 0, te_ref[i] != te_ref[jnp.maximum(i - 1, 0)])
        filled = nsub_ref[i]

        @pl.when(jnp.logical_and(changed, nxt_ref[i] >= 0))
        def _():
            for cp in weight_copies(nxt_ref[i], 1 - wslot_ref[i]):
                cp.start()

        def convert_weights():
            wsl = wslot_ref[i]
            for cp in weight_copies(te_ref[i], wsl):
                cp.wait()
            wgb[...] = wgf[wsl].astype(BF16)
            wub[...] = wuf[wsl].astype(BF16)
            wdb[...] = wdf[wsl].astype(BF16)

        def expert_mlp(k):
            r = k * MOE_SUBTILE
            x = _unpack_bf16_pairs(xbuf[sl, :r, :])
            g = jnp.dot(x, wgb[...], preferred_element_type=F32)
            u = jnp.dot(x, wub[...], preferred_element_type=F32)
            hdn = (g * jax.nn.sigmoid(g) * u).astype(BF16)
            y = jnp.dot(hdn, wdb[...], preferred_element_type=F32)
            ybuf[sl, :r, :] = _pack_bf16_pairs(y.astype(BF16).astype(F32))
            if r < tm:
                ybuf[sl, r:, :] = jnp.zeros((tm - r, ybuf.shape[2]), ybuf.dtype)

        @pl.when(jnp.logical_and(changed, filled < n_sub))
        def _():
            convert_weights()

        @pl.when(jnp.logical_and(changed, filled == n_sub))
        def _():
            convert_weights()
            expert_mlp(n_sub)

        for k in range(1, n_sub):
            @pl.when(filled == k)
            def _(k=k):
                expert_mlp(k)

        @pl.when(jnp.logical_and(jnp.logical_not(changed), filled == n_sub))
        def _():
            expert_mlp(n_sub)

        y_copy(i, sl).start()
        return c

    lax.fori_loop(0, nu, tile, 0)

    @pl.when(nu >= 2)
    def _():
        y_copy(nu - 2, nu % 2).wait()
    y_copy(nu - 1, (nu - 1) % 2).wait()

    def wait_zero(t, c):
        zero_copy(t).wait()
        return c
    lax.fori_loop(nu, n_tiles, wait_zero, 0)


def _moe(xs, tile_e, n_used, next_e, w_slot, tile_sub, w_g, w_u, w_d, tm):
    n_rows, dp = xs.shape
    d = w_g.shape[1]
    ff = w_g.shape[2]
    hbm = pl.BlockSpec(memory_space=pl.ANY)
    grid_spec = pltpu.PrefetchScalarGridSpec(
        num_scalar_prefetch=5,
        grid=(1,),
        in_specs=[hbm, hbm, hbm, hbm],
        out_specs=hbm,
        scratch_shapes=[pltpu.VMEM((2, tm, dp), jnp.uint32), pltpu.VMEM((2, tm, dp), jnp.uint32),
                        pltpu.VMEM((tm, dp), jnp.uint32),
                        pltpu.VMEM((2, d, ff), F32), pltpu.VMEM((2, d, ff), F32), pltpu.VMEM((2, ff, d), F32),
                        pltpu.VMEM((d, ff), BF16), pltpu.VMEM((d, ff), BF16), pltpu.VMEM((ff, d), BF16),
                        pltpu.SemaphoreType.DMA((2,)), pltpu.SemaphoreType.DMA((2,)),
                        pltpu.SemaphoreType.DMA((1,)), pltpu.SemaphoreType.DMA((2, 3))],
    )
    return pl.pallas_call(
        _moe_kernel,
        grid_spec=grid_spec,
        out_shape=jax.ShapeDtypeStruct((n_rows, dp), jnp.uint32),
        compiler_params=_cparams(("arbitrary",), has_side_effects=True),
        name="moe",
    )(tile_e, n_used, next_e, w_slot, tile_sub, xs, w_g, w_u, w_d)


def _combine_kernel(n8_ref, loff_ref, gbase_ref, x1_ref, route_ref, nw_ref, y_hbm, o_ref,
                    ybuf, sem, *, lcap, n_tt):
    i = pl.program_id(0)
    slot = i % 2
    tm = x1_ref.shape[0]

    def fetch(tile, sl, wait):
        _segment_copies(n8_ref, gbase_ref, loff_ref, y_hbm, ybuf.at[sl], sem.at[sl], tile, wait)

    @pl.when(i == 0)
    def _():
        ybuf[...] = jnp.zeros(ybuf.shape, ybuf.dtype)
        fetch(0, 0, False)

    @pl.when(i + 1 < n_tt)
    def _():
        fetch(i + 1, 1 - slot, False)

    fetch(i, slot, True)
    yb = _unpack_bf16_pairs(ybuf[slot])
    half = tm // 2
    for r0 in (0, half):
        rows = slice(r0, r0 + half)
        g1 = route_ref[rows, 0:1]
        g2 = route_ref[rows, 1:2]
        lp1 = route_ref[rows, 2:3]
        lp2 = route_ref[rows, 3:4]
        cpos = lax.broadcasted_iota(jnp.int32, (half, lcap), 1).astype(F32)
        gsel = jnp.where(cpos == lp1, g1, jnp.where(cpos == lp2, g2, 0.0)).astype(BF16)
        x = x1_ref[rows, :] + jnp.dot(gsel, yb, preferred_element_type=F32)
        o_ref[rows, :] = x * lax.rsqrt(jnp.mean(x * x, axis=-1, keepdims=True) + NORM_EPS) * nw_ref[...]


def _combine(x1, y, route, norm_w, tables):
    t, d = x1.shape
    tm = min(ROW_TILE, t)
    n_tt = t // tm
    lcap = _local_rows(tm)
    n8, loff, gbase = tables[:3]
    grid_spec = pltpu.PrefetchScalarGridSpec(
        num_scalar_prefetch=3,
        grid=(n_tt,),
        in_specs=[pl.BlockSpec((tm, d), lambda i, *_: (i, 0)),
                  pl.BlockSpec((tm, LANES), lambda i, *_: (i, 0)),
                  pl.BlockSpec((1, d), lambda i, *_: (0, 0)),
                  pl.BlockSpec(memory_space=pl.ANY)],
        out_specs=pl.BlockSpec((tm, d), lambda i, *_: (i, 0)),
        scratch_shapes=[pltpu.VMEM((2, lcap, d // 2), jnp.uint32), pltpu.SemaphoreType.DMA((2,))],
    )
    return pl.pallas_call(
        functools.partial(_combine_kernel, lcap=lcap, n_tt=n_tt),
        grid_spec=grid_spec,
        out_shape=jax.ShapeDtypeStruct((t, d), F32),
        compiler_params=_cparams(("arbitrary",)),
        name="combine",
    )(n8, loff, gbase, x1, route, norm_w.reshape(1, d), y)


def _block_diag(w):
    n, bi, bj = w.shape
    eye = jnp.eye(n, dtype=w.dtype)
    return jnp.einsum('nij,nm->nimj', w, eye).reshape(n * bi, n * bj)


def kernel(x, mix_norm_w, w_in, lambda_q1, lambda_k1, lambda_q2, lambda_k2, head_norm_w, conv_w, conv_b, w_rgate, b_rgate, w_igate, b_igate, lru_lambda, w_out, ffn_norm_w, w_router_group, b_router_group, w_router_expert, b_router_expert, w_exp_gate, w_exp_up, w_exp_down, final_norm_w):
    b, s, d = x.shape
    t = b * s
    assert w_in.shape[0] == 1, "single-layer stack only"
    att_w = N_ATT_HEADS * HEAD_DIM
    tm_moe = MOE_TILE
    xf = x.reshape(t, d)
    for l in range(1):
        lambda_init = 0.8 - 0.6 * math.exp(-0.3 * l)
        assert s % ATT_TILE == 0, "sequence length must be a multiple of the attention tile"
        w_bd = jnp.concatenate([_block_diag(w_rgate[l]), _block_diag(w_igate[l])], axis=1).astype(BF16)
        b_cat = jnp.concatenate([b_rgate[l], b_igate[l]])
        qt, ka, vt, rnn = _in_proj(xf, mix_norm_w[l], w_in[l], att_w, s,
                                   conv_w[l], conv_b[l], w_bd, b_cat, lru_lambda[l])
        lam_params = jnp.stack([lambda_q1[l], lambda_k1[l], lambda_q2[l], lambda_k2[l]]).astype(F32)
        att = _diff_attention(qt, ka, vt, lam_params, head_norm_w[l], lambda_init, b, s)
        w_route = jnp.concatenate([w_router_group[l], w_router_expert[l]], axis=1).T
        w_route = jnp.pad(w_route, ((0, LANES - w_route.shape[0]), (0, 0))).astype(BF16)
        b_route = jnp.concatenate([b_router_group[l], b_router_expert[l]])
        b_route = jnp.pad(b_route, (0, LANES - b_route.shape[0])).reshape(LANES, 1).astype(F32)
        x1, hn, route, route_t, n8_tiles = _out_proj(att.reshape(t, att_w), rnn.reshape(t, -1), xf,
                                                     w_out[l], ffn_norm_w[l], w_route, b_route)
        n_tt = n8_tiles.shape[0]
        max_rows = TOP_K * t + n_tt * N_EXPERTS * (SEG_ROWS - 1) + N_EXPERTS * (tm_moe - 1)
        n_tiles = -(-max_rows // tm_moe)
        tables = _segment_tables(n8_tiles, tm_moe, n_tiles)
        xs = _dispatch(hn, route_t, tables, n_tiles * tm_moe, tm_moe)
        y = _moe(xs, tables[3], tables[4], tables[7], tables[8], tables[9],
                 w_exp_gate[l], w_exp_up[l], w_exp_down[l], tm_moe)
        out = _combine(x1, y, route, final_norm_w, tables)
    return out.reshape(b, s, d)
```

```python
import functools
import math

import numpy as np
import jax
import jax.numpy as jnp
from jax import lax
from jax.experimental import pallas as pl
from jax.experimental.pallas import tpu as pltpu

F32 = jnp.float32
BF16 = jnp.bfloat16

N_ATT_HEADS = 4
HEAD_DIM = 128
QK_DIM = 64
N_RNN_BLOCKS = 8
CONV_WIDTH = 4
LRU_C = 8.0
N_GROUPS = 4
EXPERTS_PER_GROUP = 8
N_EXPERTS = N_GROUPS * EXPERTS_PER_GROUP
TOP_K = 2
NORM_EPS = 1e-6
HEAD_NORM_EPS = 1e-5
LOG2E = math.log2(math.e)


def _bf16_terms(x, n):
    terms = []
    for _ in range(n):
        t = float(np.float32(x).astype(jnp.bfloat16))
        terms.append(t)
        x -= t
    return tuple(terms)


LOG2E_TERMS = _bf16_terms(LOG2E, 3)
LANES = 128
SUBLANES = 8
SEG_ROWS = SUBLANES
NEG_BIG = -1e30

ROW_TILE = 512
ROUTE_ROWS = 48
ATT_TILE = 512
ATT_HEADS_PER_STEP = 4
V_ROWS = HEAD_DIM + 16
LRU_CHUNK = 128
MOE_TILE = 512
MOE_SUBTILE = 128
VMEM_LIMIT = 48 * 1024 * 1024


def _cparams(sem, vmem=VMEM_LIMIT, **kw):
    return pltpu.CompilerParams(dimension_semantics=sem, vmem_limit_bytes=vmem, **kw)


def _inproj_kernel(slope_ref, x_ref, nw_ref, w_ref, cw_ref, cb_ref, wg_ref, bg_ref, lam_ref,
                   qt_ref, ka_ref, vt_ref, rnn_ref, wb, xs, carry_h, a_s, u_s, *, att_w, s_len, ch):
    i = pl.program_id(0)
    x = x_ref[...]
    tm = x.shape[0]
    c_w = (w_ref.shape[1] - 3 * att_w) // 2

    @pl.when(i == 0)
    def _():
        wb[...] = w_ref[...].astype(BF16)

    @pl.when((i * tm) % s_len == 0)
    def _():
        xs[0:SUBLANES, :] = jnp.zeros((SUBLANES, c_w), F32)
        carry_h[...] = jnp.zeros(carry_h.shape, F32)

    ms = jnp.mean(x * x, axis=-1, keepdims=True)
    hn = (x * lax.rsqrt(ms + NORM_EPS) * nw_ref[...]).astype(BF16)
    p_lru = jnp.dot(hn, wb[:, 3 * att_w:], preferred_element_type=F32)

    xs[SUBLANES:, :] = p_lru[:, :c_w]
    neg_lam = -lam_ref[...]
    sp = jnp.maximum(neg_lam, 0.0) + jnp.log1p(jnp.exp(-jnp.abs(neg_lam)))
    cw = cw_ref[...]
    cb = cb_ref[...]
    bias = bg_ref[...]
    r8 = lax.broadcasted_iota(jnp.int32, (ch // SUBLANES, SUBLANES, c_w), 1)
    xcs, zs = [], []
    for c in range(tm // ch):
        r0 = c * ch
        win = xs[r0:r0 + ch + SUBLANES, :]
        xc = cw[3:4, :] * win[SUBLANES:] + cb
        for k in (1, 2, 3):
            xc = xc + cw[3 - k:4 - k, :] * pltpu.roll(win, k, axis=0)[SUBLANES:]
        xcs.append(xc)
        zs.append(jnp.dot(xc.astype(BF16), wg_ref[...], preferred_element_type=F32) + bias)
    xs[0:SUBLANES, :] = xs[tm:tm + SUBLANES, :]

    p_q = jnp.dot(hn, wb[:, :att_w], preferred_element_type=F32)
    p_v = jnp.dot(hn, wb[:, 2 * att_w:3 * att_w], preferred_element_type=F32)
    p_k = jnp.dot(hn, wb[:, att_w:2 * att_w], preferred_element_type=F32)

    for c in range(tm // ch):
        r0 = c * ch
        xc, z = xcs[c], zs[c]
        r = _sigmoid(z[:, :c_w])
        ig = _sigmoid(z[:, c_w:])
        log_a = (-LRU_C) * r * sp
        a = jnp.exp(log_a)
        w = jnp.tanh(-log_a) * (1.0 + a * a)
        u = jnp.where(w > 0.0, w * lax.rsqrt(w), 0.0) * ig * xc
        a = a.reshape(ch // SUBLANES, SUBLANES, c_w)
        u = u.reshape(ch // SUBLANES, SUBLANES, c_w)
        for k in (1, 2, 4):
            a_sh = pltpu.roll(a, k, axis=1)
            u_sh = pltpu.roll(u, k, axis=1)
            ok = r8 >= k
            u = jnp.where(ok, u + a * u_sh, u)
            a = jnp.where(ok, a * a_sh, a)
        a_s[r0:r0 + ch, :] = a.reshape(ch, c_w)
        u_s[r0:r0 + ch, :] = u.reshape(ch, c_w)

    qt = (p_q * (QK_DIM ** -0.5 * LOG2E)).T
    arow = lax.broadcasted_iota(jnp.int32, (QK_DIM, tm), 0)
    ones2 = jnp.where(arow < 2, LOG2E_TERMS[0], jnp.where(arow < 4, LOG2E_TERMS[1],
                                                          jnp.where(arow < 6, LOG2E_TERMS[2], 0.0)))
    pieces = []
    for g in range(2 * N_ATT_HEADS):
        pieces += [qt[g * QK_DIM:(g + 1) * QK_DIM], ones2]
    qt_ref[0] = jnp.concatenate(pieces, axis=0).astype(BF16)

    lane = lax.broadcasted_iota(jnp.int32, (tm, HEAD_DIM), 1)
    j = (i * tm) % s_len + lax.broadcasted_iota(jnp.int32, (tm, HEAD_DIM), 0)
    j_lo = (j & 255).astype(F32)
    j_hi = (j - (j & 255)).astype(F32)
    vtt = p_v.T
    ones_rows = jnp.where(lax.broadcasted_iota(jnp.int32, (V_ROWS - HEAD_DIM, tm), 0) == 0, 1.0, 0.0)
    for h in range(N_ATT_HEADS):
        slope = slope_ref[h]
        kk = p_k[:, h * HEAD_DIM:(h + 1) * HEAD_DIM]
        in_aug = (lane >= QK_DIM) & (lane < QK_DIM + 2 * len(LOG2E_TERMS))
        aug = jnp.where(in_aug, jnp.where((lane & 1) == 0, slope * j_hi, slope * j_lo), 0.0)
        ka_ref[:, 2 * h * HEAD_DIM:(2 * h + 1) * HEAD_DIM] = jnp.where(lane < QK_DIM, kk, aug).astype(BF16)
        ka_ref[:, (2 * h + 1) * HEAD_DIM:(2 * h + 2) * HEAD_DIM] = jnp.where(
            lane < QK_DIM, pltpu.roll(kk, QK_DIM, axis=1), aug).astype(BF16)
        vt_ref[0, h * V_ROWS:h * V_ROWS + HEAD_DIM, :] = vtt[h * HEAD_DIM:(h + 1) * HEAD_DIM].astype(BF16)
        vt_ref[0, h * V_ROWS + HEAD_DIM:(h + 1) * V_ROWS, :] = ones_rows.astype(BF16)

    hprev = carry_h[0:1, :]
    for r0 in range(0, tm, SUBLANES):
        hg = u_s[r0:r0 + SUBLANES, :] + a_s[r0:r0 + SUBLANES, :] * hprev
        u_s[r0:r0 + SUBLANES, :] = hg
        hprev = hg[SUBLANES - 1:SUBLANES, :]
    carry_h[0:1, :] = hprev
    rnn_ref[...] = (u_s[...] * _gelu_tanh(p_lru[:, c_w:])).astype(rnn_ref.dtype)


def _gelu_tanh(x):
    k1 = -2.0 * math.sqrt(2.0 / math.pi) * LOG2E
    return x / (1.0 + jnp.exp2(x * (k1 + (k1 * 0.044715) * (x * x))))


def _sigmoid(x):
    return 0.5 * jnp.tanh(0.5 * x) + 0.5


def _alibi_slopes():
    nh = N_ATT_HEADS
    return jnp.asarray(np.array([2.0 ** (-8.0 * (i + 1) / nh) for i in range(nh)], dtype=np.float32))


def _in_proj(xf, norm_w, w_in, att_w, s_len, conv_w, conv_b, w_gates_bf, b_gates, lru_lambda):
    t, d = xf.shape
    n = w_in.shape[1]
    tm = min(ATT_TILE, t)
    ch = min(LRU_CHUNK, tm)
    nh = N_ATT_HEADS
    c_w = (n - 3 * att_w) // 2
    fix = lambda i: (0, 0)
    return pl.pallas_call(
        functools.partial(_inproj_kernel, att_w=att_w, s_len=s_len, ch=ch),
        grid=(t // tm,),
        in_specs=[pl.BlockSpec(memory_space=pltpu.SMEM),
                  pl.BlockSpec((tm, d), lambda i: (i, 0)),
                  pl.BlockSpec((1, d), fix),
                  pl.BlockSpec((d, n), fix, pipeline_mode=pl.Buffered(1)),
                  pl.BlockSpec((CONV_WIDTH, c_w), fix),
                  pl.BlockSpec((1, c_w), fix),
                  pl.BlockSpec((c_w, 2 * c_w), fix),
                  pl.BlockSpec((1, 2 * c_w), fix),
                  pl.BlockSpec((1, c_w), fix)],
        out_specs=[pl.BlockSpec((1, 2 * att_w, tm), lambda i: (i, 0, 0)),
                   pl.BlockSpec((tm, 2 * att_w), lambda i: (i, 0)),
                   pl.BlockSpec((1, nh * V_ROWS, tm), lambda i: (i, 0, 0)),
                   pl.BlockSpec((tm, c_w), lambda i: (i, 0))],
        out_shape=[jax.ShapeDtypeStruct((t // tm, 2 * att_w, tm), BF16),
                   jax.ShapeDtypeStruct((t, 2 * att_w), BF16),
                   jax.ShapeDtypeStruct((t // tm, nh * V_ROWS, tm), BF16),
                   jax.ShapeDtypeStruct((t, c_w), BF16)],
        scratch_shapes=[pltpu.VMEM((d, n), BF16),
                        pltpu.VMEM((tm + SUBLANES, c_w), F32), pltpu.VMEM((SUBLANES, c_w), F32),
                        pltpu.VMEM((tm, c_w), F32), pltpu.VMEM((tm, c_w), F32)],
        compiler_params=_cparams(("arbitrary",)),
        name="in_proj",
    )(_alibi_slopes(), xf, norm_w.reshape(1, d), w_in, conv_w, conv_b.reshape(1, c_w), w_gates_bf,
      b_gates.reshape(1, 2 * c_w), lru_lambda.reshape(1, c_w))


def _attn_kernel(lam_ref, hw_ref, q_ref, k_ref, vt, o_ref, sb, mx, acc, *, tq, n_heads, lambda_init):
    qi = pl.program_id(2)
    n_maps = 2 * n_heads
    mx[...] = jnp.full(mx.shape, NEG_BIG, F32)
    acc[...] = jnp.zeros(acc.shape, F32)

    def values(c, n, lanes=slice(None)):
        return vt[c, (n // 2) * V_ROWS:(n // 2 + 1) * V_ROWS, lanes]

    def scores(c, slot):
        rows = pl.ds(pl.multiple_of(c * tq, tq), tq)
        for n in range(n_maps):
            sb[n, slot] = jnp.dot(k_ref[0, rows, n * HEAD_DIM:(n + 1) * HEAD_DIM],
                                  q_ref[0, n * HEAD_DIM:(n + 1) * HEAD_DIM, :],
                                  preferred_element_type=F32)

    def softmax_pv(c, slot):
        for n in range(n_maps):
            s = sb[n, slot]
            m_prev = mx[n]
            m_new = jnp.maximum(m_prev, jnp.max(s, axis=0, keepdims=True))
            p = jnp.exp2(s - m_new).astype(BF16)
            acc[n] = jnp.exp2(m_prev - m_new) * acc[n] + jnp.dot(values(c, n), p, preferred_element_type=F32)
            mx[n] = m_new

    def softmax_pv_diagonal(c, slot):
        hq = tq // 2
        keep_t = (lax.broadcasted_iota(jnp.int32, (hq, tq), 0) <= lax.broadcasted_iota(jnp.int32, (hq, tq), 1))
        keep_b = (lax.broadcasted_iota(jnp.int32, (hq, hq), 0) <= lax.broadcasted_iota(jnp.int32, (hq, hq), 1))
        for n in range(n_maps):
            top = jnp.where(keep_t, sb[n, slot, :hq, :], NEG_BIG)
            bot = jnp.where(keep_b, sb[n, slot, hq:, hq:], NEG_BIG)
            mt = jnp.max(top, axis=0, keepdims=True)
            mb = jnp.max(bot, axis=0, keepdims=True)
            m_prev = mx[n]
            m_new = jnp.maximum(m_prev, jnp.concatenate([mt[:, :hq], jnp.maximum(mt[:, hq:], mb)], axis=1))
            p_top = jnp.exp2(top - m_new).astype(BF16)
            p_bot = jnp.exp2(bot - m_new[:, hq:]).astype(BF16)
            acc[n] = (jnp.exp2(m_prev - m_new) * acc[n]
                      + jnp.dot(values(c, n, slice(0, hq)), p_top, preferred_element_type=F32))
            acc[n, :, hq:] += jnp.dot(values(c, n, slice(hq, tq)), p_bot, preferred_element_type=F32)
            mx[n] = m_new

    scores(0, 0)

    def body(j, c):
        scores(2 * j + 1, 1)
        softmax_pv(2 * j, 0)
        scores(2 * j + 2, 0)
        softmax_pv(2 * j + 1, 1)
        return c

    lax.fori_loop(0, qi // 2, body, 0)

    @pl.when(qi % 2 == 0)
    def _():
        softmax_pv_diagonal(qi, 0)

    @pl.when(qi % 2 == 1)
    def _():
        scores(qi, 1)
        softmax_pv(qi - 1, 0)
        softmax_pv_diagonal(qi, 1)

    lam = (jnp.exp(jnp.sum(lam_ref[0:1, :] * lam_ref[1:2, :], axis=1, keepdims=True))
           - jnp.exp(jnp.sum(lam_ref[2:3, :] * lam_ref[3:4, :], axis=1, keepdims=True))
           + lambda_init)
    for hh in range(n_heads):
        o1 = acc[2 * hh, :HEAD_DIM, :] * (1.0 / acc[2 * hh, HEAD_DIM:HEAD_DIM + 1, :])
        o2 = acc[2 * hh + 1, :HEAD_DIM, :] * (1.0 / acc[2 * hh + 1, HEAD_DIM:HEAD_DIM + 1, :])
        o = o1 - lam * o2
        o = o * lax.rsqrt(jnp.mean(o * o, axis=0, keepdims=True) + HEAD_NORM_EPS)
        o_ref[0, :, hh * HEAD_DIM:(hh + 1) * HEAD_DIM] = (
            o.T * hw_ref[...] * (1.0 - lambda_init)).astype(o_ref.dtype)


def _diff_attention(qt, ka, vt, lam_params, head_norm_w, lambda_init, b, s):
    nh = N_ATT_HEADS
    hp = ATT_HEADS_PER_STEP
    tq = qt.shape[2]
    nq = s // tq
    return pl.pallas_call(
        functools.partial(_attn_kernel, tq=tq, n_heads=hp, lambda_init=lambda_init),
        grid=(b, nh // hp, nq),
        in_specs=[pl.BlockSpec((4, QK_DIM), lambda bi, hi, qi: (0, 0)),
                  pl.BlockSpec((1, HEAD_DIM), lambda bi, hi, qi: (0, 0)),
                  pl.BlockSpec((1, hp * 2 * HEAD_DIM, tq), lambda bi, hi, qi: (bi * nq + qi, hi, 0)),
                  pl.BlockSpec((1, s, hp * 2 * HEAD_DIM), lambda bi, hi, qi: (bi, 0, hi)),
                  pl.BlockSpec((nq, hp * V_ROWS, tq), lambda bi, hi, qi: (bi, hi, 0))],
        out_specs=pl.BlockSpec((1, tq, hp * HEAD_DIM), lambda bi, hi, qi: (bi, qi, hi)),
        out_shape=jax.ShapeDtypeStruct((b, s, nh * HEAD_DIM), BF16),
        scratch_shapes=[pltpu.VMEM((2 * hp, 2, tq, tq), F32), pltpu.VMEM((2 * hp, 1, tq), F32),
                        pltpu.VMEM((2 * hp, V_ROWS, tq), F32)],
        compiler_params=_cparams(("parallel", "parallel", "arbitrary"), vmem=56 * 1024 * 1024),
        name="diff_attn",
    )(lam_params, head_norm_w.reshape(1, HEAD_DIM), qt, ka.reshape(b, s, ka.shape[1]), vt)


def _outproj_kernel(att_ref, rnn_ref, x_ref, wo_ref, nw_ref, wrt_ref, brc_ref,
                    x1_ref, hn_ref, route_ref, route_t_ref, n8_ref, wob, *, att_w):
    @pl.when(pl.program_id(0) == 0)
    def _():
        wob[...] = wo_ref[...].astype(BF16)

    y = jnp.dot(att_ref[...], wob[:att_w, :], preferred_element_type=F32)
    y = y + jnp.dot(rnn_ref[...], wob[att_w:, :], preferred_element_type=F32)
    x1 = x_ref[...] + y
    x1_ref[...] = x1
    hn = (x1 * lax.rsqrt(jnp.mean(x1 * x1, axis=-1, keepdims=True) + NORM_EPS) * nw_ref[...]).astype(BF16)
    hn_ref[...] = hn
    tm = hn.shape[0]

    lg = lax.dot_general(wrt_ref[...], hn, (((1,), (1,)), ((), ())), preferred_element_type=F32)
    lg = lg[:ROUTE_ROWS] + brc_ref[:ROUTE_ROWS, 0:1]
    rowf = lax.broadcasted_iota(jnp.int32, lg.shape, 0).astype(F32)
    big = float(LANES)
    ninf = -jnp.inf
    is_g = rowf < N_GROUPS
    lgm = jnp.where(is_g, lg, ninf)
    mg = jnp.max(lgm, axis=0, keepdims=True)
    g_sel = jnp.min(jnp.where(lgm == mg, rowf, big), axis=0, keepdims=True)
    pg = 1.0 / jnp.sum(jnp.where(is_g, jnp.exp(lgm - mg), 0.0), axis=0, keepdims=True)
    lo = N_GROUPS + EXPERTS_PER_GROUP * g_sel
    in_grp = (rowf >= lo) & (rowf < lo + EXPERTS_PER_GROUP)
    lem = jnp.where(in_grp, lg, ninf)
    v1 = jnp.max(lem, axis=0, keepdims=True)
    i1 = jnp.min(jnp.where(lem == v1, rowf, big), axis=0, keepdims=True)
    lem2 = jnp.where(rowf == i1, ninf, lem)
    v2 = jnp.max(lem2, axis=0, keepdims=True)
    i2 = jnp.min(jnp.where(lem2 == v2, rowf, big), axis=0, keepdims=True)
    e2 = jnp.exp(v2 - v1)
    den = 1.0 + e2
    g1 = pg / den
    g2 = pg * e2 / den

    oh1 = jnp.where(rowf == i1, 1.0, 0.0)
    oh2 = jnp.where(rowf == i2, 1.0, 0.0)
    oh = oh1 + oh2
    earlier = (lax.broadcasted_iota(jnp.int32, (tm, tm), 0)
               < lax.broadcasted_iota(jnp.int32, (tm, tm), 1)).astype(BF16)
    pref = jnp.dot(oh.astype(BF16), earlier, preferred_element_type=F32)
    cnt = jnp.sum(oh, axis=1, keepdims=True)
    n8 = jnp.floor((cnt + (SEG_ROWS - 1)) * (1.0 / SEG_ROWS))
    n8_b = jnp.broadcast_to(n8, (ROUTE_ROWS, LANES))
    before = (lax.broadcasted_iota(jnp.int32, (ROUTE_ROWS, ROUTE_ROWS), 1)
              < lax.broadcasted_iota(jnp.int32, (ROUTE_ROWS, ROUTE_ROWS), 0)).astype(BF16)
    loff8 = jnp.dot(before, n8_b.astype(BF16), preferred_element_type=F32)[:, 0:1]
    pos = SEG_ROWS * loff8 + pref
    lp1 = jnp.sum(oh1 * pos, axis=0, keepdims=True)
    lp2 = jnp.sum(oh2 * pos, axis=0, keepdims=True)
    route_t = jnp.concatenate([g1, g2, lp1, lp2, jnp.zeros((LANES - 4, tm), F32)], axis=0)
    route_t_ref[0] = route_t[:SUBLANES]
    route_ref[...] = route_t.T
    n8_ref[0] = n8_b


def _out_proj(att, rnn, xf, w_out, norm_w, w_route_t_bf, b_route_col):
    t, d = xf.shape
    att_w = att.shape[1]
    tm = min(ROW_TILE, t)
    row = lambda i: (i, 0)
    fix = lambda i: (0, 0)
    return pl.pallas_call(
        functools.partial(_outproj_kernel, att_w=att_w),
        grid=(t // tm,),
        in_specs=[pl.BlockSpec((tm, att_w), row), pl.BlockSpec((tm, rnn.shape[1]), row),
                  pl.BlockSpec((tm, d), row), pl.BlockSpec(w_out.shape, fix, pipeline_mode=pl.Buffered(1)),
                  pl.BlockSpec((1, d), fix), pl.BlockSpec((LANES, d), fix), pl.BlockSpec((LANES, 1), fix)],
        out_specs=[pl.BlockSpec((tm, d), row), pl.BlockSpec((tm, d), row), pl.BlockSpec((tm, LANES), row),
                   pl.BlockSpec((1, SUBLANES, tm), lambda i: (i, 0, 0)),
                   pl.BlockSpec((1, ROUTE_ROWS, LANES), lambda i: (i, 0, 0))],
        out_shape=[jax.ShapeDtypeStruct((t, d), F32), jax.ShapeDtypeStruct((t, d), BF16),
                   jax.ShapeDtypeStruct((t, LANES), F32),
                   jax.ShapeDtypeStruct((t // tm, SUBLANES, tm), F32),
                   jax.ShapeDtypeStruct((t // tm, ROUTE_ROWS, LANES), F32)],
        scratch_shapes=[pltpu.VMEM(w_out.shape, BF16)],
        compiler_params=_cparams(("arbitrary",)),
        name="out_proj",
    )(att, rnn, xf, w_out, norm_w.reshape(1, d), w_route_t_bf, b_route_col)


def _local_rows(tm):
    return -(-(TOP_K * tm + N_EXPERTS * (SEG_ROWS - 1)) // LANES) * LANES


def _segment_tables(n8_tiles, tm_moe, n_tiles):
    n8 = n8_tiles[:, N_GROUPS:N_GROUPS + N_EXPERTS, 0].astype(jnp.int32)
    c8 = n8 * SEG_ROWS
    loff = jnp.cumsum(c8, axis=1) - c8
    gtot = jnp.sum(c8, axis=0)
    gpad = (gtot + tm_moe - 1) // tm_moe * tm_moe
    gend = jnp.cumsum(gpad)
    gstart = gend - gpad
    gbase = gstart[None, :] + jnp.cumsum(c8, axis=0) - c8
    tile_row0 = jnp.arange(n_tiles, dtype=jnp.int32) * tm_moe
    tile_e = jnp.minimum(jnp.sum((gend[None, :] <= tile_row0[:, None]).astype(jnp.int32), axis=1),
                         N_EXPERTS - 1).astype(jnp.int32)
    n_used = (gend[-1] // tm_moe).astype(jnp.int32).reshape(1)
    tail_start = (gstart + gtot).astype(jnp.int32)
    tail_n8 = ((gpad - gtot) // SEG_ROWS).astype(jnp.int32)
    after = gend[tile_e] // tm_moe
    next_e = jnp.where(after < n_used[0], tile_e[jnp.minimum(after, n_tiles - 1)], -1).astype(jnp.int32)
    first = jnp.concatenate([jnp.ones((1,), jnp.int32), (tile_e[1:] != tile_e[:-1]).astype(jnp.int32)])
    w_slot = ((jnp.cumsum(first) - 1) % 2).astype(jnp.int32)
    rows_in_tile = jnp.clip((gstart + gtot)[tile_e] - tile_row0, 1, tm_moe)
    tile_sub = ((rows_in_tile + MOE_SUBTILE - 1) // MOE_SUBTILE).astype(jnp.int32)
    return (n8.reshape(-1), loff.reshape(-1).astype(jnp.int32), gbase.reshape(-1).astype(jnp.int32),
            tile_e, n_used, tail_start, tail_n8, next_e, w_slot, tile_sub)


def _seg_aligned(rows):
    return pl.multiple_of(rows, SEG_ROWS) if SEG_ROWS > 1 else rows


def _tile_rows(n8_ref, loff_ref, tile):
    last = tile * N_EXPERTS + (N_EXPERTS - 1)
    return _seg_aligned(loff_ref[last] + n8_ref[last] * SEG_ROWS)


def _segment_copies(n8_ref, src_off_ref, dst_off_ref, src, dst, sem, tile, wait, loff_ref=None):
    def rows_of(e):
        return _seg_aligned(n8_ref[tile * N_EXPERTS + e] * SEG_ROWS)

    if wait:
        total = _tile_rows(n8_ref, loff_ref, tile)
        pltpu.make_async_copy(src.at[pl.ds(0, total), :], dst.at[pl.ds(0, total), :], sem).wait()
        return

    def per_expert(e, c):
        k = tile * N_EXPERTS + e
        rows = rows_of(e)

        @pl.when(rows > 0)
        def _():
            pltpu.make_async_copy(
                src.at[pl.ds(_seg_aligned(src_off_ref[k]), rows), :],
                dst.at[pl.ds(_seg_aligned(dst_off_ref[k]), rows), :], sem).start()
        return c
    lax.fori_loop(0, N_EXPERTS, per_expert, 0)


def _pack_bf16_pairs(x):
    n = x.shape[1] // 2
    bits = lax.bitcast_convert_type(x, jnp.uint32)
    return (bits[:, :n] >> 16) | (bits[:, n:] & jnp.uint32(0xFFFF0000))


def _unpack_bf16_pairs(p):
    lo = lax.bitcast_convert_type(p << 16, F32)
    hi = lax.bitcast_convert_type(p & jnp.uint32(0xFFFF0000), F32)
    return jnp.concatenate([lo, hi], axis=1).astype(BF16)


def _dispatch_kernel(n8_ref, loff_ref, gbase_ref, tstart_ref, tn8_ref, nu_ref, hn_ref, route_ref, xs_hbm,
                     stage, zbuf, sem, zsem, *, lcap, n_tt):
    i = pl.program_id(0)
    slot = i % 2
    tm = hn_ref.shape[0]
    tm_moe = zbuf.shape[0]
    n_tiles = xs_hbm.shape[0] // tm_moe

    def tail_copies(wait):
        def go(cp):
            if wait:
                cp.wait()
            else:
                cp.start()

        def per_expert(e, c):
            rows = _seg_aligned(tn8_ref[e] * SEG_ROWS)

            @pl.when(rows > 0)
            def _():
                go(pltpu.make_async_copy(
                    zbuf.at[pl.ds(0, rows), :],
                    xs_hbm.at[pl.ds(_seg_aligned(tstart_ref[e]), rows), :], zsem.at[0]))
            return c
        lax.fori_loop(0, N_EXPERTS, per_expert, 0)

        def per_unused_tile(j, c):
            go(pltpu.make_async_copy(zbuf, xs_hbm.at[pl.ds(pl.multiple_of(j * tm_moe, tm_moe), tm_moe), :],
                                     zsem.at[0]))
            return c
        lax.fori_loop(nu_ref[0], n_tiles, per_unused_tile, 0)

    @pl.when(i == 0)
    def _():
        zbuf[...] = jnp.zeros(zbuf.shape, zbuf.dtype)
        tail_copies(False)

    def segments(tile, sl, wait):
        _segment_copies(n8_ref, loff_ref, gbase_ref, stage.at[sl], xs_hbm, sem.at[sl], tile, wait,
                        loff_ref=loff_ref)

    @pl.when(i >= 2)
    def _():
        segments(i - 2, slot, True)

    def sort_rows(n_rows):
        lp1 = route_ref[0, 2:3, :]
        lp2 = route_ref[0, 3:4, :]
        rpos = lax.broadcasted_iota(jnp.int32, (n_rows, tm), 0).astype(F32)
        sel = jnp.where((rpos == lp1) | (rpos == lp2), 1.0, 0.0).astype(BF16)
        stage[slot, :n_rows, :] = _pack_bf16_pairs(jnp.dot(sel, hn_ref[...], preferred_element_type=F32))

    used = _tile_rows(n8_ref, loff_ref, i)
    short = lcap - LANES
    pl.when(used <= short)(functools.partial(sort_rows, short))
    pl.when(used > short)(functools.partial(sort_rows, lcap))
    segments(i, slot, False)

    @pl.when(i == n_tt - 1)
    def _():
        segments(i, slot, True)
        if n_tt > 1:
            segments(i - 1, 1 - slot, True)
        tail_copies(True)


def _dispatch(hn, route, tables, n_rows, tm_moe):
    t, d = hn.shape
    tm = min(ROW_TILE, t)
    n_tt = t // tm
    lcap = _local_rows(tm)
    n8, loff, gbase, _, n_used, tail_start, tail_n8 = tables[:7]
    grid_spec = pltpu.PrefetchScalarGridSpec(
        num_scalar_prefetch=6,
        grid=(n_tt,),
        in_specs=[pl.BlockSpec((tm, d), lambda i, *_: (i, 0)),
                  pl.BlockSpec((1, SUBLANES, tm), lambda i, *_: (i, 0, 0))],
        out_specs=pl.BlockSpec(memory_space=pl.ANY),
        scratch_shapes=[pltpu.VMEM((2, lcap, d // 2), jnp.uint32), pltpu.VMEM((tm_moe, d // 2), jnp.uint32),
                        pltpu.SemaphoreType.DMA((2,)), pltpu.SemaphoreType.DMA((1,))],
    )
    return pl.pallas_call(
        functools.partial(_dispatch_kernel, lcap=lcap, n_tt=n_tt),
        grid_spec=grid_spec,
        out_shape=jax.ShapeDtypeStruct((n_rows, d // 2), jnp.uint32),
        compiler_params=_cparams(("arbitrary",), has_side_effects=True),
        name="dispatch",
    )(n8, loff, gbase, tail_start, tail_n8, n_used, hn, route)


def _moe_kernel(te_ref, nu_ref, nxt_ref, wslot_ref, nsub_ref, xs_hbm, wg_hbm, wu_hbm, wd_hbm, y_hbm,
                xbuf, ybuf, zbuf, wgf, wuf, wdf, wgb, wub, wdb, xsem, ysem, zsem, wsem):
    tm = xbuf.shape[1]
    n_sub = tm // MOE_SUBTILE
    n_tiles = y_hbm.shape[0] // tm
    nu = nu_ref[0]

    def tile_rows(t):
        return pl.ds(pl.multiple_of(t * tm, tm), tm)

    def x_copy(t, sl):
        return pltpu.make_async_copy(xs_hbm.at[tile_rows(t), :], xbuf.at[sl], xsem.at[sl])

    def y_copy(t, sl):
        return pltpu.make_async_copy(ybuf.at[sl], y_hbm.at[tile_rows(t), :], ysem.at[sl])

    def zero_copy(t):
        return pltpu.make_async_copy(zbuf, y_hbm.at[tile_rows(t), :], zsem.at[0])

    def weight_copies(e, sl):
        return (pltpu.make_async_copy(wg_hbm.at[e], wgf.at[sl], wsem.at[sl, 0]),
                pltpu.make_async_copy(wu_hbm.at[e], wuf.at[sl], wsem.at[sl, 1]),
                pltpu.make_async_copy(wd_hbm.at[e], wdf.at[sl], wsem.at[sl, 2]))

    zbuf[...] = jnp.zeros(zbuf.shape, zbuf.dtype)

    def start_zero(t, c):
        zero_copy(t).start()
        return c
    lax.fori_loop(nu, n_tiles, start_zero, 0)

    for cp in weight_copies(te_ref[0], wslot_ref[0]):
        cp.start()
    x_copy(0, 0).start()

    def tile(i, c):
        sl = i % 2
        x_copy(i, sl).wait()

        @pl.when(i + 1 < nu)
        def _():
            x_copy(i + 1, 1 - sl).start()

        @pl.when(i >= 2)
        def _():
            y_copy(i - 2, sl).wait()

        changed = jnp.logical_or(i == 0, te_ref[i] != te_ref[jnp.maximum(i - 1, 0)])
        filled = nsub_ref[i]

        @pl.when(jnp.logical_and(changed, nxt_ref[i] >= 0))
        def _():
            for cp in weight_copies(nxt_ref[i], 1 - wslot_ref[i]):
                cp.start()

        def convert_weights():
            wsl = wslot_ref[i]
            for cp in weight_copies(te_ref[i], wsl):
                cp.wait()
            wgb[...] = wgf[wsl].astype(BF16)
            wub[...] = wuf[wsl].astype(BF16)
            wdb[...] = wdf[wsl].astype(BF16)

        def expert_mlp(k):
            r = k * MOE_SUBTILE
            x = _unpack_bf16_pairs(xbuf[sl, :r, :])
            g = jnp.dot(x, wgb[...], preferred_element_type=F32)
            u = jnp.dot(x, wub[...], preferred_element_type=F32)
            hdn = (g * jax.nn.sigmoid(g) * u).astype(BF16)
            y = jnp.dot(hdn, wdb[...], preferred_element_type=F32)
            ybuf[sl, :r, :] = _pack_bf16_pairs(y.astype(BF16).astype(F32))
            if r < tm:
                ybuf[sl, r:, :] = jnp.zeros((tm - r, ybuf.shape[2]), ybuf.dtype)

        @pl.when(jnp.logical_and(changed, filled < n_sub))
        def _():
            convert_weights()

        @pl.when(jnp.logical_and(changed, filled == n_sub))
        def _():
            convert_weights()
            expert_mlp(n_sub)

        for k in range(1, n_sub):
            @pl.when(filled == k)
            def _(k=k):
                expert_mlp(k)

        @pl.when(jnp.logical_and(jnp.logical_not(changed), filled == n_sub))
        def _():
            expert_mlp(n_sub)

        y_copy(i, sl).start()
        return c

    lax.fori_loop(0, nu, tile, 0)

    @pl.when(nu >= 2)
    def _():
        y_copy(nu - 2, nu % 2).wait()
    y_copy(nu - 1, (nu - 1) % 2).wait()

    def wait_zero(t, c):
        zero_copy(t).wait()
        return c
    lax.fori_loop(nu, n_tiles, wait_zero, 0)


def _moe(xs, tile_e, n_used, next_e, w_slot, tile_sub, w_g, w_u, w_d, tm):
    n_rows, dp = xs.shape
    d = w_g.shape[1]
    ff = w_g.shape[2]
    hbm = pl.BlockSpec(memory_space=pl.ANY)
    grid_spec = pltpu.PrefetchScalarGridSpec(
        num_scalar_prefetch=5,
        grid=(1,),
        in_specs=[hbm, hbm, hbm, hbm],
        out_specs=hbm,
        scratch_shapes=[pltpu.VMEM((2, tm, dp), jnp.uint32), pltpu.VMEM((2, tm, dp), jnp.uint32),
                        pltpu.VMEM((tm, dp), jnp.uint32),
                        pltpu.VMEM((2, d, ff), F32), pltpu.VMEM((2, d, ff), F32), pltpu.VMEM((2, ff, d), F32),
                        pltpu.VMEM((d, ff), BF16), pltpu.VMEM((d, ff), BF16), pltpu.VMEM((ff, d), BF16),
                        pltpu.SemaphoreType.DMA((2,)), pltpu.SemaphoreType.DMA((2,)),
                        pltpu.SemaphoreType.DMA((1,)), pltpu.SemaphoreType.DMA((2, 3))],
    )
    return pl.pallas_call(
        _moe_kernel,
        grid_spec=grid_spec,
        out_shape=jax.ShapeDtypeStruct((n_rows, dp), jnp.uint32),
        compiler_params=_cparams(("arbitrary",), has_side_effects=True),
        name="moe",
    )(tile_e, n_used, next_e, w_slot, tile_sub, xs, w_g, w_u, w_d)


def _combine_kernel(n8_ref, loff_ref, gbase_ref, x1_ref, route_ref, nw_ref, y_hbm, o_ref,
                    ybuf, sem, *, lcap, n_tt):
    i = pl.program_id(0)
    slot = i % 2
    tm = x1_ref.shape[0]

    def fetch(tile, sl, wait):
        _segment_copies(n8_ref, gbase_ref, loff_ref, y_hbm, ybuf.at[sl], sem.at[sl], tile, wait,
                        loff_ref=loff_ref)

    @pl.when(i == 0)
    def _():
        ybuf[...] = jnp.zeros(ybuf.shape, ybuf.dtype)
        fetch(0, 0, False)

    @pl.when(i + 1 < n_tt)
    def _():
        fetch(i + 1, 1 - slot, False)

    fetch(i, slot, True)
    def unsort_rows(n_rows):
        yb = _unpack_bf16_pairs(ybuf[slot, :n_rows, :])
        half = tm // 2
        for r0 in (0, half):
            rows = slice(r0, r0 + half)
            g1 = route_ref[rows, 0:1]
            g2 = route_ref[rows, 1:2]
            lp1 = route_ref[rows, 2:3]
            lp2 = route_ref[rows, 3:4]
            cpos = lax.broadcasted_iota(jnp.int32, (half, n_rows), 1).astype(F32)
            gsel = jnp.where(cpos == lp1, g1, jnp.where(cpos == lp2, g2, 0.0)).astype(BF16)
            x = x1_ref[rows, :] + jnp.dot(gsel, yb, preferred_element_type=F32)
            o_ref[rows, :] = x * lax.rsqrt(jnp.mean(x * x, axis=-1, keepdims=True) + NORM_EPS) * nw_ref[...]

    used = _tile_rows(n8_ref, loff_ref, i)
    short = lcap - LANES
    pl.when(used <= short)(functools.partial(unsort_rows, short))
    pl.when(used > short)(functools.partial(unsort_rows, lcap))


def _combine(x1, y, route, norm_w, tables):
    t, d = x1.shape
    tm = min(ROW_TILE, t)
    n_tt = t // tm
    lcap = _local_rows(tm)
    n8, loff, gbase = tables[:3]
    grid_spec = pltpu.PrefetchScalarGridSpec(
        num_scalar_prefetch=3,
        grid=(n_tt,),
        in_specs=[pl.BlockSpec((tm, d), lambda i, *_: (i, 0)),
                  pl.BlockSpec((tm, LANES), lambda i, *_: (i, 0)),
                  pl.BlockSpec((1, d), lambda i, *_: (0, 0)),
                  pl.BlockSpec(memory_space=pl.ANY)],
        out_specs=pl.BlockSpec((tm, d), lambda i, *_: (i, 0)),
        scratch_shapes=[pltpu.VMEM((2, lcap, d // 2), jnp.uint32), pltpu.SemaphoreType.DMA((2,))],
    )
    return pl.pallas_call(
        functools.partial(_combine_kernel, lcap=lcap, n_tt=n_tt),
        grid_spec=grid_spec,
        out_shape=jax.ShapeDtypeStruct((t, d), F32),
        compiler_params=_cparams(("arbitrary",)),
        name="combine",
    )(n8, loff, gbase, x1, route, norm_w.reshape(1, d), y)


def _block_diag(w):
    n, bi, bj = w.shape
    eye = jnp.eye(n, dtype=w.dtype)
    return jnp.einsum('nij,nm->nimj', w, eye).reshape(n * bi, n * bj)


def kernel(x, mix_norm_w, w_in, lambda_q1, lambda_k1, lambda_q2, lambda_k2, head_norm_w, conv_w, conv_b, w_rgate, b_rgate, w_igate, b_igate, lru_lambda, w_out, ffn_norm_w, w_router_group, b_router_group, w_router_expert, b_router_expert, w_exp_gate, w_exp_up, w_exp_down, final_norm_w):
    b, s, d = x.shape
    t = b * s
    assert w_in.shape[0] == 1, "single-layer stack only"
    att_w = N_ATT_HEADS * HEAD_DIM
    tm_moe = MOE_TILE
    xf = x.reshape(t, d)
    for l in range(1):
        lambda_init = 0.8 - 0.6 * math.exp(-0.3 * l)
        assert s % ATT_TILE == 0, "sequence length must be a multiple of the attention tile"
        w_bd = jnp.concatenate([_block_diag(w_rgate[l]), _block_diag(w_igate[l])], axis=1).astype(BF16)
        b_cat = jnp.concatenate([b_rgate[l], b_igate[l]])
        qt, ka, vt, rnn = _in_proj(xf, mix_norm_w[l], w_in[l], att_w, s,
                                   conv_w[l], conv_b[l], w_bd, b_cat, lru_lambda[l])
        lam_params = jnp.stack([lambda_q1[l], lambda_k1[l], lambda_q2[l], lambda_k2[l]]).astype(F32)
        att = _diff_attention(qt, ka, vt, lam_params, head_norm_w[l], lambda_init, b, s)
        w_route = jnp.concatenate([w_router_group[l], w_router_expert[l]], axis=1).T
        w_route = jnp.pad(w_route, ((0, LANES - w_route.shape[0]), (0, 0))).astype(BF16)
        b_route = jnp.concatenate([b_router_group[l], b_router_expert[l]])
        b_route = jnp.pad(b_route, (0, LANES - b_route.shape[0])).reshape(LANES, 1).astype(F32)
        x1, hn, route, route_t, n8_tiles = _out_proj(att.reshape(t, att_w), rnn.reshape(t, -1), xf,
                                                     w_out[l], ffn_norm_w[l], w_route, b_route)
        n_tt = n8_tiles.shape[0]
        max_rows = TOP_K * t + n_tt * N_EXPERTS * (SEG_ROWS - 1) + N_EXPERTS * (tm_moe - 1)
        n_tiles = -(-max_rows // tm_moe)
        tables = _segment_tables(n8_tiles, tm_moe, n_tiles)
        xs = _dispatch(hn, route_t, tables, n_tiles * tm_moe, tm_moe)
        y = _moe(xs, tables[3], tables[4], tables[7], tables[8], tables[9],
                 w_exp_gate[l], w_exp_up[l], w_exp_down[l], tm_moe)
        out = _combine(x1, y, route, final_norm_w, tables)
    return out.reshape(b, s, d)
```

```python
import functools
import math

import numpy as np
import jax
import jax.numpy as jnp
from jax import lax
from jax.experimental import pallas as pl
from jax.experimental.pallas import tpu as pltpu

F32 = jnp.float32
BF16 = jnp.bfloat16

N_ATT_HEADS = 4
HEAD_DIM = 128
QK_DIM = 64
N_RNN_BLOCKS = 8
CONV_WIDTH = 4
LRU_C = 8.0
N_GROUPS = 4
EXPERTS_PER_GROUP = 8
N_EXPERTS = N_GROUPS * EXPERTS_PER_GROUP
TOP_K = 2
NORM_EPS = 1e-6
HEAD_NORM_EPS = 1e-5
LOG2E = math.log2(math.e)


def _bf16_terms(x, n):
    terms = []
    for _ in range(n):
        t = float(np.float32(x).astype(jnp.bfloat16))
        terms.append(t)
        x -= t
    return tuple(terms)


LOG2E_TERMS = _bf16_terms(LOG2E, 3)
LANES = 128
SUBLANES = 8
SEG_ROWS = SUBLANES
NEG_BIG = -1e30

ROW_TILE = 512
ROUTE_ROWS = 48
ATT_TILE = 512
ATT_HEADS_PER_STEP = 4
V_ROWS = HEAD_DIM + 16
LRU_CHUNK = 128
MOE_TILE = 512
MOE_SUBTILE = 128
VMEM_LIMIT = 48 * 1024 * 1024


def _cparams(sem, vmem=VMEM_LIMIT, **kw):
    return pltpu.CompilerParams(dimension_semantics=sem, vmem_limit_bytes=vmem, **kw)


def _inproj_kernel(slope_ref, x_ref, nw_ref, w_ref, cw_ref, cb_ref, wg_ref, bg_ref, lam_ref,
                   qt_ref, ka_ref, vt_ref, rnn_ref, wb, xs, carry_h, a_s, u_s, *, att_w, s_len, ch):
    i = pl.program_id(0)
    x = x_ref[...]
    tm = x.shape[0]
    c_w = (w_ref.shape[1] - 3 * att_w) // 2

    @pl.when(i == 0)
    def _():
        wb[...] = w_ref[...].astype(BF16)

    @pl.when((i * tm) % s_len == 0)
    def _():
        xs[0:SUBLANES, :] = jnp.zeros((SUBLANES, c_w), F32)
        carry_h[...] = jnp.zeros(carry_h.shape, F32)

    ms = jnp.mean(x * x, axis=-1, keepdims=True)
    hn = (x * lax.rsqrt(ms + NORM_EPS) * nw_ref[...]).astype(BF16)
    p_lru = jnp.dot(hn, wb[:, 3 * att_w:], preferred_element_type=F32)

    xs[SUBLANES:, :] = p_lru[:, :c_w]
    neg_lam = -lam_ref[...]
    sp = jnp.maximum(neg_lam, 0.0) + jnp.log1p(jnp.exp(-jnp.abs(neg_lam)))
    cw = cw_ref[...]
    cb = cb_ref[...]
    bias = bg_ref[...]
    r8 = lax.broadcasted_iota(jnp.int32, (ch // SUBLANES, SUBLANES, c_w), 1)
    xcs, zs = [], []
    for c in range(tm // ch):
        r0 = c * ch
        win = xs[r0:r0 + ch + SUBLANES, :]
        xc = cw[3:4, :] * win[SUBLANES:] + cb
        for k in (1, 2, 3):
            xc = xc + cw[3 - k:4 - k, :] * pltpu.roll(win, k, axis=0)[SUBLANES:]
        xcs.append(xc)
        zs.append(jnp.dot(xc.astype(BF16), wg_ref[...], preferred_element_type=F32) + bias)
    xs[0:SUBLANES, :] = xs[tm:tm + SUBLANES, :]

    p_q = jnp.dot(hn, wb[:, :att_w], preferred_element_type=F32)
    p_v = jnp.dot(hn, wb[:, 2 * att_w:3 * att_w], preferred_element_type=F32)
    p_k = jnp.dot(hn, wb[:, att_w:2 * att_w], preferred_element_type=F32)

    for c in range(tm // ch):
        r0 = c * ch
        xc, z = xcs[c], zs[c]
        r = _sigmoid(z[:, :c_w])
        ig = _sigmoid(z[:, c_w:])
        log_a = (-LRU_C) * r * sp
        a = jnp.exp(log_a)
        w = jnp.tanh(-log_a) * (1.0 + a * a)
        u = jnp.where(w > 0.0, w * lax.rsqrt(w), 0.0) * ig * xc
        a = a.reshape(ch // SUBLANES, SUBLANES, c_w)
        u = u.reshape(ch // SUBLANES, SUBLANES, c_w)
        for k in (1, 2, 4):
            a_sh = pltpu.roll(a, k, axis=1)
            u_sh = pltpu.roll(u, k, axis=1)
            ok = r8 >= k
            u = jnp.where(ok, u + a * u_sh, u)
            a = jnp.where(ok, a * a_sh, a)
        a_s[r0:r0 + ch, :] = a.reshape(ch, c_w)
        u_s[r0:r0 + ch, :] = u.reshape(ch, c_w)

    qt = (p_q * (QK_DIM ** -0.5 * LOG2E)).T
    arow = lax.broadcasted_iota(jnp.int32, (QK_DIM, tm), 0)
    ones2 = jnp.where(arow < 2, LOG2E_TERMS[0], jnp.where(arow < 4, LOG2E_TERMS[1],
                                                          jnp.where(arow < 6, LOG2E_TERMS[2], 0.0)))
    pieces = []
    for g in range(2 * N_ATT_HEADS):
        pieces += [qt[g * QK_DIM:(g + 1) * QK_DIM], ones2]
    qt_ref[0] = jnp.concatenate(pieces, axis=0).astype(BF16)

    lane = lax.broadcasted_iota(jnp.int32, (tm, HEAD_DIM), 1)
    j = (i * tm) % s_len + lax.broadcasted_iota(jnp.int32, (tm, HEAD_DIM), 0)
    j_lo = (j & 255).astype(F32)
    j_hi = (j - (j & 255)).astype(F32)
    vtt = p_v.T
    ones_rows = jnp.where(lax.broadcasted_iota(jnp.int32, (V_ROWS - HEAD_DIM, tm), 0) == 0, 1.0, 0.0)
    for h in range(N_ATT_HEADS):
        slope = slope_ref[h]
        kk = p_k[:, h * HEAD_DIM:(h + 1) * HEAD_DIM]
        in_aug = (lane >= QK_DIM) & (lane < QK_DIM + 2 * len(LOG2E_TERMS))
        aug = jnp.where(in_aug, jnp.where((lane & 1) == 0, slope * j_hi, slope * j_lo), 0.0)
        ka_ref[:, 2 * h * HEAD_DIM:(2 * h + 1) * HEAD_DIM] = jnp.where(lane < QK_DIM, kk, aug).astype(BF16)
        ka_ref[:, (2 * h + 1) * HEAD_DIM:(2 * h + 2) * HEAD_DIM] = jnp.where(
            lane < QK_DIM, pltpu.roll(kk, QK_DIM, axis=1), aug).astype(BF16)
        vt_ref[0, h * V_ROWS:h * V_ROWS + HEAD_DIM, :] = vtt[h * HEAD_DIM:(h + 1) * HEAD_DIM].astype(BF16)
        vt_ref[0, h * V_ROWS + HEAD_DIM:(h + 1) * V_ROWS, :] = ones_rows.astype(BF16)

    hprev = carry_h[0:1, :]
    for r0 in range(0, tm, SUBLANES):
        hg = u_s[r0:r0 + SUBLANES, :] + a_s[r0:r0 + SUBLANES, :] * hprev
        u_s[r0:r0 + SUBLANES, :] = hg
        hprev = hg[SUBLANES - 1:SUBLANES, :]
    carry_h[0:1, :] = hprev
    rnn_ref[...] = (u_s[...] * _gelu_tanh(p_lru[:, c_w:])).astype(rnn_ref.dtype)


def _gelu_tanh(x):
    k1 = -2.0 * math.sqrt(2.0 / math.pi) * LOG2E
    return x / (1.0 + jnp.exp2(x * (k1 + (k1 * 0.044715) * (x * x))))


def _sigmoid(x):
    return 0.5 * jnp.tanh(0.5 * x) + 0.5


def _alibi_slopes():
    nh = N_ATT_HEADS
    return jnp.asarray(np.array([2.0 ** (-8.0 * (i + 1) / nh) for i in range(nh)], dtype=np.float32))


def _in_proj(xf, norm_w, w_in, att_w, s_len, conv_w, conv_b, w_gates_bf, b_gates, lru_lambda):
    t, d = xf.shape
    n = w_in.shape[1]
    tm = min(ATT_TILE, t)
    ch = min(LRU_CHUNK, tm)
    nh = N_ATT_HEADS
    c_w = (n - 3 * att_w) // 2
    fix = lambda i: (0, 0)
    return pl.pallas_call(
        functools.partial(_inproj_kernel, att_w=att_w, s_len=s_len, ch=ch),
        grid=(t // tm,),
        in_specs=[pl.BlockSpec(memory_space=pltpu.SMEM),
                  pl.BlockSpec((tm, d), lambda i: (i, 0)),
                  pl.BlockSpec((1, d), fix),
                  pl.BlockSpec((d, n), fix, pipeline_mode=pl.Buffered(1)),
                  pl.BlockSpec((CONV_WIDTH, c_w), fix),
                  pl.BlockSpec((1, c_w), fix),
                  pl.BlockSpec((c_w, 2 * c_w), fix),
                  pl.BlockSpec((1, 2 * c_w), fix),
                  pl.BlockSpec((1, c_w), fix)],
        out_specs=[pl.BlockSpec((1, 2 * att_w, tm), lambda i: (i, 0, 0)),
                   pl.BlockSpec((tm, 2 * att_w), lambda i: (i, 0)),
                   pl.BlockSpec((1, nh * V_ROWS, tm), lambda i: (i, 0, 0)),
                   pl.BlockSpec((tm, c_w), lambda i: (i, 0))],
        out_shape=[jax.ShapeDtypeStruct((t // tm, 2 * att_w, tm), BF16),
                   jax.ShapeDtypeStruct((t, 2 * att_w), BF16),
                   jax.ShapeDtypeStruct((t // tm, nh * V_ROWS, tm), BF16),
                   jax.ShapeDtypeStruct((t, c_w), BF16)],
        scratch_shapes=[pltpu.VMEM((d, n), BF16),
                        pltpu.VMEM((tm + SUBLANES, c_w), F32), pltpu.VMEM((SUBLANES, c_w), F32),
                        pltpu.VMEM((tm, c_w), F32), pltpu.VMEM((tm, c_w), F32)],
        compiler_params=_cparams(("arbitrary",)),
        name="in_proj",
    )(_alibi_slopes(), xf, norm_w.reshape(1, d), w_in, conv_w, conv_b.reshape(1, c_w), w_gates_bf,
      b_gates.reshape(1, 2 * c_w), lru_lambda.reshape(1, c_w))


def _attn_kernel(lam_ref, hw_ref, q_ref, k_ref, vt, o_ref, sb, mx, acc, *, tq, n_heads, lambda_init):
    qi = pl.program_id(2)
    n_maps = 2 * n_heads
    mx[...] = jnp.full(mx.shape, NEG_BIG, F32)
    acc[...] = jnp.zeros(acc.shape, F32)

    def values(c, n, lanes=slice(None)):
        return vt[c, (n // 2) * V_ROWS:(n // 2 + 1) * V_ROWS, lanes]

    def scores(c, slot):
        rows = pl.ds(pl.multiple_of(c * tq, tq), tq)
        for n in range(n_maps):
            sb[n, slot] = jnp.dot(k_ref[0, rows, n * HEAD_DIM:(n + 1) * HEAD_DIM],
                                  q_ref[0, n * HEAD_DIM:(n + 1) * HEAD_DIM, :],
                                  preferred_element_type=F32)

    def softmax_pv(c, slot):
        for n in range(n_maps):
            s = sb[n, slot]
            m_prev = mx[n]
            m_new = jnp.maximum(m_prev, jnp.max(s, axis=0, keepdims=True))
            p = jnp.exp2(s - m_new).astype(BF16)
            acc[n] = jnp.exp2(m_prev - m_new) * acc[n] + jnp.dot(values(c, n), p, preferred_element_type=F32)
            mx[n] = m_new

    def softmax_pv_diagonal(c, slot):
        hq = tq // 2
        keep_t = (lax.broadcasted_iota(jnp.int32, (hq, tq), 0) <= lax.broadcasted_iota(jnp.int32, (hq, tq), 1))
        keep_b = (lax.broadcasted_iota(jnp.int32, (hq, hq), 0) <= lax.broadcasted_iota(jnp.int32, (hq, hq), 1))
        for n in range(n_maps):
            top = jnp.where(keep_t, sb[n, slot, :hq, :], NEG_BIG)
            bot = jnp.where(keep_b, sb[n, slot, hq:, hq:], NEG_BIG)
            mt = jnp.max(top, axis=0, keepdims=True)
            mb = jnp.max(bot, axis=0, keepdims=True)
            m_prev = mx[n]
            m_new = jnp.maximum(m_prev, jnp.concatenate([mt[:, :hq], jnp.maximum(mt[:, hq:], mb)], axis=1))
            p_top = jnp.exp2(top - m_new).astype(BF16)
            p_bot = jnp.exp2(bot - m_new[:, hq:]).astype(BF16)
            acc[n] = (jnp.exp2(m_prev - m_new) * acc[n]
                      + jnp.dot(values(c, n, slice(0, hq)), p_top, preferred_element_type=F32))
            acc[n, :, hq:] += jnp.dot(values(c, n, slice(hq, tq)), p_bot, preferred_element_type=F32)
            mx[n] = m_new

    scores(0, 0)

    def body(j, c):
        scores(2 * j + 1, 1)
        softmax_pv(2 * j, 0)
        scores(2 * j + 2, 0)
        softmax_pv(2 * j + 1, 1)
        return c

    lax.fori_loop(0, qi // 2, body, 0)

    @pl.when(qi % 2 == 0)
    def _():
        softmax_pv_diagonal(qi, 0)

    @pl.when(qi % 2 == 1)
    def _():
        scores(qi, 1)
        softmax_pv(qi - 1, 0)
        softmax_pv_diagonal(qi, 1)

    lam = (jnp.exp(jnp.sum(lam_ref[0:1, :] * lam_ref[1:2, :], axis=1, keepdims=True))
           - jnp.exp(jnp.sum(lam_ref[2:3, :] * lam_ref[3:4, :], axis=1, keepdims=True))
           + lambda_init)
    for hh in range(n_heads):
        o1 = acc[2 * hh, :HEAD_DIM, :] * (1.0 / acc[2 * hh, HEAD_DIM:HEAD_DIM + 1, :])
        o2 = acc[2 * hh + 1, :HEAD_DIM, :] * (1.0 / acc[2 * hh + 1, HEAD_DIM:HEAD_DIM + 1, :])
        o = o1 - lam * o2
        o = o * lax.rsqrt(jnp.mean(o * o, axis=0, keepdims=True) + HEAD_NORM_EPS)
        o_ref[0, :, hh * HEAD_DIM:(hh + 1) * HEAD_DIM] = (
            o.T * hw_ref[...] * (1.0 - lambda_init)).astype(o_ref.dtype)


def _diff_attention(qt, ka, vt, lam_params, head_norm_w, lambda_init, b, s):
    nh = N_ATT_HEADS
    hp = ATT_HEADS_PER_STEP
    tq = qt.shape[2]
    nq = s // tq
    return pl.pallas_call(
        functools.partial(_attn_kernel, tq=tq, n_heads=hp, lambda_init=lambda_init),
        grid=(b, nh // hp, nq),
        in_specs=[pl.BlockSpec((4, QK_DIM), lambda bi, hi, qi: (0, 0)),
                  pl.BlockSpec((1, HEAD_DIM), lambda bi, hi, qi: (0, 0)),
                  pl.BlockSpec((1, hp * 2 * HEAD_DIM, tq), lambda bi, hi, qi: (bi * nq + qi, hi, 0)),
                  pl.BlockSpec((1, s, hp * 2 * HEAD_DIM), lambda bi, hi, qi: (bi, 0, hi)),
                  pl.BlockSpec((nq, hp * V_ROWS, tq), lambda bi, hi, qi: (bi, hi, 0))],
        out_specs=pl.BlockSpec((1, tq, hp * HEAD_DIM), lambda bi, hi, qi: (bi, qi, hi)),
        out_shape=jax.ShapeDtypeStruct((b, s, nh * HEAD_DIM), BF16),
        scratch_shapes=[pltpu.VMEM((2 * hp, 2, tq, tq), F32), pltpu.VMEM((2 * hp, 1, tq), F32),
                        pltpu.VMEM((2 * hp, V_ROWS, tq), F32)],
        compiler_params=_cparams(("parallel", "parallel", "arbitrary"), vmem=56 * 1024 * 1024),
        name="diff_attn",
    )(lam_params, head_norm_w.reshape(1, HEAD_DIM), qt, ka.reshape(b, s, ka.shape[1]), vt)


def _outproj_kernel(att_ref, rnn_ref, x_ref, wo_ref, nw_ref, wrt_ref, brc_ref,
                    x1_ref, hn_ref, route_ref, route_t_ref, n8_ref, wob, *, att_w):
    @pl.when(pl.program_id(0) == 0)
    def _():
        wob[...] = wo_ref[...].astype(BF16)

    y = jnp.dot(att_ref[...], wob[:att_w, :], preferred_element_type=F32)
    y = y + jnp.dot(rnn_ref[...], wob[att_w:, :], preferred_element_type=F32)
    x1 = x_ref[...] + y
    x1_ref[...] = x1
    hn = (x1 * lax.rsqrt(jnp.mean(x1 * x1, axis=-1, keepdims=True) + NORM_EPS) * nw_ref[...]).astype(BF16)
    hn_ref[...] = hn
    tm = hn.shape[0]

    lg = lax.dot_general(wrt_ref[...], hn, (((1,), (1,)), ((), ())), preferred_element_type=F32)
    lg = lg[:ROUTE_ROWS] + brc_ref[:ROUTE_ROWS, 0:1]
    rowf = lax.broadcasted_iota(jnp.int32, lg.shape, 0).astype(F32)
    big = float(LANES)
    ninf = -jnp.inf
    is_g = rowf < N_GROUPS
    lgm = jnp.where(is_g, lg, ninf)
    mg = jnp.max(lgm, axis=0, keepdims=True)
    g_sel = jnp.min(jnp.where(lgm == mg, rowf, big), axis=0, keepdims=True)
    pg = 1.0 / jnp.sum(jnp.where(is_g, jnp.exp(lgm - mg), 0.0), axis=0, keepdims=True)
    lo = N_GROUPS + EXPERTS_PER_GROUP * g_sel
    in_grp = (rowf >= lo) & (rowf < lo + EXPERTS_PER_GROUP)
    lem = jnp.where(in_grp, lg, ninf)
    v1 = jnp.max(lem, axis=0, keepdims=True)
    i1 = jnp.min(jnp.where(lem == v1, rowf, big), axis=0, keepdims=True)
    lem2 = jnp.where(rowf == i1, ninf, lem)
    v2 = jnp.max(lem2, axis=0, keepdims=True)
    i2 = jnp.min(jnp.where(lem2 == v2, rowf, big), axis=0, keepdims=True)
    e2 = jnp.exp(v2 - v1)
    den = 1.0 + e2
    g1 = pg / den
    g2 = pg * e2 / den

    oh1 = jnp.where(rowf == i1, 1.0, 0.0)
    oh2 = jnp.where(rowf == i2, 1.0, 0.0)
    oh = oh1 + oh2
    earlier = (lax.broadcasted_iota(jnp.int32, (tm, tm), 0)
               < lax.broadcasted_iota(jnp.int32, (tm, tm), 1)).astype(BF16)
    pref = jnp.dot(oh.astype(BF16), earlier, preferred_element_type=F32)
    cnt = jnp.sum(oh, axis=1, keepdims=True)
    n8 = jnp.floor((cnt + (SEG_ROWS - 1)) * (1.0 / SEG_ROWS))
    n8_b = jnp.broadcast_to(n8, (ROUTE_ROWS, LANES))
    before = (lax.broadcasted_iota(jnp.int32, (ROUTE_ROWS, ROUTE_ROWS), 1)
              < lax.broadcasted_iota(jnp.int32, (ROUTE_ROWS, ROUTE_ROWS), 0)).astype(BF16)
    loff8 = jnp.dot(before, n8_b.astype(BF16), preferred_element_type=F32)[:, 0:1]
    pos = SEG_ROWS * loff8 + pref
    lp1 = jnp.sum(oh1 * pos, axis=0, keepdims=True)
    lp2 = jnp.sum(oh2 * pos, axis=0, keepdims=True)
    route_t = jnp.concatenate([g1, g2, lp1, lp2, jnp.zeros((LANES - 4, tm), F32)], axis=0)
    route_t_ref[0] = route_t[:SUBLANES]
    route_ref[...] = route_t.T
    n8_ref[0] = n8_b


def _out_proj(att, rnn, xf, w_out, norm_w, w_route_t_bf, b_route_col):
    t, d = xf.shape
    att_w = att.shape[1]
    tm = min(ROW_TILE, t)
    row = lambda i: (i, 0)
    fix = lambda i: (0, 0)
    return pl.pallas_call(
        functools.partial(_outproj_kernel, att_w=att_w),
        grid=(t // tm,),
        in_specs=[pl.BlockSpec((tm, att_w), row), pl.BlockSpec((tm, rnn.shape[1]), row),
                  pl.BlockSpec((tm, d), row), pl.BlockSpec(w_out.shape, fix, pipeline_mode=pl.Buffered(1)),
                  pl.BlockSpec((1, d), fix), pl.BlockSpec((LANES, d), fix), pl.BlockSpec((LANES, 1), fix)],
        out_specs=[pl.BlockSpec((tm, d), row), pl.BlockSpec((tm, d), row), pl.BlockSpec((tm, LANES), row),
                   pl.BlockSpec((1, SUBLANES, tm), lambda i: (i, 0, 0)),
                   pl.BlockSpec((1, ROUTE_ROWS, LANES), lambda i: (i, 0, 0))],
        out_shape=[jax.ShapeDtypeStruct((t, d), F32), jax.ShapeDtypeStruct((t, d), BF16),
                   jax.ShapeDtypeStruct((t, LANES), F32),
                   jax.ShapeDtypeStruct((t // tm, SUBLANES, tm), F32),
                   jax.ShapeDtypeStruct((t // tm, ROUTE_ROWS, LANES), F32)],
        scratch_shapes=[pltpu.VMEM(w_out.shape, BF16)],
        compiler_params=_cparams(("arbitrary",)),
        name="out_proj",
    )(att, rnn, xf, w_out, norm_w.reshape(1, d), w_route_t_bf, b_route_col)


def _local_rows(tm):
    return -(-(TOP_K * tm + N_EXPERTS * (SEG_ROWS - 1)) // LANES) * LANES


def _segment_tables(n8_tiles, tm_moe, n_tiles):
    n8 = n8_tiles[:, N_GROUPS:N_GROUPS + N_EXPERTS, 0].astype(jnp.int32)
    c8 = n8 * SEG_ROWS
    loff = jnp.cumsum(c8, axis=1) - c8
    gtot = jnp.sum(c8, axis=0)
    gpad = (gtot + tm_moe - 1) // tm_moe * tm_moe
    gend = jnp.cumsum(gpad)
    gstart = gend - gpad
    gbase = gstart[None, :] + jnp.cumsum(c8, axis=0) - c8
    tile_row0 = jnp.arange(n_tiles, dtype=jnp.int32) * tm_moe
    tile_e = jnp.minimum(jnp.sum((gend[None, :] <= tile_row0[:, None]).astype(jnp.int32), axis=1),
                         N_EXPERTS - 1).astype(jnp.int32)
    n_used = (gend[-1] // tm_moe).astype(jnp.int32).reshape(1)
    tail_start = (gstart + gtot).astype(jnp.int32)
    tail_n8 = ((gpad - gtot) // SEG_ROWS).astype(jnp.int32)
    after = gend[tile_e] // tm_moe
    next_e = jnp.where(after < n_used[0], tile_e[jnp.minimum(after, n_tiles - 1)], -1).astype(jnp.int32)
    first = jnp.concatenate([jnp.ones((1,), jnp.int32), (tile_e[1:] != tile_e[:-1]).astype(jnp.int32)])
    w_slot = ((jnp.cumsum(first) - 1) % 2).astype(jnp.int32)
    rows_in_tile = jnp.clip((gstart + gtot)[tile_e] - tile_row0, 1, tm_moe)
    tile_sub = ((rows_in_tile + MOE_SUBTILE - 1) // MOE_SUBTILE).astype(jnp.int32)
    return (n8.reshape(-1), loff.reshape(-1).astype(jnp.int32), gbase.reshape(-1).astype(jnp.int32),
            tile_e, n_used, tail_start, tail_n8, next_e, w_slot, tile_sub)


def _seg_aligned(rows):
    return pl.multiple_of(rows, SEG_ROWS) if SEG_ROWS > 1 else rows


def _tile_rows(n8_ref, loff_ref, tile):
    last = tile * N_EXPERTS + (N_EXPERTS - 1)
    return _seg_aligned(loff_ref[last] + n8_ref[last] * SEG_ROWS)


def _segment_copies(n8_ref, src_off_ref, dst_off_ref, src, dst, sem, tile, wait, loff_ref=None):
    def rows_of(e):
        return _seg_aligned(n8_ref[tile * N_EXPERTS + e] * SEG_ROWS)

    if wait:
        total = _tile_rows(n8_ref, loff_ref, tile)
        pltpu.make_async_copy(src.at[pl.ds(0, total), :], dst.at[pl.ds(0, total), :], sem).wait()
        return

    def per_expert(e, c):
        k = tile * N_EXPERTS + e
        rows = rows_of(e)

        @pl.when(rows > 0)
        def _():
            pltpu.make_async_copy(
                src.at[pl.ds(_seg_aligned(src_off_ref[k]), rows), :],
                dst.at[pl.ds(_seg_aligned(dst_off_ref[k]), rows), :], sem).start()
        return c
    lax.fori_loop(0, N_EXPERTS, per_expert, 0, unroll=True)


def _pack_bf16_pairs(x):
    n = x.shape[1] // 2
    bits = lax.bitcast_convert_type(x, jnp.uint32)
    return (bits[:, :n] >> 16) | (bits[:, n:] & jnp.uint32(0xFFFF0000))


def _unpack_bf16_pairs(p):
    lo = lax.bitcast_convert_type(p << 16, F32)
    hi = lax.bitcast_convert_type(p & jnp.uint32(0xFFFF0000), F32)
    return jnp.concatenate([lo, hi], axis=1).astype(BF16)


def _dispatch_kernel(n8_ref, loff_ref, gbase_ref, tstart_ref, tn8_ref, nu_ref, hn_ref, route_ref, xs_hbm,
                     stage, zbuf, sem, zsem, *, lcap, n_tt):
    i = pl.program_id(0)
    slot = i % 2
    tm = hn_ref.shape[0]
    tm_moe = zbuf.shape[0]
    n_tiles = xs_hbm.shape[0] // tm_moe

    def tail_copies(wait):
        def go(cp):
            if wait:
                cp.wait()
            else:
                cp.start()

        def per_expert(e, c):
            rows = _seg_aligned(tn8_ref[e] * SEG_ROWS)

            @pl.when(rows > 0)
            def _():
                go(pltpu.make_async_copy(
                    zbuf.at[pl.ds(0, rows), :],
                    xs_hbm.at[pl.ds(_seg_aligned(tstart_ref[e]), rows), :], zsem.at[0]))
            return c
        lax.fori_loop(0, N_EXPERTS, per_expert, 0)

        def per_unused_tile(j, c):
            go(pltpu.make_async_copy(zbuf, xs_hbm.at[pl.ds(pl.multiple_of(j * tm_moe, tm_moe), tm_moe), :],
                                     zsem.at[0]))
            return c
        lax.fori_loop(nu_ref[0], n_tiles, per_unused_tile, 0)

    @pl.when(i == 0)
    def _():
        zbuf[...] = jnp.zeros(zbuf.shape, zbuf.dtype)
        tail_copies(False)

    def segments(tile, sl, wait):
        _segment_copies(n8_ref, loff_ref, gbase_ref, stage.at[sl], xs_hbm, sem.at[sl], tile, wait,
                        loff_ref=loff_ref)

    @pl.when(i >= 2)
    def _():
        segments(i - 2, slot, True)

    def sort_rows(n_rows):
        lp1 = route_ref[0, 2:3, :]
        lp2 = route_ref[0, 3:4, :]
        rpos = lax.broadcasted_iota(jnp.int32, (n_rows, tm), 0).astype(F32)
        sel = jnp.where((rpos == lp1) | (rpos == lp2), 1.0, 0.0).astype(BF16)
        stage[slot, :n_rows, :] = _pack_bf16_pairs(jnp.dot(sel, hn_ref[...], preferred_element_type=F32))

    used = _tile_rows(n8_ref, loff_ref, i)
    short = lcap - LANES
    pl.when(used <= short)(functools.partial(sort_rows, short))
    pl.when(used > short)(functools.partial(sort_rows, lcap))
    segments(i, slot, False)

    @pl.when(i == n_tt - 1)
    def _():
        segments(i, slot, True)
        if n_tt > 1:
            segments(i - 1, 1 - slot, True)
        tail_copies(True)


def _dispatch(hn, route, tables, n_rows, tm_moe):
    t, d = hn.shape
    tm = min(ROW_TILE, t)
    n_tt = t // tm
    lcap = _local_rows(tm)
    n8, loff, gbase, _, n_used, tail_start, tail_n8 = tables[:7]
    grid_spec = pltpu.PrefetchScalarGridSpec(
        num_scalar_prefetch=6,
        grid=(n_tt,),
        in_specs=[pl.BlockSpec((tm, d), lambda i, *_: (i, 0)),
                  pl.BlockSpec((1, SUBLANES, tm), lambda i, *_: (i, 0, 0))],
        out_specs=pl.BlockSpec(memory_space=pl.ANY),
        scratch_shapes=[pltpu.VMEM((2, lcap, d // 2), jnp.uint32), pltpu.VMEM((tm_moe, d // 2), jnp.uint32),
                        pltpu.SemaphoreType.DMA((2,)), pltpu.SemaphoreType.DMA((1,))],
    )
    return pl.pallas_call(
        functools.partial(_dispatch_kernel, lcap=lcap, n_tt=n_tt),
        grid_spec=grid_spec,
        out_shape=jax.ShapeDtypeStruct((n_rows, d // 2), jnp.uint32),
        compiler_params=_cparams(("arbitrary",), has_side_effects=True),
        name="dispatch",
    )(n8, loff, gbase, tail_start, tail_n8, n_used, hn, route)


def _moe_kernel(te_ref, nu_ref, nxt_ref, wslot_ref, nsub_ref, xs_hbm, wg_hbm, wu_hbm, wd_hbm, y_hbm,
                xbuf, ybuf, zbuf, wgf, wuf, wdf, wgb, wub, wdb, xsem, ysem, zsem, wsem):
    tm = xbuf.shape[1]
    n_sub = tm // MOE_SUBTILE
    n_tiles = y_hbm.shape[0] // tm
    nu = nu_ref[0]

    def tile_rows(t):
        return pl.ds(pl.multiple_of(t * tm, tm), tm)

    def x_copy(t, sl):
        return pltpu.make_async_copy(xs_hbm.at[tile_rows(t), :], xbuf.at[sl], xsem.at[sl])

    def y_copy(t, sl):
        return pltpu.make_async_copy(ybuf.at[sl], y_hbm.at[tile_rows(t), :], ysem.at[sl])

    def zero_copy(t):
        return pltpu.make_async_copy(zbuf, y_hbm.at[tile_rows(t), :], zsem.at[0])

    def weight_copies(e, sl):
        return (pltpu.make_async_copy(wg_hbm.at[e], wgf.at[sl], wsem.at[sl, 0]),
                pltpu.make_async_copy(wu_hbm.at[e], wuf.at[sl], wsem.at[sl, 1]),
                pltpu.make_async_copy(wd_hbm.at[e], wdf.at[sl], wsem.at[sl, 2]))

    zbuf[...] = jnp.zeros(zbuf.shape, zbuf.dtype)

    def start_zero(t, c):
        zero_copy(t).start()
        return c
    lax.fori_loop(nu, n_tiles, start_zero, 0)

    for cp in weight_copies(te_ref[0], wslot_ref[0]):
        cp.start()
    x_copy(0, 0).start()

    def tile(i, c):
        sl = i % 2
        x_copy(i, sl).wait()

        @pl.when(i + 1 < nu)
        def _():
            x_copy(i + 1, 1 - sl).start()

        @pl.when(i >= 2)
        def _():
            y_copy(i - 2, sl).wait()

        changed = jnp.logical_or(i == 0, te_ref[i] != te_ref[jnp.maximum(i - 1, 0)])
        filled = nsub_ref[i]

        @pl.when(jnp.logical_and(changed, nxt_ref[i] >= 0))
        def _():
            for cp in weight_copies(nxt_ref[i], 1 - wslot_ref[i]):
                cp.start()

        def convert_weights():
            wsl = wslot_ref[i]
            for cp in weight_copies(te_ref[i], wsl):
                cp.wait()
            wgb[...] = wgf[wsl].astype(BF16)
            wub[...] = wuf[wsl].astype(BF16)
            wdb[...] = wdf[wsl].astype(BF16)

        def expert_mlp(k):
            r = k * MOE_SUBTILE
            x = _unpack_bf16_pairs(xbuf[sl, :r, :])
            g = jnp.dot(x, wgb[...], preferred_element_type=F32)
            u = jnp.dot(x, wub[...], preferred_element_type=F32)
            hdn = (g * jax.nn.sigmoid(g) * u).astype(BF16)
            y = jnp.dot(hdn, wdb[...], preferred_element_type=F32)
            ybuf[sl, :r, :] = _pack_bf16_pairs(y.astype(BF16).astype(F32))
            if r < tm:
                ybuf[sl, r:, :] = jnp.zeros((tm - r, ybuf.shape[2]), ybuf.dtype)

        @pl.when(jnp.logical_and(changed, filled < n_sub))
        def _():
            convert_weights()

        @pl.when(jnp.logical_and(changed, filled == n_sub))
        def _():
            convert_weights()
            expert_mlp(n_sub)

        for k in range(1, n_sub):
            @pl.when(filled == k)
            def _(k=k):
                expert_mlp(k)

        @pl.when(jnp.logical_and(jnp.logical_not(changed), filled == n_sub))
        def _():
            expert_mlp(n_sub)

        y_copy(i, sl).start()
        return c

    lax.fori_loop(0, nu, tile, 0)

    @pl.when(nu >= 2)
    def _():
        y_copy(nu - 2, nu % 2).wait()
    y_copy(nu - 1, (nu - 1) % 2).wait()

    def wait_zero(t, c):
        zero_copy(t).wait()
        return c
    lax.fori_loop(nu, n_tiles, wait_zero, 0)


def _moe(xs, tile_e, n_used, next_e, w_slot, tile_sub, w_g, w_u, w_d, tm):
    n_rows, dp = xs.shape
    d = w_g.shape[1]
    ff = w_g.shape[2]
    hbm = pl.BlockSpec(memory_space=pl.ANY)
    grid_spec = pltpu.PrefetchScalarGridSpec(
        num_scalar_prefetch=5,
        grid=(1,),
        in_specs=[hbm, hbm, hbm, hbm],
        out_specs=hbm,
        scratch_shapes=[pltpu.VMEM((2, tm, dp), jnp.uint32), pltpu.VMEM((2, tm, dp), jnp.uint32),
                        pltpu.VMEM((tm, dp), jnp.uint32),
                        pltpu.VMEM((2, d, ff), F32), pltpu.VMEM((2, d, ff), F32), pltpu.VMEM((2, ff, d), F32),
                        pltpu.VMEM((d, ff), BF16), pltpu.VMEM((d, ff), BF16), pltpu.VMEM((ff, d), BF16),
                        pltpu.SemaphoreType.DMA((2,)), pltpu.SemaphoreType.DMA((2,)),
                        pltpu.SemaphoreType.DMA((1,)), pltpu.SemaphoreType.DMA((2, 3))],
    )
    return pl.pallas_call(
        _moe_kernel,
        grid_spec=grid_spec,
        out_shape=jax.ShapeDtypeStruct((n_rows, dp), jnp.uint32),
        compiler_params=_cparams(("arbitrary",), has_side_effects=True),
        name="moe",
    )(tile_e, n_used, next_e, w_slot, tile_sub, xs, w_g, w_u, w_d)


def _combine_kernel(n8_ref, loff_ref, gbase_ref, x1_ref, route_ref, nw_ref, y_hbm, o_ref,
                    ybuf, sem, *, lcap, n_tt):
    i = pl.program_id(0)
    slot = i % 2
    tm = x1_ref.shape[0]

    def fetch(tile, sl, wait):
        _segment_copies(n8_ref, gbase_ref, loff_ref, y_hbm, ybuf.at[sl], sem.at[sl], tile, wait,
                        loff_ref=loff_ref)

    @pl.when(i == 0)
    def _():
        ybuf[...] = jnp.zeros(ybuf.shape, ybuf.dtype)
        fetch(0, 0, False)

    @pl.when(i + 1 < n_tt)
    def _():
        fetch(i + 1, 1 - slot, False)

    fetch(i, slot, True)
    def unsort_rows(n_rows):
        yb = _unpack_bf16_pairs(ybuf[slot, :n_rows, :])
        half = tm // 2
        for r0 in (0, half):
            rows = slice(r0, r0 + half)
            g1 = route_ref[rows, 0:1]
            g2 = route_ref[rows, 1:2]
            lp1 = route_ref[rows, 2:3]
            lp2 = route_ref[rows, 3:4]
            cpos = lax.broadcasted_iota(jnp.int32, (half, n_rows), 1).astype(F32)
            gsel = jnp.where(cpos == lp1, g1, jnp.where(cpos == lp2, g2, 0.0)).astype(BF16)
            x = x1_ref[rows, :] + jnp.dot(gsel, yb, preferred_element_type=F32)
            o_ref[rows, :] = x * lax.rsqrt(jnp.mean(x * x, axis=-1, keepdims=True) + NORM_EPS) * nw_ref[...]

    used = _tile_rows(n8_ref, loff_ref, i)
    short = lcap - LANES
    pl.when(used <= short)(functools.partial(unsort_rows, short))
    pl.when(used > short)(functools.partial(unsort_rows, lcap))


def _combine(x1, y, route, norm_w, tables):
    t, d = x1.shape
    tm = min(ROW_TILE, t)
    n_tt = t // tm
    lcap = _local_rows(tm)
    n8, loff, gbase = tables[:3]
    grid_spec = pltpu.PrefetchScalarGridSpec(
        num_scalar_prefetch=3,
        grid=(n_tt,),
        in_specs=[pl.BlockSpec((tm, d), lambda i, *_: (i, 0)),
                  pl.BlockSpec((tm, LANES), lambda i, *_: (i, 0)),
                  pl.BlockSpec((1, d), lambda i, *_: (0, 0)),
                  pl.BlockSpec(memory_space=pl.ANY)],
        out_specs=pl.BlockSpec((tm, d), lambda i, *_: (i, 0)),
        scratch_shapes=[pltpu.VMEM((2, lcap, d // 2), jnp.uint32), pltpu.SemaphoreType.DMA((2,))],
    )
    return pl.pallas_call(
        functools.partial(_combine_kernel, lcap=lcap, n_tt=n_tt),
        grid_spec=grid_spec,
        out_shape=jax.ShapeDtypeStruct((t, d), F32),
        compiler_params=_cparams(("arbitrary",)),
        name="combine",
    )(n8, loff, gbase, x1, route, norm_w.reshape(1, d), y)


def _block_diag(w):
    n, bi, bj = w.shape
    eye = jnp.eye(n, dtype=w.dtype)
    return jnp.einsum('nij,nm->nimj', w, eye).reshape(n * bi, n * bj)


def kernel(x, mix_norm_w, w_in, lambda_q1, lambda_k1, lambda_q2, lambda_k2, head_norm_w, conv_w, conv_b, w_rgate, b_rgate, w_igate, b_igate, lru_lambda, w_out, ffn_norm_w, w_router_group, b_router_group, w_router_expert, b_router_expert, w_exp_gate, w_exp_up, w_exp_down, final_norm_w):
    b, s, d = x.shape
    t = b * s
    assert w_in.shape[0] == 1, "single-layer stack only"
    att_w = N_ATT_HEADS * HEAD_DIM
    tm_moe = MOE_TILE
    xf = x.reshape(t, d)
    for l in range(1):
        lambda_init = 0.8 - 0.6 * math.exp(-0.3 * l)
        assert s % ATT_TILE == 0, "sequence length must be a multiple of the attention tile"
        w_bd = jnp.concatenate([_block_diag(w_rgate[l]), _block_diag(w_igate[l])], axis=1).astype(BF16)
        b_cat = jnp.concatenate([b_rgate[l], b_igate[l]])
        qt, ka, vt, rnn = _in_proj(xf, mix_norm_w[l], w_in[l], att_w, s,
                                   conv_w[l], conv_b[l], w_bd, b_cat, lru_lambda[l])
        lam_params = jnp.stack([lambda_q1[l], lambda_k1[l], lambda_q2[l], lambda_k2[l]]).astype(F32)
        att = _diff_attention(qt, ka, vt, lam_params, head_norm_w[l], lambda_init, b, s)
        w_route = jnp.concatenate([w_router_group[l], w_router_expert[l]], axis=1).T
        w_route = jnp.pad(w_route, ((0, LANES - w_route.shape[0]), (0, 0))).astype(BF16)
        b_route = jnp.concatenate([b_router_group[l], b_router_expert[l]])
        b_route = jnp.pad(b_route, (0, LANES - b_route.shape[0])).reshape(LANES, 1).astype(F32)
        x1, hn, route, route_t, n8_tiles = _out_proj(att.reshape(t, att_w), rnn.reshape(t, -1), xf,
                                                     w_out[l], ffn_norm_w[l], w_route, b_route)
        n_tt = n8_tiles.shape[0]
        max_rows = TOP_K * t + n_tt * N_EXPERTS * (SEG_ROWS - 1) + N_EXPERTS * (tm_moe - 1)
        n_tiles = -(-max_rows // tm_moe)
        tables = _segment_tables(n8_tiles, tm_moe, n_tiles)
        xs = _dispatch(hn, route_t, tables, n_tiles * tm_moe, tm_moe)
        y = _moe(xs, tables[3], tables[4], tables[7], tables[8], tables[9],
                 w_exp_gate[l], w_exp_up[l], w_exp_down[l], tm_moe)
        out = _combine(x1, y, route, final_norm_w, tables)
    return out.reshape(b, s, d)
```

```python
import functools
import math

import numpy as np
import jax
import jax.numpy as jnp
from jax import lax
from jax.experimental import pallas as pl
from jax.experimental.pallas import tpu as pltpu

F32 = jnp.float32
BF16 = jnp.bfloat16

N_ATT_HEADS = 4
HEAD_DIM = 128
QK_DIM = 64
N_RNN_BLOCKS = 8
CONV_WIDTH = 4
LRU_C = 8.0
N_GROUPS = 4
EXPERTS_PER_GROUP = 8
N_EXPERTS = N_GROUPS * EXPERTS_PER_GROUP
TOP_K = 2
NORM_EPS = 1e-6
HEAD_NORM_EPS = 1e-5
LOG2E = math.log2(math.e)


def _bf16_terms(x, n):
    terms = []
    for _ in range(n):
        t = float(np.float32(x).astype(jnp.bfloat16))
        terms.append(t)
        x -= t
    return tuple(terms)


LOG2E_TERMS = _bf16_terms(LOG2E, 3)
LANES = 128
SUBLANES = 8
SEG_ROWS = SUBLANES
NEG_BIG = -1e30

ROW_TILE = 512
ROUTE_ROWS = 48
ATT_TILE = 512
ATT_HEADS_PER_STEP = 4
V_ROWS = HEAD_DIM + 16
LRU_CHUNK = 128
MOE_TILE = 512
MOE_SUBTILE = 128
VMEM_LIMIT = 48 * 1024 * 1024


def _cparams(sem, vmem=VMEM_LIMIT, **kw):
    return pltpu.CompilerParams(dimension_semantics=sem, vmem_limit_bytes=vmem, **kw)


def _inproj_kernel(slope_ref, x_ref, nw_ref, w_ref, cw_ref, cb_ref, wg_ref, bg_ref, lam_ref,
                   qt_ref, ka_ref, vt_ref, rnn_ref, wb, xs, carry_h, a_s, u_s, *, att_w, s_len, ch):
    i = pl.program_id(0)
    x = x_ref[...]
    tm = x.shape[0]
    c_w = (w_ref.shape[1] - 3 * att_w) // 2

    @pl.when(i == 0)
    def _():
        wb[...] = w_ref[...].astype(BF16)

    @pl.when((i * tm) % s_len == 0)
    def _():
        xs[0:SUBLANES, :] = jnp.zeros((SUBLANES, c_w), F32)
        carry_h[...] = jnp.zeros(carry_h.shape, F32)

    ms = jnp.mean(x * x, axis=-1, keepdims=True)
    hn = (x * lax.rsqrt(ms + NORM_EPS) * nw_ref[...]).astype(BF16)
    p_lru = jnp.dot(hn, wb[:, 3 * att_w:], preferred_element_type=F32)

    xs[SUBLANES:, :] = p_lru[:, :c_w]
    neg_lam = -lam_ref[...]
    sp = jnp.maximum(neg_lam, 0.0) + jnp.log1p(jnp.exp(-jnp.abs(neg_lam)))
    cw = cw_ref[...]
    cb = cb_ref[...]
    bias = bg_ref[...]
    r8 = lax.broadcasted_iota(jnp.int32, (ch // SUBLANES, SUBLANES, c_w), 1)
    xcs, zs = [], []
    for c in range(tm // ch):
        r0 = c * ch
        win = xs[r0:r0 + ch + SUBLANES, :]
        xc = cw[3:4, :] * win[SUBLANES:] + cb
        for k in (1, 2, 3):
            xc = xc + cw[3 - k:4 - k, :] * pltpu.roll(win, k, axis=0)[SUBLANES:]
        xcs.append(xc)
        zs.append(jnp.dot(xc.astype(BF16), wg_ref[...], preferred_element_type=F32) + bias)
    xs[0:SUBLANES, :] = xs[tm:tm + SUBLANES, :]

    p_q = jnp.dot(hn, wb[:, :att_w], preferred_element_type=F32)
    p_v = jnp.dot(hn, wb[:, 2 * att_w:3 * att_w], preferred_element_type=F32)
    p_k = jnp.dot(hn, wb[:, att_w:2 * att_w], preferred_element_type=F32)

    for c in range(tm // ch):
        r0 = c * ch
        xc, z = xcs[c], zs[c]
        r = _sigmoid(z[:, :c_w])
        ig = _sigmoid(z[:, c_w:])
        log_a = (-LRU_C) * r * sp
        a = jnp.exp(log_a)
        w = jnp.tanh(-log_a) * (1.0 + a * a)
        u = jnp.where(w > 0.0, w * lax.rsqrt(w), 0.0) * ig * xc
        a = a.reshape(ch // SUBLANES, SUBLANES, c_w)
        u = u.reshape(ch // SUBLANES, SUBLANES, c_w)
        for k in (1, 2, 4):
            a_sh = pltpu.roll(a, k, axis=1)
            u_sh = pltpu.roll(u, k, axis=1)
            ok = r8 >= k
            u = jnp.where(ok, u + a * u_sh, u)
            a = jnp.where(ok, a * a_sh, a)
        a_s[r0:r0 + ch, :] = a.reshape(ch, c_w)
        u_s[r0:r0 + ch, :] = u.reshape(ch, c_w)

    qt = (p_q * (QK_DIM ** -0.5 * LOG2E)).T
    arow = lax.broadcasted_iota(jnp.int32, (QK_DIM, tm), 0)
    ones2 = jnp.where(arow < 2, LOG2E_TERMS[0], jnp.where(arow < 4, LOG2E_TERMS[1],
                                                          jnp.where(arow < 6, LOG2E_TERMS[2], 0.0)))
    pieces = []
    for g in range(2 * N_ATT_HEADS):
        pieces += [qt[g * QK_DIM:(g + 1) * QK_DIM], ones2]
    qt_ref[0] = jnp.concatenate(pieces, axis=0).astype(BF16)

    lane = lax.broadcasted_iota(jnp.int32, (tm, HEAD_DIM), 1)
    j = (i * tm) % s_len + lax.broadcasted_iota(jnp.int32, (tm, HEAD_DIM), 0)
    j_lo = (j & 255).astype(F32)
    j_hi = (j - (j & 255)).astype(F32)
    vtt = p_v.T
    ones_rows = jnp.where(lax.broadcasted_iota(jnp.int32, (V_ROWS - HEAD_DIM, tm), 0) == 0, 1.0, 0.0)
    for h in range(N_ATT_HEADS):
        slope = slope_ref[h]
        kk = p_k[:, h * HEAD_DIM:(h + 1) * HEAD_DIM]
        in_aug = (lane >= QK_DIM) & (lane < QK_DIM + 2 * len(LOG2E_TERMS))
        aug = jnp.where(in_aug, jnp.where((lane & 1) == 0, slope * j_hi, slope * j_lo), 0.0)
        ka_ref[:, 2 * h * HEAD_DIM:(2 * h + 1) * HEAD_DIM] = jnp.where(lane < QK_DIM, kk, aug).astype(BF16)
        ka_ref[:, (2 * h + 1) * HEAD_DIM:(2 * h + 2) * HEAD_DIM] = jnp.where(
            lane < QK_DIM, pltpu.roll(kk, QK_DIM, axis=1), aug).astype(BF16)
        vt_ref[0, h * V_ROWS:h * V_ROWS + HEAD_DIM, :] = vtt[h * HEAD_DIM:(h + 1) * HEAD_DIM].astype(BF16)
        vt_ref[0, h * V_ROWS + HEAD_DIM:(h + 1) * V_ROWS, :] = ones_rows.astype(BF16)

    hprev = carry_h[0:1, :]
    for r0 in range(0, tm, SUBLANES):
        hg = u_s[r0:r0 + SUBLANES, :] + a_s[r0:r0 + SUBLANES, :] * hprev
        u_s[r0:r0 + SUBLANES, :] = hg
        hprev = hg[SUBLANES - 1:SUBLANES, :]
    carry_h[0:1, :] = hprev
    rnn_ref[...] = (u_s[...] * _gelu_tanh(p_lru[:, c_w:])).astype(rnn_ref.dtype)


def _gelu_tanh(x):
    k1 = -2.0 * math.sqrt(2.0 / math.pi) * LOG2E
    return x / (1.0 + jnp.exp2(x * (k1 + (k1 * 0.044715) * (x * x))))


def _sigmoid(x):
    return 0.5 * jnp.tanh(0.5 * x) + 0.5


def _alibi_slopes():
    nh = N_ATT_HEADS
    return jnp.asarray(np.array([2.0 ** (-8.0 * (i + 1) / nh) for i in range(nh)], dtype=np.float32))


def _in_proj(xf, norm_w, w_in, att_w, s_len, conv_w, conv_b, w_gates_bf, b_gates, lru_lambda):
    t, d = xf.shape
    n = w_in.shape[1]
    tm = min(ATT_TILE, t)
    ch = min(LRU_CHUNK, tm)
    nh = N_ATT_HEADS
    c_w = (n - 3 * att_w) // 2
    fix = lambda i: (0, 0)
    return pl.pallas_call(
        functools.partial(_inproj_kernel, att_w=att_w, s_len=s_len, ch=ch),
        grid=(t // tm,),
        in_specs=[pl.BlockSpec(memory_space=pltpu.SMEM),
                  pl.BlockSpec((tm, d), lambda i: (i, 0)),
                  pl.BlockSpec((1, d), fix),
                  pl.BlockSpec((d, n), fix, pipeline_mode=pl.Buffered(1)),
                  pl.BlockSpec((CONV_WIDTH, c_w), fix),
                  pl.BlockSpec((1, c_w), fix),
                  pl.BlockSpec((c_w, 2 * c_w), fix),
                  pl.BlockSpec((1, 2 * c_w), fix),
                  pl.BlockSpec((1, c_w), fix)],
        out_specs=[pl.BlockSpec((1, 2 * att_w, tm), lambda i: (i, 0, 0)),
                   pl.BlockSpec((tm, 2 * att_w), lambda i: (i, 0)),
                   pl.BlockSpec((1, nh * V_ROWS, tm), lambda i: (i, 0, 0)),
                   pl.BlockSpec((tm, c_w), lambda i: (i, 0))],
        out_shape=[jax.ShapeDtypeStruct((t // tm, 2 * att_w, tm), BF16),
                   jax.ShapeDtypeStruct((t, 2 * att_w), BF16),
                   jax.ShapeDtypeStruct((t // tm, nh * V_ROWS, tm), BF16),
                   jax.ShapeDtypeStruct((t, c_w), BF16)],
        scratch_shapes=[pltpu.VMEM((d, n), BF16),
                        pltpu.VMEM((tm + SUBLANES, c_w), F32), pltpu.VMEM((SUBLANES, c_w), F32),
                        pltpu.VMEM((tm, c_w), F32), pltpu.VMEM((tm, c_w), F32)],
        compiler_params=_cparams(("arbitrary",)),
        name="in_proj",
    )(_alibi_slopes(), xf, norm_w.reshape(1, d), w_in, conv_w, conv_b.reshape(1, c_w), w_gates_bf,
      b_gates.reshape(1, 2 * c_w), lru_lambda.reshape(1, c_w))


def _attn_kernel(lam_ref, hw_ref, q_ref, k_ref, vt, o_ref, sb, mx, acc, *, tq, n_heads, lambda_init):
    qi = pl.program_id(2)
    n_maps = 2 * n_heads
    mx[...] = jnp.full(mx.shape, NEG_BIG, F32)
    acc[...] = jnp.zeros(acc.shape, F32)

    def values(c, n, lanes=slice(None)):
        return vt[c, (n // 2) * V_ROWS:(n // 2 + 1) * V_ROWS, lanes]

    def scores(c, slot):
        rows = pl.ds(pl.multiple_of(c * tq, tq), tq)
        for n in range(n_maps):
            sb[n, slot] = jnp.dot(k_ref[0, rows, n * HEAD_DIM:(n + 1) * HEAD_DIM],
                                  q_ref[0, n * HEAD_DIM:(n + 1) * HEAD_DIM, :],
                                  preferred_element_type=F32)

    def softmax_pv(c, slot):
        for n in range(n_maps):
            s = sb[n, slot]
            m_prev = mx[n]
            m_new = jnp.maximum(m_prev, jnp.max(s, axis=0, keepdims=True))
            p = jnp.exp2(s - m_new).astype(BF16)
            acc[n] = jnp.exp2(m_prev - m_new) * acc[n] + jnp.dot(values(c, n), p, preferred_element_type=F32)
            mx[n] = m_new

    def softmax_pv_diagonal(c, slot):
        hq = tq // 2
        keep_t = (lax.broadcasted_iota(jnp.int32, (hq, tq), 0) <= lax.broadcasted_iota(jnp.int32, (hq, tq), 1))
        keep_b = (lax.broadcasted_iota(jnp.int32, (hq, hq), 0) <= lax.broadcasted_iota(jnp.int32, (hq, hq), 1))
        for n in range(n_maps):
            top = jnp.where(keep_t, sb[n, slot, :hq, :], NEG_BIG)
            bot = jnp.where(keep_b, sb[n, slot, hq:, hq:], NEG_BIG)
            mt = jnp.max(top, axis=0, keepdims=True)
            mb = jnp.max(bot, axis=0, keepdims=True)
            m_prev = mx[n]
            m_new = jnp.maximum(m_prev, jnp.concatenate([mt[:, :hq], jnp.maximum(mt[:, hq:], mb)], axis=1))
            p_top = jnp.exp2(top - m_new).astype(BF16)
            p_bot = jnp.exp2(bot - m_new[:, hq:]).astype(BF16)
            acc[n] = (jnp.exp2(m_prev - m_new) * acc[n]
                      + jnp.dot(values(c, n, slice(0, hq)), p_top, preferred_element_type=F32))
            acc[n, :, hq:] += jnp.dot(values(c, n, slice(hq, tq)), p_bot, preferred_element_type=F32)
            mx[n] = m_new

    scores(0, 0)

    def body(j, c):
        scores(2 * j + 1, 1)
        softmax_pv(2 * j, 0)
        scores(2 * j + 2, 0)
        softmax_pv(2 * j + 1, 1)
        return c

    lax.fori_loop(0, qi // 2, body, 0)

    @pl.when(qi % 2 == 0)
    def _():
        softmax_pv_diagonal(qi, 0)

    @pl.when(qi % 2 == 1)
    def _():
        scores(qi, 1)
        softmax_pv(qi - 1, 0)
        softmax_pv_diagonal(qi, 1)

    lam = (jnp.exp(jnp.sum(lam_ref[0:1, :] * lam_ref[1:2, :], axis=1, keepdims=True))
           - jnp.exp(jnp.sum(lam_ref[2:3, :] * lam_ref[3:4, :], axis=1, keepdims=True))
           + lambda_init)
    for hh in range(n_heads):
        o1 = acc[2 * hh, :HEAD_DIM, :] * (1.0 / acc[2 * hh, HEAD_DIM:HEAD_DIM + 1, :])
        o2 = acc[2 * hh + 1, :HEAD_DIM, :] * (1.0 / acc[2 * hh + 1, HEAD_DIM:HEAD_DIM + 1, :])
        o = o1 - lam * o2
        o = o * lax.rsqrt(jnp.mean(o * o, axis=0, keepdims=True) + HEAD_NORM_EPS)
        o_ref[0, :, hh * HEAD_DIM:(hh + 1) * HEAD_DIM] = (
            o.T * hw_ref[...] * (1.0 - lambda_init)).astype(o_ref.dtype)


def _diff_attention(qt, ka, vt, lam_params, head_norm_w, lambda_init, b, s):
    nh = N_ATT_HEADS
    hp = ATT_HEADS_PER_STEP
    tq = qt.shape[2]
    nq = s // tq
    return pl.pallas_call(
        functools.partial(_attn_kernel, tq=tq, n_heads=hp, lambda_init=lambda_init),
        grid=(b, nh // hp, nq),
        in_specs=[pl.BlockSpec((4, QK_DIM), lambda bi, hi, qi: (0, 0)),
                  pl.BlockSpec((1, HEAD_DIM), lambda bi, hi, qi: (0, 0)),
                  pl.BlockSpec((1, hp * 2 * HEAD_DIM, tq), lambda bi, hi, qi: (bi * nq + qi, hi, 0)),
                  pl.BlockSpec((1, s, hp * 2 * HEAD_DIM), lambda bi, hi, qi: (bi, 0, hi)),
                  pl.BlockSpec((nq, hp * V_ROWS, tq), lambda bi, hi, qi: (bi, hi, 0))],
        out_specs=pl.BlockSpec((1, tq, hp * HEAD_DIM), lambda bi, hi, qi: (bi, qi, hi)),
        out_shape=jax.ShapeDtypeStruct((b, s, nh * HEAD_DIM), BF16),
        scratch_shapes=[pltpu.VMEM((2 * hp, 2, tq, tq), F32), pltpu.VMEM((2 * hp, 1, tq), F32),
                        pltpu.VMEM((2 * hp, V_ROWS, tq), F32)],
        compiler_params=_cparams(("parallel", "parallel", "arbitrary"), vmem=56 * 1024 * 1024),
        name="diff_attn",
    )(lam_params, head_norm_w.reshape(1, HEAD_DIM), qt, ka.reshape(b, s, ka.shape[1]), vt)


def _outproj_kernel(att_ref, rnn_ref, x_ref, wo_ref, nw_ref, wrt_ref, brc_ref,
                    x1_ref, hn_ref, route_ref, route_t_ref, n8_ref, wob, *, att_w):
    @pl.when(pl.program_id(0) == 0)
    def _():
        wob[...] = wo_ref[...].astype(BF16)

    y = jnp.dot(att_ref[...], wob[:att_w, :], preferred_element_type=F32)
    y = y + jnp.dot(rnn_ref[...], wob[att_w:, :], preferred_element_type=F32)
    x1 = x_ref[...] + y
    x1_ref[...] = x1
    hn = (x1 * lax.rsqrt(jnp.mean(x1 * x1, axis=-1, keepdims=True) + NORM_EPS) * nw_ref[...]).astype(BF16)
    hn_ref[...] = hn
    tm = hn.shape[0]

    lg = lax.dot_general(wrt_ref[...], hn, (((1,), (1,)), ((), ())), preferred_element_type=F32)
    lg = lg[:ROUTE_ROWS] + brc_ref[:ROUTE_ROWS, 0:1]
    rowf = lax.broadcasted_iota(jnp.int32, lg.shape, 0).astype(F32)
    big = float(LANES)
    ninf = -jnp.inf
    is_g = rowf < N_GROUPS
    lgm = jnp.where(is_g, lg, ninf)
    mg = jnp.max(lgm, axis=0, keepdims=True)
    g_sel = jnp.min(jnp.where(lgm == mg, rowf, big), axis=0, keepdims=True)
    pg = 1.0 / jnp.sum(jnp.where(is_g, jnp.exp(lgm - mg), 0.0), axis=0, keepdims=True)
    lo = N_GROUPS + EXPERTS_PER_GROUP * g_sel
    in_grp = (rowf >= lo) & (rowf < lo + EXPERTS_PER_GROUP)
    lem = jnp.where(in_grp, lg, ninf)
    v1 = jnp.max(lem, axis=0, keepdims=True)
    i1 = jnp.min(jnp.where(lem == v1, rowf, big), axis=0, keepdims=True)
    lem2 = jnp.where(rowf == i1, ninf, lem)
    v2 = jnp.max(lem2, axis=0, keepdims=True)
    i2 = jnp.min(jnp.where(lem2 == v2, rowf, big), axis=0, keepdims=True)
    e2 = jnp.exp(v2 - v1)
    den = 1.0 + e2
    g1 = pg / den
    g2 = pg * e2 / den

    oh1 = jnp.where(rowf == i1, 1.0, 0.0)
    oh2 = jnp.where(rowf == i2, 1.0, 0.0)
    oh = oh1 + oh2
    earlier = (lax.broadcasted_iota(jnp.int32, (tm, tm), 0)
               < lax.broadcasted_iota(jnp.int32, (tm, tm), 1)).astype(BF16)
    pref = jnp.dot(oh.astype(BF16), earlier, preferred_element_type=F32)
    cnt = jnp.sum(oh, axis=1, keepdims=True)
    n8 = jnp.floor((cnt + (SEG_ROWS - 1)) * (1.0 / SEG_ROWS))
    n8_b = jnp.broadcast_to(n8, (ROUTE_ROWS, LANES))
    before = (lax.broadcasted_iota(jnp.int32, (ROUTE_ROWS, ROUTE_ROWS), 1)
              < lax.broadcasted_iota(jnp.int32, (ROUTE_ROWS, ROUTE_ROWS), 0)).astype(BF16)
    loff8 = jnp.dot(before, n8_b.astype(BF16), preferred_element_type=F32)[:, 0:1]
    pos = SEG_ROWS * loff8 + pref
    lp1 = jnp.sum(oh1 * pos, axis=0, keepdims=True)
    lp2 = jnp.sum(oh2 * pos, axis=0, keepdims=True)
    route_t = jnp.concatenate([g1, g2, lp1, lp2, jnp.zeros((LANES - 4, tm), F32)], axis=0)
    route_t_ref[0] = route_t[:SUBLANES]
    route_ref[...] = route_t.T
    n8_ref[0] = n8_b


def _out_proj(att, rnn, xf, w_out, norm_w, w_route_t_bf, b_route_col):
    t, d = xf.shape
    att_w = att.shape[1]
    tm = min(ROW_TILE, t)
    row = lambda i: (i, 0)
    fix = lambda i: (0, 0)
    return pl.pallas_call(
        functools.partial(_outproj_kernel, att_w=att_w),
        grid=(t // tm,),
        in_specs=[pl.BlockSpec((tm, att_w), row), pl.BlockSpec((tm, rnn.shape[1]), row),
                  pl.BlockSpec((tm, d), row), pl.BlockSpec(w_out.shape, fix, pipeline_mode=pl.Buffered(1)),
                  pl.BlockSpec((1, d), fix), pl.BlockSpec((LANES, d), fix), pl.BlockSpec((LANES, 1), fix)],
        out_specs=[pl.BlockSpec((tm, d), row), pl.BlockSpec((tm, d), row), pl.BlockSpec((tm, LANES), row),
                   pl.BlockSpec((1, SUBLANES, tm), lambda i: (i, 0, 0)),
                   pl.BlockSpec((1, ROUTE_ROWS, LANES), lambda i: (i, 0, 0))],
        out_shape=[jax.ShapeDtypeStruct((t, d), F32), jax.ShapeDtypeStruct((t, d), BF16),
                   jax.ShapeDtypeStruct((t, LANES), F32),
                   jax.ShapeDtypeStruct((t // tm, SUBLANES, tm), F32),
                   jax.ShapeDtypeStruct((t // tm, ROUTE_ROWS, LANES), F32)],
        scratch_shapes=[pltpu.VMEM(w_out.shape, BF16)],
        compiler_params=_cparams(("arbitrary",)),
        name="out_proj",
    )(att, rnn, xf, w_out, norm_w.reshape(1, d), w_route_t_bf, b_route_col)


def _local_rows(tm):
    return -(-(TOP_K * tm + N_EXPERTS * (SEG_ROWS - 1)) // LANES) * LANES


def _segment_tables(n8_tiles, tm_moe, n_tiles):
    n8 = n8_tiles[:, N_GROUPS:N_GROUPS + N_EXPERTS, 0].astype(jnp.int32)
    c8 = n8 * SEG_ROWS
    loff = jnp.cumsum(c8, axis=1) - c8
    gtot = jnp.sum(c8, axis=0)
    gpad = (gtot + tm_moe - 1) // tm_moe * tm_moe
    gend = jnp.cumsum(gpad)
    gstart = gend - gpad
    gbase = gstart[None, :] + jnp.cumsum(c8, axis=0) - c8
    tile_row0 = jnp.arange(n_tiles, dtype=jnp.int32) * tm_moe
    tile_e = jnp.minimum(jnp.sum((gend[None, :] <= tile_row0[:, None]).astype(jnp.int32), axis=1),
                         N_EXPERTS - 1).astype(jnp.int32)
    n_used = (gend[-1] // tm_moe).astype(jnp.int32).reshape(1)
    tail_start = (gstart + gtot).astype(jnp.int32)
    tail_n8 = ((gpad - gtot) // SEG_ROWS).astype(jnp.int32)
    after = gend[tile_e] // tm_moe
    next_e = jnp.where(after < n_used[0], tile_e[jnp.minimum(after, n_tiles - 1)], -1).astype(jnp.int32)
    first = jnp.concatenate([jnp.ones((1,), jnp.int32), (tile_e[1:] != tile_e[:-1]).astype(jnp.int32)])
    w_slot = ((jnp.cumsum(first) - 1) % 2).astype(jnp.int32)
    rows_in_tile = jnp.clip((gstart + gtot)[tile_e] - tile_row0, 1, tm_moe)
    tile_sub = ((rows_in_tile + MOE_SUBTILE - 1) // MOE_SUBTILE).astype(jnp.int32)
    return (n8.reshape(-1), loff.reshape(-1).astype(jnp.int32), gbase.reshape(-1).astype(jnp.int32),
            tile_e, n_used, tail_start, tail_n8, next_e, w_slot, tile_sub)


def _seg_aligned(rows):
    return pl.multiple_of(rows, SEG_ROWS) if SEG_ROWS > 1 else rows


def _tile_rows(n8_ref, loff_ref, tile):
    last = tile * N_EXPERTS + (N_EXPERTS - 1)
    return _seg_aligned(loff_ref[last] + n8_ref[last] * SEG_ROWS)


def _segment_copies(n8_ref, src_off_ref, dst_off_ref, src, dst, sem, tile, wait, loff_ref=None):
    def rows_of(e):
        return _seg_aligned(n8_ref[tile * N_EXPERTS + e] * SEG_ROWS)

    if wait:
        total = _tile_rows(n8_ref, loff_ref, tile)
        pltpu.make_async_copy(src.at[pl.ds(0, total), :], dst.at[pl.ds(0, total), :], sem).wait()
        return

    for e in range(N_EXPERTS):
        k = tile * N_EXPERTS + e
        rows = rows_of(e)

        @pl.when(rows > 0)
        def _(e=e, k=k, rows=rows):
            pltpu.make_async_copy(
                src.at[pl.ds(_seg_aligned(src_off_ref[k]), rows), :],
                dst.at[pl.ds(_seg_aligned(dst_off_ref[k]), rows), :], sem).start(priority=e % 2)


def _pack_bf16_pairs(x):
    n = x.shape[1] // 2
    bits = lax.bitcast_convert_type(x, jnp.uint32)
    return (bits[:, :n] >> 16) | (bits[:, n:] & jnp.uint32(0xFFFF0000))


def _unpack_bf16_pairs(p):
    lo = lax.bitcast_convert_type(p << 16, F32)
    hi = lax.bitcast_convert_type(p & jnp.uint32(0xFFFF0000), F32)
    return jnp.concatenate([lo, hi], axis=1).astype(BF16)


def _dispatch_kernel(n8_ref, loff_ref, gbase_ref, tstart_ref, tn8_ref, nu_ref, hn_ref, route_ref, xs_hbm,
                     stage, zbuf, sem, zsem, *, lcap, n_tt):
    i = pl.program_id(0)
    slot = i % 2
    tm = hn_ref.shape[0]
    tm_moe = zbuf.shape[0]
    n_tiles = xs_hbm.shape[0] // tm_moe

    def tail_copies(wait):
        def go(cp):
            if wait:
                cp.wait()
            else:
                cp.start()

        def per_expert(e, c):
            rows = _seg_aligned(tn8_ref[e] * SEG_ROWS)

            @pl.when(rows > 0)
            def _():
                go(pltpu.make_async_copy(
                    zbuf.at[pl.ds(0, rows), :],
                    xs_hbm.at[pl.ds(_seg_aligned(tstart_ref[e]), rows), :], zsem.at[0]))
            return c
        lax.fori_loop(0, N_EXPERTS, per_expert, 0)

        def per_unused_tile(j, c):
            go(pltpu.make_async_copy(zbuf, xs_hbm.at[pl.ds(pl.multiple_of(j * tm_moe, tm_moe), tm_moe), :],
                                     zsem.at[0]))
            return c
        lax.fori_loop(nu_ref[0], n_tiles, per_unused_tile, 0)

    @pl.when(i == 0)
    def _():
        zbuf[...] = jnp.zeros(zbuf.shape, zbuf.dtype)
        tail_copies(False)

    def segments(tile, sl, wait):
        _segment_copies(n8_ref, loff_ref, gbase_ref, stage.at[sl], xs_hbm, sem.at[sl], tile, wait,
                        loff_ref=loff_ref)

    @pl.when(i >= 2)
    def _():
        segments(i - 2, slot, True)

    def sort_rows(n_rows):
        lp1 = route_ref[0, 2:3, :]
        lp2 = route_ref[0, 3:4, :]
        rpos = lax.broadcasted_iota(jnp.int32, (n_rows, tm), 0).astype(F32)
        sel = jnp.where((rpos == lp1) | (rpos == lp2), 1.0, 0.0).astype(BF16)
        stage[slot, :n_rows, :] = _pack_bf16_pairs(jnp.dot(sel, hn_ref[...], preferred_element_type=F32))

    used = _tile_rows(n8_ref, loff_ref, i)
    short = lcap - LANES
    pl.when(used <= short)(functools.partial(sort_rows, short))
    pl.when(used > short)(functools.partial(sort_rows, lcap))
    segments(i, slot, False)

    @pl.when(i == n_tt - 1)
    def _():
        segments(i, slot, True)
        if n_tt > 1:
            segments(i - 1, 1 - slot, True)
        tail_copies(True)


def _dispatch(hn, route, tables, n_rows, tm_moe):
    t, d = hn.shape
    tm = min(ROW_TILE, t)
    n_tt = t // tm
    lcap = _local_rows(tm)
    n8, loff, gbase, _, n_used, tail_start, tail_n8 = tables[:7]
    grid_spec = pltpu.PrefetchScalarGridSpec(
        num_scalar_prefetch=6,
        grid=(n_tt,),
        in_specs=[pl.BlockSpec((tm, d), lambda i, *_: (i, 0)),
                  pl.BlockSpec((1, SUBLANES, tm), lambda i, *_: (i, 0, 0))],
        out_specs=pl.BlockSpec(memory_space=pl.ANY),
        scratch_shapes=[pltpu.VMEM((2, lcap, d // 2), jnp.uint32), pltpu.VMEM((tm_moe, d // 2), jnp.uint32),
                        pltpu.SemaphoreType.DMA((2,)), pltpu.SemaphoreType.DMA((1,))],
    )
    return pl.pallas_call(
        functools.partial(_dispatch_kernel, lcap=lcap, n_tt=n_tt),
        grid_spec=grid_spec,
        out_shape=jax.ShapeDtypeStruct((n_rows, d // 2), jnp.uint32),
        compiler_params=_cparams(("arbitrary",), has_side_effects=True),
        name="dispatch",
    )(n8, loff, gbase, tail_start, tail_n8, n_used, hn, route)


def _moe_kernel(te_ref, nu_ref, nxt_ref, wslot_ref, nsub_ref, xs_hbm, wg_hbm, wu_hbm, wd_hbm, y_hbm,
                xbuf, ybuf, zbuf, wgf, wuf, wdf, wgb, wub, wdb, xsem, ysem, zsem, wsem):
    tm = xbuf.shape[1]
    n_sub = tm // MOE_SUBTILE
    n_tiles = y_hbm.shape[0] // tm
    nu = nu_ref[0]

    def tile_rows(t):
        return pl.ds(pl.multiple_of(t * tm, tm), tm)

    def x_copy(t, sl):
        return pltpu.make_async_copy(xs_hbm.at[tile_rows(t), :], xbuf.at[sl], xsem.at[sl])

    def y_copy(t, sl):
        return pltpu.make_async_copy(ybuf.at[sl], y_hbm.at[tile_rows(t), :], ysem.at[sl])

    def zero_copy(t):
        return pltpu.make_async_copy(zbuf, y_hbm.at[tile_rows(t), :], zsem.at[0])

    def weight_copies(e, sl):
        return (pltpu.make_async_copy(wg_hbm.at[e], wgf.at[sl], wsem.at[sl, 0]),
                pltpu.make_async_copy(wu_hbm.at[e], wuf.at[sl], wsem.at[sl, 1]),
                pltpu.make_async_copy(wd_hbm.at[e], wdf.at[sl], wsem.at[sl, 2]))

    zbuf[...] = jnp.zeros(zbuf.shape, zbuf.dtype)

    def start_zero(t, c):
        zero_copy(t).start()
        return c
    lax.fori_loop(nu, n_tiles, start_zero, 0)

    for cp in weight_copies(te_ref[0], wslot_ref[0]):
        cp.start()
    x_copy(0, 0).start()

    def tile(i, c):
        sl = i % 2
        x_copy(i, sl).wait()

        @pl.when(i + 1 < nu)
        def _():
            x_copy(i + 1, 1 - sl).start()

        @pl.when(i >= 2)
        def _():
            y_copy(i - 2, sl).wait()

        changed = jnp.logical_or(i == 0, te_ref[i] != te_ref[jnp.maximum(i - 1, 0)])
        filled = nsub_ref[i]

        @pl.when(jnp.logical_and(changed, nxt_ref[i] >= 0))
        def _():
            for cp in weight_copies(nxt_ref[i], 1 - wslot_ref[i]):
                cp.start()

        def convert_weights():
            wsl = wslot_ref[i]
            for cp in weight_copies(te_ref[i], wsl):
                cp.wait()
            wgb[...] = wgf[wsl].astype(BF16)
            wub[...] = wuf[wsl].astype(BF16)
            wdb[...] = wdf[wsl].astype(BF16)

        def expert_mlp(k):
            r = k * MOE_SUBTILE
            x = _unpack_bf16_pairs(xbuf[sl, :r, :])
            g = jnp.dot(x, wgb[...], preferred_element_type=F32)
            u = jnp.dot(x, wub[...], preferred_element_type=F32)
            hdn = (g * jax.nn.sigmoid(g) * u).astype(BF16)
            y = jnp.dot(hdn, wdb[...], preferred_element_type=F32)
            ybuf[sl, :r, :] = _pack_bf16_pairs(y.astype(BF16).astype(F32))
            if r < tm:
                ybuf[sl, r:, :] = jnp.zeros((tm - r, ybuf.shape[2]), ybuf.dtype)

        @pl.when(jnp.logical_and(changed, filled < n_sub))
        def _():
            convert_weights()

        @pl.when(jnp.logical_and(changed, filled == n_sub))
        def _():
            convert_weights()
            expert_mlp(n_sub)

        for k in range(1, n_sub):
            @pl.when(filled == k)
            def _(k=k):
                expert_mlp(k)

        @pl.when(jnp.logical_and(jnp.logical_not(changed), filled == n_sub))
        def _():
            expert_mlp(n_sub)

        y_copy(i, sl).start()
        return c

    lax.fori_loop(0, nu, tile, 0)

    @pl.when(nu >= 2)
    def _():
        y_copy(nu - 2, nu % 2).wait()
    y_copy(nu - 1, (nu - 1) % 2).wait()

    def wait_zero(t, c):
        zero_copy(t).wait()
        return c
    lax.fori_loop(nu, n_tiles, wait_zero, 0)


def _moe(xs, tile_e, n_used, next_e, w_slot, tile_sub, w_g, w_u, w_d, tm):
    n_rows, dp = xs.shape
    d = w_g.shape[1]
    ff = w_g.shape[2]
    hbm = pl.BlockSpec(memory_space=pl.ANY)
    grid_spec = pltpu.PrefetchScalarGridSpec(
        num_scalar_prefetch=5,
        grid=(1,),
        in_specs=[hbm, hbm, hbm, hbm],
        out_specs=hbm,
        scratch_shapes=[pltpu.VMEM((2, tm, dp), jnp.uint32), pltpu.VMEM((2, tm, dp), jnp.uint32),
                        pltpu.VMEM((tm, dp), jnp.uint32),
                        pltpu.VMEM((2, d, ff), F32), pltpu.VMEM((2, d, ff), F32), pltpu.VMEM((2, ff, d), F32),
                        pltpu.VMEM((d, ff), BF16), pltpu.VMEM((d, ff), BF16), pltpu.VMEM((ff, d), BF16),
                        pltpu.SemaphoreType.DMA((2,)), pltpu.SemaphoreType.DMA((2,)),
                        pltpu.SemaphoreType.DMA((1,)), pltpu.SemaphoreType.DMA((2, 3))],
    )
    return pl.pallas_call(
        _moe_kernel,
        grid_spec=grid_spec,
        out_shape=jax.ShapeDtypeStruct((n_rows, dp), jnp.uint32),
        compiler_params=_cparams(("arbitrary",), has_side_effects=True),
        name="moe",
    )(tile_e, n_used, next_e, w_slot, tile_sub, xs, w_g, w_u, w_d)


def _combine_kernel(n8_ref, loff_ref, gbase_ref, x1_ref, route_ref, nw_ref, y_hbm, o_ref,
                    ybuf, sem, *, lcap, n_tt):
    i = pl.program_id(0)
    slot = i % 2
    tm = x1_ref.shape[0]

    def fetch(tile, sl, wait):
        _segment_copies(n8_ref, gbase_ref, loff_ref, y_hbm, ybuf.at[sl], sem.at[sl], tile, wait,
                        loff_ref=loff_ref)

    @pl.when(i == 0)
    def _():
        ybuf[...] = jnp.zeros(ybuf.shape, ybuf.dtype)
        fetch(0, 0, False)

    @pl.when(i + 1 < n_tt)
    def _():
        fetch(i + 1, 1 - slot, False)

    fetch(i, slot, True)
    def unsort_rows(n_rows):
        yb = _unpack_bf16_pairs(ybuf[slot, :n_rows, :])
        half = tm // 2
        for r0 in (0, half):
            rows = slice(r0, r0 + half)
            g1 = route_ref[rows, 0:1]
            g2 = route_ref[rows, 1:2]
            lp1 = route_ref[rows, 2:3]
            lp2 = route_ref[rows, 3:4]
            cpos = lax.broadcasted_iota(jnp.int32, (half, n_rows), 1).astype(F32)
            gsel = jnp.where(cpos == lp1, g1, jnp.where(cpos == lp2, g2, 0.0)).astype(BF16)
            x = x1_ref[rows, :] + jnp.dot(gsel, yb, preferred_element_type=F32)
            o_ref[rows, :] = x * lax.rsqrt(jnp.mean(x * x, axis=-1, keepdims=True) + NORM_EPS) * nw_ref[...]

    used = _tile_rows(n8_ref, loff_ref, i)
    short = lcap - LANES
    pl.when(used <= short)(functools.partial(unsort_rows, short))
    pl.when(used > short)(functools.partial(unsort_rows, lcap))


def _combine(x1, y, route, norm_w, tables):
    t, d = x1.shape
    tm = min(ROW_TILE, t)
    n_tt = t // tm
    lcap = _local_rows(tm)
    n8, loff, gbase = tables[:3]
    grid_spec = pltpu.PrefetchScalarGridSpec(
        num_scalar_prefetch=3,
        grid=(n_tt,),
        in_specs=[pl.BlockSpec((tm, d), lambda i, *_: (i, 0)),
                  pl.BlockSpec((tm, LANES), lambda i, *_: (i, 0)),
                  pl.BlockSpec((1, d), lambda i, *_: (0, 0)),
                  pl.BlockSpec(memory_space=pl.ANY)],
        out_specs=pl.BlockSpec((tm, d), lambda i, *_: (i, 0)),
        scratch_shapes=[pltpu.VMEM((2, lcap, d // 2), jnp.uint32), pltpu.SemaphoreType.DMA((2,))],
    )
    return pl.pallas_call(
        functools.partial(_combine_kernel, lcap=lcap, n_tt=n_tt),
        grid_spec=grid_spec,
        out_shape=jax.ShapeDtypeStruct((t, d), F32),
        compiler_params=_cparams(("arbitrary",)),
        name="combine",
    )(n8, loff, gbase, x1, route, norm_w.reshape(1, d), y)


def _block_diag(w):
    n, bi, bj = w.shape
    eye = jnp.eye(n, dtype=w.dtype)
    return jnp.einsum('nij,nm->nimj', w, eye).reshape(n * bi, n * bj)


def kernel(x, mix_norm_w, w_in, lambda_q1, lambda_k1, lambda_q2, lambda_k2, head_norm_w, conv_w, conv_b, w_rgate, b_rgate, w_igate, b_igate, lru_lambda, w_out, ffn_norm_w, w_router_group, b_router_group, w_router_expert, b_router_expert, w_exp_gate, w_exp_up, w_exp_down, final_norm_w):
    b, s, d = x.shape
    t = b * s
    assert w_in.shape[0] == 1, "single-layer stack only"
    att_w = N_ATT_HEADS * HEAD_DIM
    tm_moe = MOE_TILE
    xf = x.reshape(t, d)
    for l in range(1):
        lambda_init = 0.8 - 0.6 * math.exp(-0.3 * l)
        assert s % ATT_TILE == 0, "sequence length must be a multiple of the attention tile"
        w_bd = jnp.concatenate([_block_diag(w_rgate[l]), _block_diag(w_igate[l])], axis=1).astype(BF16)
        b_cat = jnp.concatenate([b_rgate[l], b_igate[l]])
        qt, ka, vt, rnn = _in_proj(xf, mix_norm_w[l], w_in[l], att_w, s,
                                   conv_w[l], conv_b[l], w_bd, b_cat, lru_lambda[l])
        lam_params = jnp.stack([lambda_q1[l], lambda_k1[l], lambda_q2[l], lambda_k2[l]]).astype(F32)
        att = _diff_attention(qt, ka, vt, lam_params, head_norm_w[l], lambda_init, b, s)
        w_route = jnp.concatenate([w_router_group[l], w_router_expert[l]], axis=1).T
        w_route = jnp.pad(w_route, ((0, LANES - w_route.shape[0]), (0, 0))).astype(BF16)
        b_route = jnp.concatenate([b_router_group[l], b_router_expert[l]])
        b_route = jnp.pad(b_route, (0, LANES - b_route.shape[0])).reshape(LANES, 1).astype(F32)
        x1, hn, route, route_t, n8_tiles = _out_proj(att.reshape(t, att_w), rnn.reshape(t, -1), xf,
                                                     w_out[l], ffn_norm_w[l], w_route, b_route)
        n_tt = n8_tiles.shape[0]
        max_rows = TOP_K * t + n_tt * N_EXPERTS * (SEG_ROWS - 1) + N_EXPERTS * (tm_moe - 1)
        n_tiles = -(-max_rows // tm_moe)
        tables = _segment_tables(n8_tiles, tm_moe, n_tiles)
        xs = _dispatch(hn, route_t, tables, n_tiles * tm_moe, tm_moe)
        y = _moe(xs, tables[3], tables[4], tables[7], tables[8], tables[9],
                 w_exp_gate[l], w_exp_up[l], w_exp_down[l], tm_moe)
        out = _combine(x1, y, route, final_norm_w, tables)
    return out.reshape(b, s, d)
```

```python
import functools
import math

import numpy as np
import jax
import jax.numpy as jnp
from jax import lax
from jax.experimental import pallas as pl
from jax.experimental.pallas import tpu as pltpu

F32 = jnp.float32
BF16 = jnp.bfloat16

N_ATT_HEADS = 4
HEAD_DIM = 128
QK_DIM = 64
N_RNN_BLOCKS = 8
CONV_WIDTH = 4
LRU_C = 8.0
N_GROUPS = 4
EXPERTS_PER_GROUP = 8
N_EXPERTS = N_GROUPS * EXPERTS_PER_GROUP
TOP_K = 2
NORM_EPS = 1e-6
HEAD_NORM_EPS = 1e-5
LOG2E = math.log2(math.e)


def _bf16_terms(x, n):
    terms = []
    for _ in range(n):
        t = float(np.float32(x).astype(jnp.bfloat16))
        terms.append(t)
        x -= t
    return tuple(terms)


LOG2E_TERMS = _bf16_terms(LOG2E, 3)
LANES = 128
SUBLANES = 8
SEG_ROWS = SUBLANES
NEG_BIG = -1e30

ROW_TILE = 512
ROUTE_ROWS = 48
ATT_TILE = 512
ATT_HEADS_PER_STEP = 4
V_ROWS = HEAD_DIM + 16
LRU_CHUNK = 128
MOE_TILE = 512
MOE_SUBTILE = 128
X1_RING = 3
VMEM_LIMIT = 48 * 1024 * 1024


def _cparams(sem, vmem=VMEM_LIMIT, **kw):
    return pltpu.CompilerParams(dimension_semantics=sem, vmem_limit_bytes=vmem, **kw)


def _inproj_kernel(slope_ref, x_ref, nw_ref, w_ref, cw_ref, cb_ref, wg_ref, bg_ref, lam_ref,
                   qt_ref, ka_ref, vt_ref, rnn_ref, wb, xs, carry_h, a_s, u_s, *, att_w, s_len, ch):
    i = pl.program_id(0)
    x = x_ref[...]
    tm = x.shape[0]
    c_w = (w_ref.shape[1] - 3 * att_w) // 2

    @pl.when(i == 0)
    def _():
        wb[...] = w_ref[...].astype(BF16)

    @pl.when((i * tm) % s_len == 0)
    def _():
        xs[0:SUBLANES, :] = jnp.zeros((SUBLANES, c_w), F32)
        carry_h[...] = jnp.zeros(carry_h.shape, F32)

    ms = jnp.mean(x * x, axis=-1, keepdims=True)
    hn = (x * lax.rsqrt(ms + NORM_EPS) * nw_ref[...]).astype(BF16)
    p_lru = jnp.dot(hn, wb[:, 3 * att_w:], preferred_element_type=F32)

    xs[SUBLANES:, :] = p_lru[:, :c_w]
    neg_lam = -lam_ref[...]
    sp = jnp.maximum(neg_lam, 0.0) + jnp.log1p(jnp.exp(-jnp.abs(neg_lam)))
    cw = cw_ref[...]
    cb = cb_ref[...]
    bias = bg_ref[...]
    r8 = lax.broadcasted_iota(jnp.int32, (ch // SUBLANES, SUBLANES, c_w), 1)
    xcs, zs = [], []
    for c in range(tm // ch):
        r0 = c * ch
        win = xs[r0:r0 + ch + SUBLANES, :]
        xc = cw[3:4, :] * win[SUBLANES:] + cb
        for k in (1, 2, 3):
            xc = xc + cw[3 - k:4 - k, :] * pltpu.roll(win, k, axis=0)[SUBLANES:]
        xcs.append(xc)
        zs.append(jnp.dot(xc.astype(BF16), wg_ref[...], preferred_element_type=F32) + bias)
    xs[0:SUBLANES, :] = xs[tm:tm + SUBLANES, :]

    p_q = jnp.dot(hn, wb[:, :att_w], preferred_element_type=F32)
    p_v = jnp.dot(hn, wb[:, 2 * att_w:3 * att_w], preferred_element_type=F32)
    p_k = jnp.dot(hn, wb[:, att_w:2 * att_w], preferred_element_type=F32)

    for c in range(tm // ch):
        r0 = c * ch
        xc, z = xcs[c], zs[c]
        r = _sigmoid(z[:, :c_w])
        ig = _sigmoid(z[:, c_w:])
        log_a = (-LRU_C) * r * sp
        a = jnp.exp(log_a)
        w = jnp.tanh(-log_a) * (1.0 + a * a)
        u = jnp.where(w > 0.0, w * lax.rsqrt(w), 0.0) * ig * xc
        a = a.reshape(ch // SUBLANES, SUBLANES, c_w)
        u = u.reshape(ch // SUBLANES, SUBLANES, c_w)
        for k in (1, 2, 4):
            a_sh = pltpu.roll(a, k, axis=1)
            u_sh = pltpu.roll(u, k, axis=1)
            ok = r8 >= k
            u = jnp.where(ok, u + a * u_sh, u)
            a = jnp.where(ok, a * a_sh, a)
        a_s[r0:r0 + ch, :] = a.reshape(ch, c_w)
        u_s[r0:r0 + ch, :] = u.reshape(ch, c_w)

    qt = (p_q * (QK_DIM ** -0.5 * LOG2E)).T
    arow = lax.broadcasted_iota(jnp.int32, (QK_DIM, tm), 0)
    ones2 = jnp.where(arow < 2, LOG2E_TERMS[0], jnp.where(arow < 4, LOG2E_TERMS[1],
                                                          jnp.where(arow < 6, LOG2E_TERMS[2], 0.0)))
    pieces = []
    for g in range(2 * N_ATT_HEADS):
        pieces += [qt[g * QK_DIM:(g + 1) * QK_DIM], ones2]
    qt_ref[0] = jnp.concatenate(pieces, axis=0).astype(BF16)

    lane = lax.broadcasted_iota(jnp.int32, (tm, HEAD_DIM), 1)
    j = (i * tm) % s_len + lax.broadcasted_iota(jnp.int32, (tm, HEAD_DIM), 0)
    j_lo = (j & 255).astype(F32)
    j_hi = (j - (j & 255)).astype(F32)
    vtt = p_v.T
    ones_rows = jnp.where(lax.broadcasted_iota(jnp.int32, (V_ROWS - HEAD_DIM, tm), 0) == 0, 1.0, 0.0)
    for h in range(N_ATT_HEADS):
        slope = slope_ref[h]
        kk = p_k[:, h * HEAD_DIM:(h + 1) * HEAD_DIM]
        in_aug = (lane >= QK_DIM) & (lane < QK_DIM + 2 * len(LOG2E_TERMS))
        aug = jnp.where(in_aug, jnp.where((lane & 1) == 0, slope * j_hi, slope * j_lo), 0.0)
        ka_ref[:, 2 * h * HEAD_DIM:(2 * h + 1) * HEAD_DIM] = jnp.where(lane < QK_DIM, kk, aug).astype(BF16)
        ka_ref[:, (2 * h + 1) * HEAD_DIM:(2 * h + 2) * HEAD_DIM] = jnp.where(
            lane < QK_DIM, pltpu.roll(kk, QK_DIM, axis=1), aug).astype(BF16)
        vt_ref[0, h * V_ROWS:h * V_ROWS + HEAD_DIM, :] = vtt[h * HEAD_DIM:(h + 1) * HEAD_DIM].astype(BF16)
        vt_ref[0, h * V_ROWS + HEAD_DIM:(h + 1) * V_ROWS, :] = ones_rows.astype(BF16)

    hprev = carry_h[0:1, :]
    for r0 in range(0, tm, SUBLANES):
        hg = u_s[r0:r0 + SUBLANES, :] + a_s[r0:r0 + SUBLANES, :] * hprev
        u_s[r0:r0 + SUBLANES, :] = hg
        hprev = hg[SUBLANES - 1:SUBLANES, :]
    carry_h[0:1, :] = hprev
    rnn_ref[...] = (u_s[...] * _gelu_tanh(p_lru[:, c_w:])).astype(rnn_ref.dtype)


def _gelu_tanh(x):
    k1 = -2.0 * math.sqrt(2.0 / math.pi) * LOG2E
    return x / (1.0 + jnp.exp2(x * (k1 + (k1 * 0.044715) * (x * x))))


def _sigmoid(x):
    return 0.5 * jnp.tanh(0.5 * x) + 0.5


def _alibi_slopes():
    nh = N_ATT_HEADS
    return jnp.asarray(np.array([2.0 ** (-8.0 * (i + 1) / nh) for i in range(nh)], dtype=np.float32))


def _in_proj(xf, norm_w, w_in, att_w, s_len, conv_w, conv_b, w_gates_bf, b_gates, lru_lambda):
    t, d = xf.shape
    n = w_in.shape[1]
    tm = min(ATT_TILE, t)
    ch = min(LRU_CHUNK, tm)
    nh = N_ATT_HEADS
    c_w = (n - 3 * att_w) // 2
    fix = lambda i: (0, 0)
    return pl.pallas_call(
        functools.partial(_inproj_kernel, att_w=att_w, s_len=s_len, ch=ch),
        grid=(t // tm,),
        in_specs=[pl.BlockSpec(memory_space=pltpu.SMEM),
                  pl.BlockSpec((tm, d), lambda i: (i, 0)),
                  pl.BlockSpec((1, d), fix),
                  pl.BlockSpec((d, n), fix, pipeline_mode=pl.Buffered(1)),
                  pl.BlockSpec((CONV_WIDTH, c_w), fix),
                  pl.BlockSpec((1, c_w), fix),
                  pl.BlockSpec((c_w, 2 * c_w), fix),
                  pl.BlockSpec((1, 2 * c_w), fix),
                  pl.BlockSpec((1, c_w), fix)],
        out_specs=[pl.BlockSpec((1, 2 * att_w, tm), lambda i: (i, 0, 0)),
                   pl.BlockSpec((tm, 2 * att_w), lambda i: (i, 0)),
                   pl.BlockSpec((1, nh * V_ROWS, tm), lambda i: (i, 0, 0)),
                   pl.BlockSpec((tm, c_w), lambda i: (i, 0))],
        out_shape=[jax.ShapeDtypeStruct((t // tm, 2 * att_w, tm), BF16),
                   jax.ShapeDtypeStruct((t, 2 * att_w), BF16),
                   jax.ShapeDtypeStruct((t // tm, nh * V_ROWS, tm), BF16),
                   jax.ShapeDtypeStruct((t, c_w), BF16)],
        scratch_shapes=[pltpu.VMEM((d, n), BF16),
                        pltpu.VMEM((tm + SUBLANES, c_w), F32), pltpu.VMEM((SUBLANES, c_w), F32),
                        pltpu.VMEM((tm, c_w), F32), pltpu.VMEM((tm, c_w), F32)],
        compiler_params=_cparams(("arbitrary",)),
        name="in_proj",
    )(_alibi_slopes(), xf, norm_w.reshape(1, d), w_in, conv_w, conv_b.reshape(1, c_w), w_gates_bf,
      b_gates.reshape(1, 2 * c_w), lru_lambda.reshape(1, c_w))


def _attn_kernel(lam_ref, hw_ref, q_ref, k_ref, vt, o_ref, sb, mx, acc, *, tq, n_heads, lambda_init):
    qi = pl.program_id(2)
    n_maps = 2 * n_heads
    mx[...] = jnp.full(mx.shape, NEG_BIG, F32)
    acc[...] = jnp.zeros(acc.shape, F32)

    def values(c, n, lanes=slice(None)):
        return vt[c, (n // 2) * V_ROWS:(n // 2 + 1) * V_ROWS, lanes]

    def scores(c, slot):
        rows = pl.ds(pl.multiple_of(c * tq, tq), tq)
        for n in range(n_maps):
            sb[n, slot] = jnp.dot(k_ref[0, rows, n * HEAD_DIM:(n + 1) * HEAD_DIM],
                                  q_ref[0, n * HEAD_DIM:(n + 1) * HEAD_DIM, :],
                                  preferred_element_type=F32)

    def softmax_pv(c, slot):
        for n in range(n_maps):
            s = sb[n, slot]
            m_prev = mx[n]
            m_new = jnp.maximum(m_prev, jnp.max(s, axis=0, keepdims=True))
            p = jnp.exp2(s - m_new).astype(BF16)
            acc[n] = jnp.exp2(m_prev - m_new) * acc[n] + jnp.dot(values(c, n), p, preferred_element_type=F32)
            mx[n] = m_new

    def softmax_pv_diagonal(c, slot):
        hq = tq // 2
        keep_t = (lax.broadcasted_iota(jnp.int32, (hq, tq), 0) <= lax.broadcasted_iota(jnp.int32, (hq, tq), 1))
        keep_b = (lax.broadcasted_iota(jnp.int32, (hq, hq), 0) <= lax.broadcasted_iota(jnp.int32, (hq, hq), 1))
        for n in range(n_maps):
            top = jnp.where(keep_t, sb[n, slot, :hq, :], NEG_BIG)
            bot = jnp.where(keep_b, sb[n, slot, hq:, hq:], NEG_BIG)
            mt = jnp.max(top, axis=0, keepdims=True)
            mb = jnp.max(bot, axis=0, keepdims=True)
            m_prev = mx[n]
            m_new = jnp.maximum(m_prev, jnp.concatenate([mt[:, :hq], jnp.maximum(mt[:, hq:], mb)], axis=1))
            p_top = jnp.exp2(top - m_new).astype(BF16)
            p_bot = jnp.exp2(bot - m_new[:, hq:]).astype(BF16)
            acc[n] = (jnp.exp2(m_prev - m_new) * acc[n]
                      + jnp.dot(values(c, n, slice(0, hq)), p_top, preferred_element_type=F32))
            acc[n, :, hq:] += jnp.dot(values(c, n, slice(hq, tq)), p_bot, preferred_element_type=F32)
            mx[n] = m_new

    scores(0, 0)

    def body(j, c):
        scores(2 * j + 1, 1)
        softmax_pv(2 * j, 0)
        scores(2 * j + 2, 0)
        softmax_pv(2 * j + 1, 1)
        return c

    lax.fori_loop(0, qi // 2, body, 0)

    @pl.when(qi % 2 == 0)
    def _():
        softmax_pv_diagonal(qi, 0)

    @pl.when(qi % 2 == 1)
    def _():
        scores(qi, 1)
        softmax_pv(qi - 1, 0)
        softmax_pv_diagonal(qi, 1)

    lam = (jnp.exp(jnp.sum(lam_ref[0:1, :] * lam_ref[1:2, :], axis=1, keepdims=True))
           - jnp.exp(jnp.sum(lam_ref[2:3, :] * lam_ref[3:4, :], axis=1, keepdims=True))
           + lambda_init)
    for hh in range(n_heads):
        o1 = acc[2 * hh, :HEAD_DIM, :] * (1.0 / acc[2 * hh, HEAD_DIM:HEAD_DIM + 1, :])
        o2 = acc[2 * hh + 1, :HEAD_DIM, :] * (1.0 / acc[2 * hh + 1, HEAD_DIM:HEAD_DIM + 1, :])
        o = o1 - lam * o2
        o = o * lax.rsqrt(jnp.mean(o * o, axis=0, keepdims=True) + HEAD_NORM_EPS)
        o_ref[0, :, hh * HEAD_DIM:(hh + 1) * HEAD_DIM] = (
            o.T * hw_ref[...] * (1.0 - lambda_init)).astype(o_ref.dtype)


def _diff_attention(qt, ka, vt, lam_params, head_norm_w, lambda_init, b, s):
    nh = N_ATT_HEADS
    hp = ATT_HEADS_PER_STEP
    tq = qt.shape[2]
    nq = s // tq
    return pl.pallas_call(
        functools.partial(_attn_kernel, tq=tq, n_heads=hp, lambda_init=lambda_init),
        grid=(b, nh // hp, nq),
        in_specs=[pl.BlockSpec((4, QK_DIM), lambda bi, hi, qi: (0, 0)),
                  pl.BlockSpec((1, HEAD_DIM), lambda bi, hi, qi: (0, 0)),
                  pl.BlockSpec((1, hp * 2 * HEAD_DIM, tq), lambda bi, hi, qi: (bi * nq + qi, hi, 0)),
                  pl.BlockSpec((1, s, hp * 2 * HEAD_DIM), lambda bi, hi, qi: (bi, 0, hi)),
                  pl.BlockSpec((nq, hp * V_ROWS, tq), lambda bi, hi, qi: (bi, hi, 0))],
        out_specs=pl.BlockSpec((1, tq, hp * HEAD_DIM), lambda bi, hi, qi: (bi, qi, hi)),
        out_shape=jax.ShapeDtypeStruct((b, s, nh * HEAD_DIM), BF16),
        scratch_shapes=[pltpu.VMEM((2 * hp, 2, tq, tq), F32), pltpu.VMEM((2 * hp, 1, tq), F32),
                        pltpu.VMEM((2 * hp, V_ROWS, tq), F32)],
        compiler_params=_cparams(("parallel", "parallel", "arbitrary"), vmem=56 * 1024 * 1024),
        name="diff_attn",
    )(lam_params, head_norm_w.reshape(1, HEAD_DIM), qt, ka.reshape(b, s, ka.shape[1]), vt)


def _outproj_kernel(att_ref, rnn_ref, x_ref, wo_ref, nw_ref, wrt_ref, brc_ref,
                    x1_ref, hn_ref, route_ref, route_t_ref, n8_ref, wob, *, att_w):
    @pl.when(pl.program_id(0) == 0)
    def _():
        wob[...] = wo_ref[...].astype(BF16)

    y = jnp.dot(att_ref[...], wob[:att_w, :], preferred_element_type=F32)
    y = y + jnp.dot(rnn_ref[...], wob[att_w:, :], preferred_element_type=F32)
    x1 = x_ref[...] + y
    x1_ref[...] = x1
    hn = (x1 * lax.rsqrt(jnp.mean(x1 * x1, axis=-1, keepdims=True) + NORM_EPS) * nw_ref[...]).astype(BF16)
    hn_ref[...] = hn
    tm = hn.shape[0]

    lg = lax.dot_general(wrt_ref[...], hn, (((1,), (1,)), ((), ())), preferred_element_type=F32)
    lg = lg[:ROUTE_ROWS] + brc_ref[:ROUTE_ROWS, 0:1]
    rowf = lax.broadcasted_iota(jnp.int32, lg.shape, 0).astype(F32)
    big = float(LANES)
    ninf = -jnp.inf
    is_g = rowf < N_GROUPS
    lgm = jnp.where(is_g, lg, ninf)
    mg = jnp.max(lgm, axis=0, keepdims=True)
    g_sel = jnp.min(jnp.where(lgm == mg, rowf, big), axis=0, keepdims=True)
    pg = 1.0 / jnp.sum(jnp.where(is_g, jnp.exp(lgm - mg), 0.0), axis=0, keepdims=True)
    lo = N_GROUPS + EXPERTS_PER_GROUP * g_sel
    in_grp = (rowf >= lo) & (rowf < lo + EXPERTS_PER_GROUP)
    lem = jnp.where(in_grp, lg, ninf)
    v1 = jnp.max(lem, axis=0, keepdims=True)
    i1 = jnp.min(jnp.where(lem == v1, rowf, big), axis=0, keepdims=True)
    lem2 = jnp.where(rowf == i1, ninf, lem)
    v2 = jnp.max(lem2, axis=0, keepdims=True)
    i2 = jnp.min(jnp.where(lem2 == v2, rowf, big), axis=0, keepdims=True)
    e2 = jnp.exp(v2 - v1)
    den = 1.0 + e2
    g1 = pg / den
    g2 = pg * e2 / den

    oh1 = jnp.where(rowf == i1, 1.0, 0.0)
    oh2 = jnp.where(rowf == i2, 1.0, 0.0)
    oh = oh1 + oh2
    earlier = (lax.broadcasted_iota(jnp.int32, (tm, tm), 0)
               < lax.broadcasted_iota(jnp.int32, (tm, tm), 1)).astype(BF16)
    pref = jnp.dot(oh.astype(BF16), earlier, preferred_element_type=F32)
    cnt = jnp.sum(oh, axis=1, keepdims=True)
    n8 = jnp.floor((cnt + (SEG_ROWS - 1)) * (1.0 / SEG_ROWS))
    n8_b = jnp.broadcast_to(n8, (ROUTE_ROWS, LANES))
    before = (lax.broadcasted_iota(jnp.int32, (ROUTE_ROWS, ROUTE_ROWS), 1)
              < lax.broadcasted_iota(jnp.int32, (ROUTE_ROWS, ROUTE_ROWS), 0)).astype(BF16)
    loff8 = jnp.dot(before, n8_b.astype(BF16), preferred_element_type=F32)[:, 0:1]
    pos = SEG_ROWS * loff8 + pref
    lp1 = jnp.sum(oh1 * pos, axis=0, keepdims=True)
    lp2 = jnp.sum(oh2 * pos, axis=0, keepdims=True)
    route_t = jnp.concatenate([g1, g2, lp1, lp2, jnp.zeros((LANES - 4, tm), F32)], axis=0)
    route_t_ref[0] = route_t[:SUBLANES]
    route_ref[...] = route_t.T
    n8_ref[0] = n8_b


def _out_proj(att, rnn, xf, w_out, norm_w, w_route_t_bf, b_route_col):
    t, d = xf.shape
    att_w = att.shape[1]
    tm = min(ROW_TILE, t)
    row = lambda i: (i, 0)
    fix = lambda i: (0, 0)
    return pl.pallas_call(
        functools.partial(_outproj_kernel, att_w=att_w),
        grid=(t // tm,),
        in_specs=[pl.BlockSpec((tm, att_w), row), pl.BlockSpec((tm, rnn.shape[1]), row),
                  pl.BlockSpec((tm, d), row), pl.BlockSpec(w_out.shape, fix, pipeline_mode=pl.Buffered(1)),
                  pl.BlockSpec((1, d), fix), pl.BlockSpec((LANES, d), fix), pl.BlockSpec((LANES, 1), fix)],
        out_specs=[pl.BlockSpec((tm, d), row), pl.BlockSpec((tm, d), row), pl.BlockSpec((tm, LANES), row),
                   pl.BlockSpec((1, SUBLANES, tm), lambda i: (i, 0, 0)),
                   pl.BlockSpec((1, ROUTE_ROWS, LANES), lambda i: (i, 0, 0))],
        out_shape=[jax.ShapeDtypeStruct((t, d), F32), jax.ShapeDtypeStruct((t, d), BF16),
                   jax.ShapeDtypeStruct((t, LANES), F32),
                   jax.ShapeDtypeStruct((t // tm, SUBLANES, tm), F32),
                   jax.ShapeDtypeStruct((t // tm, ROUTE_ROWS, LANES), F32)],
        scratch_shapes=[pltpu.VMEM(w_out.shape, BF16)],
        compiler_params=_cparams(("arbitrary",)),
        name="out_proj",
    )(att, rnn, xf, w_out, norm_w.reshape(1, d), w_route_t_bf, b_route_col)


def _local_rows(tm):
    return -(-(TOP_K * tm + N_EXPERTS * (SEG_ROWS - 1)) // LANES) * LANES


def _segment_tables(n8_tiles, tm_moe, n_tiles):
    n8 = n8_tiles[:, N_GROUPS:N_GROUPS + N_EXPERTS, 0].astype(jnp.int32)
    c8 = n8 * SEG_ROWS
    loff = jnp.cumsum(c8, axis=1) - c8
    gtot = jnp.sum(c8, axis=0)
    gpad = (gtot + tm_moe - 1) // tm_moe * tm_moe
    gend = jnp.cumsum(gpad)
    gstart = gend - gpad
    gbase = gstart[None, :] + jnp.cumsum(c8, axis=0) - c8
    tile_row0 = jnp.arange(n_tiles, dtype=jnp.int32) * tm_moe
    tile_e = jnp.minimum(jnp.sum((gend[None, :] <= tile_row0[:, None]).astype(jnp.int32), axis=1),
                         N_EXPERTS - 1).astype(jnp.int32)
    n_used = (gend[-1] // tm_moe).astype(jnp.int32).reshape(1)
    tail_start = (gstart + gtot).astype(jnp.int32)
    tail_n8 = ((gpad - gtot) // SEG_ROWS).astype(jnp.int32)
    after = gend[tile_e] // tm_moe
    next_e = jnp.where(after < n_used[0], tile_e[jnp.minimum(after, n_tiles - 1)], -1).astype(jnp.int32)
    first = jnp.concatenate([jnp.ones((1,), jnp.int32), (tile_e[1:] != tile_e[:-1]).astype(jnp.int32)])
    w_slot = ((jnp.cumsum(first) - 1) % 2).astype(jnp.int32)
    rows_in_tile = jnp.clip((gstart + gtot)[tile_e] - tile_row0, 1, tm_moe)
    tile_sub = ((rows_in_tile + MOE_SUBTILE - 1) // MOE_SUBTILE).astype(jnp.int32)
    return (n8.reshape(-1), loff.reshape(-1).astype(jnp.int32), gbase.reshape(-1).astype(jnp.int32),
            tile_e, n_used, tail_start, tail_n8, next_e, w_slot, tile_sub)


def _seg_aligned(rows):
    return pl.multiple_of(rows, SEG_ROWS) if SEG_ROWS > 1 else rows


def _tile_rows(n8_ref, loff_ref, tile):
    last = tile * N_EXPERTS + (N_EXPERTS - 1)
    return _seg_aligned(loff_ref[last] + n8_ref[last] * SEG_ROWS)


def _segment_copies(n8_ref, src_off_ref, dst_off_ref, src, dst, sem, tile, wait, loff_ref=None):
    def rows_of(e):
        return _seg_aligned(n8_ref[tile * N_EXPERTS + e] * SEG_ROWS)

    if wait:
        total = _tile_rows(n8_ref, loff_ref, tile)
        pltpu.make_async_copy(src.at[pl.ds(0, total), :], dst.at[pl.ds(0, total), :], sem).wait()
        return

    def per_expert(e, c):
        k = tile * N_EXPERTS + e
        rows = rows_of(e)

        @pl.when(rows > 0)
        def _():
            pltpu.make_async_copy(
                src.at[pl.ds(_seg_aligned(src_off_ref[k]), rows), :],
                dst.at[pl.ds(_seg_aligned(dst_off_ref[k]), rows), :], sem).start()
        return c
    lax.fori_loop(0, N_EXPERTS, per_expert, 0)


def _pack_bf16_pairs(x):
    n = x.shape[1] // 2
    bits = lax.bitcast_convert_type(x, jnp.uint32)
    return (bits[:, :n] >> 16) | (bits[:, n:] & jnp.uint32(0xFFFF0000))


def _unpack_bf16_pairs(p):
    lo = lax.bitcast_convert_type(p << 16, F32)
    hi = lax.bitcast_convert_type(p & jnp.uint32(0xFFFF0000), F32)
    return jnp.concatenate([lo, hi], axis=1).astype(BF16)


def _dispatch_kernel(n8_ref, loff_ref, gbase_ref, tstart_ref, tn8_ref, nu_ref, hn_ref, route_ref, xs_hbm,
                     stage, zbuf, sem, zsem, *, lcap, n_tt):
    i = pl.program_id(0)
    slot = i % 2
    tm = hn_ref.shape[0]
    tm_moe = zbuf.shape[0]
    n_tiles = xs_hbm.shape[0] // tm_moe

    def tail_copies(wait):
        def go(cp):
            if wait:
                cp.wait()
            else:
                cp.start()

        def per_expert(e, c):
            rows = _seg_aligned(tn8_ref[e] * SEG_ROWS)

            @pl.when(rows > 0)
            def _():
                go(pltpu.make_async_copy(
                    zbuf.at[pl.ds(0, rows), :],
                    xs_hbm.at[pl.ds(_seg_aligned(tstart_ref[e]), rows), :], zsem.at[0]))
            return c
        lax.fori_loop(0, N_EXPERTS, per_expert, 0)

        def per_unused_tile(j, c):
            go(pltpu.make_async_copy(zbuf, xs_hbm.at[pl.ds(pl.multiple_of(j * tm_moe, tm_moe), tm_moe), :],
                                     zsem.at[0]))
            return c
        lax.fori_loop(nu_ref[0], n_tiles, per_unused_tile, 0)

    @pl.when(i == 0)
    def _():
        zbuf[...] = jnp.zeros(zbuf.shape, zbuf.dtype)
        tail_copies(False)

    def segments(tile, sl, wait):
        _segment_copies(n8_ref, loff_ref, gbase_ref, stage.at[sl], xs_hbm, sem.at[sl], tile, wait,
                        loff_ref=loff_ref)

    @pl.when(i >= 2)
    def _():
        segments(i - 2, slot, True)

    def sort_rows(n_rows):
        lp1 = route_ref[0, 2:3, :]
        lp2 = route_ref[0, 3:4, :]
        rpos = lax.broadcasted_iota(jnp.int32, (n_rows, tm), 0).astype(F32)
        sel = jnp.where((rpos == lp1) | (rpos == lp2), 1.0, 0.0).astype(BF16)
        stage[slot, :n_rows, :] = _pack_bf16_pairs(jnp.dot(sel, hn_ref[...], preferred_element_type=F32))

    used = _tile_rows(n8_ref, loff_ref, i)
    short = lcap - LANES
    pl.when(used <= short)(functools.partial(sort_rows, short))
    pl.when(used > short)(functools.partial(sort_rows, lcap))
    segments(i, slot, False)

    @pl.when(i == n_tt - 1)
    def _():
        segments(i, slot, True)
        if n_tt > 1:
            segments(i - 1, 1 - slot, True)
        tail_copies(True)


def _dispatch(hn, route, tables, n_rows, tm_moe):
    t, d = hn.shape
    tm = min(ROW_TILE, t)
    n_tt = t // tm
    lcap = _local_rows(tm)
    n8, loff, gbase, _, n_used, tail_start, tail_n8 = tables[:7]
    grid_spec = pltpu.PrefetchScalarGridSpec(
        num_scalar_prefetch=6,
        grid=(n_tt,),
        in_specs=[pl.BlockSpec((tm, d), lambda i, *_: (i, 0)),
                  pl.BlockSpec((1, SUBLANES, tm), lambda i, *_: (i, 0, 0))],
        out_specs=pl.BlockSpec(memory_space=pl.ANY),
        scratch_shapes=[pltpu.VMEM((2, lcap, d // 2), jnp.uint32), pltpu.VMEM((tm_moe, d // 2), jnp.uint32),
                        pltpu.SemaphoreType.DMA((2,)), pltpu.SemaphoreType.DMA((1,))],
    )
    return pl.pallas_call(
        functools.partial(_dispatch_kernel, lcap=lcap, n_tt=n_tt),
        grid_spec=grid_spec,
        out_shape=jax.ShapeDtypeStruct((n_rows, d // 2), jnp.uint32),
        compiler_params=_cparams(("arbitrary",), has_side_effects=True),
        name="dispatch",
    )(n8, loff, gbase, tail_start, tail_n8, n_used, hn, route)


def _moe_kernel(te_ref, nu_ref, nxt_ref, wslot_ref, nsub_ref, xs_hbm, wg_hbm, wu_hbm, wd_hbm, y_hbm,
                xbuf, ybuf, zbuf, wgf, wuf, wdf, wgb, wub, wdb, xsem, ysem, zsem, wsem):
    tm = xbuf.shape[1]
    n_sub = tm // MOE_SUBTILE
    n_tiles = y_hbm.shape[0] // tm
    nu = nu_ref[0]

    def tile_rows(t):
        return pl.ds(pl.multiple_of(t * tm, tm), tm)

    def x_copy(t, sl):
        return pltpu.make_async_copy(xs_hbm.at[tile_rows(t), :], xbuf.at[sl], xsem.at[sl])

    def y_copy(t, sl):
        return pltpu.make_async_copy(ybuf.at[sl], y_hbm.at[tile_rows(t), :], ysem.at[sl])

    def zero_copy(t):
        return pltpu.make_async_copy(zbuf, y_hbm.at[tile_rows(t), :], zsem.at[0])

    def weight_copies(e, sl):
        return (pltpu.make_async_copy(wg_hbm.at[e], wgf.at[sl], wsem.at[sl, 0]),
                pltpu.make_async_copy(wu_hbm.at[e], wuf.at[sl], wsem.at[sl, 1]),
                pltpu.make_async_copy(wd_hbm.at[e], wdf.at[sl], wsem.at[sl, 2]))

    zbuf[...] = jnp.zeros(zbuf.shape, zbuf.dtype)

    def start_zero(t, c):
        zero_copy(t).start()
        return c
    lax.fori_loop(nu, n_tiles, start_zero, 0)

    for cp in weight_copies(te_ref[0], wslot_ref[0]):
        cp.start()
    x_copy(0, 0).start()

    def tile(i, c):
        sl = i % 2
        x_copy(i, sl).wait()

        @pl.when(i + 1 < nu)
        def _():
            x_copy(i + 1, 1 - sl).start()

        @pl.when(i >= 2)
        def _():
            y_copy(i - 2, sl).wait()

        changed = jnp.logical_or(i == 0, te_ref[i] != te_ref[jnp.maximum(i - 1, 0)])
        filled = nsub_ref[i]

        @pl.when(jnp.logical_and(changed, nxt_ref[i] >= 0))
        def _():
            for cp in weight_copies(nxt_ref[i], 1 - wslot_ref[i]):
                cp.start()

        def convert_weights():
            wsl = wslot_ref[i]
            for cp in weight_copies(te_ref[i], wsl):
                cp.wait()
            wgb[...] = wgf[wsl].astype(BF16)
            wub[...] = wuf[wsl].astype(BF16)
            wdb[...] = wdf[wsl].astype(BF16)

        def expert_mlp(k):
            r = k * MOE_SUBTILE
            x = _unpack_bf16_pairs(xbuf[sl, :r, :])
            g = jnp.dot(x, wgb[...], preferred_element_type=F32)
            u = jnp.dot(x, wub[...], preferred_element_type=F32)
            hdn = (g * jax.nn.sigmoid(g) * u).astype(BF16)
            y = jnp.dot(hdn, wdb[...], preferred_element_type=F32)
            ybuf[sl, :r, :] = _pack_bf16_pairs(y.astype(BF16).astype(F32))
            if r < tm:
                ybuf[sl, r:, :] = jnp.zeros((tm - r, ybuf.shape[2]), ybuf.dtype)

        @pl.when(jnp.logical_and(changed, filled < n_sub))
        def _():
            convert_weights()

        @pl.when(jnp.logical_and(changed, filled == n_sub))
        def _():
            convert_weights()
            expert_mlp(n_sub)

        for k in range(1, n_sub):
            @pl.when(filled == k)
            def _(k=k):
                expert_mlp(k)

        @pl.when(jnp.logical_and(jnp.logical_not(changed), filled == n_sub))
        def _():
            expert_mlp(n_sub)

        y_copy(i, sl).start()
        return c

    lax.fori_loop(0, nu, tile, 0)

    @pl.when(nu >= 2)
    def _():
        y_copy(nu - 2, nu % 2).wait()
    y_copy(nu - 1, (nu - 1) % 2).wait()

    def wait_zero(t, c):
        zero_copy(t).wait()
        return c
    lax.fori_loop(nu, n_tiles, wait_zero, 0)


def _moe(xs, tile_e, n_used, next_e, w_slot, tile_sub, w_g, w_u, w_d, tm):
    n_rows, dp = xs.shape
    d = w_g.shape[1]
    ff = w_g.shape[2]
    hbm = pl.BlockSpec(memory_space=pl.ANY)
    grid_spec = pltpu.PrefetchScalarGridSpec(
        num_scalar_prefetch=5,
        grid=(1,),
        in_specs=[hbm, hbm, hbm, hbm],
        out_specs=hbm,
        scratch_shapes=[pltpu.VMEM((2, tm, dp), jnp.uint32), pltpu.VMEM((2, tm, dp), jnp.uint32),
                        pltpu.VMEM((tm, dp), jnp.uint32),
                        pltpu.VMEM((2, d, ff), F32), pltpu.VMEM((2, d, ff), F32), pltpu.VMEM((2, ff, d), F32),
                        pltpu.VMEM((d, ff), BF16), pltpu.VMEM((d, ff), BF16), pltpu.VMEM((ff, d), BF16),
                        pltpu.SemaphoreType.DMA((2,)), pltpu.SemaphoreType.DMA((2,)),
                        pltpu.SemaphoreType.DMA((1,)), pltpu.SemaphoreType.DMA((2, 3))],
    )
    return pl.pallas_call(
        _moe_kernel,
        grid_spec=grid_spec,
        out_shape=jax.ShapeDtypeStruct((n_rows, dp), jnp.uint32),
        compiler_params=_cparams(("arbitrary",), has_side_effects=True),
        name="moe",
    )(tile_e, n_used, next_e, w_slot, tile_sub, xs, w_g, w_u, w_d)


def _combine_kernel(n8_ref, loff_ref, gbase_ref, x1_hbm, route_ref, nw_ref, y_hbm, o_ref,
                    ybuf, sem, xbuf, xsem, *, lcap, n_tt):
    i = pl.program_id(0)
    slot = i % 2
    tm = o_ref.shape[0]
    n_x = xbuf.shape[0]

    def fetch(tile, sl, wait):
        _segment_copies(n8_ref, gbase_ref, loff_ref, y_hbm, ybuf.at[sl], sem.at[sl], tile, wait,
                        loff_ref=loff_ref)

    def x1_copy(tile):
        sl = tile % n_x
        return pltpu.make_async_copy(x1_hbm.at[pl.ds(pl.multiple_of(tile * tm, tm), tm), :], xbuf.at[sl],
                                     xsem.at[sl])

    @pl.when(i == 0)
    def _():
        ybuf[...] = jnp.zeros(ybuf.shape, ybuf.dtype)
        fetch(0, 0, False)
        for t0 in range(min(n_x - 1, n_tt)):
            x1_copy(t0).start()

    @pl.when(i + n_x - 1 < n_tt)
    def _():
        x1_copy(i + n_x - 1).start()

    @pl.when(i + 1 < n_tt)
    def _():
        fetch(i + 1, 1 - slot, False)

    fetch(i, slot, True)
    x1_copy(i).wait()
    x1_ref = xbuf.at[i % n_x]

    def unsort_rows(n_rows):
        yb = _unpack_bf16_pairs(ybuf[slot, :n_rows, :])
        half = tm // 2
        for r0 in (0, half):
            rows = slice(r0, r0 + half)
            g1 = route_ref[rows, 0:1]
            g2 = route_ref[rows, 1:2]
            lp1 = route_ref[rows, 2:3]
            lp2 = route_ref[rows, 3:4]
            cpos = lax.broadcasted_iota(jnp.int32, (half, n_rows), 1).astype(F32)
            gsel = jnp.where(cpos == lp1, g1, jnp.where(cpos == lp2, g2, 0.0)).astype(BF16)
            x = x1_ref[rows, :] + jnp.dot(gsel, yb, preferred_element_type=F32)
            o_ref[rows, :] = x * lax.rsqrt(jnp.mean(x * x, axis=-1, keepdims=True) + NORM_EPS) * nw_ref[...]

    used = _tile_rows(n8_ref, loff_ref, i)
    short = lcap - LANES
    pl.when(used <= short)(functools.partial(unsort_rows, short))
    pl.when(used > short)(functools.partial(unsort_rows, lcap))


def _combine(x1, y, route, norm_w, tables):
    t, d = x1.shape
    tm = min(ROW_TILE, t)
    n_tt = t // tm
    lcap = _local_rows(tm)
    n8, loff, gbase = tables[:3]
    grid_spec = pltpu.PrefetchScalarGridSpec(
        num_scalar_prefetch=3,
        grid=(n_tt,),
        in_specs=[pl.BlockSpec(memory_space=pl.ANY),
                  pl.BlockSpec((tm, LANES), lambda i, *_: (i, 0)),
                  pl.BlockSpec((1, d), lambda i, *_: (0, 0)),
                  pl.BlockSpec(memory_space=pl.ANY)],
        out_specs=pl.BlockSpec((tm, d), lambda i, *_: (i, 0)),
        scratch_shapes=[pltpu.VMEM((2, lcap, d // 2), jnp.uint32), pltpu.SemaphoreType.DMA((2,)),
                        pltpu.VMEM((X1_RING, tm, d), F32), pltpu.SemaphoreType.DMA((X1_RING,))],
    )
    return pl.pallas_call(
        functools.partial(_combine_kernel, lcap=lcap, n_tt=n_tt),
        grid_spec=grid_spec,
        out_shape=jax.ShapeDtypeStruct((t, d), F32),
        compiler_params=_cparams(("arbitrary",)),
        name="combine",
    )(n8, loff, gbase, x1, route, norm_w.reshape(1, d), y)


def _block_diag(w):
    n, bi, bj = w.shape
    eye = jnp.eye(n, dtype=w.dtype)
    return jnp.einsum('nij,nm->nimj', w, eye).reshape(n * bi, n * bj)


def kernel(x, mix_norm_w, w_in, lambda_q1, lambda_k1, lambda_q2, lambda_k2, head_norm_w, conv_w, conv_b, w_rgate, b_rgate, w_igate, b_igate, lru_lambda, w_out, ffn_norm_w, w_router_group, b_router_group, w_router_expert, b_router_expert, w_exp_gate, w_exp_up, w_exp_down, final_norm_w):
    b, s, d = x.shape
    t = b * s
    assert w_in.shape[0] == 1, "single-layer stack only"
    att_w = N_ATT_HEADS * HEAD_DIM
    tm_moe = MOE_TILE
    xf = x.reshape(t, d)
    for l in range(1):
        lambda_init = 0.8 - 0.6 * math.exp(-0.3 * l)
        assert s % ATT_TILE == 0, "sequence length must be a multiple of the attention tile"
        w_bd = jnp.concatenate([_block_diag(w_rgate[l]), _block_diag(w_igate[l])], axis=1).astype(BF16)
        b_cat = jnp.concatenate([b_rgate[l], b_igate[l]])
        qt, ka, vt, rnn = _in_proj(xf, mix_norm_w[l], w_in[l], att_w, s,
                                   conv_w[l], conv_b[l], w_bd, b_cat, lru_lambda[l])
        lam_params = jnp.stack([lambda_q1[l], lambda_k1[l], lambda_q2[l], lambda_k2[l]]).astype(F32)
        att = _diff_attention(qt, ka, vt, lam_params, head_norm_w[l], lambda_init, b, s)
        w_route = jnp.concatenate([w_router_group[l], w_router_expert[l]], axis=1).T
        w_route = jnp.pad(w_route, ((0, LANES - w_route.shape[0]), (0, 0))).astype(BF16)
        b_route = jnp.concatenate([b_router_group[l], b_router_expert[l]])
        b_route = jnp.pad(b_route, (0, LANES - b_route.shape[0])).reshape(LANES, 1).astype(F32)
        x1, hn, route, route_t, n8_tiles = _out_proj(att.reshape(t, att_w), rnn.reshape(t, -1), xf,
                                                     w_out[l], ffn_norm_w[l], w_route, b_route)
        n_tt = n8_tiles.shape[0]
        max_rows = TOP_K * t + n_tt * N_EXPERTS * (SEG_ROWS - 1) + N_EXPERTS * (tm_moe - 1)
        n_tiles = -(-max_rows // tm_moe)
        tables = _segment_tables(n8_tiles, tm_moe, n_tiles)
        xs = _dispatch(hn, route_t, tables, n_tiles * tm_moe, tm_moe)
        y = _moe(xs, tables[3], tables[4], tables[7], tables[8], tables[9],
                 w_exp_gate[l], w_exp_up[l], w_exp_down[l], tm_moe)
        out = _combine(x1, y, route, final_norm_w, tables)
    return out.reshape(b, s, d)
```
